```python
import jax, jax.numpy as jnp
from jax import lax
import numpy as np

D_MODEL = 4096
BATCH = 4
SEQ = 2048
DEPTH = 1

CHUNK = 64
Q_BLOCK = 128
EPS = 1e-6

A_HEADS = 16
A_NOPE = 128
A_ROPE = 64
A_QK = A_NOPE + A_ROPE
A_V = 128
A_Q_LORA = 1024
A_KV_LORA = 512
ROPE_THETA = 10000.0

B_HEADS = 16
B_HEAD_DIM = 128
B_LEFT_CHUNKS = 8
B_BAND = (B_LEFT_CHUNKS + 1) * CHUNK
B_MAX_REL = 128

N_BRANCH = 2
A_WIDTH = A_HEADS * A_V
B_WIDTH = B_HEADS * B_HEAD_DIM

OFF_QL = 0
OFF_KVL = OFF_QL + A_Q_LORA
OFF_KR = OFF_KVL + A_KV_LORA
OFF_BQ = OFF_KR + A_ROPE
OFF_BK = OFF_BQ + B_WIDTH
OFF_BV = OFF_BK + B_WIDTH
OFF_GATE = OFF_BV + B_WIDTH
IN_COLS = OFF_GATE + N_BRANCH * D_MODEL

N_GROUPS = 8
EXP_PER_GROUP = 8
N_EXPERTS = N_GROUPS * EXP_PER_GROUP
TOP_K = 2
D_EXPERT = 512
MOE_BLOCK = 128

kernel_name = "hybrid_mla_chunkband_hiermoe"


def rmsnorm(x, g):
    xf = x.astype(jnp.float32)
    y = xf * lax.rsqrt(jnp.mean(xf * xf, axis=-1, keepdims=True) + EPS)
    return (y * g.astype(jnp.float32)).astype(x.dtype)


def rope(x, positions):
    half = x.shape[-1] // 2
    inv = ROPE_THETA ** (-jnp.arange(half, dtype=jnp.float32) / half)
    ang = positions.astype(jnp.float32)[..., None] * inv
    cos = jnp.cos(ang)[:, :, None, :]
    sin = jnp.sin(ang)[:, :, None, :]
    x1 = x[..., :half].astype(jnp.float32)
    x2 = x[..., half:].astype(jnp.float32)
    out = jnp.concatenate([x1 * cos - x2 * sin, x2 * cos + x1 * sin], axis=-1)
    return out.astype(x.dtype)


def mla_branch(c_q, c_kv, k_rope, positions, q_norm_g, kv_norm_g, w_uq, w_ukv, a_q_norm_g, a_k_norm_g):
    B, S, _ = c_q.shape
    q = (rmsnorm(c_q, q_norm_g) @ w_uq).reshape(B, S, A_HEADS, A_QK)
    kv = (rmsnorm(c_kv, kv_norm_g) @ w_ukv).reshape(B, S, A_HEADS, A_NOPE + A_V)
    k_nope, v = kv[..., :A_NOPE], kv[..., A_NOPE:]
    k = jnp.concatenate([k_nope, jnp.broadcast_to(k_rope[:, :, None, :], (B, S, A_HEADS, A_ROPE))], axis=-1)
    q = rmsnorm(q, a_q_norm_g)
    k = rmsnorm(k, a_k_norm_g)
    q = jnp.concatenate([q[..., :A_NOPE], rope(q[..., A_NOPE:], positions)], axis=-1)
    k = jnp.concatenate([k[..., :A_NOPE], rope(k[..., A_NOPE:], positions)], axis=-1)
    nb = S // Q_BLOCK
    qb = q.reshape(B, nb, Q_BLOCK, A_HEADS, A_QK).swapaxes(0, 1)
    key_chunk = jnp.arange(S) // CHUNK
    scale = A_QK ** -0.5

    def block(args):
        qi, b_idx = args
        s = jnp.einsum('bqhd,bkhd->bhqk', qi, k, preferred_element_type=jnp.float32) * scale
        q_chunk = (b_idx * Q_BLOCK + jnp.arange(Q_BLOCK)) // CHUNK
        mask = key_chunk[None, :] <= q_chunk[:, None]
        s = jnp.where(mask[None, None], s, -jnp.inf)
        p = jax.nn.softmax(s, axis=-1).astype(v.dtype)
        return jnp.einsum('bhqk,bkhd->bqhd', p, v)

    o = lax.map(block, (qb, jnp.arange(nb)))
    return o.swapaxes(0, 1).reshape(B, S, A_WIDTH)


def chunk_band_branch(q, k, v, b_q_norm_g, b_k_norm_g, rel_bias):
    B, S, H, Dh = q.shape
    q = rmsnorm(q, b_q_norm_g)
    k = rmsnorm(k, b_k_norm_g)
    pad = B_LEFT_CHUNKS * CHUNK
    kp = jnp.pad(k, ((0, 0), (pad, 0), (0, 0), (0, 0)))
    vp = jnp.pad(v, ((0, 0), (pad, 0), (0, 0), (0, 0)))
    n = jnp.arange(CHUNK)
    m = jnp.arange(B_BAND)
    dist = n[:, None] + pad - m[None, :]
    bias = rel_bias[:, jnp.clip(dist, -B_MAX_REL, B_MAX_REL) + B_MAX_REL].astype(jnp.float32)
    scale = Dh ** -0.5
    nc = S // CHUNK

    def chunk(c):
        qc = lax.dynamic_slice_in_dim(q, c * CHUNK, CHUNK, axis=1)
        kc = lax.dynamic_slice_in_dim(kp, c * CHUNK, B_BAND, axis=1)
        vc = lax.dynamic_slice_in_dim(vp, c * CHUNK, B_BAND, axis=1)
        s = jnp.einsum('bqhd,bkhd->bhqk', qc, kc, preferred_element_type=jnp.float32) * scale + bias[None]
        valid = m >= pad - c * CHUNK
        s = jnp.where(valid[None, None, None, :], s, -jnp.inf)
        p = jax.nn.softmax(s, axis=-1).astype(vc.dtype)
        return jnp.einsum('bhqk,bkhd->bqhd', p, vc)

    o = lax.map(chunk, jnp.arange(nc))
    return o.transpose(1, 0, 2, 3, 4).reshape(B, S, H * Dh)


def hier_moe(xn, w_group, w_expert, w_exp_gate, w_exp_up, w_exp_down):
    B, S, D = xn.shape
    T = B * S
    xt = xn.reshape(T, D)
    g_logits = (xt @ w_group).astype(jnp.float32)
    g_prob = jax.nn.softmax(g_logits, axis=-1)
    grp = jnp.argmax(g_logits, axis=-1)
    p_grp = jnp.take_along_axis(g_prob, grp[:, None], axis=-1)
    e_logits = (xt @ w_expert).astype(jnp.float32).reshape(T, N_GROUPS, EXP_PER_GROUP)
    e_logits = jnp.take_along_axis(e_logits, grp[:, None, None], axis=1)[:, 0]
    top_val, top_idx = lax.top_k(e_logits, TOP_K)
    gate = p_grp * jax.nn.softmax(top_val, axis=-1)
    expert_id = grp[:, None] * EXP_PER_GROUP + top_idx

    flat_e = expert_id.reshape(-1)
    flat_tok = jnp.repeat(jnp.arange(T, dtype=jnp.int32), TOP_K)
    flat_w = gate.reshape(-1)
    order = jnp.argsort(flat_e)
    se = flat_e[order]
    counts = jnp.bincount(flat_e, length=N_EXPERTS)
    padded = (counts + MOE_BLOCK - 1) // MOE_BLOCK * MOE_BLOCK
    start = jnp.cumsum(counts) - counts
    pend = jnp.cumsum(padded)
    pstart = pend - padded
    dest = pstart[se] + (jnp.arange(T * TOP_K) - start[se])
    n_blocks = (T * TOP_K + MOE_BLOCK - 1) // MOE_BLOCK + N_EXPERTS
    rows = n_blocks * MOE_BLOCK
    row_tok = jnp.zeros((rows,), jnp.int32).at[dest].set(flat_tok[order])
    row_w = jnp.zeros((rows,), jnp.float32).at[dest].set(flat_w[order])
    block_e = jnp.minimum(jnp.searchsorted(pend, jnp.arange(n_blocks) * MOE_BLOCK, side='right'), N_EXPERTS - 1)

    def run(args):
        toks, e = args
        xb = xt[toks]
        h = jax.nn.silu(xb @ w_exp_gate[e]) * (xb @ w_exp_up[e])
        return h @ w_exp_down[e]

    yb = lax.map(run, (row_tok.reshape(n_blocks, MOE_BLOCK), block_e))
    y = jnp.zeros((T, D), jnp.float32).at[row_tok].add(yb.reshape(rows, D).astype(jnp.float32) * row_w[:, None])
    return y.astype(xn.dtype).reshape(B, S, D)


def setup_inputs(seed: int = 0) -> dict:
    key = jax.random.key(seed)
    ks = jax.random.split(key, 24)
    f32 = jnp.float32

    def nrm(k, shape, scale):
        return jax.random.normal(k, shape, f32) * scale

    def gain(k, n):
        return 1.0 + 0.05 * jax.random.normal(k, (n,), f32)

    x = jax.random.normal(ks[0], (BATCH, SEQ, D_MODEL), f32)
    offsets = jax.random.randint(ks[1], (BATCH, 1), 0, 4096, dtype=jnp.int32)
    positions = (offsets + jnp.arange(SEQ, dtype=jnp.int32)[None, :]).astype(jnp.int32)
    return {
        "x": x,
        "positions": positions,
        "g_mix": gain(ks[2], D_MODEL),
        "w_in": nrm(ks[3], (D_MODEL, IN_COLS), D_MODEL ** -0.5),
        "b_gate": nrm(ks[4], (N_BRANCH * D_MODEL,), 0.1),
        "q_norm_g": gain(ks[5], A_Q_LORA),
        "kv_norm_g": gain(ks[6], A_KV_LORA),
        "w_uq": nrm(ks[7], (A_Q_LORA, A_HEADS * A_QK), A_Q_LORA ** -0.5),
        "w_ukv": nrm(ks[8], (A_KV_LORA, A_HEADS * (A_NOPE + A_V)), A_KV_LORA ** -0.5),
        "a_q_norm_g": gain(ks[9], A_QK),
        "a_k_norm_g": gain(ks[10], A_QK),
        "b_q_norm_g": gain(ks[11], B_HEAD_DIM),
        "b_k_norm_g": gain(ks[12], B_HEAD_DIM),
        "rel_bias": nrm(ks[13], (B_HEADS, 2 * B_MAX_REL + 1), 0.5),
        "w_o_a": nrm(ks[14], (A_WIDTH, D_MODEL), A_WIDTH ** -0.5),
        "w_o_b": nrm(ks[15], (B_WIDTH, D_MODEL), B_WIDTH ** -0.5),
        "w_out": nrm(ks[16], (D_MODEL, D_MODEL), D_MODEL ** -0.5),
        "g_ffn": gain(ks[17], D_MODEL),
        "w_group": nrm(ks[18], (D_MODEL, N_GROUPS), D_MODEL ** -0.5),
        "w_expert": nrm(ks[19], (D_MODEL, N_EXPERTS), D_MODEL ** -0.5),
        "w_exp_gate": nrm(ks[20], (N_EXPERTS, D_MODEL, D_EXPERT), D_MODEL ** -0.5),
        "w_exp_up": nrm(ks[21], (N_EXPERTS, D_MODEL, D_EXPERT), D_MODEL ** -0.5),
        "w_exp_down": nrm(ks[22], (N_EXPERTS, D_EXPERT, D_MODEL), D_EXPERT ** -0.5),
    }


def reference(x, positions, g_mix, w_in, b_gate, q_norm_g, kv_norm_g, w_uq, w_ukv,
              a_q_norm_g, a_k_norm_g, b_q_norm_g, b_k_norm_g, rel_bias,
              w_o_a, w_o_b, w_out, g_ffn, w_group, w_expert,
              w_exp_gate, w_exp_up, w_exp_down):
    B, S, _ = x.shape
    for _layer in range(DEPTH):
        xn = rmsnorm(x, g_mix)
        z = xn @ w_in
        c_q = z[..., OFF_QL:OFF_KVL]
        c_kv = z[..., OFF_KVL:OFF_KR]
        k_rope = z[..., OFF_KR:OFF_BQ]
        bq = z[..., OFF_BQ:OFF_BK].reshape(B, S, B_HEADS, B_HEAD_DIM)
        bk = z[..., OFF_BK:OFF_BV].reshape(B, S, B_HEADS, B_HEAD_DIM)
        bv = z[..., OFF_BV:OFF_GATE].reshape(B, S, B_HEADS, B_HEAD_DIM)
        gates = jax.nn.sigmoid((z[..., OFF_GATE:] + b_gate).astype(jnp.float32)).astype(x.dtype)
        gates = gates.reshape(B, S, N_BRANCH, D_MODEL)

        o_a = mla_branch(c_q, c_kv, k_rope, positions, q_norm_g, kv_norm_g, w_uq, w_ukv, a_q_norm_g, a_k_norm_g)
        o_b = chunk_band_branch(bq, bk, bv, b_q_norm_g, b_k_norm_g, rel_bias)
        merged = gates[:, :, 0, :] * (o_a @ w_o_a) + gates[:, :, 1, :] * (o_b @ w_o_b)
        x = x + merged @ w_out

        x = x + hier_moe(rmsnorm(x, g_ffn), w_group, w_expert, w_exp_gate, w_exp_up, w_exp_down)
    return x
```

```python
import functools

import jax
import jax.numpy as jnp
from jax import lax
from jax.experimental import pallas as pl
from jax.experimental.pallas import tpu as pltpu

F32 = jnp.float32
BF16 = jnp.bfloat16

CHUNK = 64
EPS = 1e-6
A_NOPE = 128
A_ROPE = 64
A_V = 128
A_QK = A_NOPE + A_ROPE
B_HEAD_DIM = 128
B_LEFT_CHUNKS = 8
B_MAX_REL = 128
ROPE_THETA = 10000.0
TOP_K = 2

LANE = 128
A_HEAD_PAD = 2 * LANE
V7X_VMEM_BYTES = 64 * 2**20

ATTN_TILE = 256
MOE_ROWS = 256
NEG_INF = float("-inf")


def _nbytes(shape, dtype):
    n = 1
    for s in shape:
        n *= s
    return n * jnp.dtype(dtype).itemsize


def _params(semantics, pipelined_bytes, resident_bytes=0):
    need = 2 * pipelined_bytes + resident_bytes
    return pltpu.CompilerParams(dimension_semantics=semantics,
                                vmem_limit_bytes=min(int(need), V7X_VMEM_BYTES))


def _tile(n, want):
    t = want
    while t > LANE and n % t:
        t //= 2
    assert n % t == 0, (n, want)
    return t


def _rmsnorm_kernel(x_ref, g_ref, o_ref):
    x = x_ref[...]
    r = lax.rsqrt(jnp.mean(x * x, axis=-1, keepdims=True) + EPS)
    o_ref[...] = (x * r * g_ref[...]).astype(o_ref.dtype)


def _rmsnorm_rows(x, g, out_dtype, tm):
    t, d = x.shape
    blocks = _nbytes((tm, d), F32) + _nbytes((tm, d), out_dtype) + _nbytes((1, d), F32)
    return pl.pallas_call(
        _rmsnorm_kernel,
        grid=(t // tm,),
        in_specs=[pl.BlockSpec((tm, d), lambda i: (i, 0)), pl.BlockSpec((1, d), lambda i: (0, 0))],
        out_specs=pl.BlockSpec((tm, d), lambda i: (i, 0)),
        out_shape=jax.ShapeDtypeStruct((t, d), out_dtype),
        compiler_params=_params(("parallel",), blocks, _nbytes((tm, d), F32)),
        name="rmsnorm",
    )(x, g.reshape(1, d))


def _matmul_kernel(a_ref, w_ref, o_ref):
    o_ref[...] = jnp.dot(a_ref[...], w_ref[...], preferred_element_type=F32).astype(o_ref.dtype)


def _matmul(a, w, out_dtype, tm, tn):
    m, k = a.shape
    n = w.shape[1]
    blocks = _nbytes((tm, k), a.dtype) + _nbytes((k, tn), w.dtype) + _nbytes((tm, tn), out_dtype)
    return pl.pallas_call(
        _matmul_kernel,
        grid=(m // tm, n // tn),
        in_specs=[pl.BlockSpec((tm, k), lambda i, j: (i, 0)), pl.BlockSpec((k, tn), lambda i, j: (0, j))],
        out_specs=pl.BlockSpec((tm, tn), lambda i, j: (i, j)),
        out_shape=jax.ShapeDtypeStruct((m, n), out_dtype),
        compiler_params=_params(("parallel", "arbitrary"), blocks, _nbytes((tm, tn), F32)),
        name="inproj_a",
    )(a, w)


def _inproj_b_kernel(a_ref, w_ref, gb_ref, o_ref, *, n_norm_blocks, n_plain_end):
    j = pl.program_id(1)
    acc = jnp.dot(a_ref[...], w_ref[...], preferred_element_type=F32)

    @pl.when(j < n_norm_blocks)
    def _():
        for h in range(acc.shape[1] // B_HEAD_DIM):
            sl = slice(h * B_HEAD_DIM, (h + 1) * B_HEAD_DIM)
            z = acc[:, sl]
            r = lax.rsqrt(jnp.mean(z * z, axis=-1, keepdims=True) + EPS)
            o_ref[:, sl] = (z * r * gb_ref[:, sl]).astype(o_ref.dtype)

    @pl.when((j >= n_norm_blocks) & (j < n_plain_end))
    def _():
        o_ref[...] = acc.astype(o_ref.dtype)

    @pl.when(j >= n_plain_end)
    def _():
        o_ref[...] = (1.0 / (1.0 + jnp.exp(-(acc + gb_ref[...])))).astype(o_ref.dtype)


def _inproj_b(xn, w, gb, tm, tn, n_norm_blocks, n_plain_end):
    m, k = xn.shape
    n = w.shape[1]
    blocks = (_nbytes((tm, k), xn.dtype) + _nbytes((k, tn), w.dtype) + _nbytes((tm, tn), BF16)
              + _nbytes((1, tn), F32))
    return pl.pallas_call(
        functools.partial(_inproj_b_kernel, n_norm_blocks=n_norm_blocks, n_plain_end=n_plain_end),
        grid=(m // tm, n // tn),
        in_specs=[pl.BlockSpec((tm, k), lambda i, j: (i, 0)),
                  pl.BlockSpec((k, tn), lambda i, j: (0, j)),
                  pl.BlockSpec((1, tn), lambda i, j: (0, j))],
        out_specs=pl.BlockSpec((tm, tn), lambda i, j: (i, j)),
        out_shape=jax.ShapeDtypeStruct((m, n), BF16),
        compiler_params=_params(("parallel", "arbitrary"), blocks, 2 * _nbytes((tm, tn), F32)),
        name="inproj_b",
    )(xn, w, gb)


def _rope_padded(v, cos, sin_lo, sin_hi):
    half = A_ROPE // 2
    return v * cos + pltpu.roll(v, LANE - half, 1) * sin_lo + pltpu.roll(v, half, 1) * sin_hi


def _mla_proj_kernel(za_ref, cos_ref, sl_ref, sh_ref, wuq_ref, wukv_ref, gq_ref, gkv_ref, gaq_ref, gak_ref,
                     q_ref, k_ref, v_ref, *, heads, q_lora, kv_lora):
    cos, sin_lo, sin_hi = cos_ref[...], sl_ref[...], sh_ref[...]

    def norm(z, g):
        r = lax.rsqrt(jnp.mean(z * z, axis=-1, keepdims=True) + EPS)
        return (z * r * g).astype(BF16)

    cq = norm(za_ref[:, :q_lora], gq_ref[...])
    ckv = norm(za_ref[:, q_lora:q_lora + kv_lora], gkv_ref[...])
    k_rope = za_ref[:, q_lora + kv_lora:q_lora + kv_lora + LANE]
    qacc = jnp.dot(cq, wuq_ref[...], preferred_element_type=F32)
    kvacc = jnp.dot(ckv, wukv_ref[...], preferred_element_type=F32)

    gq_lo, gq_hi = gaq_ref[:, :LANE], gaq_ref[:, LANE:]
    gk_lo, gk_hi = gak_ref[:, :LANE], gak_ref[:, LANE:]
    kr_ss = jnp.sum(k_rope * k_rope, axis=-1, keepdims=True)
    kr_rot = _rope_padded(k_rope * gk_hi, cos, sin_lo, sin_hi)
    for h in range(heads):
        base = h * A_HEAD_PAD
        q_lo = qacc[:, base:base + LANE]
        q_hi = qacc[:, base + LANE:base + A_HEAD_PAD]
        ss = jnp.sum(q_lo * q_lo, axis=-1, keepdims=True) + jnp.sum(q_hi * q_hi, axis=-1, keepdims=True)
        r = lax.rsqrt(ss / A_QK + EPS)
        q_ref[:, base:base + LANE] = (q_lo * r * gq_lo).astype(BF16)
        q_ref[:, base + LANE:base + A_HEAD_PAD] = _rope_padded(q_hi * r * gq_hi, cos, sin_lo, sin_hi).astype(BF16)

        k_lo = kvacc[:, base:base + LANE]
        ssk = jnp.sum(k_lo * k_lo, axis=-1, keepdims=True) + kr_ss
        rk = lax.rsqrt(ssk / A_QK + EPS)
        k_ref[:, base:base + LANE] = (k_lo * rk * gk_lo).astype(BF16)
        k_ref[:, base + LANE:base + A_HEAD_PAD] = (kr_rot * rk).astype(BF16)
        v_ref[:, h * A_V:(h + 1) * A_V] = kvacc[:, base + LANE:base + A_HEAD_PAD].astype(BF16)


def _mla_proj(za, cos, sin_lo, sin_hi, wuq, wukv, gq, gkv, gaq, gak, heads, tm):
    t, za_cols = za.shape
    q_lora, kv_lora = wuq.shape[0], wukv.shape[0]
    hp = heads * A_HEAD_PAD
    row = lambda i: (i, 0)
    fix = lambda i: (0, 0)
    blocks = (_nbytes((tm, za_cols), F32) + 3 * _nbytes((tm, LANE), F32) + _nbytes(wuq.shape, BF16)
              + _nbytes(wukv.shape, BF16) + 2 * _nbytes((tm, hp), BF16) + _nbytes((tm, heads * A_V), BF16))
    return pl.pallas_call(
        functools.partial(_mla_proj_kernel, heads=heads, q_lora=q_lora, kv_lora=kv_lora),
        grid=(t // tm,),
        in_specs=[pl.BlockSpec((tm, za_cols), row),
                  pl.BlockSpec((tm, LANE), row), pl.BlockSpec((tm, LANE), row), pl.BlockSpec((tm, LANE), row),
                  pl.BlockSpec(wuq.shape, fix), pl.BlockSpec(wukv.shape, fix),
                  pl.BlockSpec((1, q_lora), fix), pl.BlockSpec((1, kv_lora), fix),
                  pl.BlockSpec((1, A_HEAD_PAD), fix), pl.BlockSpec((1, A_HEAD_PAD), fix)],
        out_specs=[pl.BlockSpec((tm, hp), row), pl.BlockSpec((tm, hp), row), pl.BlockSpec((tm, heads * A_V), row)],
        out_shape=[jax.ShapeDtypeStruct((t, hp), BF16), jax.ShapeDtypeStruct((t, hp), BF16),
                   jax.ShapeDtypeStruct((t, heads * A_V), BF16)],
        compiler_params=_params(("parallel",), blocks, 3 * _nbytes((tm, hp), F32)),
        name="mla_proj",
    )(za, cos, sin_lo, sin_hi, wuq, wukv, gq, gkv, gaq, gak)


def _nt_dot(a, b):
    return lax.dot_general(a, b, (((1,), (1,)), ((), ())), preferred_element_type=F32)


def _mla_attn_kernel(q_ref, k_ref, v_ref, o_ref, *, seq):
    tq = ATTN_TILE
    rc = lax.broadcasted_iota(jnp.int32, (tq, tq), 0) // CHUNK
    cc = lax.broadcasted_iota(jnp.int32, (tq, tq), 1) // CHUNK
    diag_ok = cc <= rc
    for i in range(seq // tq):
        q = q_ref[i * tq:(i + 1) * tq, :]
        sd = jnp.where(diag_ok, _nt_dot(q, k_ref[i * tq:(i + 1) * tq, :]), NEG_INF)
        m = jnp.max(sd, axis=-1, keepdims=True)
        if i > 0:
            s0 = _nt_dot(q, k_ref[:i * tq, :])
            m = jnp.maximum(m, jnp.max(s0, axis=-1, keepdims=True))
        pd = jnp.exp(sd - m)
        l = jnp.sum(pd, axis=-1, keepdims=True)
        o = jnp.dot(pd.astype(BF16), v_ref[i * tq:(i + 1) * tq, :], preferred_element_type=F32)
        if i > 0:
            p0 = jnp.exp(s0 - m)
            l = l + jnp.sum(p0, axis=-1, keepdims=True)
            o = o + jnp.dot(p0.astype(BF16), v_ref[:i * tq, :], preferred_element_type=F32)
        o_ref[i * tq:(i + 1) * tq, :] = (o / l).astype(o_ref.dtype)


def _mla_attn(q, k, v, batch, seq, heads):
    t = batch * seq
    blocks = 2 * _nbytes((seq, A_HEAD_PAD), BF16) + 2 * _nbytes((seq, A_V), BF16)
    temps = 6 * _nbytes((ATTN_TILE, seq), F32)
    return pl.pallas_call(
        functools.partial(_mla_attn_kernel, seq=seq),
        grid=(batch, heads),
        in_specs=[pl.BlockSpec((seq, A_HEAD_PAD), lambda b, h: (b, h)),
                  pl.BlockSpec((seq, A_HEAD_PAD), lambda b, h: (b, h)),
                  pl.BlockSpec((seq, A_V), lambda b, h: (b, h))],
        out_specs=pl.BlockSpec((seq, A_V), lambda b, h: (b, h)),
        out_shape=jax.ShapeDtypeStruct((t, heads * A_V), BF16),
        compiler_params=_params(("parallel", "parallel"), blocks, temps),
        name="mla_attn",
    )(q, k, v)


def _band_attn_kernel(q_ref, k_ref, v_ref, w_ref, o_ref, *, seq):
    tq = ATTN_TILE
    left = B_LEFT_CHUNKS * CHUNK
    for i in range(seq // tq):
        q0 = i * tq
        k0 = max(0, q0 - left)
        kw = q0 + tq - k0
        u0 = left - q0 + k0
        q = q_ref[q0:q0 + tq, :]
        s = _nt_dot(q, k_ref[k0:k0 + kw, :]) + w_ref[:, u0:u0 + kw]
        m = jnp.max(s, axis=-1, keepdims=True)
        p = jnp.exp(s - m)
        l = jnp.sum(p, axis=-1, keepdims=True)
        o = jnp.dot(p.astype(BF16), v_ref[k0:k0 + kw, :], preferred_element_type=F32)
        o_ref[q0:q0 + tq, :] = (o / l).astype(o_ref.dtype)


def _band_bias_table(rel_bias):
    left = B_LEFT_CHUNKS * CHUNK
    n = jnp.arange(ATTN_TILE, dtype=jnp.int32)[:, None]
    u = jnp.arange(left + ATTN_TILE, dtype=jnp.int32)[None, :]
    dist = n - u + left
    q_chunk = n // CHUNK
    k_chunk = jnp.floor_divide(u - left, CHUNK)
    valid = (k_chunk <= q_chunk) & (k_chunk >= q_chunk - B_LEFT_CHUNKS)
    bias = rel_bias[:, jnp.clip(dist, -B_MAX_REL, B_MAX_REL) + B_MAX_REL].astype(F32)
    return jnp.where(valid[None], bias, NEG_INF)


def _band_attn(zb, table, batch, seq, heads):
    t = batch * seq
    d = B_HEAD_DIM
    tw = table.shape[2]
    blocks = 4 * _nbytes((seq, d), BF16) + _nbytes((ATTN_TILE, tw), F32)
    temps = 6 * _nbytes((ATTN_TILE, tw), F32)
    return pl.pallas_call(
        functools.partial(_band_attn_kernel, seq=seq),
        grid=(batch, heads),
        in_specs=[pl.BlockSpec((seq, d), lambda b, h: (b, h)),
                  pl.BlockSpec((seq, d), lambda b, h: (b, heads + h)),
                  pl.BlockSpec((seq, d), lambda b, h: (b, 2 * heads + h)),
                  pl.BlockSpec((None, ATTN_TILE, tw), lambda b, h: (h, 0, 0))],
        out_specs=pl.BlockSpec((seq, d), lambda b, h: (b, h)),
        out_shape=jax.ShapeDtypeStruct((t, heads * d), BF16),
        compiler_params=_params(("parallel", "parallel"), blocks, temps),
        name="band_attn",
    )(zb, zb, zb, table)


def _merge_kernel(oa_ref, ob_ref, woa_ref, wob_ref, g0_ref, g1_ref, o_ref):
    a = jnp.dot(oa_ref[...], woa_ref[...], preferred_element_type=F32)
    b = jnp.dot(ob_ref[...], wob_ref[...], preferred_element_type=F32)
    o_ref[...] = (g0_ref[...].astype(F32) * a + g1_ref[...].astype(F32) * b).astype(o_ref.dtype)


def _merge(oa, ob, woa, wob, zb, gate_col, tm, tn):
    m = oa.shape[0]
    d = woa.shape[1]
    g0, g1 = gate_col // tn, (gate_col + d) // tn
    blocks = (_nbytes((tm, oa.shape[1]), BF16) + _nbytes((tm, ob.shape[1]), BF16) + _nbytes((woa.shape[0], tn), BF16)
              + _nbytes((wob.shape[0], tn), BF16) + 3 * _nbytes((tm, tn), BF16))
    return pl.pallas_call(
        _merge_kernel,
        grid=(m // tm, d // tn),
        in_specs=[pl.BlockSpec((tm, oa.shape[1]), lambda i, j: (i, 0)),
                  pl.BlockSpec((tm, ob.shape[1]), lambda i, j: (i, 0)),
                  pl.BlockSpec((woa.shape[0], tn), lambda i, j: (0, j)),
                  pl.BlockSpec((wob.shape[0], tn), lambda i, j: (0, j)),
                  pl.BlockSpec((tm, tn), lambda i, j: (i, g0 + j)),
                  pl.BlockSpec((tm, tn), lambda i, j: (i, g1 + j))],
        out_specs=pl.BlockSpec((tm, tn), lambda i, j: (i, j)),
        out_shape=jax.ShapeDtypeStruct((m, d), BF16),
        compiler_params=_params(("parallel", "arbitrary"), blocks, 3 * _nbytes((tm, tn), F32)),
        name="merge",
    )(oa, ob, woa, wob, zb, zb)


def _out_proj_kernel(a_ref, w_ref, x_ref, o_ref):
    o_ref[...] = x_ref[...] + jnp.dot(a_ref[...], w_ref[...], preferred_element_type=F32)


def _out_proj(merged, wout, x, tm, tn):
    m, k = merged.shape
    n = wout.shape[1]
    blocks = _nbytes((tm, k), BF16) + _nbytes((k, tn), BF16) + 2 * _nbytes((tm, tn), F32)
    return pl.pallas_call(
        _out_proj_kernel,
        grid=(m // tm, n // tn),
        in_specs=[pl.BlockSpec((tm, k), lambda i, j: (i, 0)),
                  pl.BlockSpec((k, tn), lambda i, j: (0, j)),
                  pl.BlockSpec((tm, tn), lambda i, j: (i, j))],
        out_specs=pl.BlockSpec((tm, tn), lambda i, j: (i, j)),
        out_shape=jax.ShapeDtypeStruct((m, n), F32),
        compiler_params=_params(("parallel", "arbitrary"), blocks, _nbytes((tm, tn), F32)),
        name="out_proj",
    )(merged, wout, x)


def _router_kernel(x_ref, g_ref, wr_ref, xn_ref, route_ref, *, n_groups, per_group):
    x = x_ref[...]
    r = lax.rsqrt(jnp.mean(x * x, axis=-1, keepdims=True) + EPS)
    xn = x * r * g_ref[...]
    xn_ref[...] = xn
    logits = jnp.dot(xn, wr_ref[...], preferred_element_type=F32, precision=lax.Precision.HIGHEST)
    lane = lax.broadcasted_iota(jnp.int32, logits.shape, 1).astype(F32)
    far = float(LANE)

    def top(vals):
        best = jnp.max(vals, axis=-1, keepdims=True)
        return best, jnp.min(jnp.where(vals == best, lane, far), axis=-1, keepdims=True)

    gl = jnp.where(lane < n_groups, logits, NEG_INF)
    gmax, grp = top(gl)
    p_grp = 1.0 / jnp.sum(jnp.exp(gl - gmax), axis=-1, keepdims=True)
    lo = n_groups + grp * per_group
    el = jnp.where((lane >= lo) & (lane < lo + per_group), logits, NEG_INF)
    t1, i1 = top(el)
    t2, i2 = top(jnp.where(lane == i1, NEG_INF, el))
    d = jnp.exp(t2 - t1)
    w1 = p_grp / (1.0 + d)
    w2 = p_grp * d / (1.0 + d)
    route_ref[...] = jnp.where(lane == 0, i1 - n_groups,
                               jnp.where(lane == 1, i2 - n_groups,
                                         jnp.where(lane == 2, w1, jnp.where(lane == 3, w2, 0.0))))


def _router(x1, g, wr, n_groups, per_group, tm):
    t, d = x1.shape
    blocks = 2 * _nbytes((tm, d), F32) + _nbytes((1, d), F32) + _nbytes((d, LANE), F32) + _nbytes((tm, LANE), F32)
    return pl.pallas_call(
        functools.partial(_router_kernel, n_groups=n_groups, per_group=per_group),
        grid=(t // tm,),
        in_specs=[pl.BlockSpec((tm, d), lambda i: (i, 0)), pl.BlockSpec((1, d), lambda i: (0, 0)),
                  pl.BlockSpec((d, LANE), lambda i: (0, 0))],
        out_specs=[pl.BlockSpec((tm, d), lambda i: (i, 0)), pl.BlockSpec((tm, LANE), lambda i: (i, 0))],
        out_shape=[jax.ShapeDtypeStruct((t, d), F32), jax.ShapeDtypeStruct((t, LANE), F32)],
        compiler_params=_params(("parallel",), blocks, 2 * _nbytes((tm, d), F32)),
        name="router",
    )(x1, g.reshape(1, d), wr)


def _one_hots(route):
    lane = lax.broadcasted_iota(jnp.int32, route.shape, 1).astype(F32)
    return (lane == route[:, 0:1]).astype(F32), (lane == route[:, 1:2]).astype(F32)


def _rank_kernel(route_ref, rank_ref, starts_ref, count_acc, start_acc):
    i = pl.program_id(0)

    @pl.when(i == 0)
    def _():
        count_acc[...] = jnp.zeros_like(count_acc)
        start_acc[...] = jnp.zeros_like(start_acc)

    oh1, oh2 = _one_hots(route_ref[...])
    oh = (oh1 + oh2).astype(BF16)
    tm = oh.shape[0]
    earlier = (lax.broadcasted_iota(jnp.int32, (tm, tm), 0) > lax.broadcasted_iota(jnp.int32, (tm, tm), 1))
    before = jnp.dot(earlier.astype(BF16), oh, preferred_element_type=F32) + count_acc[...]
    lane = lax.broadcasted_iota(jnp.int32, (tm, LANE), 1)
    rank_ref[...] = jnp.where(lane == 0, jnp.sum(oh1 * before, axis=-1, keepdims=True),
                              jnp.where(lane == 1, jnp.sum(oh2 * before, axis=-1, keepdims=True), 0.0))
    lower = (lax.broadcasted_iota(jnp.int32, (LANE, LANE), 0) < lax.broadcasted_iota(jnp.int32, (LANE, LANE), 1))
    below = jnp.dot(oh, lower.astype(BF16), preferred_element_type=F32)
    count_acc[...] += jnp.sum(oh.astype(F32), axis=0, keepdims=True)
    start_acc[...] += jnp.sum(below, axis=0, keepdims=True)
    starts_ref[...] = start_acc[...]


def _rank(route, tm):
    t = route.shape[0]
    blocks = 2 * _nbytes((tm, LANE), F32) + _nbytes((1, LANE), F32)
    return pl.pallas_call(
        _rank_kernel,
        grid=(t // tm,),
        in_specs=[pl.BlockSpec((tm, LANE), lambda i: (i, 0))],
        out_specs=[pl.BlockSpec((tm, LANE), lambda i: (i, 0)), pl.BlockSpec((1, LANE), lambda i: (0, 0))],
        out_shape=[jax.ShapeDtypeStruct((t, LANE), F32), jax.ShapeDtypeStruct((1, LANE), F32)],
        scratch_shapes=[pltpu.VMEM((1, LANE), F32), pltpu.VMEM((1, LANE), F32)],
        compiler_params=_params(("arbitrary",), blocks, 2 * _nbytes((tm, tm), F32)),
        name="moe_rank",
    )(route)


def _dest_kernel(route_ref, rank_ref, starts_ref, dest_ref):
    oh1, oh2 = _one_hots(route_ref[...])
    rank = rank_ref[...]
    starts = starts_ref[...]
    d1 = jnp.sum(oh1 * starts, axis=-1, keepdims=True) + rank[:, 0:1]
    d2 = jnp.sum(oh2 * starts, axis=-1, keepdims=True) + rank[:, 1:2]
    lane = lax.broadcasted_iota(jnp.int32, rank.shape, 1)
    dest_ref[...] = jnp.where(lane == 0, d1, jnp.where(lane == 1, d2, 0.0)).astype(jnp.int32)


def _dest(route, rank, starts, tm):
    t = route.shape[0]
    blocks = 3 * _nbytes((tm, LANE), F32) + _nbytes((1, LANE), F32)
    return pl.pallas_call(
        _dest_kernel,
        grid=(t // tm,),
        in_specs=[pl.BlockSpec((tm, LANE), lambda i: (i, 0)), pl.BlockSpec((tm, LANE), lambda i: (i, 0)),
                  pl.BlockSpec((1, LANE), lambda i: (0, 0))],
        out_specs=pl.BlockSpec((tm, LANE), lambda i: (i, 0)),
        out_shape=jax.ShapeDtypeStruct((t, LANE), jnp.int32),
        compiler_params=_params(("parallel",), blocks, 4 * _nbytes((tm, LANE), F32)),
        name="moe_dest",
    )(route, rank, starts)


def _row_copy(src_hbm, dst_hbm, src_row, dst_row, sem):
    return pltpu.make_async_copy(src_hbm.at[pl.ds(src_row, 1)], dst_hbm.at[pl.ds(dst_row, 1)], sem)


def _scatter_rows_kernel(dest_ref, x_hbm, xs_hbm, sem, *, tb):
    base = pl.program_id(0) * tb

    def copies(t):
        tok = base + t
        return [_row_copy(x_hbm, xs_hbm, tok, dest_ref[TOP_K * tok + k], sem) for k in range(TOP_K)]

    def start(t, carry):
        for c in copies(t):
            c.start()
        return carry

    def wait(t, carry):
        for c in copies(t):
            c.wait()
        return carry

    lax.fori_loop(0, tb, start, 0)
    lax.fori_loop(0, tb, wait, 0)


def _scatter_rows(dest_flat, xn, tb):
    t, d = xn.shape
    return pl.pallas_call(
        functools.partial(_scatter_rows_kernel, tb=tb),
        grid_spec=pltpu.PrefetchScalarGridSpec(
            num_scalar_prefetch=1,
            grid=(t // tb,),
            in_specs=[pl.BlockSpec(memory_space=pl.ANY)],
            out_specs=pl.BlockSpec(memory_space=pl.ANY),
            scratch_shapes=[pltpu.SemaphoreType.DMA(())]),
        out_shape=jax.ShapeDtypeStruct((TOP_K * t, d), xn.dtype),
        compiler_params=pltpu.CompilerParams(dimension_semantics=("arbitrary",)),
        name="moe_scatter_rows",
    )(dest_flat, xn)


def _work_items(starts, counts, n_rows):
    n_blk = n_rows // MOE_ROWS
    ends = starts + counts
    pts = jnp.sort(jnp.concatenate([jnp.arange(n_blk, dtype=jnp.int32) * MOE_ROWS, starts[1:]]))
    lo = pts
    hi = jnp.concatenate([pts[1:], jnp.full((1,), n_rows, jnp.int32)])
    r = jnp.minimum(lo // MOE_ROWS, n_blk - 1)
    e = jnp.minimum(jnp.searchsorted(ends, lo, side="right").astype(jnp.int32), starts.shape[0] - 1)
    return r.astype(jnp.int32), e, lo.astype(jnp.int32), hi.astype(jnp.int32)


def _item(r_ref, e_ref, lo_ref, hi_ref, w):
    e = e_ref[w]
    new_expert = (w == 0) | (e != e_ref[jnp.maximum(w - 1, 0)])
    return r_ref[w], lo_ref[w], hi_ref[w], new_expert


def _store_item_rows(o_ref, val, r, lo, hi):
    rows = r * MOE_ROWS + lax.broadcasted_iota(jnp.int32, (MOE_ROWS, 1), 0)
    mine = (rows >= lo) & (rows < hi)

    @pl.when(lo == r * MOE_ROWS)
    def _():
        o_ref[...] = val

    @pl.when(lo != r * MOE_ROWS)
    def _():
        o_ref[...] = jnp.where(mine, val, o_ref[...])


def _moe_up_kernel(r_ref, e_ref, lo_ref, hi_ref, xs_ref, wg_ref, wu_ref, h_ref, wg_bf, wu_bf):
    r, lo, hi, new_expert = _item(r_ref, e_ref, lo_ref, hi_ref, pl.program_id(1))

    @pl.when(new_expert)
    def _():
        wg_bf[...] = wg_ref[...].astype(BF16)
        wu_bf[...] = wu_ref[...].astype(BF16)

    @pl.when(hi > lo)
    def _():
        x = xs_ref[...].astype(BF16)
        g = jnp.dot(x, wg_bf[...], preferred_element_type=F32)
        u = jnp.dot(x, wu_bf[...], preferred_element_type=F32)
        h = (g * (1.0 / (1.0 + jnp.exp(-g)))) * u
        _store_item_rows(h_ref, h.astype(h_ref.dtype), r, lo, hi)


def _moe_up(items, xs, wg, wu, tf):
    n_rows, d = xs.shape
    n_exp, _, f = wg.shape
    n_items = items[0].shape[0]
    blocks = _nbytes((MOE_ROWS, d), xs.dtype) + 2 * _nbytes((d, tf), F32) + _nbytes((MOE_ROWS, tf), BF16)
    resident = 2 * _nbytes((d, tf), BF16) + _nbytes((MOE_ROWS, d), BF16) + 4 * _nbytes((MOE_ROWS, tf), F32)
    wspec = pl.BlockSpec((None, d, tf), lambda c, w, r, e, lo, hi: (e[w], 0, c))
    return pl.pallas_call(
        _moe_up_kernel,
        grid_spec=pltpu.PrefetchScalarGridSpec(
            num_scalar_prefetch=4,
            grid=(f // tf, n_items),
            in_specs=[pl.BlockSpec((MOE_ROWS, d), lambda c, w, r, e, lo, hi: (r[w], 0)), wspec, wspec],
            out_specs=pl.BlockSpec((MOE_ROWS, tf), lambda c, w, r, e, lo, hi: (r[w], c)),
            scratch_shapes=[pltpu.VMEM((d, tf), BF16), pltpu.VMEM((d, tf), BF16)]),
        out_shape=jax.ShapeDtypeStruct((n_rows, f), BF16),
        compiler_params=_params(("arbitrary", "arbitrary"), blocks, resident),
        name="moe_up",
    )(*items, xs, wg, wu)


def _moe_down_kernel(r_ref, e_ref, lo_ref, hi_ref, h_ref, wd_ref, y_ref, wd_bf):
    r, lo, hi, new_expert = _item(r_ref, e_ref, lo_ref, hi_ref, pl.program_id(0))

    @pl.when(new_expert)
    def _():
        wd_bf[...] = wd_ref[...].astype(BF16)

    @pl.when(hi > lo)
    def _():
        y = jnp.dot(h_ref[...], wd_bf[...], preferred_element_type=F32)
        _store_item_rows(y_ref, y, r, lo, hi)


def _moe_down(items, h, wd):
    n_rows, f = h.shape
    d = wd.shape[2]
    n_items = items[0].shape[0]
    blocks = _nbytes((MOE_ROWS, f), BF16) + _nbytes((f, d), F32) + _nbytes((MOE_ROWS, d), F32)
    resident = _nbytes((f, d), BF16) + 2 * _nbytes((MOE_ROWS, d), F32)
    return pl.pallas_call(
        _moe_down_kernel,
        grid_spec=pltpu.PrefetchScalarGridSpec(
            num_scalar_prefetch=4,
            grid=(n_items,),
            in_specs=[pl.BlockSpec((MOE_ROWS, f), lambda w, r, e, lo, hi: (r[w], 0)),
                      pl.BlockSpec((None, f, d), lambda w, r, e, lo, hi: (e[w], 0, 0))],
            out_specs=pl.BlockSpec((MOE_ROWS, d), lambda w, r, e, lo, hi: (r[w], 0)),
            scratch_shapes=[pltpu.VMEM((f, d), BF16)]),
        out_shape=jax.ShapeDtypeStruct((n_rows, d), F32),
        compiler_params=_params(("arbitrary",), blocks, resident),
        name="moe_down",
    )(*items, h, wd)


def _combine_kernel(dest_ref, x_ref, route_ref, y_hbm, o_ref, ybuf, sem, *, tm):
    base = pl.program_id(0) * tm

    def copies(t):
        return [pltpu.make_async_copy(y_hbm.at[pl.ds(dest_ref[TOP_K * (base + t) + k], 1)],
                                      ybuf.at[k, pl.ds(t, 1)], sem) for k in range(TOP_K)]

    def start(t, carry):
        for c in copies(t):
            c.start()
        return carry

    def wait(t, carry):
        for c in copies(t):
            c.wait()
        return carry

    lax.fori_loop(0, tm, start, 0)
    lax.fori_loop(0, tm, wait, 0)
    route = route_ref[...]
    o_ref[...] = x_ref[...] + (route[:, 2:3] * ybuf[0] + route[:, 3:4] * ybuf[1])


def _combine(dest_flat, x1, route, y, tm):
    t, d = x1.shape
    blocks = 2 * _nbytes((tm, d), F32) + _nbytes((tm, LANE), F32)
    resident = TOP_K * _nbytes((tm, d), F32) + _nbytes((tm, d), F32)
    return pl.pallas_call(
        functools.partial(_combine_kernel, tm=tm),
        grid_spec=pltpu.PrefetchScalarGridSpec(
            num_scalar_prefetch=1,
            grid=(t // tm,),
            in_specs=[pl.BlockSpec((tm, d), lambda i, dest: (i, 0)),
                      pl.BlockSpec((tm, LANE), lambda i, dest: (i, 0)),
                      pl.BlockSpec(memory_space=pl.ANY)],
            out_specs=pl.BlockSpec((tm, d), lambda i, dest: (i, 0)),
            scratch_shapes=[pltpu.VMEM((TOP_K, tm, d), F32), pltpu.SemaphoreType.DMA(())]),
        out_shape=jax.ShapeDtypeStruct((t, d), F32),
        compiler_params=_params(("arbitrary",), blocks, resident),
        name="moe_combine",
    )(dest_flat, x1, route, y)


def _pad_cols(w, n):
    return jnp.pad(w, ((0, 0), (0, n - w.shape[1])))


def kernel(x, positions, g_mix, w_in, b_gate, q_norm_g, kv_norm_g, w_uq, w_ukv, a_q_norm_g, a_k_norm_g,
           b_q_norm_g, b_k_norm_g, rel_bias, w_o_a, w_o_b, w_out, g_ffn, w_group, w_expert,
           w_exp_gate, w_exp_up, w_exp_down):
    batch, seq, d = x.shape
    t = batch * seq
    q_lora, kv_lora = q_norm_g.shape[0], kv_norm_g.shape[0]
    a_heads = w_uq.shape[1] // A_QK
    b_heads = w_o_b.shape[0] // B_HEAD_DIM
    b_width = b_heads * B_HEAD_DIM
    n_groups, n_experts = w_group.shape[1], w_expert.shape[1]
    per_group = n_experts // n_groups
    d_expert = w_exp_gate.shape[2]
    off_b = q_lora + kv_lora + A_ROPE
    assert seq % ATTN_TILE == 0 and (TOP_K * t) % MOE_ROWS == 0 and n_groups + n_experts <= LANE

    xf = x.reshape(t, d)
    tm_big = min(1024, t)
    tn = _tile(b_width, 512)
    assert d % tn == 0

    za_cols = -(-(off_b + A_ROPE) // 512) * 512
    w_in_a = _pad_cols(w_in[:, :off_b], za_cols).astype(BF16)
    w_in_b = w_in[:, off_b:].astype(BF16)
    wuq = jnp.pad(w_uq.reshape(q_lora, a_heads, A_QK), ((0, 0), (0, 0), (0, A_HEAD_PAD - A_QK)))
    wuq = wuq.reshape(q_lora, a_heads * A_HEAD_PAD).astype(BF16)
    wukv = w_ukv.astype(BF16)
    pad_gain = lambda g, s: jnp.pad(g * s, (0, A_HEAD_PAD - A_QK)).reshape(1, A_HEAD_PAD)
    gaq = pad_gain(a_q_norm_g, A_QK ** -0.5)
    gak = pad_gain(a_k_norm_g, 1.0)
    gb = jnp.concatenate([jnp.tile(b_q_norm_g * B_HEAD_DIM ** -0.5, b_heads), jnp.tile(b_k_norm_g, b_heads),
                          jnp.ones((b_width,), F32), b_gate]).reshape(1, -1)

    half = A_ROPE // 2
    inv = ROPE_THETA ** (-jnp.arange(half, dtype=F32) / half)
    ang = positions.astype(F32).reshape(t, 1) * inv
    cos, sin = jnp.cos(ang), jnp.sin(ang)
    zeros = jnp.zeros((t, LANE), F32)
    cos_t = zeros.at[:, :half].set(cos).at[:, half:A_ROPE].set(cos)
    sin_lo = zeros.at[:, :half].set(-sin)
    sin_hi = zeros.at[:, half:A_ROPE].set(sin)

    xn = _rmsnorm_rows(xf, g_mix, BF16, min(256, t))
    za = _matmul(xn, w_in_a, F32, tm_big, _tile(za_cols, 512))
    zb = _inproj_b(xn, w_in_b, gb, tm_big, tn, 2 * b_width // tn, 3 * b_width // tn)
    q, k, v = _mla_proj(za, cos_t, sin_lo, sin_hi, wuq, wukv, q_norm_g.reshape(1, -1), kv_norm_g.reshape(1, -1),
                        gaq, gak, a_heads, min(256, t))
    o_a = _mla_attn(q, k, v, batch, seq, a_heads)
    o_b = _band_attn(zb, _band_bias_table(rel_bias), batch, seq, b_heads)
    merged = _merge(o_a, o_b, w_o_a.astype(BF16), w_o_b.astype(BF16), zb, 3 * b_width, tm_big, tn)
    x1 = _out_proj(merged, w_out.astype(BF16), xf, tm_big, tn)

    wr = _pad_cols(jnp.concatenate([w_group, w_expert], axis=1), LANE)
    xn2, route = _router(x1, g_ffn, wr, n_groups, per_group, min(256, t))
    rank, starts_f = _rank(route, min(512, t))
    dest = _dest(route, rank, starts_f, min(512, t))[:, :TOP_K].reshape(-1)
    xs = _scatter_rows(dest, xn2, min(512, t))
    starts = starts_f[0, :n_experts].astype(jnp.int32)
    counts = jnp.concatenate([starts[1:], jnp.full((1,), TOP_K * t, jnp.int32)]) - starts
    items = _work_items(starts, counts, TOP_K * t)
    h = _moe_up(items, xs, w_exp_gate, w_exp_up, _tile(d_expert, 256))
    y = _moe_down(items, h, w_exp_down)
    out = _combine(dest, x1, route, y, min(256, t))
    return out.reshape(batch, seq, d)
```

```python
import functools

import jax
import jax.numpy as jnp
from jax import lax
from jax.experimental import pallas as pl
from jax.experimental.pallas import tpu as pltpu

F32 = jnp.float32
BF16 = jnp.bfloat16

CHUNK = 64
EPS = 1e-6
A_NOPE = 128
A_ROPE = 64
A_V = 128
A_QK = A_NOPE + A_ROPE
B_HEAD_DIM = 128
B_LEFT_CHUNKS = 8
B_MAX_REL = 128
ROPE_THETA = 10000.0
TOP_K = 2

LANE = 128
A_HEAD_PAD = 2 * LANE
V7X_VMEM_BYTES = 64 * 2**20

ATTN_TILE = 256
MOE_ROWS = 256
NEG_INF = float("-inf")


def _nbytes(shape, dtype):
    n = 1
    for s in shape:
        n *= s
    return n * jnp.dtype(dtype).itemsize


def _params(semantics, pipelined_bytes, resident_bytes=0):
    need = 2 * pipelined_bytes + resident_bytes
    return pltpu.CompilerParams(dimension_semantics=semantics,
                                vmem_limit_bytes=min(int(need), V7X_VMEM_BYTES))


def _tile(n, want):
    t = want
    while t > LANE and n % t:
        t //= 2
    assert n % t == 0, (n, want)
    return t


def _rmsnorm_kernel(x_ref, g_ref, o_ref):
    x = x_ref[...]
    r = lax.rsqrt(jnp.mean(x * x, axis=-1, keepdims=True) + EPS)
    o_ref[...] = (x * r * g_ref[...]).astype(o_ref.dtype)


def _rmsnorm_rows(x, g, out_dtype, tm):
    t, d = x.shape
    blocks = _nbytes((tm, d), F32) + _nbytes((tm, d), out_dtype) + _nbytes((1, d), F32)
    return pl.pallas_call(
        _rmsnorm_kernel,
        grid=(t // tm,),
        in_specs=[pl.BlockSpec((tm, d), lambda i: (i, 0)), pl.BlockSpec((1, d), lambda i: (0, 0))],
        out_specs=pl.BlockSpec((tm, d), lambda i: (i, 0)),
        out_shape=jax.ShapeDtypeStruct((t, d), out_dtype),
        compiler_params=_params(("parallel",), blocks, _nbytes((tm, d), F32)),
        name="rmsnorm",
    )(x, g.reshape(1, d))


def _matmul_kernel(a_ref, w_ref, o_ref):
    o_ref[...] = jnp.dot(a_ref[...], w_ref[...], preferred_element_type=F32).astype(o_ref.dtype)


def _matmul(a, w, out_dtype, tm, tn):
    m, k = a.shape
    n = w.shape[1]
    blocks = _nbytes((tm, k), a.dtype) + _nbytes((k, tn), w.dtype) + _nbytes((tm, tn), out_dtype)
    return pl.pallas_call(
        _matmul_kernel,
        grid=(m // tm, n // tn),
        in_specs=[pl.BlockSpec((tm, k), lambda i, j: (i, 0)), pl.BlockSpec((k, tn), lambda i, j: (0, j))],
        out_specs=pl.BlockSpec((tm, tn), lambda i, j: (i, j)),
        out_shape=jax.ShapeDtypeStruct((m, n), out_dtype),
        compiler_params=_params(("parallel", "arbitrary"), blocks, _nbytes((tm, tn), F32)),
        name="inproj_a",
    )(a, w)


def _inproj_b_kernel(a_ref, w_ref, gb_ref, o_ref, *, n_norm_blocks, n_plain_end):
    j = pl.program_id(1)
    acc = jnp.dot(a_ref[...], w_ref[...], preferred_element_type=F32)

    @pl.when(j < n_norm_blocks)
    def _():
        for h in range(acc.shape[1] // B_HEAD_DIM):
            sl = slice(h * B_HEAD_DIM, (h + 1) * B_HEAD_DIM)
            z = acc[:, sl]
            r = lax.rsqrt(jnp.mean(z * z, axis=-1, keepdims=True) + EPS)
            o_ref[:, sl] = (z * r * gb_ref[:, sl]).astype(o_ref.dtype)

    @pl.when((j >= n_norm_blocks) & (j < n_plain_end))
    def _():
        o_ref[...] = acc.astype(o_ref.dtype)

    @pl.when(j >= n_plain_end)
    def _():
        o_ref[...] = (1.0 / (1.0 + jnp.exp(-(acc + gb_ref[...])))).astype(o_ref.dtype)


def _inproj_b(xn, w, gb, tm, tn, n_norm_blocks, n_plain_end):
    m, k = xn.shape
    n = w.shape[1]
    blocks = (_nbytes((tm, k), xn.dtype) + _nbytes((k, tn), w.dtype) + _nbytes((tm, tn), BF16)
              + _nbytes((1, tn), F32))
    return pl.pallas_call(
        functools.partial(_inproj_b_kernel, n_norm_blocks=n_norm_blocks, n_plain_end=n_plain_end),
        grid=(m // tm, n // tn),
        in_specs=[pl.BlockSpec((tm, k), lambda i, j: (i, 0)),
                  pl.BlockSpec((k, tn), lambda i, j: (0, j)),
                  pl.BlockSpec((1, tn), lambda i, j: (0, j))],
        out_specs=pl.BlockSpec((tm, tn), lambda i, j: (i, j)),
        out_shape=jax.ShapeDtypeStruct((m, n), BF16),
        compiler_params=_params(("parallel", "arbitrary"), blocks, 2 * _nbytes((tm, tn), F32)),
        name="inproj_b",
    )(xn, w, gb)


def _rope_padded(v, cos, sin_lo, sin_hi):
    half = A_ROPE // 2
    return v * cos + pltpu.roll(v, LANE - half, 1) * sin_lo + pltpu.roll(v, half, 1) * sin_hi


def _mla_proj_kernel(za_ref, cos_ref, sl_ref, sh_ref, wuq_ref, wukv_ref, gq_ref, gkv_ref, gaq_ref, gak_ref,
                     q_ref, k_ref, v_ref, *, heads, q_lora, kv_lora):
    cos, sin_lo, sin_hi = cos_ref[...], sl_ref[...], sh_ref[...]

    def norm(z, g):
        r = lax.rsqrt(jnp.mean(z * z, axis=-1, keepdims=True) + EPS)
        return (z * r * g).astype(BF16)

    cq = norm(za_ref[:, :q_lora], gq_ref[...])
    ckv = norm(za_ref[:, q_lora:q_lora + kv_lora], gkv_ref[...])
    k_rope = za_ref[:, q_lora + kv_lora:q_lora + kv_lora + LANE]
    qacc = jnp.dot(cq, wuq_ref[...], preferred_element_type=F32)
    kvacc = jnp.dot(ckv, wukv_ref[...], preferred_element_type=F32)

    gq_lo, gq_hi = gaq_ref[:, :LANE], gaq_ref[:, LANE:]
    gk_lo, gk_hi = gak_ref[:, :LANE], gak_ref[:, LANE:]
    kr_ss = jnp.sum(k_rope * k_rope, axis=-1, keepdims=True)
    kr_rot = _rope_padded(k_rope * gk_hi, cos, sin_lo, sin_hi)
    for h in range(heads):
        base = h * A_HEAD_PAD
        q_lo = qacc[:, base:base + LANE]
        q_hi = qacc[:, base + LANE:base + A_HEAD_PAD]
        ss = jnp.sum(q_lo * q_lo, axis=-1, keepdims=True) + jnp.sum(q_hi * q_hi, axis=-1, keepdims=True)
        r = lax.rsqrt(ss / A_QK + EPS)
        q_ref[:, base:base + LANE] = (q_lo * r * gq_lo).astype(BF16)
        q_ref[:, base + LANE:base + A_HEAD_PAD] = _rope_padded(q_hi * r * gq_hi, cos, sin_lo, sin_hi).astype(BF16)

        k_lo = kvacc[:, base:base + LANE]
        ssk = jnp.sum(k_lo * k_lo, axis=-1, keepdims=True) + kr_ss
        rk = lax.rsqrt(ssk / A_QK + EPS)
        k_ref[:, base:base + LANE] = (k_lo * rk * gk_lo).astype(BF16)
        k_ref[:, base + LANE:base + A_HEAD_PAD] = (kr_rot * rk).astype(BF16)
        v_ref[:, h * A_V:(h + 1) * A_V] = kvacc[:, base + LANE:base + A_HEAD_PAD].astype(BF16)


def _mla_proj(za, cos, sin_lo, sin_hi, wuq, wukv, gq, gkv, gaq, gak, heads, tm):
    t, za_cols = za.shape
    q_lora, kv_lora = wuq.shape[0], wukv.shape[0]
    hp = heads * A_HEAD_PAD
    row = lambda i: (i, 0)
    fix = lambda i: (0, 0)
    blocks = (_nbytes((tm, za_cols), F32) + 3 * _nbytes((tm, LANE), F32) + _nbytes(wuq.shape, BF16)
              + _nbytes(wukv.shape, BF16) + 2 * _nbytes((tm, hp), BF16) + _nbytes((tm, heads * A_V), BF16))
    return pl.pallas_call(
        functools.partial(_mla_proj_kernel, heads=heads, q_lora=q_lora, kv_lora=kv_lora),
        grid=(t // tm,),
        in_specs=[pl.BlockSpec((tm, za_cols), row),
                  pl.BlockSpec((tm, LANE), row), pl.BlockSpec((tm, LANE), row), pl.BlockSpec((tm, LANE), row),
                  pl.BlockSpec(wuq.shape, fix), pl.BlockSpec(wukv.shape, fix),
                  pl.BlockSpec((1, q_lora), fix), pl.BlockSpec((1, kv_lora), fix),
                  pl.BlockSpec((1, A_HEAD_PAD), fix), pl.BlockSpec((1, A_HEAD_PAD), fix)],
        out_specs=[pl.BlockSpec((tm, hp), row), pl.BlockSpec((tm, hp), row), pl.BlockSpec((tm, heads * A_V), row)],
        out_shape=[jax.ShapeDtypeStruct((t, hp), BF16), jax.ShapeDtypeStruct((t, hp), BF16),
                   jax.ShapeDtypeStruct((t, heads * A_V), BF16)],
        compiler_params=_params(("parallel",), blocks, 3 * _nbytes((tm, hp), F32)),
        name="mla_proj",
    )(za, cos, sin_lo, sin_hi, wuq, wukv, gq, gkv, gaq, gak)


def _nt_dot(a, b):
    return lax.dot_general(a, b, (((1,), (1,)), ((), ())), preferred_element_type=F32)


def _mla_attn_kernel(q_ref, k_ref, v_ref, o_ref, *, seq):
    tq = ATTN_TILE
    rc = lax.broadcasted_iota(jnp.int32, (tq, tq), 0) // CHUNK
    cc = lax.broadcasted_iota(jnp.int32, (tq, tq), 1) // CHUNK
    diag_ok = cc <= rc
    for i in range(seq // tq):
        q = q_ref[i * tq:(i + 1) * tq, :]
        sd = jnp.where(diag_ok, _nt_dot(q, k_ref[i * tq:(i + 1) * tq, :]), NEG_INF)
        m = jnp.max(sd, axis=-1, keepdims=True)
        if i > 0:
            s0 = _nt_dot(q, k_ref[:i * tq, :])
            m = jnp.maximum(m, jnp.max(s0, axis=-1, keepdims=True))
        pd = jnp.exp(sd - m)
        l = jnp.sum(pd, axis=-1, keepdims=True)
        o = jnp.dot(pd.astype(BF16), v_ref[i * tq:(i + 1) * tq, :], preferred_element_type=F32)
        if i > 0:
            p0 = jnp.exp(s0 - m)
            l = l + jnp.sum(p0, axis=-1, keepdims=True)
            o = o + jnp.dot(p0.astype(BF16), v_ref[:i * tq, :], preferred_element_type=F32)
        o_ref[i * tq:(i + 1) * tq, :] = (o / l).astype(o_ref.dtype)


def _mla_attn(q, k, v, batch, seq, heads):
    t = batch * seq
    blocks = 2 * _nbytes((seq, A_HEAD_PAD), BF16) + 2 * _nbytes((seq, A_V), BF16)
    temps = 6 * _nbytes((ATTN_TILE, seq), F32)
    return pl.pallas_call(
        functools.partial(_mla_attn_kernel, seq=seq),
        grid=(batch, heads),
        in_specs=[pl.BlockSpec((seq, A_HEAD_PAD), lambda b, h: (b, h)),
                  pl.BlockSpec((seq, A_HEAD_PAD), lambda b, h: (b, h)),
                  pl.BlockSpec((seq, A_V), lambda b, h: (b, h))],
        out_specs=pl.BlockSpec((seq, A_V), lambda b, h: (b, h)),
        out_shape=jax.ShapeDtypeStruct((t, heads * A_V), BF16),
        compiler_params=_params(("parallel", "parallel"), blocks, temps),
        name="mla_attn",
    )(q, k, v)


def _band_attn_kernel(q_ref, k_ref, v_ref, r_ref, o_ref, *, seq):
    tq = ATTN_TILE
    left = B_LEFT_CHUNKS * CHUNK
    width = r_ref.shape[1]
    bias = pltpu.roll(jnp.broadcast_to(r_ref[...], (tq, width)), 0, 1, stride=1, stride_axis=0)[:, :left + tq]
    q_chunk = lax.broadcasted_iota(jnp.int32, (tq, left + tq), 0) // CHUNK
    k_chunk = lax.broadcasted_iota(jnp.int32, (tq, left + tq), 1) // CHUNK
    table = jnp.where((k_chunk >= q_chunk) & (k_chunk <= q_chunk + B_LEFT_CHUNKS), bias, NEG_INF)
    for i in range(seq // tq):
        q0 = i * tq
        k0 = max(0, q0 - left)
        kw = q0 + tq - k0
        u0 = left - q0 + k0
        q = q_ref[q0:q0 + tq, :]
        s = _nt_dot(q, k_ref[k0:k0 + kw, :]) + table[:, u0:u0 + kw]
        m = jnp.max(s, axis=-1, keepdims=True)
        p = jnp.exp(s - m)
        l = jnp.sum(p, axis=-1, keepdims=True)
        o = jnp.dot(p.astype(BF16), v_ref[k0:k0 + kw, :], preferred_element_type=F32)
        o_ref[q0:q0 + tq, :] = (o / l).astype(o_ref.dtype)


def _band_bias_rows(rel_bias):
    left = B_LEFT_CHUNKS * CHUNK
    width = left + 2 * ATTN_TILE
    m = jnp.arange(width, dtype=jnp.int32)
    j = jnp.where(m < left + ATTN_TILE, m, m - width)
    dist = left - j
    rows = rel_bias[:, jnp.clip(dist, -B_MAX_REL, B_MAX_REL) + B_MAX_REL].astype(F32)
    return rows.reshape(rel_bias.shape[0], 1, width)


def _band_attn(zb, bias_rows, batch, seq, heads):
    t = batch * seq
    d = B_HEAD_DIM
    width = bias_rows.shape[2]
    blocks = 4 * _nbytes((seq, d), BF16) + _nbytes((1, width), F32)
    temps = 8 * _nbytes((ATTN_TILE, width), F32)
    return pl.pallas_call(
        functools.partial(_band_attn_kernel, seq=seq),
        grid=(batch, heads),
        in_specs=[pl.BlockSpec((seq, d), lambda b, h: (b, h)),
                  pl.BlockSpec((seq, d), lambda b, h: (b, heads + h)),
                  pl.BlockSpec((seq, d), lambda b, h: (b, 2 * heads + h)),
                  pl.BlockSpec((None, 1, width), lambda b, h: (h, 0, 0))],
        out_specs=pl.BlockSpec((seq, d), lambda b, h: (b, h)),
        out_shape=jax.ShapeDtypeStruct((t, heads * d), BF16),
        compiler_params=_params(("parallel", "parallel"), blocks, temps),
        name="band_attn",
    )(zb, zb, zb, bias_rows)


def _merge_kernel(oa_ref, ob_ref, woa_ref, wob_ref, g0_ref, g1_ref, o_ref):
    a = jnp.dot(oa_ref[...], woa_ref[...], preferred_element_type=F32)
    b = jnp.dot(ob_ref[...], wob_ref[...], preferred_element_type=F32)
    o_ref[...] = (g0_ref[...].astype(F32) * a + g1_ref[...].astype(F32) * b).astype(o_ref.dtype)


def _merge(oa, ob, woa, wob, zb, gate_col, tm, tn):
    m = oa.shape[0]
    d = woa.shape[1]
    g0, g1 = gate_col // tn, (gate_col + d) // tn
    blocks = (_nbytes((tm, oa.shape[1]), BF16) + _nbytes((tm, ob.shape[1]), BF16) + _nbytes((woa.shape[0], tn), BF16)
              + _nbytes((wob.shape[0], tn), BF16) + 3 * _nbytes((tm, tn), BF16))
    return pl.pallas_call(
        _merge_kernel,
        grid=(m // tm, d // tn),
        in_specs=[pl.BlockSpec((tm, oa.shape[1]), lambda i, j: (i, 0)),
                  pl.BlockSpec((tm, ob.shape[1]), lambda i, j: (i, 0)),
                  pl.BlockSpec((woa.shape[0], tn), lambda i, j: (0, j)),
                  pl.BlockSpec((wob.shape[0], tn), lambda i, j: (0, j)),
                  pl.BlockSpec((tm, tn), lambda i, j: (i, g0 + j)),
                  pl.BlockSpec((tm, tn), lambda i, j: (i, g1 + j))],
        out_specs=pl.BlockSpec((tm, tn), lambda i, j: (i, j)),
        out_shape=jax.ShapeDtypeStruct((m, d), BF16),
        compiler_params=_params(("parallel", "arbitrary"), blocks, 3 * _nbytes((tm, tn), F32)),
        name="merge",
    )(oa, ob, woa, wob, zb, zb)


def _out_proj_kernel(a_ref, w_ref, x_ref, o_ref):
    o_ref[...] = x_ref[...] + jnp.dot(a_ref[...], w_ref[...], preferred_element_type=F32)


def _out_proj(merged, wout, x, tm, tn):
    m, k = merged.shape
    n = wout.shape[1]
    blocks = _nbytes((tm, k), BF16) + _nbytes((k, tn), BF16) + 2 * _nbytes((tm, tn), F32)
    return pl.pallas_call(
        _out_proj_kernel,
        grid=(m // tm, n // tn),
        in_specs=[pl.BlockSpec((tm, k), lambda i, j: (i, 0)),
                  pl.BlockSpec((k, tn), lambda i, j: (0, j)),
                  pl.BlockSpec((tm, tn), lambda i, j: (i, j))],
        out_specs=pl.BlockSpec((tm, tn), lambda i, j: (i, j)),
        out_shape=jax.ShapeDtypeStruct((m, n), F32),
        compiler_params=_params(("parallel", "arbitrary"), blocks, _nbytes((tm, tn), F32)),
        name="out_proj",
    )(merged, wout, x)


def _router_kernel(x_ref, g_ref, wr_ref, xn_ref, route_ref, *, n_groups, per_group):
    x = x_ref[...]
    r = lax.rsqrt(jnp.mean(x * x, axis=-1, keepdims=True) + EPS)
    xn = x * r * g_ref[...]
    xn_ref[...] = xn
    logits = jnp.dot(xn, wr_ref[...], preferred_element_type=F32, precision=lax.Precision.HIGHEST)
    lane = lax.broadcasted_iota(jnp.int32, logits.shape, 1).astype(F32)
    far = float(LANE)

    def top(vals):
        best = jnp.max(vals, axis=-1, keepdims=True)
        return best, jnp.min(jnp.where(vals == best, lane, far), axis=-1, keepdims=True)

    gl = jnp.where(lane < n_groups, logits, NEG_INF)
    gmax, grp = top(gl)
    p_grp = 1.0 / jnp.sum(jnp.exp(gl - gmax), axis=-1, keepdims=True)
    lo = n_groups + grp * per_group
    el = jnp.where((lane >= lo) & (lane < lo + per_group), logits, NEG_INF)
    t1, i1 = top(el)
    t2, i2 = top(jnp.where(lane == i1, NEG_INF, el))
    d = jnp.exp(t2 - t1)
    w1 = p_grp / (1.0 + d)
    w2 = p_grp * d / (1.0 + d)
    route_ref[...] = jnp.where(lane == 0, i1 - n_groups,
                               jnp.where(lane == 1, i2 - n_groups,
                                         jnp.where(lane == 2, w1, jnp.where(lane == 3, w2, 0.0))))


def _router(x1, g, wr, n_groups, per_group, tm):
    t, d = x1.shape
    blocks = 2 * _nbytes((tm, d), F32) + _nbytes((1, d), F32) + _nbytes((d, LANE), F32) + _nbytes((tm, LANE), F32)
    return pl.pallas_call(
        functools.partial(_router_kernel, n_groups=n_groups, per_group=per_group),
        grid=(t // tm,),
        in_specs=[pl.BlockSpec((tm, d), lambda i: (i, 0)), pl.BlockSpec((1, d), lambda i: (0, 0)),
                  pl.BlockSpec((d, LANE), lambda i: (0, 0))],
        out_specs=[pl.BlockSpec((tm, d), lambda i: (i, 0)), pl.BlockSpec((tm, LANE), lambda i: (i, 0))],
        out_shape=[jax.ShapeDtypeStruct((t, d), F32), jax.ShapeDtypeStruct((t, LANE), F32)],
        compiler_params=_params(("parallel",), blocks, 2 * _nbytes((tm, d), F32)),
        name="router",
    )(x1, g.reshape(1, d), wr)


def _one_hots(route):
    lane = lax.broadcasted_iota(jnp.int32, route.shape, 1).astype(F32)
    return (lane == route[:, 0:1]).astype(F32), (lane == route[:, 1:2]).astype(F32)


def _rank_kernel(route_ref, rank_ref, starts_ref, count_acc, start_acc):
    i = pl.program_id(0)

    @pl.when(i == 0)
    def _():
        count_acc[...] = jnp.zeros_like(count_acc)
        start_acc[...] = jnp.zeros_like(start_acc)

    oh1, oh2 = _one_hots(route_ref[...])
    oh = (oh1 + oh2).astype(BF16)
    tm = oh.shape[0]
    earlier = (lax.broadcasted_iota(jnp.int32, (tm, tm), 0) > lax.broadcasted_iota(jnp.int32, (tm, tm), 1))
    before = jnp.dot(earlier.astype(BF16), oh, preferred_element_type=F32) + count_acc[...]
    lane = lax.broadcasted_iota(jnp.int32, (tm, LANE), 1)
    rank_ref[...] = jnp.where(lane == 0, jnp.sum(oh1 * before, axis=-1, keepdims=True),
                              jnp.where(lane == 1, jnp.sum(oh2 * before, axis=-1, keepdims=True), 0.0))
    lower = (lax.broadcasted_iota(jnp.int32, (LANE, LANE), 0) < lax.broadcasted_iota(jnp.int32, (LANE, LANE), 1))
    below = jnp.dot(oh, lower.astype(BF16), preferred_element_type=F32)
    count_acc[...] += jnp.sum(oh.astype(F32), axis=0, keepdims=True)
    start_acc[...] += jnp.sum(below, axis=0, keepdims=True)
    starts_ref[...] = start_acc[...]


def _rank(route, tm):
    t = route.shape[0]
    blocks = 2 * _nbytes((tm, LANE), F32) + _nbytes((1, LANE), F32)
    return pl.pallas_call(
        _rank_kernel,
        grid=(t // tm,),
        in_specs=[pl.BlockSpec((tm, LANE), lambda i: (i, 0))],
        out_specs=[pl.BlockSpec((tm, LANE), lambda i: (i, 0)), pl.BlockSpec((1, LANE), lambda i: (0, 0))],
        out_shape=[jax.ShapeDtypeStruct((t, LANE), F32), jax.ShapeDtypeStruct((1, LANE), F32)],
        scratch_shapes=[pltpu.VMEM((1, LANE), F32), pltpu.VMEM((1, LANE), F32)],
        compiler_params=_params(("arbitrary",), blocks, 2 * _nbytes((tm, tm), F32)),
        name="moe_rank",
    )(route)


def _dest_kernel(route_ref, rank_ref, starts_ref, dest_ref):
    oh1, oh2 = _one_hots(route_ref[...])
    rank = rank_ref[...]
    starts = starts_ref[...]
    d1 = jnp.sum(oh1 * starts, axis=-1, keepdims=True) + rank[:, 0:1]
    d2 = jnp.sum(oh2 * starts, axis=-1, keepdims=True) + rank[:, 1:2]
    lane = lax.broadcasted_iota(jnp.int32, rank.shape, 1)
    dest_ref[...] = jnp.where(lane == 0, d1, jnp.where(lane == 1, d2, 0.0)).astype(jnp.int32)


def _dest(route, rank, starts, tm):
    t = route.shape[0]
    blocks = 3 * _nbytes((tm, LANE), F32) + _nbytes((1, LANE), F32)
    return pl.pallas_call(
        _dest_kernel,
        grid=(t // tm,),
        in_specs=[pl.BlockSpec((tm, LANE), lambda i: (i, 0)), pl.BlockSpec((tm, LANE), lambda i: (i, 0)),
                  pl.BlockSpec((1, LANE), lambda i: (0, 0))],
        out_specs=pl.BlockSpec((tm, LANE), lambda i: (i, 0)),
        out_shape=jax.ShapeDtypeStruct((t, LANE), jnp.int32),
        compiler_params=_params(("parallel",), blocks, 4 * _nbytes((tm, LANE), F32)),
        name="moe_dest",
    )(route, rank, starts)


def _gather_rows_kernel(dest_ref, x_hbm, xs_ref, source, sem, *, n_assign):
    rows = xs_ref.shape[0]
    base = pl.program_id(0) * rows

    @pl.when(pl.program_id(0) == 0)
    def _():
        def invert(a, carry):
            source[dest_ref[a]] = a
            return carry
        lax.fori_loop(0, n_assign, invert, 0)

    def copy(j):
        return pltpu.make_async_copy(x_hbm.at[pl.ds(source[base + j] // TOP_K, 1)], xs_ref.at[pl.ds(j, 1)], sem)

    def start(j, carry):
        copy(j).start()
        return carry

    def wait(j, carry):
        copy(j).wait()
        return carry

    lax.fori_loop(0, rows, start, 0)
    lax.fori_loop(0, rows, wait, 0)


def _gather_rows(dest_flat, xn, rows):
    t, d = xn.shape
    n_assign = dest_flat.shape[0]
    return pl.pallas_call(
        functools.partial(_gather_rows_kernel, n_assign=n_assign),
        grid_spec=pltpu.PrefetchScalarGridSpec(
            num_scalar_prefetch=1,
            grid=(n_assign // rows,),
            in_specs=[pl.BlockSpec(memory_space=pl.ANY)],
            out_specs=pl.BlockSpec((rows, d), lambda i, dest: (i, 0)),
            scratch_shapes=[pltpu.SMEM((n_assign,), jnp.int32), pltpu.SemaphoreType.DMA(())]),
        out_shape=jax.ShapeDtypeStruct((n_assign, d), xn.dtype),
        compiler_params=_params(("arbitrary",), _nbytes((rows, d), xn.dtype)),
        name="moe_gather_rows",
    )(dest_flat, xn)


def _work_items(starts, n_rows):
    n_exp = starts.shape[0]
    n_blk = n_rows // MOE_ROWS
    total = jnp.full((1,), n_rows, jnp.int32)
    pts = jnp.concatenate([jnp.arange(n_blk, dtype=jnp.int32) * MOE_ROWS, starts[1:]])
    idx = jnp.arange(pts.shape[0], dtype=jnp.int32)
    before = (pts[None, :] < pts[:, None]) | ((pts[None, :] == pts[:, None]) & (idx[None, :] < idx[:, None]))
    pos = jnp.sum(before.astype(jnp.int32), axis=1)
    lo = jnp.sum(jnp.where(pos[:, None] == idx[None, :], pts[:, None], 0), axis=0)
    hi = jnp.concatenate([lo[1:], total])
    ends = jnp.concatenate([starts[1:], total])
    r = jnp.minimum(lo // MOE_ROWS, n_blk - 1)
    e = jnp.minimum(jnp.sum((ends[None, :] <= lo[:, None]).astype(jnp.int32), axis=1), n_exp - 1)
    return r, e, lo, hi


def _item(r_ref, e_ref, lo_ref, hi_ref, w):
    e = e_ref[w]
    new_expert = (w == 0) | (e != e_ref[jnp.maximum(w - 1, 0)])
    return r_ref[w], lo_ref[w], hi_ref[w], new_expert


def _store_item_rows(o_ref, val, r, lo, hi):
    rows = r * MOE_ROWS + lax.broadcasted_iota(jnp.int32, (MOE_ROWS, 1), 0)
    mine = (rows >= lo) & (rows < hi)

    @pl.when(lo == r * MOE_ROWS)
    def _():
        o_ref[...] = val

    @pl.when(lo != r * MOE_ROWS)
    def _():
        o_ref[...] = jnp.where(mine, val, o_ref[...])


def _moe_up_kernel(r_ref, e_ref, lo_ref, hi_ref, xs_ref, wg_ref, wu_ref, h_ref, wg_bf, wu_bf):
    r, lo, hi, new_expert = _item(r_ref, e_ref, lo_ref, hi_ref, pl.program_id(1))

    @pl.when(new_expert)
    def _():
        wg_bf[...] = wg_ref[...].astype(BF16)
        wu_bf[...] = wu_ref[...].astype(BF16)

    @pl.when(hi > lo)
    def _():
        x = xs_ref[...].astype(BF16)
        g = jnp.dot(x, wg_bf[...], preferred_element_type=F32)
        u = jnp.dot(x, wu_bf[...], preferred_element_type=F32)
        h = (g * (1.0 / (1.0 + jnp.exp(-g)))) * u
        _store_item_rows(h_ref, h.astype(h_ref.dtype), r, lo, hi)


def _moe_up(items, xs, wg, wu, tf):
    n_rows, d = xs.shape
    n_exp, _, f = wg.shape
    n_items = items[0].shape[0]
    blocks = _nbytes((MOE_ROWS, d), xs.dtype) + 2 * _nbytes((d, tf), F32) + _nbytes((MOE_ROWS, tf), BF16)
    resident = 2 * _nbytes((d, tf), BF16) + _nbytes((MOE_ROWS, d), BF16) + 4 * _nbytes((MOE_ROWS, tf), F32)
    wspec = pl.BlockSpec((None, d, tf), lambda c, w, r, e, lo, hi: (e[w], 0, c))
    return pl.pallas_call(
        _moe_up_kernel,
        grid_spec=pltpu.PrefetchScalarGridSpec(
            num_scalar_prefetch=4,
            grid=(f // tf, n_items),
            in_specs=[pl.BlockSpec((MOE_ROWS, d), lambda c, w, r, e, lo, hi: (r[w], 0)), wspec, wspec],
            out_specs=pl.BlockSpec((MOE_ROWS, tf), lambda c, w, r, e, lo, hi: (r[w], c)),
            scratch_shapes=[pltpu.VMEM((d, tf), BF16), pltpu.VMEM((d, tf), BF16)]),
        out_shape=jax.ShapeDtypeStruct((n_rows, f), BF16),
        compiler_params=_params(("arbitrary", "arbitrary"), blocks, resident),
        name="moe_up",
    )(*items, xs, wg, wu)


def _moe_down_kernel(r_ref, e_ref, lo_ref, hi_ref, h_ref, wd_ref, y_ref, wd_bf):
    r, lo, hi, new_expert = _item(r_ref, e_ref, lo_ref, hi_ref, pl.program_id(0))

    @pl.when(new_expert)
    def _():
        wd_bf[...] = wd_ref[...].astype(BF16)

    @pl.when(hi > lo)
    def _():
        y = jnp.dot(h_ref[...], wd_bf[...], preferred_element_type=F32)
        _store_item_rows(y_ref, y, r, lo, hi)


def _moe_down(items, h, wd):
    n_rows, f = h.shape
    d = wd.shape[2]
    n_items = items[0].shape[0]
    blocks = _nbytes((MOE_ROWS, f), BF16) + _nbytes((f, d), F32) + _nbytes((MOE_ROWS, d), F32)
    resident = _nbytes((f, d), BF16) + 2 * _nbytes((MOE_ROWS, d), F32)
    return pl.pallas_call(
        _moe_down_kernel,
        grid_spec=pltpu.PrefetchScalarGridSpec(
            num_scalar_prefetch=4,
            grid=(n_items,),
            in_specs=[pl.BlockSpec((MOE_ROWS, f), lambda w, r, e, lo, hi: (r[w], 0)),
                      pl.BlockSpec((None, f, d), lambda w, r, e, lo, hi: (e[w], 0, 0))],
            out_specs=pl.BlockSpec((MOE_ROWS, d), lambda w, r, e, lo, hi: (r[w], 0)),
            scratch_shapes=[pltpu.VMEM((f, d), BF16)]),
        out_shape=jax.ShapeDtypeStruct((n_rows, d), F32),
        compiler_params=_params(("arbitrary",), blocks, resident),
        name="moe_down",
    )(*items, h, wd)


def _combine_kernel(dest_ref, x_ref, route_ref, y_hbm, o_ref, ybuf, sem, *, tm):
    base = pl.program_id(0) * tm

    def copies(t):
        return [pltpu.make_async_copy(y_hbm.at[pl.ds(dest_ref[TOP_K * (base + t) + k], 1)],
                                      ybuf.at[k, pl.ds(t, 1)], sem) for k in range(TOP_K)]

    def start(t, carry):
        for c in copies(t):
            c.start()
        return carry

    def wait(t, carry):
        for c in copies(t):
            c.wait()
        return carry

    lax.fori_loop(0, tm, start, 0)
    lax.fori_loop(0, tm, wait, 0)
    route = route_ref[...]
    o_ref[...] = x_ref[...] + (route[:, 2:3] * ybuf[0] + route[:, 3:4] * ybuf[1])


def _combine(dest_flat, x1, route, y, tm):
    t, d = x1.shape
    blocks = 2 * _nbytes((tm, d), F32) + _nbytes((tm, LANE), F32)
    resident = TOP_K * _nbytes((tm, d), F32) + _nbytes((tm, d), F32)
    return pl.pallas_call(
        functools.partial(_combine_kernel, tm=tm),
        grid_spec=pltpu.PrefetchScalarGridSpec(
            num_scalar_prefetch=1,
            grid=(t // tm,),
            in_specs=[pl.BlockSpec((tm, d), lambda i, dest: (i, 0)),
                      pl.BlockSpec((tm, LANE), lambda i, dest: (i, 0)),
                      pl.BlockSpec(memory_space=pl.ANY)],
            out_specs=pl.BlockSpec((tm, d), lambda i, dest: (i, 0)),
            scratch_shapes=[pltpu.VMEM((TOP_K, tm, d), F32), pltpu.SemaphoreType.DMA(())]),
        out_shape=jax.ShapeDtypeStruct((t, d), F32),
        compiler_params=_params(("arbitrary",), blocks, resident),
        name="moe_combine",
    )(dest_flat, x1, route, y)


def _pad_cols(w, n):
    return jnp.pad(w, ((0, 0), (0, n - w.shape[1])))


def kernel(x, positions, g_mix, w_in, b_gate, q_norm_g, kv_norm_g, w_uq, w_ukv, a_q_norm_g, a_k_norm_g,
           b_q_norm_g, b_k_norm_g, rel_bias, w_o_a, w_o_b, w_out, g_ffn, w_group, w_expert,
           w_exp_gate, w_exp_up, w_exp_down):
    batch, seq, d = x.shape
    t = batch * seq
    q_lora, kv_lora = q_norm_g.shape[0], kv_norm_g.shape[0]
    a_heads = w_uq.shape[1] // A_QK
    b_heads = w_o_b.shape[0] // B_HEAD_DIM
    b_width = b_heads * B_HEAD_DIM
    n_groups, n_experts = w_group.shape[1], w_expert.shape[1]
    per_group = n_experts // n_groups
    d_expert = w_exp_gate.shape[2]
    off_b = q_lora + kv_lora + A_ROPE
    assert seq % ATTN_TILE == 0 and (TOP_K * t) % MOE_ROWS == 0 and n_groups + n_experts <= LANE

    xf = x.reshape(t, d)
    tm_big = min(1024, t)
    tn = _tile(b_width, 512)
    assert d % tn == 0

    za_cols = -(-(off_b + A_ROPE) // 512) * 512
    w_in_a = _pad_cols(w_in[:, :off_b], za_cols).astype(BF16)
    w_in_b = w_in[:, off_b:].astype(BF16)
    wuq = jnp.pad(w_uq.reshape(q_lora, a_heads, A_QK), ((0, 0), (0, 0), (0, A_HEAD_PAD - A_QK)))
    wuq = wuq.reshape(q_lora, a_heads * A_HEAD_PAD).astype(BF16)
    wukv = w_ukv.astype(BF16)
    pad_gain = lambda g, s: jnp.pad(g * s, (0, A_HEAD_PAD - A_QK)).reshape(1, A_HEAD_PAD)
    gaq = pad_gain(a_q_norm_g, A_QK ** -0.5)
    gak = pad_gain(a_k_norm_g, 1.0)
    gb = jnp.concatenate([jnp.tile(b_q_norm_g * B_HEAD_DIM ** -0.5, b_heads), jnp.tile(b_k_norm_g, b_heads),
                          jnp.ones((b_width,), F32), b_gate]).reshape(1, -1)

    half = A_ROPE // 2
    inv = ROPE_THETA ** (-jnp.arange(half, dtype=F32) / half)
    ang = positions.astype(F32).reshape(t, 1) * inv
    cos, sin = jnp.cos(ang), jnp.sin(ang)
    zeros = jnp.zeros((t, half), F32)
    cos_t = jnp.concatenate([cos, cos, zeros, zeros], axis=1)
    sin_lo = jnp.concatenate([-sin, zeros, zeros, zeros], axis=1)
    sin_hi = jnp.concatenate([zeros, sin, zeros, zeros], axis=1)

    xn = _rmsnorm_rows(xf, g_mix, BF16, min(256, t))
    za = _matmul(xn, w_in_a, F32, tm_big, _tile(za_cols, 512))
    zb = _inproj_b(xn, w_in_b, gb, tm_big, tn, 2 * b_width // tn, 3 * b_width // tn)
    q, k, v = _mla_proj(za, cos_t, sin_lo, sin_hi, wuq, wukv, q_norm_g.reshape(1, -1), kv_norm_g.reshape(1, -1),
                        gaq, gak, a_heads, min(256, t))
    o_a = _mla_attn(q, k, v, batch, seq, a_heads)
    o_b = _band_attn(zb, _band_bias_rows(rel_bias), batch, seq, b_heads)
    merged = _merge(o_a, o_b, w_o_a.astype(BF16), w_o_b.astype(BF16), zb, 3 * b_width, tm_big, tn)
    x1 = _out_proj(merged, w_out.astype(BF16), xf, tm_big, tn)

    wr = _pad_cols(jnp.concatenate([w_group, w_expert], axis=1), LANE)
    xn2, route = _router(x1, g_ffn, wr, n_groups, per_group, min(256, t))
    rank, starts_f = _rank(route, min(512, t))
    dest = _dest(route, rank, starts_f, min(512, t))[:, :TOP_K].reshape(-1)
    xs = _gather_rows(dest, xn2, MOE_ROWS)
    items = _work_items(starts_f[0, :n_experts].astype(jnp.int32), TOP_K * t)
    h = _moe_up(items, xs, w_exp_gate, w_exp_up, _tile(d_expert, 256))
    y = _moe_down(items, h, w_exp_down)
    out = _combine(dest, x1, route, y, min(256, t))
    return out.reshape(batch, seq, d)
```

```python
import functools

import jax
import jax.numpy as jnp
from jax import lax
from jax.experimental import pallas as pl
from jax.experimental.pallas import tpu as pltpu

F32 = jnp.float32
BF16 = jnp.bfloat16

CHUNK = 64
EPS = 1e-6
A_NOPE = 128
A_ROPE = 64
A_V = 128
A_QK = A_NOPE + A_ROPE
B_HEAD_DIM = 128
B_LEFT_CHUNKS = 8
B_MAX_REL = 128
ROPE_THETA = 10000.0
TOP_K = 2

LANE = 128
A_HEAD_PAD = 2 * LANE
V7X_VMEM_BYTES = 64 * 2**20

ATTN_TILE = 256
MOE_ROWS = 128
GATHER_ROWS = 256
NEG_INF = float("-inf")


def _nbytes(shape, dtype):
    n = 1
    for s in shape:
        n *= s
    return n * jnp.dtype(dtype).itemsize


def _params(semantics, pipelined_bytes, resident_bytes=0):
    need = 2 * pipelined_bytes + resident_bytes
    return pltpu.CompilerParams(dimension_semantics=semantics,
                                vmem_limit_bytes=min(int(need), V7X_VMEM_BYTES))


def _tile(n, want):
    t = want
    while t > LANE and n % t:
        t //= 2
    assert n % t == 0, (n, want)
    return t


def _rmsnorm_kernel(x_ref, g_ref, o_ref):
    x = x_ref[...]
    r = lax.rsqrt(jnp.mean(x * x, axis=-1, keepdims=True) + EPS)
    o_ref[...] = (x * r * g_ref[...]).astype(o_ref.dtype)


def _rmsnorm_rows(x, g, out_dtype, tm):
    t, d = x.shape
    blocks = _nbytes((tm, d), F32) + _nbytes((tm, d), out_dtype) + _nbytes((1, d), F32)
    return pl.pallas_call(
        _rmsnorm_kernel,
        grid=(t // tm,),
        in_specs=[pl.BlockSpec((tm, d), lambda i: (i, 0)), pl.BlockSpec((1, d), lambda i: (0, 0))],
        out_specs=pl.BlockSpec((tm, d), lambda i: (i, 0)),
        out_shape=jax.ShapeDtypeStruct((t, d), out_dtype),
        compiler_params=_params(("parallel",), blocks, _nbytes((tm, d), F32)),
        name="rmsnorm",
    )(x, g.reshape(1, d))


def _matmul_kernel(a_ref, w_ref, o_ref):
    o_ref[...] = jnp.dot(a_ref[...], w_ref[...], preferred_element_type=F32).astype(o_ref.dtype)


def _matmul(a, w, out_dtype, tm, tn):
    m, k = a.shape
    n = w.shape[1]
    blocks = _nbytes((tm, k), a.dtype) + _nbytes((k, tn), w.dtype) + _nbytes((tm, tn), out_dtype)
    return pl.pallas_call(
        _matmul_kernel,
        grid=(m // tm, n // tn),
        in_specs=[pl.BlockSpec((tm, k), lambda i, j: (i, 0)), pl.BlockSpec((k, tn), lambda i, j: (0, j))],
        out_specs=pl.BlockSpec((tm, tn), lambda i, j: (i, j)),
        out_shape=jax.ShapeDtypeStruct((m, n), out_dtype),
        compiler_params=_params(("parallel", "arbitrary"), blocks, _nbytes((tm, tn), F32)),
        name="inproj_a",
    )(a, w)


def _inproj_b_kernel(a_ref, w_ref, gb_ref, o_ref, *, n_norm_blocks, n_plain_end):
    j = pl.program_id(1)
    acc = jnp.dot(a_ref[...], w_ref[...], preferred_element_type=F32)

    @pl.when(j < n_norm_blocks)
    def _():
        for h in range(acc.shape[1] // B_HEAD_DIM):
            sl = slice(h * B_HEAD_DIM, (h + 1) * B_HEAD_DIM)
            z = acc[:, sl]
            r = lax.rsqrt(jnp.mean(z * z, axis=-1, keepdims=True) + EPS)
            o_ref[:, sl] = (z * r * gb_ref[:, sl]).astype(o_ref.dtype)

    @pl.when((j >= n_norm_blocks) & (j < n_plain_end))
    def _():
        o_ref[...] = acc.astype(o_ref.dtype)

    @pl.when(j >= n_plain_end)
    def _():
        o_ref[...] = (1.0 / (1.0 + jnp.exp(-(acc + gb_ref[...])))).astype(o_ref.dtype)


def _inproj_b(xn, w, gb, tm, tn, n_norm_blocks, n_plain_end):
    m, k = xn.shape
    n = w.shape[1]
    blocks = (_nbytes((tm, k), xn.dtype) + _nbytes((k, tn), w.dtype) + _nbytes((tm, tn), BF16)
              + _nbytes((1, tn), F32))
    return pl.pallas_call(
        functools.partial(_inproj_b_kernel, n_norm_blocks=n_norm_blocks, n_plain_end=n_plain_end),
        grid=(m // tm, n // tn),
        in_specs=[pl.BlockSpec((tm, k), lambda i, j: (i, 0)),
                  pl.BlockSpec((k, tn), lambda i, j: (0, j)),
                  pl.BlockSpec((1, tn), lambda i, j: (0, j))],
        out_specs=pl.BlockSpec((tm, tn), lambda i, j: (i, j)),
        out_shape=jax.ShapeDtypeStruct((m, n), BF16),
        compiler_params=_params(("parallel", "arbitrary"), blocks, 2 * _nbytes((tm, tn), F32)),
        name="inproj_b",
    )(xn, w, gb)


def _rope_padded(v, cos, sin_lo, sin_hi):
    half = A_ROPE // 2
    return v * cos + pltpu.roll(v, LANE - half, 1) * sin_lo + pltpu.roll(v, half, 1) * sin_hi


def _mla_proj_kernel(za_ref, cos_ref, sl_ref, sh_ref, wuq_ref, wukv_ref, gq_ref, gkv_ref, gaq_ref, gak_ref,
                     q_ref, k_ref, v_ref, *, heads, q_lora, kv_lora):
    cos, sin_lo, sin_hi = cos_ref[...], sl_ref[...], sh_ref[...]

    def norm(z, g):
        r = lax.rsqrt(jnp.mean(z * z, axis=-1, keepdims=True) + EPS)
        return (z * r * g).astype(BF16)

    cq = norm(za_ref[:, :q_lora], gq_ref[...])
    ckv = norm(za_ref[:, q_lora:q_lora + kv_lora], gkv_ref[...])
    k_rope = za_ref[:, q_lora + kv_lora:q_lora + kv_lora + LANE]
    qacc = jnp.dot(cq, wuq_ref[...], preferred_element_type=F32)
    kvacc = jnp.dot(ckv, wukv_ref[...], preferred_element_type=F32)

    gq_lo, gq_hi = gaq_ref[:, :LANE], gaq_ref[:, LANE:]
    gk_lo, gk_hi = gak_ref[:, :LANE], gak_ref[:, LANE:]
    kr_ss = jnp.sum(k_rope * k_rope, axis=-1, keepdims=True)
    kr_rot = _rope_padded(k_rope * gk_hi, cos, sin_lo, sin_hi)
    for h in range(heads):
        base = h * A_HEAD_PAD
        q_lo = qacc[:, base:base + LANE]
        q_hi = qacc[:, base + LANE:base + A_HEAD_PAD]
        ss = jnp.sum(q_lo * q_lo, axis=-1, keepdims=True) + jnp.sum(q_hi * q_hi, axis=-1, keepdims=True)
        r = lax.rsqrt(ss / A_QK + EPS)
        q_ref[:, base:base + LANE] = (q_lo * r * gq_lo).astype(BF16)
        q_ref[:, base + LANE:base + A_HEAD_PAD] = _rope_padded(q_hi * r * gq_hi, cos, sin_lo, sin_hi).astype(BF16)

        k_lo = kvacc[:, base:base + LANE]
        ssk = jnp.sum(k_lo * k_lo, axis=-1, keepdims=True) + kr_ss
        rk = lax.rsqrt(ssk / A_QK + EPS)
        k_ref[:, base:base + LANE] = (k_lo * rk * gk_lo).astype(BF16)
        k_ref[:, base + LANE:base + A_HEAD_PAD] = (kr_rot * rk).astype(BF16)
        v_ref[:, h * A_V:(h + 1) * A_V] = kvacc[:, base + LANE:base + A_HEAD_PAD].astype(BF16)


def _mla_proj(za, cos, sin_lo, sin_hi, wuq, wukv, gq, gkv, gaq, gak, heads, tm):
    t, za_cols = za.shape
    q_lora, kv_lora = wuq.shape[0], wukv.shape[0]
    hp = heads * A_HEAD_PAD
    row = lambda i: (i, 0)
    fix = lambda i: (0, 0)
    blocks = (_nbytes((tm, za_cols), F32) + 3 * _nbytes((tm, LANE), F32) + _nbytes(wuq.shape, BF16)
              + _nbytes(wukv.shape, BF16) + 2 * _nbytes((tm, hp), BF16) + _nbytes((tm, heads * A_V), BF16))
    return pl.pallas_call(
        functools.partial(_mla_proj_kernel, heads=heads, q_lora=q_lora, kv_lora=kv_lora),
        grid=(t // tm,),
        in_specs=[pl.BlockSpec((tm, za_cols), row),
                  pl.BlockSpec((tm, LANE), row), pl.BlockSpec((tm, LANE), row), pl.BlockSpec((tm, LANE), row),
                  pl.BlockSpec(wuq.shape, fix), pl.BlockSpec(wukv.shape, fix),
                  pl.BlockSpec((1, q_lora), fix), pl.BlockSpec((1, kv_lora), fix),
                  pl.BlockSpec((1, A_HEAD_PAD), fix), pl.BlockSpec((1, A_HEAD_PAD), fix)],
        out_specs=[pl.BlockSpec((tm, hp), row), pl.BlockSpec((tm, hp), row), pl.BlockSpec((tm, heads * A_V), row)],
        out_shape=[jax.ShapeDtypeStruct((t, hp), BF16), jax.ShapeDtypeStruct((t, hp), BF16),
                   jax.ShapeDtypeStruct((t, heads * A_V), BF16)],
        compiler_params=_params(("parallel",), blocks, 3 * _nbytes((tm, hp), F32)),
        name="mla_proj",
    )(za, cos, sin_lo, sin_hi, wuq, wukv, gq, gkv, gaq, gak)


def _nt_dot(a, b):
    return lax.dot_general(a, b, (((1,), (1,)), ((), ())), preferred_element_type=F32)


def _mla_attn_kernel(q_ref, k_ref, v_ref, o_ref, *, seq):
    tq = ATTN_TILE
    rc = lax.broadcasted_iota(jnp.int32, (tq, tq), 0) // CHUNK
    cc = lax.broadcasted_iota(jnp.int32, (tq, tq), 1) // CHUNK
    diag_ok = cc <= rc
    for i in range(seq // tq):
        q = q_ref[i * tq:(i + 1) * tq, :]
        sd = jnp.where(diag_ok, _nt_dot(q, k_ref[i * tq:(i + 1) * tq, :]), NEG_INF)
        m = jnp.max(sd, axis=-1, keepdims=True)
        if i > 0:
            s0 = _nt_dot(q, k_ref[:i * tq, :])
            m = jnp.maximum(m, jnp.max(s0, axis=-1, keepdims=True))
        pd = jnp.exp(sd - m)
        l = jnp.sum(pd, axis=-1, keepdims=True)
        o = jnp.dot(pd.astype(BF16), v_ref[i * tq:(i + 1) * tq, :], preferred_element_type=F32)
        if i > 0:
            p0 = jnp.exp(s0 - m)
            l = l + jnp.sum(p0, axis=-1, keepdims=True)
            o = o + jnp.dot(p0.astype(BF16), v_ref[:i * tq, :], preferred_element_type=F32)
        o_ref[i * tq:(i + 1) * tq, :] = (o / l).astype(o_ref.dtype)


def _mla_attn(q, k, v, batch, seq, heads):
    t = batch * seq
    blocks = 2 * _nbytes((seq, A_HEAD_PAD), BF16) + 2 * _nbytes((seq, A_V), BF16)
    temps = 6 * _nbytes((ATTN_TILE, seq), F32)
    return pl.pallas_call(
        functools.partial(_mla_attn_kernel, seq=seq),
        grid=(batch, heads),
        in_specs=[pl.BlockSpec((seq, A_HEAD_PAD), lambda b, h: (b, h)),
                  pl.BlockSpec((seq, A_HEAD_PAD), lambda b, h: (b, h)),
                  pl.BlockSpec((seq, A_V), lambda b, h: (b, h))],
        out_specs=pl.BlockSpec((seq, A_V), lambda b, h: (b, h)),
        out_shape=jax.ShapeDtypeStruct((t, heads * A_V), BF16),
        compiler_params=_params(("parallel", "parallel"), blocks, temps),
        name="mla_attn",
    )(q, k, v)


def _band_attn_kernel(q_ref, k_ref, v_ref, r_ref, o_ref, *, seq):
    tq = ATTN_TILE
    left = B_LEFT_CHUNKS * CHUNK
    width = r_ref.shape[1]
    bias = pltpu.roll(jnp.broadcast_to(r_ref[...], (tq, width)), 0, 1, stride=1, stride_axis=0)[:, :left + tq]
    q_chunk = lax.broadcasted_iota(jnp.int32, (tq, left + tq), 0) // CHUNK
    k_chunk = lax.broadcasted_iota(jnp.int32, (tq, left + tq), 1) // CHUNK
    table = jnp.where((k_chunk >= q_chunk) & (k_chunk <= q_chunk + B_LEFT_CHUNKS), bias, NEG_INF)
    for i in range(seq // tq):
        q0 = i * tq
        k0 = max(0, q0 - left)
        kw = q0 + tq - k0
        u0 = left - q0 + k0
        q = q_ref[q0:q0 + tq, :]
        s = _nt_dot(q, k_ref[k0:k0 + kw, :]) + table[:, u0:u0 + kw]
        m = jnp.max(s, axis=-1, keepdims=True)
        p = jnp.exp(s - m)
        l = jnp.sum(p, axis=-1, keepdims=True)
        o = jnp.dot(p.astype(BF16), v_ref[k0:k0 + kw, :], preferred_element_type=F32)
        o_ref[q0:q0 + tq, :] = (o / l).astype(o_ref.dtype)


def _band_bias_rows(rel_bias):
    left = B_LEFT_CHUNKS * CHUNK
    width = left + 2 * ATTN_TILE
    m = jnp.arange(width, dtype=jnp.int32)
    j = jnp.where(m < left + ATTN_TILE, m, m - width)
    dist = left - j
    rows = rel_bias[:, jnp.clip(dist, -B_MAX_REL, B_MAX_REL) + B_MAX_REL].astype(F32)
    return rows.reshape(rel_bias.shape[0], 1, width)


def _band_attn(zb, bias_rows, batch, seq, heads):
    t = batch * seq
    d = B_HEAD_DIM
    width = bias_rows.shape[2]
    blocks = 4 * _nbytes((seq, d), BF16) + _nbytes((1, width), F32)
    temps = 8 * _nbytes((ATTN_TILE, width), F32)
    return pl.pallas_call(
        functools.partial(_band_attn_kernel, seq=seq),
        grid=(batch, heads),
        in_specs=[pl.BlockSpec((seq, d), lambda b, h: (b, h)),
                  pl.BlockSpec((seq, d), lambda b, h: (b, heads + h)),
                  pl.BlockSpec((seq, d), lambda b, h: (b, 2 * heads + h)),
                  pl.BlockSpec((None, 1, width), lambda b, h: (h, 0, 0))],
        out_specs=pl.BlockSpec((seq, d), lambda b, h: (b, h)),
        out_shape=jax.ShapeDtypeStruct((t, heads * d), BF16),
        compiler_params=_params(("parallel", "parallel"), blocks, temps),
        name="band_attn",
    )(zb, zb, zb, bias_rows)


def _merge_kernel(oa_ref, ob_ref, woa_ref, wob_ref, g0_ref, g1_ref, o_ref):
    a = jnp.dot(oa_ref[...], woa_ref[...], preferred_element_type=F32)
    b = jnp.dot(ob_ref[...], wob_ref[...], preferred_element_type=F32)
    o_ref[...] = (g0_ref[...].astype(F32) * a + g1_ref[...].astype(F32) * b).astype(o_ref.dtype)


def _merge(oa, ob, woa, wob, zb, gate_col, tm, tn):
    m = oa.shape[0]
    d = woa.shape[1]
    g0, g1 = gate_col // tn, (gate_col + d) // tn
    blocks = (_nbytes((tm, oa.shape[1]), BF16) + _nbytes((tm, ob.shape[1]), BF16) + _nbytes((woa.shape[0], tn), BF16)
              + _nbytes((wob.shape[0], tn), BF16) + 3 * _nbytes((tm, tn), BF16))
    return pl.pallas_call(
        _merge_kernel,
        grid=(m // tm, d // tn),
        in_specs=[pl.BlockSpec((tm, oa.shape[1]), lambda i, j: (i, 0)),
                  pl.BlockSpec((tm, ob.shape[1]), lambda i, j: (i, 0)),
                  pl.BlockSpec((woa.shape[0], tn), lambda i, j: (0, j)),
                  pl.BlockSpec((wob.shape[0], tn), lambda i, j: (0, j)),
                  pl.BlockSpec((tm, tn), lambda i, j: (i, g0 + j)),
                  pl.BlockSpec((tm, tn), lambda i, j: (i, g1 + j))],
        out_specs=pl.BlockSpec((tm, tn), lambda i, j: (i, j)),
        out_shape=jax.ShapeDtypeStruct((m, d), BF16),
        compiler_params=_params(("parallel", "arbitrary"), blocks, 3 * _nbytes((tm, tn), F32)),
        name="merge",
    )(oa, ob, woa, wob, zb, zb)


def _out_proj_kernel(a_ref, w_ref, x_ref, o_ref):
    o_ref[...] = x_ref[...] + jnp.dot(a_ref[...], w_ref[...], preferred_element_type=F32)


def _out_proj(merged, wout, x, tm, tn):
    m, k = merged.shape
    n = wout.shape[1]
    blocks = _nbytes((tm, k), BF16) + _nbytes((k, tn), BF16) + 2 * _nbytes((tm, tn), F32)
    return pl.pallas_call(
        _out_proj_kernel,
        grid=(m // tm, n // tn),
        in_specs=[pl.BlockSpec((tm, k), lambda i, j: (i, 0)),
                  pl.BlockSpec((k, tn), lambda i, j: (0, j)),
                  pl.BlockSpec((tm, tn), lambda i, j: (i, j))],
        out_specs=pl.BlockSpec((tm, tn), lambda i, j: (i, j)),
        out_shape=jax.ShapeDtypeStruct((m, n), F32),
        compiler_params=_params(("parallel", "arbitrary"), blocks, _nbytes((tm, tn), F32)),
        name="out_proj",
    )(merged, wout, x)


def _router_kernel(x_ref, g_ref, wr_ref, xn_ref, route_ref, *, n_groups, per_group):
    x = x_ref[...]
    r = lax.rsqrt(jnp.mean(x * x, axis=-1, keepdims=True) + EPS)
    xn = x * r * g_ref[...]
    xn_ref[...] = xn
    logits = jnp.dot(xn, wr_ref[...], preferred_element_type=F32, precision=lax.Precision.HIGHEST)
    lane = lax.broadcasted_iota(jnp.int32, logits.shape, 1).astype(F32)
    far = float(LANE)

    def top(vals):
        best = jnp.max(vals, axis=-1, keepdims=True)
        return best, jnp.min(jnp.where(vals == best, lane, far), axis=-1, keepdims=True)

    gl = jnp.where(lane < n_groups, logits, NEG_INF)
    gmax, grp = top(gl)
    p_grp = 1.0 / jnp.sum(jnp.exp(gl - gmax), axis=-1, keepdims=True)
    lo = n_groups + grp * per_group
    el = jnp.where((lane >= lo) & (lane < lo + per_group), logits, NEG_INF)
    t1, i1 = top(el)
    t2, i2 = top(jnp.where(lane == i1, NEG_INF, el))
    d = jnp.exp(t2 - t1)
    w1 = p_grp / (1.0 + d)
    w2 = p_grp * d / (1.0 + d)
    route_ref[...] = jnp.where(lane == 0, i1 - n_groups,
                               jnp.where(lane == 1, i2 - n_groups,
                                         jnp.where(lane == 2, w1, jnp.where(lane == 3, w2, 0.0))))


def _router(x1, g, wr, n_groups, per_group, tm):
    t, d = x1.shape
    blocks = 2 * _nbytes((tm, d), F32) + _nbytes((1, d), F32) + _nbytes((d, LANE), F32) + _nbytes((tm, LANE), F32)
    return pl.pallas_call(
        functools.partial(_router_kernel, n_groups=n_groups, per_group=per_group),
        grid=(t // tm,),
        in_specs=[pl.BlockSpec((tm, d), lambda i: (i, 0)), pl.BlockSpec((1, d), lambda i: (0, 0)),
                  pl.BlockSpec((d, LANE), lambda i: (0, 0))],
        out_specs=[pl.BlockSpec((tm, d), lambda i: (i, 0)), pl.BlockSpec((tm, LANE), lambda i: (i, 0))],
        out_shape=[jax.ShapeDtypeStruct((t, d), F32), jax.ShapeDtypeStruct((t, LANE), F32)],
        compiler_params=_params(("parallel",), blocks, 2 * _nbytes((tm, d), F32)),
        name="router",
    )(x1, g.reshape(1, d), wr)


def _one_hots(route):
    lane = lax.broadcasted_iota(jnp.int32, route.shape, 1).astype(F32)
    return (lane == route[:, 0:1]).astype(F32), (lane == route[:, 1:2]).astype(F32)


def _rank_kernel(route_ref, rank_ref, starts_ref, count_acc, start_acc):
    i = pl.program_id(0)

    @pl.when(i == 0)
    def _():
        count_acc[...] = jnp.zeros_like(count_acc)
        start_acc[...] = jnp.zeros_like(start_acc)

    oh1, oh2 = _one_hots(route_ref[...])
    oh = (oh1 + oh2).astype(BF16)
    tm = oh.shape[0]
    earlier = (lax.broadcasted_iota(jnp.int32, (tm, tm), 0) > lax.broadcasted_iota(jnp.int32, (tm, tm), 1))
    before = jnp.dot(earlier.astype(BF16), oh, preferred_element_type=F32) + count_acc[...]
    lane = lax.broadcasted_iota(jnp.int32, (tm, LANE), 1)
    rank_ref[...] = jnp.where(lane == 0, jnp.sum(oh1 * before, axis=-1, keepdims=True),
                              jnp.where(lane == 1, jnp.sum(oh2 * before, axis=-1, keepdims=True), 0.0))
    lower = (lax.broadcasted_iota(jnp.int32, (LANE, LANE), 0) < lax.broadcasted_iota(jnp.int32, (LANE, LANE), 1))
    below = jnp.dot(oh, lower.astype(BF16), preferred_element_type=F32)
    count_acc[...] += jnp.sum(oh.astype(F32), axis=0, keepdims=True)
    start_acc[...] += jnp.sum(below, axis=0, keepdims=True)
    starts_ref[...] = start_acc[...]


def _rank(route, tm):
    t = route.shape[0]
    blocks = 2 * _nbytes((tm, LANE), F32) + _nbytes((1, LANE), F32)
    return pl.pallas_call(
        _rank_kernel,
        grid=(t // tm,),
        in_specs=[pl.BlockSpec((tm, LANE), lambda i: (i, 0))],
        out_specs=[pl.BlockSpec((tm, LANE), lambda i: (i, 0)), pl.BlockSpec((1, LANE), lambda i: (0, 0))],
        out_shape=[jax.ShapeDtypeStruct((t, LANE), F32), jax.ShapeDtypeStruct((1, LANE), F32)],
        scratch_shapes=[pltpu.VMEM((1, LANE), F32), pltpu.VMEM((1, LANE), F32)],
        compiler_params=_params(("arbitrary",), blocks, 2 * _nbytes((tm, tm), F32)),
        name="moe_rank",
    )(route)


def _dest_kernel(route_ref, rank_ref, starts_ref, dest_ref):
    oh1, oh2 = _one_hots(route_ref[...])
    rank = rank_ref[...]
    starts = starts_ref[...]
    d1 = jnp.sum(oh1 * starts, axis=-1, keepdims=True) + rank[:, 0:1]
    d2 = jnp.sum(oh2 * starts, axis=-1, keepdims=True) + rank[:, 1:2]
    lane = lax.broadcasted_iota(jnp.int32, rank.shape, 1)
    dest_ref[...] = jnp.where(lane == 0, d1, jnp.where(lane == 1, d2, 0.0)).astype(jnp.int32)


def _dest(route, rank, starts, tm):
    t = route.shape[0]
    blocks = 3 * _nbytes((tm, LANE), F32) + _nbytes((1, LANE), F32)
    return pl.pallas_call(
        _dest_kernel,
        grid=(t // tm,),
        in_specs=[pl.BlockSpec((tm, LANE), lambda i: (i, 0)), pl.BlockSpec((tm, LANE), lambda i: (i, 0)),
                  pl.BlockSpec((1, LANE), lambda i: (0, 0))],
        out_specs=pl.BlockSpec((tm, LANE), lambda i: (i, 0)),
        out_shape=jax.ShapeDtypeStruct((t, LANE), jnp.int32),
        compiler_params=_params(("parallel",), blocks, 4 * _nbytes((tm, LANE), F32)),
        name="moe_dest",
    )(route, rank, starts)


def _gather_rows_kernel(dest_ref, x_hbm, xs_ref, source, rows_f32, sem, *, n_assign):
    rows = xs_ref.shape[0]
    base = pl.program_id(0) * rows

    @pl.when(pl.program_id(0) == 0)
    def _():
        def invert(a, carry):
            source[dest_ref[a]] = a
            return carry
        lax.fori_loop(0, n_assign, invert, 0, unroll=8)

    def copy(j):
        return pltpu.make_async_copy(x_hbm.at[pl.ds(source[base + j] // TOP_K, 1)], rows_f32.at[pl.ds(j, 1)], sem)

    def start(j, carry):
        copy(j).start()
        return carry

    def wait(j, carry):
        copy(j).wait()
        return carry

    lax.fori_loop(0, rows, start, 0, unroll=4)
    lax.fori_loop(0, rows, wait, 0, unroll=4)
    xs_ref[...] = rows_f32[...].astype(xs_ref.dtype)


def _gather_rows(dest_flat, xn, rows):
    t, d = xn.shape
    n_assign = dest_flat.shape[0]
    return pl.pallas_call(
        functools.partial(_gather_rows_kernel, n_assign=n_assign),
        grid_spec=pltpu.PrefetchScalarGridSpec(
            num_scalar_prefetch=1,
            grid=(n_assign // rows,),
            in_specs=[pl.BlockSpec(memory_space=pl.ANY)],
            out_specs=pl.BlockSpec((rows, d), lambda i, dest: (i, 0)),
            scratch_shapes=[pltpu.SMEM((n_assign,), jnp.int32), pltpu.VMEM((rows, d), xn.dtype),
                            pltpu.SemaphoreType.DMA(())]),
        out_shape=jax.ShapeDtypeStruct((n_assign, d), BF16),
        compiler_params=_params(("arbitrary",), _nbytes((rows, d), BF16), 2 * _nbytes((rows, d), xn.dtype)),
        name="moe_gather_rows",
    )(dest_flat, xn)


def _work_items(starts, n_rows):
    n_exp = starts.shape[0]
    n_blk = n_rows // MOE_ROWS
    total = jnp.full((1,), n_rows, jnp.int32)
    pts = jnp.concatenate([jnp.arange(n_blk, dtype=jnp.int32) * MOE_ROWS, starts[1:]])
    idx = jnp.arange(pts.shape[0], dtype=jnp.int32)
    before = (pts[None, :] < pts[:, None]) | ((pts[None, :] == pts[:, None]) & (idx[None, :] < idx[:, None]))
    pos = jnp.sum(before.astype(jnp.int32), axis=1)
    lo = jnp.sum(jnp.where(pos[:, None] == idx[None, :], pts[:, None], 0), axis=0)
    hi = jnp.concatenate([lo[1:], total])
    ends = jnp.concatenate([starts[1:], total])
    r = jnp.minimum(lo // MOE_ROWS, n_blk - 1)
    e = jnp.minimum(jnp.sum((ends[None, :] <= lo[:, None]).astype(jnp.int32), axis=1), n_exp - 1)
    changed = jnp.concatenate([jnp.ones((1,), jnp.int32), (e[1:] != e[:-1]).astype(jnp.int32)])
    slot = (jnp.sum(jnp.where(idx[None, :] <= idx[:, None], changed[None, :], 0), axis=1) - 1) % 2
    later = jnp.where(e[None, :] > e[:, None], e[None, :], n_exp)
    nxt = jnp.min(later, axis=1)
    nxt = jnp.where(nxt == n_exp, -1, nxt)
    return r, e, lo, hi, slot, nxt


def _stream_expert_weights(w, e_ref, slot_ref, nxt_ref, streams, convert):
    e = e_ref[w]
    slot = slot_ref[w]

    def copies(expert, into):
        return [pltpu.make_async_copy(hbm.at[expert], stage.at[into], sem.at[into]) for hbm, stage, sem in streams]

    @pl.when(w == 0)
    def _():
        for c in copies(e, slot):
            c.start()

    @pl.when((w == 0) | (e != e_ref[jnp.maximum(w - 1, 0)]))
    def _():
        for c in copies(e, slot):
            c.wait()
        nxt = nxt_ref[w]

        @pl.when(nxt >= 0)
        def _():
            for c in copies(nxt, 1 - slot):
                c.start()

        convert(slot)


def _store_item_rows(o_ref, val, r, lo, hi):
    rows = r * MOE_ROWS + lax.broadcasted_iota(jnp.int32, (MOE_ROWS, 1), 0)
    mine = (rows >= lo) & (rows < hi)

    @pl.when(lo == r * MOE_ROWS)
    def _():
        o_ref[...] = val

    @pl.when(lo != r * MOE_ROWS)
    def _():
        o_ref[...] = jnp.where(mine, val, o_ref[...])


def _moe_up_kernel(r_ref, e_ref, lo_ref, hi_ref, slot_ref, nxt_ref, xs_ref, wg_hbm, wu_hbm, h_ref,
                   wg_stage, wu_stage, wg_bf, wu_bf, sem_g, sem_u):
    w = pl.program_id(0)

    def convert(slot):
        wg_bf[...] = wg_stage[slot].astype(BF16)
        wu_bf[...] = wu_stage[slot].astype(BF16)

    _stream_expert_weights(w, e_ref, slot_ref, nxt_ref,
                           [(wg_hbm, wg_stage, sem_g), (wu_hbm, wu_stage, sem_u)], convert)
    r, lo, hi = r_ref[w], lo_ref[w], hi_ref[w]

    @pl.when(hi > lo)
    def _():
        x = xs_ref[...]
        g = jnp.dot(x, wg_bf[...], preferred_element_type=F32)
        u = jnp.dot(x, wu_bf[...], preferred_element_type=F32)
        h = (g * (1.0 / (1.0 + jnp.exp(-g)))) * u
        _store_item_rows(h_ref, h.astype(h_ref.dtype), r, lo, hi)


def _moe_up(items, xs, wg, wu):
    n_rows, d = xs.shape
    _, _, f = wg.shape
    n_items = items[0].shape[0]
    blocks = _nbytes((MOE_ROWS, d), BF16) + _nbytes((MOE_ROWS, f), BF16)
    resident = 4 * _nbytes((d, f), F32) + 2 * _nbytes((d, f), BF16) + 6 * _nbytes((MOE_ROWS, f), F32)
    row_block = lambda w, r, e, lo, hi, slot, nxt: (r[w], 0)
    return pl.pallas_call(
        _moe_up_kernel,
        grid_spec=pltpu.PrefetchScalarGridSpec(
            num_scalar_prefetch=6,
            grid=(n_items,),
            in_specs=[pl.BlockSpec((MOE_ROWS, d), row_block),
                      pl.BlockSpec(memory_space=pl.ANY), pl.BlockSpec(memory_space=pl.ANY)],
            out_specs=pl.BlockSpec((MOE_ROWS, f), row_block),
            scratch_shapes=[pltpu.VMEM((2, d, f), F32), pltpu.VMEM((2, d, f), F32),
                            pltpu.VMEM((d, f), BF16), pltpu.VMEM((d, f), BF16),
                            pltpu.SemaphoreType.DMA((2,)), pltpu.SemaphoreType.DMA((2,))]),
        out_shape=jax.ShapeDtypeStruct((n_rows, f), BF16),
        compiler_params=_params(("arbitrary",), blocks, resident),
        name="moe_up",
    )(*items, xs, wg, wu)


def _moe_down_kernel(r_ref, e_ref, lo_ref, hi_ref, slot_ref, nxt_ref, h_ref, wd_hbm, y_ref, wd_stage, wd_bf, sem):
    w = pl.program_id(0)

    def convert(slot):
        wd_bf[...] = wd_stage[slot].astype(BF16)

    _stream_expert_weights(w, e_ref, slot_ref, nxt_ref, [(wd_hbm, wd_stage, sem)], convert)
    r, lo, hi = r_ref[w], lo_ref[w], hi_ref[w]

    @pl.when(hi > lo)
    def _():
        y = jnp.dot(h_ref[...], wd_bf[...], preferred_element_type=F32)
        _store_item_rows(y_ref, y, r, lo, hi)


def _moe_down(items, h, wd):
    n_rows, f = h.shape
    d = wd.shape[2]
    n_items = items[0].shape[0]
    blocks = _nbytes((MOE_ROWS, f), BF16) + _nbytes((MOE_ROWS, d), F32)
    resident = 2 * _nbytes((f, d), F32) + _nbytes((f, d), BF16) + 3 * _nbytes((MOE_ROWS, d), F32)
    row_block = lambda w, r, e, lo, hi, slot, nxt: (r[w], 0)
    return pl.pallas_call(
        _moe_down_kernel,
        grid_spec=pltpu.PrefetchScalarGridSpec(
            num_scalar_prefetch=6,
            grid=(n_items,),
            in_specs=[pl.BlockSpec((MOE_ROWS, f), row_block), pl.BlockSpec(memory_space=pl.ANY)],
            out_specs=pl.BlockSpec((MOE_ROWS, d), row_block),
            scratch_shapes=[pltpu.VMEM((2, f, d), F32), pltpu.VMEM((f, d), BF16), pltpu.SemaphoreType.DMA((2,))]),
        out_shape=jax.ShapeDtypeStruct((n_rows, d), F32),
        compiler_params=_params(("arbitrary",), blocks, resident),
        name="moe_down",
    )(*items, h, wd)


def _combine_kernel(dest_ref, x_ref, route_ref, y_hbm, o_ref, ybuf, sem, *, tm):
    i = pl.program_id(0)

    def copies(tile, t):
        slot = tile % 2
        return [pltpu.make_async_copy(y_hbm.at[pl.ds(dest_ref[TOP_K * (tile * tm + t) + k], 1)],
                                      ybuf.at[slot, k, pl.ds(t, 1)], sem.at[slot]) for k in range(TOP_K)]

    def start_tile(tile):
        def start(t, carry):
            for c in copies(tile, t):
                c.start()
            return carry
        lax.fori_loop(0, tm, start, 0, unroll=2)

    @pl.when(i == 0)
    def _():
        start_tile(i)

    @pl.when(i + 1 < pl.num_programs(0))
    def _():
        start_tile(i + 1)

    def wait(t, carry):
        for c in copies(i, t):
            c.wait()
        return carry

    lax.fori_loop(0, tm, wait, 0, unroll=2)
    route = route_ref[...]
    rows = ybuf[i % 2]
    o_ref[...] = x_ref[...] + (route[:, 2:3] * rows[0] + route[:, 3:4] * rows[1])


def _combine(dest_flat, x1, route, y, tm):
    t, d = x1.shape
    blocks = 2 * _nbytes((tm, d), F32) + _nbytes((tm, LANE), F32)
    resident = 2 * TOP_K * _nbytes((tm, d), F32) + _nbytes((tm, d), F32)
    return pl.pallas_call(
        functools.partial(_combine_kernel, tm=tm),
        grid_spec=pltpu.PrefetchScalarGridSpec(
            num_scalar_prefetch=1,
            grid=(t // tm,),
            in_specs=[pl.BlockSpec((tm, d), lambda i, dest: (i, 0)),
                      pl.BlockSpec((tm, LANE), lambda i, dest: (i, 0)),
                      pl.BlockSpec(memory_space=pl.ANY)],
            out_specs=pl.BlockSpec((tm, d), lambda i, dest: (i, 0)),
            scratch_shapes=[pltpu.VMEM((2, TOP_K, tm, d), F32), pltpu.SemaphoreType.DMA((2,))]),
        out_shape=jax.ShapeDtypeStruct((t, d), F32),
        compiler_params=_params(("arbitrary",), blocks, resident),
        name="moe_combine",
    )(dest_flat, x1, route, y)


def _pad_cols(w, n):
    return jnp.pad(w, ((0, 0), (0, n - w.shape[1])))


def kernel(x, positions, g_mix, w_in, b_gate, q_norm_g, kv_norm_g, w_uq, w_ukv, a_q_norm_g, a_k_norm_g,
           b_q_norm_g, b_k_norm_g, rel_bias, w_o_a, w_o_b, w_out, g_ffn, w_group, w_expert,
           w_exp_gate, w_exp_up, w_exp_down):
    batch, seq, d = x.shape
    t = batch * seq
    q_lora, kv_lora = q_norm_g.shape[0], kv_norm_g.shape[0]
    a_heads = w_uq.shape[1] // A_QK
    b_heads = w_o_b.shape[0] // B_HEAD_DIM
    b_width = b_heads * B_HEAD_DIM
    n_groups, n_experts = w_group.shape[1], w_expert.shape[1]
    per_group = n_experts // n_groups
    d_expert = w_exp_gate.shape[2]
    off_b = q_lora + kv_lora + A_ROPE
    assert seq % ATTN_TILE == 0 and (TOP_K * t) % MOE_ROWS == 0 and n_groups + n_experts <= LANE

    xf = x.reshape(t, d)
    tm_big = min(1024, t)
    tn = _tile(b_width, 512)
    assert d % tn == 0

    za_cols = -(-(off_b + A_ROPE) // 512) * 512
    w_in_a = _pad_cols(w_in[:, :off_b], za_cols).astype(BF16)
    w_in_b = w_in[:, off_b:].astype(BF16)
    wuq = jnp.pad(w_uq.reshape(q_lora, a_heads, A_QK), ((0, 0), (0, 0), (0, A_HEAD_PAD - A_QK)))
    wuq = wuq.reshape(q_lora, a_heads * A_HEAD_PAD).astype(BF16)
    wukv = w_ukv.astype(BF16)
    pad_gain = lambda g, s: jnp.pad(g * s, (0, A_HEAD_PAD - A_QK)).reshape(1, A_HEAD_PAD)
    gaq = pad_gain(a_q_norm_g, A_QK ** -0.5)
    gak = pad_gain(a_k_norm_g, 1.0)
    gb = jnp.concatenate([jnp.tile(b_q_norm_g * B_HEAD_DIM ** -0.5, b_heads), jnp.tile(b_k_norm_g, b_heads),
                          jnp.ones((b_width,), F32), b_gate]).reshape(1, -1)

    half = A_ROPE // 2
    inv = ROPE_THETA ** (-jnp.arange(half, dtype=F32) / half)
    ang = positions.astype(F32).reshape(t, 1) * inv
    cos, sin = jnp.cos(ang), jnp.sin(ang)
    zeros = jnp.zeros((t, half), F32)
    cos_t = jnp.concatenate([cos, cos, zeros, zeros], axis=1)
    sin_lo = jnp.concatenate([-sin, zeros, zeros, zeros], axis=1)
    sin_hi = jnp.concatenate([zeros, sin, zeros, zeros], axis=1)

    xn = _rmsnorm_rows(xf, g_mix, BF16, min(256, t))
    za = _matmul(xn, w_in_a, F32, tm_big, _tile(za_cols, 512))
    zb = _inproj_b(xn, w_in_b, gb, tm_big, tn, 2 * b_width // tn, 3 * b_width // tn)
    q, k, v = _mla_proj(za, cos_t, sin_lo, sin_hi, wuq, wukv, q_norm_g.reshape(1, -1), kv_norm_g.reshape(1, -1),
                        gaq, gak, a_heads, min(256, t))
    o_a = _mla_attn(q, k, v, batch, seq, a_heads)
    o_b = _band_attn(zb, _band_bias_rows(rel_bias), batch, seq, b_heads)
    merged = _merge(o_a, o_b, w_o_a.astype(BF16), w_o_b.astype(BF16), zb, 3 * b_width, tm_big, tn)
    x1 = _out_proj(merged, w_out.astype(BF16), xf, tm_big, tn)

    wr = _pad_cols(jnp.concatenate([w_group, w_expert], axis=1), LANE)
    xn2, route = _router(x1, g_ffn, wr, n_groups, per_group, min(256, t))
    rank, starts_f = _rank(route, min(512, t))
    dest = _dest(route, rank, starts_f, min(512, t))[:, :TOP_K].reshape(-1)
    xs = _gather_rows(dest, xn2, GATHER_ROWS)
    items = _work_items(starts_f[0, :n_experts].astype(jnp.int32), TOP_K * t)
    h = _moe_up(items, xs, w_exp_gate, w_exp_up)
    y = _moe_down(items, h, w_exp_down)
    out = _combine(dest, x1, route, y, min(256, t))
    return out.reshape(batch, seq, d)
```

```python
import functools

import jax
import jax.numpy as jnp
from jax import lax
from jax.experimental import pallas as pl
from jax.experimental.pallas import tpu as pltpu

F32 = jnp.float32
BF16 = jnp.bfloat16

CHUNK = 64
EPS = 1e-6
A_NOPE = 128
A_ROPE = 64
A_V = 128
A_QK = A_NOPE + A_ROPE
B_HEAD_DIM = 128
B_LEFT_CHUNKS = 8
B_MAX_REL = 128
ROPE_THETA = 10000.0
TOP_K = 2

LANE = 128
A_HEAD_PAD = 2 * LANE
V7X_VMEM_BYTES = 64 * 2**20

ATTN_TILE = 256
MOE_ROWS = 128
GATHER_ROWS = 256
NEG_INF = float("-inf")


def _nbytes(shape, dtype):
    n = 1
    for s in shape:
        n *= s
    return n * jnp.dtype(dtype).itemsize


def _params(semantics, pipelined_bytes, resident_bytes=0):
    need = 2 * pipelined_bytes + resident_bytes
    return pltpu.CompilerParams(dimension_semantics=semantics,
                                vmem_limit_bytes=min(int(need), V7X_VMEM_BYTES))


def _tile(n, want):
    t = want
    while t > LANE and n % t:
        t //= 2
    assert n % t == 0, (n, want)
    return t


def _rmsnorm_kernel(x_ref, g_ref, o_ref):
    x = x_ref[...]
    r = lax.rsqrt(jnp.mean(x * x, axis=-1, keepdims=True) + EPS)
    o_ref[...] = (x * r * g_ref[...]).astype(o_ref.dtype)


def _rmsnorm_rows(x, g, out_dtype, tm):
    t, d = x.shape
    blocks = _nbytes((tm, d), F32) + _nbytes((tm, d), out_dtype) + _nbytes((1, d), F32)
    return pl.pallas_call(
        _rmsnorm_kernel,
        grid=(t // tm,),
        in_specs=[pl.BlockSpec((tm, d), lambda i: (i, 0)), pl.BlockSpec((1, d), lambda i: (0, 0))],
        out_specs=pl.BlockSpec((tm, d), lambda i: (i, 0)),
        out_shape=jax.ShapeDtypeStruct((t, d), out_dtype),
        compiler_params=_params(("parallel",), blocks, _nbytes((tm, d), F32)),
        name="rmsnorm",
    )(x, g.reshape(1, d))


def _inproj_a_kernel(a_ref, w_ref, o_ref, w_bf):
    @pl.when(pl.program_id(1) == 0)
    def _():
        w_bf[...] = w_ref[...].astype(BF16)

    o_ref[...] = jnp.dot(a_ref[...], w_bf[...], preferred_element_type=F32)


def _inproj_a(xn, w_in, n_cols, tm, tn):
    m, k = xn.shape
    blocks = _nbytes((tm, k), BF16) + _nbytes((k, tn), F32) + _nbytes((tm, tn), F32)
    return pl.pallas_call(
        _inproj_a_kernel,
        grid=(n_cols // tn, m // tm),
        in_specs=[pl.BlockSpec((tm, k), lambda j, i: (i, 0)), pl.BlockSpec((k, tn), lambda j, i: (0, j))],
        out_specs=pl.BlockSpec((tm, tn), lambda j, i: (i, j)),
        out_shape=jax.ShapeDtypeStruct((m, n_cols), F32),
        scratch_shapes=[pltpu.VMEM((k, tn), BF16)],
        compiler_params=_params(("parallel", "arbitrary"), blocks, 2 * _nbytes((k, tn), BF16)),
        name="inproj_a",
    )(xn, w_in)


def _inproj_b_kernel(a_ref, wa_ref, wb_ref, gb_ref, o_ref, w_bf, *, shift, n_norm_blocks, n_plain_end):
    j = pl.program_id(0)
    k, tn = w_bf.shape
    rows = min(k, 512)

    @pl.when(pl.program_id(1) == 0)
    def _():
        for c in range(k // rows):
            sl = slice(c * rows, (c + 1) * rows)
            window = jnp.concatenate([wa_ref[sl, :], wb_ref[sl, :]], axis=1)
            w_bf[sl, :] = pltpu.roll(window, window.shape[1] - shift, 1)[:, :tn].astype(BF16)

    acc = jnp.dot(a_ref[...], w_bf[...], preferred_element_type=F32)

    @pl.when(j < n_norm_blocks)
    def _():
        for h in range(acc.shape[1] // B_HEAD_DIM):
            sl = slice(h * B_HEAD_DIM, (h + 1) * B_HEAD_DIM)
            z = acc[:, sl]
            r = lax.rsqrt(jnp.mean(z * z, axis=-1, keepdims=True) + EPS)
            o_ref[:, sl] = (z * r * gb_ref[:, sl]).astype(o_ref.dtype)

    @pl.when((j >= n_norm_blocks) & (j < n_plain_end))
    def _():
        o_ref[...] = acc.astype(o_ref.dtype)

    @pl.when(j >= n_plain_end)
    def _():
        o_ref[...] = (1.0 / (1.0 + jnp.exp(-(acc + gb_ref[...])))).astype(o_ref.dtype)


def _inproj_b(xn, w_in, first_col, gb, tm, tn, n_norm_blocks, n_plain_end):
    m, k = xn.shape
    n = w_in.shape[1] - first_col
    base, shift = first_col // LANE * LANE, first_col % LANE
    assert base % tn == 0 and n % tn == 0 and shift > 0
    a_blk, b_blk = base // tn, (base + tn) // LANE
    blocks = (_nbytes((tm, k), BF16) + _nbytes((k, tn + LANE), F32) + _nbytes((tm, tn), BF16)
              + _nbytes((1, tn), F32))
    resident = _nbytes((k, tn), BF16) + 2 * _nbytes((tm, tn), F32) + 4 * _nbytes((min(k, 512), tn + LANE), F32)
    return pl.pallas_call(
        functools.partial(_inproj_b_kernel, shift=shift, n_norm_blocks=n_norm_blocks, n_plain_end=n_plain_end),
        grid=(n // tn, m // tm),
        in_specs=[pl.BlockSpec((tm, k), lambda j, i: (i, 0)),
                  pl.BlockSpec((k, tn), lambda j, i: (0, a_blk + j)),
                  pl.BlockSpec((k, LANE), lambda j, i: (0, b_blk + j * (tn // LANE))),
                  pl.BlockSpec((1, tn), lambda j, i: (0, j))],
        out_specs=pl.BlockSpec((tm, tn), lambda j, i: (i, j)),
        out_shape=jax.ShapeDtypeStruct((m, n), BF16),
        scratch_shapes=[pltpu.VMEM((k, tn), BF16)],
        compiler_params=_params(("parallel", "arbitrary"), blocks, resident),
        name="inproj_b",
    )(xn, w_in, w_in, gb)


def _rope_padded(v, cos, sin_lo, sin_hi):
    half = A_ROPE // 2
    return v * cos + pltpu.roll(v, LANE - half, 1) * sin_lo + pltpu.roll(v, half, 1) * sin_hi


def _mla_proj_kernel(za_ref, cos_ref, sl_ref, sh_ref, wuq_ref, wukv_ref, gq_ref, gkv_ref, gaq_ref, gak_ref,
                     q_ref, k_ref, v_ref, *, heads, q_lora, kv_lora):
    cos, sin_lo, sin_hi = cos_ref[...], sl_ref[...], sh_ref[...]

    def norm(z, g):
        r = lax.rsqrt(jnp.mean(z * z, axis=-1, keepdims=True) + EPS)
        return (z * r * g).astype(BF16)

    cq = norm(za_ref[:, :q_lora], gq_ref[...])
    ckv = norm(za_ref[:, q_lora:q_lora + kv_lora], gkv_ref[...])
    slab = za_ref[:, q_lora + kv_lora:q_lora + kv_lora + LANE]
    k_rope = jnp.where(lax.broadcasted_iota(jnp.int32, slab.shape, 1) < A_ROPE, slab, 0.0)
    qacc = jnp.dot(cq, wuq_ref[...], preferred_element_type=F32)
    kvacc = jnp.dot(ckv, wukv_ref[...], preferred_element_type=F32)

    gq_lo, gq_hi = gaq_ref[:, :LANE], gaq_ref[:, LANE:]
    gk_lo, gk_hi = gak_ref[:, :LANE], gak_ref[:, LANE:]
    kr_ss = jnp.sum(k_rope * k_rope, axis=-1, keepdims=True)
    kr_rot = _rope_padded(k_rope * gk_hi, cos, sin_lo, sin_hi)
    for h in range(heads):
        base = h * A_HEAD_PAD
        q_lo = qacc[:, base:base + LANE]
        q_hi = qacc[:, base + LANE:base + A_HEAD_PAD]
        ss = jnp.sum(q_lo * q_lo, axis=-1, keepdims=True) + jnp.sum(q_hi * q_hi, axis=-1, keepdims=True)
        r = lax.rsqrt(ss / A_QK + EPS)
        q_ref[:, base:base + LANE] = (q_lo * r * gq_lo).astype(BF16)
        q_ref[:, base + LANE:base + A_HEAD_PAD] = _rope_padded(q_hi * r * gq_hi, cos, sin_lo, sin_hi).astype(BF16)

        k_lo = kvacc[:, base:base + LANE]
        ssk = jnp.sum(k_lo * k_lo, axis=-1, keepdims=True) + kr_ss
        rk = lax.rsqrt(ssk / A_QK + EPS)
        k_ref[:, base:base + LANE] = (k_lo * rk * gk_lo).astype(BF16)
        k_ref[:, base + LANE:base + A_HEAD_PAD] = (kr_rot * rk).astype(BF16)
        v_ref[:, h * A_V:(h + 1) * A_V] = kvacc[:, base + LANE:base + A_HEAD_PAD].astype(BF16)


def _mla_proj(za, cos, sin_lo, sin_hi, wuq, wukv, gq, gkv, gaq, gak, heads, tm):
    t, za_cols = za.shape
    q_lora, kv_lora = wuq.shape[0], wukv.shape[0]
    hp = heads * A_HEAD_PAD
    row = lambda i: (i, 0)
    fix = lambda i: (0, 0)
    blocks = (_nbytes((tm, za_cols), F32) + 3 * _nbytes((tm, LANE), F32) + _nbytes(wuq.shape, BF16)
              + _nbytes(wukv.shape, BF16) + 2 * _nbytes((tm, hp), BF16) + _nbytes((tm, heads * A_V), BF16))
    return pl.pallas_call(
        functools.partial(_mla_proj_kernel, heads=heads, q_lora=q_lora, kv_lora=kv_lora),
        grid=(t // tm,),
        in_specs=[pl.BlockSpec((tm, za_cols), row),
                  pl.BlockSpec((tm, LANE), row), pl.BlockSpec((tm, LANE), row), pl.BlockSpec((tm, LANE), row),
                  pl.BlockSpec(wuq.shape, fix), pl.BlockSpec(wukv.shape, fix),
                  pl.BlockSpec((1, q_lora), fix), pl.BlockSpec((1, kv_lora), fix),
                  pl.BlockSpec((1, A_HEAD_PAD), fix), pl.BlockSpec((1, A_HEAD_PAD), fix)],
        out_specs=[pl.BlockSpec((tm, hp), row), pl.BlockSpec((tm, hp), row), pl.BlockSpec((tm, heads * A_V), row)],
        out_shape=[jax.ShapeDtypeStruct((t, hp), BF16), jax.ShapeDtypeStruct((t, hp), BF16),
                   jax.ShapeDtypeStruct((t, heads * A_V), BF16)],
        compiler_params=_params(("parallel",), blocks, 3 * _nbytes((tm, hp), F32)),
        name="mla_proj",
    )(za, cos, sin_lo, sin_hi, wuq, wukv, gq, gkv, gaq, gak)


def _nt_dot(a, b):
    return lax.dot_general(a, b, (((1,), (1,)), ((), ())), preferred_element_type=F32)


def _mla_attn_kernel(q_ref, k_ref, v_ref, o_ref, *, seq):
    tq = ATTN_TILE
    rc = lax.broadcasted_iota(jnp.int32, (tq, tq), 0) // CHUNK
    cc = lax.broadcasted_iota(jnp.int32, (tq, tq), 1) // CHUNK
    diag_ok = cc <= rc
    for i in range(seq // tq):
        q = q_ref[i * tq:(i + 1) * tq, :]
        sd = jnp.where(diag_ok, _nt_dot(q, k_ref[i * tq:(i + 1) * tq, :]), NEG_INF)
        m = jnp.max(sd, axis=-1, keepdims=True)
        if i > 0:
            s0 = _nt_dot(q, k_ref[:i * tq, :])
            m = jnp.maximum(m, jnp.max(s0, axis=-1, keepdims=True))
        pd = jnp.exp(sd - m)
        l = jnp.sum(pd, axis=-1, keepdims=True)
        o = jnp.dot(pd.astype(BF16), v_ref[i * tq:(i + 1) * tq, :], preferred_element_type=F32)
        if i > 0:
            p0 = jnp.exp(s0 - m)
            l = l + jnp.sum(p0, axis=-1, keepdims=True)
            o = o + jnp.dot(p0.astype(BF16), v_ref[:i * tq, :], preferred_element_type=F32)
        o_ref[i * tq:(i + 1) * tq, :] = (o / l).astype(o_ref.dtype)


def _mla_attn(q, k, v, batch, seq, heads):
    t = batch * seq
    blocks = 2 * _nbytes((seq, A_HEAD_PAD), BF16) + 2 * _nbytes((seq, A_V), BF16)
    temps = 6 * _nbytes((ATTN_TILE, seq), F32)
    return pl.pallas_call(
        functools.partial(_mla_attn_kernel, seq=seq),
        grid=(batch, heads),
        in_specs=[pl.BlockSpec((seq, A_HEAD_PAD), lambda b, h: (b, h)),
                  pl.BlockSpec((seq, A_HEAD_PAD), lambda b, h: (b, h)),
                  pl.BlockSpec((seq, A_V), lambda b, h: (b, h))],
        out_specs=pl.BlockSpec((seq, A_V), lambda b, h: (b, h)),
        out_shape=jax.ShapeDtypeStruct((t, heads * A_V), BF16),
        compiler_params=_params(("parallel", "parallel"), blocks, temps),
        name="mla_attn",
    )(q, k, v)


def _band_attn_kernel(q_ref, k_ref, v_ref, r_ref, o_ref, *, seq):
    tq = ATTN_TILE
    left = B_LEFT_CHUNKS * CHUNK
    width = r_ref.shape[1]
    bias = pltpu.roll(jnp.broadcast_to(r_ref[...], (tq, width)), 0, 1, stride=1, stride_axis=0)[:, :left + tq]
    q_chunk = lax.broadcasted_iota(jnp.int32, (tq, left + tq), 0) // CHUNK
    k_chunk = lax.broadcasted_iota(jnp.int32, (tq, left + tq), 1) // CHUNK
    table = jnp.where((k_chunk >= q_chunk) & (k_chunk <= q_chunk + B_LEFT_CHUNKS), bias, NEG_INF)
    for i in range(seq // tq):
        q0 = i * tq
        k0 = max(0, q0 - left)
        kw = q0 + tq - k0
        u0 = left - q0 + k0
        q = q_ref[q0:q0 + tq, :]
        s = _nt_dot(q, k_ref[k0:k0 + kw, :]) + table[:, u0:u0 + kw]
        m = jnp.max(s, axis=-1, keepdims=True)
        p = jnp.exp(s - m)
        l = jnp.sum(p, axis=-1, keepdims=True)
        o = jnp.dot(p.astype(BF16), v_ref[k0:k0 + kw, :], preferred_element_type=F32)
        o_ref[q0:q0 + tq, :] = (o / l).astype(o_ref.dtype)


def _band_bias_rows(rel_bias):
    left = B_LEFT_CHUNKS * CHUNK
    width = left + 2 * ATTN_TILE
    m = jnp.arange(width, dtype=jnp.int32)
    j = jnp.where(m < left + ATTN_TILE, m, m - width)
    dist = left - j
    rows = rel_bias[:, jnp.clip(dist, -B_MAX_REL, B_MAX_REL) + B_MAX_REL].astype(F32)
    return rows.reshape(rel_bias.shape[0], 1, width)


def _band_attn(zb, bias_rows, batch, seq, heads):
    t = batch * seq
    d = B_HEAD_DIM
    width = bias_rows.shape[2]
    blocks = 4 * _nbytes((seq, d), BF16) + _nbytes((1, width), F32)
    temps = 8 * _nbytes((ATTN_TILE, width), F32)
    return pl.pallas_call(
        functools.partial(_band_attn_kernel, seq=seq),
        grid=(batch, heads),
        in_specs=[pl.BlockSpec((seq, d), lambda b, h: (b, h)),
                  pl.BlockSpec((seq, d), lambda b, h: (b, heads + h)),
                  pl.BlockSpec((seq, d), lambda b, h: (b, 2 * heads + h)),
                  pl.BlockSpec((None, 1, width), lambda b, h: (h, 0, 0))],
        out_specs=pl.BlockSpec((seq, d), lambda b, h: (b, h)),
        out_shape=jax.ShapeDtypeStruct((t, heads * d), BF16),
        compiler_params=_params(("parallel", "parallel"), blocks, temps),
        name="band_attn",
    )(zb, zb, zb, bias_rows)


def _merge_kernel(oa_ref, ob_ref, woa_ref, wob_ref, g0_ref, g1_ref, o_ref, woa_bf, wob_bf):
    @pl.when(pl.program_id(1) == 0)
    def _():
        woa_bf[...] = woa_ref[...].astype(BF16)
        wob_bf[...] = wob_ref[...].astype(BF16)

    a = jnp.dot(oa_ref[...], woa_bf[...], preferred_element_type=F32)
    b = jnp.dot(ob_ref[...], wob_bf[...], preferred_element_type=F32)
    o_ref[...] = (g0_ref[...].astype(F32) * a + g1_ref[...].astype(F32) * b).astype(o_ref.dtype)


def _merge(oa, ob, woa, wob, zb, gate_col, tm, tn):
    m = oa.shape[0]
    d = woa.shape[1]
    g0, g1 = gate_col // tn, (gate_col + d) // tn
    blocks = (_nbytes((tm, oa.shape[1]), BF16) + _nbytes((tm, ob.shape[1]), BF16) + _nbytes((woa.shape[0], tn), F32)
              + _nbytes((wob.shape[0], tn), F32) + 3 * _nbytes((tm, tn), BF16))
    resident = (2 * _nbytes((woa.shape[0], tn), BF16) + 2 * _nbytes((wob.shape[0], tn), BF16)
                + 3 * _nbytes((tm, tn), F32))
    return pl.pallas_call(
        _merge_kernel,
        grid=(d // tn, m // tm),
        in_specs=[pl.BlockSpec((tm, oa.shape[1]), lambda j, i: (i, 0)),
                  pl.BlockSpec((tm, ob.shape[1]), lambda j, i: (i, 0)),
                  pl.BlockSpec((woa.shape[0], tn), lambda j, i: (0, j)),
                  pl.BlockSpec((wob.shape[0], tn), lambda j, i: (0, j)),
                  pl.BlockSpec((tm, tn), lambda j, i: (i, g0 + j)),
                  pl.BlockSpec((tm, tn), lambda j, i: (i, g1 + j))],
        out_specs=pl.BlockSpec((tm, tn), lambda j, i: (i, j)),
        out_shape=jax.ShapeDtypeStruct((m, d), BF16),
        scratch_shapes=[pltpu.VMEM((woa.shape[0], tn), BF16), pltpu.VMEM((wob.shape[0], tn), BF16)],
        compiler_params=_params(("parallel", "arbitrary"), blocks, resident),
        name="merge",
    )(oa, ob, woa, wob, zb, zb)


def _out_proj_kernel(a_ref, w_ref, x_ref, o_ref, w_bf):
    @pl.when(pl.program_id(1) == 0)
    def _():
        w_bf[...] = w_ref[...].astype(BF16)

    o_ref[...] = x_ref[...] + jnp.dot(a_ref[...], w_bf[...], preferred_element_type=F32)


def _out_proj(merged, wout, x, tm, tn):
    m, k = merged.shape
    n = wout.shape[1]
    blocks = _nbytes((tm, k), BF16) + _nbytes((k, tn), F32) + 2 * _nbytes((tm, tn), F32)
    return pl.pallas_call(
        _out_proj_kernel,
        grid=(n // tn, m // tm),
        in_specs=[pl.BlockSpec((tm, k), lambda j, i: (i, 0)),
                  pl.BlockSpec((k, tn), lambda j, i: (0, j)),
                  pl.BlockSpec((tm, tn), lambda j, i: (i, j))],
        out_specs=pl.BlockSpec((tm, tn), lambda j, i: (i, j)),
        out_shape=jax.ShapeDtypeStruct((m, n), F32),
        scratch_shapes=[pltpu.VMEM((k, tn), BF16)],
        compiler_params=_params(("parallel", "arbitrary"), blocks, 2 * _nbytes((k, tn), BF16) + _nbytes((tm, tn), F32)),
        name="out_proj",
    )(merged, wout, x)


def _router_kernel(x_ref, g_ref, wr_ref, xn_ref, route_ref, *, n_groups, per_group):
    x = x_ref[...]
    r = lax.rsqrt(jnp.mean(x * x, axis=-1, keepdims=True) + EPS)
    xn = x * r * g_ref[...]
    xn_ref[...] = xn
    logits = jnp.dot(xn, wr_ref[...], preferred_element_type=F32, precision=lax.Precision.HIGHEST)
    lane = lax.broadcasted_iota(jnp.int32, logits.shape, 1).astype(F32)
    far = float(LANE)

    def top(vals):
        best = jnp.max(vals, axis=-1, keepdims=True)
        return best, jnp.min(jnp.where(vals == best, lane, far), axis=-1, keepdims=True)

    gl = jnp.where(lane < n_groups, logits, NEG_INF)
    gmax, grp = top(gl)
    p_grp = 1.0 / jnp.sum(jnp.exp(gl - gmax), axis=-1, keepdims=True)
    lo = n_groups + grp * per_group
    el = jnp.where((lane >= lo) & (lane < lo + per_group), logits, NEG_INF)
    t1, i1 = top(el)
    t2, i2 = top(jnp.where(lane == i1, NEG_INF, el))
    d = jnp.exp(t2 - t1)
    w1 = p_grp / (1.0 + d)
    w2 = p_grp * d / (1.0 + d)
    route_ref[...] = jnp.where(lane == 0, i1 - n_groups,
                               jnp.where(lane == 1, i2 - n_groups,
                                         jnp.where(lane == 2, w1, jnp.where(lane == 3, w2, 0.0))))


def _router(x1, g, wr, n_groups, per_group, tm):
    t, d = x1.shape
    blocks = 2 * _nbytes((tm, d), F32) + _nbytes((1, d), F32) + _nbytes((d, LANE), F32) + _nbytes((tm, LANE), F32)
    return pl.pallas_call(
        functools.partial(_router_kernel, n_groups=n_groups, per_group=per_group),
        grid=(t // tm,),
        in_specs=[pl.BlockSpec((tm, d), lambda i: (i, 0)), pl.BlockSpec((1, d), lambda i: (0, 0)),
                  pl.BlockSpec((d, LANE), lambda i: (0, 0))],
        out_specs=[pl.BlockSpec((tm, d), lambda i: (i, 0)), pl.BlockSpec((tm, LANE), lambda i: (i, 0))],
        out_shape=[jax.ShapeDtypeStruct((t, d), F32), jax.ShapeDtypeStruct((t, LANE), F32)],
        compiler_params=_params(("parallel",), blocks, 2 * _nbytes((tm, d), F32)),
        name="router",
    )(x1, g.reshape(1, d), wr)


def _one_hots(route):
    lane = lax.broadcasted_iota(jnp.int32, route.shape, 1).astype(F32)
    return (lane == route[:, 0:1]).astype(F32), (lane == route[:, 1:2]).astype(F32)


def _rank_kernel(route_ref, rank_ref, starts_ref, count_acc, start_acc):
    i = pl.program_id(0)

    @pl.when(i == 0)
    def _():
        count_acc[...] = jnp.zeros_like(count_acc)
        start_acc[...] = jnp.zeros_like(start_acc)

    oh1, oh2 = _one_hots(route_ref[...])
    oh = (oh1 + oh2).astype(BF16)
    tm = oh.shape[0]
    earlier = (lax.broadcasted_iota(jnp.int32, (tm, tm), 0) > lax.broadcasted_iota(jnp.int32, (tm, tm), 1))
    before = jnp.dot(earlier.astype(BF16), oh, preferred_element_type=F32) + count_acc[...]
    lane = lax.broadcasted_iota(jnp.int32, (tm, LANE), 1)
    rank_ref[...] = jnp.where(lane == 0, jnp.sum(oh1 * before, axis=-1, keepdims=True),
                              jnp.where(lane == 1, jnp.sum(oh2 * before, axis=-1, keepdims=True), 0.0))
    lower = (lax.broadcasted_iota(jnp.int32, (LANE, LANE), 0) < lax.broadcasted_iota(jnp.int32, (LANE, LANE), 1))
    below = jnp.dot(oh, lower.astype(BF16), preferred_element_type=F32)
    count_acc[...] += jnp.sum(oh.astype(F32), axis=0, keepdims=True)
    start_acc[...] += jnp.sum(below, axis=0, keepdims=True)
    starts_ref[...] = start_acc[...]


def _rank(route, tm):
    t = route.shape[0]
    blocks = 2 * _nbytes((tm, LANE), F32) + _nbytes((1, LANE), F32)
    return pl.pallas_call(
        _rank_kernel,
        grid=(t // tm,),
        in_specs=[pl.BlockSpec((tm, LANE), lambda i: (i, 0))],
        out_specs=[pl.BlockSpec((tm, LANE), lambda i: (i, 0)), pl.BlockSpec((1, LANE), lambda i: (0, 0))],
        out_shape=[jax.ShapeDtypeStruct((t, LANE), F32), jax.ShapeDtypeStruct((1, LANE), F32)],
        scratch_shapes=[pltpu.VMEM((1, LANE), F32), pltpu.VMEM((1, LANE), F32)],
        compiler_params=_params(("arbitrary",), blocks, 2 * _nbytes((tm, tm), F32)),
        name="moe_rank",
    )(route)


def _dest_kernel(route_ref, rank_ref, starts_ref, dest_ref):
    oh1, oh2 = _one_hots(route_ref[...])
    rank = rank_ref[...]
    starts = starts_ref[...]
    d1 = jnp.sum(oh1 * starts, axis=-1, keepdims=True) + rank[:, 0:1]
    d2 = jnp.sum(oh2 * starts, axis=-1, keepdims=True) + rank[:, 1:2]
    lane = lax.broadcasted_iota(jnp.int32, rank.shape, 1)
    dest_ref[...] = jnp.where(lane == 0, d1, jnp.where(lane == 1, d2, 0.0)).astype(jnp.int32)


def _dest(route, rank, starts, tm):
    t = route.shape[0]
    blocks = 3 * _nbytes((tm, LANE), F32) + _nbytes((1, LANE), F32)
    return pl.pallas_call(
        _dest_kernel,
        grid=(t // tm,),
        in_specs=[pl.BlockSpec((tm, LANE), lambda i: (i, 0)), pl.BlockSpec((tm, LANE), lambda i: (i, 0)),
                  pl.BlockSpec((1, LANE), lambda i: (0, 0))],
        out_specs=pl.BlockSpec((tm, LANE), lambda i: (i, 0)),
        out_shape=jax.ShapeDtypeStruct((t, LANE), jnp.int32),
        compiler_params=_params(("parallel",), blocks, 4 * _nbytes((tm, LANE), F32)),
        name="moe_dest",
    )(route, rank, starts)


def _gather_rows_kernel(dest_ref, x_hbm, xs_ref, source, rows_f32, sem, *, n_assign):
    rows = xs_ref.shape[0]
    base = pl.program_id(0) * rows

    @pl.when(pl.program_id(0) == 0)
    def _():
        def invert(a, carry):
            source[dest_ref[a]] = a
            return carry
        lax.fori_loop(0, n_assign, invert, 0, unroll=8)

    def copy(j):
        return pltpu.make_async_copy(x_hbm.at[pl.ds(source[base + j] // TOP_K, 1)], rows_f32.at[pl.ds(j, 1)], sem)

    def start(j, carry):
        copy(j).start()
        return carry

    def wait(j, carry):
        copy(j).wait()
        return carry

    lax.fori_loop(0, rows, start, 0, unroll=4)
    lax.fori_loop(0, rows, wait, 0, unroll=4)
    xs_ref[...] = rows_f32[...].astype(xs_ref.dtype)


def _gather_rows(dest_flat, xn, rows):
    t, d = xn.shape
    n_assign = dest_flat.shape[0]
    return pl.pallas_call(
        functools.partial(_gather_rows_kernel, n_assign=n_assign),
        grid_spec=pltpu.PrefetchScalarGridSpec(
            num_scalar_prefetch=1,
            grid=(n_assign // rows,),
            in_specs=[pl.BlockSpec(memory_space=pl.ANY)],
            out_specs=pl.BlockSpec((rows, d), lambda i, dest: (i, 0)),
            scratch_shapes=[pltpu.SMEM((n_assign,), jnp.int32), pltpu.VMEM((rows, d), xn.dtype),
                            pltpu.SemaphoreType.DMA(())]),
        out_shape=jax.ShapeDtypeStruct((n_assign, d), BF16),
        compiler_params=_params(("arbitrary",), _nbytes((rows, d), BF16), 2 * _nbytes((rows, d), xn.dtype)),
        name="moe_gather_rows",
    )(dest_flat, xn)


def _work_items(starts, n_rows):
    n_exp = starts.shape[0]
    n_blk = n_rows // MOE_ROWS
    total = jnp.full((1,), n_rows, jnp.int32)
    pts = jnp.concatenate([jnp.arange(n_blk, dtype=jnp.int32) * MOE_ROWS, starts[1:]])
    idx = jnp.arange(pts.shape[0], dtype=jnp.int32)
    before = (pts[None, :] < pts[:, None]) | ((pts[None, :] == pts[:, None]) & (idx[None, :] < idx[:, None]))
    pos = jnp.sum(before.astype(jnp.int32), axis=1)
    lo = jnp.sum(jnp.where(pos[:, None] == idx[None, :], pts[:, None], 0), axis=0)
    hi = jnp.concatenate([lo[1:], total])
    ends = jnp.concatenate([starts[1:], total])
    r = jnp.minimum(lo // MOE_ROWS, n_blk - 1)
    e = jnp.minimum(jnp.sum((ends[None, :] <= lo[:, None]).astype(jnp.int32), axis=1), n_exp - 1)
    changed = jnp.concatenate([jnp.ones((1,), jnp.int32), (e[1:] != e[:-1]).astype(jnp.int32)])
    slot = (jnp.sum(jnp.where(idx[None, :] <= idx[:, None], changed[None, :], 0), axis=1) - 1) % 2
    later = jnp.where(e[None, :] > e[:, None], e[None, :], n_exp)
    nxt = jnp.min(later, axis=1)
    nxt = jnp.where(nxt == n_exp, -1, nxt)
    return r, e, lo, hi, slot, nxt


def _stream_expert_weights(w, e_ref, slot_ref, nxt_ref, streams, convert):
    e = e_ref[w]
    slot = slot_ref[w]

    def copies(expert, into):
        return [pltpu.make_async_copy(hbm.at[expert], stage.at[into], sem.at[into]) for hbm, stage, sem in streams]

    @pl.when(w == 0)
    def _():
        for c in copies(e, slot):
            c.start()

    @pl.when((w == 0) | (e != e_ref[jnp.maximum(w - 1, 0)]))
    def _():
        for c in copies(e, slot):
            c.wait()
        nxt = nxt_ref[w]

        @pl.when(nxt >= 0)
        def _():
            for c in copies(nxt, 1 - slot):
                c.start()

        convert(slot)


def _store_item_rows(o_ref, val, r, lo, hi):
    rows = r * MOE_ROWS + lax.broadcasted_iota(jnp.int32, (MOE_ROWS, 1), 0)
    mine = (rows >= lo) & (rows < hi)

    @pl.when(lo == r * MOE_ROWS)
    def _():
        o_ref[...] = val

    @pl.when(lo != r * MOE_ROWS)
    def _():
        o_ref[...] = jnp.where(mine, val, o_ref[...])


def _moe_up_kernel(r_ref, e_ref, lo_ref, hi_ref, slot_ref, nxt_ref, xs_ref, wg_hbm, wu_hbm, h_ref,
                   wg_stage, wu_stage, wg_bf, wu_bf, sem_g, sem_u):
    w = pl.program_id(0)

    def convert(slot):
        wg_bf[...] = wg_stage[slot].astype(BF16)
        wu_bf[...] = wu_stage[slot].astype(BF16)

    _stream_expert_weights(w, e_ref, slot_ref, nxt_ref,
                           [(wg_hbm, wg_stage, sem_g), (wu_hbm, wu_stage, sem_u)], convert)
    r, lo, hi = r_ref[w], lo_ref[w], hi_ref[w]

    @pl.when(hi > lo)
    def _():
        x = xs_ref[...]
        g = jnp.dot(x, wg_bf[...], preferred_element_type=F32)
        u = jnp.dot(x, wu_bf[...], preferred_element_type=F32)
        h = (g * (1.0 / (1.0 + jnp.exp(-g)))) * u
        _store_item_rows(h_ref, h.astype(h_ref.dtype), r, lo, hi)


def _moe_up(items, xs, wg, wu):
    n_rows, d = xs.shape
    _, _, f = wg.shape
    n_items = items[0].shape[0]
    blocks = _nbytes((MOE_ROWS, d), BF16) + _nbytes((MOE_ROWS, f), BF16)
    resident = 4 * _nbytes((d, f), F32) + 2 * _nbytes((d, f), BF16) + 6 * _nbytes((MOE_ROWS, f), F32)
    row_block = lambda w, r, e, lo, hi, slot, nxt: (r[w], 0)
    return pl.pallas_call(
        _moe_up_kernel,
        grid_spec=pltpu.PrefetchScalarGridSpec(
            num_scalar_prefetch=6,
            grid=(n_items,),
            in_specs=[pl.BlockSpec((MOE_ROWS, d), row_block),
                      pl.BlockSpec(memory_space=pl.ANY), pl.BlockSpec(memory_space=pl.ANY)],
            out_specs=pl.BlockSpec((MOE_ROWS, f), row_block),
            scratch_shapes=[pltpu.VMEM((2, d, f), F32), pltpu.VMEM((2, d, f), F32),
                            pltpu.VMEM((d, f), BF16), pltpu.VMEM((d, f), BF16),
                            pltpu.SemaphoreType.DMA((2,)), pltpu.SemaphoreType.DMA((2,))]),
        out_shape=jax.ShapeDtypeStruct((n_rows, f), BF16),
        compiler_params=_params(("arbitrary",), blocks, resident),
        name="moe_up",
    )(*items, xs, wg, wu)


def _moe_down_kernel(r_ref, e_ref, lo_ref, hi_ref, slot_ref, nxt_ref, h_ref, wd_hbm, y_ref, wd_stage, wd_bf, sem):
    w = pl.program_id(0)

    def convert(slot):
        wd_bf[...] = wd_stage[slot].astype(BF16)

    _stream_expert_weights(w, e_ref, slot_ref, nxt_ref, [(wd_hbm, wd_stage, sem)], convert)
    r, lo, hi = r_ref[w], lo_ref[w], hi_ref[w]

    @pl.when(hi > lo)
    def _():
        y = jnp.dot(h_ref[...], wd_bf[...], preferred_element_type=F32)
        _store_item_rows(y_ref, y, r, lo, hi)


def _moe_down(items, h, wd):
    n_rows, f = h.shape
    d = wd.shape[2]
    n_items = items[0].shape[0]
    blocks = _nbytes((MOE_ROWS, f), BF16) + _nbytes((MOE_ROWS, d), F32)
    resident = 2 * _nbytes((f, d), F32) + _nbytes((f, d), BF16) + 3 * _nbytes((MOE_ROWS, d), F32)
    row_block = lambda w, r, e, lo, hi, slot, nxt: (r[w], 0)
    return pl.pallas_call(
        _moe_down_kernel,
        grid_spec=pltpu.PrefetchScalarGridSpec(
            num_scalar_prefetch=6,
            grid=(n_items,),
            in_specs=[pl.BlockSpec((MOE_ROWS, f), row_block), pl.BlockSpec(memory_space=pl.ANY)],
            out_specs=pl.BlockSpec((MOE_ROWS, d), row_block),
            scratch_shapes=[pltpu.VMEM((2, f, d), F32), pltpu.VMEM((f, d), BF16), pltpu.SemaphoreType.DMA((2,))]),
        out_shape=jax.ShapeDtypeStruct((n_rows, d), F32),
        compiler_params=_params(("arbitrary",), blocks, resident),
        name="moe_down",
    )(*items, h, wd)


def _combine_kernel(dest_ref, x_ref, route_ref, y_hbm, o_ref, ybuf, sem, *, tm):
    i = pl.program_id(0)

    def copies(tile, t):
        slot = tile % 2
        return [pltpu.make_async_copy(y_hbm.at[pl.ds(dest_ref[TOP_K * (tile * tm + t) + k], 1)],
                                      ybuf.at[slot, k, pl.ds(t, 1)], sem.at[slot]) for k in range(TOP_K)]

    def start_tile(tile):
        def start(t, carry):
            for c in copies(tile, t):
                c.start()
            return carry
        lax.fori_loop(0, tm, start, 0, unroll=2)

    @pl.when(i == 0)
    def _():
        start_tile(i)

    @pl.when(i + 1 < pl.num_programs(0))
    def _():
        start_tile(i + 1)

    def wait(t, carry):
        for c in copies(i, t):
            c.wait()
        return carry

    lax.fori_loop(0, tm, wait, 0, unroll=2)
    route = route_ref[...]
    rows = ybuf[i % 2]
    o_ref[...] = x_ref[...] + (route[:, 2:3] * rows[0] + route[:, 3:4] * rows[1])


def _combine(dest_flat, x1, route, y, tm):
    t, d = x1.shape
    blocks = 2 * _nbytes((tm, d), F32) + _nbytes((tm, LANE), F32)
    resident = 2 * TOP_K * _nbytes((tm, d), F32) + _nbytes((tm, d), F32)
    return pl.pallas_call(
        functools.partial(_combine_kernel, tm=tm),
        grid_spec=pltpu.PrefetchScalarGridSpec(
            num_scalar_prefetch=1,
            grid=(t // tm,),
            in_specs=[pl.BlockSpec((tm, d), lambda i, dest: (i, 0)),
                      pl.BlockSpec((tm, LANE), lambda i, dest: (i, 0)),
                      pl.BlockSpec(memory_space=pl.ANY)],
            out_specs=pl.BlockSpec((tm, d), lambda i, dest: (i, 0)),
            scratch_shapes=[pltpu.VMEM((2, TOP_K, tm, d), F32), pltpu.SemaphoreType.DMA((2,))]),
        out_shape=jax.ShapeDtypeStruct((t, d), F32),
        compiler_params=_params(("arbitrary",), blocks, resident),
        name="moe_combine",
    )(dest_flat, x1, route, y)


def _pad_cols(w, n):
    return jnp.pad(w, ((0, 0), (0, n - w.shape[1])))


def kernel(x, positions, g_mix, w_in, b_gate, q_norm_g, kv_norm_g, w_uq, w_ukv, a_q_norm_g, a_k_norm_g,
           b_q_norm_g, b_k_norm_g, rel_bias, w_o_a, w_o_b, w_out, g_ffn, w_group, w_expert,
           w_exp_gate, w_exp_up, w_exp_down):
    batch, seq, d = x.shape
    t = batch * seq
    q_lora, kv_lora = q_norm_g.shape[0], kv_norm_g.shape[0]
    a_heads = w_uq.shape[1] // A_QK
    b_heads = w_o_b.shape[0] // B_HEAD_DIM
    b_width = b_heads * B_HEAD_DIM
    n_groups, n_experts = w_group.shape[1], w_expert.shape[1]
    per_group = n_experts // n_groups
    off_b = q_lora + kv_lora + A_ROPE
    assert seq % ATTN_TILE == 0 and (TOP_K * t) % GATHER_ROWS == 0 and n_groups + n_experts <= LANE

    xf = x.reshape(t, d)
    tm_big = min(1024, t)
    tn = _tile(b_width, 512)
    assert d % tn == 0

    za_cols = -(-(q_lora + kv_lora + LANE) // tn) * tn
    wuq = jnp.pad(w_uq.reshape(q_lora, a_heads, A_QK), ((0, 0), (0, 0), (0, A_HEAD_PAD - A_QK)))
    wuq = wuq.reshape(q_lora, a_heads * A_HEAD_PAD).astype(BF16)
    wukv = w_ukv.astype(BF16)
    pad_gain = lambda g, s: jnp.pad(g * s, (0, A_HEAD_PAD - A_QK)).reshape(1, A_HEAD_PAD)
    gaq = pad_gain(a_q_norm_g, A_QK ** -0.5)
    gak = pad_gain(a_k_norm_g, 1.0)
    gb = jnp.concatenate([jnp.tile(b_q_norm_g * B_HEAD_DIM ** -0.5, b_heads), jnp.tile(b_k_norm_g, b_heads),
                          jnp.ones((b_width,), F32), b_gate]).reshape(1, -1)

    half = A_ROPE // 2
    inv = ROPE_THETA ** (-jnp.arange(half, dtype=F32) / half)
    ang = positions.astype(F32).reshape(t, 1) * inv
    cos, sin = jnp.cos(ang), jnp.sin(ang)
    zeros = jnp.zeros((t, half), F32)
    cos_t = jnp.concatenate([cos, cos, zeros, zeros], axis=1)
    sin_lo = jnp.concatenate([-sin, zeros, zeros, zeros], axis=1)
    sin_hi = jnp.concatenate([zeros, sin, zeros, zeros], axis=1)

    xn = _rmsnorm_rows(xf, g_mix, BF16, min(256, t))
    za = _inproj_a(xn, w_in, za_cols, tm_big, tn)
    zb = _inproj_b(xn, w_in, off_b, gb, tm_big, tn, 2 * b_width // tn, 3 * b_width // tn)
    q, k, v = _mla_proj(za, cos_t, sin_lo, sin_hi, wuq, wukv, q_norm_g.reshape(1, -1), kv_norm_g.reshape(1, -1),
                        gaq, gak, a_heads, min(256, t))
    o_a = _mla_attn(q, k, v, batch, seq, a_heads)
    o_b = _band_attn(zb, _band_bias_rows(rel_bias), batch, seq, b_heads)
    merged = _merge(o_a, o_b, w_o_a, w_o_b, zb, 3 * b_width, tm_big, tn)
    x1 = _out_proj(merged, w_out, xf, tm_big, tn)

    wr = _pad_cols(jnp.concatenate([w_group, w_expert], axis=1), LANE)
    xn2, route = _router(x1, g_ffn, wr, n_groups, per_group, min(256, t))
    rank, starts_f = _rank(route, min(512, t))
    dest = _dest(route, rank, starts_f, min(512, t))[:, :TOP_K].reshape(-1)
    xs = _gather_rows(dest, xn2, GATHER_ROWS)
    items = _work_items(starts_f[0, :n_experts].astype(jnp.int32), TOP_K * t)
    h = _moe_up(items, xs, w_exp_gate, w_exp_up)
    y = _moe_down(items, h, w_exp_down)
    out = _combine(dest, x1, route, y, min(256, t))
    return out.reshape(batch, seq, d)
```

```python
import functools

import jax
import jax.numpy as jnp
from jax import lax
from jax.experimental import pallas as pl
from jax.experimental.pallas import tpu as pltpu

F32 = jnp.float32
BF16 = jnp.bfloat16

CHUNK = 64
EPS = 1e-6
A_NOPE = 128
A_ROPE = 64
A_V = 128
A_QK = A_NOPE + A_ROPE
B_HEAD_DIM = 128
B_LEFT_CHUNKS = 8
B_MAX_REL = 128
ROPE_THETA = 10000.0
TOP_K = 2

LANE = 128
A_HEAD_PAD = 2 * LANE
V7X_VMEM_BYTES = 64 * 2**20

ATTN_TILE = 256
MOE_ROWS = 128
GATHER_ROWS = 256
NEG_INF = float("-inf")


def _nbytes(shape, dtype):
    n = 1
    for s in shape:
        n *= s
    return n * jnp.dtype(dtype).itemsize


def _params(semantics, pipelined_bytes, resident_bytes=0):
    need = 2 * pipelined_bytes + resident_bytes
    return pltpu.CompilerParams(dimension_semantics=semantics,
                                vmem_limit_bytes=min(int(need), V7X_VMEM_BYTES))


def _tile(n, want):
    t = want
    while t > LANE and n % t:
        t //= 2
    assert n % t == 0, (n, want)
    return t


def _rmsnorm_kernel(x_ref, g_ref, o_ref):
    x = x_ref[...]
    r = lax.rsqrt(jnp.mean(x * x, axis=-1, keepdims=True) + EPS)
    o_ref[...] = (x * r * g_ref[...]).astype(o_ref.dtype)


def _rmsnorm_rows(x, g, out_dtype, tm):
    t, d = x.shape
    blocks = _nbytes((tm, d), F32) + _nbytes((tm, d), out_dtype) + _nbytes((1, d), F32)
    return pl.pallas_call(
        _rmsnorm_kernel,
        grid=(t // tm,),
        in_specs=[pl.BlockSpec((tm, d), lambda i: (i, 0)), pl.BlockSpec((1, d), lambda i: (0, 0))],
        out_specs=pl.BlockSpec((tm, d), lambda i: (i, 0)),
        out_shape=jax.ShapeDtypeStruct((t, d), out_dtype),
        compiler_params=_params(("parallel",), blocks, _nbytes((tm, d), F32)),
        name="rmsnorm",
    )(x, g.reshape(1, d))


def _nt_dot(a, b):
    return lax.dot_general(a, b, (((1,), (1,)), ((), ())), preferred_element_type=F32)


def _inproj_a_kernel(a_ref, wt_ref, o_ref, wt_bf):
    @pl.when(pl.program_id(1) == 0)
    def _():
        wt_bf[...] = wt_ref[...].astype(BF16)

    o_ref[...] = _nt_dot(a_ref[...], wt_bf[...])


def _inproj_a(xn, w_in_t, n_cols, tm, tn):
    m, k = xn.shape
    blocks = _nbytes((tm, k), BF16) + _nbytes((tn, k), F32) + _nbytes((tm, tn), F32)
    return pl.pallas_call(
        _inproj_a_kernel,
        grid=(n_cols // tn, m // tm),
        in_specs=[pl.BlockSpec((tm, k), lambda j, i: (i, 0)), pl.BlockSpec((tn, k), lambda j, i: (j, 0))],
        out_specs=pl.BlockSpec((tm, tn), lambda j, i: (i, j)),
        out_shape=jax.ShapeDtypeStruct((m, n_cols), F32),
        scratch_shapes=[pltpu.VMEM((tn, k), BF16)],
        compiler_params=_params(("parallel", "arbitrary"), blocks, 2 * _nbytes((tn, k), BF16)),
        name="inproj_a",
    )(xn, w_in_t)


def _inproj_b_kernel(a_ref, wt_ref, gb_ref, o_ref, wt_bf, *, n_norm_blocks, n_plain_end):
    j = pl.program_id(0)

    @pl.when(pl.program_id(1) == 0)
    def _():
        wt_bf[...] = wt_ref[...].astype(BF16)

    acc = _nt_dot(a_ref[...], wt_bf[...])

    @pl.when(j < n_norm_blocks)
    def _():
        for h in range(acc.shape[1] // B_HEAD_DIM):
            sl = slice(h * B_HEAD_DIM, (h + 1) * B_HEAD_DIM)
            z = acc[:, sl]
            r = lax.rsqrt(jnp.mean(z * z, axis=-1, keepdims=True) + EPS)
            o_ref[:, sl] = (z * r * gb_ref[:, sl]).astype(o_ref.dtype)

    @pl.when((j >= n_norm_blocks) & (j < n_plain_end))
    def _():
        o_ref[...] = acc.astype(o_ref.dtype)

    @pl.when(j >= n_plain_end)
    def _():
        o_ref[...] = (1.0 / (1.0 + jnp.exp(-(acc + gb_ref[...])))).astype(o_ref.dtype)


def _inproj_b(xn, w_in_t, first_col, gb, tm, tn, n_norm_blocks, n_plain_end):
    m, k = xn.shape
    n = w_in_t.shape[0] - first_col
    assert n % tn == 0 and first_col % 8 == 0
    blocks = _nbytes((tm, k), BF16) + _nbytes((tn, k), F32) + _nbytes((tm, tn), BF16) + _nbytes((1, tn), F32)
    resident = 2 * _nbytes((tn, k), BF16) + 2 * _nbytes((tm, tn), F32)
    return pl.pallas_call(
        functools.partial(_inproj_b_kernel, n_norm_blocks=n_norm_blocks, n_plain_end=n_plain_end),
        grid=(n // tn, m // tm),
        in_specs=[pl.BlockSpec((tm, k), lambda j, i: (i, 0)),
                  pl.BlockSpec((pl.Element(tn), pl.Element(k)),
                               lambda j, i: (pl.multiple_of(first_col + j * tn, 8), 0)),
                  pl.BlockSpec((1, tn), lambda j, i: (0, j))],
        out_specs=pl.BlockSpec((tm, tn), lambda j, i: (i, j)),
        out_shape=jax.ShapeDtypeStruct((m, n), BF16),
        scratch_shapes=[pltpu.VMEM((tn, k), BF16)],
        compiler_params=_params(("parallel", "arbitrary"), blocks, resident),
        name="inproj_b",
    )(xn, w_in_t, gb)


def _rope_padded(v, cos, sin_lo, sin_hi):
    half = A_ROPE // 2
    return v * cos + pltpu.roll(v, LANE - half, 1) * sin_lo + pltpu.roll(v, half, 1) * sin_hi


def _mla_proj_kernel(za_ref, cos_ref, sl_ref, sh_ref, wuq_ref, wukv_ref, gq_ref, gkv_ref, gaq_ref, gak_ref,
                     q_ref, k_ref, v_ref, *, heads, q_lora, kv_lora):
    cos, sin_lo, sin_hi = cos_ref[...], sl_ref[...], sh_ref[...]

    def norm(z, g):
        r = lax.rsqrt(jnp.mean(z * z, axis=-1, keepdims=True) + EPS)
        return (z * r * g).astype(BF16)

    cq = norm(za_ref[:, :q_lora], gq_ref[...])
    ckv = norm(za_ref[:, q_lora:q_lora + kv_lora], gkv_ref[...])
    slab = za_ref[:, q_lora + kv_lora:q_lora + kv_lora + LANE]
    k_rope = jnp.where(lax.broadcasted_iota(jnp.int32, slab.shape, 1) < A_ROPE, slab, 0.0)
    qacc = jnp.dot(cq, wuq_ref[...], preferred_element_type=F32)
    kvacc = jnp.dot(ckv, wukv_ref[...], preferred_element_type=F32)

    gq_lo, gq_hi = gaq_ref[:, :LANE], gaq_ref[:, LANE:]
    gk_lo, gk_hi = gak_ref[:, :LANE], gak_ref[:, LANE:]
    kr_ss = jnp.sum(k_rope * k_rope, axis=-1, keepdims=True)
    kr_rot = _rope_padded(k_rope * gk_hi, cos, sin_lo, sin_hi)
    for h in range(heads):
        base = h * A_HEAD_PAD
        q_lo = qacc[:, base:base + LANE]
        q_hi = qacc[:, base + LANE:base + A_HEAD_PAD]
        ss = jnp.sum(q_lo * q_lo, axis=-1, keepdims=True) + jnp.sum(q_hi * q_hi, axis=-1, keepdims=True)
        r = lax.rsqrt(ss / A_QK + EPS)
        q_ref[:, base:base + LANE] = (q_lo * r * gq_lo).astype(BF16)
        q_ref[:, base + LANE:base + A_HEAD_PAD] = _rope_padded(q_hi * r * gq_hi, cos, sin_lo, sin_hi).astype(BF16)

        k_lo = kvacc[:, base:base + LANE]
        ssk = jnp.sum(k_lo * k_lo, axis=-1, keepdims=True) + kr_ss
        rk = lax.rsqrt(ssk / A_QK + EPS)
        k_ref[:, base:base + LANE] = (k_lo * rk * gk_lo).astype(BF16)
        k_ref[:, base + LANE:base + A_HEAD_PAD] = (kr_rot * rk).astype(BF16)
        v_ref[:, h * A_V:(h + 1) * A_V] = kvacc[:, base + LANE:base + A_HEAD_PAD].astype(BF16)


def _mla_proj(za, cos, sin_lo, sin_hi, wuq, wukv, gq, gkv, gaq, gak, heads, tm):
    t, za_cols = za.shape
    q_lora, kv_lora = wuq.shape[0], wukv.shape[0]
    hp = heads * A_HEAD_PAD
    row = lambda i: (i, 0)
    fix = lambda i: (0, 0)
    blocks = (_nbytes((tm, za_cols), F32) + 3 * _nbytes((tm, LANE), F32) + _nbytes(wuq.shape, BF16)
              + _nbytes(wukv.shape, BF16) + 2 * _nbytes((tm, hp), BF16) + _nbytes((tm, heads * A_V), BF16))
    return pl.pallas_call(
        functools.partial(_mla_proj_kernel, heads=heads, q_lora=q_lora, kv_lora=kv_lora),
        grid=(t // tm,),
        in_specs=[pl.BlockSpec((tm, za_cols), row),
                  pl.BlockSpec((tm, LANE), row), pl.BlockSpec((tm, LANE), row), pl.BlockSpec((tm, LANE), row),
                  pl.BlockSpec(wuq.shape, fix), pl.BlockSpec(wukv.shape, fix),
                  pl.BlockSpec((1, q_lora), fix), pl.BlockSpec((1, kv_lora), fix),
                  pl.BlockSpec((1, A_HEAD_PAD), fix), pl.BlockSpec((1, A_HEAD_PAD), fix)],
        out_specs=[pl.BlockSpec((tm, hp), row), pl.BlockSpec((tm, hp), row), pl.BlockSpec((tm, heads * A_V), row)],
        out_shape=[jax.ShapeDtypeStruct((t, hp), BF16), jax.ShapeDtypeStruct((t, hp), BF16),
                   jax.ShapeDtypeStruct((t, heads * A_V), BF16)],
        compiler_params=_params(("parallel",), blocks, 3 * _nbytes((tm, hp), F32)),
        name="mla_proj",
    )(za, cos, sin_lo, sin_hi, wuq, wukv, gq, gkv, gaq, gak)


def _mla_attn_kernel(q_ref, k_ref, v_ref, o_ref, *, seq):
    tq = ATTN_TILE
    rc = lax.broadcasted_iota(jnp.int32, (tq, tq), 0) // CHUNK
    cc = lax.broadcasted_iota(jnp.int32, (tq, tq), 1) // CHUNK
    diag_ok = cc <= rc
    for i in range(seq // tq):
        q = q_ref[i * tq:(i + 1) * tq, :]
        sd = jnp.where(diag_ok, _nt_dot(q, k_ref[i * tq:(i + 1) * tq, :]), NEG_INF)
        m = jnp.max(sd, axis=-1, keepdims=True)
        if i > 0:
            s0 = _nt_dot(q, k_ref[:i * tq, :])
            m = jnp.maximum(m, jnp.max(s0, axis=-1, keepdims=True))
        pd = jnp.exp(sd - m)
        l = jnp.sum(pd, axis=-1, keepdims=True)
        o = jnp.dot(pd.astype(BF16), v_ref[i * tq:(i + 1) * tq, :], preferred_element_type=F32)
        if i > 0:
            p0 = jnp.exp(s0 - m)
            l = l + jnp.sum(p0, axis=-1, keepdims=True)
            o = o + jnp.dot(p0.astype(BF16), v_ref[:i * tq, :], preferred_element_type=F32)
        o_ref[i * tq:(i + 1) * tq, :] = (o / l).astype(o_ref.dtype)


def _mla_attn(q, k, v, batch, seq, heads):
    t = batch * seq
    blocks = 2 * _nbytes((seq, A_HEAD_PAD), BF16) + 2 * _nbytes((seq, A_V), BF16)
    temps = 6 * _nbytes((ATTN_TILE, seq), F32)
    return pl.pallas_call(
        functools.partial(_mla_attn_kernel, seq=seq),
        grid=(batch, heads),
        in_specs=[pl.BlockSpec((seq, A_HEAD_PAD), lambda b, h: (b, h)),
                  pl.BlockSpec((seq, A_HEAD_PAD), lambda b, h: (b, h)),
                  pl.BlockSpec((seq, A_V), lambda b, h: (b, h))],
        out_specs=pl.BlockSpec((seq, A_V), lambda b, h: (b, h)),
        out_shape=jax.ShapeDtypeStruct((t, heads * A_V), BF16),
        compiler_params=_params(("parallel", "parallel"), blocks, temps),
        name="mla_attn",
    )(q, k, v)


def _band_attn_kernel(q_ref, k_ref, v_ref, r_ref, o_ref, *, seq):
    tq = ATTN_TILE
    left = B_LEFT_CHUNKS * CHUNK
    width = r_ref.shape[1]
    bias = pltpu.roll(jnp.broadcast_to(r_ref[...], (tq, width)), 0, 1, stride=1, stride_axis=0)[:, :left + tq]
    q_chunk = lax.broadcasted_iota(jnp.int32, (tq, left + tq), 0) // CHUNK
    k_chunk = lax.broadcasted_iota(jnp.int32, (tq, left + tq), 1) // CHUNK
    table = jnp.where((k_chunk >= q_chunk) & (k_chunk <= q_chunk + B_LEFT_CHUNKS), bias, NEG_INF)
    for i in range(seq // tq):
        q0 = i * tq
        k0 = max(0, q0 - left)
        kw = q0 + tq - k0
        u0 = left - q0 + k0
        q = q_ref[q0:q0 + tq, :]
        s = _nt_dot(q, k_ref[k0:k0 + kw, :]) + table[:, u0:u0 + kw]
        m = jnp.max(s, axis=-1, keepdims=True)
        p = jnp.exp(s - m)
        l = jnp.sum(p, axis=-1, keepdims=True)
        o = jnp.dot(p.astype(BF16), v_ref[k0:k0 + kw, :], preferred_element_type=F32)
        o_ref[q0:q0 + tq, :] = (o / l).astype(o_ref.dtype)


def _band_bias_rows(rel_bias):
    left = B_LEFT_CHUNKS * CHUNK
    width = left + 2 * ATTN_TILE
    m = jnp.arange(width, dtype=jnp.int32)
    j = jnp.where(m < left + ATTN_TILE, m, m - width)
    dist = left - j
    rows = rel_bias[:, jnp.clip(dist, -B_MAX_REL, B_MAX_REL) + B_MAX_REL].astype(F32)
    return rows.reshape(rel_bias.shape[0], 1, width)


def _band_attn(zb, bias_rows, batch, seq, heads):
    t = batch * seq
    d = B_HEAD_DIM
    width = bias_rows.shape[2]
    blocks = 4 * _nbytes((seq, d), BF16) + _nbytes((1, width), F32)
    temps = 8 * _nbytes((ATTN_TILE, width), F32)
    return pl.pallas_call(
        functools.partial(_band_attn_kernel, seq=seq),
        grid=(batch, heads),
        in_specs=[pl.BlockSpec((seq, d), lambda b, h: (b, h)),
                  pl.BlockSpec((seq, d), lambda b, h: (b, heads + h)),
                  pl.BlockSpec((seq, d), lambda b, h: (b, 2 * heads + h)),
                  pl.BlockSpec((None, 1, width), lambda b, h: (h, 0, 0))],
        out_specs=pl.BlockSpec((seq, d), lambda b, h: (b, h)),
        out_shape=jax.ShapeDtypeStruct((t, heads * d), BF16),
        compiler_params=_params(("parallel", "parallel"), blocks, temps),
        name="band_attn",
    )(zb, zb, zb, bias_rows)


def _merge_kernel(oa_ref, ob_ref, woa_ref, wob_ref, g0_ref, g1_ref, o_ref, woa_bf, wob_bf):
    @pl.when(pl.program_id(1) == 0)
    def _():
        woa_bf[...] = woa_ref[...].astype(BF16)
        wob_bf[...] = wob_ref[...].astype(BF16)

    a = jnp.dot(oa_ref[...], woa_bf[...], preferred_element_type=F32)
    b = jnp.dot(ob_ref[...], wob_bf[...], preferred_element_type=F32)
    o_ref[...] = (g0_ref[...].astype(F32) * a + g1_ref[...].astype(F32) * b).astype(o_ref.dtype)


def _merge(oa, ob, woa, wob, zb, gate_col, tm, tn):
    m = oa.shape[0]
    d = woa.shape[1]
    g0, g1 = gate_col // tn, (gate_col + d) // tn
    blocks = (_nbytes((tm, oa.shape[1]), BF16) + _nbytes((tm, ob.shape[1]), BF16) + _nbytes((woa.shape[0], tn), F32)
              + _nbytes((wob.shape[0], tn), F32) + 3 * _nbytes((tm, tn), BF16))
    resident = (2 * _nbytes((woa.shape[0], tn), BF16) + 2 * _nbytes((wob.shape[0], tn), BF16)
                + 3 * _nbytes((tm, tn), F32))
    return pl.pallas_call(
        _merge_kernel,
        grid=(d // tn, m // tm),
        in_specs=[pl.BlockSpec((tm, oa.shape[1]), lambda j, i: (i, 0)),
                  pl.BlockSpec((tm, ob.shape[1]), lambda j, i: (i, 0)),
                  pl.BlockSpec((woa.shape[0], tn), lambda j, i: (0, j)),
                  pl.BlockSpec((wob.shape[0], tn), lambda j, i: (0, j)),
                  pl.BlockSpec((tm, tn), lambda j, i: (i, g0 + j)),
                  pl.BlockSpec((tm, tn), lambda j, i: (i, g1 + j))],
        out_specs=pl.BlockSpec((tm, tn), lambda j, i: (i, j)),
        out_shape=jax.ShapeDtypeStruct((m, d), BF16),
        scratch_shapes=[pltpu.VMEM((woa.shape[0], tn), BF16), pltpu.VMEM((wob.shape[0], tn), BF16)],
        compiler_params=_params(("parallel", "arbitrary"), blocks, resident),
        name="merge",
    )(oa, ob, woa, wob, zb, zb)


def _out_proj_kernel(a_ref, w_ref, x_ref, o_ref, w_bf):
    @pl.when(pl.program_id(1) == 0)
    def _():
        w_bf[...] = w_ref[...].astype(BF16)

    o_ref[...] = x_ref[...] + jnp.dot(a_ref[...], w_bf[...], preferred_element_type=F32)


def _out_proj(merged, wout, x, tm, tn):
    m, k = merged.shape
    n = wout.shape[1]
    blocks = _nbytes((tm, k), BF16) + _nbytes((k, tn), F32) + 2 * _nbytes((tm, tn), F32)
    return pl.pallas_call(
        _out_proj_kernel,
        grid=(n // tn, m // tm),
        in_specs=[pl.BlockSpec((tm, k), lambda j, i: (i, 0)),
                  pl.BlockSpec((k, tn), lambda j, i: (0, j)),
                  pl.BlockSpec((tm, tn), lambda j, i: (i, j))],
        out_specs=pl.BlockSpec((tm, tn), lambda j, i: (i, j)),
        out_shape=jax.ShapeDtypeStruct((m, n), F32),
        scratch_shapes=[pltpu.VMEM((k, tn), BF16)],
        compiler_params=_params(("parallel", "arbitrary"), blocks, 2 * _nbytes((k, tn), BF16) + _nbytes((tm, tn), F32)),
        name="out_proj",
    )(merged, wout, x)


def _router_kernel(x_ref, g_ref, wr_ref, xn_ref, route_ref, *, n_groups, per_group):
    x = x_ref[...]
    r = lax.rsqrt(jnp.mean(x * x, axis=-1, keepdims=True) + EPS)
    xn = x * r * g_ref[...]
    xn_ref[...] = xn
    logits = jnp.dot(xn, wr_ref[...], preferred_element_type=F32, precision=lax.Precision.HIGHEST)
    lane = lax.broadcasted_iota(jnp.int32, logits.shape, 1).astype(F32)
    far = float(LANE)

    def top(vals):
        best = jnp.max(vals, axis=-1, keepdims=True)
        return best, jnp.min(jnp.where(vals == best, lane, far), axis=-1, keepdims=True)

    gl = jnp.where(lane < n_groups, logits, NEG_INF)
    gmax, grp = top(gl)
    p_grp = 1.0 / jnp.sum(jnp.exp(gl - gmax), axis=-1, keepdims=True)
    lo = n_groups + grp * per_group
    el = jnp.where((lane >= lo) & (lane < lo + per_group), logits, NEG_INF)
    t1, i1 = top(el)
    t2, i2 = top(jnp.where(lane == i1, NEG_INF, el))
    d = jnp.exp(t2 - t1)
    w1 = p_grp / (1.0 + d)
    w2 = p_grp * d / (1.0 + d)
    route_ref[...] = jnp.where(lane == 0, i1 - n_groups,
                               jnp.where(lane == 1, i2 - n_groups,
                                         jnp.where(lane == 2, w1, jnp.where(lane == 3, w2, 0.0))))


def _router(x1, g, wr, n_groups, per_group, tm):
    t, d = x1.shape
    blocks = 2 * _nbytes((tm, d), F32) + _nbytes((1, d), F32) + _nbytes((d, LANE), F32) + _nbytes((tm, LANE), F32)
    return pl.pallas_call(
        functools.partial(_router_kernel, n_groups=n_groups, per_group=per_group),
        grid=(t // tm,),
        in_specs=[pl.BlockSpec((tm, d), lambda i: (i, 0)), pl.BlockSpec((1, d), lambda i: (0, 0)),
                  pl.BlockSpec((d, LANE), lambda i: (0, 0))],
        out_specs=[pl.BlockSpec((tm, d), lambda i: (i, 0)), pl.BlockSpec((tm, LANE), lambda i: (i, 0))],
        out_shape=[jax.ShapeDtypeStruct((t, d), F32), jax.ShapeDtypeStruct((t, LANE), F32)],
        compiler_params=_params(("parallel",), blocks, 2 * _nbytes((tm, d), F32)),
        name="router",
    )(x1, g.reshape(1, d), wr)


def _one_hots(route):
    lane = lax.broadcasted_iota(jnp.int32, route.shape, 1).astype(F32)
    return (lane == route[:, 0:1]).astype(F32), (lane == route[:, 1:2]).astype(F32)


def _rank_kernel(route_ref, rank_ref, starts_ref, count_acc, start_acc):
    i = pl.program_id(0)

    @pl.when(i == 0)
    def _():
        count_acc[...] = jnp.zeros_like(count_acc)
        start_acc[...] = jnp.zeros_like(start_acc)

    oh1, oh2 = _one_hots(route_ref[...])
    oh = (oh1 + oh2).astype(BF16)
    tm = oh.shape[0]
    earlier = (lax.broadcasted_iota(jnp.int32, (tm, tm), 0) > lax.broadcasted_iota(jnp.int32, (tm, tm), 1))
    before = jnp.dot(earlier.astype(BF16), oh, preferred_element_type=F32) + count_acc[...]
    lane = lax.broadcasted_iota(jnp.int32, (tm, LANE), 1)
    rank_ref[...] = jnp.where(lane == 0, jnp.sum(oh1 * before, axis=-1, keepdims=True),
                              jnp.where(lane == 1, jnp.sum(oh2 * before, axis=-1, keepdims=True), 0.0))
    lower = (lax.broadcasted_iota(jnp.int32, (LANE, LANE), 0) < lax.broadcasted_iota(jnp.int32, (LANE, LANE), 1))
    below = jnp.dot(oh, lower.astype(BF16), preferred_element_type=F32)
    count_acc[...] += jnp.sum(oh.astype(F32), axis=0, keepdims=True)
    start_acc[...] += jnp.sum(below, axis=0, keepdims=True)
    starts_ref[...] = start_acc[...]


def _rank(route, tm):
    t = route.shape[0]
    blocks = 2 * _nbytes((tm, LANE), F32) + _nbytes((1, LANE), F32)
    return pl.pallas_call(
        _rank_kernel,
        grid=(t // tm,),
        in_specs=[pl.BlockSpec((tm, LANE), lambda i: (i, 0))],
        out_specs=[pl.BlockSpec((tm, LANE), lambda i: (i, 0)), pl.BlockSpec((1, LANE), lambda i: (0, 0))],
        out_shape=[jax.ShapeDtypeStruct((t, LANE), F32), jax.ShapeDtypeStruct((1, LANE), F32)],
        scratch_shapes=[pltpu.VMEM((1, LANE), F32), pltpu.VMEM((1, LANE), F32)],
        compiler_params=_params(("arbitrary",), blocks, 2 * _nbytes((tm, tm), F32)),
        name="moe_rank",
    )(route)


def _dest_kernel(route_ref, rank_ref, starts_ref, dest_ref):
    oh1, oh2 = _one_hots(route_ref[...])
    rank = rank_ref[...]
    starts = starts_ref[...]
    d1 = jnp.sum(oh1 * starts, axis=-1, keepdims=True) + rank[:, 0:1]
    d2 = jnp.sum(oh2 * starts, axis=-1, keepdims=True) + rank[:, 1:2]
    lane = lax.broadcasted_iota(jnp.int32, rank.shape, 1)
    dest_ref[...] = jnp.where(lane == 0, d1, jnp.where(lane == 1, d2, 0.0)).astype(jnp.int32)


def _dest(route, rank, starts, tm):
    t = route.shape[0]
    blocks = 3 * _nbytes((tm, LANE), F32) + _nbytes((1, LANE), F32)
    return pl.pallas_call(
        _dest_kernel,
        grid=(t // tm,),
        in_specs=[pl.BlockSpec((tm, LANE), lambda i: (i, 0)), pl.BlockSpec((tm, LANE), lambda i: (i, 0)),
                  pl.BlockSpec((1, LANE), lambda i: (0, 0))],
        out_specs=pl.BlockSpec((tm, LANE), lambda i: (i, 0)),
        out_shape=jax.ShapeDtypeStruct((t, LANE), jnp.int32),
        compiler_params=_params(("parallel",), blocks, 4 * _nbytes((tm, LANE), F32)),
        name="moe_dest",
    )(route, rank, starts)


def _gather_rows_kernel(dest_ref, x_hbm, xs_ref, source, rows_f32, sem, *, n_assign):
    rows = xs_ref.shape[0]
    base = pl.program_id(0) * rows

    @pl.when(pl.program_id(0) == 0)
    def _():
        def invert(a, carry):
            source[dest_ref[a]] = a
            return carry
        lax.fori_loop(0, n_assign, invert, 0, unroll=8)

    def copy(j):
        return pltpu.make_async_copy(x_hbm.at[pl.ds(source[base + j] // TOP_K, 1)], rows_f32.at[pl.ds(j, 1)], sem)

    def start(j, carry):
        copy(j).start()
        return carry

    def wait(j, carry):
        copy(j).wait()
        return carry

    lax.fori_loop(0, rows, start, 0, unroll=4)
    lax.fori_loop(0, rows, wait, 0, unroll=4)
    xs_ref[...] = rows_f32[...].astype(xs_ref.dtype)


def _gather_rows(dest_flat, xn, rows):
    t, d = xn.shape
    n_assign = dest_flat.shape[0]
    return pl.pallas_call(
        functools.partial(_gather_rows_kernel, n_assign=n_assign),
        grid_spec=pltpu.PrefetchScalarGridSpec(
            num_scalar_prefetch=1,
            grid=(n_assign // rows,),
            in_specs=[pl.BlockSpec(memory_space=pl.ANY)],
            out_specs=pl.BlockSpec((rows, d), lambda i, dest: (i, 0)),
            scratch_shapes=[pltpu.SMEM((n_assign,), jnp.int32), pltpu.VMEM((rows, d), xn.dtype),
                            pltpu.SemaphoreType.DMA(())]),
        out_shape=jax.ShapeDtypeStruct((n_assign, d), BF16),
        compiler_params=_params(("arbitrary",), _nbytes((rows, d), BF16), 2 * _nbytes((rows, d), xn.dtype)),
        name="moe_gather_rows",
    )(dest_flat, xn)


def _work_items(starts, n_rows):
    n_exp = starts.shape[0]
    n_blk = n_rows // MOE_ROWS
    total = jnp.full((1,), n_rows, jnp.int32)
    pts = jnp.concatenate([jnp.arange(n_blk, dtype=jnp.int32) * MOE_ROWS, starts[1:]])
    idx = jnp.arange(pts.shape[0], dtype=jnp.int32)
    before = (pts[None, :] < pts[:, None]) | ((pts[None, :] == pts[:, None]) & (idx[None, :] < idx[:, None]))
    pos = jnp.sum(before.astype(jnp.int32), axis=1)
    lo = jnp.sum(jnp.where(pos[:, None] == idx[None, :], pts[:, None], 0), axis=0)
    hi = jnp.concatenate([lo[1:], total])
    ends = jnp.concatenate([starts[1:], total])
    r = jnp.minimum(lo // MOE_ROWS, n_blk - 1)
    e = jnp.minimum(jnp.sum((ends[None, :] <= lo[:, None]).astype(jnp.int32), axis=1), n_exp - 1)
    changed = jnp.concatenate([jnp.ones((1,), jnp.int32), (e[1:] != e[:-1]).astype(jnp.int32)])
    slot = (jnp.sum(jnp.where(idx[None, :] <= idx[:, None], changed[None, :], 0), axis=1) - 1) % 2
    later = jnp.where(e[None, :] > e[:, None], e[None, :], n_exp)
    nxt = jnp.min(later, axis=1)
    nxt = jnp.where(nxt == n_exp, -1, nxt)
    return r, e, lo, hi, slot, nxt


def _stream_expert_weights(w, e_ref, slot_ref, nxt_ref, streams, convert):
    e = e_ref[w]
    slot = slot_ref[w]

    def copies(expert, into):
        return [pltpu.make_async_copy(hbm.at[expert], stage.at[into], sem.at[into]) for hbm, stage, sem in streams]

    @pl.when(w == 0)
    def _():
        for c in copies(e, slot):
            c.start()

    @pl.when((w == 0) | (e != e_ref[jnp.maximum(w - 1, 0)]))
    def _():
        for c in copies(e, slot):
            c.wait()
        nxt = nxt_ref[w]

        @pl.when(nxt >= 0)
        def _():
            for c in copies(nxt, 1 - slot):
                c.start()

        convert(slot)


def _store_item_rows(o_ref, val, r, lo, hi):
    rows = r * MOE_ROWS + lax.broadcasted_iota(jnp.int32, (MOE_ROWS, 1), 0)
    mine = (rows >= lo) & (rows < hi)

    @pl.when(lo == r * MOE_ROWS)
    def _():
        o_ref[...] = val

    @pl.when(lo != r * MOE_ROWS)
    def _():
        o_ref[...] = jnp.where(mine, val, o_ref[...])


def _moe_up_kernel(r_ref, e_ref, lo_ref, hi_ref, slot_ref, nxt_ref, xs_ref, wg_hbm, wu_hbm, h_ref,
                   wg_stage, wu_stage, wg_bf, wu_bf, sem_g, sem_u):
    w = pl.program_id(0)

    def convert(slot):
        wg_bf[...] = wg_stage[slot].astype(BF16)
        wu_bf[...] = wu_stage[slot].astype(BF16)

    _stream_expert_weights(w, e_ref, slot_ref, nxt_ref,
                           [(wg_hbm, wg_stage, sem_g), (wu_hbm, wu_stage, sem_u)], convert)
    r, lo, hi = r_ref[w], lo_ref[w], hi_ref[w]

    @pl.when(hi > lo)
    def _():
        x = xs_ref[...]
        g = jnp.dot(x, wg_bf[...], preferred_element_type=F32)
        u = jnp.dot(x, wu_bf[...], preferred_element_type=F32)
        h = (g * (1.0 / (1.0 + jnp.exp(-g)))) * u
        _store_item_rows(h_ref, h.astype(h_ref.dtype), r, lo, hi)


def _moe_up(items, xs, wg, wu):
    n_rows, d = xs.shape
    _, _, f = wg.shape
    n_items = items[0].shape[0]
    blocks = _nbytes((MOE_ROWS, d), BF16) + _nbytes((MOE_ROWS, f), BF16)
    resident = 4 * _nbytes((d, f), F32) + 2 * _nbytes((d, f), BF16) + 6 * _nbytes((MOE_ROWS, f), F32)
    row_block = lambda w, r, e, lo, hi, slot, nxt: (r[w], 0)
    return pl.pallas_call(
        _moe_up_kernel,
        grid_spec=pltpu.PrefetchScalarGridSpec(
            num_scalar_prefetch=6,
            grid=(n_items,),
            in_specs=[pl.BlockSpec((MOE_ROWS, d), row_block),
                      pl.BlockSpec(memory_space=pl.ANY), pl.BlockSpec(memory_space=pl.ANY)],
            out_specs=pl.BlockSpec((MOE_ROWS, f), row_block),
            scratch_shapes=[pltpu.VMEM((2, d, f), F32), pltpu.VMEM((2, d, f), F32),
                            pltpu.VMEM((d, f), BF16), pltpu.VMEM((d, f), BF16),
                            pltpu.SemaphoreType.DMA((2,)), pltpu.SemaphoreType.DMA((2,))]),
        out_shape=jax.ShapeDtypeStruct((n_rows, f), BF16),
        compiler_params=_params(("arbitrary",), blocks, resident),
        name="moe_up",
    )(*items, xs, wg, wu)


def _moe_down_kernel(r_ref, e_ref, lo_ref, hi_ref, slot_ref, nxt_ref, h_ref, wd_hbm, y_ref, wd_stage, wd_bf, sem):
    w = pl.program_id(0)

    def convert(slot):
        wd_bf[...] = wd_stage[slot].astype(BF16)

    _stream_expert_weights(w, e_ref, slot_ref, nxt_ref, [(wd_hbm, wd_stage, sem)], convert)
    r, lo, hi = r_ref[w], lo_ref[w], hi_ref[w]

    @pl.when(hi > lo)
    def _():
        y = jnp.dot(h_ref[...], wd_bf[...], preferred_element_type=F32)
        _store_item_rows(y_ref, y, r, lo, hi)


def _moe_down(items, h, wd):
    n_rows, f = h.shape
    d = wd.shape[2]
    n_items = items[0].shape[0]
    blocks = _nbytes((MOE_ROWS, f), BF16) + _nbytes((MOE_ROWS, d), F32)
    resident = 2 * _nbytes((f, d), F32) + _nbytes((f, d), BF16) + 3 * _nbytes((MOE_ROWS, d), F32)
    row_block = lambda w, r, e, lo, hi, slot, nxt: (r[w], 0)
    return pl.pallas_call(
        _moe_down_kernel,
        grid_spec=pltpu.PrefetchScalarGridSpec(
            num_scalar_prefetch=6,
            grid=(n_items,),
            in_specs=[pl.BlockSpec((MOE_ROWS, f), row_block), pl.BlockSpec(memory_space=pl.ANY)],
            out_specs=pl.BlockSpec((MOE_ROWS, d), row_block),
            scratch_shapes=[pltpu.VMEM((2, f, d), F32), pltpu.VMEM((f, d), BF16), pltpu.SemaphoreType.DMA((2,))]),
        out_shape=jax.ShapeDtypeStruct((n_rows, d), F32),
        compiler_params=_params(("arbitrary",), blocks, resident),
        name="moe_down",
    )(*items, h, wd)


def _combine_kernel(dest_ref, x_ref, route_ref, y_hbm, o_ref, ybuf, sem, *, tm):
    i = pl.program_id(0)

    def copies(tile, t):
        slot = tile % 2
        return [pltpu.make_async_copy(y_hbm.at[pl.ds(dest_ref[TOP_K * (tile * tm + t) + k], 1)],
                                      ybuf.at[slot, k, pl.ds(t, 1)], sem.at[slot]) for k in range(TOP_K)]

    def start_tile(tile):
        def start(t, carry):
            for c in copies(tile, t):
                c.start()
            return carry
        lax.fori_loop(0, tm, start, 0, unroll=2)

    @pl.when(i == 0)
    def _():
        start_tile(i)

    @pl.when(i + 1 < pl.num_programs(0))
    def _():
        start_tile(i + 1)

    def wait(t, carry):
        for c in copies(i, t):
            c.wait()
        return carry

    lax.fori_loop(0, tm, wait, 0, unroll=2)
    route = route_ref[...]
    rows = ybuf[i % 2]
    o_ref[...] = x_ref[...] + (route[:, 2:3] * rows[0] + route[:, 3:4] * rows[1])


def _combine(dest_flat, x1, route, y, tm):
    t, d = x1.shape
    blocks = 2 * _nbytes((tm, d), F32) + _nbytes((tm, LANE), F32)
    resident = 2 * TOP_K * _nbytes((tm, d), F32) + _nbytes((tm, d), F32)
    return pl.pallas_call(
        functools.partial(_combine_kernel, tm=tm),
        grid_spec=pltpu.PrefetchScalarGridSpec(
            num_scalar_prefetch=1,
            grid=(t // tm,),
            in_specs=[pl.BlockSpec((tm, d), lambda i, dest: (i, 0)),
                      pl.BlockSpec((tm, LANE), lambda i, dest: (i, 0)),
                      pl.BlockSpec(memory_space=pl.ANY)],
            out_specs=pl.BlockSpec((tm, d), lambda i, dest: (i, 0)),
            scratch_shapes=[pltpu.VMEM((2, TOP_K, tm, d), F32), pltpu.SemaphoreType.DMA((2,))]),
        out_shape=jax.ShapeDtypeStruct((t, d), F32),
        compiler_params=_params(("arbitrary",), blocks, resident),
        name="moe_combine",
    )(dest_flat, x1, route, y)


def _pad_cols(w, n):
    return jnp.pad(w, ((0, 0), (0, n - w.shape[1])))


def kernel(x, positions, g_mix, w_in, b_gate, q_norm_g, kv_norm_g, w_uq, w_ukv, a_q_norm_g, a_k_norm_g,
           b_q_norm_g, b_k_norm_g, rel_bias, w_o_a, w_o_b, w_out, g_ffn, w_group, w_expert,
           w_exp_gate, w_exp_up, w_exp_down):
    batch, seq, d = x.shape
    t = batch * seq
    q_lora, kv_lora = q_norm_g.shape[0], kv_norm_g.shape[0]
    a_heads = w_uq.shape[1] // A_QK
    b_heads = w_o_b.shape[0] // B_HEAD_DIM
    b_width = b_heads * B_HEAD_DIM
    n_groups, n_experts = w_group.shape[1], w_expert.shape[1]
    per_group = n_experts // n_groups
    off_b = q_lora + kv_lora + A_ROPE
    assert seq % ATTN_TILE == 0 and (TOP_K * t) % GATHER_ROWS == 0 and n_groups + n_experts <= LANE

    xf = x.reshape(t, d)
    tm_big = min(1024, t)
    tn = _tile(b_width, 512)
    assert d % tn == 0

    za_cols = -(-(q_lora + kv_lora + LANE) // tn) * tn
    wuq = jnp.pad(w_uq.reshape(q_lora, a_heads, A_QK), ((0, 0), (0, 0), (0, A_HEAD_PAD - A_QK)))
    wuq = wuq.reshape(q_lora, a_heads * A_HEAD_PAD).astype(BF16)
    wukv = w_ukv.astype(BF16)
    pad_gain = lambda g, s: jnp.pad(g * s, (0, A_HEAD_PAD - A_QK)).reshape(1, A_HEAD_PAD)
    gaq = pad_gain(a_q_norm_g, A_QK ** -0.5)
    gak = pad_gain(a_k_norm_g, 1.0)
    gb = jnp.concatenate([jnp.tile(b_q_norm_g * B_HEAD_DIM ** -0.5, b_heads), jnp.tile(b_k_norm_g, b_heads),
                          jnp.ones((b_width,), F32), b_gate]).reshape(1, -1)

    half = A_ROPE // 2
    inv = ROPE_THETA ** (-jnp.arange(half, dtype=F32) / half)
    ang = positions.astype(F32).reshape(t, 1) * inv
    cos, sin = jnp.cos(ang), jnp.sin(ang)
    zeros = jnp.zeros((t, half), F32)
    cos_t = jnp.concatenate([cos, cos, zeros, zeros], axis=1)
    sin_lo = jnp.concatenate([-sin, zeros, zeros, zeros], axis=1)
    sin_hi = jnp.concatenate([zeros, sin, zeros, zeros], axis=1)

    xn = _rmsnorm_rows(xf, g_mix, BF16, min(256, t))
    w_in_t = w_in.T
    za = _inproj_a(xn, w_in_t, za_cols, tm_big, tn)
    zb = _inproj_b(xn, w_in_t, off_b, gb, tm_big, tn, 2 * b_width // tn, 3 * b_width // tn)
    q, k, v = _mla_proj(za, cos_t, sin_lo, sin_hi, wuq, wukv, q_norm_g.reshape(1, -1), kv_norm_g.reshape(1, -1),
                        gaq, gak, a_heads, min(256, t))
    o_a = _mla_attn(q, k, v, batch, seq, a_heads)
    o_b = _band_attn(zb, _band_bias_rows(rel_bias), batch, seq, b_heads)
    merged = _merge(o_a, o_b, w_o_a, w_o_b, zb, 3 * b_width, tm_big, tn)
    x1 = _out_proj(merged, w_out, xf, tm_big, tn)

    wr = _pad_cols(jnp.concatenate([w_group, w_expert], axis=1), LANE)
    xn2, route = _router(x1, g_ffn, wr, n_groups, per_group, min(256, t))
    rank, starts_f = _rank(route, min(512, t))
    dest = _dest(route, rank, starts_f, min(512, t))[:, :TOP_K].reshape(-1)
    xs = _gather_rows(dest, xn2, GATHER_ROWS)
    items = _work_items(starts_f[0, :n_experts].astype(jnp.int32), TOP_K * t)
    h = _moe_up(items, xs, w_exp_gate, w_exp_up)
    y = _moe_down(items, h, w_exp_down)
    out = _combine(dest, x1, route, y, min(256, t))
    return out.reshape(batch, seq, d)
```

```python
import functools

import jax
import jax.numpy as jnp
from jax import lax
from jax.experimental import pallas as pl
from jax.experimental.pallas import tpu as pltpu

F32 = jnp.float32
BF16 = jnp.bfloat16

CHUNK = 64
EPS = 1e-6
A_NOPE = 128
A_ROPE = 64
A_V = 128
A_QK = A_NOPE + A_ROPE
B_HEAD_DIM = 128
B_LEFT_CHUNKS = 8
B_MAX_REL = 128
ROPE_THETA = 10000.0
TOP_K = 2

LANE = 128
A_HEAD_PAD = 2 * LANE
V7X_VMEM_BYTES = 64 * 2**20

ATTN_TILE = 256
MOE_ROWS = 128
GATHER_ROWS = 256
NEG_INF = float("-inf")


def _nbytes(shape, dtype):
    n = 1
    for s in shape:
        n *= s
    return n * jnp.dtype(dtype).itemsize


def _params(semantics, pipelined_bytes, resident_bytes=0):
    need = 2 * pipelined_bytes + resident_bytes
    return pltpu.CompilerParams(dimension_semantics=semantics,
                                vmem_limit_bytes=min(int(need), V7X_VMEM_BYTES))


def _tile(n, want):
    t = want
    while t > LANE and n % t:
        t //= 2
    assert n % t == 0, (n, want)
    return t


def _rmsnorm_kernel(x_ref, g_ref, o_ref):
    x = x_ref[...]
    r = lax.rsqrt(jnp.mean(x * x, axis=-1, keepdims=True) + EPS)
    o_ref[...] = (x * r * g_ref[...]).astype(o_ref.dtype)


def _rmsnorm_rows(x, g, out_dtype, tm):
    t, d = x.shape
    blocks = _nbytes((tm, d), F32) + _nbytes((tm, d), out_dtype) + _nbytes((1, d), F32)
    return pl.pallas_call(
        _rmsnorm_kernel,
        grid=(t // tm,),
        in_specs=[pl.BlockSpec((tm, d), lambda i: (i, 0)), pl.BlockSpec((1, d), lambda i: (0, 0))],
        out_specs=pl.BlockSpec((tm, d), lambda i: (i, 0)),
        out_shape=jax.ShapeDtypeStruct((t, d), out_dtype),
        compiler_params=_params(("parallel",), blocks, _nbytes((tm, d), F32)),
        name="rmsnorm",
    )(x, g.reshape(1, d))


def _nt_dot(a, b):
    return lax.dot_general(a, b, (((1,), (1,)), ((), ())), preferred_element_type=F32)


def _inproj_a_kernel(a_ref, wt_ref, o_ref, wt_bf):
    @pl.when(pl.program_id(1) == 0)
    def _():
        wt_bf[...] = wt_ref[...].astype(BF16)

    o_ref[...] = _nt_dot(a_ref[...], wt_bf[...])


def _inproj_a(xn, w_in_t, n_cols, tm, tn):
    m, k = xn.shape
    blocks = _nbytes((tm, k), BF16) + _nbytes((tn, k), F32) + _nbytes((tm, tn), F32)
    return pl.pallas_call(
        _inproj_a_kernel,
        grid=(n_cols // tn, m // tm),
        in_specs=[pl.BlockSpec((tm, k), lambda j, i: (i, 0)), pl.BlockSpec((tn, k), lambda j, i: (j, 0))],
        out_specs=pl.BlockSpec((tm, tn), lambda j, i: (i, j)),
        out_shape=jax.ShapeDtypeStruct((m, n_cols), F32),
        scratch_shapes=[pltpu.VMEM((tn, k), BF16)],
        compiler_params=_params(("parallel", "arbitrary"), blocks, 2 * _nbytes((tn, k), BF16)),
        name="inproj_a",
    )(xn, w_in_t)


def _inproj_b_kernel(a_ref, wt_hbm, gb_ref, o_ref, stage, wt_bf, sem, *, first_col, n_norm_blocks, n_plain_end):
    j = pl.program_id(0)
    tn = wt_bf.shape[0]

    def fetch(block):
        rows = pl.ds(pl.multiple_of(first_col + block * tn, 8), tn)
        return pltpu.make_async_copy(wt_hbm.at[rows], stage, sem)

    @pl.when(pl.program_id(1) == 0)
    def _():
        @pl.when(j == 0)
        def _():
            fetch(j).start()

        fetch(j).wait()
        wt_bf[...] = stage[...].astype(BF16)

        @pl.when(j + 1 < pl.num_programs(0))
        def _():
            fetch(j + 1).start()

    acc = _nt_dot(a_ref[...], wt_bf[...])

    @pl.when(j < n_norm_blocks)
    def _():
        for h in range(acc.shape[1] // B_HEAD_DIM):
            sl = slice(h * B_HEAD_DIM, (h + 1) * B_HEAD_DIM)
            z = acc[:, sl]
            r = lax.rsqrt(jnp.mean(z * z, axis=-1, keepdims=True) + EPS)
            o_ref[:, sl] = (z * r * gb_ref[:, sl]).astype(o_ref.dtype)

    @pl.when((j >= n_norm_blocks) & (j < n_plain_end))
    def _():
        o_ref[...] = acc.astype(o_ref.dtype)

    @pl.when(j >= n_plain_end)
    def _():
        o_ref[...] = (1.0 / (1.0 + jnp.exp(-(acc + gb_ref[...])))).astype(o_ref.dtype)


def _inproj_b(xn, w_in_t, first_col, gb, tm, tn, n_norm_blocks, n_plain_end):
    m, k = xn.shape
    n = w_in_t.shape[0] - first_col
    assert n % tn == 0 and first_col % 8 == 0
    blocks = _nbytes((tm, k), BF16) + _nbytes((tm, tn), BF16) + _nbytes((1, tn), F32)
    resident = _nbytes((tn, k), F32) + 2 * _nbytes((tn, k), BF16) + 2 * _nbytes((tm, tn), F32)
    return pl.pallas_call(
        functools.partial(_inproj_b_kernel, first_col=first_col, n_norm_blocks=n_norm_blocks,
                          n_plain_end=n_plain_end),
        grid=(n // tn, m // tm),
        in_specs=[pl.BlockSpec((tm, k), lambda j, i: (i, 0)),
                  pl.BlockSpec(memory_space=pl.ANY),
                  pl.BlockSpec((1, tn), lambda j, i: (0, j))],
        out_specs=pl.BlockSpec((tm, tn), lambda j, i: (i, j)),
        out_shape=jax.ShapeDtypeStruct((m, n), BF16),
        scratch_shapes=[pltpu.VMEM((tn, k), F32), pltpu.VMEM((tn, k), BF16), pltpu.SemaphoreType.DMA(())],
        compiler_params=_params(("arbitrary", "arbitrary"), blocks, resident),
        name="inproj_b",
    )(xn, w_in_t, gb)


def _rope_padded(v, cos, sin_lo, sin_hi):
    half = A_ROPE // 2
    return v * cos + pltpu.roll(v, LANE - half, 1) * sin_lo + pltpu.roll(v, half, 1) * sin_hi


def _mla_proj_kernel(za_ref, cos_ref, sl_ref, sh_ref, wuq_ref, wukv_ref, gq_ref, gkv_ref, gaq_ref, gak_ref,
                     q_ref, k_ref, v_ref, *, heads, q_lora, kv_lora):
    cos, sin_lo, sin_hi = cos_ref[...], sl_ref[...], sh_ref[...]

    def norm(z, g):
        r = lax.rsqrt(jnp.mean(z * z, axis=-1, keepdims=True) + EPS)
        return (z * r * g).astype(BF16)

    cq = norm(za_ref[:, :q_lora], gq_ref[...])
    ckv = norm(za_ref[:, q_lora:q_lora + kv_lora], gkv_ref[...])
    slab = za_ref[:, q_lora + kv_lora:q_lora + kv_lora + LANE]
    k_rope = jnp.where(lax.broadcasted_iota(jnp.int32, slab.shape, 1) < A_ROPE, slab, 0.0)
    qacc = jnp.dot(cq, wuq_ref[...], preferred_element_type=F32)
    kvacc = jnp.dot(ckv, wukv_ref[...], preferred_element_type=F32)

    gq_lo, gq_hi = gaq_ref[:, :LANE], gaq_ref[:, LANE:]
    gk_lo, gk_hi = gak_ref[:, :LANE], gak_ref[:, LANE:]
    kr_ss = jnp.sum(k_rope * k_rope, axis=-1, keepdims=True)
    kr_rot = _rope_padded(k_rope * gk_hi, cos, sin_lo, sin_hi)
    for h in range(heads):
        base = h * A_HEAD_PAD
        q_lo = qacc[:, base:base + LANE]
        q_hi = qacc[:, base + LANE:base + A_HEAD_PAD]
        ss = jnp.sum(q_lo * q_lo, axis=-1, keepdims=True) + jnp.sum(q_hi * q_hi, axis=-1, keepdims=True)
        r = lax.rsqrt(ss / A_QK + EPS)
        q_ref[:, base:base + LANE] = (q_lo * r * gq_lo).astype(BF16)
        q_ref[:, base + LANE:base + A_HEAD_PAD] = _rope_padded(q_hi * r * gq_hi, cos, sin_lo, sin_hi).astype(BF16)

        k_lo = kvacc[:, base:base + LANE]
        ssk = jnp.sum(k_lo * k_lo, axis=-1, keepdims=True) + kr_ss
        rk = lax.rsqrt(ssk / A_QK + EPS)
        k_ref[:, base:base + LANE] = (k_lo * rk * gk_lo).astype(BF16)
        k_ref[:, base + LANE:base + A_HEAD_PAD] = (kr_rot * rk).astype(BF16)
        v_ref[:, h * A_V:(h + 1) * A_V] = kvacc[:, base + LANE:base + A_HEAD_PAD].astype(BF16)


def _mla_proj(za, cos, sin_lo, sin_hi, wuq, wukv, gq, gkv, gaq, gak, heads, tm):
    t, za_cols = za.shape
    q_lora, kv_lora = wuq.shape[0], wukv.shape[0]
    hp = heads * A_HEAD_PAD
    row = lambda i: (i, 0)
    fix = lambda i: (0, 0)
    blocks = (_nbytes((tm, za_cols), F32) + 3 * _nbytes((tm, LANE), F32) + _nbytes(wuq.shape, BF16)
              + _nbytes(wukv.shape, BF16) + 2 * _nbytes((tm, hp), BF16) + _nbytes((tm, heads * A_V), BF16))
    return pl.pallas_call(
        functools.partial(_mla_proj_kernel, heads=heads, q_lora=q_lora, kv_lora=kv_lora),
        grid=(t // tm,),
        in_specs=[pl.BlockSpec((tm, za_cols), row),
                  pl.BlockSpec((tm, LANE), row), pl.BlockSpec((tm, LANE), row), pl.BlockSpec((tm, LANE), row),
                  pl.BlockSpec(wuq.shape, fix), pl.BlockSpec(wukv.shape, fix),
                  pl.BlockSpec((1, q_lora), fix), pl.BlockSpec((1, kv_lora), fix),
                  pl.BlockSpec((1, A_HEAD_PAD), fix), pl.BlockSpec((1, A_HEAD_PAD), fix)],
        out_specs=[pl.BlockSpec((tm, hp), row), pl.BlockSpec((tm, hp), row), pl.BlockSpec((tm, heads * A_V), row)],
        out_shape=[jax.ShapeDtypeStruct((t, hp), BF16), jax.ShapeDtypeStruct((t, hp), BF16),
                   jax.ShapeDtypeStruct((t, heads * A_V), BF16)],
        compiler_params=_params(("parallel",), blocks, 3 * _nbytes((tm, hp), F32)),
        name="mla_proj",
    )(za, cos, sin_lo, sin_hi, wuq, wukv, gq, gkv, gaq, gak)


def _mla_attn_tile(i, q_ref, k_ref, v_ref, o_ref, diag_ok):
    tq = ATTN_TILE
    q = q_ref[i * tq:(i + 1) * tq, :]
    sd = jnp.where(diag_ok, _nt_dot(q, k_ref[i * tq:(i + 1) * tq, :]), NEG_INF)
    m = jnp.max(sd, axis=-1, keepdims=True)
    if i > 0:
        s0 = _nt_dot(q, k_ref[:i * tq, :])
        m = jnp.maximum(m, jnp.max(s0, axis=-1, keepdims=True))
    pd = jnp.exp(sd - m)
    l = jnp.sum(pd, axis=-1, keepdims=True)
    o = jnp.dot(pd.astype(BF16), v_ref[i * tq:(i + 1) * tq, :], preferred_element_type=F32)
    if i > 0:
        p0 = jnp.exp(s0 - m)
        l = l + jnp.sum(p0, axis=-1, keepdims=True)
        o = o + jnp.dot(p0.astype(BF16), v_ref[:i * tq, :], preferred_element_type=F32)
    o_ref[i * tq:(i + 1) * tq, :] = (o / l).astype(o_ref.dtype)


def _band_attn_tile(i, q_ref, k_ref, v_ref, o_ref, table):
    tq = ATTN_TILE
    left = B_LEFT_CHUNKS * CHUNK
    q0 = i * tq
    k0 = max(0, q0 - left)
    kw = q0 + tq - k0
    u0 = left - q0 + k0
    q = q_ref[q0:q0 + tq, :]
    s = _nt_dot(q, k_ref[k0:k0 + kw, :]) + table[:, u0:u0 + kw]
    m = jnp.max(s, axis=-1, keepdims=True)
    p = jnp.exp(s - m)
    l = jnp.sum(p, axis=-1, keepdims=True)
    o = jnp.dot(p.astype(BF16), v_ref[k0:k0 + kw, :], preferred_element_type=F32)
    o_ref[q0:q0 + tq, :] = (o / l).astype(o_ref.dtype)


def _attn_kernel(aq_ref, ak_ref, av_ref, bq_ref, bk_ref, bv_ref, r_ref, oa_ref, ob_ref, *, seq):
    tq = ATTN_TILE
    left = B_LEFT_CHUNKS * CHUNK
    width = r_ref.shape[1]
    rc = lax.broadcasted_iota(jnp.int32, (tq, tq), 0) // CHUNK
    cc = lax.broadcasted_iota(jnp.int32, (tq, tq), 1) // CHUNK
    diag_ok = cc <= rc
    bias = pltpu.roll(jnp.broadcast_to(r_ref[...], (tq, width)), 0, 1, stride=1, stride_axis=0)[:, :left + tq]
    q_chunk = lax.broadcasted_iota(jnp.int32, (tq, left + tq), 0) // CHUNK
    k_chunk = lax.broadcasted_iota(jnp.int32, (tq, left + tq), 1) // CHUNK
    table = jnp.where((k_chunk >= q_chunk) & (k_chunk <= q_chunk + B_LEFT_CHUNKS), bias, NEG_INF)
    for i in range(seq // tq):
        _mla_attn_tile(i, aq_ref, ak_ref, av_ref, oa_ref, diag_ok)
        _band_attn_tile(i, bq_ref, bk_ref, bv_ref, ob_ref, table)


def _band_bias_rows(rel_bias):
    left = B_LEFT_CHUNKS * CHUNK
    width = left + 2 * ATTN_TILE
    m = jnp.arange(width, dtype=jnp.int32)
    j = jnp.where(m < left + ATTN_TILE, m, m - width)
    dist = left - j
    rows = rel_bias[:, jnp.clip(dist, -B_MAX_REL, B_MAX_REL) + B_MAX_REL].astype(F32)
    return rows.reshape(rel_bias.shape[0], 1, width)


def _attention(q, k, v, zb, bias_rows, batch, seq, heads):
    t = batch * seq
    d = B_HEAD_DIM
    width = bias_rows.shape[2]
    head = lambda b, h: (b, h)
    blocks = (2 * _nbytes((seq, A_HEAD_PAD), BF16) + 2 * _nbytes((seq, A_V), BF16) + 4 * _nbytes((seq, d), BF16)
              + _nbytes((1, width), F32))
    temps = 6 * _nbytes((ATTN_TILE, seq), F32) + 8 * _nbytes((ATTN_TILE, width), F32)
    return pl.pallas_call(
        functools.partial(_attn_kernel, seq=seq),
        grid=(batch, heads),
        in_specs=[pl.BlockSpec((seq, A_HEAD_PAD), head), pl.BlockSpec((seq, A_HEAD_PAD), head),
                  pl.BlockSpec((seq, A_V), head),
                  pl.BlockSpec((seq, d), head),
                  pl.BlockSpec((seq, d), lambda b, h: (b, heads + h)),
                  pl.BlockSpec((seq, d), lambda b, h: (b, 2 * heads + h)),
                  pl.BlockSpec((None, 1, width), lambda b, h: (h, 0, 0))],
        out_specs=[pl.BlockSpec((seq, A_V), head), pl.BlockSpec((seq, d), head)],
        out_shape=[jax.ShapeDtypeStruct((t, heads * A_V), BF16), jax.ShapeDtypeStruct((t, heads * d), BF16)],
        compiler_params=_params(("parallel", "parallel"), blocks, temps),
        name="attention",
    )(q, k, v, zb, zb, zb, bias_rows)


def _merge_kernel(oa_ref, ob_ref, woa_ref, wob_ref, g0_ref, g1_ref, o_ref, woa_bf, wob_bf):
    @pl.when(pl.program_id(1) == 0)
    def _():
        woa_bf[...] = woa_ref[...].astype(BF16)
        wob_bf[...] = wob_ref[...].astype(BF16)

    a = jnp.dot(oa_ref[...], woa_bf[...], preferred_element_type=F32)
    b = jnp.dot(ob_ref[...], wob_bf[...], preferred_element_type=F32)
    o_ref[...] = (g0_ref[...].astype(F32) * a + g1_ref[...].astype(F32) * b).astype(o_ref.dtype)


def _merge(oa, ob, woa, wob, zb, gate_col, tm, tn):
    m = oa.shape[0]
    d = woa.shape[1]
    g0, g1 = gate_col // tn, (gate_col + d) // tn
    blocks = (_nbytes((tm, oa.shape[1]), BF16) + _nbytes((tm, ob.shape[1]), BF16) + _nbytes((woa.shape[0], tn), F32)
              + _nbytes((wob.shape[0], tn), F32) + 3 * _nbytes((tm, tn), BF16))
    resident = (2 * _nbytes((woa.shape[0], tn), BF16) + 2 * _nbytes((wob.shape[0], tn), BF16)
                + 3 * _nbytes((tm, tn), F32))
    return pl.pallas_call(
        _merge_kernel,
        grid=(d // tn, m // tm),
        in_specs=[pl.BlockSpec((tm, oa.shape[1]), lambda j, i: (i, 0)),
                  pl.BlockSpec((tm, ob.shape[1]), lambda j, i: (i, 0)),
                  pl.BlockSpec((woa.shape[0], tn), lambda j, i: (0, j)),
                  pl.BlockSpec((wob.shape[0], tn), lambda j, i: (0, j)),
                  pl.BlockSpec((tm, tn), lambda j, i: (i, g0 + j)),
                  pl.BlockSpec((tm, tn), lambda j, i: (i, g1 + j))],
        out_specs=pl.BlockSpec((tm, tn), lambda j, i: (i, j)),
        out_shape=jax.ShapeDtypeStruct((m, d), BF16),
        scratch_shapes=[pltpu.VMEM((woa.shape[0], tn), BF16), pltpu.VMEM((wob.shape[0], tn), BF16)],
        compiler_params=_params(("parallel", "arbitrary"), blocks, resident),
        name="merge",
    )(oa, ob, woa, wob, zb, zb)


def _out_proj_kernel(a_ref, w_ref, x_ref, o_ref, w_bf):
    @pl.when(pl.program_id(1) == 0)
    def _():
        w_bf[...] = w_ref[...].astype(BF16)

    o_ref[...] = x_ref[...] + jnp.dot(a_ref[...], w_bf[...], preferred_element_type=F32)


def _out_proj(merged, wout, x, tm, tn):
    m, k = merged.shape
    n = wout.shape[1]
    blocks = _nbytes((tm, k), BF16) + _nbytes((k, tn), F32) + 2 * _nbytes((tm, tn), F32)
    return pl.pallas_call(
        _out_proj_kernel,
        grid=(n // tn, m // tm),
        in_specs=[pl.BlockSpec((tm, k), lambda j, i: (i, 0)),
                  pl.BlockSpec((k, tn), lambda j, i: (0, j)),
                  pl.BlockSpec((tm, tn), lambda j, i: (i, j))],
        out_specs=pl.BlockSpec((tm, tn), lambda j, i: (i, j)),
        out_shape=jax.ShapeDtypeStruct((m, n), F32),
        scratch_shapes=[pltpu.VMEM((k, tn), BF16)],
        compiler_params=_params(("parallel", "arbitrary"), blocks, 2 * _nbytes((k, tn), BF16) + _nbytes((tm, tn), F32)),
        name="out_proj",
    )(merged, wout, x)


def _router_kernel(x_ref, g_ref, wr_ref, xn_ref, route_ref, *, n_groups, per_group):
    x = x_ref[...]
    r = lax.rsqrt(jnp.mean(x * x, axis=-1, keepdims=True) + EPS)
    xn = x * r * g_ref[...]
    xn_ref[...] = xn
    logits = jnp.dot(xn, wr_ref[...], preferred_element_type=F32, precision=lax.Precision.HIGHEST)
    lane = lax.broadcasted_iota(jnp.int32, logits.shape, 1).astype(F32)
    far = float(LANE)

    def top(vals):
        best = jnp.max(vals, axis=-1, keepdims=True)
        return best, jnp.min(jnp.where(vals == best, lane, far), axis=-1, keepdims=True)

    gl = jnp.where(lane < n_groups, logits, NEG_INF)
    gmax, grp = top(gl)
    p_grp = 1.0 / jnp.sum(jnp.exp(gl - gmax), axis=-1, keepdims=True)
    lo = n_groups + grp * per_group
    el = jnp.where((lane >= lo) & (lane < lo + per_group), logits, NEG_INF)
    t1, i1 = top(el)
    t2, i2 = top(jnp.where(lane == i1, NEG_INF, el))
    d = jnp.exp(t2 - t1)
    w1 = p_grp / (1.0 + d)
    w2 = p_grp * d / (1.0 + d)
    route_ref[...] = jnp.where(lane == 0, i1 - n_groups,
                               jnp.where(lane == 1, i2 - n_groups,
                                         jnp.where(lane == 2, w1, jnp.where(lane == 3, w2, 0.0))))


def _router(x1, g, wr, n_groups, per_group, tm):
    t, d = x1.shape
    blocks = 2 * _nbytes((tm, d), F32) + _nbytes((1, d), F32) + _nbytes((d, LANE), F32) + _nbytes((tm, LANE), F32)
    return pl.pallas_call(
        functools.partial(_router_kernel, n_groups=n_groups, per_group=per_group),
        grid=(t // tm,),
        in_specs=[pl.BlockSpec((tm, d), lambda i: (i, 0)), pl.BlockSpec((1, d), lambda i: (0, 0)),
                  pl.BlockSpec((d, LANE), lambda i: (0, 0))],
        out_specs=[pl.BlockSpec((tm, d), lambda i: (i, 0)), pl.BlockSpec((tm, LANE), lambda i: (i, 0))],
        out_shape=[jax.ShapeDtypeStruct((t, d), F32), jax.ShapeDtypeStruct((t, LANE), F32)],
        compiler_params=_params(("parallel",), blocks, 2 * _nbytes((tm, d), F32)),
        name="router",
    )(x1, g.reshape(1, d), wr)


def _one_hots(route):
    lane = lax.broadcasted_iota(jnp.int32, route.shape, 1).astype(F32)
    return (lane == route[:, 0:1]).astype(F32), (lane == route[:, 1:2]).astype(F32)


def _rank_kernel(route_ref, rank_ref, starts_ref, count_acc, start_acc):
    i = pl.program_id(0)

    @pl.when(i == 0)
    def _():
        count_acc[...] = jnp.zeros_like(count_acc)
        start_acc[...] = jnp.zeros_like(start_acc)

    oh1, oh2 = _one_hots(route_ref[...])
    oh = (oh1 + oh2).astype(BF16)
    tm = oh.shape[0]
    earlier = (lax.broadcasted_iota(jnp.int32, (tm, tm), 0) > lax.broadcasted_iota(jnp.int32, (tm, tm), 1))
    before = jnp.dot(earlier.astype(BF16), oh, preferred_element_type=F32) + count_acc[...]
    lane = lax.broadcasted_iota(jnp.int32, (tm, LANE), 1)
    rank_ref[...] = jnp.where(lane == 0, jnp.sum(oh1 * before, axis=-1, keepdims=True),
                              jnp.where(lane == 1, jnp.sum(oh2 * before, axis=-1, keepdims=True), 0.0))
    lower = (lax.broadcasted_iota(jnp.int32, (LANE, LANE), 0) < lax.broadcasted_iota(jnp.int32, (LANE, LANE), 1))
    below = jnp.dot(oh, lower.astype(BF16), preferred_element_type=F32)
    count_acc[...] += jnp.sum(oh.astype(F32), axis=0, keepdims=True)
    start_acc[...] += jnp.sum(below, axis=0, keepdims=True)
    starts_ref[...] = start_acc[...]


def _rank(route, tm):
    t = route.shape[0]
    blocks = 2 * _nbytes((tm, LANE), F32) + _nbytes((1, LANE), F32)
    return pl.pallas_call(
        _rank_kernel,
        grid=(t // tm,),
        in_specs=[pl.BlockSpec((tm, LANE), lambda i: (i, 0))],
        out_specs=[pl.BlockSpec((tm, LANE), lambda i: (i, 0)), pl.BlockSpec((1, LANE), lambda i: (0, 0))],
        out_shape=[jax.ShapeDtypeStruct((t, LANE), F32), jax.ShapeDtypeStruct((1, LANE), F32)],
        scratch_shapes=[pltpu.VMEM((1, LANE), F32), pltpu.VMEM((1, LANE), F32)],
        compiler_params=_params(("arbitrary",), blocks, 2 * _nbytes((tm, tm), F32)),
        name="moe_rank",
    )(route)


def _dest_kernel(route_ref, rank_ref, starts_ref, dest_ref):
    oh1, oh2 = _one_hots(route_ref[...])
    rank = rank_ref[...]
    starts = starts_ref[...]
    d1 = jnp.sum(oh1 * starts, axis=-1, keepdims=True) + rank[:, 0:1]
    d2 = jnp.sum(oh2 * starts, axis=-1, keepdims=True) + rank[:, 1:2]
    lane = lax.broadcasted_iota(jnp.int32, rank.shape, 1)
    dest_ref[...] = jnp.where(lane == 0, d1, jnp.where(lane == 1, d2, 0.0)).astype(jnp.int32)


def _dest(route, rank, starts, tm):
    t = route.shape[0]
    blocks = 3 * _nbytes((tm, LANE), F32) + _nbytes((1, LANE), F32)
    return pl.pallas_call(
        _dest_kernel,
        grid=(t // tm,),
        in_specs=[pl.BlockSpec((tm, LANE), lambda i: (i, 0)), pl.BlockSpec((tm, LANE), lambda i: (i, 0)),
                  pl.BlockSpec((1, LANE), lambda i: (0, 0))],
        out_specs=pl.BlockSpec((tm, LANE), lambda i: (i, 0)),
        out_shape=jax.ShapeDtypeStruct((t, LANE), jnp.int32),
        compiler_params=_params(("parallel",), blocks, 4 * _nbytes((tm, LANE), F32)),
        name="moe_dest",
    )(route, rank, starts)


def _gather_rows_kernel(dest_ref, x_hbm, xs_ref, source, rows_f32, sem, *, n_assign):
    rows = xs_ref.shape[0]
    base = pl.program_id(0) * rows

    @pl.when(pl.program_id(0) == 0)
    def _():
        def invert(a, carry):
            source[dest_ref[a]] = a
            return carry
        lax.fori_loop(0, n_assign, invert, 0, unroll=8)

    def copy(j):
        return pltpu.make_async_copy(x_hbm.at[pl.ds(source[base + j] // TOP_K, 1)], rows_f32.at[pl.ds(j, 1)], sem)

    def start(j, carry):
        copy(j).start()
        return carry

    def wait(j, carry):
        copy(j).wait()
        return carry

    lax.fori_loop(0, rows, start, 0, unroll=4)
    lax.fori_loop(0, rows, wait, 0, unroll=4)
    xs_ref[...] = rows_f32[...].astype(xs_ref.dtype)


def _gather_rows(dest_flat, xn, rows):
    t, d = xn.shape
    n_assign = dest_flat.shape[0]
    return pl.pallas_call(
        functools.partial(_gather_rows_kernel, n_assign=n_assign),
        grid_spec=pltpu.PrefetchScalarGridSpec(
            num_scalar_prefetch=1,
            grid=(n_assign // rows,),
            in_specs=[pl.BlockSpec(memory_space=pl.ANY)],
            out_specs=pl.BlockSpec((rows, d), lambda i, dest: (i, 0)),
            scratch_shapes=[pltpu.SMEM((n_assign,), jnp.int32), pltpu.VMEM((rows, d), xn.dtype),
                            pltpu.SemaphoreType.DMA(())]),
        out_shape=jax.ShapeDtypeStruct((n_assign, d), BF16),
        compiler_params=_params(("arbitrary",), _nbytes((rows, d), BF16), 2 * _nbytes((rows, d), xn.dtype)),
        name="moe_gather_rows",
    )(dest_flat, xn)


def _work_items(starts, n_rows):
    n_exp = starts.shape[0]
    n_blk = n_rows // MOE_ROWS
    total = jnp.full((1,), n_rows, jnp.int32)
    pts = jnp.concatenate([jnp.arange(n_blk, dtype=jnp.int32) * MOE_ROWS, starts[1:]])
    idx = jnp.arange(pts.shape[0], dtype=jnp.int32)
    before = (pts[None, :] < pts[:, None]) | ((pts[None, :] == pts[:, None]) & (idx[None, :] < idx[:, None]))
    pos = jnp.sum(before.astype(jnp.int32), axis=1)
    lo = jnp.sum(jnp.where(pos[:, None] == idx[None, :], pts[:, None], 0), axis=0)
    hi = jnp.concatenate([lo[1:], total])
    ends = jnp.concatenate([starts[1:], total])
    r = jnp.minimum(lo // MOE_ROWS, n_blk - 1)
    e = jnp.minimum(jnp.sum((ends[None, :] <= lo[:, None]).astype(jnp.int32), axis=1), n_exp - 1)
    changed = jnp.concatenate([jnp.ones((1,), jnp.int32), (e[1:] != e[:-1]).astype(jnp.int32)])
    slot = (jnp.sum(jnp.where(idx[None, :] <= idx[:, None], changed[None, :], 0), axis=1) - 1) % 2
    later = jnp.where(e[None, :] > e[:, None], e[None, :], n_exp)
    nxt = jnp.min(later, axis=1)
    nxt = jnp.where(nxt == n_exp, -1, nxt)
    return r, e, lo, hi, slot, nxt


def _stream_expert_weights(w, e_ref, slot_ref, nxt_ref, streams, convert):
    e = e_ref[w]
    slot = slot_ref[w]

    def copies(expert, into):
        return [pltpu.make_async_copy(hbm.at[expert], stage.at[into], sem.at[into]) for hbm, stage, sem in streams]

    @pl.when(w == 0)
    def _():
        for c in copies(e, slot):
            c.start()

    @pl.when((w == 0) | (e != e_ref[jnp.maximum(w - 1, 0)]))
    def _():
        for c in copies(e, slot):
            c.wait()
        nxt = nxt_ref[w]

        @pl.when(nxt >= 0)
        def _():
            for c in copies(nxt, 1 - slot):
                c.start()

        convert(slot)


def _store_item_rows(o_ref, val, r, lo, hi):
    rows = r * MOE_ROWS + lax.broadcasted_iota(jnp.int32, (MOE_ROWS, 1), 0)
    mine = (rows >= lo) & (rows < hi)

    @pl.when(lo == r * MOE_ROWS)
    def _():
        o_ref[...] = val

    @pl.when(lo != r * MOE_ROWS)
    def _():
        o_ref[...] = jnp.where(mine, val, o_ref[...])


def _moe_up_kernel(r_ref, e_ref, lo_ref, hi_ref, slot_ref, nxt_ref, xs_ref, wg_hbm, wu_hbm, h_ref,
                   wg_stage, wu_stage, wg_bf, wu_bf, sem_g, sem_u):
    w = pl.program_id(0)

    def convert(slot):
        wg_bf[...] = wg_stage[slot].astype(BF16)
        wu_bf[...] = wu_stage[slot].astype(BF16)

    _stream_expert_weights(w, e_ref, slot_ref, nxt_ref,
                           [(wg_hbm, wg_stage, sem_g), (wu_hbm, wu_stage, sem_u)], convert)
    r, lo, hi = r_ref[w], lo_ref[w], hi_ref[w]

    @pl.when(hi > lo)
    def _():
        x = xs_ref[...]
        g = jnp.dot(x, wg_bf[...], preferred_element_type=F32)
        u = jnp.dot(x, wu_bf[...], preferred_element_type=F32)
        h = (g * (1.0 / (1.0 + jnp.exp(-g)))) * u
        _store_item_rows(h_ref, h.astype(h_ref.dtype), r, lo, hi)


def _moe_up(items, xs, wg, wu):
    n_rows, d = xs.shape
    _, _, f = wg.shape
    n_items = items[0].shape[0]
    blocks = _nbytes((MOE_ROWS, d), BF16) + _nbytes((MOE_ROWS, f), BF16)
    resident = 4 * _nbytes((d, f), F32) + 2 * _nbytes((d, f), BF16) + 6 * _nbytes((MOE_ROWS, f), F32)
    row_block = lambda w, r, e, lo, hi, slot, nxt: (r[w], 0)
    return pl.pallas_call(
        _moe_up_kernel,
        grid_spec=pltpu.PrefetchScalarGridSpec(
            num_scalar_prefetch=6,
            grid=(n_items,),
            in_specs=[pl.BlockSpec((MOE_ROWS, d), row_block),
                      pl.BlockSpec(memory_space=pl.ANY), pl.BlockSpec(memory_space=pl.ANY)],
            out_specs=pl.BlockSpec((MOE_ROWS, f), row_block),
            scratch_shapes=[pltpu.VMEM((2, d, f), F32), pltpu.VMEM((2, d, f), F32),
                            pltpu.VMEM((d, f), BF16), pltpu.VMEM((d, f), BF16),
                            pltpu.SemaphoreType.DMA((2,)), pltpu.SemaphoreType.DMA((2,))]),
        out_shape=jax.ShapeDtypeStruct((n_rows, f), BF16),
        compiler_params=_params(("arbitrary",), blocks, resident),
        name="moe_up",
    )(*items, xs, wg, wu)


def _moe_down_kernel(r_ref, e_ref, lo_ref, hi_ref, slot_ref, nxt_ref, h_ref, wd_hbm, y_ref, wd_stage, wd_bf, sem):
    w = pl.program_id(0)

    def convert(slot):
        wd_bf[...] = wd_stage[slot].astype(BF16)

    _stream_expert_weights(w, e_ref, slot_ref, nxt_ref, [(wd_hbm, wd_stage, sem)], convert)
    r, lo, hi = r_ref[w], lo_ref[w], hi_ref[w]

    @pl.when(hi > lo)
    def _():
        y = jnp.dot(h_ref[...], wd_bf[...], preferred_element_type=F32)
        _store_item_rows(y_ref, y, r, lo, hi)


def _moe_down(items, h, wd):
    n_rows, f = h.shape
    d = wd.shape[2]
    n_items = items[0].shape[0]
    blocks = _nbytes((MOE_ROWS, f), BF16) + _nbytes((MOE_ROWS, d), F32)
    resident = 2 * _nbytes((f, d), F32) + _nbytes((f, d), BF16) + 3 * _nbytes((MOE_ROWS, d), F32)
    row_block = lambda w, r, e, lo, hi, slot, nxt: (r[w], 0)
    return pl.pallas_call(
        _moe_down_kernel,
        grid_spec=pltpu.PrefetchScalarGridSpec(
            num_scalar_prefetch=6,
            grid=(n_items,),
            in_specs=[pl.BlockSpec((MOE_ROWS, f), row_block), pl.BlockSpec(memory_space=pl.ANY)],
            out_specs=pl.BlockSpec((MOE_ROWS, d), row_block),
            scratch_shapes=[pltpu.VMEM((2, f, d), F32), pltpu.VMEM((f, d), BF16), pltpu.SemaphoreType.DMA((2,))]),
        out_shape=jax.ShapeDtypeStruct((n_rows, d), F32),
        compiler_params=_params(("arbitrary",), blocks, resident),
        name="moe_down",
    )(*items, h, wd)


def _combine_kernel(dest_ref, x_ref, route_ref, y_hbm, o_ref, ybuf, sem, *, tm):
    i = pl.program_id(0)

    def copies(tile, t):
        slot = tile % 2
        return [pltpu.make_async_copy(y_hbm.at[pl.ds(dest_ref[TOP_K * (tile * tm + t) + k], 1)],
                                      ybuf.at[slot, k, pl.ds(t, 1)], sem.at[slot]) for k in range(TOP_K)]

    def start_tile(tile):
        def start(t, carry):
            for c in copies(tile, t):
                c.start()
            return carry
        lax.fori_loop(0, tm, start, 0, unroll=2)

    @pl.when(i == 0)
    def _():
        start_tile(i)

    @pl.when(i + 1 < pl.num_programs(0))
    def _():
        start_tile(i + 1)

    def wait(t, carry):
        for c in copies(i, t):
            c.wait()
        return carry

    lax.fori_loop(0, tm, wait, 0, unroll=2)
    route = route_ref[...]
    rows = ybuf[i % 2]
    o_ref[...] = x_ref[...] + (route[:, 2:3] * rows[0] + route[:, 3:4] * rows[1])


def _combine(dest_flat, x1, route, y, tm):
    t, d = x1.shape
    blocks = 2 * _nbytes((tm, d), F32) + _nbytes((tm, LANE), F32)
    resident = 2 * TOP_K * _nbytes((tm, d), F32) + _nbytes((tm, d), F32)
    return pl.pallas_call(
        functools.partial(_combine_kernel, tm=tm),
        grid_spec=pltpu.PrefetchScalarGridSpec(
            num_scalar_prefetch=1,
            grid=(t // tm,),
            in_specs=[pl.BlockSpec((tm, d), lambda i, dest: (i, 0)),
                      pl.BlockSpec((tm, LANE), lambda i, dest: (i, 0)),
                      pl.BlockSpec(memory_space=pl.ANY)],
            out_specs=pl.BlockSpec((tm, d), lambda i, dest: (i, 0)),
            scratch_shapes=[pltpu.VMEM((2, TOP_K, tm, d), F32), pltpu.SemaphoreType.DMA((2,))]),
        out_shape=jax.ShapeDtypeStruct((t, d), F32),
        compiler_params=_params(("arbitrary",), blocks, resident),
        name="moe_combine",
    )(dest_flat, x1, route, y)


def _pad_cols(w, n):
    return jnp.pad(w, ((0, 0), (0, n - w.shape[1])))


def kernel(x, positions, g_mix, w_in, b_gate, q_norm_g, kv_norm_g, w_uq, w_ukv, a_q_norm_g, a_k_norm_g,
           b_q_norm_g, b_k_norm_g, rel_bias, w_o_a, w_o_b, w_out, g_ffn, w_group, w_expert,
           w_exp_gate, w_exp_up, w_exp_down):
    batch, seq, d = x.shape
    t = batch * seq
    q_lora, kv_lora = q_norm_g.shape[0], kv_norm_g.shape[0]
    a_heads = w_uq.shape[1] // A_QK
    b_heads = w_o_b.shape[0] // B_HEAD_DIM
    b_width = b_heads * B_HEAD_DIM
    n_groups, n_experts = w_group.shape[1], w_expert.shape[1]
    per_group = n_experts // n_groups
    off_b = q_lora + kv_lora + A_ROPE
    assert seq % ATTN_TILE == 0 and (TOP_K * t) % GATHER_ROWS == 0 and n_groups + n_experts <= LANE
    assert a_heads == b_heads

    xf = x.reshape(t, d)
    tm_big = min(1024, t)
    tn = _tile(b_width, 512)
    assert d % tn == 0

    za_cols = -(-(q_lora + kv_lora + LANE) // tn) * tn
    wuq = jnp.pad(w_uq.reshape(q_lora, a_heads, A_QK), ((0, 0), (0, 0), (0, A_HEAD_PAD - A_QK)))
    wuq = wuq.reshape(q_lora, a_heads * A_HEAD_PAD).astype(BF16)
    wukv = w_ukv.astype(BF16)
    pad_gain = lambda g, s: jnp.pad(g * s, (0, A_HEAD_PAD - A_QK)).reshape(1, A_HEAD_PAD)
    gaq = pad_gain(a_q_norm_g, A_QK ** -0.5)
    gak = pad_gain(a_k_norm_g, 1.0)
    gb = jnp.concatenate([jnp.tile(b_q_norm_g * B_HEAD_DIM ** -0.5, b_heads), jnp.tile(b_k_norm_g, b_heads),
                          jnp.ones((b_width,), F32), b_gate]).reshape(1, -1)

    half = A_ROPE // 2
    inv = ROPE_THETA ** (-jnp.arange(half, dtype=F32) / half)
    ang = positions.astype(F32).reshape(t, 1) * inv
    cos, sin = jnp.cos(ang), jnp.sin(ang)
    zeros = jnp.zeros((t, half), F32)
    cos_t = jnp.concatenate([cos, cos, zeros, zeros], axis=1)
    sin_lo = jnp.concatenate([-sin, zeros, zeros, zeros], axis=1)
    sin_hi = jnp.concatenate([zeros, sin, zeros, zeros], axis=1)

    xn = _rmsnorm_rows(xf, g_mix, BF16, min(256, t))
    w_in_t = w_in.T
    za = _inproj_a(xn, w_in_t, za_cols, tm_big, tn)
    zb = _inproj_b(xn, w_in_t, off_b, gb, tm_big, tn, 2 * b_width // tn, 3 * b_width // tn)
    q, k, v = _mla_proj(za, cos_t, sin_lo, sin_hi, wuq, wukv, q_norm_g.reshape(1, -1), kv_norm_g.reshape(1, -1),
                        gaq, gak, a_heads, min(256, t))
    o_a, o_b = _attention(q, k, v, zb, _band_bias_rows(rel_bias), batch, seq, a_heads)
    merged = _merge(o_a, o_b, w_o_a, w_o_b, zb, 3 * b_width, tm_big, tn)
    x1 = _out_proj(merged, w_out, xf, tm_big, tn)

    wr = _pad_cols(jnp.concatenate([w_group, w_expert], axis=1), LANE)
    xn2, route = _router(x1, g_ffn, wr, n_groups, per_group, min(256, t))
    rank, starts_f = _rank(route, min(512, t))
    dest = _dest(route, rank, starts_f, min(512, t))[:, :TOP_K].reshape(-1)
    xs = _gather_rows(dest, xn2, GATHER_ROWS)
    items = _work_items(starts_f[0, :n_experts].astype(jnp.int32), TOP_K * t)
    h = _moe_up(items, xs, w_exp_gate, w_exp_up)
    y = _moe_down(items, h, w_exp_down)
    out = _combine(dest, x1, route, y, min(256, t))
    return out.reshape(batch, seq, d)
```

```python
import functools

import jax
import jax.numpy as jnp
from jax import lax
from jax.experimental import pallas as pl
from jax.experimental.pallas import tpu as pltpu

F32 = jnp.float32
BF16 = jnp.bfloat16

CHUNK = 64
EPS = 1e-6
A_NOPE = 128
A_ROPE = 64
A_V = 128
A_QK = A_NOPE + A_ROPE
B_HEAD_DIM = 128
B_LEFT_CHUNKS = 8
B_MAX_REL = 128
ROPE_THETA = 10000.0
TOP_K = 2

LANE = 128
A_HEAD_PAD = 2 * LANE
V_PAD = 2 * LANE
V7X_VMEM_BYTES = 64 * 2**20

ATTN_TILE = 256
MOE_ROWS = 128
GATHER_ROWS = 256
WEIGHT_DMA_PARTS = 4
NEG_INF = float("-inf")


def _nbytes(shape, dtype):
    n = 1
    for s in shape:
        n *= s
    return n * jnp.dtype(dtype).itemsize


def _params(semantics, pipelined_bytes, resident_bytes=0):
    need = 2 * pipelined_bytes + resident_bytes
    return pltpu.CompilerParams(dimension_semantics=semantics,
                                vmem_limit_bytes=min(int(need), V7X_VMEM_BYTES))


def _tile(n, want):
    t = want
    while t > LANE and n % t:
        t //= 2
    assert n % t == 0, (n, want)
    return t


def _rmsnorm_kernel(x_ref, g_ref, o_ref):
    x = x_ref[...]
    r = lax.rsqrt(jnp.mean(x * x, axis=-1, keepdims=True) + EPS)
    o_ref[...] = (x * r * g_ref[...]).astype(o_ref.dtype)


def _rmsnorm_rows(x, g, out_dtype, tm):
    t, d = x.shape
    blocks = _nbytes((tm, d), F32) + _nbytes((tm, d), out_dtype) + _nbytes((1, d), F32)
    return pl.pallas_call(
        _rmsnorm_kernel,
        grid=(t // tm,),
        in_specs=[pl.BlockSpec((tm, d), lambda i: (i, 0)), pl.BlockSpec((1, d), lambda i: (0, 0))],
        out_specs=pl.BlockSpec((tm, d), lambda i: (i, 0)),
        out_shape=jax.ShapeDtypeStruct((t, d), out_dtype),
        compiler_params=_params(("parallel",), blocks, _nbytes((tm, d), F32)),
        name="rmsnorm",
    )(x, g.reshape(1, d))


def _nt_dot(a, b):
    return lax.dot_general(a, b, (((1,), (1,)), ((), ())), preferred_element_type=F32)


def _inproj_a_kernel(a_ref, wt_ref, o_ref, wt_bf):
    @pl.when(pl.program_id(1) == 0)
    def _():
        wt_bf[...] = wt_ref[...].astype(BF16)

    o_ref[...] = _nt_dot(a_ref[...], wt_bf[...])


def _inproj_a(xn, w_in_t, n_cols, tm, tn):
    m, k = xn.shape
    blocks = _nbytes((tm, k), BF16) + _nbytes((tn, k), F32) + _nbytes((tm, tn), F32)
    return pl.pallas_call(
        _inproj_a_kernel,
        grid=(n_cols // tn, m // tm),
        in_specs=[pl.BlockSpec((tm, k), lambda j, i: (i, 0)), pl.BlockSpec((tn, k), lambda j, i: (j, 0))],
        out_specs=pl.BlockSpec((tm, tn), lambda j, i: (i, j)),
        out_shape=jax.ShapeDtypeStruct((m, n_cols), F32),
        scratch_shapes=[pltpu.VMEM((tn, k), BF16)],
        compiler_params=_params(("parallel", "arbitrary"), blocks, 2 * _nbytes((tn, k), BF16)),
        name="inproj_a",
    )(xn, w_in_t)


def _inproj_b_kernel(a_ref, wt_hbm, gb_ref, o_ref, stage, wt_bf, sem, *, first_col, n_norm_blocks, n_plain_end):
    j = pl.program_id(0)
    tn = wt_bf.shape[0]

    def fetch(block):
        rows = pl.ds(pl.multiple_of(first_col + block * tn, 8), tn)
        return pltpu.make_async_copy(wt_hbm.at[rows], stage, sem)

    @pl.when(pl.program_id(1) == 0)
    def _():
        @pl.when(j == 0)
        def _():
            fetch(j).start()

        fetch(j).wait()
        wt_bf[...] = stage[...].astype(BF16)

        @pl.when(j + 1 < pl.num_programs(0))
        def _():
            fetch(j + 1).start()

    acc = _nt_dot(a_ref[...], wt_bf[...])

    @pl.when(j < n_norm_blocks)
    def _():
        for h in range(acc.shape[1] // B_HEAD_DIM):
            sl = slice(h * B_HEAD_DIM, (h + 1) * B_HEAD_DIM)
            z = acc[:, sl]
            r = lax.rsqrt(jnp.mean(z * z, axis=-1, keepdims=True) + EPS)
            o_ref[:, sl] = (z * r * gb_ref[:, sl]).astype(o_ref.dtype)

    @pl.when((j >= n_norm_blocks) & (j < n_plain_end))
    def _():
        o_ref[...] = acc.astype(o_ref.dtype)

    @pl.when(j >= n_plain_end)
    def _():
        o_ref[...] = (1.0 / (1.0 + jnp.exp(-(acc + gb_ref[...])))).astype(o_ref.dtype)


def _inproj_b(xn, w_in_t, first_col, gb, tm, tn, n_norm_blocks, n_plain_end):
    m, k = xn.shape
    n = w_in_t.shape[0] - first_col
    assert n % tn == 0 and first_col % 8 == 0
    blocks = _nbytes((tm, k), BF16) + _nbytes((tm, tn), BF16) + _nbytes((1, tn), F32)
    resident = _nbytes((tn, k), F32) + 2 * _nbytes((tn, k), BF16) + 2 * _nbytes((tm, tn), F32)
    return pl.pallas_call(
        functools.partial(_inproj_b_kernel, first_col=first_col, n_norm_blocks=n_norm_blocks,
                          n_plain_end=n_plain_end),
        grid=(n // tn, m // tm),
        in_specs=[pl.BlockSpec((tm, k), lambda j, i: (i, 0)),
                  pl.BlockSpec(memory_space=pl.ANY),
                  pl.BlockSpec((1, tn), lambda j, i: (0, j))],
        out_specs=pl.BlockSpec((tm, tn), lambda j, i: (i, j)),
        out_shape=jax.ShapeDtypeStruct((m, n), BF16),
        scratch_shapes=[pltpu.VMEM((tn, k), F32), pltpu.VMEM((tn, k), BF16), pltpu.SemaphoreType.DMA(())],
        compiler_params=_params(("arbitrary", "arbitrary"), blocks, resident),
        name="inproj_b",
    )(xn, w_in_t, gb)


def _rope_padded(v, cos, sin_lo, sin_hi):
    half = A_ROPE // 2
    return v * cos + pltpu.roll(v, LANE - half, 1) * sin_lo + pltpu.roll(v, half, 1) * sin_hi


def _ones_column(rows):
    lane = lax.broadcasted_iota(jnp.int32, (rows, V_PAD - A_V), 1)
    return jnp.where(lane == 0, 1.0, 0.0).astype(BF16)


def _mla_proj_kernel(za_ref, cos_ref, sl_ref, sh_ref, wuq_ref, wukv_ref, gq_ref, gkv_ref, gaq_ref, gak_ref,
                     q_ref, k_ref, v_ref, *, heads, q_lora, kv_lora):
    cos, sin_lo, sin_hi = cos_ref[...], sl_ref[...], sh_ref[...]

    def norm(z, g):
        r = lax.rsqrt(jnp.mean(z * z, axis=-1, keepdims=True) + EPS)
        return (z * r * g).astype(BF16)

    cq = norm(za_ref[:, :q_lora], gq_ref[...])
    ckv = norm(za_ref[:, q_lora:q_lora + kv_lora], gkv_ref[...])
    slab = za_ref[:, q_lora + kv_lora:q_lora + kv_lora + LANE]
    k_rope = jnp.where(lax.broadcasted_iota(jnp.int32, slab.shape, 1) < A_ROPE, slab, 0.0)
    qacc = jnp.dot(cq, wuq_ref[...], preferred_element_type=F32)
    kvacc = jnp.dot(ckv, wukv_ref[...], preferred_element_type=F32)

    gq_lo, gq_hi = gaq_ref[:, :LANE], gaq_ref[:, LANE:]
    gk_lo, gk_hi = gak_ref[:, :LANE], gak_ref[:, LANE:]
    kr_ss = jnp.sum(k_rope * k_rope, axis=-1, keepdims=True)
    kr_rot = _rope_padded(k_rope * gk_hi, cos, sin_lo, sin_hi)
    ones_col = _ones_column(slab.shape[0])
    for h in range(heads):
        base = h * A_HEAD_PAD
        q_lo = qacc[:, base:base + LANE]
        q_hi = qacc[:, base + LANE:base + A_HEAD_PAD]
        ss = jnp.sum(q_lo * q_lo, axis=-1, keepdims=True) + jnp.sum(q_hi * q_hi, axis=-1, keepdims=True)
        r = lax.rsqrt(ss / A_QK + EPS)
        q_ref[:, base:base + LANE] = (q_lo * r * gq_lo).astype(BF16)
        q_ref[:, base + LANE:base + A_HEAD_PAD] = _rope_padded(q_hi * r * gq_hi, cos, sin_lo, sin_hi).astype(BF16)

        k_lo = kvacc[:, base:base + LANE]
        ssk = jnp.sum(k_lo * k_lo, axis=-1, keepdims=True) + kr_ss
        rk = lax.rsqrt(ssk / A_QK + EPS)
        k_ref[:, base:base + LANE] = (k_lo * rk * gk_lo).astype(BF16)
        k_ref[:, base + LANE:base + A_HEAD_PAD] = (kr_rot * rk).astype(BF16)
        v_ref[:, h * V_PAD:h * V_PAD + A_V] = kvacc[:, base + LANE:base + A_HEAD_PAD].astype(BF16)
        v_ref[:, h * V_PAD + A_V:(h + 1) * V_PAD] = ones_col


def _mla_proj(za, cos, sin_lo, sin_hi, wuq, wukv, gq, gkv, gaq, gak, heads, tm):
    t, za_cols = za.shape
    q_lora, kv_lora = wuq.shape[0], wukv.shape[0]
    hp = heads * A_HEAD_PAD
    row = lambda i: (i, 0)
    fix = lambda i: (0, 0)
    blocks = (_nbytes((tm, za_cols), F32) + 3 * _nbytes((tm, LANE), F32) + _nbytes(wuq.shape, BF16)
              + _nbytes(wukv.shape, BF16) + 2 * _nbytes((tm, hp), BF16) + _nbytes((tm, heads * V_PAD), BF16))
    return pl.pallas_call(
        functools.partial(_mla_proj_kernel, heads=heads, q_lora=q_lora, kv_lora=kv_lora),
        grid=(t // tm,),
        in_specs=[pl.BlockSpec((tm, za_cols), row),
                  pl.BlockSpec((tm, LANE), row), pl.BlockSpec((tm, LANE), row), pl.BlockSpec((tm, LANE), row),
                  pl.BlockSpec(wuq.shape, fix), pl.BlockSpec(wukv.shape, fix),
                  pl.BlockSpec((1, q_lora), fix), pl.BlockSpec((1, kv_lora), fix),
                  pl.BlockSpec((1, A_HEAD_PAD), fix), pl.BlockSpec((1, A_HEAD_PAD), fix)],
        out_specs=[pl.BlockSpec((tm, hp), row), pl.BlockSpec((tm, hp), row), pl.BlockSpec((tm, heads * V_PAD), row)],
        out_shape=[jax.ShapeDtypeStruct((t, hp), BF16), jax.ShapeDtypeStruct((t, hp), BF16),
                   jax.ShapeDtypeStruct((t, heads * V_PAD), BF16)],
        compiler_params=_params(("parallel",), blocks, 3 * _nbytes((tm, hp), F32)),
        name="mla_proj",
    )(za, cos, sin_lo, sin_hi, wuq, wukv, gq, gkv, gaq, gak)


def _mla_scores(i, q_ref, k_ref, diag_ok):
    tq = ATTN_TILE
    q = q_ref[i * tq:(i + 1) * tq, :]
    sd = jnp.where(diag_ok, _nt_dot(q, k_ref[i * tq:(i + 1) * tq, :]), NEG_INF)
    s0 = _nt_dot(q, k_ref[:i * tq, :]) if i > 0 else None
    return sd, s0


def _mla_softmax(sd, s0):
    m = jnp.max(sd, axis=-1, keepdims=True)
    if s0 is None:
        return jnp.exp(sd - m).astype(BF16), None
    m = jnp.maximum(m, jnp.max(s0, axis=-1, keepdims=True))
    return jnp.exp(sd - m).astype(BF16), jnp.exp(s0 - m).astype(BF16)


def _mla_values(i, pd, p0, v_ref, o_ref):
    tq = ATTN_TILE
    o = jnp.dot(pd, v_ref[i * tq:(i + 1) * tq, :], preferred_element_type=F32)
    if p0 is not None:
        o = o + jnp.dot(p0, v_ref[:i * tq, :], preferred_element_type=F32)
    o_ref[i * tq:(i + 1) * tq, :] = (o[:, :A_V] / o[:, A_V:A_V + 1]).astype(o_ref.dtype)


def _band_window(i):
    left = B_LEFT_CHUNKS * CHUNK
    q0 = i * ATTN_TILE
    k0 = max(0, q0 - left)
    return q0, k0, q0 + ATTN_TILE - k0, left - q0 + k0


def _band_scores(i, q_ref, k_ref, table):
    q0, k0, kw, u0 = _band_window(i)
    return _nt_dot(q_ref[q0:q0 + ATTN_TILE, :], k_ref[k0:k0 + kw, :]) + table[:, u0:u0 + kw]


def _band_softmax(s):
    return jnp.exp(s - jnp.max(s, axis=-1, keepdims=True)).astype(BF16)


def _band_values(i, p, v_ref, o_ref):
    q0, k0, kw, _ = _band_window(i)
    o = jnp.dot(p, v_ref[k0:k0 + kw, :], preferred_element_type=F32)
    o_ref[q0:q0 + ATTN_TILE, :] = (o[:, :B_HEAD_DIM] / o[:, B_HEAD_DIM:B_HEAD_DIM + 1]).astype(o_ref.dtype)


def _attn_kernel(aq_ref, ak_ref, av_ref, bq_ref, bk_ref, bv_ref, r_ref, oa_ref, ob_ref, bv_pad, *, seq):
    tq = ATTN_TILE
    left = B_LEFT_CHUNKS * CHUNK
    width = r_ref.shape[1]
    rc = lax.broadcasted_iota(jnp.int32, (tq, tq), 0) // CHUNK
    cc = lax.broadcasted_iota(jnp.int32, (tq, tq), 1) // CHUNK
    diag_ok = cc <= rc
    bias = pltpu.roll(jnp.broadcast_to(r_ref[...], (tq, width)), 0, 1, stride=1, stride_axis=0)[:, :left + tq]
    q_chunk = lax.broadcasted_iota(jnp.int32, (tq, left + tq), 0) // CHUNK
    k_chunk = lax.broadcasted_iota(jnp.int32, (tq, left + tq), 1) // CHUNK
    table = jnp.where((k_chunk >= q_chunk) & (k_chunk <= q_chunk + B_LEFT_CHUNKS), bias, NEG_INF)
    bv_pad[:, :B_HEAD_DIM] = bv_ref[...]
    bv_pad[:, B_HEAD_DIM:] = _ones_column(seq)

    n_tiles = seq // tq
    sd, s0 = _mla_scores(0, aq_ref, ak_ref, diag_ok)
    for i in range(n_tiles):
        sb = _band_scores(i, bq_ref, bk_ref, table)
        pd, p0 = _mla_softmax(sd, s0)
        if i + 1 < n_tiles:
            sd, s0 = _mla_scores(i + 1, aq_ref, ak_ref, diag_ok)
        _mla_values(i, pd, p0, av_ref, oa_ref)
        _band_values(i, _band_softmax(sb), bv_pad, ob_ref)


def _band_bias_rows(rel_bias):
    left = B_LEFT_CHUNKS * CHUNK
    width = left + 2 * ATTN_TILE
    m = jnp.arange(width, dtype=jnp.int32)
    j = jnp.where(m < left + ATTN_TILE, m, m - width)
    dist = left - j
    rows = rel_bias[:, jnp.clip(dist, -B_MAX_REL, B_MAX_REL) + B_MAX_REL].astype(F32)
    return rows.reshape(rel_bias.shape[0], 1, width)


def _attention(q, k, v, zb, bias_rows, batch, seq, heads):
    t = batch * seq
    d = B_HEAD_DIM
    width = bias_rows.shape[2]
    head = lambda b, h: (b, h)
    blocks = (2 * _nbytes((seq, A_HEAD_PAD), BF16) + _nbytes((seq, V_PAD), BF16) + _nbytes((seq, A_V), BF16)
              + 4 * _nbytes((seq, d), BF16) + _nbytes((1, width), F32))
    temps = _nbytes((seq, V_PAD), BF16) + 8 * _nbytes((ATTN_TILE, seq), F32) + 8 * _nbytes((ATTN_TILE, width), F32)
    return pl.pallas_call(
        functools.partial(_attn_kernel, seq=seq),
        grid=(batch, heads),
        in_specs=[pl.BlockSpec((seq, A_HEAD_PAD), head), pl.BlockSpec((seq, A_HEAD_PAD), head),
                  pl.BlockSpec((seq, V_PAD), head),
                  pl.BlockSpec((seq, d), head),
                  pl.BlockSpec((seq, d), lambda b, h: (b, heads + h)),
                  pl.BlockSpec((seq, d), lambda b, h: (b, 2 * heads + h)),
                  pl.BlockSpec((None, 1, width), lambda b, h: (h, 0, 0))],
        out_specs=[pl.BlockSpec((seq, A_V), head), pl.BlockSpec((seq, d), head)],
        out_shape=[jax.ShapeDtypeStruct((t, heads * A_V), BF16), jax.ShapeDtypeStruct((t, heads * d), BF16)],
        scratch_shapes=[pltpu.VMEM((seq, V_PAD), BF16)],
        compiler_params=_params(("parallel", "parallel"), blocks, temps),
        name="attention",
    )(q, k, v, zb, zb, zb, bias_rows)


def _merge_kernel(oa_ref, ob_ref, woa_ref, wob_ref, g0_ref, g1_ref, o_ref, woa_bf, wob_bf):
    @pl.when(pl.program_id(1) == 0)
    def _():
        woa_bf[...] = woa_ref[...].astype(BF16)
        wob_bf[...] = wob_ref[...].astype(BF16)

    a = jnp.dot(oa_ref[...], woa_bf[...], preferred_element_type=F32)
    b = jnp.dot(ob_ref[...], wob_bf[...], preferred_element_type=F32)
    o_ref[...] = (g0_ref[...].astype(F32) * a + g1_ref[...].astype(F32) * b).astype(o_ref.dtype)


def _merge(oa, ob, woa, wob, zb, gate_col, tm, tn):
    m = oa.shape[0]
    d = woa.shape[1]
    g0, g1 = gate_col // tn, (gate_col + d) // tn
    blocks = (_nbytes((tm, oa.shape[1]), BF16) + _nbytes((tm, ob.shape[1]), BF16) + _nbytes((woa.shape[0], tn), F32)
              + _nbytes((wob.shape[0], tn), F32) + 3 * _nbytes((tm, tn), BF16))
    resident = (2 * _nbytes((woa.shape[0], tn), BF16) + 2 * _nbytes((wob.shape[0], tn), BF16)
                + 3 * _nbytes((tm, tn), F32))
    return pl.pallas_call(
        _merge_kernel,
        grid=(d // tn, m // tm),
        in_specs=[pl.BlockSpec((tm, oa.shape[1]), lambda j, i: (i, 0)),
                  pl.BlockSpec((tm, ob.shape[1]), lambda j, i: (i, 0)),
                  pl.BlockSpec((woa.shape[0], tn), lambda j, i: (0, j)),
                  pl.BlockSpec((wob.shape[0], tn), lambda j, i: (0, j)),
                  pl.BlockSpec((tm, tn), lambda j, i: (i, g0 + j)),
                  pl.BlockSpec((tm, tn), lambda j, i: (i, g1 + j))],
        out_specs=pl.BlockSpec((tm, tn), lambda j, i: (i, j)),
        out_shape=jax.ShapeDtypeStruct((m, d), BF16),
        scratch_shapes=[pltpu.VMEM((woa.shape[0], tn), BF16), pltpu.VMEM((wob.shape[0], tn), BF16)],
        compiler_params=_params(("parallel", "arbitrary"), blocks, resident),
        name="merge",
    )(oa, ob, woa, wob, zb, zb)


def _out_proj_kernel(a_ref, w_ref, x_ref, o_ref, w_bf):
    @pl.when(pl.program_id(1) == 0)
    def _():
        w_bf[...] = w_ref[...].astype(BF16)

    o_ref[...] = x_ref[...] + jnp.dot(a_ref[...], w_bf[...], preferred_element_type=F32)


def _out_proj(merged, wout, x, tm, tn):
    m, k = merged.shape
    n = wout.shape[1]
    blocks = _nbytes((tm, k), BF16) + _nbytes((k, tn), F32) + 2 * _nbytes((tm, tn), F32)
    return pl.pallas_call(
        _out_proj_kernel,
        grid=(n // tn, m // tm),
        in_specs=[pl.BlockSpec((tm, k), lambda j, i: (i, 0)),
                  pl.BlockSpec((k, tn), lambda j, i: (0, j)),
                  pl.BlockSpec((tm, tn), lambda j, i: (i, j))],
        out_specs=pl.BlockSpec((tm, tn), lambda j, i: (i, j)),
        out_shape=jax.ShapeDtypeStruct((m, n), F32),
        scratch_shapes=[pltpu.VMEM((k, tn), BF16)],
        compiler_params=_params(("parallel", "arbitrary"), blocks, 2 * _nbytes((k, tn), BF16) + _nbytes((tm, tn), F32)),
        name="out_proj",
    )(merged, wout, x)


def _router_kernel(x_ref, g_ref, wr_ref, xn_ref, route_ref, *, n_groups, per_group):
    x = x_ref[...]
    r = lax.rsqrt(jnp.mean(x * x, axis=-1, keepdims=True) + EPS)
    xn = x * r * g_ref[...]
    xn_ref[...] = xn
    logits = jnp.dot(xn, wr_ref[...], preferred_element_type=F32, precision=lax.Precision.HIGHEST)
    lane = lax.broadcasted_iota(jnp.int32, logits.shape, 1).astype(F32)
    far = float(LANE)

    def top(vals):
        best = jnp.max(vals, axis=-1, keepdims=True)
        return best, jnp.min(jnp.where(vals == best, lane, far), axis=-1, keepdims=True)

    gl = jnp.where(lane < n_groups, logits, NEG_INF)
    gmax, grp = top(gl)
    p_grp = 1.0 / jnp.sum(jnp.exp(gl - gmax), axis=-1, keepdims=True)
    lo = n_groups + grp * per_group
    el = jnp.where((lane >= lo) & (lane < lo + per_group), logits, NEG_INF)
    t1, i1 = top(el)
    t2, i2 = top(jnp.where(lane == i1, NEG_INF, el))
    d = jnp.exp(t2 - t1)
    w1 = p_grp / (1.0 + d)
    w2 = p_grp * d / (1.0 + d)
    route_ref[...] = jnp.where(lane == 0, i1 - n_groups,
                               jnp.where(lane == 1, i2 - n_groups,
                                         jnp.where(lane == 2, w1, jnp.where(lane == 3, w2, 0.0))))


def _router(x1, g, wr, n_groups, per_group, tm):
    t, d = x1.shape
    blocks = 2 * _nbytes((tm, d), F32) + _nbytes((1, d), F32) + _nbytes((d, LANE), F32) + _nbytes((tm, LANE), F32)
    return pl.pallas_call(
        functools.partial(_router_kernel, n_groups=n_groups, per_group=per_group),
        grid=(t // tm,),
        in_specs=[pl.BlockSpec((tm, d), lambda i: (i, 0)), pl.BlockSpec((1, d), lambda i: (0, 0)),
                  pl.BlockSpec((d, LANE), lambda i: (0, 0))],
        out_specs=[pl.BlockSpec((tm, d), lambda i: (i, 0)), pl.BlockSpec((tm, LANE), lambda i: (i, 0))],
        out_shape=[jax.ShapeDtypeStruct((t, d), F32), jax.ShapeDtypeStruct((t, LANE), F32)],
        compiler_params=_params(("parallel",), blocks, 2 * _nbytes((tm, d), F32)),
        name="router",
    )(x1, g.reshape(1, d), wr)


def _one_hots(route):
    lane = lax.broadcasted_iota(jnp.int32, route.shape, 1).astype(F32)
    return (lane == route[:, 0:1]).astype(F32), (lane == route[:, 1:2]).astype(F32)


def _rank_kernel(route_ref, rank_ref, starts_ref, count_acc, start_acc):
    i = pl.program_id(0)

    @pl.when(i == 0)
    def _():
        count_acc[...] = jnp.zeros_like(count_acc)
        start_acc[...] = jnp.zeros_like(start_acc)

    oh1, oh2 = _one_hots(route_ref[...])
    oh = (oh1 + oh2).astype(BF16)
    tm = oh.shape[0]
    earlier = (lax.broadcasted_iota(jnp.int32, (tm, tm), 0) > lax.broadcasted_iota(jnp.int32, (tm, tm), 1))
    before = jnp.dot(earlier.astype(BF16), oh, preferred_element_type=F32) + count_acc[...]
    lane = lax.broadcasted_iota(jnp.int32, (tm, LANE), 1)
    rank_ref[...] = jnp.where(lane == 0, jnp.sum(oh1 * before, axis=-1, keepdims=True),
                              jnp.where(lane == 1, jnp.sum(oh2 * before, axis=-1, keepdims=True), 0.0))
    lower = (lax.broadcasted_iota(jnp.int32, (LANE, LANE), 0) < lax.broadcasted_iota(jnp.int32, (LANE, LANE), 1))
    below = jnp.dot(oh, lower.astype(BF16), preferred_element_type=F32)
    count_acc[...] += jnp.sum(oh.astype(F32), axis=0, keepdims=True)
    start_acc[...] += jnp.sum(below, axis=0, keepdims=True)
    starts_ref[...] = start_acc[...]


def _rank(route, tm):
    t = route.shape[0]
    blocks = 2 * _nbytes((tm, LANE), F32) + _nbytes((1, LANE), F32)
    return pl.pallas_call(
        _rank_kernel,
        grid=(t // tm,),
        in_specs=[pl.BlockSpec((tm, LANE), lambda i: (i, 0))],
        out_specs=[pl.BlockSpec((tm, LANE), lambda i: (i, 0)), pl.BlockSpec((1, LANE), lambda i: (0, 0))],
        out_shape=[jax.ShapeDtypeStruct((t, LANE), F32), jax.ShapeDtypeStruct((1, LANE), F32)],
        scratch_shapes=[pltpu.VMEM((1, LANE), F32), pltpu.VMEM((1, LANE), F32)],
        compiler_params=_params(("arbitrary",), blocks, 2 * _nbytes((tm, tm), F32)),
        name="moe_rank",
    )(route)


def _dest_kernel(route_ref, rank_ref, starts_ref, dest_ref):
    oh1, oh2 = _one_hots(route_ref[...])
    rank = rank_ref[...]
    starts = starts_ref[...]
    d1 = jnp.sum(oh1 * starts, axis=-1, keepdims=True) + rank[:, 0:1]
    d2 = jnp.sum(oh2 * starts, axis=-1, keepdims=True) + rank[:, 1:2]
    lane = lax.broadcasted_iota(jnp.int32, rank.shape, 1)
    dest_ref[...] = jnp.where(lane == 0, d1, jnp.where(lane == 1, d2, 0.0)).astype(jnp.int32)


def _dest(route, rank, starts, tm):
    t = route.shape[0]
    blocks = 3 * _nbytes((tm, LANE), F32) + _nbytes((1, LANE), F32)
    return pl.pallas_call(
        _dest_kernel,
        grid=(t // tm,),
        in_specs=[pl.BlockSpec((tm, LANE), lambda i: (i, 0)), pl.BlockSpec((tm, LANE), lambda i: (i, 0)),
                  pl.BlockSpec((1, LANE), lambda i: (0, 0))],
        out_specs=pl.BlockSpec((tm, LANE), lambda i: (i, 0)),
        out_shape=jax.ShapeDtypeStruct((t, LANE), jnp.int32),
        compiler_params=_params(("parallel",), blocks, 4 * _nbytes((tm, LANE), F32)),
        name="moe_dest",
    )(route, rank, starts)


def _gather_rows_kernel(dest_ref, x_hbm, xs_ref, source, rows_f32, sem, *, n_assign):
    rows = xs_ref.shape[0]
    i = pl.program_id(0)

    def copy(block, j):
        slot = block % 2
        return pltpu.make_async_copy(x_hbm.at[pl.ds(source[block * rows + j] // TOP_K, 1)],
                                     rows_f32.at[slot, pl.ds(j, 1)], sem.at[slot])

    def start_block(block):
        def start(j, carry):
            copy(block, j).start()
            return carry
        lax.fori_loop(0, rows, start, 0, unroll=4)

    @pl.when(i == 0)
    def _():
        def invert(a, carry):
            source[dest_ref[a]] = a
            return carry
        lax.fori_loop(0, n_assign, invert, 0, unroll=8)
        start_block(i)

    @pl.when(i + 1 < pl.num_programs(0))
    def _():
        start_block(i + 1)

    def wait(j, carry):
        copy(i, j).wait()
        return carry

    lax.fori_loop(0, rows, wait, 0, unroll=4)
    xs_ref[...] = rows_f32[i % 2].astype(xs_ref.dtype)


def _gather_rows(dest_flat, xn, rows):
    t, d = xn.shape
    n_assign = dest_flat.shape[0]
    return pl.pallas_call(
        functools.partial(_gather_rows_kernel, n_assign=n_assign),
        grid_spec=pltpu.PrefetchScalarGridSpec(
            num_scalar_prefetch=1,
            grid=(n_assign // rows,),
            in_specs=[pl.BlockSpec(memory_space=pl.ANY)],
            out_specs=pl.BlockSpec((rows, d), lambda i, dest: (i, 0)),
            scratch_shapes=[pltpu.SMEM((n_assign,), jnp.int32), pltpu.VMEM((2, rows, d), xn.dtype),
                            pltpu.SemaphoreType.DMA((2,))]),
        out_shape=jax.ShapeDtypeStruct((n_assign, d), BF16),
        compiler_params=_params(("arbitrary",), _nbytes((rows, d), BF16), 3 * _nbytes((rows, d), xn.dtype)),
        name="moe_gather_rows",
    )(dest_flat, xn)


def _work_items(starts, n_rows):
    n_exp = starts.shape[0]
    n_blk = n_rows // MOE_ROWS
    total = jnp.full((1,), n_rows, jnp.int32)
    pts = jnp.concatenate([jnp.arange(n_blk, dtype=jnp.int32) * MOE_ROWS, starts[1:]])
    idx = jnp.arange(pts.shape[0], dtype=jnp.int32)
    before = (pts[None, :] < pts[:, None]) | ((pts[None, :] == pts[:, None]) & (idx[None, :] < idx[:, None]))
    pos = jnp.sum(before.astype(jnp.int32), axis=1)
    lo = jnp.sum(jnp.where(pos[:, None] == idx[None, :], pts[:, None], 0), axis=0)
    hi = jnp.concatenate([lo[1:], total])
    ends = jnp.concatenate([starts[1:], total])
    r = jnp.minimum(lo // MOE_ROWS, n_blk - 1)
    e = jnp.minimum(jnp.sum((ends[None, :] <= lo[:, None]).astype(jnp.int32), axis=1), n_exp - 1)
    changed = jnp.concatenate([jnp.ones((1,), jnp.int32), (e[1:] != e[:-1]).astype(jnp.int32)])
    slot = (jnp.sum(jnp.where(idx[None, :] <= idx[:, None], changed[None, :], 0), axis=1) - 1) % 2
    later = jnp.where(e[None, :] > e[:, None], e[None, :], n_exp)
    nxt = jnp.min(later, axis=1)
    nxt = jnp.where(nxt == n_exp, -1, nxt)
    return r, e, lo, hi, slot, nxt


def _stream_expert_weights(w, e_ref, slot_ref, nxt_ref, streams, convert):
    e = e_ref[w]
    slot = slot_ref[w]

    def copies(expert, into):
        out = []
        for hbm, stage, sem in streams:
            rows = stage.shape[1] // WEIGHT_DMA_PARTS
            for part in range(WEIGHT_DMA_PARTS):
                sl = pl.ds(part * rows, rows)
                out.append(pltpu.make_async_copy(hbm.at[expert, sl], stage.at[into, sl], sem.at[into]))
        return out

    @pl.when(w == 0)
    def _():
        for c in copies(e, slot):
            c.start()

    @pl.when((w == 0) | (e != e_ref[jnp.maximum(w - 1, 0)]))
    def _():
        for c in copies(e, slot):
            c.wait()
        nxt = nxt_ref[w]

        @pl.when(nxt >= 0)
        def _():
            for c in copies(nxt, 1 - slot):
                c.start()

        convert(slot)


def _store_item_rows(o_ref, val, r, lo, hi):
    rows = r * MOE_ROWS + lax.broadcasted_iota(jnp.int32, (MOE_ROWS, 1), 0)
    mine = (rows >= lo) & (rows < hi)

    @pl.when(lo == r * MOE_ROWS)
    def _():
        o_ref[...] = val

    @pl.when(lo != r * MOE_ROWS)
    def _():
        o_ref[...] = jnp.where(mine, val, o_ref[...])


def _moe_up_kernel(r_ref, e_ref, lo_ref, hi_ref, slot_ref, nxt_ref, xs_ref, wg_hbm, wu_hbm, h_ref,
                   wg_stage, wu_stage, wg_bf, wu_bf, sem_g, sem_u):
    w = pl.program_id(0)

    def convert(slot):
        wg_bf[...] = wg_stage[slot].astype(BF16)
        wu_bf[...] = wu_stage[slot].astype(BF16)

    _stream_expert_weights(w, e_ref, slot_ref, nxt_ref,
                           [(wg_hbm, wg_stage, sem_g), (wu_hbm, wu_stage, sem_u)], convert)
    r, lo, hi = r_ref[w], lo_ref[w], hi_ref[w]

    @pl.when(hi > lo)
    def _():
        x = xs_ref[...]
        g = jnp.dot(x, wg_bf[...], preferred_element_type=F32)
        u = jnp.dot(x, wu_bf[...], preferred_element_type=F32)
        h = (g * (1.0 / (1.0 + jnp.exp(-g)))) * u
        _store_item_rows(h_ref, h.astype(h_ref.dtype), r, lo, hi)


def _moe_up(items, xs, wg, wu):
    n_rows, d = xs.shape
    _, _, f = wg.shape
    n_items = items[0].shape[0]
    blocks = _nbytes((MOE_ROWS, d), BF16) + _nbytes((MOE_ROWS, f), BF16)
    resident = 4 * _nbytes((d, f), F32) + 2 * _nbytes((d, f), BF16) + 6 * _nbytes((MOE_ROWS, f), F32)
    row_block = lambda w, r, e, lo, hi, slot, nxt: (r[w], 0)
    return pl.pallas_call(
        _moe_up_kernel,
        grid_spec=pltpu.PrefetchScalarGridSpec(
            num_scalar_prefetch=6,
            grid=(n_items,),
            in_specs=[pl.BlockSpec((MOE_ROWS, d), row_block),
                      pl.BlockSpec(memory_space=pl.ANY), pl.BlockSpec(memory_space=pl.ANY)],
            out_specs=pl.BlockSpec((MOE_ROWS, f), row_block),
            scratch_shapes=[pltpu.VMEM((2, d, f), F32), pltpu.VMEM((2, d, f), F32),
                            pltpu.VMEM((d, f), BF16), pltpu.VMEM((d, f), BF16),
                            pltpu.SemaphoreType.DMA((2,)), pltpu.SemaphoreType.DMA((2,))]),
        out_shape=jax.ShapeDtypeStruct((n_rows, f), BF16),
        compiler_params=_params(("arbitrary",), blocks, resident),
        name="moe_up",
    )(*items, xs, wg, wu)


def _moe_down_kernel(r_ref, e_ref, lo_ref, hi_ref, slot_ref, nxt_ref, h_ref, wd_hbm, y_ref, wd_stage, wd_bf, sem):
    w = pl.program_id(0)

    def convert(slot):
        wd_bf[...] = wd_stage[slot].astype(BF16)

    _stream_expert_weights(w, e_ref, slot_ref, nxt_ref, [(wd_hbm, wd_stage, sem)], convert)
    r, lo, hi = r_ref[w], lo_ref[w], hi_ref[w]

    @pl.when(hi > lo)
    def _():
        y = jnp.dot(h_ref[...], wd_bf[...], preferred_element_type=F32)
        _store_item_rows(y_ref, y, r, lo, hi)


def _moe_down(items, h, wd):
    n_rows, f = h.shape
    d = wd.shape[2]
    n_items = items[0].shape[0]
    blocks = _nbytes((MOE_ROWS, f), BF16) + _nbytes((MOE_ROWS, d), F32)
    resident = 2 * _nbytes((f, d), F32) + _nbytes((f, d), BF16) + 3 * _nbytes((MOE_ROWS, d), F32)
    row_block = lambda w, r, e, lo, hi, slot, nxt: (r[w], 0)
    return pl.pallas_call(
        _moe_down_kernel,
        grid_spec=pltpu.PrefetchScalarGridSpec(
            num_scalar_prefetch=6,
            grid=(n_items,),
            in_specs=[pl.BlockSpec((MOE_ROWS, f), row_block), pl.BlockSpec(memory_space=pl.ANY)],
            out_specs=pl.BlockSpec((MOE_ROWS, d), row_block),
            scratch_shapes=[pltpu.VMEM((2, f, d), F32), pltpu.VMEM((f, d), BF16), pltpu.SemaphoreType.DMA((2,))]),
        out_shape=jax.ShapeDtypeStruct((n_rows, d), F32),
        compiler_params=_params(("arbitrary",), blocks, resident),
        name="moe_down",
    )(*items, h, wd)


def _combine_kernel(dest_ref, x_ref, route_ref, y_hbm, o_ref, ybuf, sem, *, tm):
    i = pl.program_id(0)

    def copies(tile, t):
        slot = tile % 2
        return [pltpu.make_async_copy(y_hbm.at[pl.ds(dest_ref[TOP_K * (tile * tm + t) + k], 1)],
                                      ybuf.at[slot, k, pl.ds(t, 1)], sem.at[slot]) for k in range(TOP_K)]

    def start_tile(tile):
        def start(t, carry):
            for c in copies(tile, t):
                c.start()
            return carry
        lax.fori_loop(0, tm, start, 0, unroll=2)

    @pl.when(i == 0)
    def _():
        start_tile(i)

    @pl.when(i + 1 < pl.num_programs(0))
    def _():
        start_tile(i + 1)

    def wait(t, carry):
        for c in copies(i, t):
            c.wait()
        return carry

    lax.fori_loop(0, tm, wait, 0, unroll=2)
    route = route_ref[...]
    rows = ybuf[i % 2]
    o_ref[...] = x_ref[...] + (route[:, 2:3] * rows[0] + route[:, 3:4] * rows[1])


def _combine(dest_flat, x1, route, y, tm):
    t, d = x1.shape
    blocks = 2 * _nbytes((tm, d), F32) + _nbytes((tm, LANE), F32)
    resident = 2 * TOP_K * _nbytes((tm, d), F32) + _nbytes((tm, d), F32)
    return pl.pallas_call(
        functools.partial(_combine_kernel, tm=tm),
        grid_spec=pltpu.PrefetchScalarGridSpec(
            num_scalar_prefetch=1,
            grid=(t // tm,),
            in_specs=[pl.BlockSpec((tm, d), lambda i, dest: (i, 0)),
                      pl.BlockSpec((tm, LANE), lambda i, dest: (i, 0)),
                      pl.BlockSpec(memory_space=pl.ANY)],
            out_specs=pl.BlockSpec((tm, d), lambda i, dest: (i, 0)),
            scratch_shapes=[pltpu.VMEM((2, TOP_K, tm, d), F32), pltpu.SemaphoreType.DMA((2,))]),
        out_shape=jax.ShapeDtypeStruct((t, d), F32),
        compiler_params=_params(("arbitrary",), blocks, resident),
        name="moe_combine",
    )(dest_flat, x1, route, y)


def _pad_cols(w, n):
    return jnp.pad(w, ((0, 0), (0, n - w.shape[1])))


def kernel(x, positions, g_mix, w_in, b_gate, q_norm_g, kv_norm_g, w_uq, w_ukv, a_q_norm_g, a_k_norm_g,
           b_q_norm_g, b_k_norm_g, rel_bias, w_o_a, w_o_b, w_out, g_ffn, w_group, w_expert,
           w_exp_gate, w_exp_up, w_exp_down):
    batch, seq, d = x.shape
    t = batch * seq
    q_lora, kv_lora = q_norm_g.shape[0], kv_norm_g.shape[0]
    a_heads = w_uq.shape[1] // A_QK
    b_heads = w_o_b.shape[0] // B_HEAD_DIM
    b_width = b_heads * B_HEAD_DIM
    n_groups, n_experts = w_group.shape[1], w_expert.shape[1]
    per_group = n_experts // n_groups
    off_b = q_lora + kv_lora + A_ROPE
    assert seq % ATTN_TILE == 0 and (TOP_K * t) % GATHER_ROWS == 0 and n_groups + n_experts <= LANE
    assert a_heads == b_heads

    xf = x.reshape(t, d)
    tm_big = min(1024, t)
    tn = _tile(b_width, 512)
    assert d % tn == 0

    za_cols = -(-(q_lora + kv_lora + LANE) // tn) * tn
    wuq = jnp.pad(w_uq.reshape(q_lora, a_heads, A_QK), ((0, 0), (0, 0), (0, A_HEAD_PAD - A_QK)))
    wuq = wuq.reshape(q_lora, a_heads * A_HEAD_PAD).astype(BF16)
    wukv = w_ukv.astype(BF16)
    pad_gain = lambda g, s: jnp.pad(g * s, (0, A_HEAD_PAD - A_QK)).reshape(1, A_HEAD_PAD)
    gaq = pad_gain(a_q_norm_g, A_QK ** -0.5)
    gak = pad_gain(a_k_norm_g, 1.0)
    gb = jnp.concatenate([jnp.tile(b_q_norm_g * B_HEAD_DIM ** -0.5, b_heads), jnp.tile(b_k_norm_g, b_heads),
                          jnp.ones((b_width,), F32), b_gate]).reshape(1, -1)

    half = A_ROPE // 2
    inv = ROPE_THETA ** (-jnp.arange(half, dtype=F32) / half)
    ang = positions.astype(F32).reshape(t, 1) * inv
    cos, sin = jnp.cos(ang), jnp.sin(ang)
    zeros = jnp.zeros((t, half), F32)
    cos_t = jnp.concatenate([cos, cos, zeros, zeros], axis=1)
    sin_lo = jnp.concatenate([-sin, zeros, zeros, zeros], axis=1)
    sin_hi = jnp.concatenate([zeros, sin, zeros, zeros], axis=1)

    xn = _rmsnorm_rows(xf, g_mix, BF16, min(256, t))
    w_in_t = w_in.T
    za = _inproj_a(xn, w_in_t, za_cols, tm_big, tn)
    zb = _inproj_b(xn, w_in_t, off_b, gb, tm_big, tn, 2 * b_width // tn, 3 * b_width // tn)
    q, k, v = _mla_proj(za, cos_t, sin_lo, sin_hi, wuq, wukv, q_norm_g.reshape(1, -1), kv_norm_g.reshape(1, -1),
                        gaq, gak, a_heads, min(256, t))
    o_a, o_b = _attention(q, k, v, zb, _band_bias_rows(rel_bias), batch, seq, a_heads)
    merged = _merge(o_a, o_b, w_o_a, w_o_b, zb, 3 * b_width, tm_big, tn)
    x1 = _out_proj(merged, w_out, xf, tm_big, tn)

    wr = _pad_cols(jnp.concatenate([w_group, w_expert], axis=1), LANE)
    xn2, route = _router(x1, g_ffn, wr, n_groups, per_group, min(256, t))
    rank, starts_f = _rank(route, min(512, t))
    dest = _dest(route, rank, starts_f, min(512, t))[:, :TOP_K].reshape(-1)
    xs = _gather_rows(dest, xn2, GATHER_ROWS)
    items = _work_items(starts_f[0, :n_experts].astype(jnp.int32), TOP_K * t)
    h = _moe_up(items, xs, w_exp_gate, w_exp_up)
    y = _moe_down(items, h, w_exp_down)
    out = _combine(dest, x1, route, y, min(256, t))
    return out.reshape(batch, seq, d)
```

```python
import functools

import jax
import jax.numpy as jnp
from jax import lax
from jax.experimental import pallas as pl
from jax.experimental.pallas import tpu as pltpu

F32 = jnp.float32
BF16 = jnp.bfloat16

CHUNK = 64
EPS = 1e-6
A_NOPE = 128
A_ROPE = 64
A_V = 128
A_QK = A_NOPE + A_ROPE
B_HEAD_DIM = 128
B_LEFT_CHUNKS = 8
B_MAX_REL = 128
ROPE_THETA = 10000.0
TOP_K = 2

LANE = 128
A_HEAD_PAD = 2 * LANE
V_PAD = 2 * LANE
V7X_VMEM_BYTES = 64 * 2**20

ATTN_TILE = 256
MOE_ROWS = 128
GATHER_ROWS = 256
WEIGHT_DMA_PARTS = 4
NEG_INF = float("-inf")


def _nbytes(shape, dtype):
    n = 1
    for s in shape:
        n *= s
    return n * jnp.dtype(dtype).itemsize


def _params(semantics, pipelined_bytes, resident_bytes=0):
    need = 2 * pipelined_bytes + resident_bytes
    return pltpu.CompilerParams(dimension_semantics=semantics,
                                vmem_limit_bytes=min(int(need), V7X_VMEM_BYTES))


def _tile(n, want):
    t = want
    while t > LANE and n % t:
        t //= 2
    assert n % t == 0, (n, want)
    return t


def _rmsnorm_kernel(x_ref, g_ref, o_ref):
    x = x_ref[...]
    r = lax.rsqrt(jnp.mean(x * x, axis=-1, keepdims=True) + EPS)
    o_ref[...] = (x * r * g_ref[...]).astype(o_ref.dtype)


def _rmsnorm_rows(x, g, out_dtype, tm):
    t, d = x.shape
    blocks = _nbytes((tm, d), F32) + _nbytes((tm, d), out_dtype) + _nbytes((1, d), F32)
    return pl.pallas_call(
        _rmsnorm_kernel,
        grid=(t // tm,),
        in_specs=[pl.BlockSpec((tm, d), lambda i: (i, 0)), pl.BlockSpec((1, d), lambda i: (0, 0))],
        out_specs=pl.BlockSpec((tm, d), lambda i: (i, 0)),
        out_shape=jax.ShapeDtypeStruct((t, d), out_dtype),
        compiler_params=_params(("parallel",), blocks, _nbytes((tm, d), F32)),
        name="rmsnorm",
    )(x, g.reshape(1, d))


def _nt_dot(a, b):
    return lax.dot_general(a, b, (((1,), (1,)), ((), ())), preferred_element_type=F32)


def _inproj_a_kernel(a_ref, wt_ref, o_ref, wt_bf):
    @pl.when(pl.program_id(1) == 0)
    def _():
        wt_bf[...] = wt_ref[...].astype(BF16)

    o_ref[...] = _nt_dot(a_ref[...], wt_bf[...])


def _inproj_a(xn, w_in_t, n_cols, tm, tn):
    m, k = xn.shape
    blocks = _nbytes((tm, k), BF16) + _nbytes((tn, k), F32) + _nbytes((tm, tn), F32)
    return pl.pallas_call(
        _inproj_a_kernel,
        grid=(n_cols // tn, m // tm),
        in_specs=[pl.BlockSpec((tm, k), lambda j, i: (i, 0)), pl.BlockSpec((tn, k), lambda j, i: (j, 0))],
        out_specs=pl.BlockSpec((tm, tn), lambda j, i: (i, j)),
        out_shape=jax.ShapeDtypeStruct((m, n_cols), F32),
        scratch_shapes=[pltpu.VMEM((tn, k), BF16)],
        compiler_params=_params(("parallel", "arbitrary"), blocks, 2 * _nbytes((tn, k), BF16)),
        name="inproj_a",
    )(xn, w_in_t)


def _inproj_b_kernel(a_ref, wt_hbm, gb_ref, o_ref, stage, wt_bf, sem, *, first_col, n_norm_blocks, n_plain_end):
    j = pl.program_id(0)
    tn = wt_bf.shape[0]

    def fetch(block):
        rows = pl.ds(pl.multiple_of(first_col + block * tn, 8), tn)
        return pltpu.make_async_copy(wt_hbm.at[rows], stage, sem)

    @pl.when(pl.program_id(1) == 0)
    def _():
        @pl.when(j == 0)
        def _():
            fetch(j).start()

        fetch(j).wait()
        wt_bf[...] = stage[...].astype(BF16)

        @pl.when(j + 1 < pl.num_programs(0))
        def _():
            fetch(j + 1).start()

    acc = _nt_dot(a_ref[...], wt_bf[...])

    @pl.when(j < n_norm_blocks)
    def _():
        for h in range(acc.shape[1] // B_HEAD_DIM):
            sl = slice(h * B_HEAD_DIM, (h + 1) * B_HEAD_DIM)
            z = acc[:, sl]
            r = lax.rsqrt(jnp.mean(z * z, axis=-1, keepdims=True) + EPS)
            o_ref[:, sl] = (z * r * gb_ref[:, sl]).astype(o_ref.dtype)

    @pl.when((j >= n_norm_blocks) & (j < n_plain_end))
    def _():
        o_ref[...] = acc.astype(o_ref.dtype)

    @pl.when(j >= n_plain_end)
    def _():
        o_ref[...] = (1.0 / (1.0 + jnp.exp(-(acc + gb_ref[...])))).astype(o_ref.dtype)


def _inproj_b(xn, w_in_t, first_col, gb, tm, tn, n_norm_blocks, n_plain_end):
    m, k = xn.shape
    n = w_in_t.shape[0] - first_col
    assert n % tn == 0 and first_col % 8 == 0
    blocks = _nbytes((tm, k), BF16) + _nbytes((tm, tn), BF16) + _nbytes((1, tn), F32)
    resident = _nbytes((tn, k), F32) + 2 * _nbytes((tn, k), BF16) + 2 * _nbytes((tm, tn), F32)
    return pl.pallas_call(
        functools.partial(_inproj_b_kernel, first_col=first_col, n_norm_blocks=n_norm_blocks,
                          n_plain_end=n_plain_end),
        grid=(n // tn, m // tm),
        in_specs=[pl.BlockSpec((tm, k), lambda j, i: (i, 0)),
                  pl.BlockSpec(memory_space=pl.ANY),
                  pl.BlockSpec((1, tn), lambda j, i: (0, j))],
        out_specs=pl.BlockSpec((tm, tn), lambda j, i: (i, j)),
        out_shape=jax.ShapeDtypeStruct((m, n), BF16),
        scratch_shapes=[pltpu.VMEM((tn, k), F32), pltpu.VMEM((tn, k), BF16), pltpu.SemaphoreType.DMA(())],
        compiler_params=_params(("arbitrary", "arbitrary"), blocks, resident),
        name="inproj_b",
    )(xn, w_in_t, gb)


def _rope_padded(v, cos, sin_lo, sin_hi):
    half = A_ROPE // 2
    return v * cos + pltpu.roll(v, LANE - half, 1) * sin_lo + pltpu.roll(v, half, 1) * sin_hi


def _ones_column(rows):
    lane = lax.broadcasted_iota(jnp.int32, (rows, V_PAD - A_V), 1)
    return jnp.where(lane == 0, 1.0, 0.0).astype(BF16)


def _mla_proj_kernel(za_ref, cos_ref, sl_ref, sh_ref, wuq_ref, wukv_ref, gq_ref, gkv_ref, gaq_ref, gak_ref,
                     q_ref, k_ref, v_ref, *, heads, q_lora, kv_lora):
    cos, sin_lo, sin_hi = cos_ref[...], sl_ref[...], sh_ref[...]

    def norm(z, g):
        r = lax.rsqrt(jnp.mean(z * z, axis=-1, keepdims=True) + EPS)
        return (z * r * g).astype(BF16)

    cq = norm(za_ref[:, :q_lora], gq_ref[...])
    ckv = norm(za_ref[:, q_lora:q_lora + kv_lora], gkv_ref[...])
    slab = za_ref[:, q_lora + kv_lora:q_lora + kv_lora + LANE]
    k_rope = jnp.where(lax.broadcasted_iota(jnp.int32, slab.shape, 1) < A_ROPE, slab, 0.0)
    qacc = jnp.dot(cq, wuq_ref[...], preferred_element_type=F32)
    kvacc = jnp.dot(ckv, wukv_ref[...], preferred_element_type=F32)

    gq_lo, gq_hi = gaq_ref[:, :LANE], gaq_ref[:, LANE:]
    gk_lo, gk_hi = gak_ref[:, :LANE], gak_ref[:, LANE:]
    kr_ss = jnp.sum(k_rope * k_rope, axis=-1, keepdims=True)
    kr_rot = _rope_padded(k_rope * gk_hi, cos, sin_lo, sin_hi)
    ones_col = _ones_column(slab.shape[0])
    for h in range(heads):
        base = h * A_HEAD_PAD
        q_lo = qacc[:, base:base + LANE]
        q_hi = qacc[:, base + LANE:base + A_HEAD_PAD]
        ss = jnp.sum(q_lo * q_lo, axis=-1, keepdims=True) + jnp.sum(q_hi * q_hi, axis=-1, keepdims=True)
        r = lax.rsqrt(ss / A_QK + EPS)
        q_ref[:, base:base + LANE] = (q_lo * r * gq_lo).astype(BF16)
        q_ref[:, base + LANE:base + A_HEAD_PAD] = _rope_padded(q_hi * r * gq_hi, cos, sin_lo, sin_hi).astype(BF16)

        k_lo = kvacc[:, base:base + LANE]
        ssk = jnp.sum(k_lo * k_lo, axis=-1, keepdims=True) + kr_ss
        rk = lax.rsqrt(ssk / A_QK + EPS)
        k_ref[:, base:base + LANE] = (k_lo * rk * gk_lo).astype(BF16)
        k_ref[:, base + LANE:base + A_HEAD_PAD] = (kr_rot * rk).astype(BF16)
        v_ref[:, h * V_PAD:h * V_PAD + A_V] = kvacc[:, base + LANE:base + A_HEAD_PAD].astype(BF16)
        v_ref[:, h * V_PAD + A_V:(h + 1) * V_PAD] = ones_col


def _mla_proj(za, cos, sin_lo, sin_hi, wuq, wukv, gq, gkv, gaq, gak, heads, tm):
    t, za_cols = za.shape
    q_lora, kv_lora = wuq.shape[0], wukv.shape[0]
    hp = heads * A_HEAD_PAD
    row = lambda i: (i, 0)
    fix = lambda i: (0, 0)
    blocks = (_nbytes((tm, za_cols), F32) + 3 * _nbytes((tm, LANE), F32) + _nbytes(wuq.shape, BF16)
              + _nbytes(wukv.shape, BF16) + 2 * _nbytes((tm, hp), BF16) + _nbytes((tm, heads * V_PAD), BF16))
    return pl.pallas_call(
        functools.partial(_mla_proj_kernel, heads=heads, q_lora=q_lora, kv_lora=kv_lora),
        grid=(t // tm,),
        in_specs=[pl.BlockSpec((tm, za_cols), row),
                  pl.BlockSpec((tm, LANE), row), pl.BlockSpec((tm, LANE), row), pl.BlockSpec((tm, LANE), row),
                  pl.BlockSpec(wuq.shape, fix), pl.BlockSpec(wukv.shape, fix),
                  pl.BlockSpec((1, q_lora), fix), pl.BlockSpec((1, kv_lora), fix),
                  pl.BlockSpec((1, A_HEAD_PAD), fix), pl.BlockSpec((1, A_HEAD_PAD), fix)],
        out_specs=[pl.BlockSpec((tm, hp), row), pl.BlockSpec((tm, hp), row), pl.BlockSpec((tm, heads * V_PAD), row)],
        out_shape=[jax.ShapeDtypeStruct((t, hp), BF16), jax.ShapeDtypeStruct((t, hp), BF16),
                   jax.ShapeDtypeStruct((t, heads * V_PAD), BF16)],
        compiler_params=_params(("parallel",), blocks, 3 * _nbytes((tm, hp), F32)),
        name="mla_proj",
    )(za, cos, sin_lo, sin_hi, wuq, wukv, gq, gkv, gaq, gak)


def _mla_scores(i, q_ref, k_ref, diag_ok):
    tq = ATTN_TILE
    q = q_ref[i * tq:(i + 1) * tq, :]
    sd = jnp.where(diag_ok, _nt_dot(q, k_ref[i * tq:(i + 1) * tq, :]), NEG_INF)
    s0 = _nt_dot(q, k_ref[:i * tq, :]) if i > 0 else None
    return sd, s0


def _mla_softmax(sd, s0):
    m = jnp.max(sd, axis=-1, keepdims=True)
    if s0 is None:
        return jnp.exp(sd - m).astype(BF16), None
    m = jnp.maximum(m, jnp.max(s0, axis=-1, keepdims=True))
    return jnp.exp(sd - m).astype(BF16), jnp.exp(s0 - m).astype(BF16)


def _mla_values(i, pd, p0, v_ref, o_ref):
    tq = ATTN_TILE
    o = jnp.dot(pd, v_ref[i * tq:(i + 1) * tq, :], preferred_element_type=F32)
    if p0 is not None:
        o = o + jnp.dot(p0, v_ref[:i * tq, :], preferred_element_type=F32)
    o_ref[i * tq:(i + 1) * tq, :] = (o[:, :A_V] / o[:, A_V:A_V + 1]).astype(o_ref.dtype)


def _band_window(i):
    left = B_LEFT_CHUNKS * CHUNK
    q0 = i * ATTN_TILE
    k0 = max(0, q0 - left)
    return q0, k0, q0 + ATTN_TILE - k0, left - q0 + k0


def _band_scores(i, q_ref, k_ref, table):
    q0, k0, kw, u0 = _band_window(i)
    return _nt_dot(q_ref[q0:q0 + ATTN_TILE, :], k_ref[k0:k0 + kw, :]) + table[:, u0:u0 + kw]


def _band_softmax(s):
    return jnp.exp(s - jnp.max(s, axis=-1, keepdims=True)).astype(BF16)


def _band_values(i, p, v_ref, o_ref):
    q0, k0, kw, _ = _band_window(i)
    o = jnp.dot(p, v_ref[k0:k0 + kw, :], preferred_element_type=F32)
    o_ref[q0:q0 + ATTN_TILE, :] = (o[:, :B_HEAD_DIM] / o[:, B_HEAD_DIM:B_HEAD_DIM + 1]).astype(o_ref.dtype)


def _attn_kernel(aq_ref, ak_ref, av_ref, bq_ref, bk_ref, bv_ref, r_ref, oa_ref, ob_ref, bv_pad, *, seq):
    tq = ATTN_TILE
    left = B_LEFT_CHUNKS * CHUNK
    width = r_ref.shape[1]
    rc = lax.broadcasted_iota(jnp.int32, (tq, tq), 0) // CHUNK
    cc = lax.broadcasted_iota(jnp.int32, (tq, tq), 1) // CHUNK
    diag_ok = cc <= rc
    bias = pltpu.roll(jnp.broadcast_to(r_ref[...], (tq, width)), 0, 1, stride=1, stride_axis=0)[:, :left + tq]
    q_chunk = lax.broadcasted_iota(jnp.int32, (tq, left + tq), 0) // CHUNK
    k_chunk = lax.broadcasted_iota(jnp.int32, (tq, left + tq), 1) // CHUNK
    table = jnp.where((k_chunk >= q_chunk) & (k_chunk <= q_chunk + B_LEFT_CHUNKS), bias, NEG_INF)
    bv_pad[:, :B_HEAD_DIM] = bv_ref[...]
    bv_pad[:, B_HEAD_DIM:] = _ones_column(seq)

    n_tiles = seq // tq
    sd, s0 = _mla_scores(0, aq_ref, ak_ref, diag_ok)
    for i in range(n_tiles):
        sb = _band_scores(i, bq_ref, bk_ref, table)
        pd, p0 = _mla_softmax(sd, s0)
        if i + 1 < n_tiles:
            sd, s0 = _mla_scores(i + 1, aq_ref, ak_ref, diag_ok)
        _mla_values(i, pd, p0, av_ref, oa_ref)
        _band_values(i, _band_softmax(sb), bv_pad, ob_ref)


def _band_bias_rows(rel_bias):
    left = B_LEFT_CHUNKS * CHUNK
    width = left + 2 * ATTN_TILE
    m = jnp.arange(width, dtype=jnp.int32)
    j = jnp.where(m < left + ATTN_TILE, m, m - width)
    dist = left - j
    rows = rel_bias[:, jnp.clip(dist, -B_MAX_REL, B_MAX_REL) + B_MAX_REL].astype(F32)
    return rows.reshape(rel_bias.shape[0], 1, width)


def _attention(q, k, v, zb, bias_rows, batch, seq, heads):
    t = batch * seq
    d = B_HEAD_DIM
    width = bias_rows.shape[2]
    head = lambda b, h: (b, h)
    blocks = (2 * _nbytes((seq, A_HEAD_PAD), BF16) + _nbytes((seq, V_PAD), BF16) + _nbytes((seq, A_V), BF16)
              + 4 * _nbytes((seq, d), BF16) + _nbytes((1, width), F32))
    temps = _nbytes((seq, V_PAD), BF16) + 8 * _nbytes((ATTN_TILE, seq), F32) + 8 * _nbytes((ATTN_TILE, width), F32)
    return pl.pallas_call(
        functools.partial(_attn_kernel, seq=seq),
        grid=(batch, heads),
        in_specs=[pl.BlockSpec((seq, A_HEAD_PAD), head), pl.BlockSpec((seq, A_HEAD_PAD), head),
                  pl.BlockSpec((seq, V_PAD), head),
                  pl.BlockSpec((seq, d), head),
                  pl.BlockSpec((seq, d), lambda b, h: (b, heads + h)),
                  pl.BlockSpec((seq, d), lambda b, h: (b, 2 * heads + h)),
                  pl.BlockSpec((None, 1, width), lambda b, h: (h, 0, 0))],
        out_specs=[pl.BlockSpec((seq, A_V), head), pl.BlockSpec((seq, d), head)],
        out_shape=[jax.ShapeDtypeStruct((t, heads * A_V), BF16), jax.ShapeDtypeStruct((t, heads * d), BF16)],
        scratch_shapes=[pltpu.VMEM((seq, V_PAD), BF16)],
        compiler_params=_params(("parallel", "parallel"), blocks, temps),
        name="attention",
    )(q, k, v, zb, zb, zb, bias_rows)


def _merge_kernel(oa_ref, ob_ref, woa_ref, wob_ref, g0_ref, g1_ref, o_ref, woa_bf, wob_bf):
    @pl.when(pl.program_id(1) == 0)
    def _():
        woa_bf[...] = woa_ref[...].astype(BF16)
        wob_bf[...] = wob_ref[...].astype(BF16)

    a = jnp.dot(oa_ref[...], woa_bf[...], preferred_element_type=F32)
    b = jnp.dot(ob_ref[...], wob_bf[...], preferred_element_type=F32)
    o_ref[...] = (g0_ref[...].astype(F32) * a + g1_ref[...].astype(F32) * b).astype(o_ref.dtype)


def _merge(oa, ob, woa, wob, zb, gate_col, tm, tn):
    m = oa.shape[0]
    d = woa.shape[1]
    g0, g1 = gate_col // tn, (gate_col + d) // tn
    blocks = (_nbytes((tm, oa.shape[1]), BF16) + _nbytes((tm, ob.shape[1]), BF16) + _nbytes((woa.shape[0], tn), F32)
              + _nbytes((wob.shape[0], tn), F32) + 3 * _nbytes((tm, tn), BF16))
    resident = (2 * _nbytes((woa.shape[0], tn), BF16) + 2 * _nbytes((wob.shape[0], tn), BF16)
                + 3 * _nbytes((tm, tn), F32))
    return pl.pallas_call(
        _merge_kernel,
        grid=(d // tn, m // tm),
        in_specs=[pl.BlockSpec((tm, oa.shape[1]), lambda j, i: (i, 0)),
                  pl.BlockSpec((tm, ob.shape[1]), lambda j, i: (i, 0)),
                  pl.BlockSpec((woa.shape[0], tn), lambda j, i: (0, j)),
                  pl.BlockSpec((wob.shape[0], tn), lambda j, i: (0, j)),
                  pl.BlockSpec((tm, tn), lambda j, i: (i, g0 + j)),
                  pl.BlockSpec((tm, tn), lambda j, i: (i, g1 + j))],
        out_specs=pl.BlockSpec((tm, tn), lambda j, i: (i, j)),
        out_shape=jax.ShapeDtypeStruct((m, d), BF16),
        scratch_shapes=[pltpu.VMEM((woa.shape[0], tn), BF16), pltpu.VMEM((wob.shape[0], tn), BF16)],
        compiler_params=_params(("parallel", "arbitrary"), blocks, resident),
        name="merge",
    )(oa, ob, woa, wob, zb, zb)


def _out_proj_kernel(a_ref, w_ref, x_ref, o_ref, w_bf):
    @pl.when(pl.program_id(1) == 0)
    def _():
        w_bf[...] = w_ref[...].astype(BF16)

    o_ref[...] = x_ref[...] + jnp.dot(a_ref[...], w_bf[...], preferred_element_type=F32)


def _out_proj(merged, wout, x, tm, tn):
    m, k = merged.shape
    n = wout.shape[1]
    blocks = _nbytes((tm, k), BF16) + _nbytes((k, tn), F32) + 2 * _nbytes((tm, tn), F32)
    return pl.pallas_call(
        _out_proj_kernel,
        grid=(n // tn, m // tm),
        in_specs=[pl.BlockSpec((tm, k), lambda j, i: (i, 0)),
                  pl.BlockSpec((k, tn), lambda j, i: (0, j)),
                  pl.BlockSpec((tm, tn), lambda j, i: (i, j))],
        out_specs=pl.BlockSpec((tm, tn), lambda j, i: (i, j)),
        out_shape=jax.ShapeDtypeStruct((m, n), F32),
        scratch_shapes=[pltpu.VMEM((k, tn), BF16)],
        compiler_params=_params(("parallel", "arbitrary"), blocks, 2 * _nbytes((k, tn), BF16) + _nbytes((tm, tn), F32)),
        name="out_proj",
    )(merged, wout, x)


def _router_kernel(x_ref, g_ref, wr_ref, xn_ref, route_ref, *, n_groups, per_group):
    x = x_ref[...]
    r = lax.rsqrt(jnp.mean(x * x, axis=-1, keepdims=True) + EPS)
    xn = x * r * g_ref[...]
    xn_ref[...] = xn
    logits = jnp.dot(xn, wr_ref[...], preferred_element_type=F32, precision=lax.Precision.HIGHEST)
    lane = lax.broadcasted_iota(jnp.int32, logits.shape, 1).astype(F32)
    far = float(LANE)

    def top(vals):
        best = jnp.max(vals, axis=-1, keepdims=True)
        return best, jnp.min(jnp.where(vals == best, lane, far), axis=-1, keepdims=True)

    gl = jnp.where(lane < n_groups, logits, NEG_INF)
    gmax, grp = top(gl)
    p_grp = 1.0 / jnp.sum(jnp.exp(gl - gmax), axis=-1, keepdims=True)
    lo = n_groups + grp * per_group
    el = jnp.where((lane >= lo) & (lane < lo + per_group), logits, NEG_INF)
    t1, i1 = top(el)
    t2, i2 = top(jnp.where(lane == i1, NEG_INF, el))
    d = jnp.exp(t2 - t1)
    w1 = p_grp / (1.0 + d)
    w2 = p_grp * d / (1.0 + d)
    route_ref[...] = jnp.where(lane == 0, i1 - n_groups,
                               jnp.where(lane == 1, i2 - n_groups,
                                         jnp.where(lane == 2, w1, jnp.where(lane == 3, w2, 0.0))))


def _router(x1, g, wr, n_groups, per_group, tm):
    t, d = x1.shape
    blocks = 2 * _nbytes((tm, d), F32) + _nbytes((1, d), F32) + _nbytes((d, LANE), F32) + _nbytes((tm, LANE), F32)
    return pl.pallas_call(
        functools.partial(_router_kernel, n_groups=n_groups, per_group=per_group),
        grid=(t // tm,),
        in_specs=[pl.BlockSpec((tm, d), lambda i: (i, 0)), pl.BlockSpec((1, d), lambda i: (0, 0)),
                  pl.BlockSpec((d, LANE), lambda i: (0, 0))],
        out_specs=[pl.BlockSpec((tm, d), lambda i: (i, 0)), pl.BlockSpec((tm, LANE), lambda i: (i, 0))],
        out_shape=[jax.ShapeDtypeStruct((t, d), F32), jax.ShapeDtypeStruct((t, LANE), F32)],
        compiler_params=_params(("parallel",), blocks, 2 * _nbytes((tm, d), F32)),
        name="router",
    )(x1, g.reshape(1, d), wr)


def _one_hots(route):
    lane = lax.broadcasted_iota(jnp.int32, route.shape, 1).astype(F32)
    return (lane == route[:, 0:1]).astype(F32), (lane == route[:, 1:2]).astype(F32)


def _rank_kernel(route_ref, rank_ref, starts_ref, count_acc, start_acc):
    i = pl.program_id(0)

    @pl.when(i == 0)
    def _():
        count_acc[...] = jnp.zeros_like(count_acc)
        start_acc[...] = jnp.zeros_like(start_acc)

    oh1, oh2 = _one_hots(route_ref[...])
    oh = (oh1 + oh2).astype(BF16)
    tm = oh.shape[0]
    earlier = (lax.broadcasted_iota(jnp.int32, (tm, tm), 0) > lax.broadcasted_iota(jnp.int32, (tm, tm), 1))
    before = jnp.dot(earlier.astype(BF16), oh, preferred_element_type=F32) + count_acc[...]
    lane = lax.broadcasted_iota(jnp.int32, (tm, LANE), 1)
    rank_ref[...] = jnp.where(lane == 0, jnp.sum(oh1 * before, axis=-1, keepdims=True),
                              jnp.where(lane == 1, jnp.sum(oh2 * before, axis=-1, keepdims=True), 0.0))
    lower = (lax.broadcasted_iota(jnp.int32, (LANE, LANE), 0) < lax.broadcasted_iota(jnp.int32, (LANE, LANE), 1))
    below = jnp.dot(oh, lower.astype(BF16), preferred_element_type=F32)
    count_acc[...] += jnp.sum(oh.astype(F32), axis=0, keepdims=True)
    start_acc[...] += jnp.sum(below, axis=0, keepdims=True)
    starts_ref[...] = start_acc[...]


def _rank(route, tm):
    t = route.shape[0]
    blocks = 2 * _nbytes((tm, LANE), F32) + _nbytes((1, LANE), F32)
    return pl.pallas_call(
        _rank_kernel,
        grid=(t // tm,),
        in_specs=[pl.BlockSpec((tm, LANE), lambda i: (i, 0))],
        out_specs=[pl.BlockSpec((tm, LANE), lambda i: (i, 0)), pl.BlockSpec((1, LANE), lambda i: (0, 0))],
        out_shape=[jax.ShapeDtypeStruct((t, LANE), F32), jax.ShapeDtypeStruct((1, LANE), F32)],
        scratch_shapes=[pltpu.VMEM((1, LANE), F32), pltpu.VMEM((1, LANE), F32)],
        compiler_params=_params(("arbitrary",), blocks, 2 * _nbytes((tm, tm), F32)),
        name="moe_rank",
    )(route)


def _dest_kernel(route_ref, rank_ref, starts_ref, dest_ref):
    oh1, oh2 = _one_hots(route_ref[...])
    rank = rank_ref[...]
    starts = starts_ref[...]
    d1 = jnp.sum(oh1 * starts, axis=-1, keepdims=True) + rank[:, 0:1]
    d2 = jnp.sum(oh2 * starts, axis=-1, keepdims=True) + rank[:, 1:2]
    lane = lax.broadcasted_iota(jnp.int32, rank.shape, 1)
    dest_ref[...] = jnp.where(lane == 0, d1, jnp.where(lane == 1, d2, 0.0)).astype(jnp.int32)


def _dest(route, rank, starts, tm):
    t = route.shape[0]
    blocks = 3 * _nbytes((tm, LANE), F32) + _nbytes((1, LANE), F32)
    return pl.pallas_call(
        _dest_kernel,
        grid=(t // tm,),
        in_specs=[pl.BlockSpec((tm, LANE), lambda i: (i, 0)), pl.BlockSpec((tm, LANE), lambda i: (i, 0)),
                  pl.BlockSpec((1, LANE), lambda i: (0, 0))],
        out_specs=pl.BlockSpec((tm, LANE), lambda i: (i, 0)),
        out_shape=jax.ShapeDtypeStruct((t, LANE), jnp.int32),
        compiler_params=_params(("parallel",), blocks, 4 * _nbytes((tm, LANE), F32)),
        name="moe_dest",
    )(route, rank, starts)


def _gather_rows_kernel(dest_ref, x_hbm, xs_ref, source, rows_f32, sem, *, n_assign):
    half = xs_ref.shape[0] // 2
    i = pl.program_id(0)

    def copy(src_row, slot, j):
        return pltpu.make_async_copy(x_hbm.at[pl.ds(src_row, 1)], rows_f32.at[slot, pl.ds(j, 1)], sem.at[slot])

    def start_half(block, slot):
        for j in range(half):
            copy(source[block * half + j], slot, j).start()

    def finish_half(slot):
        for j in range(half):
            copy(0, slot, j).wait()
        xs_ref[slot * half:(slot + 1) * half, :] = rows_f32[slot].astype(xs_ref.dtype)

    @pl.when(i == 0)
    def _():
        def invert(tok, carry):
            for k in range(TOP_K):
                source[dest_ref[TOP_K * tok + k]] = tok
            return carry
        lax.fori_loop(0, n_assign // TOP_K, invert, 0, unroll=4)
        start_half(2 * i, 0)

    start_half(2 * i + 1, 1)
    finish_half(0)

    @pl.when(i + 1 < pl.num_programs(0))
    def _():
        start_half(2 * i + 2, 0)

    finish_half(1)


def _gather_rows(dest_flat, xn, rows):
    t, d = xn.shape
    n_assign = dest_flat.shape[0]
    return pl.pallas_call(
        functools.partial(_gather_rows_kernel, n_assign=n_assign),
        grid_spec=pltpu.PrefetchScalarGridSpec(
            num_scalar_prefetch=1,
            grid=(n_assign // rows,),
            in_specs=[pl.BlockSpec(memory_space=pl.ANY)],
            out_specs=pl.BlockSpec((rows, d), lambda i, dest: (i, 0)),
            scratch_shapes=[pltpu.SMEM((n_assign,), jnp.int32), pltpu.VMEM((2, rows // 2, d), xn.dtype),
                            pltpu.SemaphoreType.DMA((2,))]),
        out_shape=jax.ShapeDtypeStruct((n_assign, d), BF16),
        compiler_params=_params(("arbitrary",), _nbytes((rows, d), BF16), 2 * _nbytes((rows, d), xn.dtype)),
        name="moe_gather_rows",
    )(dest_flat, xn)


def _work_items(starts, n_rows):
    n_exp = starts.shape[0]
    n_blk = n_rows // MOE_ROWS
    total = jnp.full((1,), n_rows, jnp.int32)
    pts = jnp.concatenate([jnp.arange(n_blk, dtype=jnp.int32) * MOE_ROWS, starts[1:]])
    idx = jnp.arange(pts.shape[0], dtype=jnp.int32)
    before = (pts[None, :] < pts[:, None]) | ((pts[None, :] == pts[:, None]) & (idx[None, :] < idx[:, None]))
    pos = jnp.sum(before.astype(jnp.int32), axis=1)
    lo = jnp.sum(jnp.where(pos[:, None] == idx[None, :], pts[:, None], 0), axis=0)
    hi = jnp.concatenate([lo[1:], total])
    ends = jnp.concatenate([starts[1:], total])
    r = jnp.minimum(lo // MOE_ROWS, n_blk - 1)
    e = jnp.minimum(jnp.sum((ends[None, :] <= lo[:, None]).astype(jnp.int32), axis=1), n_exp - 1)
    changed = jnp.concatenate([jnp.ones((1,), jnp.int32), (e[1:] != e[:-1]).astype(jnp.int32)])
    slot = (jnp.sum(jnp.where(idx[None, :] <= idx[:, None], changed[None, :], 0), axis=1) - 1) % 2
    later = jnp.where(e[None, :] > e[:, None], e[None, :], n_exp)
    nxt = jnp.min(later, axis=1)
    nxt = jnp.where(nxt == n_exp, -1, nxt)
    return r, e, lo, hi, slot, nxt


def _stream_expert_weights(w, e_ref, slot_ref, nxt_ref, streams, convert):
    e = e_ref[w]
    slot = slot_ref[w]

    def copies(expert, into):
        out = []
        for hbm, stage, sem in streams:
            rows = stage.shape[1] // WEIGHT_DMA_PARTS
            for part in range(WEIGHT_DMA_PARTS):
                sl = pl.ds(part * rows, rows)
                out.append(pltpu.make_async_copy(hbm.at[expert, sl], stage.at[into, sl], sem.at[into]))
        return out

    @pl.when(w == 0)
    def _():
        for c in copies(e, slot):
            c.start()

    @pl.when((w == 0) | (e != e_ref[jnp.maximum(w - 1, 0)]))
    def _():
        for c in copies(e, slot):
            c.wait()
        nxt = nxt_ref[w]

        @pl.when(nxt >= 0)
        def _():
            for c in copies(nxt, 1 - slot):
                c.start()

        convert(slot)


def _store_item_rows(o_ref, val, r, lo, hi):
    rows = r * MOE_ROWS + lax.broadcasted_iota(jnp.int32, (MOE_ROWS, 1), 0)
    mine = (rows >= lo) & (rows < hi)

    @pl.when(lo == r * MOE_ROWS)
    def _():
        o_ref[...] = val

    @pl.when(lo != r * MOE_ROWS)
    def _():
        o_ref[...] = jnp.where(mine, val, o_ref[...])


def _moe_up_kernel(r_ref, e_ref, lo_ref, hi_ref, slot_ref, nxt_ref, xs_ref, wg_hbm, wu_hbm, h_ref,
                   wg_stage, wu_stage, wg_bf, wu_bf, sem_g, sem_u):
    w = pl.program_id(0)

    def convert(slot):
        wg_bf[...] = wg_stage[slot].astype(BF16)
        wu_bf[...] = wu_stage[slot].astype(BF16)

    _stream_expert_weights(w, e_ref, slot_ref, nxt_ref,
                           [(wg_hbm, wg_stage, sem_g), (wu_hbm, wu_stage, sem_u)], convert)
    r, lo, hi = r_ref[w], lo_ref[w], hi_ref[w]

    @pl.when(hi > lo)
    def _():
        x = xs_ref[...]
        g = jnp.dot(x, wg_bf[...], preferred_element_type=F32)
        u = jnp.dot(x, wu_bf[...], preferred_element_type=F32)
        h = (g * (1.0 / (1.0 + jnp.exp(-g)))) * u
        _store_item_rows(h_ref, h.astype(h_ref.dtype), r, lo, hi)


def _moe_up(items, xs, wg, wu):
    n_rows, d = xs.shape
    _, _, f = wg.shape
    n_items = items[0].shape[0]
    blocks = _nbytes((MOE_ROWS, d), BF16) + _nbytes((MOE_ROWS, f), BF16)
    resident = 4 * _nbytes((d, f), F32) + 2 * _nbytes((d, f), BF16) + 6 * _nbytes((MOE_ROWS, f), F32)
    row_block = lambda w, r, e, lo, hi, slot, nxt: (r[w], 0)
    return pl.pallas_call(
        _moe_up_kernel,
        grid_spec=pltpu.PrefetchScalarGridSpec(
            num_scalar_prefetch=6,
            grid=(n_items,),
            in_specs=[pl.BlockSpec((MOE_ROWS, d), row_block),
                      pl.BlockSpec(memory_space=pl.ANY), pl.BlockSpec(memory_space=pl.ANY)],
            out_specs=pl.BlockSpec((MOE_ROWS, f), row_block),
            scratch_shapes=[pltpu.VMEM((2, d, f), F32), pltpu.VMEM((2, d, f), F32),
                            pltpu.VMEM((d, f), BF16), pltpu.VMEM((d, f), BF16),
                            pltpu.SemaphoreType.DMA((2,)), pltpu.SemaphoreType.DMA((2,))]),
        out_shape=jax.ShapeDtypeStruct((n_rows, f), BF16),
        compiler_params=_params(("arbitrary",), blocks, resident),
        name="moe_up",
    )(*items, xs, wg, wu)


def _moe_down_kernel(r_ref, e_ref, lo_ref, hi_ref, slot_ref, nxt_ref, h_ref, wd_hbm, y_ref, wd_stage, wd_bf, sem):
    w = pl.program_id(0)

    def convert(slot):
        wd_bf[...] = wd_stage[slot].astype(BF16)

    _stream_expert_weights(w, e_ref, slot_ref, nxt_ref, [(wd_hbm, wd_stage, sem)], convert)
    r, lo, hi = r_ref[w], lo_ref[w], hi_ref[w]

    @pl.when(hi > lo)
    def _():
        y = jnp.dot(h_ref[...], wd_bf[...], preferred_element_type=F32)
        _store_item_rows(y_ref, y, r, lo, hi)


def _moe_down(items, h, wd):
    n_rows, f = h.shape
    d = wd.shape[2]
    n_items = items[0].shape[0]
    blocks = _nbytes((MOE_ROWS, f), BF16) + _nbytes((MOE_ROWS, d), F32)
    resident = 2 * _nbytes((f, d), F32) + _nbytes((f, d), BF16) + 3 * _nbytes((MOE_ROWS, d), F32)
    row_block = lambda w, r, e, lo, hi, slot, nxt: (r[w], 0)
    return pl.pallas_call(
        _moe_down_kernel,
        grid_spec=pltpu.PrefetchScalarGridSpec(
            num_scalar_prefetch=6,
            grid=(n_items,),
            in_specs=[pl.BlockSpec((MOE_ROWS, f), row_block), pl.BlockSpec(memory_space=pl.ANY)],
            out_specs=pl.BlockSpec((MOE_ROWS, d), row_block),
            scratch_shapes=[pltpu.VMEM((2, f, d), F32), pltpu.VMEM((f, d), BF16), pltpu.SemaphoreType.DMA((2,))]),
        out_shape=jax.ShapeDtypeStruct((n_rows, d), F32),
        compiler_params=_params(("arbitrary",), blocks, resident),
        name="moe_down",
    )(*items, h, wd)


def _combine_kernel(dest_ref, x_ref, route_ref, y_hbm, o_ref, ybuf, sem, *, tm):
    half = tm // 2
    i = pl.program_id(0)

    def copy(src_row, slot, k, t):
        return pltpu.make_async_copy(y_hbm.at[pl.ds(src_row, 1)], ybuf.at[slot, k, pl.ds(t, 1)], sem.at[slot])

    def start_half(index, slot):
        for t in range(half):
            for k in range(TOP_K):
                copy(dest_ref[TOP_K * (index * half + t) + k], slot, k, t).start()

    def finish_half(slot):
        for t in range(half):
            for k in range(TOP_K):
                copy(0, slot, k, t).wait()
        rows = slice(slot * half, (slot + 1) * half)
        route = route_ref[rows, :]
        o_ref[rows, :] = x_ref[rows, :] + (route[:, 2:3] * ybuf[slot, 0] + route[:, 3:4] * ybuf[slot, 1])

    @pl.when(i == 0)
    def _():
        start_half(2 * i, 0)

    start_half(2 * i + 1, 1)
    finish_half(0)

    @pl.when(i + 1 < pl.num_programs(0))
    def _():
        start_half(2 * i + 2, 0)

    finish_half(1)


def _combine(dest_flat, x1, route, y, tm):
    t, d = x1.shape
    blocks = 2 * _nbytes((tm, d), F32) + _nbytes((tm, LANE), F32)
    resident = TOP_K * _nbytes((tm, d), F32) + _nbytes((tm, d), F32)
    return pl.pallas_call(
        functools.partial(_combine_kernel, tm=tm),
        grid_spec=pltpu.PrefetchScalarGridSpec(
            num_scalar_prefetch=1,
            grid=(t // tm,),
            in_specs=[pl.BlockSpec((tm, d), lambda i, dest: (i, 0)),
                      pl.BlockSpec((tm, LANE), lambda i, dest: (i, 0)),
                      pl.BlockSpec(memory_space=pl.ANY)],
            out_specs=pl.BlockSpec((tm, d), lambda i, dest: (i, 0)),
            scratch_shapes=[pltpu.VMEM((2, TOP_K, tm // 2, d), F32), pltpu.SemaphoreType.DMA((2,))]),
        out_shape=jax.ShapeDtypeStruct((t, d), F32),
        compiler_params=_params(("arbitrary",), blocks, resident),
        name="moe_combine",
    )(dest_flat, x1, route, y)


def _pad_cols(w, n):
    return jnp.pad(w, ((0, 0), (0, n - w.shape[1])))


def kernel(x, positions, g_mix, w_in, b_gate, q_norm_g, kv_norm_g, w_uq, w_ukv, a_q_norm_g, a_k_norm_g,
           b_q_norm_g, b_k_norm_g, rel_bias, w_o_a, w_o_b, w_out, g_ffn, w_group, w_expert,
           w_exp_gate, w_exp_up, w_exp_down):
    batch, seq, d = x.shape
    t = batch * seq
    q_lora, kv_lora = q_norm_g.shape[0], kv_norm_g.shape[0]
    a_heads = w_uq.shape[1] // A_QK
    b_heads = w_o_b.shape[0] // B_HEAD_DIM
    b_width = b_heads * B_HEAD_DIM
    n_groups, n_experts = w_group.shape[1], w_expert.shape[1]
    per_group = n_experts // n_groups
    off_b = q_lora + kv_lora + A_ROPE
    assert seq % ATTN_TILE == 0 and (TOP_K * t) % GATHER_ROWS == 0 and n_groups + n_experts <= LANE
    assert a_heads == b_heads

    xf = x.reshape(t, d)
    tm_big = min(1024, t)
    tn = _tile(b_width, 512)
    assert d % tn == 0

    za_cols = -(-(q_lora + kv_lora + LANE) // tn) * tn
    wuq = jnp.pad(w_uq.reshape(q_lora, a_heads, A_QK), ((0, 0), (0, 0), (0, A_HEAD_PAD - A_QK)))
    wuq = wuq.reshape(q_lora, a_heads * A_HEAD_PAD).astype(BF16)
    wukv = w_ukv.astype(BF16)
    pad_gain = lambda g, s: jnp.pad(g * s, (0, A_HEAD_PAD - A_QK)).reshape(1, A_HEAD_PAD)
    gaq = pad_gain(a_q_norm_g, A_QK ** -0.5)
    gak = pad_gain(a_k_norm_g, 1.0)
    gb = jnp.concatenate([jnp.tile(b_q_norm_g * B_HEAD_DIM ** -0.5, b_heads), jnp.tile(b_k_norm_g, b_heads),
                          jnp.ones((b_width,), F32), b_gate]).reshape(1, -1)

    half = A_ROPE // 2
    inv = ROPE_THETA ** (-jnp.arange(half, dtype=F32) / half)
    ang = positions.astype(F32).reshape(t, 1) * inv
    cos, sin = jnp.cos(ang), jnp.sin(ang)
    zeros = jnp.zeros((t, half), F32)
    cos_t = jnp.concatenate([cos, cos, zeros, zeros], axis=1)
    sin_lo = jnp.concatenate([-sin, zeros, zeros, zeros], axis=1)
    sin_hi = jnp.concatenate([zeros, sin, zeros, zeros], axis=1)

    xn = _rmsnorm_rows(xf, g_mix, BF16, min(256, t))
    w_in_t = w_in.T
    za = _inproj_a(xn, w_in_t, za_cols, tm_big, tn)
    zb = _inproj_b(xn, w_in_t, off_b, gb, tm_big, tn, 2 * b_width // tn, 3 * b_width // tn)
    q, k, v = _mla_proj(za, cos_t, sin_lo, sin_hi, wuq, wukv, q_norm_g.reshape(1, -1), kv_norm_g.reshape(1, -1),
                        gaq, gak, a_heads, min(256, t))
    o_a, o_b = _attention(q, k, v, zb, _band_bias_rows(rel_bias), batch, seq, a_heads)
    merged = _merge(o_a, o_b, w_o_a, w_o_b, zb, 3 * b_width, tm_big, tn)
    x1 = _out_proj(merged, w_out, xf, tm_big, tn)

    wr = _pad_cols(jnp.concatenate([w_group, w_expert], axis=1), LANE)
    xn2, route = _router(x1, g_ffn, wr, n_groups, per_group, min(256, t))
    rank, starts_f = _rank(route, min(512, t))
    dest = _dest(route, rank, starts_f, min(512, t))[:, :TOP_K].reshape(-1)
    xs = _gather_rows(dest, xn2, GATHER_ROWS)
    items = _work_items(starts_f[0, :n_experts].astype(jnp.int32), TOP_K * t)
    h = _moe_up(items, xs, w_exp_gate, w_exp_up)
    y = _moe_down(items, h, w_exp_down)
    out = _combine(dest, x1, route, y, min(256, t))
    return out.reshape(batch, seq, d)
```

```python
import functools

import jax
import jax.numpy as jnp
from jax import lax
from jax.experimental import pallas as pl
from jax.experimental.pallas import tpu as pltpu

F32 = jnp.float32
BF16 = jnp.bfloat16

CHUNK = 64
EPS = 1e-6
A_NOPE = 128
A_ROPE = 64
A_V = 128
A_QK = A_NOPE + A_ROPE
B_HEAD_DIM = 128
B_LEFT_CHUNKS = 8
B_MAX_REL = 128
ROPE_THETA = 10000.0
TOP_K = 2

LANE = 128
A_HEAD_PAD = 2 * LANE
V_PAD = 2 * LANE
V7X_VMEM_BYTES = 64 * 2**20

ATTN_TILE = 256
MOE_ROWS = 128
GATHER_ROWS = 256
WEIGHT_DMA_PARTS = 4
PROJ_ROW_GROUPS = 4
NEG_INF = float("-inf")


def _nbytes(shape, dtype):
    n = 1
    for s in shape:
        n *= s
    return n * jnp.dtype(dtype).itemsize


def _params(semantics, pipelined_bytes, resident_bytes=0):
    need = 2 * pipelined_bytes + resident_bytes
    return pltpu.CompilerParams(dimension_semantics=semantics,
                                vmem_limit_bytes=min(int(need), V7X_VMEM_BYTES))


def _tile(n, want):
    t = want
    while t > LANE and n % t:
        t //= 2
    assert n % t == 0, (n, want)
    return t


def _rmsnorm_kernel(x_ref, g_ref, o_ref):
    x = x_ref[...]
    r = lax.rsqrt(jnp.mean(x * x, axis=-1, keepdims=True) + EPS)
    o_ref[...] = (x * r * g_ref[...]).astype(o_ref.dtype)


def _rmsnorm_rows(x, g, out_dtype, tm):
    t, d = x.shape
    blocks = _nbytes((tm, d), F32) + _nbytes((tm, d), out_dtype) + _nbytes((1, d), F32)
    return pl.pallas_call(
        _rmsnorm_kernel,
        grid=(t // tm,),
        in_specs=[pl.BlockSpec((tm, d), lambda i: (i, 0)), pl.BlockSpec((1, d), lambda i: (0, 0))],
        out_specs=pl.BlockSpec((tm, d), lambda i: (i, 0)),
        out_shape=jax.ShapeDtypeStruct((t, d), out_dtype),
        compiler_params=_params(("parallel",), blocks, _nbytes((tm, d), F32)),
        name="rmsnorm",
    )(x, g.reshape(1, d))


def _nt_dot(a, b):
    return lax.dot_general(a, b, (((1,), (1,)), ((), ())), preferred_element_type=F32)


def _inproj_a_kernel(a_ref, wt_ref, o_ref, wt_bf):
    @pl.when(pl.program_id(1) == 0)
    def _():
        wt_bf[...] = wt_ref[...].astype(BF16)

    o_ref[...] = _nt_dot(a_ref[...], wt_bf[...])


def _inproj_a(xn, w_in_t, n_cols, tm, tn):
    m, k = xn.shape
    blocks = _nbytes((tm, k), BF16) + _nbytes((tn, k), F32) + _nbytes((tm, tn), F32)
    return pl.pallas_call(
        _inproj_a_kernel,
        grid=(n_cols // tn, m // tm),
        in_specs=[pl.BlockSpec((tm, k), lambda j, i: (i, 0)), pl.BlockSpec((tn, k), lambda j, i: (j, 0))],
        out_specs=pl.BlockSpec((tm, tn), lambda j, i: (i, j)),
        out_shape=jax.ShapeDtypeStruct((m, n_cols), F32),
        scratch_shapes=[pltpu.VMEM((tn, k), BF16)],
        compiler_params=_params(("parallel", "arbitrary"), blocks, 2 * _nbytes((tn, k), BF16)),
        name="inproj_a",
    )(xn, w_in_t)


def _head_norm_epilogue(acc, gb_ref):
    heads = []
    for h in range(acc.shape[1] // B_HEAD_DIM):
        sl = slice(h * B_HEAD_DIM, (h + 1) * B_HEAD_DIM)
        z = acc[:, sl]
        r = lax.rsqrt(jnp.mean(z * z, axis=-1, keepdims=True) + EPS)
        heads.append(z * r * gb_ref[:, sl])
    return jnp.concatenate(heads, axis=1)


def _plain_epilogue(acc, gb_ref):
    del gb_ref
    return acc


def _sigmoid_epilogue(acc, gb_ref):
    return 1.0 / (1.0 + jnp.exp(-(acc + gb_ref[...])))


def _inproj_cols_kernel(a_ref, wt_hbm, *refs, first_col, epilogue):
    gb_ref = refs[0] if len(refs) == 5 else None
    o_ref, stage, wt_bf, sem = refs[-4:]
    j = pl.program_id(0)
    tn = wt_bf.shape[0]

    def fetch(block):
        rows = pl.ds(pl.multiple_of(first_col + block * tn, 8), tn)
        return pltpu.make_async_copy(wt_hbm.at[rows], stage, sem)

    @pl.when(pl.program_id(1) == 0)
    def _():
        @pl.when(j == 0)
        def _():
            fetch(j).start()

        fetch(j).wait()
        wt_bf[...] = stage[...].astype(BF16)

        @pl.when(j + 1 < pl.num_programs(0))
        def _():
            fetch(j + 1).start()

    rows = a_ref.shape[0] // PROJ_ROW_GROUPS
    groups = [slice(s * rows, (s + 1) * rows) for s in range(PROJ_ROW_GROUPS)]
    accs = [_nt_dot(a_ref[sl, :], wt_bf[...]) for sl in groups]
    for sl, acc in zip(groups, accs):
        o_ref[sl, :] = epilogue(acc, gb_ref).astype(o_ref.dtype)


def _inproj_cols(xn, w_in_t, first_col, n_cols, gb, epilogue, tm, tn):
    m, k = xn.shape
    assert n_cols % tn == 0 and first_col % 8 == 0 and tm % PROJ_ROW_GROUPS == 0
    blocks = _nbytes((tm, k), BF16) + _nbytes((tm, tn), BF16) + _nbytes((1, tn), F32)
    resident = _nbytes((tn, k), F32) + 2 * _nbytes((tn, k), BF16) + 3 * _nbytes((tm, tn), F32)
    gb_specs = [] if gb is None else [pl.BlockSpec((1, tn), lambda j, i: (0, j))]
    gb_args = [] if gb is None else [gb.reshape(1, n_cols)]
    return pl.pallas_call(
        functools.partial(_inproj_cols_kernel, first_col=first_col, epilogue=epilogue),
        grid=(n_cols // tn, m // tm),
        in_specs=[pl.BlockSpec((tm, k), lambda j, i: (i, 0)), pl.BlockSpec(memory_space=pl.ANY)] + gb_specs,
        out_specs=pl.BlockSpec((tm, tn), lambda j, i: (i, j)),
        out_shape=jax.ShapeDtypeStruct((m, n_cols), BF16),
        scratch_shapes=[pltpu.VMEM((tn, k), F32), pltpu.VMEM((tn, k), BF16), pltpu.SemaphoreType.DMA(())],
        compiler_params=_params(("arbitrary", "arbitrary"), blocks, resident),
        name="inproj_" + epilogue.__name__.strip("_").replace("_epilogue", ""),
    )(xn, w_in_t, *gb_args)


def _rope_padded(v, cos, sin_lo, sin_hi):
    half = A_ROPE // 2
    return v * cos + pltpu.roll(v, LANE - half, 1) * sin_lo + pltpu.roll(v, half, 1) * sin_hi


def _ones_column(rows):
    lane = lax.broadcasted_iota(jnp.int32, (rows, V_PAD - A_V), 1)
    return jnp.where(lane == 0, 1.0, 0.0).astype(BF16)


def _mla_proj_kernel(za_ref, cos_ref, sl_ref, sh_ref, wuq_ref, wukv_ref, gq_ref, gkv_ref, gaq_ref, gak_ref,
                     q_ref, k_ref, v_ref, *, heads, q_lora, kv_lora):
    cos, sin_lo, sin_hi = cos_ref[...], sl_ref[...], sh_ref[...]

    def norm(z, g):
        r = lax.rsqrt(jnp.mean(z * z, axis=-1, keepdims=True) + EPS)
        return (z * r * g).astype(BF16)

    cq = norm(za_ref[:, :q_lora], gq_ref[...])
    ckv = norm(za_ref[:, q_lora:q_lora + kv_lora], gkv_ref[...])
    slab = za_ref[:, q_lora + kv_lora:q_lora + kv_lora + LANE]
    k_rope = jnp.where(lax.broadcasted_iota(jnp.int32, slab.shape, 1) < A_ROPE, slab, 0.0)
    qacc = jnp.dot(cq, wuq_ref[...], preferred_element_type=F32)
    kvacc = jnp.dot(ckv, wukv_ref[...], preferred_element_type=F32)

    gq_lo, gq_hi = gaq_ref[:, :LANE], gaq_ref[:, LANE:]
    gk_lo, gk_hi = gak_ref[:, :LANE], gak_ref[:, LANE:]
    kr_ss = jnp.sum(k_rope * k_rope, axis=-1, keepdims=True)
    kr_rot = _rope_padded(k_rope * gk_hi, cos, sin_lo, sin_hi)
    ones_col = _ones_column(slab.shape[0])
    for h in range(heads):
        base = h * A_HEAD_PAD
        q_lo = qacc[:, base:base + LANE]
        q_hi = qacc[:, base + LANE:base + A_HEAD_PAD]
        ss = jnp.sum(q_lo * q_lo, axis=-1, keepdims=True) + jnp.sum(q_hi * q_hi, axis=-1, keepdims=True)
        r = lax.rsqrt(ss / A_QK + EPS)
        q_ref[:, base:base + LANE] = (q_lo * r * gq_lo).astype(BF16)
        q_ref[:, base + LANE:base + A_HEAD_PAD] = _rope_padded(q_hi * r * gq_hi, cos, sin_lo, sin_hi).astype(BF16)

        k_lo = kvacc[:, base:base + LANE]
        ssk = jnp.sum(k_lo * k_lo, axis=-1, keepdims=True) + kr_ss
        rk = lax.rsqrt(ssk / A_QK + EPS)
        k_ref[:, base:base + LANE] = (k_lo * rk * gk_lo).astype(BF16)
        k_ref[:, base + LANE:base + A_HEAD_PAD] = (kr_rot * rk).astype(BF16)
        v_ref[:, h * V_PAD:h * V_PAD + A_V] = kvacc[:, base + LANE:base + A_HEAD_PAD].astype(BF16)
        v_ref[:, h * V_PAD + A_V:(h + 1) * V_PAD] = ones_col


def _mla_proj(za, cos, sin_lo, sin_hi, wuq, wukv, gq, gkv, gaq, gak, heads, tm):
    t, za_cols = za.shape
    q_lora, kv_lora = wuq.shape[0], wukv.shape[0]
    hp = heads * A_HEAD_PAD
    row = lambda i: (i, 0)
    fix = lambda i: (0, 0)
    blocks = (_nbytes((tm, za_cols), F32) + 3 * _nbytes((tm, LANE), F32) + _nbytes(wuq.shape, BF16)
              + _nbytes(wukv.shape, BF16) + 2 * _nbytes((tm, hp), BF16) + _nbytes((tm, heads * V_PAD), BF16))
    return pl.pallas_call(
        functools.partial(_mla_proj_kernel, heads=heads, q_lora=q_lora, kv_lora=kv_lora),
        grid=(t // tm,),
        in_specs=[pl.BlockSpec((tm, za_cols), row),
                  pl.BlockSpec((tm, LANE), row), pl.BlockSpec((tm, LANE), row), pl.BlockSpec((tm, LANE), row),
                  pl.BlockSpec(wuq.shape, fix), pl.BlockSpec(wukv.shape, fix),
                  pl.BlockSpec((1, q_lora), fix), pl.BlockSpec((1, kv_lora), fix),
                  pl.BlockSpec((1, A_HEAD_PAD), fix), pl.BlockSpec((1, A_HEAD_PAD), fix)],
        out_specs=[pl.BlockSpec((tm, hp), row), pl.BlockSpec((tm, hp), row), pl.BlockSpec((tm, heads * V_PAD), row)],
        out_shape=[jax.ShapeDtypeStruct((t, hp), BF16), jax.ShapeDtypeStruct((t, hp), BF16),
                   jax.ShapeDtypeStruct((t, heads * V_PAD), BF16)],
        compiler_params=_params(("parallel",), blocks, 3 * _nbytes((tm, hp), F32)),
        name="mla_proj",
    )(za, cos, sin_lo, sin_hi, wuq, wukv, gq, gkv, gaq, gak)


def _mla_scores(i, q_ref, k_ref, diag_ok):
    tq = ATTN_TILE
    q = q_ref[i * tq:(i + 1) * tq, :]
    sd = jnp.where(diag_ok, _nt_dot(q, k_ref[i * tq:(i + 1) * tq, :]), NEG_INF)
    s0 = _nt_dot(q, k_ref[:i * tq, :]) if i > 0 else None
    return sd, s0


def _mla_softmax(sd, s0):
    m = jnp.max(sd, axis=-1, keepdims=True)
    if s0 is None:
        return jnp.exp(sd - m).astype(BF16), None
    m = jnp.maximum(m, jnp.max(s0, axis=-1, keepdims=True))
    return jnp.exp(sd - m).astype(BF16), jnp.exp(s0 - m).astype(BF16)


def _mla_values(i, pd, p0, v_ref, o_ref):
    tq = ATTN_TILE
    o = jnp.dot(pd, v_ref[i * tq:(i + 1) * tq, :], preferred_element_type=F32)
    if p0 is not None:
        o = o + jnp.dot(p0, v_ref[:i * tq, :], preferred_element_type=F32)
    o_ref[i * tq:(i + 1) * tq, :] = (o[:, :A_V] / o[:, A_V:A_V + 1]).astype(o_ref.dtype)


def _band_window(i):
    left = B_LEFT_CHUNKS * CHUNK
    q0 = i * ATTN_TILE
    k0 = max(0, q0 - left)
    return q0, k0, q0 + ATTN_TILE - k0, left - q0 + k0


def _band_scores(i, q_ref, k_ref, table):
    q0, k0, kw, u0 = _band_window(i)
    return _nt_dot(q_ref[q0:q0 + ATTN_TILE, :], k_ref[k0:k0 + kw, :]) + table[:, u0:u0 + kw]


def _band_softmax(s):
    return jnp.exp(s - jnp.max(s, axis=-1, keepdims=True)).astype(BF16)


def _band_values(i, p, v_ref, o_ref):
    q0, k0, kw, _ = _band_window(i)
    o = jnp.dot(p, v_ref[k0:k0 + kw, :], preferred_element_type=F32)
    o_ref[q0:q0 + ATTN_TILE, :] = (o[:, :B_HEAD_DIM] / o[:, B_HEAD_DIM:B_HEAD_DIM + 1]).astype(o_ref.dtype)


def _attn_kernel(aq_ref, ak_ref, av_ref, bq_ref, bk_ref, bv_ref, r_ref, oa_ref, ob_ref, bv_pad, *, seq):
    tq = ATTN_TILE
    left = B_LEFT_CHUNKS * CHUNK
    width = r_ref.shape[1]
    rc = lax.broadcasted_iota(jnp.int32, (tq, tq), 0) // CHUNK
    cc = lax.broadcasted_iota(jnp.int32, (tq, tq), 1) // CHUNK
    diag_ok = cc <= rc
    bias = pltpu.roll(jnp.broadcast_to(r_ref[...], (tq, width)), 0, 1, stride=1, stride_axis=0)[:, :left + tq]
    q_chunk = lax.broadcasted_iota(jnp.int32, (tq, left + tq), 0) // CHUNK
    k_chunk = lax.broadcasted_iota(jnp.int32, (tq, left + tq), 1) // CHUNK
    table = jnp.where((k_chunk >= q_chunk) & (k_chunk <= q_chunk + B_LEFT_CHUNKS), bias, NEG_INF)
    bv_pad[:, :B_HEAD_DIM] = bv_ref[...]
    bv_pad[:, B_HEAD_DIM:] = _ones_column(seq)

    n_tiles = seq // tq
    sd, s0 = _mla_scores(0, aq_ref, ak_ref, diag_ok)
    for i in range(n_tiles):
        sb = _band_scores(i, bq_ref, bk_ref, table)
        pd, p0 = _mla_softmax(sd, s0)
        if i + 1 < n_tiles:
            sd, s0 = _mla_scores(i + 1, aq_ref, ak_ref, diag_ok)
        _mla_values(i, pd, p0, av_ref, oa_ref)
        _band_values(i, _band_softmax(sb), bv_pad, ob_ref)


def _band_bias_rows(rel_bias):
    left = B_LEFT_CHUNKS * CHUNK
    width = left + 2 * ATTN_TILE
    m = jnp.arange(width, dtype=jnp.int32)
    j = jnp.where(m < left + ATTN_TILE, m, m - width)
    dist = left - j
    rows = rel_bias[:, jnp.clip(dist, -B_MAX_REL, B_MAX_REL) + B_MAX_REL].astype(F32)
    return rows.reshape(rel_bias.shape[0], 1, width)


def _attention(q, k, v, bqk, bv, bias_rows, batch, seq, heads):
    t = batch * seq
    d = B_HEAD_DIM
    width = bias_rows.shape[2]
    head = lambda b, h: (b, h)
    blocks = (2 * _nbytes((seq, A_HEAD_PAD), BF16) + _nbytes((seq, V_PAD), BF16) + _nbytes((seq, A_V), BF16)
              + 4 * _nbytes((seq, d), BF16) + _nbytes((1, width), F32))
    temps = _nbytes((seq, V_PAD), BF16) + 8 * _nbytes((ATTN_TILE, seq), F32) + 8 * _nbytes((ATTN_TILE, width), F32)
    return pl.pallas_call(
        functools.partial(_attn_kernel, seq=seq),
        grid=(batch, heads),
        in_specs=[pl.BlockSpec((seq, A_HEAD_PAD), head), pl.BlockSpec((seq, A_HEAD_PAD), head),
                  pl.BlockSpec((seq, V_PAD), head),
                  pl.BlockSpec((seq, d), head),
                  pl.BlockSpec((seq, d), lambda b, h: (b, heads + h)),
                  pl.BlockSpec((seq, d), head),
                  pl.BlockSpec((None, 1, width), lambda b, h: (h, 0, 0))],
        out_specs=[pl.BlockSpec((seq, A_V), head), pl.BlockSpec((seq, d), head)],
        out_shape=[jax.ShapeDtypeStruct((t, heads * A_V), BF16), jax.ShapeDtypeStruct((t, heads * d), BF16)],
        scratch_shapes=[pltpu.VMEM((seq, V_PAD), BF16)],
        compiler_params=_params(("parallel", "parallel"), blocks, temps),
        name="attention",
    )(q, k, v, bqk, bqk, bv, bias_rows)


def _merge_kernel(oa_ref, ob_ref, woa_ref, wob_ref, g0_ref, g1_ref, o_ref, woa_bf, wob_bf):
    @pl.when(pl.program_id(1) == 0)
    def _():
        woa_bf[...] = woa_ref[...].astype(BF16)
        wob_bf[...] = wob_ref[...].astype(BF16)

    a = jnp.dot(oa_ref[...], woa_bf[...], preferred_element_type=F32)
    b = jnp.dot(ob_ref[...], wob_bf[...], preferred_element_type=F32)
    o_ref[...] = (g0_ref[...].astype(F32) * a + g1_ref[...].astype(F32) * b).astype(o_ref.dtype)


def _merge(oa, ob, woa, wob, gates, tm, tn):
    m = oa.shape[0]
    d = woa.shape[1]
    g1 = d // tn
    blocks = (_nbytes((tm, oa.shape[1]), BF16) + _nbytes((tm, ob.shape[1]), BF16) + _nbytes((woa.shape[0], tn), F32)
              + _nbytes((wob.shape[0], tn), F32) + 3 * _nbytes((tm, tn), BF16))
    resident = (2 * _nbytes((woa.shape[0], tn), BF16) + 2 * _nbytes((wob.shape[0], tn), BF16)
                + 3 * _nbytes((tm, tn), F32))
    return pl.pallas_call(
        _merge_kernel,
        grid=(d // tn, m // tm),
        in_specs=[pl.BlockSpec((tm, oa.shape[1]), lambda j, i: (i, 0)),
                  pl.BlockSpec((tm, ob.shape[1]), lambda j, i: (i, 0)),
                  pl.BlockSpec((woa.shape[0], tn), lambda j, i: (0, j)),
                  pl.BlockSpec((wob.shape[0], tn), lambda j, i: (0, j)),
                  pl.BlockSpec((tm, tn), lambda j, i: (i, j)),
                  pl.BlockSpec((tm, tn), lambda j, i: (i, g1 + j))],
        out_specs=pl.BlockSpec((tm, tn), lambda j, i: (i, j)),
        out_shape=jax.ShapeDtypeStruct((m, d), BF16),
        scratch_shapes=[pltpu.VMEM((woa.shape[0], tn), BF16), pltpu.VMEM((wob.shape[0], tn), BF16)],
        compiler_params=_params(("parallel", "arbitrary"), blocks, resident),
        name="merge",
    )(oa, ob, woa, wob, gates, gates)


def _out_proj_kernel(a_ref, w_ref, x_ref, o_ref, w_bf):
    @pl.when(pl.program_id(1) == 0)
    def _():
        w_bf[...] = w_ref[...].astype(BF16)

    o_ref[...] = x_ref[...] + jnp.dot(a_ref[...], w_bf[...], preferred_element_type=F32)


def _out_proj(merged, wout, x, tm, tn):
    m, k = merged.shape
    n = wout.shape[1]
    blocks = _nbytes((tm, k), BF16) + _nbytes((k, tn), F32) + 2 * _nbytes((tm, tn), F32)
    return pl.pallas_call(
        _out_proj_kernel,
        grid=(n // tn, m // tm),
        in_specs=[pl.BlockSpec((tm, k), lambda j, i: (i, 0)),
                  pl.BlockSpec((k, tn), lambda j, i: (0, j)),
                  pl.BlockSpec((tm, tn), lambda j, i: (i, j))],
        out_specs=pl.BlockSpec((tm, tn), lambda j, i: (i, j)),
        out_shape=jax.ShapeDtypeStruct((m, n), F32),
        scratch_shapes=[pltpu.VMEM((k, tn), BF16)],
        compiler_params=_params(("parallel", "arbitrary"), blocks, 2 * _nbytes((k, tn), BF16) + _nbytes((tm, tn), F32)),
        name="out_proj",
    )(merged, wout, x)


def _router_kernel(x_ref, g_ref, wr_ref, xn_ref, route_ref, *, n_groups, per_group):
    x = x_ref[...]
    r = lax.rsqrt(jnp.mean(x * x, axis=-1, keepdims=True) + EPS)
    xn = x * r * g_ref[...]
    xn_ref[...] = xn
    logits = jnp.dot(xn, wr_ref[...], preferred_element_type=F32, precision=lax.Precision.HIGHEST)
    lane = lax.broadcasted_iota(jnp.int32, logits.shape, 1).astype(F32)
    far = float(LANE)

    def top(vals):
        best = jnp.max(vals, axis=-1, keepdims=True)
        return best, jnp.min(jnp.where(vals == best, lane, far), axis=-1, keepdims=True)

    gl = jnp.where(lane < n_groups, logits, NEG_INF)
    gmax, grp = top(gl)
    p_grp = 1.0 / jnp.sum(jnp.exp(gl - gmax), axis=-1, keepdims=True)
    lo = n_groups + grp * per_group
    el = jnp.where((lane >= lo) & (lane < lo + per_group), logits, NEG_INF)
    t1, i1 = top(el)
    t2, i2 = top(jnp.where(lane == i1, NEG_INF, el))
    d = jnp.exp(t2 - t1)
    w1 = p_grp / (1.0 + d)
    w2 = p_grp * d / (1.0 + d)
    route_ref[...] = jnp.where(lane == 0, i1 - n_groups,
                               jnp.where(lane == 1, i2 - n_groups,
                                         jnp.where(lane == 2, w1, jnp.where(lane == 3, w2, 0.0))))


def _router(x1, g, wr, n_groups, per_group, tm):
    t, d = x1.shape
    blocks = 2 * _nbytes((tm, d), F32) + _nbytes((1, d), F32) + _nbytes((d, LANE), F32) + _nbytes((tm, LANE), F32)
    return pl.pallas_call(
        functools.partial(_router_kernel, n_groups=n_groups, per_group=per_group),
        grid=(t // tm,),
        in_specs=[pl.BlockSpec((tm, d), lambda i: (i, 0)), pl.BlockSpec((1, d), lambda i: (0, 0)),
                  pl.BlockSpec((d, LANE), lambda i: (0, 0))],
        out_specs=[pl.BlockSpec((tm, d), lambda i: (i, 0)), pl.BlockSpec((tm, LANE), lambda i: (i, 0))],
        out_shape=[jax.ShapeDtypeStruct((t, d), F32), jax.ShapeDtypeStruct((t, LANE), F32)],
        compiler_params=_params(("parallel",), blocks, 2 * _nbytes((tm, d), F32)),
        name="router",
    )(x1, g.reshape(1, d), wr)


def _one_hots(route):
    lane = lax.broadcasted_iota(jnp.int32, route.shape, 1).astype(F32)
    return (lane == route[:, 0:1]).astype(F32), (lane == route[:, 1:2]).astype(F32)


def _rank_kernel(route_ref, rank_ref, starts_ref, count_acc, start_acc):
    i = pl.program_id(0)

    @pl.when(i == 0)
    def _():
        count_acc[...] = jnp.zeros_like(count_acc)
        start_acc[...] = jnp.zeros_like(start_acc)

    oh1, oh2 = _one_hots(route_ref[...])
    oh = (oh1 + oh2).astype(BF16)
    tm = oh.shape[0]
    earlier = (lax.broadcasted_iota(jnp.int32, (tm, tm), 0) > lax.broadcasted_iota(jnp.int32, (tm, tm), 1))
    before = jnp.dot(earlier.astype(BF16), oh, preferred_element_type=F32) + count_acc[...]
    lane = lax.broadcasted_iota(jnp.int32, (tm, LANE), 1)
    rank_ref[...] = jnp.where(lane == 0, jnp.sum(oh1 * before, axis=-1, keepdims=True),
                              jnp.where(lane == 1, jnp.sum(oh2 * before, axis=-1, keepdims=True), 0.0))
    lower = (lax.broadcasted_iota(jnp.int32, (LANE, LANE), 0) < lax.broadcasted_iota(jnp.int32, (LANE, LANE), 1))
    below = jnp.dot(oh, lower.astype(BF16), preferred_element_type=F32)
    count_acc[...] += jnp.sum(oh.astype(F32), axis=0, keepdims=True)
    start_acc[...] += jnp.sum(below, axis=0, keepdims=True)
    starts_ref[...] = start_acc[...]


def _rank(route, tm):
    t = route.shape[0]
    blocks = 2 * _nbytes((tm, LANE), F32) + _nbytes((1, LANE), F32)
    return pl.pallas_call(
        _rank_kernel,
        grid=(t // tm,),
        in_specs=[pl.BlockSpec((tm, LANE), lambda i: (i, 0))],
        out_specs=[pl.BlockSpec((tm, LANE), lambda i: (i, 0)), pl.BlockSpec((1, LANE), lambda i: (0, 0))],
        out_shape=[jax.ShapeDtypeStruct((t, LANE), F32), jax.ShapeDtypeStruct((1, LANE), F32)],
        scratch_shapes=[pltpu.VMEM((1, LANE), F32), pltpu.VMEM((1, LANE), F32)],
        compiler_params=_params(("arbitrary",), blocks, 2 * _nbytes((tm, tm), F32)),
        name="moe_rank",
    )(route)


def _dest_kernel(route_ref, rank_ref, starts_ref, dest_ref):
    oh1, oh2 = _one_hots(route_ref[...])
    rank = rank_ref[...]
    starts = starts_ref[...]
    d1 = jnp.sum(oh1 * starts, axis=-1, keepdims=True) + rank[:, 0:1]
    d2 = jnp.sum(oh2 * starts, axis=-1, keepdims=True) + rank[:, 1:2]
    lane = lax.broadcasted_iota(jnp.int32, rank.shape, 1)
    dest_ref[...] = jnp.where(lane == 0, d1, jnp.where(lane == 1, d2, 0.0)).astype(jnp.int32)


def _dest(route, rank, starts, tm):
    t = route.shape[0]
    blocks = 3 * _nbytes((tm, LANE), F32) + _nbytes((1, LANE), F32)
    return pl.pallas_call(
        _dest_kernel,
        grid=(t // tm,),
        in_specs=[pl.BlockSpec((tm, LANE), lambda i: (i, 0)), pl.BlockSpec((tm, LANE), lambda i: (i, 0)),
                  pl.BlockSpec((1, LANE), lambda i: (0, 0))],
        out_specs=pl.BlockSpec((tm, LANE), lambda i: (i, 0)),
        out_shape=jax.ShapeDtypeStruct((t, LANE), jnp.int32),
        compiler_params=_params(("parallel",), blocks, 4 * _nbytes((tm, LANE), F32)),
        name="moe_dest",
    )(route, rank, starts)


def _gather_rows_kernel(dest_ref, x_hbm, xs_ref, source, rows_f32, sem, *, n_assign):
    half = xs_ref.shape[0] // 2
    i = pl.program_id(0)

    def copy(src_row, slot, j):
        return pltpu.make_async_copy(x_hbm.at[pl.ds(src_row, 1)], rows_f32.at[slot, pl.ds(j, 1)], sem.at[slot])

    def start_half(block, slot):
        for j in range(half):
            copy(source[block * half + j], slot, j).start()

    def finish_half(slot):
        for j in range(half):
            copy(0, slot, j).wait()
        xs_ref[slot * half:(slot + 1) * half, :] = rows_f32[slot].astype(xs_ref.dtype)

    @pl.when(i == 0)
    def _():
        def invert(tok, carry):
            for k in range(TOP_K):
                source[dest_ref[TOP_K * tok + k]] = tok
            return carry
        lax.fori_loop(0, n_assign // TOP_K, invert, 0, unroll=4)
        start_half(2 * i, 0)

    start_half(2 * i + 1, 1)
    finish_half(0)

    @pl.when(i + 1 < pl.num_programs(0))
    def _():
        start_half(2 * i + 2, 0)

    finish_half(1)


def _gather_rows(dest_flat, xn, rows):
    t, d = xn.shape
    n_assign = dest_flat.shape[0]
    return pl.pallas_call(
        functools.partial(_gather_rows_kernel, n_assign=n_assign),
        grid_spec=pltpu.PrefetchScalarGridSpec(
            num_scalar_prefetch=1,
            grid=(n_assign // rows,),
            in_specs=[pl.BlockSpec(memory_space=pl.ANY)],
            out_specs=pl.BlockSpec((rows, d), lambda i, dest: (i, 0)),
            scratch_shapes=[pltpu.SMEM((n_assign,), jnp.int32), pltpu.VMEM((2, rows // 2, d), xn.dtype),
                            pltpu.SemaphoreType.DMA((2,))]),
        out_shape=jax.ShapeDtypeStruct((n_assign, d), BF16),
        compiler_params=_params(("arbitrary",), _nbytes((rows, d), BF16), 2 * _nbytes((rows, d), xn.dtype)),
        name="moe_gather_rows",
    )(dest_flat, xn)


def _work_items(starts, n_rows):
    n_exp = starts.shape[0]
    n_blk = n_rows // MOE_ROWS
    total = jnp.full((1,), n_rows, jnp.int32)
    pts = jnp.concatenate([jnp.arange(n_blk, dtype=jnp.int32) * MOE_ROWS, starts[1:]])
    idx = jnp.arange(pts.shape[0], dtype=jnp.int32)
    before = (pts[None, :] < pts[:, None]) | ((pts[None, :] == pts[:, None]) & (idx[None, :] < idx[:, None]))
    pos = jnp.sum(before.astype(jnp.int32), axis=1)
    lo = jnp.sum(jnp.where(pos[:, None] == idx[None, :], pts[:, None], 0), axis=0)
    hi = jnp.concatenate([lo[1:], total])
    ends = jnp.concatenate([starts[1:], total])
    r = jnp.minimum(lo // MOE_ROWS, n_blk - 1)
    e = jnp.minimum(jnp.sum((ends[None, :] <= lo[:, None]).astype(jnp.int32), axis=1), n_exp - 1)
    changed = jnp.concatenate([jnp.ones((1,), jnp.int32), (e[1:] != e[:-1]).astype(jnp.int32)])
    slot = (jnp.sum(jnp.where(idx[None, :] <= idx[:, None], changed[None, :], 0), axis=1) - 1) % 2
    later = jnp.where(e[None, :] > e[:, None], e[None, :], n_exp)
    nxt = jnp.min(later, axis=1)
    nxt = jnp.where(nxt == n_exp, -1, nxt)
    return r, e, lo, hi, slot, nxt


def _stream_expert_weights(w, e_ref, slot_ref, nxt_ref, streams, convert):
    e = e_ref[w]
    slot = slot_ref[w]

    def copies(expert, into):
        out = []
        for hbm, stage, sem in streams:
            rows = stage.shape[1] // WEIGHT_DMA_PARTS
            for part in range(WEIGHT_DMA_PARTS):
                sl = pl.ds(part * rows, rows)
                out.append(pltpu.make_async_copy(hbm.at[expert, sl], stage.at[into, sl], sem.at[into]))
        return out

    @pl.when(w == 0)
    def _():
        for c in copies(e, slot):
            c.start()

    @pl.when((w == 0) | (e != e_ref[jnp.maximum(w - 1, 0)]))
    def _():
        for c in copies(e, slot):
            c.wait()
        nxt = nxt_ref[w]

        @pl.when(nxt >= 0)
        def _():
            for c in copies(nxt, 1 - slot):
                c.start()

        convert(slot)


def _store_item_rows(o_ref, val, r, lo, hi):
    rows = r * MOE_ROWS + lax.broadcasted_iota(jnp.int32, (MOE_ROWS, 1), 0)
    mine = (rows >= lo) & (rows < hi)

    @pl.when(lo == r * MOE_ROWS)
    def _():
        o_ref[...] = val

    @pl.when(lo != r * MOE_ROWS)
    def _():
        o_ref[...] = jnp.where(mine, val, o_ref[...])


def _moe_up_kernel(r_ref, e_ref, lo_ref, hi_ref, slot_ref, nxt_ref, xs_ref, wg_hbm, wu_hbm, h_ref,
                   wg_stage, wu_stage, wg_bf, wu_bf, sem_g, sem_u):
    w = pl.program_id(0)

    def convert(slot):
        wg_bf[...] = wg_stage[slot].astype(BF16)
        wu_bf[...] = wu_stage[slot].astype(BF16)

    _stream_expert_weights(w, e_ref, slot_ref, nxt_ref,
                           [(wg_hbm, wg_stage, sem_g), (wu_hbm, wu_stage, sem_u)], convert)
    r, lo, hi = r_ref[w], lo_ref[w], hi_ref[w]

    @pl.when(hi > lo)
    def _():
        x = xs_ref[...]
        g = jnp.dot(x, wg_bf[...], preferred_element_type=F32)
        u = jnp.dot(x, wu_bf[...], preferred_element_type=F32)
        h = (g * (1.0 / (1.0 + jnp.exp(-g)))) * u
        _store_item_rows(h_ref, h.astype(h_ref.dtype), r, lo, hi)


def _moe_up(items, xs, wg, wu):
    n_rows, d = xs.shape
    _, _, f = wg.shape
    n_items = items[0].shape[0]
    blocks = _nbytes((MOE_ROWS, d), BF16) + _nbytes((MOE_ROWS, f), BF16)
    resident = 4 * _nbytes((d, f), F32) + 2 * _nbytes((d, f), BF16) + 6 * _nbytes((MOE_ROWS, f), F32)
    row_block = lambda w, r, e, lo, hi, slot, nxt: (r[w], 0)
    return pl.pallas_call(
        _moe_up_kernel,
        grid_spec=pltpu.PrefetchScalarGridSpec(
            num_scalar_prefetch=6,
            grid=(n_items,),
            in_specs=[pl.BlockSpec((MOE_ROWS, d), row_block),
                      pl.BlockSpec(memory_space=pl.ANY), pl.BlockSpec(memory_space=pl.ANY)],
            out_specs=pl.BlockSpec((MOE_ROWS, f), row_block),
            scratch_shapes=[pltpu.VMEM((2, d, f), F32), pltpu.VMEM((2, d, f), F32),
                            pltpu.VMEM((d, f), BF16), pltpu.VMEM((d, f), BF16),
                            pltpu.SemaphoreType.DMA((2,)), pltpu.SemaphoreType.DMA((2,))]),
        out_shape=jax.ShapeDtypeStruct((n_rows, f), BF16),
        compiler_params=_params(("arbitrary",), blocks, resident),
        name="moe_up",
    )(*items, xs, wg, wu)


def _moe_down_kernel(r_ref, e_ref, lo_ref, hi_ref, slot_ref, nxt_ref, h_ref, wd_hbm, y_ref, wd_stage, wd_bf, sem):
    w = pl.program_id(0)

    def convert(slot):
        wd_bf[...] = wd_stage[slot].astype(BF16)

    _stream_expert_weights(w, e_ref, slot_ref, nxt_ref, [(wd_hbm, wd_stage, sem)], convert)
    r, lo, hi = r_ref[w], lo_ref[w], hi_ref[w]

    @pl.when(hi > lo)
    def _():
        y = jnp.dot(h_ref[...], wd_bf[...], preferred_element_type=F32)
        _store_item_rows(y_ref, y, r, lo, hi)


def _moe_down(items, h, wd):
    n_rows, f = h.shape
    d = wd.shape[2]
    n_items = items[0].shape[0]
    blocks = _nbytes((MOE_ROWS, f), BF16) + _nbytes((MOE_ROWS, d), F32)
    resident = 2 * _nbytes((f, d), F32) + _nbytes((f, d), BF16) + 3 * _nbytes((MOE_ROWS, d), F32)
    row_block = lambda w, r, e, lo, hi, slot, nxt: (r[w], 0)
    return pl.pallas_call(
        _moe_down_kernel,
        grid_spec=pltpu.PrefetchScalarGridSpec(
            num_scalar_prefetch=6,
            grid=(n_items,),
            in_specs=[pl.BlockSpec((MOE_ROWS, f), row_block), pl.BlockSpec(memory_space=pl.ANY)],
            out_specs=pl.BlockSpec((MOE_ROWS, d), row_block),
            scratch_shapes=[pltpu.VMEM((2, f, d), F32), pltpu.VMEM((f, d), BF16), pltpu.SemaphoreType.DMA((2,))]),
        out_shape=jax.ShapeDtypeStruct((n_rows, d), F32),
        compiler_params=_params(("arbitrary",), blocks, resident),
        name="moe_down",
    )(*items, h, wd)


def _combine_kernel(dest_ref, x_ref, route_ref, y_hbm, o_ref, ybuf, sem, *, tm):
    half = tm // 2
    i = pl.program_id(0)

    def copy(src_row, slot, k, t):
        return pltpu.make_async_copy(y_hbm.at[pl.ds(src_row, 1)], ybuf.at[slot, k, pl.ds(t, 1)], sem.at[slot])

    def start_half(index, slot):
        for t in range(half):
            for k in range(TOP_K):
                copy(dest_ref[TOP_K * (index * half + t) + k], slot, k, t).start()

    def finish_half(slot):
        for t in range(half):
            for k in range(TOP_K):
                copy(0, slot, k, t).wait()
        rows = slice(slot * half, (slot + 1) * half)
        route = route_ref[rows, :]
        o_ref[rows, :] = x_ref[rows, :] + (route[:, 2:3] * ybuf[slot, 0] + route[:, 3:4] * ybuf[slot, 1])

    @pl.when(i == 0)
    def _():
        start_half(2 * i, 0)

    start_half(2 * i + 1, 1)
    finish_half(0)

    @pl.when(i + 1 < pl.num_programs(0))
    def _():
        start_half(2 * i + 2, 0)

    finish_half(1)


def _combine(dest_flat, x1, route, y, tm):
    t, d = x1.shape
    blocks = 2 * _nbytes((tm, d), F32) + _nbytes((tm, LANE), F32)
    resident = TOP_K * _nbytes((tm, d), F32) + _nbytes((tm, d), F32)
    return pl.pallas_call(
        functools.partial(_combine_kernel, tm=tm),
        grid_spec=pltpu.PrefetchScalarGridSpec(
            num_scalar_prefetch=1,
            grid=(t // tm,),
            in_specs=[pl.BlockSpec((tm, d), lambda i, dest: (i, 0)),
                      pl.BlockSpec((tm, LANE), lambda i, dest: (i, 0)),
                      pl.BlockSpec(memory_space=pl.ANY)],
            out_specs=pl.BlockSpec((tm, d), lambda i, dest: (i, 0)),
            scratch_shapes=[pltpu.VMEM((2, TOP_K, tm // 2, d), F32), pltpu.SemaphoreType.DMA((2,))]),
        out_shape=jax.ShapeDtypeStruct((t, d), F32),
        compiler_params=_params(("arbitrary",), blocks, resident),
        name="moe_combine",
    )(dest_flat, x1, route, y)


def _pad_cols(w, n):
    return jnp.pad(w, ((0, 0), (0, n - w.shape[1])))


def kernel(x, positions, g_mix, w_in, b_gate, q_norm_g, kv_norm_g, w_uq, w_ukv, a_q_norm_g, a_k_norm_g,
           b_q_norm_g, b_k_norm_g, rel_bias, w_o_a, w_o_b, w_out, g_ffn, w_group, w_expert,
           w_exp_gate, w_exp_up, w_exp_down):
    batch, seq, d = x.shape
    t = batch * seq
    q_lora, kv_lora = q_norm_g.shape[0], kv_norm_g.shape[0]
    a_heads = w_uq.shape[1] // A_QK
    b_heads = w_o_b.shape[0] // B_HEAD_DIM
    b_width = b_heads * B_HEAD_DIM
    n_groups, n_experts = w_group.shape[1], w_expert.shape[1]
    per_group = n_experts // n_groups
    off_b = q_lora + kv_lora + A_ROPE
    assert seq % ATTN_TILE == 0 and (TOP_K * t) % GATHER_ROWS == 0 and n_groups + n_experts <= LANE
    assert a_heads == b_heads

    xf = x.reshape(t, d)
    tm_big = min(1024, t)
    tn = _tile(b_width, 512)
    assert d % tn == 0

    za_cols = -(-(q_lora + kv_lora + LANE) // tn) * tn
    wuq = jnp.pad(w_uq.reshape(q_lora, a_heads, A_QK), ((0, 0), (0, 0), (0, A_HEAD_PAD - A_QK)))
    wuq = wuq.reshape(q_lora, a_heads * A_HEAD_PAD).astype(BF16)
    wukv = w_ukv.astype(BF16)
    pad_gain = lambda g, s: jnp.pad(g * s, (0, A_HEAD_PAD - A_QK)).reshape(1, A_HEAD_PAD)
    gaq = pad_gain(a_q_norm_g, A_QK ** -0.5)
    gak = pad_gain(a_k_norm_g, 1.0)
    g_bqk = jnp.concatenate([jnp.tile(b_q_norm_g * B_HEAD_DIM ** -0.5, b_heads), jnp.tile(b_k_norm_g, b_heads)])

    half = A_ROPE // 2
    inv = ROPE_THETA ** (-jnp.arange(half, dtype=F32) / half)
    ang = positions.astype(F32).reshape(t, 1) * inv
    cos, sin = jnp.cos(ang), jnp.sin(ang)
    zeros = jnp.zeros((t, half), F32)
    cos_t = jnp.concatenate([cos, cos, zeros, zeros], axis=1)
    sin_lo = jnp.concatenate([-sin, zeros, zeros, zeros], axis=1)
    sin_hi = jnp.concatenate([zeros, sin, zeros, zeros], axis=1)

    xn = _rmsnorm_rows(xf, g_mix, BF16, min(256, t))
    w_in_t = w_in.T
    za = _inproj_a(xn, w_in_t, za_cols, tm_big, tn)
    bqk = _inproj_cols(xn, w_in_t, off_b, 2 * b_width, g_bqk, _head_norm_epilogue, tm_big, tn)
    bv = _inproj_cols(xn, w_in_t, off_b + 2 * b_width, b_width, None, _plain_epilogue, tm_big, tn)
    gates = _inproj_cols(xn, w_in_t, off_b + 3 * b_width, 2 * d, b_gate, _sigmoid_epilogue, tm_big, tn)
    q, k, v = _mla_proj(za, cos_t, sin_lo, sin_hi, wuq, wukv, q_norm_g.reshape(1, -1), kv_norm_g.reshape(1, -1),
                        gaq, gak, a_heads, min(256, t))
    o_a, o_b = _attention(q, k, v, bqk, bv, _band_bias_rows(rel_bias), batch, seq, a_heads)
    merged = _merge(o_a, o_b, w_o_a, w_o_b, gates, tm_big, tn)
    x1 = _out_proj(merged, w_out, xf, tm_big, tn)

    wr = _pad_cols(jnp.concatenate([w_group, w_expert], axis=1), LANE)
    xn2, route = _router(x1, g_ffn, wr, n_groups, per_group, min(256, t))
    rank, starts_f = _rank(route, min(512, t))
    dest = _dest(route, rank, starts_f, min(512, t))[:, :TOP_K].reshape(-1)
    xs = _gather_rows(dest, xn2, GATHER_ROWS)
    items = _work_items(starts_f[0, :n_experts].astype(jnp.int32), TOP_K * t)
    h = _moe_up(items, xs, w_exp_gate, w_exp_up)
    y = _moe_down(items, h, w_exp_down)
    out = _combine(dest, x1, route, y, min(256, t))
    return out.reshape(batch, seq, d)
```

```python
import functools

import jax
import jax.numpy as jnp
from jax import lax
from jax.experimental import pallas as pl
from jax.experimental.pallas import tpu as pltpu

F32 = jnp.float32
BF16 = jnp.bfloat16

CHUNK = 64
EPS = 1e-6
A_NOPE = 128
A_ROPE = 64
A_V = 128
A_QK = A_NOPE + A_ROPE
B_HEAD_DIM = 128
B_LEFT_CHUNKS = 8
B_MAX_REL = 128
ROPE_THETA = 10000.0
TOP_K = 2

LANE = 128
A_HEAD_PAD = 2 * LANE
V_PAD = 2 * LANE
V7X_VMEM_BYTES = 64 * 2**20

ATTN_TILE = 256
MOE_ROWS = 128
GATHER_ROWS = 256
WEIGHT_DMA_PARTS = 4
PROJ_ROW_GROUPS = 4
NEG_INF = float("-inf")


def _nbytes(shape, dtype):
    n = 1
    for s in shape:
        n *= s
    return n * jnp.dtype(dtype).itemsize


def _params(semantics, pipelined_bytes, resident_bytes=0):
    need = 2 * pipelined_bytes + resident_bytes
    return pltpu.CompilerParams(dimension_semantics=semantics,
                                vmem_limit_bytes=min(int(need), V7X_VMEM_BYTES))


def _tile(n, want):
    t = want
    while t > LANE and n % t:
        t //= 2
    assert n % t == 0, (n, want)
    return t


def _rmsnorm_kernel(x_ref, g_ref, o_ref):
    x = x_ref[...]
    r = lax.rsqrt(jnp.mean(x * x, axis=-1, keepdims=True) + EPS)
    o_ref[...] = (x * r * g_ref[...]).astype(o_ref.dtype)


def _rmsnorm_rows(x, g, out_dtype, tm):
    t, d = x.shape
    blocks = _nbytes((tm, d), F32) + _nbytes((tm, d), out_dtype) + _nbytes((1, d), F32)
    return pl.pallas_call(
        _rmsnorm_kernel,
        grid=(t // tm,),
        in_specs=[pl.BlockSpec((tm, d), lambda i: (i, 0)), pl.BlockSpec((1, d), lambda i: (0, 0))],
        out_specs=pl.BlockSpec((tm, d), lambda i: (i, 0)),
        out_shape=jax.ShapeDtypeStruct((t, d), out_dtype),
        compiler_params=_params(("parallel",), blocks, _nbytes((tm, d), F32)),
        name="rmsnorm",
    )(x, g.reshape(1, d))


def _nt_dot(a, b):
    return lax.dot_general(a, b, (((1,), (1,)), ((), ())), preferred_element_type=F32)


def _inproj_a_kernel(a_ref, wt_ref, o_ref, wt_bf):
    @pl.when(pl.program_id(1) == 0)
    def _():
        wt_bf[...] = wt_ref[...].astype(BF16)

    o_ref[...] = _nt_dot(a_ref[...], wt_bf[...])


def _inproj_a(xn, w_in_t, n_cols, tm, tn):
    m, k = xn.shape
    blocks = _nbytes((tm, k), BF16) + _nbytes((tn, k), F32) + _nbytes((tm, tn), F32)
    return pl.pallas_call(
        _inproj_a_kernel,
        grid=(n_cols // tn, m // tm),
        in_specs=[pl.BlockSpec((tm, k), lambda j, i: (i, 0)), pl.BlockSpec((tn, k), lambda j, i: (j, 0))],
        out_specs=pl.BlockSpec((tm, tn), lambda j, i: (i, j)),
        out_shape=jax.ShapeDtypeStruct((m, n_cols), F32),
        scratch_shapes=[pltpu.VMEM((tn, k), BF16)],
        compiler_params=_params(("parallel", "arbitrary"), blocks, 2 * _nbytes((tn, k), BF16)),
        name="inproj_a",
    )(xn, w_in_t)


def _head_norm_epilogue(acc, gb_ref):
    heads = []
    for h in range(acc.shape[1] // B_HEAD_DIM):
        sl = slice(h * B_HEAD_DIM, (h + 1) * B_HEAD_DIM)
        z = acc[:, sl]
        r = lax.rsqrt(jnp.mean(z * z, axis=-1, keepdims=True) + EPS)
        heads.append(z * r * gb_ref[:, sl])
    return jnp.concatenate(heads, axis=1)


def _plain_epilogue(acc, gb_ref):
    del gb_ref
    return acc


def _sigmoid_epilogue(acc, gb_ref):
    return 1.0 / (1.0 + jnp.exp(-(acc + gb_ref[...])))


def _inproj_cols_kernel(a_ref, wt_hbm, *refs, first_col, epilogue):
    gb_ref = refs[0] if len(refs) == 5 else None
    o_ref, stage, wt_bf, sem = refs[-4:]
    j = pl.program_id(0)
    tn = wt_bf.shape[0]

    def fetch(block):
        rows = pl.ds(pl.multiple_of(first_col + block * tn, 8), tn)
        return pltpu.make_async_copy(wt_hbm.at[rows], stage, sem)

    @pl.when(pl.program_id(1) == 0)
    def _():
        @pl.when(j == 0)
        def _():
            fetch(j).start()

        fetch(j).wait()
        wt_bf[...] = stage[...].astype(BF16)

        @pl.when(j + 1 < pl.num_programs(0))
        def _():
            fetch(j + 1).start()

    rows = a_ref.shape[0] // PROJ_ROW_GROUPS
    groups = [slice(s * rows, (s + 1) * rows) for s in range(PROJ_ROW_GROUPS)]
    accs = [_nt_dot(a_ref[sl, :], wt_bf[...]) for sl in groups]
    for sl, acc in zip(groups, accs):
        o_ref[sl, :] = epilogue(acc, gb_ref).astype(o_ref.dtype)


def _inproj_cols(xn, w_in_t, first_col, n_cols, gb, epilogue, tm, tn):
    m, k = xn.shape
    assert n_cols % tn == 0 and first_col % 8 == 0 and tm % PROJ_ROW_GROUPS == 0
    blocks = _nbytes((tm, k), BF16) + _nbytes((tm, tn), BF16) + _nbytes((1, tn), F32)
    resident = _nbytes((tn, k), F32) + _nbytes((tn, k), BF16) + 3 * _nbytes((tm, tn), F32)
    gb_specs = [] if gb is None else [pl.BlockSpec((1, tn), lambda j, i: (0, j))]
    gb_args = [] if gb is None else [gb.reshape(1, n_cols)]
    return pl.pallas_call(
        functools.partial(_inproj_cols_kernel, first_col=first_col, epilogue=epilogue),
        grid=(n_cols // tn, m // tm),
        in_specs=[pl.BlockSpec((tm, k), lambda j, i: (i, 0)), pl.BlockSpec(memory_space=pl.ANY)] + gb_specs,
        out_specs=pl.BlockSpec((tm, tn), lambda j, i: (i, j)),
        out_shape=jax.ShapeDtypeStruct((m, n_cols), BF16),
        scratch_shapes=[pltpu.VMEM((tn, k), F32), pltpu.VMEM((tn, k), BF16), pltpu.SemaphoreType.DMA(())],
        compiler_params=_params(("arbitrary", "arbitrary"), blocks, resident),
        name="inproj_" + epilogue.__name__.strip("_").replace("_epilogue", ""),
    )(xn, w_in_t, *gb_args)


def _rope_padded(v, cos, sin_lo, sin_hi):
    half = A_ROPE // 2
    return v * cos + pltpu.roll(v, LANE - half, 1) * sin_lo + pltpu.roll(v, half, 1) * sin_hi


def _ones_column(rows):
    lane = lax.broadcasted_iota(jnp.int32, (rows, V_PAD - A_V), 1)
    return jnp.where(lane == 0, 1.0, 0.0).astype(BF16)


def _mla_proj_kernel(za_ref, cos_ref, sl_ref, sh_ref, wuq_ref, wukv_ref, gq_ref, gkv_ref, gaq_ref, gak_ref,
                     q_ref, k_ref, v_ref, *, heads, q_lora, kv_lora):
    cos, sin_lo, sin_hi = cos_ref[...], sl_ref[...], sh_ref[...]

    def norm(z, g):
        r = lax.rsqrt(jnp.mean(z * z, axis=-1, keepdims=True) + EPS)
        return (z * r * g).astype(BF16)

    cq = norm(za_ref[:, :q_lora], gq_ref[...])
    ckv = norm(za_ref[:, q_lora:q_lora + kv_lora], gkv_ref[...])
    slab = za_ref[:, q_lora + kv_lora:q_lora + kv_lora + LANE]
    k_rope = jnp.where(lax.broadcasted_iota(jnp.int32, slab.shape, 1) < A_ROPE, slab, 0.0)
    qacc = jnp.dot(cq, wuq_ref[...], preferred_element_type=F32)
    kvacc = jnp.dot(ckv, wukv_ref[...], preferred_element_type=F32)

    gq_lo, gq_hi = gaq_ref[:, :LANE], gaq_ref[:, LANE:]
    gk_lo, gk_hi = gak_ref[:, :LANE], gak_ref[:, LANE:]
    kr_ss = jnp.sum(k_rope * k_rope, axis=-1, keepdims=True)
    kr_rot = _rope_padded(k_rope * gk_hi, cos, sin_lo, sin_hi)
    ones_col = _ones_column(slab.shape[0])
    for h in range(heads):
        base = h * A_HEAD_PAD
        q_lo = qacc[:, base:base + LANE]
        q_hi = qacc[:, base + LANE:base + A_HEAD_PAD]
        ss = jnp.sum(q_lo * q_lo, axis=-1, keepdims=True) + jnp.sum(q_hi * q_hi, axis=-1, keepdims=True)
        r = lax.rsqrt(ss / A_QK + EPS)
        q_ref[:, base:base + LANE] = (q_lo * r * gq_lo).astype(BF16)
        q_ref[:, base + LANE:base + A_HEAD_PAD] = _rope_padded(q_hi * r * gq_hi, cos, sin_lo, sin_hi).astype(BF16)

        k_lo = kvacc[:, base:base + LANE]
        ssk = jnp.sum(k_lo * k_lo, axis=-1, keepdims=True) + kr_ss
        rk = lax.rsqrt(ssk / A_QK + EPS)
        k_ref[:, base:base + LANE] = (k_lo * rk * gk_lo).astype(BF16)
        k_ref[:, base + LANE:base + A_HEAD_PAD] = (kr_rot * rk).astype(BF16)
        v_ref[:, h * V_PAD:h * V_PAD + A_V] = kvacc[:, base + LANE:base + A_HEAD_PAD].astype(BF16)
        v_ref[:, h * V_PAD + A_V:(h + 1) * V_PAD] = ones_col


def _mla_proj(za, cos, sin_lo, sin_hi, wuq, wukv, gq, gkv, gaq, gak, heads, tm):
    t, za_cols = za.shape
    q_lora, kv_lora = wuq.shape[0], wukv.shape[0]
    hp = heads * A_HEAD_PAD
    row = lambda i: (i, 0)
    fix = lambda i: (0, 0)
    blocks = (_nbytes((tm, za_cols), F32) + 3 * _nbytes((tm, LANE), F32) + _nbytes(wuq.shape, BF16)
              + _nbytes(wukv.shape, BF16) + 2 * _nbytes((tm, hp), BF16) + _nbytes((tm, heads * V_PAD), BF16))
    return pl.pallas_call(
        functools.partial(_mla_proj_kernel, heads=heads, q_lora=q_lora, kv_lora=kv_lora),
        grid=(t // tm,),
        in_specs=[pl.BlockSpec((tm, za_cols), row),
                  pl.BlockSpec((tm, LANE), row), pl.BlockSpec((tm, LANE), row), pl.BlockSpec((tm, LANE), row),
                  pl.BlockSpec(wuq.shape, fix), pl.BlockSpec(wukv.shape, fix),
                  pl.BlockSpec((1, q_lora), fix), pl.BlockSpec((1, kv_lora), fix),
                  pl.BlockSpec((1, A_HEAD_PAD), fix), pl.BlockSpec((1, A_HEAD_PAD), fix)],
        out_specs=[pl.BlockSpec((tm, hp), row), pl.BlockSpec((tm, hp), row), pl.BlockSpec((tm, heads * V_PAD), row)],
        out_shape=[jax.ShapeDtypeStruct((t, hp), BF16), jax.ShapeDtypeStruct((t, hp), BF16),
                   jax.ShapeDtypeStruct((t, heads * V_PAD), BF16)],
        compiler_params=_params(("parallel",), blocks, 3 * _nbytes((tm, hp), F32)),
        name="mla_proj",
    )(za, cos, sin_lo, sin_hi, wuq, wukv, gq, gkv, gaq, gak)


def _mla_scores(i, q_ref, k_ref, diag_ok):
    tq = ATTN_TILE
    q = q_ref[i * tq:(i + 1) * tq, :]
    sd = jnp.where(diag_ok, _nt_dot(q, k_ref[i * tq:(i + 1) * tq, :]), NEG_INF)
    s0 = _nt_dot(q, k_ref[:i * tq, :]) if i > 0 else None
    return sd, s0


def _mla_softmax(sd, s0):
    m = jnp.max(sd, axis=-1, keepdims=True)
    if s0 is None:
        return jnp.exp(sd - m).astype(BF16), None
    m = jnp.maximum(m, jnp.max(s0, axis=-1, keepdims=True))
    return jnp.exp(sd - m).astype(BF16), jnp.exp(s0 - m).astype(BF16)


def _mla_values(i, pd, p0, v_ref, o_ref):
    tq = ATTN_TILE
    o = jnp.dot(pd, v_ref[i * tq:(i + 1) * tq, :], preferred_element_type=F32)
    if p0 is not None:
        o = o + jnp.dot(p0, v_ref[:i * tq, :], preferred_element_type=F32)
    o_ref[i * tq:(i + 1) * tq, :] = (o[:, :A_V] / o[:, A_V:A_V + 1]).astype(o_ref.dtype)


def _band_window(i):
    left = B_LEFT_CHUNKS * CHUNK
    q0 = i * ATTN_TILE
    k0 = max(0, q0 - left)
    return q0, k0, q0 + ATTN_TILE - k0, left - q0 + k0


def _band_scores(i, q_ref, k_ref, table):
    q0, k0, kw, u0 = _band_window(i)
    return _nt_dot(q_ref[q0:q0 + ATTN_TILE, :], k_ref[k0:k0 + kw, :]) + table[:, u0:u0 + kw]


def _band_softmax(s):
    return jnp.exp(s - jnp.max(s, axis=-1, keepdims=True)).astype(BF16)


def _band_values(i, p, v_ref, o_ref):
    q0, k0, kw, _ = _band_window(i)
    o = jnp.dot(p, v_ref[k0:k0 + kw, :], preferred_element_type=F32)
    o_ref[q0:q0 + ATTN_TILE, :] = (o[:, :B_HEAD_DIM] / o[:, B_HEAD_DIM:B_HEAD_DIM + 1]).astype(o_ref.dtype)


def _attn_kernel(aq_ref, ak_ref, av_ref, bq_ref, bk_ref, bv_ref, r_ref, oa_ref, ob_ref, bv_pad, *, seq):
    tq = ATTN_TILE
    left = B_LEFT_CHUNKS * CHUNK
    width = r_ref.shape[1]
    rc = lax.broadcasted_iota(jnp.int32, (tq, tq), 0) // CHUNK
    cc = lax.broadcasted_iota(jnp.int32, (tq, tq), 1) // CHUNK
    diag_ok = cc <= rc
    bias = pltpu.roll(jnp.broadcast_to(r_ref[...], (tq, width)), 0, 1, stride=1, stride_axis=0)[:, :left + tq]
    q_chunk = lax.broadcasted_iota(jnp.int32, (tq, left + tq), 0) // CHUNK
    k_chunk = lax.broadcasted_iota(jnp.int32, (tq, left + tq), 1) // CHUNK
    table = jnp.where((k_chunk >= q_chunk) & (k_chunk <= q_chunk + B_LEFT_CHUNKS), bias, NEG_INF)
    bv_pad[:, :B_HEAD_DIM] = bv_ref[...]
    bv_pad[:, B_HEAD_DIM:] = _ones_column(seq)

    n_tiles = seq // tq
    sd, s0 = _mla_scores(0, aq_ref, ak_ref, diag_ok)
    for i in range(n_tiles):
        sb = _band_scores(i, bq_ref, bk_ref, table)
        pd, p0 = _mla_softmax(sd, s0)
        if i + 1 < n_tiles:
            sd, s0 = _mla_scores(i + 1, aq_ref, ak_ref, diag_ok)
        _mla_values(i, pd, p0, av_ref, oa_ref)
        _band_values(i, _band_softmax(sb), bv_pad, ob_ref)


def _band_bias_rows(rel_bias):
    left = B_LEFT_CHUNKS * CHUNK
    width = left + 2 * ATTN_TILE
    m = jnp.arange(width, dtype=jnp.int32)
    j = jnp.where(m < left + ATTN_TILE, m, m - width)
    dist = left - j
    rows = rel_bias[:, jnp.clip(dist, -B_MAX_REL, B_MAX_REL) + B_MAX_REL].astype(F32)
    return rows.reshape(rel_bias.shape[0], 1, width)


def _attention(q, k, v, bqk, bv, bias_rows, batch, seq, heads):
    t = batch * seq
    d = B_HEAD_DIM
    width = bias_rows.shape[2]
    head = lambda b, h: (b, h)
    blocks = (2 * _nbytes((seq, A_HEAD_PAD), BF16) + _nbytes((seq, V_PAD), BF16) + _nbytes((seq, A_V), BF16)
              + 4 * _nbytes((seq, d), BF16) + _nbytes((1, width), F32))
    temps = _nbytes((seq, V_PAD), BF16) + 8 * _nbytes((ATTN_TILE, seq), F32) + 8 * _nbytes((ATTN_TILE, width), F32)
    return pl.pallas_call(
        functools.partial(_attn_kernel, seq=seq),
        grid=(batch, heads),
        in_specs=[pl.BlockSpec((seq, A_HEAD_PAD), head), pl.BlockSpec((seq, A_HEAD_PAD), head),
                  pl.BlockSpec((seq, V_PAD), head),
                  pl.BlockSpec((seq, d), head),
                  pl.BlockSpec((seq, d), lambda b, h: (b, heads + h)),
                  pl.BlockSpec((seq, d), head),
                  pl.BlockSpec((None, 1, width), lambda b, h: (h, 0, 0))],
        out_specs=[pl.BlockSpec((seq, A_V), head), pl.BlockSpec((seq, d), head)],
        out_shape=[jax.ShapeDtypeStruct((t, heads * A_V), BF16), jax.ShapeDtypeStruct((t, heads * d), BF16)],
        scratch_shapes=[pltpu.VMEM((seq, V_PAD), BF16)],
        compiler_params=_params(("parallel", "parallel"), blocks, temps),
        name="attention",
    )(q, k, v, bqk, bqk, bv, bias_rows)


def _merge_kernel(oa_ref, ob_ref, woa_ref, wob_ref, g0_ref, g1_ref, o_ref, woa_bf, wob_bf):
    @pl.when(pl.program_id(1) == 0)
    def _():
        woa_bf[...] = woa_ref[...].astype(BF16)
        wob_bf[...] = wob_ref[...].astype(BF16)

    a = jnp.dot(oa_ref[...], woa_bf[...], preferred_element_type=F32)
    b = jnp.dot(ob_ref[...], wob_bf[...], preferred_element_type=F32)
    o_ref[...] = (g0_ref[...].astype(F32) * a + g1_ref[...].astype(F32) * b).astype(o_ref.dtype)


def _merge(oa, ob, woa, wob, gates, tm, tn):
    m = oa.shape[0]
    d = woa.shape[1]
    g1 = d // tn
    blocks = (_nbytes((tm, oa.shape[1]), BF16) + _nbytes((tm, ob.shape[1]), BF16) + _nbytes((woa.shape[0], tn), F32)
              + _nbytes((wob.shape[0], tn), F32) + 3 * _nbytes((tm, tn), BF16))
    resident = (2 * _nbytes((woa.shape[0], tn), BF16) + 2 * _nbytes((wob.shape[0], tn), BF16)
                + 3 * _nbytes((tm, tn), F32))
    return pl.pallas_call(
        _merge_kernel,
        grid=(d // tn, m // tm),
        in_specs=[pl.BlockSpec((tm, oa.shape[1]), lambda j, i: (i, 0)),
                  pl.BlockSpec((tm, ob.shape[1]), lambda j, i: (i, 0)),
                  pl.BlockSpec((woa.shape[0], tn), lambda j, i: (0, j)),
                  pl.BlockSpec((wob.shape[0], tn), lambda j, i: (0, j)),
                  pl.BlockSpec((tm, tn), lambda j, i: (i, j)),
                  pl.BlockSpec((tm, tn), lambda j, i: (i, g1 + j))],
        out_specs=pl.BlockSpec((tm, tn), lambda j, i: (i, j)),
        out_shape=jax.ShapeDtypeStruct((m, d), BF16),
        scratch_shapes=[pltpu.VMEM((woa.shape[0], tn), BF16), pltpu.VMEM((wob.shape[0], tn), BF16)],
        compiler_params=_params(("parallel", "arbitrary"), blocks, resident),
        name="merge",
    )(oa, ob, woa, wob, gates, gates)


def _out_proj_kernel(a_ref, w_ref, x_ref, o_ref, w_bf):
    @pl.when(pl.program_id(1) == 0)
    def _():
        w_bf[...] = w_ref[...].astype(BF16)

    o_ref[...] = x_ref[...] + jnp.dot(a_ref[...], w_bf[...], preferred_element_type=F32)


def _out_proj(merged, wout, x, tm, tn):
    m, k = merged.shape
    n = wout.shape[1]
    blocks = _nbytes((tm, k), BF16) + _nbytes((k, tn), F32) + 2 * _nbytes((tm, tn), F32)
    return pl.pallas_call(
        _out_proj_kernel,
        grid=(n // tn, m // tm),
        in_specs=[pl.BlockSpec((tm, k), lambda j, i: (i, 0)),
                  pl.BlockSpec((k, tn), lambda j, i: (0, j)),
                  pl.BlockSpec((tm, tn), lambda j, i: (i, j))],
        out_specs=pl.BlockSpec((tm, tn), lambda j, i: (i, j)),
        out_shape=jax.ShapeDtypeStruct((m, n), F32),
        scratch_shapes=[pltpu.VMEM((k, tn), BF16)],
        compiler_params=_params(("parallel", "arbitrary"), blocks, 2 * _nbytes((k, tn), BF16) + _nbytes((tm, tn), F32)),
        name="out_proj",
    )(merged, wout, x)


def _split_bf16(v):
    hi = v.astype(BF16)
    return hi, (v - hi.astype(F32)).astype(BF16)


def _router_kernel(x_ref, g_ref, wr_ref, xn_ref, route_ref, *, n_groups, per_group):
    x = x_ref[...]
    r = lax.rsqrt(jnp.mean(x * x, axis=-1, keepdims=True) + EPS)
    xn = x * r * g_ref[...]
    xn_ref[...] = xn
    x_hi, x_lo = _split_bf16(xn)
    w_hi, w_lo = _split_bf16(wr_ref[...])
    dot = functools.partial(jnp.dot, preferred_element_type=F32)
    logits = dot(x_hi, w_hi) + (dot(x_lo, w_hi) + dot(x_hi, w_lo))
    lane = lax.broadcasted_iota(jnp.int32, logits.shape, 1).astype(F32)
    far = float(LANE)

    def top(vals):
        best = jnp.max(vals, axis=-1, keepdims=True)
        return best, jnp.min(jnp.where(vals == best, lane, far), axis=-1, keepdims=True)

    gl = jnp.where(lane < n_groups, logits, NEG_INF)
    gmax, grp = top(gl)
    p_grp = 1.0 / jnp.sum(jnp.exp(gl - gmax), axis=-1, keepdims=True)
    lo = n_groups + grp * per_group
    el = jnp.where((lane >= lo) & (lane < lo + per_group), logits, NEG_INF)
    t1, i1 = top(el)
    t2, i2 = top(jnp.where(lane == i1, NEG_INF, el))
    d = jnp.exp(t2 - t1)
    w1 = p_grp / (1.0 + d)
    w2 = p_grp * d / (1.0 + d)
    route_ref[...] = jnp.where(lane == 0, i1 - n_groups,
                               jnp.where(lane == 1, i2 - n_groups,
                                         jnp.where(lane == 2, w1, jnp.where(lane == 3, w2, 0.0))))


def _router(x1, g, wr, n_groups, per_group, tm):
    t, d = x1.shape
    blocks = 2 * _nbytes((tm, d), F32) + _nbytes((1, d), F32) + _nbytes((d, LANE), F32) + _nbytes((tm, LANE), F32)
    return pl.pallas_call(
        functools.partial(_router_kernel, n_groups=n_groups, per_group=per_group),
        grid=(t // tm,),
        in_specs=[pl.BlockSpec((tm, d), lambda i: (i, 0)), pl.BlockSpec((1, d), lambda i: (0, 0)),
                  pl.BlockSpec((d, LANE), lambda i: (0, 0))],
        out_specs=[pl.BlockSpec((tm, d), lambda i: (i, 0)), pl.BlockSpec((tm, LANE), lambda i: (i, 0))],
        out_shape=[jax.ShapeDtypeStruct((t, d), F32), jax.ShapeDtypeStruct((t, LANE), F32)],
        compiler_params=_params(("parallel",), blocks, 2 * _nbytes((tm, d), F32)),
        name="router",
    )(x1, g.reshape(1, d), wr)


def _one_hots(route):
    lane = lax.broadcasted_iota(jnp.int32, route.shape, 1).astype(F32)
    return (lane == route[:, 0:1]).astype(F32), (lane == route[:, 1:2]).astype(F32)


def _rank_kernel(route_ref, rank_ref, starts_ref, count_acc, start_acc):
    i = pl.program_id(0)

    @pl.when(i == 0)
    def _():
        count_acc[...] = jnp.zeros_like(count_acc)
        start_acc[...] = jnp.zeros_like(start_acc)

    oh1, oh2 = _one_hots(route_ref[...])
    oh = (oh1 + oh2).astype(BF16)
    tm = oh.shape[0]
    earlier = (lax.broadcasted_iota(jnp.int32, (tm, tm), 0) > lax.broadcasted_iota(jnp.int32, (tm, tm), 1))
    before = jnp.dot(earlier.astype(BF16), oh, preferred_element_type=F32) + count_acc[...]
    lane = lax.broadcasted_iota(jnp.int32, (tm, LANE), 1)
    rank_ref[...] = jnp.where(lane == 0, jnp.sum(oh1 * before, axis=-1, keepdims=True),
                              jnp.where(lane == 1, jnp.sum(oh2 * before, axis=-1, keepdims=True), 0.0))
    lower = (lax.broadcasted_iota(jnp.int32, (LANE, LANE), 0) < lax.broadcasted_iota(jnp.int32, (LANE, LANE), 1))
    below = jnp.dot(oh, lower.astype(BF16), preferred_element_type=F32)
    count_acc[...] += jnp.sum(oh.astype(F32), axis=0, keepdims=True)
    start_acc[...] += jnp.sum(below, axis=0, keepdims=True)
    starts_ref[...] = start_acc[...]


def _rank(route, tm):
    t = route.shape[0]
    blocks = 2 * _nbytes((tm, LANE), F32) + _nbytes((1, LANE), F32)
    return pl.pallas_call(
        _rank_kernel,
        grid=(t // tm,),
        in_specs=[pl.BlockSpec((tm, LANE), lambda i: (i, 0))],
        out_specs=[pl.BlockSpec((tm, LANE), lambda i: (i, 0)), pl.BlockSpec((1, LANE), lambda i: (0, 0))],
        out_shape=[jax.ShapeDtypeStruct((t, LANE), F32), jax.ShapeDtypeStruct((1, LANE), F32)],
        scratch_shapes=[pltpu.VMEM((1, LANE), F32), pltpu.VMEM((1, LANE), F32)],
        compiler_params=_params(("arbitrary",), blocks, 2 * _nbytes((tm, tm), F32)),
        name="moe_rank",
    )(route)


def _dest_kernel(route_ref, rank_ref, starts_ref, dest_ref):
    oh1, oh2 = _one_hots(route_ref[...])
    rank = rank_ref[...]
    starts = starts_ref[...]
    d1 = jnp.sum(oh1 * starts, axis=-1, keepdims=True) + rank[:, 0:1]
    d2 = jnp.sum(oh2 * starts, axis=-1, keepdims=True) + rank[:, 1:2]
    lane = lax.broadcasted_iota(jnp.int32, rank.shape, 1)
    dest_ref[...] = jnp.where(lane == 0, d1, jnp.where(lane == 1, d2, 0.0)).astype(jnp.int32)


def _dest(route, rank, starts, tm):
    t = route.shape[0]
    blocks = 3 * _nbytes((tm, LANE), F32) + _nbytes((1, LANE), F32)
    return pl.pallas_call(
        _dest_kernel,
        grid=(t // tm,),
        in_specs=[pl.BlockSpec((tm, LANE), lambda i: (i, 0)), pl.BlockSpec((tm, LANE), lambda i: (i, 0)),
                  pl.BlockSpec((1, LANE), lambda i: (0, 0))],
        out_specs=pl.BlockSpec((tm, LANE), lambda i: (i, 0)),
        out_shape=jax.ShapeDtypeStruct((t, LANE), jnp.int32),
        compiler_params=_params(("parallel",), blocks, 4 * _nbytes((tm, LANE), F32)),
        name="moe_dest",
    )(route, rank, starts)


def _gather_rows_kernel(dest_ref, x_hbm, xs_ref, source, rows_f32, sem, *, n_assign):
    half = xs_ref.shape[0] // 2
    i = pl.program_id(0)

    def copy(src_row, slot, j):
        return pltpu.make_async_copy(x_hbm.at[pl.ds(src_row, 1)], rows_f32.at[slot, pl.ds(j, 1)], sem.at[slot])

    def start_half(block, slot):
        for j in range(half):
            copy(source[block * half + j], slot, j).start()

    def finish_half(slot):
        for j in range(half):
            copy(0, slot, j).wait()
        xs_ref[slot * half:(slot + 1) * half, :] = rows_f32[slot].astype(xs_ref.dtype)

    @pl.when(i == 0)
    def _():
        def invert(tok, carry):
            for k in range(TOP_K):
                source[dest_ref[TOP_K * tok + k]] = tok
            return carry
        lax.fori_loop(0, n_assign // TOP_K, invert, 0, unroll=4)
        start_half(2 * i, 0)

    start_half(2 * i + 1, 1)
    finish_half(0)

    @pl.when(i + 1 < pl.num_programs(0))
    def _():
        start_half(2 * i + 2, 0)

    finish_half(1)


def _gather_rows(dest_flat, xn, rows):
    t, d = xn.shape
    n_assign = dest_flat.shape[0]
    return pl.pallas_call(
        functools.partial(_gather_rows_kernel, n_assign=n_assign),
        grid_spec=pltpu.PrefetchScalarGridSpec(
            num_scalar_prefetch=1,
            grid=(n_assign // rows,),
            in_specs=[pl.BlockSpec(memory_space=pl.ANY)],
            out_specs=pl.BlockSpec((rows, d), lambda i, dest: (i, 0)),
            scratch_shapes=[pltpu.SMEM((n_assign,), jnp.int32), pltpu.VMEM((2, rows // 2, d), xn.dtype),
                            pltpu.SemaphoreType.DMA((2,))]),
        out_shape=jax.ShapeDtypeStruct((n_assign, d), BF16),
        compiler_params=_params(("arbitrary",), _nbytes((rows, d), BF16), 2 * _nbytes((rows, d), xn.dtype)),
        name="moe_gather_rows",
    )(dest_flat, xn)


def _work_items(starts, n_rows):
    n_exp = starts.shape[0]
    n_blk = n_rows // MOE_ROWS
    total = jnp.full((1,), n_rows, jnp.int32)
    pts = jnp.concatenate([jnp.arange(n_blk, dtype=jnp.int32) * MOE_ROWS, starts[1:]])
    idx = jnp.arange(pts.shape[0], dtype=jnp.int32)
    before = (pts[None, :] < pts[:, None]) | ((pts[None, :] == pts[:, None]) & (idx[None, :] < idx[:, None]))
    pos = jnp.sum(before.astype(jnp.int32), axis=1)
    lo = jnp.sum(jnp.where(pos[:, None] == idx[None, :], pts[:, None], 0), axis=0)
    hi = jnp.concatenate([lo[1:], total])
    ends = jnp.concatenate([starts[1:], total])
    r = jnp.minimum(lo // MOE_ROWS, n_blk - 1)
    e = jnp.minimum(jnp.sum((ends[None, :] <= lo[:, None]).astype(jnp.int32), axis=1), n_exp - 1)
    changed = jnp.concatenate([jnp.ones((1,), jnp.int32), (e[1:] != e[:-1]).astype(jnp.int32)])
    slot = (jnp.sum(jnp.where(idx[None, :] <= idx[:, None], changed[None, :], 0), axis=1) - 1) % 2
    later = jnp.where(e[None, :] > e[:, None], e[None, :], n_exp)
    nxt = jnp.min(later, axis=1)
    nxt = jnp.where(nxt == n_exp, -1, nxt)
    return r, e, lo, hi, slot, nxt


def _stream_expert_weights(w, e_ref, slot_ref, nxt_ref, streams, convert):
    e = e_ref[w]
    slot = slot_ref[w]

    def copies(expert, into):
        out = []
        for hbm, stage, sem in streams:
            rows = stage.shape[1] // WEIGHT_DMA_PARTS
            for part in range(WEIGHT_DMA_PARTS):
                sl = pl.ds(part * rows, rows)
                out.append(pltpu.make_async_copy(hbm.at[expert, sl], stage.at[into, sl], sem.at[into]))
        return out

    @pl.when(w == 0)
    def _():
        for c in copies(e, slot):
            c.start()

    @pl.when((w == 0) | (e != e_ref[jnp.maximum(w - 1, 0)]))
    def _():
        for c in copies(e, slot):
            c.wait()
        nxt = nxt_ref[w]

        @pl.when(nxt >= 0)
        def _():
            for c in copies(nxt, 1 - slot):
                c.start()

        convert(slot)


def _store_item_rows(o_ref, val, r, lo, hi):
    rows = r * MOE_ROWS + lax.broadcasted_iota(jnp.int32, (MOE_ROWS, 1), 0)
    mine = (rows >= lo) & (rows < hi)

    @pl.when(lo == r * MOE_ROWS)
    def _():
        o_ref[...] = val

    @pl.when(lo != r * MOE_ROWS)
    def _():
        o_ref[...] = jnp.where(mine, val, o_ref[...])


def _moe_up_kernel(r_ref, e_ref, lo_ref, hi_ref, slot_ref, nxt_ref, xs_ref, wg_hbm, wu_hbm, h_ref,
                   wg_stage, wu_stage, wg_bf, wu_bf, sem_g, sem_u):
    w = pl.program_id(0)

    def convert(slot):
        wg_bf[...] = wg_stage[slot].astype(BF16)
        wu_bf[...] = wu_stage[slot].astype(BF16)

    _stream_expert_weights(w, e_ref, slot_ref, nxt_ref,
                           [(wg_hbm, wg_stage, sem_g), (wu_hbm, wu_stage, sem_u)], convert)
    r, lo, hi = r_ref[w], lo_ref[w], hi_ref[w]

    @pl.when(hi > lo)
    def _():
        x = xs_ref[...]
        g = jnp.dot(x, wg_bf[...], preferred_element_type=F32)
        u = jnp.dot(x, wu_bf[...], preferred_element_type=F32)
        h = (g * (1.0 / (1.0 + jnp.exp(-g)))) * u
        _store_item_rows(h_ref, h.astype(h_ref.dtype), r, lo, hi)


def _moe_up(items, xs, wg, wu):
    n_rows, d = xs.shape
    _, _, f = wg.shape
    n_items = items[0].shape[0]
    blocks = _nbytes((MOE_ROWS, d), BF16) + _nbytes((MOE_ROWS, f), BF16)
    resident = 4 * _nbytes((d, f), F32) + 2 * _nbytes((d, f), BF16) + 6 * _nbytes((MOE_ROWS, f), F32)
    row_block = lambda w, r, e, lo, hi, slot, nxt: (r[w], 0)
    return pl.pallas_call(
        _moe_up_kernel,
        grid_spec=pltpu.PrefetchScalarGridSpec(
            num_scalar_prefetch=6,
            grid=(n_items,),
            in_specs=[pl.BlockSpec((MOE_ROWS, d), row_block),
                      pl.BlockSpec(memory_space=pl.ANY), pl.BlockSpec(memory_space=pl.ANY)],
            out_specs=pl.BlockSpec((MOE_ROWS, f), row_block),
            scratch_shapes=[pltpu.VMEM((2, d, f), F32), pltpu.VMEM((2, d, f), F32),
                            pltpu.VMEM((d, f), BF16), pltpu.VMEM((d, f), BF16),
                            pltpu.SemaphoreType.DMA((2,)), pltpu.SemaphoreType.DMA((2,))]),
        out_shape=jax.ShapeDtypeStruct((n_rows, f), BF16),
        compiler_params=_params(("arbitrary",), blocks, resident),
        name="moe_up",
    )(*items, xs, wg, wu)


def _moe_down_kernel(r_ref, e_ref, lo_ref, hi_ref, slot_ref, nxt_ref, h_ref, wd_hbm, y_ref, wd_stage, wd_bf, sem):
    w = pl.program_id(0)

    def convert(slot):
        wd_bf[...] = wd_stage[slot].astype(BF16)

    _stream_expert_weights(w, e_ref, slot_ref, nxt_ref, [(wd_hbm, wd_stage, sem)], convert)
    r, lo, hi = r_ref[w], lo_ref[w], hi_ref[w]

    @pl.when(hi > lo)
    def _():
        y = jnp.dot(h_ref[...], wd_bf[...], preferred_element_type=F32)
        _store_item_rows(y_ref, y, r, lo, hi)


def _moe_down(items, h, wd):
    n_rows, f = h.shape
    d = wd.shape[2]
    n_items = items[0].shape[0]
    blocks = _nbytes((MOE_ROWS, f), BF16) + _nbytes((MOE_ROWS, d), F32)
    resident = 2 * _nbytes((f, d), F32) + _nbytes((f, d), BF16) + 3 * _nbytes((MOE_ROWS, d), F32)
    row_block = lambda w, r, e, lo, hi, slot, nxt: (r[w], 0)
    return pl.pallas_call(
        _moe_down_kernel,
        grid_spec=pltpu.PrefetchScalarGridSpec(
            num_scalar_prefetch=6,
            grid=(n_items,),
            in_specs=[pl.BlockSpec((MOE_ROWS, f), row_block), pl.BlockSpec(memory_space=pl.ANY)],
            out_specs=pl.BlockSpec((MOE_ROWS, d), row_block),
            scratch_shapes=[pltpu.VMEM((2, f, d), F32), pltpu.VMEM((f, d), BF16), pltpu.SemaphoreType.DMA((2,))]),
        out_shape=jax.ShapeDtypeStruct((n_rows, d), F32),
        compiler_params=_params(("arbitrary",), blocks, resident),
        name="moe_down",
    )(*items, h, wd)


def _combine_kernel(dest_ref, x_ref, route_ref, y_hbm, o_ref, ybuf, sem, *, tm):
    half = tm // 2
    i = pl.program_id(0)

    def copy(src_row, slot, k, t):
        return pltpu.make_async_copy(y_hbm.at[pl.ds(src_row, 1)], ybuf.at[slot, k, pl.ds(t, 1)], sem.at[slot])

    def start_half(index, slot):
        for t in range(half):
            for k in range(TOP_K):
                copy(dest_ref[TOP_K * (index * half + t) + k], slot, k, t).start()

    def finish_half(slot):
        for t in range(half):
            for k in range(TOP_K):
                copy(0, slot, k, t).wait()
        rows = slice(slot * half, (slot + 1) * half)
        route = route_ref[rows, :]
        o_ref[rows, :] = x_ref[rows, :] + (route[:, 2:3] * ybuf[slot, 0] + route[:, 3:4] * ybuf[slot, 1])

    @pl.when(i == 0)
    def _():
        start_half(2 * i, 0)

    start_half(2 * i + 1, 1)
    finish_half(0)

    @pl.when(i + 1 < pl.num_programs(0))
    def _():
        start_half(2 * i + 2, 0)

    finish_half(1)


def _combine(dest_flat, x1, route, y, tm):
    t, d = x1.shape
    blocks = 2 * _nbytes((tm, d), F32) + _nbytes((tm, LANE), F32)
    resident = TOP_K * _nbytes((tm, d), F32) + _nbytes((tm, d), F32)
    return pl.pallas_call(
        functools.partial(_combine_kernel, tm=tm),
        grid_spec=pltpu.PrefetchScalarGridSpec(
            num_scalar_prefetch=1,
            grid=(t // tm,),
            in_specs=[pl.BlockSpec((tm, d), lambda i, dest: (i, 0)),
                      pl.BlockSpec((tm, LANE), lambda i, dest: (i, 0)),
                      pl.BlockSpec(memory_space=pl.ANY)],
            out_specs=pl.BlockSpec((tm, d), lambda i, dest: (i, 0)),
            scratch_shapes=[pltpu.VMEM((2, TOP_K, tm // 2, d), F32), pltpu.SemaphoreType.DMA((2,))]),
        out_shape=jax.ShapeDtypeStruct((t, d), F32),
        compiler_params=_params(("arbitrary",), blocks, resident),
        name="moe_combine",
    )(dest_flat, x1, route, y)


def _pad_cols(w, n):
    return jnp.pad(w, ((0, 0), (0, n - w.shape[1])))


def kernel(x, positions, g_mix, w_in, b_gate, q_norm_g, kv_norm_g, w_uq, w_ukv, a_q_norm_g, a_k_norm_g,
           b_q_norm_g, b_k_norm_g, rel_bias, w_o_a, w_o_b, w_out, g_ffn, w_group, w_expert,
           w_exp_gate, w_exp_up, w_exp_down):
    batch, seq, d = x.shape
    t = batch * seq
    q_lora, kv_lora = q_norm_g.shape[0], kv_norm_g.shape[0]
    a_heads = w_uq.shape[1] // A_QK
    b_heads = w_o_b.shape[0] // B_HEAD_DIM
    b_width = b_heads * B_HEAD_DIM
    n_groups, n_experts = w_group.shape[1], w_expert.shape[1]
    per_group = n_experts // n_groups
    off_b = q_lora + kv_lora + A_ROPE
    assert seq % ATTN_TILE == 0 and (TOP_K * t) % GATHER_ROWS == 0 and n_groups + n_experts <= LANE
    assert a_heads == b_heads

    xf = x.reshape(t, d)
    tm_big = min(1024, t)
    tn = _tile(b_width, 512)
    tn_wide = _tile(b_width, 1024)
    assert d % tn == 0 and d % tn_wide == 0

    za_cols = -(-(q_lora + kv_lora + LANE) // tn) * tn
    wuq = jnp.pad(w_uq.reshape(q_lora, a_heads, A_QK), ((0, 0), (0, 0), (0, A_HEAD_PAD - A_QK)))
    wuq = wuq.reshape(q_lora, a_heads * A_HEAD_PAD).astype(BF16)
    wukv = w_ukv.astype(BF16)
    pad_gain = lambda g, s: jnp.pad(g * s, (0, A_HEAD_PAD - A_QK)).reshape(1, A_HEAD_PAD)
    gaq = pad_gain(a_q_norm_g, A_QK ** -0.5)
    gak = pad_gain(a_k_norm_g, 1.0)
    g_bqk = jnp.concatenate([jnp.tile(b_q_norm_g * B_HEAD_DIM ** -0.5, b_heads), jnp.tile(b_k_norm_g, b_heads)])

    half = A_ROPE // 2
    inv = ROPE_THETA ** (-jnp.arange(half, dtype=F32) / half)
    ang = positions.astype(F32).reshape(t, 1) * inv
    cos, sin = jnp.cos(ang), jnp.sin(ang)
    zeros = jnp.zeros((t, half), F32)
    cos_t = jnp.concatenate([cos, cos, zeros, zeros], axis=1)
    sin_lo = jnp.concatenate([-sin, zeros, zeros, zeros], axis=1)
    sin_hi = jnp.concatenate([zeros, sin, zeros, zeros], axis=1)

    xn = _rmsnorm_rows(xf, g_mix, BF16, min(256, t))
    w_in_t = w_in.T
    za = _inproj_a(xn, w_in_t, za_cols, tm_big, tn)
    bqk = _inproj_cols(xn, w_in_t, off_b, 2 * b_width, g_bqk, _head_norm_epilogue, tm_big, tn_wide)
    bv = _inproj_cols(xn, w_in_t, off_b + 2 * b_width, b_width, None, _plain_epilogue, tm_big, tn_wide)
    gates = _inproj_cols(xn, w_in_t, off_b + 3 * b_width, 2 * d, b_gate, _sigmoid_epilogue, tm_big, tn_wide)
    q, k, v = _mla_proj(za, cos_t, sin_lo, sin_hi, wuq, wukv, q_norm_g.reshape(1, -1), kv_norm_g.reshape(1, -1),
                        gaq, gak, a_heads, min(256, t))
    o_a, o_b = _attention(q, k, v, bqk, bv, _band_bias_rows(rel_bias), batch, seq, a_heads)
    merged = _merge(o_a, o_b, w_o_a, w_o_b, gates, tm_big, tn)
    x1 = _out_proj(merged, w_out, xf, tm_big, tn)

    wr = _pad_cols(jnp.concatenate([w_group, w_expert], axis=1), LANE)
    xn2, route = _router(x1, g_ffn, wr, n_groups, per_group, min(256, t))
    rank, starts_f = _rank(route, min(512, t))
    dest = _dest(route, rank, starts_f, min(512, t))[:, :TOP_K].reshape(-1)
    xs = _gather_rows(dest, xn2, GATHER_ROWS)
    items = _work_items(starts_f[0, :n_experts].astype(jnp.int32), TOP_K * t)
    h = _moe_up(items, xs, w_exp_gate, w_exp_up)
    y = _moe_down(items, h, w_exp_down)
    out = _combine(dest, x1, route, y, min(256, t))
    return out.reshape(batch, seq, d)
```

```python
import functools

import jax
import jax.numpy as jnp
from jax import lax
from jax.experimental import pallas as pl
from jax.experimental.pallas import tpu as pltpu

F32 = jnp.float32
BF16 = jnp.bfloat16

CHUNK = 64
EPS = 1e-6
A_NOPE = 128
A_ROPE = 64
A_V = 128
A_QK = A_NOPE + A_ROPE
B_HEAD_DIM = 128
B_LEFT_CHUNKS = 8
B_MAX_REL = 128
ROPE_THETA = 10000.0
TOP_K = 2

LANE = 128
A_HEAD_PAD = 2 * LANE
V_PAD = 2 * LANE
V7X_VMEM_BYTES = 64 * 2**20

ATTN_TILE = 256
MOE_ROWS = 128
GATHER_ROWS = 256
WEIGHT_DMA_PARTS = 4
CONVERT_PARTS = 8
PROJ_ROW_GROUPS = 4
NEG_INF = float("-inf")


def _nbytes(shape, dtype):
    n = 1
    for s in shape:
        n *= s
    return n * jnp.dtype(dtype).itemsize


def _params(semantics, pipelined_bytes, resident_bytes=0):
    need = 2 * pipelined_bytes + resident_bytes
    return pltpu.CompilerParams(dimension_semantics=semantics,
                                vmem_limit_bytes=min(int(need), V7X_VMEM_BYTES))


def _tile(n, want):
    t = want
    while t > LANE and n % t:
        t //= 2
    assert n % t == 0, (n, want)
    return t


def _rmsnorm_kernel(x_ref, g_ref, o_ref):
    x = x_ref[...]
    r = lax.rsqrt(jnp.mean(x * x, axis=-1, keepdims=True) + EPS)
    o_ref[...] = (x * r * g_ref[...]).astype(o_ref.dtype)


def _rmsnorm_rows(x, g, out_dtype, tm):
    t, d = x.shape
    blocks = _nbytes((tm, d), F32) + _nbytes((tm, d), out_dtype) + _nbytes((1, d), F32)
    return pl.pallas_call(
        _rmsnorm_kernel,
        grid=(t // tm,),
        in_specs=[pl.BlockSpec((tm, d), lambda i: (i, 0)), pl.BlockSpec((1, d), lambda i: (0, 0))],
        out_specs=pl.BlockSpec((tm, d), lambda i: (i, 0)),
        out_shape=jax.ShapeDtypeStruct((t, d), out_dtype),
        compiler_params=_params(("parallel",), blocks, _nbytes((tm, d), F32)),
        name="rmsnorm",
    )(x, g.reshape(1, d))


def _nt_dot(a, b):
    return lax.dot_general(a, b, (((1,), (1,)), ((), ())), preferred_element_type=F32)


def _inproj_a_kernel(a_ref, wt_ref, o_ref, wt_bf):
    @pl.when(pl.program_id(1) == 0)
    def _():
        wt_bf[...] = wt_ref[...].astype(BF16)

    o_ref[...] = _nt_dot(a_ref[...], wt_bf[...])


def _inproj_a(xn, w_in_t, n_cols, tm, tn):
    m, k = xn.shape
    blocks = _nbytes((tm, k), BF16) + _nbytes((tn, k), F32) + _nbytes((tm, tn), F32)
    return pl.pallas_call(
        _inproj_a_kernel,
        grid=(n_cols // tn, m // tm),
        in_specs=[pl.BlockSpec((tm, k), lambda j, i: (i, 0)), pl.BlockSpec((tn, k), lambda j, i: (j, 0))],
        out_specs=pl.BlockSpec((tm, tn), lambda j, i: (i, j)),
        out_shape=jax.ShapeDtypeStruct((m, n_cols), F32),
        scratch_shapes=[pltpu.VMEM((tn, k), BF16)],
        compiler_params=_params(("parallel", "arbitrary"), blocks, 2 * _nbytes((tn, k), BF16)),
        name="inproj_a",
    )(xn, w_in_t)


def _head_norm_epilogue(acc, gb_ref):
    heads = []
    for h in range(acc.shape[1] // B_HEAD_DIM):
        sl = slice(h * B_HEAD_DIM, (h + 1) * B_HEAD_DIM)
        z = acc[:, sl]
        r = lax.rsqrt(jnp.mean(z * z, axis=-1, keepdims=True) + EPS)
        heads.append(z * r * gb_ref[:, sl])
    return jnp.concatenate(heads, axis=1)


def _plain_epilogue(acc, gb_ref):
    del gb_ref
    return acc


def _sigmoid_epilogue(acc, gb_ref):
    return 1.0 / (1.0 + jnp.exp(-(acc + gb_ref[...])))


def _inproj_cols_kernel(a_ref, wt_hbm, *refs, first_col, epilogue):
    gb_ref = refs[0] if len(refs) == 5 else None
    o_ref, stage, wt_bf, sem = refs[-4:]
    j = pl.program_id(0)
    tn = wt_bf.shape[0]

    def fetch(block):
        rows = pl.ds(pl.multiple_of(first_col + block * tn, 8), tn)
        return pltpu.make_async_copy(wt_hbm.at[rows], stage, sem)

    @pl.when(pl.program_id(1) == 0)
    def _():
        @pl.when(j == 0)
        def _():
            fetch(j).start()

        fetch(j).wait()
        wt_bf[...] = stage[...].astype(BF16)

        @pl.when(j + 1 < pl.num_programs(0))
        def _():
            fetch(j + 1).start()

    rows = a_ref.shape[0] // PROJ_ROW_GROUPS
    groups = [slice(s * rows, (s + 1) * rows) for s in range(PROJ_ROW_GROUPS)]
    accs = [_nt_dot(a_ref[sl, :], wt_bf[...]) for sl in groups]
    for sl, acc in zip(groups, accs):
        o_ref[sl, :] = epilogue(acc, gb_ref).astype(o_ref.dtype)


def _inproj_cols(xn, w_in_t, first_col, n_cols, gb, epilogue, tm, tn):
    m, k = xn.shape
    assert n_cols % tn == 0 and first_col % 8 == 0 and tm % PROJ_ROW_GROUPS == 0
    blocks = _nbytes((tm, k), BF16) + _nbytes((tm, tn), BF16) + _nbytes((1, tn), F32)
    resident = _nbytes((tn, k), F32) + _nbytes((tn, k), BF16) + 3 * _nbytes((tm, tn), F32)
    gb_specs = [] if gb is None else [pl.BlockSpec((1, tn), lambda j, i: (0, j))]
    gb_args = [] if gb is None else [gb.reshape(1, n_cols)]
    return pl.pallas_call(
        functools.partial(_inproj_cols_kernel, first_col=first_col, epilogue=epilogue),
        grid=(n_cols // tn, m // tm),
        in_specs=[pl.BlockSpec((tm, k), lambda j, i: (i, 0)), pl.BlockSpec(memory_space=pl.ANY)] + gb_specs,
        out_specs=pl.BlockSpec((tm, tn), lambda j, i: (i, j)),
        out_shape=jax.ShapeDtypeStruct((m, n_cols), BF16),
        scratch_shapes=[pltpu.VMEM((tn, k), F32), pltpu.VMEM((tn, k), BF16), pltpu.SemaphoreType.DMA(())],
        compiler_params=_params(("arbitrary", "arbitrary"), blocks, resident),
        name="inproj_" + epilogue.__name__.strip("_").replace("_epilogue", ""),
    )(xn, w_in_t, *gb_args)


def _rope_padded(v, cos, sin_lo, sin_hi):
    half = A_ROPE // 2
    return v * cos + pltpu.roll(v, LANE - half, 1) * sin_lo + pltpu.roll(v, half, 1) * sin_hi


def _ones_column(rows):
    lane = lax.broadcasted_iota(jnp.int32, (rows, V_PAD - A_V), 1)
    return jnp.where(lane == 0, 1.0, 0.0).astype(BF16)


def _mla_proj_kernel(za_ref, cos_ref, sl_ref, sh_ref, wuq_ref, wukv_ref, gq_ref, gkv_ref, gaq_ref, gak_ref,
                     q_ref, k_ref, v_ref, *, heads, q_lora, kv_lora):
    cos, sin_lo, sin_hi = cos_ref[...], sl_ref[...], sh_ref[...]

    def norm(z, g):
        r = lax.rsqrt(jnp.mean(z * z, axis=-1, keepdims=True) + EPS)
        return (z * r * g).astype(BF16)

    cq = norm(za_ref[:, :q_lora], gq_ref[...])
    ckv = norm(za_ref[:, q_lora:q_lora + kv_lora], gkv_ref[...])
    slab = za_ref[:, q_lora + kv_lora:q_lora + kv_lora + LANE]
    k_rope = jnp.where(lax.broadcasted_iota(jnp.int32, slab.shape, 1) < A_ROPE, slab, 0.0)
    qacc = jnp.dot(cq, wuq_ref[...], preferred_element_type=F32)
    kvacc = jnp.dot(ckv, wukv_ref[...], preferred_element_type=F32)

    gq_lo, gq_hi = gaq_ref[:, :LANE], gaq_ref[:, LANE:]
    gk_lo, gk_hi = gak_ref[:, :LANE], gak_ref[:, LANE:]
    kr_ss = jnp.sum(k_rope * k_rope, axis=-1, keepdims=True)
    kr_rot = _rope_padded(k_rope * gk_hi, cos, sin_lo, sin_hi)
    ones_col = _ones_column(slab.shape[0])
    for h in range(heads):
        base = h * A_HEAD_PAD
        q_lo = qacc[:, base:base + LANE]
        q_hi = qacc[:, base + LANE:base + A_HEAD_PAD]
        ss = jnp.sum(q_lo * q_lo, axis=-1, keepdims=True) + jnp.sum(q_hi * q_hi, axis=-1, keepdims=True)
        r = lax.rsqrt(ss / A_QK + EPS)
        q_ref[:, base:base + LANE] = (q_lo * r * gq_lo).astype(BF16)
        q_ref[:, base + LANE:base + A_HEAD_PAD] = _rope_padded(q_hi * r * gq_hi, cos, sin_lo, sin_hi).astype(BF16)

        k_lo = kvacc[:, base:base + LANE]
        ssk = jnp.sum(k_lo * k_lo, axis=-1, keepdims=True) + kr_ss
        rk = lax.rsqrt(ssk / A_QK + EPS)
        k_ref[:, base:base + LANE] = (k_lo * rk * gk_lo).astype(BF16)
        k_ref[:, base + LANE:base + A_HEAD_PAD] = (kr_rot * rk).astype(BF16)
        v_ref[:, h * V_PAD:h * V_PAD + A_V] = kvacc[:, base + LANE:base + A_HEAD_PAD].astype(BF16)
        v_ref[:, h * V_PAD + A_V:(h + 1) * V_PAD] = ones_col


def _mla_proj(za, cos, sin_lo, sin_hi, wuq, wukv, gq, gkv, gaq, gak, heads, tm):
    t, za_cols = za.shape
    q_lora, kv_lora = wuq.shape[0], wukv.shape[0]
    hp = heads * A_HEAD_PAD
    row = lambda i: (i, 0)
    fix = lambda i: (0, 0)
    blocks = (_nbytes((tm, za_cols), F32) + 3 * _nbytes((tm, LANE), F32) + _nbytes(wuq.shape, BF16)
              + _nbytes(wukv.shape, BF16) + 2 * _nbytes((tm, hp), BF16) + _nbytes((tm, heads * V_PAD), BF16))
    return pl.pallas_call(
        functools.partial(_mla_proj_kernel, heads=heads, q_lora=q_lora, kv_lora=kv_lora),
        grid=(t // tm,),
        in_specs=[pl.BlockSpec((tm, za_cols), row),
                  pl.BlockSpec((tm, LANE), row), pl.BlockSpec((tm, LANE), row), pl.BlockSpec((tm, LANE), row),
                  pl.BlockSpec(wuq.shape, fix), pl.BlockSpec(wukv.shape, fix),
                  pl.BlockSpec((1, q_lora), fix), pl.BlockSpec((1, kv_lora), fix),
                  pl.BlockSpec((1, A_HEAD_PAD), fix), pl.BlockSpec((1, A_HEAD_PAD), fix)],
        out_specs=[pl.BlockSpec((tm, hp), row), pl.BlockSpec((tm, hp), row), pl.BlockSpec((tm, heads * V_PAD), row)],
        out_shape=[jax.ShapeDtypeStruct((t, hp), BF16), jax.ShapeDtypeStruct((t, hp), BF16),
                   jax.ShapeDtypeStruct((t, heads * V_PAD), BF16)],
        compiler_params=_params(("parallel",), blocks, 3 * _nbytes((tm, hp), F32)),
        name="mla_proj",
    )(za, cos, sin_lo, sin_hi, wuq, wukv, gq, gkv, gaq, gak)


def _mla_scores(i, q_ref, k_ref, diag_ok):
    tq = ATTN_TILE
    q = q_ref[i * tq:(i + 1) * tq, :]
    sd = jnp.where(diag_ok, _nt_dot(q, k_ref[i * tq:(i + 1) * tq, :]), NEG_INF)
    s0 = _nt_dot(q, k_ref[:i * tq, :]) if i > 0 else None
    return sd, s0


def _mla_softmax(sd, s0):
    m = jnp.max(sd, axis=-1, keepdims=True)
    if s0 is None:
        return jnp.exp(sd - m).astype(BF16), None
    m = jnp.maximum(m, jnp.max(s0, axis=-1, keepdims=True))
    return jnp.exp(sd - m).astype(BF16), jnp.exp(s0 - m).astype(BF16)


def _mla_values(i, pd, p0, v_ref, o_ref):
    tq = ATTN_TILE
    o = jnp.dot(pd, v_ref[i * tq:(i + 1) * tq, :], preferred_element_type=F32)
    if p0 is not None:
        o = o + jnp.dot(p0, v_ref[:i * tq, :], preferred_element_type=F32)
    o_ref[i * tq:(i + 1) * tq, :] = (o[:, :A_V] / o[:, A_V:A_V + 1]).astype(o_ref.dtype)


def _band_window(i):
    left = B_LEFT_CHUNKS * CHUNK
    q0 = i * ATTN_TILE
    k0 = max(0, q0 - left)
    return q0, k0, q0 + ATTN_TILE - k0, left - q0 + k0


def _band_scores(i, q_ref, k_ref, table):
    q0, k0, kw, u0 = _band_window(i)
    return _nt_dot(q_ref[q0:q0 + ATTN_TILE, :], k_ref[k0:k0 + kw, :]) + table[:, u0:u0 + kw]


def _band_softmax(s):
    return jnp.exp(s - jnp.max(s, axis=-1, keepdims=True)).astype(BF16)


def _band_values(i, p, v_ref, o_ref):
    q0, k0, kw, _ = _band_window(i)
    o = jnp.dot(p, v_ref[k0:k0 + kw, :], preferred_element_type=F32)
    o_ref[q0:q0 + ATTN_TILE, :] = (o[:, :B_HEAD_DIM] / o[:, B_HEAD_DIM:B_HEAD_DIM + 1]).astype(o_ref.dtype)


def _attn_kernel(aq_ref, ak_ref, av_ref, bq_ref, bk_ref, bv_ref, r_ref, oa_ref, ob_ref, bv_pad, *, seq):
    tq = ATTN_TILE
    left = B_LEFT_CHUNKS * CHUNK
    width = r_ref.shape[1]
    rc = lax.broadcasted_iota(jnp.int32, (tq, tq), 0) // CHUNK
    cc = lax.broadcasted_iota(jnp.int32, (tq, tq), 1) // CHUNK
    diag_ok = cc <= rc
    bias = pltpu.roll(jnp.broadcast_to(r_ref[...], (tq, width)), 0, 1, stride=1, stride_axis=0)[:, :left + tq]
    q_chunk = lax.broadcasted_iota(jnp.int32, (tq, left + tq), 0) // CHUNK
    k_chunk = lax.broadcasted_iota(jnp.int32, (tq, left + tq), 1) // CHUNK
    table = jnp.where((k_chunk >= q_chunk) & (k_chunk <= q_chunk + B_LEFT_CHUNKS), bias, NEG_INF)
    bv_pad[:, :B_HEAD_DIM] = bv_ref[...]
    bv_pad[:, B_HEAD_DIM:] = _ones_column(seq)

    n_tiles = seq // tq
    sd, s0 = _mla_scores(0, aq_ref, ak_ref, diag_ok)
    for i in range(n_tiles):
        sb = _band_scores(i, bq_ref, bk_ref, table)
        pd, p0 = _mla_softmax(sd, s0)
        if i + 1 < n_tiles:
            sd, s0 = _mla_scores(i + 1, aq_ref, ak_ref, diag_ok)
        _mla_values(i, pd, p0, av_ref, oa_ref)
        _band_values(i, _band_softmax(sb), bv_pad, ob_ref)


def _band_bias_rows(rel_bias):
    left = B_LEFT_CHUNKS * CHUNK
    width = left + 2 * ATTN_TILE
    m = jnp.arange(width, dtype=jnp.int32)
    j = jnp.where(m < left + ATTN_TILE, m, m - width)
    dist = left - j
    rows = rel_bias[:, jnp.clip(dist, -B_MAX_REL, B_MAX_REL) + B_MAX_REL].astype(F32)
    return rows.reshape(rel_bias.shape[0], 1, width)


def _attention(q, k, v, bqk, bv, bias_rows, batch, seq, heads):
    t = batch * seq
    d = B_HEAD_DIM
    width = bias_rows.shape[2]
    head = lambda b, h: (b, h)
    blocks = (2 * _nbytes((seq, A_HEAD_PAD), BF16) + _nbytes((seq, V_PAD), BF16) + _nbytes((seq, A_V), BF16)
              + 4 * _nbytes((seq, d), BF16) + _nbytes((1, width), F32))
    temps = _nbytes((seq, V_PAD), BF16) + 8 * _nbytes((ATTN_TILE, seq), F32) + 8 * _nbytes((ATTN_TILE, width), F32)
    return pl.pallas_call(
        functools.partial(_attn_kernel, seq=seq),
        grid=(batch, heads),
        in_specs=[pl.BlockSpec((seq, A_HEAD_PAD), head), pl.BlockSpec((seq, A_HEAD_PAD), head),
                  pl.BlockSpec((seq, V_PAD), head),
                  pl.BlockSpec((seq, d), head),
                  pl.BlockSpec((seq, d), lambda b, h: (b, heads + h)),
                  pl.BlockSpec((seq, d), head),
                  pl.BlockSpec((None, 1, width), lambda b, h: (h, 0, 0))],
        out_specs=[pl.BlockSpec((seq, A_V), head), pl.BlockSpec((seq, d), head)],
        out_shape=[jax.ShapeDtypeStruct((t, heads * A_V), BF16), jax.ShapeDtypeStruct((t, heads * d), BF16)],
        scratch_shapes=[pltpu.VMEM((seq, V_PAD), BF16)],
        compiler_params=_params(("parallel", "parallel"), blocks, temps),
        name="attention",
    )(q, k, v, bqk, bqk, bv, bias_rows)


def _merge_kernel(oa_ref, ob_ref, woa_ref, wob_ref, g0_ref, g1_ref, o_ref, woa_bf, wob_bf):
    @pl.when(pl.program_id(1) == 0)
    def _():
        woa_bf[...] = woa_ref[...].astype(BF16)
        wob_bf[...] = wob_ref[...].astype(BF16)

    a = jnp.dot(oa_ref[...], woa_bf[...], preferred_element_type=F32)
    b = jnp.dot(ob_ref[...], wob_bf[...], preferred_element_type=F32)
    o_ref[...] = (g0_ref[...].astype(F32) * a + g1_ref[...].astype(F32) * b).astype(o_ref.dtype)


def _merge(oa, ob, woa, wob, gates, tm, tn):
    m = oa.shape[0]
    d = woa.shape[1]
    g1 = d // tn
    blocks = (_nbytes((tm, oa.shape[1]), BF16) + _nbytes((tm, ob.shape[1]), BF16) + _nbytes((woa.shape[0], tn), F32)
              + _nbytes((wob.shape[0], tn), F32) + 3 * _nbytes((tm, tn), BF16))
    resident = (2 * _nbytes((woa.shape[0], tn), BF16) + 2 * _nbytes((wob.shape[0], tn), BF16)
                + 3 * _nbytes((tm, tn), F32))
    return pl.pallas_call(
        _merge_kernel,
        grid=(d // tn, m // tm),
        in_specs=[pl.BlockSpec((tm, oa.shape[1]), lambda j, i: (i, 0)),
                  pl.BlockSpec((tm, ob.shape[1]), lambda j, i: (i, 0)),
                  pl.BlockSpec((woa.shape[0], tn), lambda j, i: (0, j)),
                  pl.BlockSpec((wob.shape[0], tn), lambda j, i: (0, j)),
                  pl.BlockSpec((tm, tn), lambda j, i: (i, j)),
                  pl.BlockSpec((tm, tn), lambda j, i: (i, g1 + j))],
        out_specs=pl.BlockSpec((tm, tn), lambda j, i: (i, j)),
        out_shape=jax.ShapeDtypeStruct((m, d), BF16),
        scratch_shapes=[pltpu.VMEM((woa.shape[0], tn), BF16), pltpu.VMEM((wob.shape[0], tn), BF16)],
        compiler_params=_params(("parallel", "arbitrary"), blocks, resident),
        name="merge",
    )(oa, ob, woa, wob, gates, gates)


def _out_proj_kernel(a_ref, w_ref, x_ref, o_ref, w_bf):
    @pl.when(pl.program_id(1) == 0)
    def _():
        w_bf[...] = w_ref[...].astype(BF16)

    o_ref[...] = x_ref[...] + jnp.dot(a_ref[...], w_bf[...], preferred_element_type=F32)


def _out_proj(merged, wout, x, tm, tn):
    m, k = merged.shape
    n = wout.shape[1]
    blocks = _nbytes((tm, k), BF16) + _nbytes((k, tn), F32) + 2 * _nbytes((tm, tn), F32)
    return pl.pallas_call(
        _out_proj_kernel,
        grid=(n // tn, m // tm),
        in_specs=[pl.BlockSpec((tm, k), lambda j, i: (i, 0)),
                  pl.BlockSpec((k, tn), lambda j, i: (0, j)),
                  pl.BlockSpec((tm, tn), lambda j, i: (i, j))],
        out_specs=pl.BlockSpec((tm, tn), lambda j, i: (i, j)),
        out_shape=jax.ShapeDtypeStruct((m, n), F32),
        scratch_shapes=[pltpu.VMEM((k, tn), BF16)],
        compiler_params=_params(("parallel", "arbitrary"), blocks, 2 * _nbytes((k, tn), BF16) + _nbytes((tm, tn), F32)),
        name="out_proj",
    )(merged, wout, x)


def _split_bf16(v):
    hi = v.astype(BF16)
    return hi, (v - hi.astype(F32)).astype(BF16)


def _router_kernel(x_ref, g_ref, wr_ref, xn_ref, route_ref, *, n_groups, per_group):
    x = x_ref[...]
    r = lax.rsqrt(jnp.mean(x * x, axis=-1, keepdims=True) + EPS)
    xn = x * r * g_ref[...]
    xn_ref[...] = xn
    x_hi, x_lo = _split_bf16(xn)
    w_hi, w_lo = _split_bf16(wr_ref[...])
    dot = functools.partial(jnp.dot, preferred_element_type=F32)
    logits = dot(x_hi, w_hi) + (dot(x_lo, w_hi) + dot(x_hi, w_lo))
    lane = lax.broadcasted_iota(jnp.int32, logits.shape, 1).astype(F32)
    far = float(LANE)

    def top(vals):
        best = jnp.max(vals, axis=-1, keepdims=True)
        return best, jnp.min(jnp.where(vals == best, lane, far), axis=-1, keepdims=True)

    gl = jnp.where(lane < n_groups, logits, NEG_INF)
    gmax, grp = top(gl)
    p_grp = 1.0 / jnp.sum(jnp.exp(gl - gmax), axis=-1, keepdims=True)
    lo = n_groups + grp * per_group
    el = jnp.where((lane >= lo) & (lane < lo + per_group), logits, NEG_INF)
    t1, i1 = top(el)
    t2, i2 = top(jnp.where(lane == i1, NEG_INF, el))
    d = jnp.exp(t2 - t1)
    w1 = p_grp / (1.0 + d)
    w2 = p_grp * d / (1.0 + d)
    route_ref[...] = jnp.where(lane == 0, i1 - n_groups,
                               jnp.where(lane == 1, i2 - n_groups,
                                         jnp.where(lane == 2, w1, jnp.where(lane == 3, w2, 0.0))))


def _router(x1, g, wr, n_groups, per_group, tm):
    t, d = x1.shape
    blocks = 2 * _nbytes((tm, d), F32) + _nbytes((1, d), F32) + _nbytes((d, LANE), F32) + _nbytes((tm, LANE), F32)
    return pl.pallas_call(
        functools.partial(_router_kernel, n_groups=n_groups, per_group=per_group),
        grid=(t // tm,),
        in_specs=[pl.BlockSpec((tm, d), lambda i: (i, 0)), pl.BlockSpec((1, d), lambda i: (0, 0)),
                  pl.BlockSpec((d, LANE), lambda i: (0, 0))],
        out_specs=[pl.BlockSpec((tm, d), lambda i: (i, 0)), pl.BlockSpec((tm, LANE), lambda i: (i, 0))],
        out_shape=[jax.ShapeDtypeStruct((t, d), F32), jax.ShapeDtypeStruct((t, LANE), F32)],
        compiler_params=_params(("parallel",), blocks, 2 * _nbytes((tm, d), F32)),
        name="router",
    )(x1, g.reshape(1, d), wr)


def _one_hots(route):
    lane = lax.broadcasted_iota(jnp.int32, route.shape, 1).astype(F32)
    return (lane == route[:, 0:1]).astype(F32), (lane == route[:, 1:2]).astype(F32)


def _rank_kernel(route_ref, rank_ref, starts_ref, count_acc, start_acc):
    i = pl.program_id(0)

    @pl.when(i == 0)
    def _():
        count_acc[...] = jnp.zeros_like(count_acc)
        start_acc[...] = jnp.zeros_like(start_acc)

    oh1, oh2 = _one_hots(route_ref[...])
    oh = (oh1 + oh2).astype(BF16)
    tm = oh.shape[0]
    earlier = (lax.broadcasted_iota(jnp.int32, (tm, tm), 0) > lax.broadcasted_iota(jnp.int32, (tm, tm), 1))
    before = jnp.dot(earlier.astype(BF16), oh, preferred_element_type=F32) + count_acc[...]
    lane = lax.broadcasted_iota(jnp.int32, (tm, LANE), 1)
    rank_ref[...] = jnp.where(lane == 0, jnp.sum(oh1 * before, axis=-1, keepdims=True),
                              jnp.where(lane == 1, jnp.sum(oh2 * before, axis=-1, keepdims=True), 0.0))
    lower = (lax.broadcasted_iota(jnp.int32, (LANE, LANE), 0) < lax.broadcasted_iota(jnp.int32, (LANE, LANE), 1))
    below = jnp.dot(oh, lower.astype(BF16), preferred_element_type=F32)
    count_acc[...] += jnp.sum(oh.astype(F32), axis=0, keepdims=True)
    start_acc[...] += jnp.sum(below, axis=0, keepdims=True)
    starts_ref[...] = start_acc[...]


def _rank(route, tm):
    t = route.shape[0]
    blocks = 2 * _nbytes((tm, LANE), F32) + _nbytes((1, LANE), F32)
    return pl.pallas_call(
        _rank_kernel,
        grid=(t // tm,),
        in_specs=[pl.BlockSpec((tm, LANE), lambda i: (i, 0))],
        out_specs=[pl.BlockSpec((tm, LANE), lambda i: (i, 0)), pl.BlockSpec((1, LANE), lambda i: (0, 0))],
        out_shape=[jax.ShapeDtypeStruct((t, LANE), F32), jax.ShapeDtypeStruct((1, LANE), F32)],
        scratch_shapes=[pltpu.VMEM((1, LANE), F32), pltpu.VMEM((1, LANE), F32)],
        compiler_params=_params(("arbitrary",), blocks, 2 * _nbytes((tm, tm), F32)),
        name="moe_rank",
    )(route)


def _dest_kernel(route_ref, rank_ref, starts_ref, dest_ref):
    oh1, oh2 = _one_hots(route_ref[...])
    rank = rank_ref[...]
    starts = starts_ref[...]
    d1 = jnp.sum(oh1 * starts, axis=-1, keepdims=True) + rank[:, 0:1]
    d2 = jnp.sum(oh2 * starts, axis=-1, keepdims=True) + rank[:, 1:2]
    lane = lax.broadcasted_iota(jnp.int32, rank.shape, 1)
    dest_ref[...] = jnp.where(lane == 0, d1, jnp.where(lane == 1, d2, 0.0)).astype(jnp.int32)


def _dest(route, rank, starts, tm):
    t = route.shape[0]
    blocks = 3 * _nbytes((tm, LANE), F32) + _nbytes((1, LANE), F32)
    return pl.pallas_call(
        _dest_kernel,
        grid=(t // tm,),
        in_specs=[pl.BlockSpec((tm, LANE), lambda i: (i, 0)), pl.BlockSpec((tm, LANE), lambda i: (i, 0)),
                  pl.BlockSpec((1, LANE), lambda i: (0, 0))],
        out_specs=pl.BlockSpec((tm, LANE), lambda i: (i, 0)),
        out_shape=jax.ShapeDtypeStruct((t, LANE), jnp.int32),
        compiler_params=_params(("parallel",), blocks, 4 * _nbytes((tm, LANE), F32)),
        name="moe_dest",
    )(route, rank, starts)


def _gather_rows_kernel(dest_ref, x_hbm, xs_ref, source, rows_f32, sem, *, n_assign):
    half = xs_ref.shape[0] // 2
    i = pl.program_id(0)

    def copy(src_row, slot, j):
        return pltpu.make_async_copy(x_hbm.at[pl.ds(src_row, 1)], rows_f32.at[slot, pl.ds(j, 1)], sem.at[slot])

    def start_half(block, slot):
        for j in range(half):
            copy(source[block * half + j], slot, j).start()

    def finish_half(slot):
        for j in range(half):
            copy(0, slot, j).wait()
        xs_ref[slot * half:(slot + 1) * half, :] = rows_f32[slot].astype(xs_ref.dtype)

    @pl.when(i == 0)
    def _():
        def invert(tok, carry):
            for k in range(TOP_K):
                source[dest_ref[TOP_K * tok + k]] = tok
            return carry
        lax.fori_loop(0, n_assign // TOP_K, invert, 0, unroll=4)
        start_half(2 * i, 0)

    start_half(2 * i + 1, 1)
    finish_half(0)

    @pl.when(i + 1 < pl.num_programs(0))
    def _():
        start_half(2 * i + 2, 0)

    finish_half(1)


def _gather_rows(dest_flat, xn, rows):
    t, d = xn.shape
    n_assign = dest_flat.shape[0]
    return pl.pallas_call(
        functools.partial(_gather_rows_kernel, n_assign=n_assign),
        grid_spec=pltpu.PrefetchScalarGridSpec(
            num_scalar_prefetch=1,
            grid=(n_assign // rows,),
            in_specs=[pl.BlockSpec(memory_space=pl.ANY)],
            out_specs=pl.BlockSpec((rows, d), lambda i, dest: (i, 0)),
            scratch_shapes=[pltpu.SMEM((n_assign,), jnp.int32), pltpu.VMEM((2, rows // 2, d), xn.dtype),
                            pltpu.SemaphoreType.DMA((2,))]),
        out_shape=jax.ShapeDtypeStruct((n_assign, d), BF16),
        compiler_params=_params(("arbitrary",), _nbytes((rows, d), BF16), 2 * _nbytes((rows, d), xn.dtype)),
        name="moe_gather_rows",
    )(dest_flat, xn)


def _work_items(starts, n_rows):
    n_exp = starts.shape[0]
    n_blk = n_rows // MOE_ROWS
    total = jnp.full((1,), n_rows, jnp.int32)
    pts = jnp.concatenate([jnp.arange(n_blk, dtype=jnp.int32) * MOE_ROWS, starts[1:]])
    idx = jnp.arange(pts.shape[0], dtype=jnp.int32)
    before = (pts[None, :] < pts[:, None]) | ((pts[None, :] == pts[:, None]) & (idx[None, :] < idx[:, None]))
    pos = jnp.sum(before.astype(jnp.int32), axis=1)
    lo = jnp.sum(jnp.where(pos[:, None] == idx[None, :], pts[:, None], 0), axis=0)
    hi = jnp.concatenate([lo[1:], total])
    ends = jnp.concatenate([starts[1:], total])
    r = jnp.minimum(lo // MOE_ROWS, n_blk - 1)
    e = jnp.minimum(jnp.sum((ends[None, :] <= lo[:, None]).astype(jnp.int32), axis=1), n_exp - 1)
    later = jnp.where(e[None, :] > e[:, None], e[None, :], n_exp)
    nxt = jnp.min(later, axis=1)
    nxt = jnp.where(nxt == n_exp, -1, nxt)
    return r, e, lo, hi, nxt


def _stream_expert_weights(w, e_ref, nxt_ref, streams):
    e = e_ref[w]

    def copies(stream, expert):
        hbm, stage, _, sem = stream
        rows = stage.shape[0] // WEIGHT_DMA_PARTS
        parts = [pl.ds(part * rows, rows) for part in range(WEIGHT_DMA_PARTS)]
        return [pltpu.make_async_copy(hbm.at[expert, sl], stage.at[sl], sem) for sl in parts]

    @pl.when(w == 0)
    def _():
        for stream in streams:
            for c in copies(stream, e):
                c.start()

    @pl.when((w == 0) | (e != e_ref[jnp.maximum(w - 1, 0)]))
    def _():
        nxt = nxt_ref[w]
        for stream in streams:
            for c in copies(stream, e):
                c.wait()
            _, stage, w_bf, _ = stream
            rows = stage.shape[0] // CONVERT_PARTS
            for part in range(CONVERT_PARTS):
                sl = slice(part * rows, (part + 1) * rows)
                w_bf[sl, :] = stage[sl, :].astype(BF16)

            @pl.when(nxt >= 0)
            def _():
                for c in copies(stream, nxt):
                    c.start()


def _store_item_rows(o_ref, val, r, lo, hi):
    rows = r * MOE_ROWS + lax.broadcasted_iota(jnp.int32, val.shape, 0)
    mine = (rows >= lo) & (rows < hi)

    @pl.when(lo == r * MOE_ROWS)
    def _():
        o_ref[...] = val

    @pl.when(lo != r * MOE_ROWS)
    def _():
        pltpu.store(o_ref, val, mask=mine)


def _moe_experts_kernel(r_ref, e_ref, lo_ref, hi_ref, nxt_ref, xs_ref, wg_hbm, wu_hbm, wd_hbm, y_ref,
                        wg_stage, wu_stage, wd_stage, wg_bf, wu_bf, wd_bf, sem):
    w = pl.program_id(0)
    _stream_expert_weights(w, e_ref, nxt_ref, [(wg_hbm, wg_stage, wg_bf, sem.at[0]),
                                                 (wu_hbm, wu_stage, wu_bf, sem.at[1]),
                                                 (wd_hbm, wd_stage, wd_bf, sem.at[2])])
    r, lo, hi = r_ref[w], lo_ref[w], hi_ref[w]

    @pl.when(hi > lo)
    def _():
        x = xs_ref[...]
        g = jnp.dot(x, wg_bf[...], preferred_element_type=F32)
        u = jnp.dot(x, wu_bf[...], preferred_element_type=F32)
        h = ((g * (1.0 / (1.0 + jnp.exp(-g)))) * u).astype(BF16)
        y = jnp.dot(h, wd_bf[...], preferred_element_type=F32)
        _store_item_rows(y_ref, y, r, lo, hi)


def _moe_experts(items, xs, wg, wu, wd):
    n_rows, d = xs.shape
    f = wg.shape[2]
    n_items = items[0].shape[0]
    blocks = _nbytes((MOE_ROWS, d), BF16) + _nbytes((MOE_ROWS, d), F32)
    resident = (3 * _nbytes((d, f), F32) + 3 * _nbytes((d, f), BF16) + 3 * _nbytes((MOE_ROWS, d), F32)
                + 6 * _nbytes((MOE_ROWS, f), F32))
    row_block = lambda w, r, e, lo, hi, nxt: (r[w], 0)
    any_space = pl.BlockSpec(memory_space=pl.ANY)
    return pl.pallas_call(
        _moe_experts_kernel,
        grid_spec=pltpu.PrefetchScalarGridSpec(
            num_scalar_prefetch=5,
            grid=(n_items,),
            in_specs=[pl.BlockSpec((MOE_ROWS, d), row_block), any_space, any_space, any_space],
            out_specs=pl.BlockSpec((MOE_ROWS, d), row_block),
            scratch_shapes=[pltpu.VMEM((d, f), F32), pltpu.VMEM((d, f), F32), pltpu.VMEM((f, d), F32),
                            pltpu.VMEM((d, f), BF16), pltpu.VMEM((d, f), BF16), pltpu.VMEM((f, d), BF16),
                            pltpu.SemaphoreType.DMA((3,))]),
        out_shape=jax.ShapeDtypeStruct((n_rows, d), F32),
        compiler_params=_params(("arbitrary",), blocks, resident),
        name="moe_experts",
    )(*items, xs, wg, wu, wd)


def _combine_kernel(dest_ref, x_ref, route_ref, y_hbm, o_ref, ybuf, sem, *, tm):
    half = tm // 2
    i = pl.program_id(0)

    def copy(src_row, slot, k, t):
        return pltpu.make_async_copy(y_hbm.at[pl.ds(src_row, 1)], ybuf.at[slot, k, pl.ds(t, 1)], sem.at[slot])

    def start_half(index, slot):
        for t in range(half):
            for k in range(TOP_K):
                copy(dest_ref[TOP_K * (index * half + t) + k], slot, k, t).start()

    def finish_half(slot):
        for t in range(half):
            for k in range(TOP_K):
                copy(0, slot, k, t).wait()
        rows = slice(slot * half, (slot + 1) * half)
        route = route_ref[rows, :]
        o_ref[rows, :] = x_ref[rows, :] + (route[:, 2:3] * ybuf[slot, 0] + route[:, 3:4] * ybuf[slot, 1])

    @pl.when(i == 0)
    def _():
        start_half(2 * i, 0)

    start_half(2 * i + 1, 1)
    finish_half(0)

    @pl.when(i + 1 < pl.num_programs(0))
    def _():
        start_half(2 * i + 2, 0)

    finish_half(1)


def _combine(dest_flat, x1, route, y, tm):
    t, d = x1.shape
    blocks = 2 * _nbytes((tm, d), F32) + _nbytes((tm, LANE), F32)
    resident = TOP_K * _nbytes((tm, d), F32) + _nbytes((tm, d), F32)
    return pl.pallas_call(
        functools.partial(_combine_kernel, tm=tm),
        grid_spec=pltpu.PrefetchScalarGridSpec(
            num_scalar_prefetch=1,
            grid=(t // tm,),
            in_specs=[pl.BlockSpec((tm, d), lambda i, dest: (i, 0)),
                      pl.BlockSpec((tm, LANE), lambda i, dest: (i, 0)),
                      pl.BlockSpec(memory_space=pl.ANY)],
            out_specs=pl.BlockSpec((tm, d), lambda i, dest: (i, 0)),
            scratch_shapes=[pltpu.VMEM((2, TOP_K, tm // 2, d), F32), pltpu.SemaphoreType.DMA((2,))]),
        out_shape=jax.ShapeDtypeStruct((t, d), F32),
        compiler_params=_params(("arbitrary",), blocks, resident),
        name="moe_combine",
    )(dest_flat, x1, route, y)


def _pad_cols(w, n):
    return jnp.pad(w, ((0, 0), (0, n - w.shape[1])))


def kernel(x, positions, g_mix, w_in, b_gate, q_norm_g, kv_norm_g, w_uq, w_ukv, a_q_norm_g, a_k_norm_g,
           b_q_norm_g, b_k_norm_g, rel_bias, w_o_a, w_o_b, w_out, g_ffn, w_group, w_expert,
           w_exp_gate, w_exp_up, w_exp_down):
    batch, seq, d = x.shape
    t = batch * seq
    q_lora, kv_lora = q_norm_g.shape[0], kv_norm_g.shape[0]
    a_heads = w_uq.shape[1] // A_QK
    b_heads = w_o_b.shape[0] // B_HEAD_DIM
    b_width = b_heads * B_HEAD_DIM
    n_groups, n_experts = w_group.shape[1], w_expert.shape[1]
    per_group = n_experts // n_groups
    off_b = q_lora + kv_lora + A_ROPE
    assert seq % ATTN_TILE == 0 and (TOP_K * t) % GATHER_ROWS == 0 and n_groups + n_experts <= LANE
    assert a_heads == b_heads

    xf = x.reshape(t, d)
    tm_big = min(1024, t)
    tn = _tile(b_width, 512)
    tn_wide = _tile(b_width, 1024)
    assert d % tn == 0 and d % tn_wide == 0

    za_cols = -(-(q_lora + kv_lora + LANE) // tn) * tn
    wuq = jnp.pad(w_uq.reshape(q_lora, a_heads, A_QK), ((0, 0), (0, 0), (0, A_HEAD_PAD - A_QK)))
    wuq = wuq.reshape(q_lora, a_heads * A_HEAD_PAD).astype(BF16)
    wukv = w_ukv.astype(BF16)
    pad_gain = lambda g, s: jnp.pad(g * s, (0, A_HEAD_PAD - A_QK)).reshape(1, A_HEAD_PAD)
    gaq = pad_gain(a_q_norm_g, A_QK ** -0.5)
    gak = pad_gain(a_k_norm_g, 1.0)
    g_bqk = jnp.concatenate([jnp.tile(b_q_norm_g * B_HEAD_DIM ** -0.5, b_heads), jnp.tile(b_k_norm_g, b_heads)])

    half = A_ROPE // 2
    inv = ROPE_THETA ** (-jnp.arange(half, dtype=F32) / half)
    ang = positions.astype(F32).reshape(t, 1) * inv
    cos, sin = jnp.cos(ang), jnp.sin(ang)
    zeros = jnp.zeros((t, half), F32)
    cos_t = jnp.concatenate([cos, cos, zeros, zeros], axis=1)
    sin_lo = jnp.concatenate([-sin, zeros, zeros, zeros], axis=1)
    sin_hi = jnp.concatenate([zeros, sin, zeros, zeros], axis=1)

    xn = _rmsnorm_rows(xf, g_mix, BF16, min(256, t))
    w_in_t = w_in.T
    za = _inproj_a(xn, w_in_t, za_cols, tm_big, tn)
    bqk = _inproj_cols(xn, w_in_t, off_b, 2 * b_width, g_bqk, _head_norm_epilogue, tm_big, tn_wide)
    bv = _inproj_cols(xn, w_in_t, off_b + 2 * b_width, b_width, None, _plain_epilogue, tm_big, tn_wide)
    gates = _inproj_cols(xn, w_in_t, off_b + 3 * b_width, 2 * d, b_gate, _sigmoid_epilogue, tm_big, tn_wide)
    q, k, v = _mla_proj(za, cos_t, sin_lo, sin_hi, wuq, wukv, q_norm_g.reshape(1, -1), kv_norm_g.reshape(1, -1),
                        gaq, gak, a_heads, min(256, t))
    o_a, o_b = _attention(q, k, v, bqk, bv, _band_bias_rows(rel_bias), batch, seq, a_heads)
    merged = _merge(o_a, o_b, w_o_a, w_o_b, gates, tm_big, tn)
    x1 = _out_proj(merged, w_out, xf, tm_big, tn)

    wr = _pad_cols(jnp.concatenate([w_group, w_expert], axis=1), LANE)
    xn2, route = _router(x1, g_ffn, wr, n_groups, per_group, min(256, t))
    rank, starts_f = _rank(route, min(512, t))
    dest = _dest(route, rank, starts_f, min(512, t))[:, :TOP_K].reshape(-1)
    xs = _gather_rows(dest, xn2, GATHER_ROWS)
    items = _work_items(starts_f[0, :n_experts].astype(jnp.int32), TOP_K * t)
    y = _moe_experts(items, xs, w_exp_gate, w_exp_up, w_exp_down)
    out = _combine(dest, x1, route, y, min(256, t))
    return out.reshape(batch, seq, d)
```

```python
import functools

import jax
import jax.numpy as jnp
from jax import lax
from jax.experimental import pallas as pl
from jax.experimental.pallas import tpu as pltpu

F32 = jnp.float32
BF16 = jnp.bfloat16

CHUNK = 64
EPS = 1e-6
A_NOPE = 128
A_ROPE = 64
A_V = 128
A_QK = A_NOPE + A_ROPE
B_HEAD_DIM = 128
B_LEFT_CHUNKS = 8
B_MAX_REL = 128
ROPE_THETA = 10000.0
TOP_K = 2

LANE = 128
A_HEAD_PAD = 2 * LANE
V_PAD = 2 * LANE
V7X_VMEM_BYTES = 64 * 2**20

ATTN_TILE = 256
MOE_ROWS = 128
WEIGHT_DMA_PARTS = 4
CONVERT_PARTS = 8
PROJ_ROW_GROUPS = 4
NEG_INF = float("-inf")


def _nbytes(shape, dtype):
    n = 1
    for s in shape:
        n *= s
    return n * jnp.dtype(dtype).itemsize


def _params(semantics, pipelined_bytes, resident_bytes=0):
    need = 2 * pipelined_bytes + resident_bytes
    return pltpu.CompilerParams(dimension_semantics=semantics,
                                vmem_limit_bytes=min(int(need), V7X_VMEM_BYTES))


def _tile(n, want):
    t = want
    while t > LANE and n % t:
        t //= 2
    assert n % t == 0, (n, want)
    return t


def _rmsnorm_kernel(x_ref, g_ref, o_ref):
    x = x_ref[...]
    r = lax.rsqrt(jnp.mean(x * x, axis=-1, keepdims=True) + EPS)
    o_ref[...] = (x * r * g_ref[...]).astype(o_ref.dtype)


def _rmsnorm_rows(x, g, out_dtype, tm):
    t, d = x.shape
    blocks = _nbytes((tm, d), F32) + _nbytes((tm, d), out_dtype) + _nbytes((1, d), F32)
    return pl.pallas_call(
        _rmsnorm_kernel,
        grid=(t // tm,),
        in_specs=[pl.BlockSpec((tm, d), lambda i: (i, 0)), pl.BlockSpec((1, d), lambda i: (0, 0))],
        out_specs=pl.BlockSpec((tm, d), lambda i: (i, 0)),
        out_shape=jax.ShapeDtypeStruct((t, d), out_dtype),
        compiler_params=_params(("parallel",), blocks, _nbytes((tm, d), F32)),
        name="rmsnorm",
    )(x, g.reshape(1, d))


def _nt_dot(a, b):
    return lax.dot_general(a, b, (((1,), (1,)), ((), ())), preferred_element_type=F32)


def _inproj_a_kernel(a_ref, wt_ref, o_ref, wt_bf):
    @pl.when(pl.program_id(1) == 0)
    def _():
        wt_bf[...] = wt_ref[...].astype(BF16)

    o_ref[...] = _nt_dot(a_ref[...], wt_bf[...])


def _inproj_a(xn, w_in_t, n_cols, tm, tn):
    m, k = xn.shape
    blocks = _nbytes((tm, k), BF16) + _nbytes((tn, k), F32) + _nbytes((tm, tn), F32)
    return pl.pallas_call(
        _inproj_a_kernel,
        grid=(n_cols // tn, m // tm),
        in_specs=[pl.BlockSpec((tm, k), lambda j, i: (i, 0)), pl.BlockSpec((tn, k), lambda j, i: (j, 0))],
        out_specs=pl.BlockSpec((tm, tn), lambda j, i: (i, j)),
        out_shape=jax.ShapeDtypeStruct((m, n_cols), F32),
        scratch_shapes=[pltpu.VMEM((tn, k), BF16)],
        compiler_params=_params(("parallel", "arbitrary"), blocks, 2 * _nbytes((tn, k), BF16)),
        name="inproj_a",
    )(xn, w_in_t)


def _head_norm_epilogue(acc, gb_ref):
    heads = []
    for h in range(acc.shape[1] // B_HEAD_DIM):
        sl = slice(h * B_HEAD_DIM, (h + 1) * B_HEAD_DIM)
        z = acc[:, sl]
        r = lax.rsqrt(jnp.mean(z * z, axis=-1, keepdims=True) + EPS)
        heads.append(z * r * gb_ref[:, sl])
    return jnp.concatenate(heads, axis=1)


def _plain_epilogue(acc, gb_ref):
    del gb_ref
    return acc


def _sigmoid_epilogue(acc, gb_ref):
    return 1.0 / (1.0 + jnp.exp(-(acc + gb_ref[...])))


def _inproj_cols_kernel(a_ref, wt_hbm, *refs, first_col, epilogue):
    gb_ref = refs[0] if len(refs) == 5 else None
    o_ref, stage, wt_bf, sem = refs[-4:]
    j = pl.program_id(0)
    tn = wt_bf.shape[0]

    def fetch(block):
        rows = pl.ds(pl.multiple_of(first_col + block * tn, 8), tn)
        return pltpu.make_async_copy(wt_hbm.at[rows], stage, sem)

    @pl.when(pl.program_id(1) == 0)
    def _():
        @pl.when(j == 0)
        def _():
            fetch(j).start()

        fetch(j).wait()
        wt_bf[...] = stage[...].astype(BF16)

        @pl.when(j + 1 < pl.num_programs(0))
        def _():
            fetch(j + 1).start()

    rows = a_ref.shape[0] // PROJ_ROW_GROUPS
    groups = [slice(s * rows, (s + 1) * rows) for s in range(PROJ_ROW_GROUPS)]
    accs = [_nt_dot(a_ref[sl, :], wt_bf[...]) for sl in groups]
    for sl, acc in zip(groups, accs):
        o_ref[sl, :] = epilogue(acc, gb_ref).astype(o_ref.dtype)


def _inproj_cols(xn, w_in_t, first_col, n_cols, gb, epilogue, tm, tn):
    m, k = xn.shape
    assert n_cols % tn == 0 and first_col % 8 == 0 and tm % PROJ_ROW_GROUPS == 0
    blocks = _nbytes((tm, k), BF16) + _nbytes((tm, tn), BF16) + _nbytes((1, tn), F32)
    resident = _nbytes((tn, k), F32) + _nbytes((tn, k), BF16) + 3 * _nbytes((tm, tn), F32)
    gb_specs = [] if gb is None else [pl.BlockSpec((1, tn), lambda j, i: (0, j))]
    gb_args = [] if gb is None else [gb.reshape(1, n_cols)]
    return pl.pallas_call(
        functools.partial(_inproj_cols_kernel, first_col=first_col, epilogue=epilogue),
        grid=(n_cols // tn, m // tm),
        in_specs=[pl.BlockSpec((tm, k), lambda j, i: (i, 0)), pl.BlockSpec(memory_space=pl.ANY)] + gb_specs,
        out_specs=pl.BlockSpec((tm, tn), lambda j, i: (i, j)),
        out_shape=jax.ShapeDtypeStruct((m, n_cols), BF16),
        scratch_shapes=[pltpu.VMEM((tn, k), F32), pltpu.VMEM((tn, k), BF16), pltpu.SemaphoreType.DMA(())],
        compiler_params=_params(("arbitrary", "arbitrary"), blocks, resident),
        name="inproj_" + epilogue.__name__.strip("_").replace("_epilogue", ""),
    )(xn, w_in_t, *gb_args)


def _rope_padded(v, cos, sin_lo, sin_hi):
    half = A_ROPE // 2
    return v * cos + pltpu.roll(v, LANE - half, 1) * sin_lo + pltpu.roll(v, half, 1) * sin_hi


def _ones_column(rows):
    lane = lax.broadcasted_iota(jnp.int32, (rows, V_PAD - A_V), 1)
    return jnp.where(lane == 0, 1.0, 0.0).astype(BF16)


def _mla_proj_kernel(za_ref, cos_ref, sl_ref, sh_ref, wuq_ref, wukv_ref, gq_ref, gkv_ref, gaq_ref, gak_ref,
                     q_ref, k_ref, v_ref, *, heads, q_lora, kv_lora):
    cos, sin_lo, sin_hi = cos_ref[...], sl_ref[...], sh_ref[...]

    def norm(z, g):
        r = lax.rsqrt(jnp.mean(z * z, axis=-1, keepdims=True) + EPS)
        return (z * r * g).astype(BF16)

    cq = norm(za_ref[:, :q_lora], gq_ref[...])
    ckv = norm(za_ref[:, q_lora:q_lora + kv_lora], gkv_ref[...])
    slab = za_ref[:, q_lora + kv_lora:q_lora + kv_lora + LANE]
    k_rope = jnp.where(lax.broadcasted_iota(jnp.int32, slab.shape, 1) < A_ROPE, slab, 0.0)
    qacc = jnp.dot(cq, wuq_ref[...], preferred_element_type=F32)
    kvacc = jnp.dot(ckv, wukv_ref[...], preferred_element_type=F32)

    gq_lo, gq_hi = gaq_ref[:, :LANE], gaq_ref[:, LANE:]
    gk_lo, gk_hi = gak_ref[:, :LANE], gak_ref[:, LANE:]
    kr_ss = jnp.sum(k_rope * k_rope, axis=-1, keepdims=True)
    kr_rot = _rope_padded(k_rope * gk_hi, cos, sin_lo, sin_hi)
    ones_col = _ones_column(slab.shape[0])
    for h in range(heads):
        base = h * A_HEAD_PAD
        q_lo = qacc[:, base:base + LANE]
        q_hi = qacc[:, base + LANE:base + A_HEAD_PAD]
        ss = jnp.sum(q_lo * q_lo, axis=-1, keepdims=True) + jnp.sum(q_hi * q_hi, axis=-1, keepdims=True)
        r = lax.rsqrt(ss / A_QK + EPS)
        q_ref[:, base:base + LANE] = (q_lo * r * gq_lo).astype(BF16)
        q_ref[:, base + LANE:base + A_HEAD_PAD] = _rope_padded(q_hi * r * gq_hi, cos, sin_lo, sin_hi).astype(BF16)

        k_lo = kvacc[:, base:base + LANE]
        ssk = jnp.sum(k_lo * k_lo, axis=-1, keepdims=True) + kr_ss
        rk = lax.rsqrt(ssk / A_QK + EPS)
        k_ref[:, base:base + LANE] = (k_lo * rk * gk_lo).astype(BF16)
        k_ref[:, base + LANE:base + A_HEAD_PAD] = (kr_rot * rk).astype(BF16)
        v_ref[:, h * V_PAD:h * V_PAD + A_V] = kvacc[:, base + LANE:base + A_HEAD_PAD].astype(BF16)
        v_ref[:, h * V_PAD + A_V:(h + 1) * V_PAD] = ones_col


def _mla_proj(za, cos, sin_lo, sin_hi, wuq, wukv, gq, gkv, gaq, gak, heads, tm):
    t, za_cols = za.shape
    q_lora, kv_lora = wuq.shape[0], wukv.shape[0]
    hp = heads * A_HEAD_PAD
    row = lambda i: (i, 0)
    fix = lambda i: (0, 0)
    blocks = (_nbytes((tm, za_cols), F32) + 3 * _nbytes((tm, LANE), F32) + _nbytes(wuq.shape, BF16)
              + _nbytes(wukv.shape, BF16) + 2 * _nbytes((tm, hp), BF16) + _nbytes((tm, heads * V_PAD), BF16))
    return pl.pallas_call(
        functools.partial(_mla_proj_kernel, heads=heads, q_lora=q_lora, kv_lora=kv_lora),
        grid=(t // tm,),
        in_specs=[pl.BlockSpec((tm, za_cols), row),
                  pl.BlockSpec((tm, LANE), row), pl.BlockSpec((tm, LANE), row), pl.BlockSpec((tm, LANE), row),
                  pl.BlockSpec(wuq.shape, fix), pl.BlockSpec(wukv.shape, fix),
                  pl.BlockSpec((1, q_lora), fix), pl.BlockSpec((1, kv_lora), fix),
                  pl.BlockSpec((1, A_HEAD_PAD), fix), pl.BlockSpec((1, A_HEAD_PAD), fix)],
        out_specs=[pl.BlockSpec((tm, hp), row), pl.BlockSpec((tm, hp), row), pl.BlockSpec((tm, heads * V_PAD), row)],
        out_shape=[jax.ShapeDtypeStruct((t, hp), BF16), jax.ShapeDtypeStruct((t, hp), BF16),
                   jax.ShapeDtypeStruct((t, heads * V_PAD), BF16)],
        compiler_params=_params(("parallel",), blocks, 3 * _nbytes((tm, hp), F32)),
        name="mla_proj",
    )(za, cos, sin_lo, sin_hi, wuq, wukv, gq, gkv, gaq, gak)


def _mla_scores(i, q_ref, k_ref, diag_ok):
    tq = ATTN_TILE
    q = q_ref[i * tq:(i + 1) * tq, :]
    sd = jnp.where(diag_ok, _nt_dot(q, k_ref[i * tq:(i + 1) * tq, :]), NEG_INF)
    s0 = _nt_dot(q, k_ref[:i * tq, :]) if i > 0 else None
    return sd, s0


def _mla_softmax(sd, s0):
    m = jnp.max(sd, axis=-1, keepdims=True)
    if s0 is None:
        return jnp.exp(sd - m).astype(BF16), None
    m = jnp.maximum(m, jnp.max(s0, axis=-1, keepdims=True))
    return jnp.exp(sd - m).astype(BF16), jnp.exp(s0 - m).astype(BF16)


def _mla_values(i, pd, p0, v_ref, o_ref):
    tq = ATTN_TILE
    o = jnp.dot(pd, v_ref[i * tq:(i + 1) * tq, :], preferred_element_type=F32)
    if p0 is not None:
        o = o + jnp.dot(p0, v_ref[:i * tq, :], preferred_element_type=F32)
    o_ref[i * tq:(i + 1) * tq, :] = (o[:, :A_V] / o[:, A_V:A_V + 1]).astype(o_ref.dtype)


def _band_window(i):
    left = B_LEFT_CHUNKS * CHUNK
    q0 = i * ATTN_TILE
    k0 = max(0, q0 - left)
    return q0, k0, q0 + ATTN_TILE - k0, left - q0 + k0


def _band_scores(i, q_ref, k_ref, table):
    q0, k0, kw, u0 = _band_window(i)
    return _nt_dot(q_ref[q0:q0 + ATTN_TILE, :], k_ref[k0:k0 + kw, :]) + table[:, u0:u0 + kw]


def _band_softmax(s):
    return jnp.exp(s - jnp.max(s, axis=-1, keepdims=True)).astype(BF16)


def _band_values(i, p, v_ref, o_ref):
    q0, k0, kw, _ = _band_window(i)
    o = jnp.dot(p, v_ref[k0:k0 + kw, :], preferred_element_type=F32)
    o_ref[q0:q0 + ATTN_TILE, :] = (o[:, :B_HEAD_DIM] / o[:, B_HEAD_DIM:B_HEAD_DIM + 1]).astype(o_ref.dtype)


def _attn_kernel(aq_ref, ak_ref, av_ref, bq_ref, bk_ref, bv_ref, r_ref, oa_ref, ob_ref, bv_pad, *, seq):
    tq = ATTN_TILE
    left = B_LEFT_CHUNKS * CHUNK
    width = r_ref.shape[1]
    rc = lax.broadcasted_iota(jnp.int32, (tq, tq), 0) // CHUNK
    cc = lax.broadcasted_iota(jnp.int32, (tq, tq), 1) // CHUNK
    diag_ok = cc <= rc
    bias = pltpu.roll(jnp.broadcast_to(r_ref[...], (tq, width)), 0, 1, stride=1, stride_axis=0)[:, :left + tq]
    q_chunk = lax.broadcasted_iota(jnp.int32, (tq, left + tq), 0) // CHUNK
    k_chunk = lax.broadcasted_iota(jnp.int32, (tq, left + tq), 1) // CHUNK
    table = jnp.where((k_chunk >= q_chunk) & (k_chunk <= q_chunk + B_LEFT_CHUNKS), bias, NEG_INF)
    bv_pad[:, :B_HEAD_DIM] = bv_ref[...]
    bv_pad[:, B_HEAD_DIM:] = _ones_column(seq)

    n_tiles = seq // tq
    sd, s0 = _mla_scores(0, aq_ref, ak_ref, diag_ok)
    for i in range(n_tiles):
        sb = _band_scores(i, bq_ref, bk_ref, table)
        pd, p0 = _mla_softmax(sd, s0)
        if i + 1 < n_tiles:
            sd, s0 = _mla_scores(i + 1, aq_ref, ak_ref, diag_ok)
        _mla_values(i, pd, p0, av_ref, oa_ref)
        _band_values(i, _band_softmax(sb), bv_pad, ob_ref)


def _band_bias_rows(rel_bias):
    left = B_LEFT_CHUNKS * CHUNK
    width = left + 2 * ATTN_TILE
    m = jnp.arange(width, dtype=jnp.int32)
    j = jnp.where(m < left + ATTN_TILE, m, m - width)
    dist = left - j
    rows = rel_bias[:, jnp.clip(dist, -B_MAX_REL, B_MAX_REL) + B_MAX_REL].astype(F32)
    return rows.reshape(rel_bias.shape[0], 1, width)


def _attention(q, k, v, bqk, bv, bias_rows, batch, seq, heads):
    t = batch * seq
    d = B_HEAD_DIM
    width = bias_rows.shape[2]
    head = lambda b, h: (b, h)
    blocks = (2 * _nbytes((seq, A_HEAD_PAD), BF16) + _nbytes((seq, V_PAD), BF16) + _nbytes((seq, A_V), BF16)
              + 4 * _nbytes((seq, d), BF16) + _nbytes((1, width), F32))
    temps = _nbytes((seq, V_PAD), BF16) + 8 * _nbytes((ATTN_TILE, seq), F32) + 8 * _nbytes((ATTN_TILE, width), F32)
    return pl.pallas_call(
        functools.partial(_attn_kernel, seq=seq),
        grid=(batch, heads),
        in_specs=[pl.BlockSpec((seq, A_HEAD_PAD), head), pl.BlockSpec((seq, A_HEAD_PAD), head),
                  pl.BlockSpec((seq, V_PAD), head),
                  pl.BlockSpec((seq, d), head),
                  pl.BlockSpec((seq, d), lambda b, h: (b, heads + h)),
                  pl.BlockSpec((seq, d), head),
                  pl.BlockSpec((None, 1, width), lambda b, h: (h, 0, 0))],
        out_specs=[pl.BlockSpec((seq, A_V), head), pl.BlockSpec((seq, d), head)],
        out_shape=[jax.ShapeDtypeStruct((t, heads * A_V), BF16), jax.ShapeDtypeStruct((t, heads * d), BF16)],
        scratch_shapes=[pltpu.VMEM((seq, V_PAD), BF16)],
        compiler_params=_params(("parallel", "parallel"), blocks, temps),
        name="attention",
    )(q, k, v, bqk, bqk, bv, bias_rows)


def _merge_kernel(oa_ref, ob_ref, woa_ref, wob_ref, g0_ref, g1_ref, o_ref, woa_bf, wob_bf):
    @pl.when(pl.program_id(1) == 0)
    def _():
        woa_bf[...] = woa_ref[...].astype(BF16)
        wob_bf[...] = wob_ref[...].astype(BF16)

    a = jnp.dot(oa_ref[...], woa_bf[...], preferred_element_type=F32)
    b = jnp.dot(ob_ref[...], wob_bf[...], preferred_element_type=F32)
    o_ref[...] = (g0_ref[...].astype(F32) * a + g1_ref[...].astype(F32) * b).astype(o_ref.dtype)


def _merge(oa, ob, woa, wob, gates, tm, tn):
    m = oa.shape[0]
    d = woa.shape[1]
    g1 = d // tn
    blocks = (_nbytes((tm, oa.shape[1]), BF16) + _nbytes((tm, ob.shape[1]), BF16) + _nbytes((woa.shape[0], tn), F32)
              + _nbytes((wob.shape[0], tn), F32) + 3 * _nbytes((tm, tn), BF16))
    resident = (2 * _nbytes((woa.shape[0], tn), BF16) + 2 * _nbytes((wob.shape[0], tn), BF16)
                + 3 * _nbytes((tm, tn), F32))
    return pl.pallas_call(
        _merge_kernel,
        grid=(d // tn, m // tm),
        in_specs=[pl.BlockSpec((tm, oa.shape[1]), lambda j, i: (i, 0)),
                  pl.BlockSpec((tm, ob.shape[1]), lambda j, i: (i, 0)),
                  pl.BlockSpec((woa.shape[0], tn), lambda j, i: (0, j)),
                  pl.BlockSpec((wob.shape[0], tn), lambda j, i: (0, j)),
                  pl.BlockSpec((tm, tn), lambda j, i: (i, j)),
                  pl.BlockSpec((tm, tn), lambda j, i: (i, g1 + j))],
        out_specs=pl.BlockSpec((tm, tn), lambda j, i: (i, j)),
        out_shape=jax.ShapeDtypeStruct((m, d), BF16),
        scratch_shapes=[pltpu.VMEM((woa.shape[0], tn), BF16), pltpu.VMEM((wob.shape[0], tn), BF16)],
        compiler_params=_params(("parallel", "arbitrary"), blocks, resident),
        name="merge",
    )(oa, ob, woa, wob, gates, gates)


def _out_proj_kernel(a_ref, w_ref, x_ref, o_ref, w_bf):
    @pl.when(pl.program_id(1) == 0)
    def _():
        w_bf[...] = w_ref[...].astype(BF16)

    o_ref[...] = x_ref[...] + jnp.dot(a_ref[...], w_bf[...], preferred_element_type=F32)


def _out_proj(merged, wout, x, tm, tn):
    m, k = merged.shape
    n = wout.shape[1]
    blocks = _nbytes((tm, k), BF16) + _nbytes((k, tn), F32) + 2 * _nbytes((tm, tn), F32)
    return pl.pallas_call(
        _out_proj_kernel,
        grid=(n // tn, m // tm),
        in_specs=[pl.BlockSpec((tm, k), lambda j, i: (i, 0)),
                  pl.BlockSpec((k, tn), lambda j, i: (0, j)),
                  pl.BlockSpec((tm, tn), lambda j, i: (i, j))],
        out_specs=pl.BlockSpec((tm, tn), lambda j, i: (i, j)),
        out_shape=jax.ShapeDtypeStruct((m, n), F32),
        scratch_shapes=[pltpu.VMEM((k, tn), BF16)],
        compiler_params=_params(("parallel", "arbitrary"), blocks, 2 * _nbytes((k, tn), BF16) + _nbytes((tm, tn), F32)),
        name="out_proj",
    )(merged, wout, x)


def _split_bf16(v):
    hi = v.astype(BF16)
    return hi, (v - hi.astype(F32)).astype(BF16)


def _router_kernel(x_ref, g_ref, wr_ref, xn_ref, route_ref, *, n_groups, per_group):
    x = x_ref[...]
    r = lax.rsqrt(jnp.mean(x * x, axis=-1, keepdims=True) + EPS)
    xn = x * r * g_ref[...]
    xn_ref[...] = xn
    x_hi, x_lo = _split_bf16(xn)
    w_hi, w_lo = _split_bf16(wr_ref[...])
    dot = functools.partial(jnp.dot, preferred_element_type=F32)
    logits = dot(x_hi, w_hi) + (dot(x_lo, w_hi) + dot(x_hi, w_lo))
    lane = lax.broadcasted_iota(jnp.int32, logits.shape, 1).astype(F32)
    far = float(LANE)

    def top(vals):
        best = jnp.max(vals, axis=-1, keepdims=True)
        return best, jnp.min(jnp.where(vals == best, lane, far), axis=-1, keepdims=True)

    gl = jnp.where(lane < n_groups, logits, NEG_INF)
    gmax, grp = top(gl)
    p_grp = 1.0 / jnp.sum(jnp.exp(gl - gmax), axis=-1, keepdims=True)
    lo = n_groups + grp * per_group
    el = jnp.where((lane >= lo) & (lane < lo + per_group), logits, NEG_INF)
    t1, i1 = top(el)
    t2, i2 = top(jnp.where(lane == i1, NEG_INF, el))
    d = jnp.exp(t2 - t1)
    w1 = p_grp / (1.0 + d)
    w2 = p_grp * d / (1.0 + d)
    route_ref[...] = jnp.where(lane == 0, i1 - n_groups,
                               jnp.where(lane == 1, i2 - n_groups,
                                         jnp.where(lane == 2, w1, jnp.where(lane == 3, w2, 0.0))))


def _router(x1, g, wr, n_groups, per_group, tm):
    t, d = x1.shape
    blocks = 2 * _nbytes((tm, d), F32) + _nbytes((1, d), F32) + _nbytes((d, LANE), F32) + _nbytes((tm, LANE), F32)
    return pl.pallas_call(
        functools.partial(_router_kernel, n_groups=n_groups, per_group=per_group),
        grid=(t // tm,),
        in_specs=[pl.BlockSpec((tm, d), lambda i: (i, 0)), pl.BlockSpec((1, d), lambda i: (0, 0)),
                  pl.BlockSpec((d, LANE), lambda i: (0, 0))],
        out_specs=[pl.BlockSpec((tm, d), lambda i: (i, 0)), pl.BlockSpec((tm, LANE), lambda i: (i, 0))],
        out_shape=[jax.ShapeDtypeStruct((t, d), F32), jax.ShapeDtypeStruct((t, LANE), F32)],
        compiler_params=_params(("parallel",), blocks, 2 * _nbytes((tm, d), F32)),
        name="router",
    )(x1, g.reshape(1, d), wr)


def _one_hots(route):
    lane = lax.broadcasted_iota(jnp.int32, route.shape, 1).astype(F32)
    return (lane == route[:, 0:1]).astype(F32), (lane == route[:, 1:2]).astype(F32)


def _rank_kernel(route_ref, rank_ref, starts_ref, count_acc, start_acc):
    i = pl.program_id(0)

    @pl.when(i == 0)
    def _():
        count_acc[...] = jnp.zeros_like(count_acc)
        start_acc[...] = jnp.zeros_like(start_acc)

    oh1, oh2 = _one_hots(route_ref[...])
    oh = (oh1 + oh2).astype(BF16)
    tm = oh.shape[0]
    earlier = (lax.broadcasted_iota(jnp.int32, (tm, tm), 0) > lax.broadcasted_iota(jnp.int32, (tm, tm), 1))
    before = jnp.dot(earlier.astype(BF16), oh, preferred_element_type=F32) + count_acc[...]
    lane = lax.broadcasted_iota(jnp.int32, (tm, LANE), 1)
    rank_ref[...] = jnp.where(lane == 0, jnp.sum(oh1 * before, axis=-1, keepdims=True),
                              jnp.where(lane == 1, jnp.sum(oh2 * before, axis=-1, keepdims=True), 0.0))
    lower = (lax.broadcasted_iota(jnp.int32, (LANE, LANE), 0) < lax.broadcasted_iota(jnp.int32, (LANE, LANE), 1))
    below = jnp.dot(oh, lower.astype(BF16), preferred_element_type=F32)
    count_acc[...] += jnp.sum(oh.astype(F32), axis=0, keepdims=True)
    start_acc[...] += jnp.sum(below, axis=0, keepdims=True)
    starts_ref[...] = start_acc[...]


def _rank(route, tm):
    t = route.shape[0]
    blocks = 2 * _nbytes((tm, LANE), F32) + _nbytes((1, LANE), F32)
    return pl.pallas_call(
        _rank_kernel,
        grid=(t // tm,),
        in_specs=[pl.BlockSpec((tm, LANE), lambda i: (i, 0))],
        out_specs=[pl.BlockSpec((tm, LANE), lambda i: (i, 0)), pl.BlockSpec((1, LANE), lambda i: (0, 0))],
        out_shape=[jax.ShapeDtypeStruct((t, LANE), F32), jax.ShapeDtypeStruct((1, LANE), F32)],
        scratch_shapes=[pltpu.VMEM((1, LANE), F32), pltpu.VMEM((1, LANE), F32)],
        compiler_params=_params(("arbitrary",), blocks, 2 * _nbytes((tm, tm), F32)),
        name="moe_rank",
    )(route)


def _dest_kernel(route_ref, rank_ref, starts_ref, dest_ref):
    oh1, oh2 = _one_hots(route_ref[...])
    rank = rank_ref[...]
    starts = starts_ref[...]
    d1 = jnp.sum(oh1 * starts, axis=-1, keepdims=True) + rank[:, 0:1]
    d2 = jnp.sum(oh2 * starts, axis=-1, keepdims=True) + rank[:, 1:2]
    lane = lax.broadcasted_iota(jnp.int32, rank.shape, 1)
    dest_ref[...] = jnp.where(lane == 0, d1, jnp.where(lane == 1, d2, 0.0)).astype(jnp.int32)


def _dest(route, rank, starts, tm):
    t = route.shape[0]
    blocks = 3 * _nbytes((tm, LANE), F32) + _nbytes((1, LANE), F32)
    return pl.pallas_call(
        _dest_kernel,
        grid=(t // tm,),
        in_specs=[pl.BlockSpec((tm, LANE), lambda i: (i, 0)), pl.BlockSpec((tm, LANE), lambda i: (i, 0)),
                  pl.BlockSpec((1, LANE), lambda i: (0, 0))],
        out_specs=pl.BlockSpec((tm, LANE), lambda i: (i, 0)),
        out_shape=jax.ShapeDtypeStruct((t, LANE), jnp.int32),
        compiler_params=_params(("parallel",), blocks, 4 * _nbytes((tm, LANE), F32)),
        name="moe_dest",
    )(route, rank, starts)


def _work_items(starts, n_rows):
    n_exp = starts.shape[0]
    n_blk = n_rows // MOE_ROWS
    total = jnp.full((1,), n_rows, jnp.int32)
    pts = jnp.concatenate([jnp.arange(n_blk, dtype=jnp.int32) * MOE_ROWS, starts[1:]])
    idx = jnp.arange(pts.shape[0], dtype=jnp.int32)
    before = (pts[None, :] < pts[:, None]) | ((pts[None, :] == pts[:, None]) & (idx[None, :] < idx[:, None]))
    pos = jnp.sum(before.astype(jnp.int32), axis=1)
    lo = jnp.sum(jnp.where(pos[:, None] == idx[None, :], pts[:, None], 0), axis=0)
    hi = jnp.concatenate([lo[1:], total])
    ends = jnp.concatenate([starts[1:], total])
    r = jnp.minimum(lo // MOE_ROWS, n_blk - 1)
    e = jnp.minimum(jnp.sum((ends[None, :] <= lo[:, None]).astype(jnp.int32), axis=1), n_exp - 1)
    later = jnp.where(e[None, :] > e[:, None], e[None, :], n_exp)
    nxt = jnp.min(later, axis=1)
    nxt = jnp.where(nxt == n_exp, -1, nxt)
    return r, e, lo, hi, nxt


def _stream_expert_weights(w, e_ref, nxt_ref, streams):
    e = e_ref[w]

    def copies(stream, expert):
        hbm, stage, _, sem = stream
        rows = stage.shape[0] // WEIGHT_DMA_PARTS
        parts = [pl.ds(part * rows, rows) for part in range(WEIGHT_DMA_PARTS)]
        return [pltpu.make_async_copy(hbm.at[expert, sl], stage.at[sl], sem) for sl in parts]

    @pl.when(w == 0)
    def _():
        for stream in streams:
            for c in copies(stream, e):
                c.start()

    @pl.when((w == 0) | (e != e_ref[jnp.maximum(w - 1, 0)]))
    def _():
        nxt = nxt_ref[w]
        for stream in streams:
            for c in copies(stream, e):
                c.wait()
            _, stage, w_bf, _ = stream
            rows = stage.shape[0] // CONVERT_PARTS
            for part in range(CONVERT_PARTS):
                sl = slice(part * rows, (part + 1) * rows)
                w_bf[sl, :] = stage[sl, :].astype(BF16)

            @pl.when(nxt >= 0)
            def _():
                for c in copies(stream, nxt):
                    c.start()


def _store_item_rows(o_ref, val, r, lo, hi):
    rows = r * MOE_ROWS + lax.broadcasted_iota(jnp.int32, val.shape, 0)
    mine = (rows >= lo) & (rows < hi)

    @pl.when(lo == r * MOE_ROWS)
    def _():
        o_ref[...] = val

    @pl.when(lo != r * MOE_ROWS)
    def _():
        pltpu.store(o_ref, val, mask=mine)


def _gather_block_rows(w, r_ref, dest_ref, x_hbm, source, rows_f32, x_bf, sem, *, n_assign):
    r = r_ref[w]
    n_blocks = n_assign // MOE_ROWS

    def copy(src_row, slot, j):
        return pltpu.make_async_copy(x_hbm.at[pl.ds(src_row, 1)], rows_f32.at[slot, pl.ds(j, 1)], sem.at[slot])

    def start_block(block, slot):
        for j in range(MOE_ROWS):
            copy(source[block * MOE_ROWS + j], slot, j).start()

    def finish_block(slot):
        for j in range(MOE_ROWS):
            copy(0, slot, j).wait()
        x_bf[...] = rows_f32[slot].astype(BF16)

    @pl.when(w == 0)
    def _():
        def invert(tok, carry):
            for k in range(TOP_K):
                source[dest_ref[TOP_K * tok + k]] = tok
            return carry
        lax.fori_loop(0, n_assign // TOP_K, invert, 0, unroll=4)
        start_block(r, 0)

    @pl.when((w == 0) | (r != r_ref[jnp.maximum(w - 1, 0)]))
    def _():
        for slot in range(2):
            @pl.when(r % 2 == slot)
            def _():
                finish_block(slot)

                @pl.when(r + 1 < n_blocks)
                def _():
                    start_block(r + 1, 1 - slot)


def _moe_experts_kernel(r_ref, e_ref, lo_ref, hi_ref, nxt_ref, dest_ref, x_hbm, wg_hbm, wu_hbm, wd_hbm, y_ref,
                        wg_stage, wu_stage, wd_stage, wg_bf, wu_bf, wd_bf, source, rows_f32, x_bf, sem, row_sem,
                        *, n_assign):
    w = pl.program_id(0)
    _stream_expert_weights(w, e_ref, nxt_ref, [(wg_hbm, wg_stage, wg_bf, sem.at[0]),
                                                 (wu_hbm, wu_stage, wu_bf, sem.at[1]),
                                                 (wd_hbm, wd_stage, wd_bf, sem.at[2])])
    _gather_block_rows(w, r_ref, dest_ref, x_hbm, source, rows_f32, x_bf, row_sem, n_assign=n_assign)
    r, lo, hi = r_ref[w], lo_ref[w], hi_ref[w]

    @pl.when(hi > lo)
    def _():
        x = x_bf[...]
        g = jnp.dot(x, wg_bf[...], preferred_element_type=F32)
        u = jnp.dot(x, wu_bf[...], preferred_element_type=F32)
        h = ((g * (1.0 / (1.0 + jnp.exp(-g)))) * u).astype(BF16)
        y = jnp.dot(h, wd_bf[...], preferred_element_type=F32)
        _store_item_rows(y_ref, y, r, lo, hi)


def _moe_experts(items, dest_flat, xn, wg, wu, wd):
    n_assign = dest_flat.shape[0]
    d = xn.shape[1]
    f = wg.shape[2]
    n_items = items[0].shape[0]
    blocks = _nbytes((MOE_ROWS, d), F32)
    resident = (3 * _nbytes((d, f), F32) + 3 * _nbytes((d, f), BF16) + 5 * _nbytes((MOE_ROWS, d), F32)
                + _nbytes((MOE_ROWS, d), BF16) + 6 * _nbytes((MOE_ROWS, f), F32))
    any_space = pl.BlockSpec(memory_space=pl.ANY)
    return pl.pallas_call(
        functools.partial(_moe_experts_kernel, n_assign=n_assign),
        grid_spec=pltpu.PrefetchScalarGridSpec(
            num_scalar_prefetch=6,
            grid=(n_items,),
            in_specs=[any_space, any_space, any_space, any_space],
            out_specs=pl.BlockSpec((MOE_ROWS, d), lambda w, r, e, lo, hi, nxt, dest: (r[w], 0)),
            scratch_shapes=[pltpu.VMEM((d, f), F32), pltpu.VMEM((d, f), F32), pltpu.VMEM((f, d), F32),
                            pltpu.VMEM((d, f), BF16), pltpu.VMEM((d, f), BF16), pltpu.VMEM((f, d), BF16),
                            pltpu.SMEM((n_assign,), jnp.int32), pltpu.VMEM((2, MOE_ROWS, d), F32),
                            pltpu.VMEM((MOE_ROWS, d), BF16),
                            pltpu.SemaphoreType.DMA((3,)), pltpu.SemaphoreType.DMA((2,))]),
        out_shape=jax.ShapeDtypeStruct((n_assign, d), F32),
        compiler_params=_params(("arbitrary",), blocks, resident),
        name="moe_experts",
    )(*items, dest_flat, xn, wg, wu, wd)


def _combine_kernel(dest_ref, x_ref, route_ref, y_hbm, o_ref, ybuf, sem, *, tm):
    half = tm // 2
    i = pl.program_id(0)

    def copy(src_row, slot, k, t):
        return pltpu.make_async_copy(y_hbm.at[pl.ds(src_row, 1)], ybuf.at[slot, k, pl.ds(t, 1)], sem.at[slot])

    def start_half(index, slot):
        for t in range(half):
            for k in range(TOP_K):
                copy(dest_ref[TOP_K * (index * half + t) + k], slot, k, t).start()

    def finish_half(slot):
        for t in range(half):
            for k in range(TOP_K):
                copy(0, slot, k, t).wait()
        rows = slice(slot * half, (slot + 1) * half)
        route = route_ref[rows, :]
        o_ref[rows, :] = x_ref[rows, :] + (route[:, 2:3] * ybuf[slot, 0] + route[:, 3:4] * ybuf[slot, 1])

    @pl.when(i == 0)
    def _():
        start_half(2 * i, 0)

    start_half(2 * i + 1, 1)
    finish_half(0)

    @pl.when(i + 1 < pl.num_programs(0))
    def _():
        start_half(2 * i + 2, 0)

    finish_half(1)


def _combine(dest_flat, x1, route, y, tm):
    t, d = x1.shape
    blocks = 2 * _nbytes((tm, d), F32) + _nbytes((tm, LANE), F32)
    resident = TOP_K * _nbytes((tm, d), F32) + _nbytes((tm, d), F32)
    return pl.pallas_call(
        functools.partial(_combine_kernel, tm=tm),
        grid_spec=pltpu.PrefetchScalarGridSpec(
            num_scalar_prefetch=1,
            grid=(t // tm,),
            in_specs=[pl.BlockSpec((tm, d), lambda i, dest: (i, 0)),
                      pl.BlockSpec((tm, LANE), lambda i, dest: (i, 0)),
                      pl.BlockSpec(memory_space=pl.ANY)],
            out_specs=pl.BlockSpec((tm, d), lambda i, dest: (i, 0)),
            scratch_shapes=[pltpu.VMEM((2, TOP_K, tm // 2, d), F32), pltpu.SemaphoreType.DMA((2,))]),
        out_shape=jax.ShapeDtypeStruct((t, d), F32),
        compiler_params=_params(("arbitrary",), blocks, resident),
        name="moe_combine",
    )(dest_flat, x1, route, y)


def _pad_cols(w, n):
    return jnp.pad(w, ((0, 0), (0, n - w.shape[1])))


def kernel(x, positions, g_mix, w_in, b_gate, q_norm_g, kv_norm_g, w_uq, w_ukv, a_q_norm_g, a_k_norm_g,
           b_q_norm_g, b_k_norm_g, rel_bias, w_o_a, w_o_b, w_out, g_ffn, w_group, w_expert,
           w_exp_gate, w_exp_up, w_exp_down):
    batch, seq, d = x.shape
    t = batch * seq
    q_lora, kv_lora = q_norm_g.shape[0], kv_norm_g.shape[0]
    a_heads = w_uq.shape[1] // A_QK
    b_heads = w_o_b.shape[0] // B_HEAD_DIM
    b_width = b_heads * B_HEAD_DIM
    n_groups, n_experts = w_group.shape[1], w_expert.shape[1]
    per_group = n_experts // n_groups
    off_b = q_lora + kv_lora + A_ROPE
    assert seq % ATTN_TILE == 0 and (TOP_K * t) % MOE_ROWS == 0 and n_groups + n_experts <= LANE
    assert a_heads == b_heads

    xf = x.reshape(t, d)
    tm_big = min(1024, t)
    tn = _tile(b_width, 512)
    tn_wide = _tile(b_width, 1024)
    assert d % tn == 0 and d % tn_wide == 0

    za_cols = -(-(q_lora + kv_lora + LANE) // tn) * tn
    wuq = jnp.pad(w_uq.reshape(q_lora, a_heads, A_QK), ((0, 0), (0, 0), (0, A_HEAD_PAD - A_QK)))
    wuq = wuq.reshape(q_lora, a_heads * A_HEAD_PAD).astype(BF16)
    wukv = w_ukv.astype(BF16)
    pad_gain = lambda g, s: jnp.pad(g * s, (0, A_HEAD_PAD - A_QK)).reshape(1, A_HEAD_PAD)
    gaq = pad_gain(a_q_norm_g, A_QK ** -0.5)
    gak = pad_gain(a_k_norm_g, 1.0)
    g_bqk = jnp.concatenate([jnp.tile(b_q_norm_g * B_HEAD_DIM ** -0.5, b_heads), jnp.tile(b_k_norm_g, b_heads)])

    half = A_ROPE // 2
    inv = ROPE_THETA ** (-jnp.arange(half, dtype=F32) / half)
    ang = positions.astype(F32).reshape(t, 1) * inv
    cos, sin = jnp.cos(ang), jnp.sin(ang)
    zeros = jnp.zeros((t, half), F32)
    cos_t = jnp.concatenate([cos, cos, zeros, zeros], axis=1)
    sin_lo = jnp.concatenate([-sin, zeros, zeros, zeros], axis=1)
    sin_hi = jnp.concatenate([zeros, sin, zeros, zeros], axis=1)

    xn = _rmsnorm_rows(xf, g_mix, BF16, min(256, t))
    w_in_t = w_in.T
    za = _inproj_a(xn, w_in_t, za_cols, tm_big, tn)
    bqk = _inproj_cols(xn, w_in_t, off_b, 2 * b_width, g_bqk, _head_norm_epilogue, tm_big, tn_wide)
    bv = _inproj_cols(xn, w_in_t, off_b + 2 * b_width, b_width, None, _plain_epilogue, tm_big, tn_wide)
    gates = _inproj_cols(xn, w_in_t, off_b + 3 * b_width, 2 * d, b_gate, _sigmoid_epilogue, tm_big, tn_wide)
    q, k, v = _mla_proj(za, cos_t, sin_lo, sin_hi, wuq, wukv, q_norm_g.reshape(1, -1), kv_norm_g.reshape(1, -1),
                        gaq, gak, a_heads, min(256, t))
    o_a, o_b = _attention(q, k, v, bqk, bv, _band_bias_rows(rel_bias), batch, seq, a_heads)
    merged = _merge(o_a, o_b, w_o_a, w_o_b, gates, tm_big, tn)
    x1 = _out_proj(merged, w_out, xf, tm_big, tn)

    wr = _pad_cols(jnp.concatenate([w_group, w_expert], axis=1), LANE)
    xn2, route = _router(x1, g_ffn, wr, n_groups, per_group, min(256, t))
    rank, starts_f = _rank(route, min(512, t))
    dest = _dest(route, rank, starts_f, min(512, t))[:, :TOP_K].reshape(-1)
    items = _work_items(starts_f[0, :n_experts].astype(jnp.int32), TOP_K * t)
    y = _moe_experts(items, dest, xn2, w_exp_gate, w_exp_up, w_exp_down)
    out = _combine(dest, x1, route, y, min(256, t))
    return out.reshape(batch, seq, d)
```

```python
import functools

import jax
import jax.numpy as jnp
from jax import lax
from jax.experimental import pallas as pl
from jax.experimental.pallas import tpu as pltpu

F32 = jnp.float32
BF16 = jnp.bfloat16

CHUNK = 64
EPS = 1e-6
A_NOPE = 128
A_ROPE = 64
A_V = 128
A_QK = A_NOPE + A_ROPE
B_HEAD_DIM = 128
B_LEFT_CHUNKS = 8
B_MAX_REL = 128
ROPE_THETA = 10000.0
TOP_K = 2

LANE = 128
A_HEAD_PAD = 2 * LANE
V_PAD = 2 * LANE
V7X_VMEM_BYTES = 64 * 2**20

ATTN_TILE = 256
MOE_ROWS = 128
WEIGHT_DMA_PARTS = 4
CONVERT_PARTS = 8
PROJ_ROW_GROUPS = 4
NEG_INF = float("-inf")


def _nbytes(shape, dtype):
    n = 1
    for s in shape:
        n *= s
    return n * jnp.dtype(dtype).itemsize


def _params(semantics, pipelined_bytes, resident_bytes=0):
    need = 2 * pipelined_bytes + resident_bytes
    return pltpu.CompilerParams(dimension_semantics=semantics,
                                vmem_limit_bytes=min(int(need), V7X_VMEM_BYTES))


def _tile(n, want):
    t = want
    while t > LANE and n % t:
        t //= 2
    assert n % t == 0, (n, want)
    return t


def _rmsnorm_kernel(x_ref, g_ref, o_ref):
    x = x_ref[...]
    r = lax.rsqrt(jnp.mean(x * x, axis=-1, keepdims=True) + EPS)
    o_ref[...] = (x * r * g_ref[...]).astype(o_ref.dtype)


def _rmsnorm_rows(x, g, out_dtype, tm):
    t, d = x.shape
    blocks = _nbytes((tm, d), F32) + _nbytes((tm, d), out_dtype) + _nbytes((1, d), F32)
    return pl.pallas_call(
        _rmsnorm_kernel,
        grid=(t // tm,),
        in_specs=[pl.BlockSpec((tm, d), lambda i: (i, 0)), pl.BlockSpec((1, d), lambda i: (0, 0))],
        out_specs=pl.BlockSpec((tm, d), lambda i: (i, 0)),
        out_shape=jax.ShapeDtypeStruct((t, d), out_dtype),
        compiler_params=_params(("parallel",), blocks, _nbytes((tm, d), F32)),
        name="rmsnorm",
    )(x, g.reshape(1, d))


def _nt_dot(a, b):
    return lax.dot_general(a, b, (((1,), (1,)), ((), ())), preferred_element_type=F32)


def _head_norm_epilogue(acc, gb_ref):
    heads = []
    for h in range(acc.shape[1] // B_HEAD_DIM):
        sl = slice(h * B_HEAD_DIM, (h + 1) * B_HEAD_DIM)
        z = acc[:, sl]
        r = lax.rsqrt(jnp.mean(z * z, axis=-1, keepdims=True) + EPS)
        heads.append(z * r * gb_ref[:, sl])
    return jnp.concatenate(heads, axis=1)


def _plain_epilogue(acc, gb_ref):
    del gb_ref
    return acc


def _sigmoid_epilogue(acc, gb_ref):
    return 1.0 / (1.0 + jnp.exp(-(acc + gb_ref[...])))


def _inproj_cols_kernel(a_ref, wt_hbm, *refs, first_col, epilogue):
    gb_ref = refs[0] if len(refs) == 5 else None
    o_ref, stage, wt_bf, sem = refs[-4:]
    j = pl.program_id(0)
    tn = wt_bf.shape[0]

    def fetch(block):
        rows = pl.ds(pl.multiple_of(first_col + block * tn, 8), tn)
        return pltpu.make_async_copy(wt_hbm.at[rows], stage, sem)

    @pl.when(pl.program_id(1) == 0)
    def _():
        @pl.when(j == 0)
        def _():
            fetch(j).start()

        fetch(j).wait()
        wt_bf[...] = stage[...].astype(BF16)

        @pl.when(j + 1 < pl.num_programs(0))
        def _():
            fetch(j + 1).start()

    rows = a_ref.shape[0] // PROJ_ROW_GROUPS
    groups = [slice(s * rows, (s + 1) * rows) for s in range(PROJ_ROW_GROUPS)]
    accs = [_nt_dot(a_ref[sl, :], wt_bf[...]) for sl in groups]
    for sl, acc in zip(groups, accs):
        o_ref[sl, :] = epilogue(acc, gb_ref).astype(o_ref.dtype)


def _inproj_cols(xn, w_in_t, first_col, n_cols, gb, epilogue, tm, tn):
    m, k = xn.shape
    assert n_cols % tn == 0 and first_col % 8 == 0 and tm % PROJ_ROW_GROUPS == 0
    blocks = _nbytes((tm, k), BF16) + _nbytes((tm, tn), BF16) + _nbytes((1, tn), F32)
    resident = _nbytes((tn, k), F32) + _nbytes((tn, k), BF16) + 3 * _nbytes((tm, tn), F32)
    gb_specs = [] if gb is None else [pl.BlockSpec((1, tn), lambda j, i: (0, j))]
    gb_args = [] if gb is None else [gb.reshape(1, n_cols)]
    return pl.pallas_call(
        functools.partial(_inproj_cols_kernel, first_col=first_col, epilogue=epilogue),
        grid=(n_cols // tn, m // tm),
        in_specs=[pl.BlockSpec((tm, k), lambda j, i: (i, 0)), pl.BlockSpec(memory_space=pl.ANY)] + gb_specs,
        out_specs=pl.BlockSpec((tm, tn), lambda j, i: (i, j)),
        out_shape=jax.ShapeDtypeStruct((m, n_cols), BF16),
        scratch_shapes=[pltpu.VMEM((tn, k), F32), pltpu.VMEM((tn, k), BF16), pltpu.SemaphoreType.DMA(())],
        compiler_params=_params(("arbitrary", "arbitrary"), blocks, resident),
        name="inproj_" + epilogue.__name__.strip("_").replace("_epilogue", ""),
    )(xn, w_in_t, *gb_args)


def _rope_padded(v, cos, sin_lo, sin_hi):
    half = A_ROPE // 2
    return v * cos + pltpu.roll(v, LANE - half, 1) * sin_lo + pltpu.roll(v, half, 1) * sin_hi


def _ones_column(rows):
    lane = lax.broadcasted_iota(jnp.int32, (rows, V_PAD - A_V), 1)
    return jnp.where(lane == 0, 1.0, 0.0).astype(BF16)


def _mla_proj_kernel(za_ref, cos_ref, sl_ref, sh_ref, wuq_ref, wukv_ref, gq_ref, gkv_ref, gaq_ref, gak_ref,
                     q_ref, k_ref, v_ref, *, heads, q_lora, kv_lora):
    cos, sin_lo, sin_hi = cos_ref[...], sl_ref[...], sh_ref[...]

    def norm(z, g):
        z = z.astype(F32)
        r = lax.rsqrt(jnp.mean(z * z, axis=-1, keepdims=True) + EPS)
        return (z * r * g).astype(BF16)

    cq = norm(za_ref[:, :q_lora], gq_ref[...])
    ckv = norm(za_ref[:, q_lora:q_lora + kv_lora], gkv_ref[...])
    slab = za_ref[:, q_lora + kv_lora:q_lora + kv_lora + LANE].astype(F32)
    k_rope = jnp.where(lax.broadcasted_iota(jnp.int32, slab.shape, 1) < A_ROPE, slab, 0.0)
    qacc = jnp.dot(cq, wuq_ref[...], preferred_element_type=F32)
    kvacc = jnp.dot(ckv, wukv_ref[...], preferred_element_type=F32)

    gq_lo, gq_hi = gaq_ref[:, :LANE], gaq_ref[:, LANE:]
    gk_lo, gk_hi = gak_ref[:, :LANE], gak_ref[:, LANE:]
    kr_ss = jnp.sum(k_rope * k_rope, axis=-1, keepdims=True)
    kr_rot = _rope_padded(k_rope * gk_hi, cos, sin_lo, sin_hi)
    ones_col = _ones_column(slab.shape[0])
    for h in range(heads):
        base = h * A_HEAD_PAD
        q_lo = qacc[:, base:base + LANE]
        q_hi = qacc[:, base + LANE:base + A_HEAD_PAD]
        ss = jnp.sum(q_lo * q_lo, axis=-1, keepdims=True) + jnp.sum(q_hi * q_hi, axis=-1, keepdims=True)
        r = lax.rsqrt(ss / A_QK + EPS)
        q_ref[:, base:base + LANE] = (q_lo * r * gq_lo).astype(BF16)
        q_ref[:, base + LANE:base + A_HEAD_PAD] = _rope_padded(q_hi * r * gq_hi, cos, sin_lo, sin_hi).astype(BF16)

        k_lo = kvacc[:, base:base + LANE]
        ssk = jnp.sum(k_lo * k_lo, axis=-1, keepdims=True) + kr_ss
        rk = lax.rsqrt(ssk / A_QK + EPS)
        k_ref[:, base:base + LANE] = (k_lo * rk * gk_lo).astype(BF16)
        k_ref[:, base + LANE:base + A_HEAD_PAD] = (kr_rot * rk).astype(BF16)
        v_ref[:, h * V_PAD:h * V_PAD + A_V] = kvacc[:, base + LANE:base + A_HEAD_PAD].astype(BF16)
        v_ref[:, h * V_PAD + A_V:(h + 1) * V_PAD] = ones_col


def _mla_proj(za, cos, sin_lo, sin_hi, wuq, wukv, gq, gkv, gaq, gak, heads, tm):
    t, za_cols = za.shape
    q_lora, kv_lora = wuq.shape[0], wukv.shape[0]
    hp = heads * A_HEAD_PAD
    row = lambda i: (i, 0)
    fix = lambda i: (0, 0)
    blocks = (_nbytes((tm, za_cols), za.dtype) + 3 * _nbytes((tm, LANE), F32) + _nbytes(wuq.shape, BF16)
              + _nbytes(wukv.shape, BF16) + 2 * _nbytes((tm, hp), BF16) + _nbytes((tm, heads * V_PAD), BF16))
    return pl.pallas_call(
        functools.partial(_mla_proj_kernel, heads=heads, q_lora=q_lora, kv_lora=kv_lora),
        grid=(t // tm,),
        in_specs=[pl.BlockSpec((tm, za_cols), row),
                  pl.BlockSpec((tm, LANE), row), pl.BlockSpec((tm, LANE), row), pl.BlockSpec((tm, LANE), row),
                  pl.BlockSpec(wuq.shape, fix), pl.BlockSpec(wukv.shape, fix),
                  pl.BlockSpec((1, q_lora), fix), pl.BlockSpec((1, kv_lora), fix),
                  pl.BlockSpec((1, A_HEAD_PAD), fix), pl.BlockSpec((1, A_HEAD_PAD), fix)],
        out_specs=[pl.BlockSpec((tm, hp), row), pl.BlockSpec((tm, hp), row), pl.BlockSpec((tm, heads * V_PAD), row)],
        out_shape=[jax.ShapeDtypeStruct((t, hp), BF16), jax.ShapeDtypeStruct((t, hp), BF16),
                   jax.ShapeDtypeStruct((t, heads * V_PAD), BF16)],
        compiler_params=_params(("parallel",), blocks, 3 * _nbytes((tm, hp), F32)),
        name="mla_proj",
    )(za, cos, sin_lo, sin_hi, wuq, wukv, gq, gkv, gaq, gak)


def _mla_scores(i, q_ref, k_ref, diag_ok):
    tq = ATTN_TILE
    q = q_ref[i * tq:(i + 1) * tq, :]
    sd = jnp.where(diag_ok, _nt_dot(q, k_ref[i * tq:(i + 1) * tq, :]), NEG_INF)
    s0 = _nt_dot(q, k_ref[:i * tq, :]) if i > 0 else None
    return sd, s0


def _mla_softmax(sd, s0):
    m = jnp.max(sd, axis=-1, keepdims=True)
    if s0 is None:
        return jnp.exp(sd - m).astype(BF16), None
    m = jnp.maximum(m, jnp.max(s0, axis=-1, keepdims=True))
    return jnp.exp(sd - m).astype(BF16), jnp.exp(s0 - m).astype(BF16)


def _mla_values(i, pd, p0, v_ref, o_ref):
    tq = ATTN_TILE
    o = jnp.dot(pd, v_ref[i * tq:(i + 1) * tq, :], preferred_element_type=F32)
    if p0 is not None:
        o = o + jnp.dot(p0, v_ref[:i * tq, :], preferred_element_type=F32)
    o_ref[i * tq:(i + 1) * tq, :] = (o[:, :A_V] / o[:, A_V:A_V + 1]).astype(o_ref.dtype)


def _band_window(i):
    left = B_LEFT_CHUNKS * CHUNK
    q0 = i * ATTN_TILE
    k0 = max(0, q0 - left)
    return q0, k0, q0 + ATTN_TILE - k0, left - q0 + k0


def _band_scores(i, q_ref, k_ref, table):
    q0, k0, kw, u0 = _band_window(i)
    return _nt_dot(q_ref[q0:q0 + ATTN_TILE, :], k_ref[k0:k0 + kw, :]) + table[:, u0:u0 + kw]


def _band_softmax(s):
    return jnp.exp(s - jnp.max(s, axis=-1, keepdims=True)).astype(BF16)


def _band_values(i, p, v_ref, o_ref):
    q0, k0, kw, _ = _band_window(i)
    o = jnp.dot(p, v_ref[k0:k0 + kw, :], preferred_element_type=F32)
    o_ref[q0:q0 + ATTN_TILE, :] = (o[:, :B_HEAD_DIM] / o[:, B_HEAD_DIM:B_HEAD_DIM + 1]).astype(o_ref.dtype)


def _attn_kernel(aq_ref, ak_ref, av_ref, bq_ref, bk_ref, bv_ref, r_ref, oa_ref, ob_ref, bv_pad, *, seq):
    tq = ATTN_TILE
    left = B_LEFT_CHUNKS * CHUNK
    width = r_ref.shape[1]
    rc = lax.broadcasted_iota(jnp.int32, (tq, tq), 0) // CHUNK
    cc = lax.broadcasted_iota(jnp.int32, (tq, tq), 1) // CHUNK
    diag_ok = cc <= rc
    bias = pltpu.roll(jnp.broadcast_to(r_ref[...], (tq, width)), 0, 1, stride=1, stride_axis=0)[:, :left + tq]
    q_chunk = lax.broadcasted_iota(jnp.int32, (tq, left + tq), 0) // CHUNK
    k_chunk = lax.broadcasted_iota(jnp.int32, (tq, left + tq), 1) // CHUNK
    table = jnp.where((k_chunk >= q_chunk) & (k_chunk <= q_chunk + B_LEFT_CHUNKS), bias, NEG_INF)
    bv_pad[:, :B_HEAD_DIM] = bv_ref[...]
    bv_pad[:, B_HEAD_DIM:] = _ones_column(seq)

    n_tiles = seq // tq
    sd, s0 = _mla_scores(0, aq_ref, ak_ref, diag_ok)
    for i in range(n_tiles):
        sb = _band_scores(i, bq_ref, bk_ref, table)
        pd, p0 = _mla_softmax(sd, s0)
        if i + 1 < n_tiles:
            sd, s0 = _mla_scores(i + 1, aq_ref, ak_ref, diag_ok)
        _mla_values(i, pd, p0, av_ref, oa_ref)
        _band_values(i, _band_softmax(sb), bv_pad, ob_ref)


def _band_bias_rows(rel_bias):
    left = B_LEFT_CHUNKS * CHUNK
    width = left + 2 * ATTN_TILE
    m = jnp.arange(width, dtype=jnp.int32)
    j = jnp.where(m < left + ATTN_TILE, m, m - width)
    dist = left - j
    rows = rel_bias[:, jnp.clip(dist, -B_MAX_REL, B_MAX_REL) + B_MAX_REL].astype(F32)
    return rows.reshape(rel_bias.shape[0], 1, width)


def _attention(q, k, v, bqk, bv, bias_rows, batch, seq, heads):
    t = batch * seq
    d = B_HEAD_DIM
    width = bias_rows.shape[2]
    head = lambda b, h: (b, h)
    blocks = (2 * _nbytes((seq, A_HEAD_PAD), BF16) + _nbytes((seq, V_PAD), BF16) + _nbytes((seq, A_V), BF16)
              + 4 * _nbytes((seq, d), BF16) + _nbytes((1, width), F32))
    temps = _nbytes((seq, V_PAD), BF16) + 8 * _nbytes((ATTN_TILE, seq), F32) + 8 * _nbytes((ATTN_TILE, width), F32)
    return pl.pallas_call(
        functools.partial(_attn_kernel, seq=seq),
        grid=(batch, heads),
        in_specs=[pl.BlockSpec((seq, A_HEAD_PAD), head), pl.BlockSpec((seq, A_HEAD_PAD), head),
                  pl.BlockSpec((seq, V_PAD), head),
                  pl.BlockSpec((seq, d), head),
                  pl.BlockSpec((seq, d), lambda b, h: (b, heads + h)),
                  pl.BlockSpec((seq, d), head),
                  pl.BlockSpec((None, 1, width), lambda b, h: (h, 0, 0))],
        out_specs=[pl.BlockSpec((seq, A_V), head), pl.BlockSpec((seq, d), head)],
        out_shape=[jax.ShapeDtypeStruct((t, heads * A_V), BF16), jax.ShapeDtypeStruct((t, heads * d), BF16)],
        scratch_shapes=[pltpu.VMEM((seq, V_PAD), BF16)],
        compiler_params=_params(("parallel", "parallel"), blocks, temps),
        name="attention",
    )(q, k, v, bqk, bqk, bv, bias_rows)


def _merge_kernel(oa_ref, ob_ref, woa_ref, wob_ref, g0_ref, g1_ref, o_ref, woa_bf, wob_bf):
    @pl.when(pl.program_id(1) == 0)
    def _():
        woa_bf[...] = woa_ref[...].astype(BF16)
        wob_bf[...] = wob_ref[...].astype(BF16)

    a = jnp.dot(oa_ref[...], woa_bf[...], preferred_element_type=F32)
    b = jnp.dot(ob_ref[...], wob_bf[...], preferred_element_type=F32)
    o_ref[...] = (g0_ref[...].astype(F32) * a + g1_ref[...].astype(F32) * b).astype(o_ref.dtype)


def _merge(oa, ob, woa, wob, gates, tm, tn):
    m = oa.shape[0]
    d = woa.shape[1]
    g1 = d // tn
    blocks = (_nbytes((tm, oa.shape[1]), BF16) + _nbytes((tm, ob.shape[1]), BF16) + _nbytes((woa.shape[0], tn), F32)
              + _nbytes((wob.shape[0], tn), F32) + 3 * _nbytes((tm, tn), BF16))
    resident = (2 * _nbytes((woa.shape[0], tn), BF16) + 2 * _nbytes((wob.shape[0], tn), BF16)
                + 3 * _nbytes((tm, tn), F32))
    return pl.pallas_call(
        _merge_kernel,
        grid=(d // tn, m // tm),
        in_specs=[pl.BlockSpec((tm, oa.shape[1]), lambda j, i: (i, 0)),
                  pl.BlockSpec((tm, ob.shape[1]), lambda j, i: (i, 0)),
                  pl.BlockSpec((woa.shape[0], tn), lambda j, i: (0, j)),
                  pl.BlockSpec((wob.shape[0], tn), lambda j, i: (0, j)),
                  pl.BlockSpec((tm, tn), lambda j, i: (i, j)),
                  pl.BlockSpec((tm, tn), lambda j, i: (i, g1 + j))],
        out_specs=pl.BlockSpec((tm, tn), lambda j, i: (i, j)),
        out_shape=jax.ShapeDtypeStruct((m, d), BF16),
        scratch_shapes=[pltpu.VMEM((woa.shape[0], tn), BF16), pltpu.VMEM((wob.shape[0], tn), BF16)],
        compiler_params=_params(("parallel", "arbitrary"), blocks, resident),
        name="merge",
    )(oa, ob, woa, wob, gates, gates)


def _out_proj_kernel(a_ref, w_ref, x_ref, o_ref, w_bf):
    @pl.when(pl.program_id(1) == 0)
    def _():
        w_bf[...] = w_ref[...].astype(BF16)

    o_ref[...] = x_ref[...] + jnp.dot(a_ref[...], w_bf[...], preferred_element_type=F32)


def _out_proj(merged, wout, x, tm, tn):
    m, k = merged.shape
    n = wout.shape[1]
    blocks = _nbytes((tm, k), BF16) + _nbytes((k, tn), F32) + 2 * _nbytes((tm, tn), F32)
    return pl.pallas_call(
        _out_proj_kernel,
        grid=(n // tn, m // tm),
        in_specs=[pl.BlockSpec((tm, k), lambda j, i: (i, 0)),
                  pl.BlockSpec((k, tn), lambda j, i: (0, j)),
                  pl.BlockSpec((tm, tn), lambda j, i: (i, j))],
        out_specs=pl.BlockSpec((tm, tn), lambda j, i: (i, j)),
        out_shape=jax.ShapeDtypeStruct((m, n), F32),
        scratch_shapes=[pltpu.VMEM((k, tn), BF16)],
        compiler_params=_params(("parallel", "arbitrary"), blocks, 2 * _nbytes((k, tn), BF16) + _nbytes((tm, tn), F32)),
        name="out_proj",
    )(merged, wout, x)


def _split_bf16(v):
    hi = v.astype(BF16)
    return hi, (v - hi.astype(F32)).astype(BF16)


def _router_kernel(x_ref, g_ref, wr_ref, xn_ref, route_ref, *, n_groups, per_group):
    x = x_ref[...]
    r = lax.rsqrt(jnp.mean(x * x, axis=-1, keepdims=True) + EPS)
    xn = x * r * g_ref[...]
    xn_ref[...] = xn
    x_hi, x_lo = _split_bf16(xn)
    w_hi, w_lo = _split_bf16(wr_ref[...])
    dot = functools.partial(jnp.dot, preferred_element_type=F32)
    logits = dot(x_hi, w_hi) + (dot(x_lo, w_hi) + dot(x_hi, w_lo))
    lane = lax.broadcasted_iota(jnp.int32, logits.shape, 1).astype(F32)
    far = float(LANE)

    def top(vals):
        best = jnp.max(vals, axis=-1, keepdims=True)
        return best, jnp.min(jnp.where(vals == best, lane, far), axis=-1, keepdims=True)

    gl = jnp.where(lane < n_groups, logits, NEG_INF)
    gmax, grp = top(gl)
    p_grp = 1.0 / jnp.sum(jnp.exp(gl - gmax), axis=-1, keepdims=True)
    lo = n_groups + grp * per_group
    el = jnp.where((lane >= lo) & (lane < lo + per_group), logits, NEG_INF)
    t1, i1 = top(el)
    t2, i2 = top(jnp.where(lane == i1, NEG_INF, el))
    d = jnp.exp(t2 - t1)
    w1 = p_grp / (1.0 + d)
    w2 = p_grp * d / (1.0 + d)
    route_ref[...] = jnp.where(lane == 0, i1 - n_groups,
                               jnp.where(lane == 1, i2 - n_groups,
                                         jnp.where(lane == 2, w1, jnp.where(lane == 3, w2, 0.0))))


def _router(x1, g, wr, n_groups, per_group, tm):
    t, d = x1.shape
    blocks = 2 * _nbytes((tm, d), F32) + _nbytes((1, d), F32) + _nbytes((d, LANE), F32) + _nbytes((tm, LANE), F32)
    return pl.pallas_call(
        functools.partial(_router_kernel, n_groups=n_groups, per_group=per_group),
        grid=(t // tm,),
        in_specs=[pl.BlockSpec((tm, d), lambda i: (i, 0)), pl.BlockSpec((1, d), lambda i: (0, 0)),
                  pl.BlockSpec((d, LANE), lambda i: (0, 0))],
        out_specs=[pl.BlockSpec((tm, d), lambda i: (i, 0)), pl.BlockSpec((tm, LANE), lambda i: (i, 0))],
        out_shape=[jax.ShapeDtypeStruct((t, d), F32), jax.ShapeDtypeStruct((t, LANE), F32)],
        compiler_params=_params(("parallel",), blocks, 2 * _nbytes((tm, d), F32)),
        name="router",
    )(x1, g.reshape(1, d), wr)


def _one_hots(route):
    lane = lax.broadcasted_iota(jnp.int32, route.shape, 1).astype(F32)
    return (lane == route[:, 0:1]).astype(F32), (lane == route[:, 1:2]).astype(F32)


def _rank_kernel(route_ref, rank_ref, starts_ref, count_acc, start_acc):
    i = pl.program_id(0)

    @pl.when(i == 0)
    def _():
        count_acc[...] = jnp.zeros_like(count_acc)
        start_acc[...] = jnp.zeros_like(start_acc)

    oh1, oh2 = _one_hots(route_ref[...])
    oh = (oh1 + oh2).astype(BF16)
    tm = oh.shape[0]
    earlier = (lax.broadcasted_iota(jnp.int32, (tm, tm), 0) > lax.broadcasted_iota(jnp.int32, (tm, tm), 1))
    before = jnp.dot(earlier.astype(BF16), oh, preferred_element_type=F32) + count_acc[...]
    lane = lax.broadcasted_iota(jnp.int32, (tm, LANE), 1)
    rank_ref[...] = jnp.where(lane == 0, jnp.sum(oh1 * before, axis=-1, keepdims=True),
                              jnp.where(lane == 1, jnp.sum(oh2 * before, axis=-1, keepdims=True), 0.0))
    lower = (lax.broadcasted_iota(jnp.int32, (LANE, LANE), 0) < lax.broadcasted_iota(jnp.int32, (LANE, LANE), 1))
    below = jnp.dot(oh, lower.astype(BF16), preferred_element_type=F32)
    count_acc[...] += jnp.sum(oh.astype(F32), axis=0, keepdims=True)
    start_acc[...] += jnp.sum(below, axis=0, keepdims=True)
    starts_ref[...] = start_acc[...]


def _rank(route, tm):
    t = route.shape[0]
    blocks = 2 * _nbytes((tm, LANE), F32) + _nbytes((1, LANE), F32)
    return pl.pallas_call(
        _rank_kernel,
        grid=(t // tm,),
        in_specs=[pl.BlockSpec((tm, LANE), lambda i: (i, 0))],
        out_specs=[pl.BlockSpec((tm, LANE), lambda i: (i, 0)), pl.BlockSpec((1, LANE), lambda i: (0, 0))],
        out_shape=[jax.ShapeDtypeStruct((t, LANE), F32), jax.ShapeDtypeStruct((1, LANE), F32)],
        scratch_shapes=[pltpu.VMEM((1, LANE), F32), pltpu.VMEM((1, LANE), F32)],
        compiler_params=_params(("arbitrary",), blocks, 2 * _nbytes((tm, tm), F32)),
        name="moe_rank",
    )(route)


def _dest_kernel(route_ref, rank_ref, starts_ref, dest_ref):
    oh1, oh2 = _one_hots(route_ref[...])
    rank = rank_ref[...]
    starts = starts_ref[...]
    d1 = jnp.sum(oh1 * starts, axis=-1, keepdims=True) + rank[:, 0:1]
    d2 = jnp.sum(oh2 * starts, axis=-1, keepdims=True) + rank[:, 1:2]
    lane = lax.broadcasted_iota(jnp.int32, rank.shape, 1)
    dest_ref[...] = jnp.where(lane == 0, d1, jnp.where(lane == 1, d2, 0.0)).astype(jnp.int32)


def _dest(route, rank, starts, tm):
    t = route.shape[0]
    blocks = 3 * _nbytes((tm, LANE), F32) + _nbytes((1, LANE), F32)
    return pl.pallas_call(
        _dest_kernel,
        grid=(t // tm,),
        in_specs=[pl.BlockSpec((tm, LANE), lambda i: (i, 0)), pl.BlockSpec((tm, LANE), lambda i: (i, 0)),
                  pl.BlockSpec((1, LANE), lambda i: (0, 0))],
        out_specs=pl.BlockSpec((tm, LANE), lambda i: (i, 0)),
        out_shape=jax.ShapeDtypeStruct((t, LANE), jnp.int32),
        compiler_params=_params(("parallel",), blocks, 4 * _nbytes((tm, LANE), F32)),
        name="moe_dest",
    )(route, rank, starts)


def _work_items(starts, n_rows):
    n_exp = starts.shape[0]
    n_blk = n_rows // MOE_ROWS
    total = jnp.full((1,), n_rows, jnp.int32)
    pts = jnp.concatenate([jnp.arange(n_blk, dtype=jnp.int32) * MOE_ROWS, starts[1:]])
    idx = jnp.arange(pts.shape[0], dtype=jnp.int32)
    before = (pts[None, :] < pts[:, None]) | ((pts[None, :] == pts[:, None]) & (idx[None, :] < idx[:, None]))
    pos = jnp.sum(before.astype(jnp.int32), axis=1)
    lo = jnp.sum(jnp.where(pos[:, None] == idx[None, :], pts[:, None], 0), axis=0)
    hi = jnp.concatenate([lo[1:], total])
    ends = jnp.concatenate([starts[1:], total])
    r = jnp.minimum(lo // MOE_ROWS, n_blk - 1)
    e = jnp.minimum(jnp.sum((ends[None, :] <= lo[:, None]).astype(jnp.int32), axis=1), n_exp - 1)
    later = jnp.where(e[None, :] > e[:, None], e[None, :], n_exp)
    nxt = jnp.min(later, axis=1)
    nxt = jnp.where(nxt == n_exp, -1, nxt)
    return r, e, lo, hi, nxt


def _stream_expert_weights(w, e_ref, nxt_ref, streams):
    e = e_ref[w]

    def copies(stream, expert):
        hbm, stage, _, sem = stream
        rows = stage.shape[0] // WEIGHT_DMA_PARTS
        parts = [pl.ds(part * rows, rows) for part in range(WEIGHT_DMA_PARTS)]
        return [pltpu.make_async_copy(hbm.at[expert, sl], stage.at[sl], sem) for sl in parts]

    @pl.when(w == 0)
    def _():
        for stream in streams:
            for c in copies(stream, e):
                c.start()

    @pl.when((w == 0) | (e != e_ref[jnp.maximum(w - 1, 0)]))
    def _():
        nxt = nxt_ref[w]
        for stream in streams:
            for c in copies(stream, e):
                c.wait()
            _, stage, w_bf, _ = stream
            rows = stage.shape[0] // CONVERT_PARTS
            for part in range(CONVERT_PARTS):
                sl = slice(part * rows, (part + 1) * rows)
                w_bf[sl, :] = stage[sl, :].astype(BF16)

            @pl.when(nxt >= 0)
            def _():
                for c in copies(stream, nxt):
                    c.start()


def _store_item_rows(o_ref, val, r, lo, hi):
    rows = r * MOE_ROWS + lax.broadcasted_iota(jnp.int32, val.shape, 0)
    mine = (rows >= lo) & (rows < hi)

    @pl.when(lo == r * MOE_ROWS)
    def _():
        o_ref[...] = val

    @pl.when(lo != r * MOE_ROWS)
    def _():
        pltpu.store(o_ref, val, mask=mine)


def _gather_block_rows(w, r_ref, dest_ref, x_hbm, source, rows_f32, x_bf, sem, *, n_assign):
    r = r_ref[w]
    n_blocks = n_assign // MOE_ROWS

    def copy(src_row, slot, j):
        return pltpu.make_async_copy(x_hbm.at[pl.ds(src_row, 1)], rows_f32.at[slot, pl.ds(j, 1)], sem.at[slot])

    def start_block(block, slot):
        for j in range(MOE_ROWS):
            copy(source[block * MOE_ROWS + j], slot, j).start()

    def finish_block(slot):
        for j in range(MOE_ROWS):
            copy(0, slot, j).wait()
        x_bf[...] = rows_f32[slot].astype(BF16)

    @pl.when(w == 0)
    def _():
        def invert(tok, carry):
            for k in range(TOP_K):
                source[dest_ref[TOP_K * tok + k]] = tok
            return carry
        lax.fori_loop(0, n_assign // TOP_K, invert, 0, unroll=4)
        start_block(r, 0)

    @pl.when((w == 0) | (r != r_ref[jnp.maximum(w - 1, 0)]))
    def _():
        for slot in range(2):
            @pl.when(r % 2 == slot)
            def _():
                finish_block(slot)

                @pl.when(r + 1 < n_blocks)
                def _():
                    start_block(r + 1, 1 - slot)


def _moe_experts_kernel(r_ref, e_ref, lo_ref, hi_ref, nxt_ref, dest_ref, x_hbm, wg_hbm, wu_hbm, wd_hbm, y_ref,
                        wg_stage, wu_stage, wd_stage, wg_bf, wu_bf, wd_bf, source, rows_f32, x_bf, sem, row_sem,
                        *, n_assign):
    w = pl.program_id(0)
    _stream_expert_weights(w, e_ref, nxt_ref, [(wg_hbm, wg_stage, wg_bf, sem.at[0]),
                                                 (wu_hbm, wu_stage, wu_bf, sem.at[1]),
                                                 (wd_hbm, wd_stage, wd_bf, sem.at[2])])
    _gather_block_rows(w, r_ref, dest_ref, x_hbm, source, rows_f32, x_bf, row_sem, n_assign=n_assign)
    r, lo, hi = r_ref[w], lo_ref[w], hi_ref[w]

    @pl.when(hi > lo)
    def _():
        x = x_bf[...]
        g = jnp.dot(x, wg_bf[...], preferred_element_type=F32)
        u = jnp.dot(x, wu_bf[...], preferred_element_type=F32)
        h = ((g * (1.0 / (1.0 + jnp.exp(-g)))) * u).astype(BF16)
        y = jnp.dot(h, wd_bf[...], preferred_element_type=F32)
        _store_item_rows(y_ref, y, r, lo, hi)


def _moe_experts(items, dest_flat, xn, wg, wu, wd):
    n_assign = dest_flat.shape[0]
    d = xn.shape[1]
    f = wg.shape[2]
    n_items = items[0].shape[0]
    blocks = _nbytes((MOE_ROWS, d), F32)
    resident = (3 * _nbytes((d, f), F32) + 3 * _nbytes((d, f), BF16) + 5 * _nbytes((MOE_ROWS, d), F32)
                + _nbytes((MOE_ROWS, d), BF16) + 6 * _nbytes((MOE_ROWS, f), F32))
    any_space = pl.BlockSpec(memory_space=pl.ANY)
    return pl.pallas_call(
        functools.partial(_moe_experts_kernel, n_assign=n_assign),
        grid_spec=pltpu.PrefetchScalarGridSpec(
            num_scalar_prefetch=6,
            grid=(n_items,),
            in_specs=[any_space, any_space, any_space, any_space],
            out_specs=pl.BlockSpec((MOE_ROWS, d), lambda w, r, e, lo, hi, nxt, dest: (r[w], 0)),
            scratch_shapes=[pltpu.VMEM((d, f), F32), pltpu.VMEM((d, f), F32), pltpu.VMEM((f, d), F32),
                            pltpu.VMEM((d, f), BF16), pltpu.VMEM((d, f), BF16), pltpu.VMEM((f, d), BF16),
                            pltpu.SMEM((n_assign,), jnp.int32), pltpu.VMEM((2, MOE_ROWS, d), F32),
                            pltpu.VMEM((MOE_ROWS, d), BF16),
                            pltpu.SemaphoreType.DMA((3,)), pltpu.SemaphoreType.DMA((2,))]),
        out_shape=jax.ShapeDtypeStruct((n_assign, d), F32),
        compiler_params=_params(("arbitrary",), blocks, resident),
        name="moe_experts",
    )(*items, dest_flat, xn, wg, wu, wd)


def _combine_kernel(dest_ref, x_ref, route_ref, y_hbm, o_ref, ybuf, sem, *, tm):
    half = tm // 2
    i = pl.program_id(0)

    def copy(src_row, slot, k, t):
        return pltpu.make_async_copy(y_hbm.at[pl.ds(src_row, 1)], ybuf.at[slot, k, pl.ds(t, 1)], sem.at[slot])

    def start_half(index, slot):
        for t in range(half):
            for k in range(TOP_K):
                copy(dest_ref[TOP_K * (index * half + t) + k], slot, k, t).start()

    def finish_half(slot):
        for t in range(half):
            for k in range(TOP_K):
                copy(0, slot, k, t).wait()
        rows = slice(slot * half, (slot + 1) * half)
        route = route_ref[rows, :]
        o_ref[rows, :] = x_ref[rows, :] + (route[:, 2:3] * ybuf[slot, 0] + route[:, 3:4] * ybuf[slot, 1])

    @pl.when(i == 0)
    def _():
        start_half(2 * i, 0)

    start_half(2 * i + 1, 1)
    finish_half(0)

    @pl.when(i + 1 < pl.num_programs(0))
    def _():
        start_half(2 * i + 2, 0)

    finish_half(1)


def _combine(dest_flat, x1, route, y, tm):
    t, d = x1.shape
    blocks = 2 * _nbytes((tm, d), F32) + _nbytes((tm, LANE), F32)
    resident = TOP_K * _nbytes((tm, d), F32) + _nbytes((tm, d), F32)
    return pl.pallas_call(
        functools.partial(_combine_kernel, tm=tm),
        grid_spec=pltpu.PrefetchScalarGridSpec(
            num_scalar_prefetch=1,
            grid=(t // tm,),
            in_specs=[pl.BlockSpec((tm, d), lambda i, dest: (i, 0)),
                      pl.BlockSpec((tm, LANE), lambda i, dest: (i, 0)),
                      pl.BlockSpec(memory_space=pl.ANY)],
            out_specs=pl.BlockSpec((tm, d), lambda i, dest: (i, 0)),
            scratch_shapes=[pltpu.VMEM((2, TOP_K, tm // 2, d), F32), pltpu.SemaphoreType.DMA((2,))]),
        out_shape=jax.ShapeDtypeStruct((t, d), F32),
        compiler_params=_params(("arbitrary",), blocks, resident),
        name="moe_combine",
    )(dest_flat, x1, route, y)


def _pad_cols(w, n):
    return jnp.pad(w, ((0, 0), (0, n - w.shape[1])))


def kernel(x, positions, g_mix, w_in, b_gate, q_norm_g, kv_norm_g, w_uq, w_ukv, a_q_norm_g, a_k_norm_g,
           b_q_norm_g, b_k_norm_g, rel_bias, w_o_a, w_o_b, w_out, g_ffn, w_group, w_expert,
           w_exp_gate, w_exp_up, w_exp_down):
    batch, seq, d = x.shape
    t = batch * seq
    q_lora, kv_lora = q_norm_g.shape[0], kv_norm_g.shape[0]
    a_heads = w_uq.shape[1] // A_QK
    b_heads = w_o_b.shape[0] // B_HEAD_DIM
    b_width = b_heads * B_HEAD_DIM
    n_groups, n_experts = w_group.shape[1], w_expert.shape[1]
    per_group = n_experts // n_groups
    off_b = q_lora + kv_lora + A_ROPE
    assert seq % ATTN_TILE == 0 and (TOP_K * t) % MOE_ROWS == 0 and n_groups + n_experts <= LANE
    assert a_heads == b_heads

    xf = x.reshape(t, d)
    tm_big = min(1024, t)
    tn = _tile(b_width, 512)
    tn_wide = _tile(b_width, 1024)
    assert d % tn == 0 and d % tn_wide == 0

    za_cols = -(-(q_lora + kv_lora + LANE) // tn_wide) * tn_wide
    wuq = jnp.pad(w_uq.reshape(q_lora, a_heads, A_QK), ((0, 0), (0, 0), (0, A_HEAD_PAD - A_QK)))
    wuq = wuq.reshape(q_lora, a_heads * A_HEAD_PAD).astype(BF16)
    wukv = w_ukv.astype(BF16)
    pad_gain = lambda g, s: jnp.pad(g * s, (0, A_HEAD_PAD - A_QK)).reshape(1, A_HEAD_PAD)
    gaq = pad_gain(a_q_norm_g, A_QK ** -0.5)
    gak = pad_gain(a_k_norm_g, 1.0)
    g_bqk = jnp.concatenate([jnp.tile(b_q_norm_g * B_HEAD_DIM ** -0.5, b_heads), jnp.tile(b_k_norm_g, b_heads)])

    half = A_ROPE // 2
    inv = ROPE_THETA ** (-jnp.arange(half, dtype=F32) / half)
    ang = positions.astype(F32).reshape(t, 1) * inv
    cos, sin = jnp.cos(ang), jnp.sin(ang)
    zeros = jnp.zeros((t, half), F32)
    cos_t = jnp.concatenate([cos, cos, zeros, zeros], axis=1)
    sin_lo = jnp.concatenate([-sin, zeros, zeros, zeros], axis=1)
    sin_hi = jnp.concatenate([zeros, sin, zeros, zeros], axis=1)

    xn = _rmsnorm_rows(xf, g_mix, BF16, min(256, t))
    w_in_t = w_in.T
    za = _inproj_cols(xn, w_in_t, 0, za_cols, None, _plain_epilogue, tm_big, tn_wide)
    bqk = _inproj_cols(xn, w_in_t, off_b, 2 * b_width, g_bqk, _head_norm_epilogue, tm_big, tn_wide)
    bv = _inproj_cols(xn, w_in_t, off_b + 2 * b_width, b_width, None, _plain_epilogue, tm_big, tn_wide)
    gates = _inproj_cols(xn, w_in_t, off_b + 3 * b_width, 2 * d, b_gate, _sigmoid_epilogue, tm_big, tn_wide)
    q, k, v = _mla_proj(za, cos_t, sin_lo, sin_hi, wuq, wukv, q_norm_g.reshape(1, -1), kv_norm_g.reshape(1, -1),
                        gaq, gak, a_heads, min(256, t))
    o_a, o_b = _attention(q, k, v, bqk, bv, _band_bias_rows(rel_bias), batch, seq, a_heads)
    merged = _merge(o_a, o_b, w_o_a, w_o_b, gates, tm_big, tn)
    x1 = _out_proj(merged, w_out, xf, tm_big, tn)

    wr = _pad_cols(jnp.concatenate([w_group, w_expert], axis=1), LANE)
    xn2, route = _router(x1, g_ffn, wr, n_groups, per_group, min(256, t))
    rank, starts_f = _rank(route, min(512, t))
    dest = _dest(route, rank, starts_f, min(512, t))[:, :TOP_K].reshape(-1)
    items = _work_items(starts_f[0, :n_experts].astype(jnp.int32), TOP_K * t)
    y = _moe_experts(items, dest, xn2, w_exp_gate, w_exp_up, w_exp_down)
    out = _combine(dest, x1, route, y, min(256, t))
    return out.reshape(batch, seq, d)
```

```python
import functools

import jax
import jax.numpy as jnp
from jax import lax
from jax.experimental import pallas as pl
from jax.experimental.pallas import tpu as pltpu

F32 = jnp.float32
BF16 = jnp.bfloat16

CHUNK = 64
EPS = 1e-6
A_NOPE = 128
A_ROPE = 64
A_V = 128
A_QK = A_NOPE + A_ROPE
B_HEAD_DIM = 128
B_LEFT_CHUNKS = 8
B_MAX_REL = 128
ROPE_THETA = 10000.0
TOP_K = 2

LANE = 128
A_HEAD_PAD = 2 * LANE
V_PAD = 2 * LANE
V7X_VMEM_BYTES = 64 * 2**20

ATTN_TILE = 256
MOE_ROWS = 128
WEIGHT_DMA_PARTS = 4
WEIGHT_DMA_PRIORITY = 1
CONVERT_PARTS = 8
PROJ_ROW_GROUPS = 4
NEG_INF = float("-inf")


def _nbytes(shape, dtype):
    n = 1
    for s in shape:
        n *= s
    return n * jnp.dtype(dtype).itemsize


def _params(semantics, pipelined_bytes, resident_bytes=0):
    need = 2 * pipelined_bytes + resident_bytes
    return pltpu.CompilerParams(dimension_semantics=semantics,
                                vmem_limit_bytes=min(int(need), V7X_VMEM_BYTES))


def _tile(n, want):
    t = want
    while t > LANE and n % t:
        t //= 2
    assert n % t == 0, (n, want)
    return t


def _rmsnorm_kernel(x_ref, g_ref, o_ref):
    x = x_ref[...]
    r = lax.rsqrt(jnp.mean(x * x, axis=-1, keepdims=True) + EPS)
    o_ref[...] = (x * r * g_ref[...]).astype(o_ref.dtype)


def _rmsnorm_rows(x, g, out_dtype, tm):
    t, d = x.shape
    blocks = _nbytes((tm, d), F32) + _nbytes((tm, d), out_dtype) + _nbytes((1, d), F32)
    return pl.pallas_call(
        _rmsnorm_kernel,
        grid=(t // tm,),
        in_specs=[pl.BlockSpec((tm, d), lambda i: (i, 0)), pl.BlockSpec((1, d), lambda i: (0, 0))],
        out_specs=pl.BlockSpec((tm, d), lambda i: (i, 0)),
        out_shape=jax.ShapeDtypeStruct((t, d), out_dtype),
        compiler_params=_params(("parallel",), blocks, _nbytes((tm, d), F32)),
        name="rmsnorm",
    )(x, g.reshape(1, d))


def _nt_dot(a, b):
    return lax.dot_general(a, b, (((1,), (1,)), ((), ())), preferred_element_type=F32)


def _head_norm_epilogue(acc, gb_ref):
    heads = []
    for h in range(acc.shape[1] // B_HEAD_DIM):
        sl = slice(h * B_HEAD_DIM, (h + 1) * B_HEAD_DIM)
        z = acc[:, sl]
        r = lax.rsqrt(jnp.mean(z * z, axis=-1, keepdims=True) + EPS)
        heads.append(z * r * gb_ref[:, sl])
    return jnp.concatenate(heads, axis=1)


def _plain_epilogue(acc, gb_ref):
    del gb_ref
    return acc


def _sigmoid_epilogue(acc, gb_ref):
    return 1.0 / (1.0 + jnp.exp(-(acc + gb_ref[...])))


def _inproj_cols_kernel(a_ref, wt_hbm, *refs, first_col, epilogue):
    gb_ref = refs[0] if len(refs) == 5 else None
    o_ref, stage, wt_bf, sem = refs[-4:]
    j = pl.program_id(0)
    tn = wt_bf.shape[0]

    def fetch(block):
        rows = pl.ds(pl.multiple_of(first_col + block * tn, 8), tn)
        return pltpu.make_async_copy(wt_hbm.at[rows], stage, sem)

    @pl.when(pl.program_id(1) == 0)
    def _():
        @pl.when(j == 0)
        def _():
            fetch(j).start()

        fetch(j).wait()
        wt_bf[...] = stage[...].astype(BF16)

        @pl.when(j + 1 < pl.num_programs(0))
        def _():
            fetch(j + 1).start()

    rows = a_ref.shape[0] // PROJ_ROW_GROUPS
    groups = [slice(s * rows, (s + 1) * rows) for s in range(PROJ_ROW_GROUPS)]
    accs = [_nt_dot(a_ref[sl, :], wt_bf[...]) for sl in groups]
    for sl, acc in zip(groups, accs):
        o_ref[sl, :] = epilogue(acc, gb_ref).astype(o_ref.dtype)


def _inproj_cols(xn, w_in_t, first_col, n_cols, gb, epilogue, tm, tn):
    m, k = xn.shape
    assert n_cols % tn == 0 and first_col % 8 == 0 and tm % PROJ_ROW_GROUPS == 0
    blocks = _nbytes((tm, k), BF16) + _nbytes((tm, tn), BF16) + _nbytes((1, tn), F32)
    resident = _nbytes((tn, k), F32) + _nbytes((tn, k), BF16) + 3 * _nbytes((tm, tn), F32)
    gb_specs = [] if gb is None else [pl.BlockSpec((1, tn), lambda j, i: (0, j))]
    gb_args = [] if gb is None else [gb.reshape(1, n_cols)]
    return pl.pallas_call(
        functools.partial(_inproj_cols_kernel, first_col=first_col, epilogue=epilogue),
        grid=(n_cols // tn, m // tm),
        in_specs=[pl.BlockSpec((tm, k), lambda j, i: (i, 0)), pl.BlockSpec(memory_space=pl.ANY)] + gb_specs,
        out_specs=pl.BlockSpec((tm, tn), lambda j, i: (i, j)),
        out_shape=jax.ShapeDtypeStruct((m, n_cols), BF16),
        scratch_shapes=[pltpu.VMEM((tn, k), F32), pltpu.VMEM((tn, k), BF16), pltpu.SemaphoreType.DMA(())],
        compiler_params=_params(("arbitrary", "arbitrary"), blocks, resident),
        name="inproj_" + epilogue.__name__.strip("_").replace("_epilogue", ""),
    )(xn, w_in_t, *gb_args)


def _rope_padded(v, cos, sin_lo, sin_hi):
    half = A_ROPE // 2
    return v * cos + pltpu.roll(v, LANE - half, 1) * sin_lo + pltpu.roll(v, half, 1) * sin_hi


def _ones_column(rows):
    lane = lax.broadcasted_iota(jnp.int32, (rows, V_PAD - A_V), 1)
    return jnp.where(lane == 0, 1.0, 0.0).astype(BF16)


def _mla_proj_kernel(za_ref, cos_ref, sl_ref, sh_ref, wuq_ref, wukv_ref, gq_ref, gkv_ref, gaq_ref, gak_ref,
                     q_ref, k_ref, v_ref, *, heads, q_lora, kv_lora):
    cos, sin_lo, sin_hi = cos_ref[...], sl_ref[...], sh_ref[...]

    def norm(z, g):
        z = z.astype(F32)
        r = lax.rsqrt(jnp.mean(z * z, axis=-1, keepdims=True) + EPS)
        return (z * r * g).astype(BF16)

    cq = norm(za_ref[:, :q_lora], gq_ref[...])
    ckv = norm(za_ref[:, q_lora:q_lora + kv_lora], gkv_ref[...])
    slab = za_ref[:, q_lora + kv_lora:q_lora + kv_lora + LANE].astype(F32)
    k_rope = jnp.where(lax.broadcasted_iota(jnp.int32, slab.shape, 1) < A_ROPE, slab, 0.0)
    qacc = jnp.dot(cq, wuq_ref[...], preferred_element_type=F32)
    kvacc = jnp.dot(ckv, wukv_ref[...], preferred_element_type=F32)

    gq_lo, gq_hi = gaq_ref[:, :LANE], gaq_ref[:, LANE:]
    gk_lo, gk_hi = gak_ref[:, :LANE], gak_ref[:, LANE:]
    kr_ss = jnp.sum(k_rope * k_rope, axis=-1, keepdims=True)
    kr_rot = _rope_padded(k_rope * gk_hi, cos, sin_lo, sin_hi)
    ones_col = _ones_column(slab.shape[0])
    for h in range(heads):
        base = h * A_HEAD_PAD
        q_lo = qacc[:, base:base + LANE]
        q_hi = qacc[:, base + LANE:base + A_HEAD_PAD]
        ss = jnp.sum(q_lo * q_lo, axis=-1, keepdims=True) + jnp.sum(q_hi * q_hi, axis=-1, keepdims=True)
        r = lax.rsqrt(ss / A_QK + EPS)
        q_ref[:, base:base + LANE] = (q_lo * r * gq_lo).astype(BF16)
        q_ref[:, base + LANE:base + A_HEAD_PAD] = _rope_padded(q_hi * r * gq_hi, cos, sin_lo, sin_hi).astype(BF16)

        k_lo = kvacc[:, base:base + LANE]
        ssk = jnp.sum(k_lo * k_lo, axis=-1, keepdims=True) + kr_ss
        rk = lax.rsqrt(ssk / A_QK + EPS)
        k_ref[:, base:base + LANE] = (k_lo * rk * gk_lo).astype(BF16)
        k_ref[:, base + LANE:base + A_HEAD_PAD] = (kr_rot * rk).astype(BF16)
        v_ref[:, h * V_PAD:h * V_PAD + A_V] = kvacc[:, base + LANE:base + A_HEAD_PAD].astype(BF16)
        v_ref[:, h * V_PAD + A_V:(h + 1) * V_PAD] = ones_col


def _mla_proj(za, cos, sin_lo, sin_hi, wuq, wukv, gq, gkv, gaq, gak, heads, tm):
    t, za_cols = za.shape
    q_lora, kv_lora = wuq.shape[0], wukv.shape[0]
    hp = heads * A_HEAD_PAD
    row = lambda i: (i, 0)
    fix = lambda i: (0, 0)
    blocks = (_nbytes((tm, za_cols), za.dtype) + 3 * _nbytes((tm, LANE), F32) + _nbytes(wuq.shape, BF16)
              + _nbytes(wukv.shape, BF16) + 2 * _nbytes((tm, hp), BF16) + _nbytes((tm, heads * V_PAD), BF16))
    return pl.pallas_call(
        functools.partial(_mla_proj_kernel, heads=heads, q_lora=q_lora, kv_lora=kv_lora),
        grid=(t // tm,),
        in_specs=[pl.BlockSpec((tm, za_cols), row),
                  pl.BlockSpec((tm, LANE), row), pl.BlockSpec((tm, LANE), row), pl.BlockSpec((tm, LANE), row),
                  pl.BlockSpec(wuq.shape, fix), pl.BlockSpec(wukv.shape, fix),
                  pl.BlockSpec((1, q_lora), fix), pl.BlockSpec((1, kv_lora), fix),
                  pl.BlockSpec((1, A_HEAD_PAD), fix), pl.BlockSpec((1, A_HEAD_PAD), fix)],
        out_specs=[pl.BlockSpec((tm, hp), row), pl.BlockSpec((tm, hp), row), pl.BlockSpec((tm, heads * V_PAD), row)],
        out_shape=[jax.ShapeDtypeStruct((t, hp), BF16), jax.ShapeDtypeStruct((t, hp), BF16),
                   jax.ShapeDtypeStruct((t, heads * V_PAD), BF16)],
        compiler_params=_params(("parallel",), blocks, 3 * _nbytes((tm, hp), F32)),
        name="mla_proj",
    )(za, cos, sin_lo, sin_hi, wuq, wukv, gq, gkv, gaq, gak)


def _mla_scores(i, q_ref, k_ref, diag_ok):
    tq = ATTN_TILE
    q = q_ref[i * tq:(i + 1) * tq, :]
    sd = jnp.where(diag_ok, _nt_dot(q, k_ref[i * tq:(i + 1) * tq, :]), NEG_INF)
    s0 = _nt_dot(q, k_ref[:i * tq, :]) if i > 0 else None
    return sd, s0


def _mla_softmax(sd, s0):
    m = jnp.max(sd, axis=-1, keepdims=True)
    if s0 is None:
        return jnp.exp(sd - m).astype(BF16), None
    m = jnp.maximum(m, jnp.max(s0, axis=-1, keepdims=True))
    return jnp.exp(sd - m).astype(BF16), jnp.exp(s0 - m).astype(BF16)


def _mla_values(i, pd, p0, v_ref, o_ref):
    tq = ATTN_TILE
    o = jnp.dot(pd, v_ref[i * tq:(i + 1) * tq, :], preferred_element_type=F32)
    if p0 is not None:
        o = o + jnp.dot(p0, v_ref[:i * tq, :], preferred_element_type=F32)
    o_ref[i * tq:(i + 1) * tq, :] = (o[:, :A_V] / o[:, A_V:A_V + 1]).astype(o_ref.dtype)


def _band_window(i):
    left = B_LEFT_CHUNKS * CHUNK
    q0 = i * ATTN_TILE
    k0 = max(0, q0 - left)
    return q0, k0, q0 + ATTN_TILE - k0, left - q0 + k0


def _band_scores(i, q_ref, k_ref, table):
    q0, k0, kw, u0 = _band_window(i)
    return _nt_dot(q_ref[q0:q0 + ATTN_TILE, :], k_ref[k0:k0 + kw, :]) + table[:, u0:u0 + kw]


def _band_softmax(s):
    return jnp.exp(s - jnp.max(s, axis=-1, keepdims=True)).astype(BF16)


def _band_values(i, p, v_ref, o_ref):
    q0, k0, kw, _ = _band_window(i)
    o = jnp.dot(p, v_ref[k0:k0 + kw, :], preferred_element_type=F32)
    o_ref[q0:q0 + ATTN_TILE, :] = (o[:, :B_HEAD_DIM] / o[:, B_HEAD_DIM:B_HEAD_DIM + 1]).astype(o_ref.dtype)


def _attn_kernel(aq_ref, ak_ref, av_ref, bq_ref, bk_ref, bv_ref, r_ref, oa_ref, ob_ref, bv_pad, *, seq):
    tq = ATTN_TILE
    left = B_LEFT_CHUNKS * CHUNK
    width = r_ref.shape[1]
    rc = lax.broadcasted_iota(jnp.int32, (tq, tq), 0) // CHUNK
    cc = lax.broadcasted_iota(jnp.int32, (tq, tq), 1) // CHUNK
    diag_ok = cc <= rc
    bias = pltpu.roll(jnp.broadcast_to(r_ref[...], (tq, width)), 0, 1, stride=1, stride_axis=0)[:, :left + tq]
    q_chunk = lax.broadcasted_iota(jnp.int32, (tq, left + tq), 0) // CHUNK
    k_chunk = lax.broadcasted_iota(jnp.int32, (tq, left + tq), 1) // CHUNK
    table = jnp.where((k_chunk >= q_chunk) & (k_chunk <= q_chunk + B_LEFT_CHUNKS), bias, NEG_INF)
    bv_pad[:, :B_HEAD_DIM] = bv_ref[...]
    bv_pad[:, B_HEAD_DIM:] = _ones_column(seq)

    n_tiles = seq // tq
    sd, s0 = _mla_scores(0, aq_ref, ak_ref, diag_ok)
    for i in range(n_tiles):
        sb = _band_scores(i, bq_ref, bk_ref, table)
        pd, p0 = _mla_softmax(sd, s0)
        if i + 1 < n_tiles:
            sd, s0 = _mla_scores(i + 1, aq_ref, ak_ref, diag_ok)
        _mla_values(i, pd, p0, av_ref, oa_ref)
        _band_values(i, _band_softmax(sb), bv_pad, ob_ref)


def _band_bias_rows(rel_bias):
    left = B_LEFT_CHUNKS * CHUNK
    width = left + 2 * ATTN_TILE
    m = jnp.arange(width, dtype=jnp.int32)
    j = jnp.where(m < left + ATTN_TILE, m, m - width)
    dist = left - j
    rows = rel_bias[:, jnp.clip(dist, -B_MAX_REL, B_MAX_REL) + B_MAX_REL].astype(F32)
    return rows.reshape(rel_bias.shape[0], 1, width)


def _attention(q, k, v, bqk, bv, bias_rows, batch, seq, heads):
    t = batch * seq
    d = B_HEAD_DIM
    width = bias_rows.shape[2]
    head = lambda b, h: (b, h)
    blocks = (2 * _nbytes((seq, A_HEAD_PAD), BF16) + _nbytes((seq, V_PAD), BF16) + _nbytes((seq, A_V), BF16)
              + 4 * _nbytes((seq, d), BF16) + _nbytes((1, width), F32))
    temps = _nbytes((seq, V_PAD), BF16) + 8 * _nbytes((ATTN_TILE, seq), F32) + 8 * _nbytes((ATTN_TILE, width), F32)
    return pl.pallas_call(
        functools.partial(_attn_kernel, seq=seq),
        grid=(batch, heads),
        in_specs=[pl.BlockSpec((seq, A_HEAD_PAD), head), pl.BlockSpec((seq, A_HEAD_PAD), head),
                  pl.BlockSpec((seq, V_PAD), head),
                  pl.BlockSpec((seq, d), head),
                  pl.BlockSpec((seq, d), lambda b, h: (b, heads + h)),
                  pl.BlockSpec((seq, d), head),
                  pl.BlockSpec((None, 1, width), lambda b, h: (h, 0, 0))],
        out_specs=[pl.BlockSpec((seq, A_V), head), pl.BlockSpec((seq, d), head)],
        out_shape=[jax.ShapeDtypeStruct((t, heads * A_V), BF16), jax.ShapeDtypeStruct((t, heads * d), BF16)],
        scratch_shapes=[pltpu.VMEM((seq, V_PAD), BF16)],
        compiler_params=_params(("parallel", "parallel"), blocks, temps),
        name="attention",
    )(q, k, v, bqk, bqk, bv, bias_rows)


def _merge_kernel(oa_ref, ob_ref, woa_ref, wob_ref, g0_ref, g1_ref, o_ref, woa_bf, wob_bf):
    @pl.when(pl.program_id(1) == 0)
    def _():
        woa_bf[...] = woa_ref[...].astype(BF16)
        wob_bf[...] = wob_ref[...].astype(BF16)

    a = jnp.dot(oa_ref[...], woa_bf[...], preferred_element_type=F32)
    b = jnp.dot(ob_ref[...], wob_bf[...], preferred_element_type=F32)
    o_ref[...] = (g0_ref[...].astype(F32) * a + g1_ref[...].astype(F32) * b).astype(o_ref.dtype)


def _merge(oa, ob, woa, wob, gates, tm, tn):
    m = oa.shape[0]
    d = woa.shape[1]
    g1 = d // tn
    blocks = (_nbytes((tm, oa.shape[1]), BF16) + _nbytes((tm, ob.shape[1]), BF16) + _nbytes((woa.shape[0], tn), F32)
              + _nbytes((wob.shape[0], tn), F32) + 3 * _nbytes((tm, tn), BF16))
    resident = (2 * _nbytes((woa.shape[0], tn), BF16) + 2 * _nbytes((wob.shape[0], tn), BF16)
                + 3 * _nbytes((tm, tn), F32))
    return pl.pallas_call(
        _merge_kernel,
        grid=(d // tn, m // tm),
        in_specs=[pl.BlockSpec((tm, oa.shape[1]), lambda j, i: (i, 0)),
                  pl.BlockSpec((tm, ob.shape[1]), lambda j, i: (i, 0)),
                  pl.BlockSpec((woa.shape[0], tn), lambda j, i: (0, j)),
                  pl.BlockSpec((wob.shape[0], tn), lambda j, i: (0, j)),
                  pl.BlockSpec((tm, tn), lambda j, i: (i, j)),
                  pl.BlockSpec((tm, tn), lambda j, i: (i, g1 + j))],
        out_specs=pl.BlockSpec((tm, tn), lambda j, i: (i, j)),
        out_shape=jax.ShapeDtypeStruct((m, d), BF16),
        scratch_shapes=[pltpu.VMEM((woa.shape[0], tn), BF16), pltpu.VMEM((wob.shape[0], tn), BF16)],
        compiler_params=_params(("parallel", "arbitrary"), blocks, resident),
        name="merge",
    )(oa, ob, woa, wob, gates, gates)


def _out_proj_kernel(a_ref, w_ref, x_ref, o_ref, w_bf):
    @pl.when(pl.program_id(1) == 0)
    def _():
        w_bf[...] = w_ref[...].astype(BF16)

    o_ref[...] = x_ref[...] + jnp.dot(a_ref[...], w_bf[...], preferred_element_type=F32)


def _out_proj(merged, wout, x, tm, tn):
    m, k = merged.shape
    n = wout.shape[1]
    blocks = _nbytes((tm, k), BF16) + _nbytes((k, tn), F32) + 2 * _nbytes((tm, tn), F32)
    return pl.pallas_call(
        _out_proj_kernel,
        grid=(n // tn, m // tm),
        in_specs=[pl.BlockSpec((tm, k), lambda j, i: (i, 0)),
                  pl.BlockSpec((k, tn), lambda j, i: (0, j)),
                  pl.BlockSpec((tm, tn), lambda j, i: (i, j))],
        out_specs=pl.BlockSpec((tm, tn), lambda j, i: (i, j)),
        out_shape=jax.ShapeDtypeStruct((m, n), F32),
        scratch_shapes=[pltpu.VMEM((k, tn), BF16)],
        compiler_params=_params(("parallel", "arbitrary"), blocks, 2 * _nbytes((k, tn), BF16) + _nbytes((tm, tn), F32)),
        name="out_proj",
    )(merged, wout, x)


def _split_bf16(v):
    hi = v.astype(BF16)
    return hi, (v - hi.astype(F32)).astype(BF16)


def _router_kernel(x_ref, g_ref, wr_ref, xn_ref, route_ref, *, n_groups, per_group):
    x = x_ref[...]
    r = lax.rsqrt(jnp.mean(x * x, axis=-1, keepdims=True) + EPS)
    xn = x * r * g_ref[...]
    xn_ref[...] = xn
    x_hi, x_lo = _split_bf16(xn)
    w_hi, w_lo = _split_bf16(wr_ref[...])
    dot = functools.partial(jnp.dot, preferred_element_type=F32)
    logits = dot(x_hi, w_hi) + (dot(x_lo, w_hi) + dot(x_hi, w_lo))
    lane = lax.broadcasted_iota(jnp.int32, logits.shape, 1).astype(F32)
    far = float(LANE)

    def top(vals):
        best = jnp.max(vals, axis=-1, keepdims=True)
        return best, jnp.min(jnp.where(vals == best, lane, far), axis=-1, keepdims=True)

    gl = jnp.where(lane < n_groups, logits, NEG_INF)
    gmax, grp = top(gl)
    p_grp = 1.0 / jnp.sum(jnp.exp(gl - gmax), axis=-1, keepdims=True)
    lo = n_groups + grp * per_group
    el = jnp.where((lane >= lo) & (lane < lo + per_group), logits, NEG_INF)
    t1, i1 = top(el)
    t2, i2 = top(jnp.where(lane == i1, NEG_INF, el))
    d = jnp.exp(t2 - t1)
    w1 = p_grp / (1.0 + d)
    w2 = p_grp * d / (1.0 + d)
    route_ref[...] = jnp.where(lane == 0, i1 - n_groups,
                               jnp.where(lane == 1, i2 - n_groups,
                                         jnp.where(lane == 2, w1, jnp.where(lane == 3, w2, 0.0))))


def _router(x1, g, wr, n_groups, per_group, tm):
    t, d = x1.shape
    blocks = 2 * _nbytes((tm, d), F32) + _nbytes((1, d), F32) + _nbytes((d, LANE), F32) + _nbytes((tm, LANE), F32)
    return pl.pallas_call(
        functools.partial(_router_kernel, n_groups=n_groups, per_group=per_group),
        grid=(t // tm,),
        in_specs=[pl.BlockSpec((tm, d), lambda i: (i, 0)), pl.BlockSpec((1, d), lambda i: (0, 0)),
                  pl.BlockSpec((d, LANE), lambda i: (0, 0))],
        out_specs=[pl.BlockSpec((tm, d), lambda i: (i, 0)), pl.BlockSpec((tm, LANE), lambda i: (i, 0))],
        out_shape=[jax.ShapeDtypeStruct((t, d), F32), jax.ShapeDtypeStruct((t, LANE), F32)],
        compiler_params=_params(("parallel",), blocks, 2 * _nbytes((tm, d), F32)),
        name="router",
    )(x1, g.reshape(1, d), wr)


def _one_hots(route):
    lane = lax.broadcasted_iota(jnp.int32, route.shape, 1).astype(F32)
    return (lane == route[:, 0:1]).astype(F32), (lane == route[:, 1:2]).astype(F32)


def _rank_kernel(route_ref, rank_ref, starts_ref, count_acc, start_acc):
    i = pl.program_id(0)

    @pl.when(i == 0)
    def _():
        count_acc[...] = jnp.zeros_like(count_acc)
        start_acc[...] = jnp.zeros_like(start_acc)

    oh1, oh2 = _one_hots(route_ref[...])
    oh = (oh1 + oh2).astype(BF16)
    tm = oh.shape[0]
    earlier = (lax.broadcasted_iota(jnp.int32, (tm, tm), 0) > lax.broadcasted_iota(jnp.int32, (tm, tm), 1))
    before = jnp.dot(earlier.astype(BF16), oh, preferred_element_type=F32) + count_acc[...]
    lane = lax.broadcasted_iota(jnp.int32, (tm, LANE), 1)
    rank_ref[...] = jnp.where(lane == 0, jnp.sum(oh1 * before, axis=-1, keepdims=True),
                              jnp.where(lane == 1, jnp.sum(oh2 * before, axis=-1, keepdims=True), 0.0))
    lower = (lax.broadcasted_iota(jnp.int32, (LANE, LANE), 0) < lax.broadcasted_iota(jnp.int32, (LANE, LANE), 1))
    below = jnp.dot(oh, lower.astype(BF16), preferred_element_type=F32)
    count_acc[...] += jnp.sum(oh.astype(F32), axis=0, keepdims=True)
    start_acc[...] += jnp.sum(below, axis=0, keepdims=True)
    starts_ref[...] = start_acc[...]


def _rank(route, tm):
    t = route.shape[0]
    blocks = 2 * _nbytes((tm, LANE), F32) + _nbytes((1, LANE), F32)
    return pl.pallas_call(
        _rank_kernel,
        grid=(t // tm,),
        in_specs=[pl.BlockSpec((tm, LANE), lambda i: (i, 0))],
        out_specs=[pl.BlockSpec((tm, LANE), lambda i: (i, 0)), pl.BlockSpec((1, LANE), lambda i: (0, 0))],
        out_shape=[jax.ShapeDtypeStruct((t, LANE), F32), jax.ShapeDtypeStruct((1, LANE), F32)],
        scratch_shapes=[pltpu.VMEM((1, LANE), F32), pltpu.VMEM((1, LANE), F32)],
        compiler_params=_params(("arbitrary",), blocks, 2 * _nbytes((tm, tm), F32)),
        name="moe_rank",
    )(route)


def _dest_kernel(route_ref, rank_ref, starts_ref, dest_ref):
    oh1, oh2 = _one_hots(route_ref[...])
    rank = rank_ref[...]
    starts = starts_ref[...]
    d1 = jnp.sum(oh1 * starts, axis=-1, keepdims=True) + rank[:, 0:1]
    d2 = jnp.sum(oh2 * starts, axis=-1, keepdims=True) + rank[:, 1:2]
    lane = lax.broadcasted_iota(jnp.int32, rank.shape, 1)
    dest_ref[...] = jnp.where(lane == 0, d1, jnp.where(lane == 1, d2, 0.0)).astype(jnp.int32)


def _dest(route, rank, starts, tm):
    t = route.shape[0]
    blocks = 3 * _nbytes((tm, LANE), F32) + _nbytes((1, LANE), F32)
    return pl.pallas_call(
        _dest_kernel,
        grid=(t // tm,),
        in_specs=[pl.BlockSpec((tm, LANE), lambda i: (i, 0)), pl.BlockSpec((tm, LANE), lambda i: (i, 0)),
                  pl.BlockSpec((1, LANE), lambda i: (0, 0))],
        out_specs=pl.BlockSpec((tm, LANE), lambda i: (i, 0)),
        out_shape=jax.ShapeDtypeStruct((t, LANE), jnp.int32),
        compiler_params=_params(("parallel",), blocks, 4 * _nbytes((tm, LANE), F32)),
        name="moe_dest",
    )(route, rank, starts)


def _work_items(starts, n_rows):
    n_exp = starts.shape[0]
    n_blk = n_rows // MOE_ROWS
    total = jnp.full((1,), n_rows, jnp.int32)
    pts = jnp.concatenate([jnp.arange(n_blk, dtype=jnp.int32) * MOE_ROWS, starts[1:]])
    idx = jnp.arange(pts.shape[0], dtype=jnp.int32)
    before = (pts[None, :] < pts[:, None]) | ((pts[None, :] == pts[:, None]) & (idx[None, :] < idx[:, None]))
    pos = jnp.sum(before.astype(jnp.int32), axis=1)
    lo = jnp.sum(jnp.where(pos[:, None] == idx[None, :], pts[:, None], 0), axis=0)
    hi = jnp.concatenate([lo[1:], total])
    ends = jnp.concatenate([starts[1:], total])
    r = jnp.minimum(lo // MOE_ROWS, n_blk - 1)
    e = jnp.minimum(jnp.sum((ends[None, :] <= lo[:, None]).astype(jnp.int32), axis=1), n_exp - 1)
    later = jnp.where(e[None, :] > e[:, None], e[None, :], n_exp)
    nxt = jnp.min(later, axis=1)
    nxt = jnp.where(nxt == n_exp, -1, nxt)
    return r, e, lo, hi, nxt


def _stream_expert_weights(w, e_ref, nxt_ref, streams):
    e = e_ref[w]

    def copies(stream, expert):
        hbm, stage, _, sem = stream
        rows = stage.shape[0] // WEIGHT_DMA_PARTS
        parts = [pl.ds(part * rows, rows) for part in range(WEIGHT_DMA_PARTS)]
        return [pltpu.make_async_copy(hbm.at[expert, sl], stage.at[sl], sem) for sl in parts]

    @pl.when(w == 0)
    def _():
        for stream in streams:
            for c in copies(stream, e):
                c.start(priority=WEIGHT_DMA_PRIORITY)

    @pl.when((w == 0) | (e != e_ref[jnp.maximum(w - 1, 0)]))
    def _():
        nxt = nxt_ref[w]
        for stream in streams:
            for c in copies(stream, e):
                c.wait()
            _, stage, w_bf, _ = stream
            rows = stage.shape[0] // CONVERT_PARTS
            for part in range(CONVERT_PARTS):
                sl = slice(part * rows, (part + 1) * rows)
                w_bf[sl, :] = stage[sl, :].astype(BF16)

            @pl.when(nxt >= 0)
            def _():
                for c in copies(stream, nxt):
                    c.start(priority=WEIGHT_DMA_PRIORITY)


def _store_item_rows(o_ref, val, r, lo, hi):
    rows = r * MOE_ROWS + lax.broadcasted_iota(jnp.int32, val.shape, 0)
    mine = (rows >= lo) & (rows < hi)

    @pl.when(lo == r * MOE_ROWS)
    def _():
        o_ref[...] = val

    @pl.when(lo != r * MOE_ROWS)
    def _():
        pltpu.store(o_ref, val, mask=mine)


def _gather_block_rows(w, r_ref, dest_ref, x_hbm, source, rows_f32, x_bf, sem, *, n_assign):
    r = r_ref[w]
    n_blocks = n_assign // MOE_ROWS

    def copy(src_row, slot, j):
        return pltpu.make_async_copy(x_hbm.at[pl.ds(src_row, 1)], rows_f32.at[slot, pl.ds(j, 1)], sem.at[slot])

    def start_block(block, slot):
        for j in range(MOE_ROWS):
            copy(source[block * MOE_ROWS + j], slot, j).start()

    def finish_block(slot):
        for j in range(MOE_ROWS):
            copy(0, slot, j).wait()
        x_bf[...] = rows_f32[slot].astype(BF16)

    @pl.when(w == 0)
    def _():
        def invert(tok, carry):
            for k in range(TOP_K):
                source[dest_ref[TOP_K * tok + k]] = tok
            return carry
        lax.fori_loop(0, n_assign // TOP_K, invert, 0, unroll=4)
        start_block(r, 0)

    @pl.when((w == 0) | (r != r_ref[jnp.maximum(w - 1, 0)]))
    def _():
        for slot in range(2):
            @pl.when(r % 2 == slot)
            def _():
                finish_block(slot)

                @pl.when(r + 1 < n_blocks)
                def _():
                    start_block(r + 1, 1 - slot)


def _moe_experts_kernel(r_ref, e_ref, lo_ref, hi_ref, nxt_ref, dest_ref, x_hbm, wg_hbm, wu_hbm, wd_hbm, y_ref,
                        wg_stage, wu_stage, wd_stage, wg_bf, wu_bf, wd_bf, source, rows_f32, x_bf, sem, row_sem,
                        *, n_assign):
    w = pl.program_id(0)
    _stream_expert_weights(w, e_ref, nxt_ref, [(wg_hbm, wg_stage, wg_bf, sem.at[0]),
                                                 (wu_hbm, wu_stage, wu_bf, sem.at[1]),
                                                 (wd_hbm, wd_stage, wd_bf, sem.at[2])])
    _gather_block_rows(w, r_ref, dest_ref, x_hbm, source, rows_f32, x_bf, row_sem, n_assign=n_assign)
    r, lo, hi = r_ref[w], lo_ref[w], hi_ref[w]

    @pl.when(hi > lo)
    def _():
        x = x_bf[...]
        g = jnp.dot(x, wg_bf[...], preferred_element_type=F32)
        u = jnp.dot(x, wu_bf[...], preferred_element_type=F32)
        h = ((g * (1.0 / (1.0 + jnp.exp(-g)))) * u).astype(BF16)
        y = jnp.dot(h, wd_bf[...], preferred_element_type=F32)
        _store_item_rows(y_ref, y, r, lo, hi)


def _moe_experts(items, dest_flat, xn, wg, wu, wd):
    n_assign = dest_flat.shape[0]
    d = xn.shape[1]
    f = wg.shape[2]
    n_items = items[0].shape[0]
    blocks = _nbytes((MOE_ROWS, d), F32)
    resident = (3 * _nbytes((d, f), F32) + 3 * _nbytes((d, f), BF16) + 5 * _nbytes((MOE_ROWS, d), F32)
                + _nbytes((MOE_ROWS, d), BF16) + 6 * _nbytes((MOE_ROWS, f), F32))
    any_space = pl.BlockSpec(memory_space=pl.ANY)
    return pl.pallas_call(
        functools.partial(_moe_experts_kernel, n_assign=n_assign),
        grid_spec=pltpu.PrefetchScalarGridSpec(
            num_scalar_prefetch=6,
            grid=(n_items,),
            in_specs=[any_space, any_space, any_space, any_space],
            out_specs=pl.BlockSpec((MOE_ROWS, d), lambda w, r, e, lo, hi, nxt, dest: (r[w], 0)),
            scratch_shapes=[pltpu.VMEM((d, f), F32), pltpu.VMEM((d, f), F32), pltpu.VMEM((f, d), F32),
                            pltpu.VMEM((d, f), BF16), pltpu.VMEM((d, f), BF16), pltpu.VMEM((f, d), BF16),
                            pltpu.SMEM((n_assign,), jnp.int32), pltpu.VMEM((2, MOE_ROWS, d), F32),
                            pltpu.VMEM((MOE_ROWS, d), BF16),
                            pltpu.SemaphoreType.DMA((3,)), pltpu.SemaphoreType.DMA((2,))]),
        out_shape=jax.ShapeDtypeStruct((n_assign, d), F32),
        compiler_params=_params(("arbitrary",), blocks, resident),
        name="moe_experts",
    )(*items, dest_flat, xn, wg, wu, wd)


def _combine_kernel(dest_ref, x_ref, route_ref, y_hbm, o_ref, ybuf, sem, *, tm):
    half = tm // 2
    i = pl.program_id(0)

    def copy(src_row, slot, k, t):
        return pltpu.make_async_copy(y_hbm.at[pl.ds(src_row, 1)], ybuf.at[slot, k, pl.ds(t, 1)], sem.at[slot])

    def start_half(index, slot):
        for t in range(half):
            for k in range(TOP_K):
                copy(dest_ref[TOP_K * (index * half + t) + k], slot, k, t).start(priority=k % 2)

    def finish_half(slot):
        for t in range(half):
            for k in range(TOP_K):
                copy(0, slot, k, t).wait()
        rows = slice(slot * half, (slot + 1) * half)
        route = route_ref[rows, :]
        o_ref[rows, :] = x_ref[rows, :] + (route[:, 2:3] * ybuf[slot, 0] + route[:, 3:4] * ybuf[slot, 1])

    @pl.when(i == 0)
    def _():
        start_half(2 * i, 0)

    start_half(2 * i + 1, 1)
    finish_half(0)

    @pl.when(i + 1 < pl.num_programs(0))
    def _():
        start_half(2 * i + 2, 0)

    finish_half(1)


def _combine(dest_flat, x1, route, y, tm):
    t, d = x1.shape
    blocks = 2 * _nbytes((tm, d), F32) + _nbytes((tm, LANE), F32)
    resident = TOP_K * _nbytes((tm, d), F32) + _nbytes((tm, d), F32)
    return pl.pallas_call(
        functools.partial(_combine_kernel, tm=tm),
        grid_spec=pltpu.PrefetchScalarGridSpec(
            num_scalar_prefetch=1,
            grid=(t // tm,),
            in_specs=[pl.BlockSpec((tm, d), lambda i, dest: (i, 0)),
                      pl.BlockSpec((tm, LANE), lambda i, dest: (i, 0)),
                      pl.BlockSpec(memory_space=pl.ANY)],
            out_specs=pl.BlockSpec((tm, d), lambda i, dest: (i, 0)),
            scratch_shapes=[pltpu.VMEM((2, TOP_K, tm // 2, d), F32), pltpu.SemaphoreType.DMA((2,))]),
        out_shape=jax.ShapeDtypeStruct((t, d), F32),
        compiler_params=_params(("arbitrary",), blocks, resident),
        name="moe_combine",
    )(dest_flat, x1, route, y)


def _pad_cols(w, n):
    return jnp.pad(w, ((0, 0), (0, n - w.shape[1])))


def kernel(x, positions, g_mix, w_in, b_gate, q_norm_g, kv_norm_g, w_uq, w_ukv, a_q_norm_g, a_k_norm_g,
           b_q_norm_g, b_k_norm_g, rel_bias, w_o_a, w_o_b, w_out, g_ffn, w_group, w_expert,
           w_exp_gate, w_exp_up, w_exp_down):
    batch, seq, d = x.shape
    t = batch * seq
    q_lora, kv_lora = q_norm_g.shape[0], kv_norm_g.shape[0]
    a_heads = w_uq.shape[1] // A_QK
    b_heads = w_o_b.shape[0] // B_HEAD_DIM
    b_width = b_heads * B_HEAD_DIM
    n_groups, n_experts = w_group.shape[1], w_expert.shape[1]
    per_group = n_experts // n_groups
    off_b = q_lora + kv_lora + A_ROPE
    assert seq % ATTN_TILE == 0 and (TOP_K * t) % MOE_ROWS == 0 and n_groups + n_experts <= LANE
    assert a_heads == b_heads

    xf = x.reshape(t, d)
    tm_big = min(1024, t)
    tn = _tile(b_width, 512)
    tn_wide = _tile(b_width, 1024)
    assert d % tn == 0 and d % tn_wide == 0

    za_cols = -(-(q_lora + kv_lora + LANE) // tn_wide) * tn_wide
    wuq = jnp.pad(w_uq.reshape(q_lora, a_heads, A_QK), ((0, 0), (0, 0), (0, A_HEAD_PAD - A_QK)))
    wuq = wuq.reshape(q_lora, a_heads * A_HEAD_PAD).astype(BF16)
    wukv = w_ukv.astype(BF16)
    pad_gain = lambda g, s: jnp.pad(g * s, (0, A_HEAD_PAD - A_QK)).reshape(1, A_HEAD_PAD)
    gaq = pad_gain(a_q_norm_g, A_QK ** -0.5)
    gak = pad_gain(a_k_norm_g, 1.0)
    g_bqk = jnp.concatenate([jnp.tile(b_q_norm_g * B_HEAD_DIM ** -0.5, b_heads), jnp.tile(b_k_norm_g, b_heads)])

    half = A_ROPE // 2
    inv = ROPE_THETA ** (-jnp.arange(half, dtype=F32) / half)
    ang = positions.astype(F32).reshape(t, 1) * inv
    cos, sin = jnp.cos(ang), jnp.sin(ang)
    zeros = jnp.zeros((t, half), F32)
    cos_t = jnp.concatenate([cos, cos, zeros, zeros], axis=1)
    sin_lo = jnp.concatenate([-sin, zeros, zeros, zeros], axis=1)
    sin_hi = jnp.concatenate([zeros, sin, zeros, zeros], axis=1)

    xn = _rmsnorm_rows(xf, g_mix, BF16, min(256, t))
    w_in_t = w_in.T
    za = _inproj_cols(xn, w_in_t, 0, za_cols, None, _plain_epilogue, tm_big, tn_wide)
    bqk = _inproj_cols(xn, w_in_t, off_b, 2 * b_width, g_bqk, _head_norm_epilogue, tm_big, tn_wide)
    bv = _inproj_cols(xn, w_in_t, off_b + 2 * b_width, b_width, None, _plain_epilogue, tm_big, tn_wide)
    gates = _inproj_cols(xn, w_in_t, off_b + 3 * b_width, 2 * d, b_gate, _sigmoid_epilogue, tm_big, tn_wide)
    q, k, v = _mla_proj(za, cos_t, sin_lo, sin_hi, wuq, wukv, q_norm_g.reshape(1, -1), kv_norm_g.reshape(1, -1),
                        gaq, gak, a_heads, min(256, t))
    o_a, o_b = _attention(q, k, v, bqk, bv, _band_bias_rows(rel_bias), batch, seq, a_heads)
    merged = _merge(o_a, o_b, w_o_a, w_o_b, gates, tm_big, tn)
    x1 = _out_proj(merged, w_out, xf, tm_big, tn)

    wr = _pad_cols(jnp.concatenate([w_group, w_expert], axis=1), LANE)
    xn2, route = _router(x1, g_ffn, wr, n_groups, per_group, min(256, t))
    rank, starts_f = _rank(route, min(512, t))
    dest = _dest(route, rank, starts_f, min(512, t))[:, :TOP_K].reshape(-1)
    items = _work_items(starts_f[0, :n_experts].astype(jnp.int32), TOP_K * t)
    y = _moe_experts(items, dest, xn2, w_exp_gate, w_exp_up, w_exp_down)
    out = _combine(dest, x1, route, y, min(256, t))
    return out.reshape(batch, seq, d)
```

```python
import functools

import jax
import jax.numpy as jnp
from jax import lax
from jax.experimental import pallas as pl
from jax.experimental.pallas import tpu as pltpu

F32 = jnp.float32
BF16 = jnp.bfloat16

CHUNK = 64
EPS = 1e-6
A_NOPE = 128
A_ROPE = 64
A_V = 128
A_QK = A_NOPE + A_ROPE
B_HEAD_DIM = 128
B_LEFT_CHUNKS = 8
B_MAX_REL = 128
ROPE_THETA = 10000.0
TOP_K = 2

LANE = 128
A_HEAD_PAD = 2 * LANE
V_PAD = 2 * LANE
V7X_VMEM_BYTES = 64 * 2**20

ATTN_TILE = 256
MOE_ROWS = 128
WEIGHT_DMA_PARTS = 4
WEIGHT_DMA_PRIORITY = 1
ROW_DMA_PRIORITY = 1
CONVERT_PARTS = 8
PROJ_ROW_GROUPS = 4
NEG_INF = float("-inf")


def _nbytes(shape, dtype):
    n = 1
    for s in shape:
        n *= s
    return n * jnp.dtype(dtype).itemsize


def _params(semantics, pipelined_bytes, resident_bytes=0):
    need = 2 * pipelined_bytes + resident_bytes
    return pltpu.CompilerParams(dimension_semantics=semantics,
                                vmem_limit_bytes=min(int(need), V7X_VMEM_BYTES))


def _tile(n, want):
    t = want
    while t > LANE and n % t:
        t //= 2
    assert n % t == 0, (n, want)
    return t


def _rmsnorm_kernel(x_ref, g_ref, o_ref):
    x = x_ref[...]
    r = lax.rsqrt(jnp.mean(x * x, axis=-1, keepdims=True) + EPS)
    o_ref[...] = (x * r * g_ref[...]).astype(o_ref.dtype)


def _rmsnorm_rows(x, g, out_dtype, tm):
    t, d = x.shape
    blocks = _nbytes((tm, d), F32) + _nbytes((tm, d), out_dtype) + _nbytes((1, d), F32)
    return pl.pallas_call(
        _rmsnorm_kernel,
        grid=(t // tm,),
        in_specs=[pl.BlockSpec((tm, d), lambda i: (i, 0)), pl.BlockSpec((1, d), lambda i: (0, 0))],
        out_specs=pl.BlockSpec((tm, d), lambda i: (i, 0)),
        out_shape=jax.ShapeDtypeStruct((t, d), out_dtype),
        compiler_params=_params(("parallel",), blocks, _nbytes((tm, d), F32)),
        name="rmsnorm",
    )(x, g.reshape(1, d))


def _nt_dot(a, b):
    return lax.dot_general(a, b, (((1,), (1,)), ((), ())), preferred_element_type=F32)


def _head_norm_epilogue(acc, gb_ref):
    heads = []
    for h in range(acc.shape[1] // B_HEAD_DIM):
        sl = slice(h * B_HEAD_DIM, (h + 1) * B_HEAD_DIM)
        z = acc[:, sl]
        r = lax.rsqrt(jnp.mean(z * z, axis=-1, keepdims=True) + EPS)
        heads.append(z * r * gb_ref[:, sl])
    return jnp.concatenate(heads, axis=1)


def _plain_epilogue(acc, gb_ref):
    del gb_ref
    return acc


def _sigmoid_epilogue(acc, gb_ref):
    return 1.0 / (1.0 + jnp.exp(-(acc + gb_ref[...])))


def _inproj_cols_kernel(a_ref, wt_hbm, *refs, first_col, epilogue):
    gb_ref = refs[0] if len(refs) == 5 else None
    o_ref, stage, wt_bf, sem = refs[-4:]
    j = pl.program_id(0)
    tn = wt_bf.shape[0]

    def fetch(block):
        rows = pl.ds(pl.multiple_of(first_col + block * tn, 8), tn)
        return pltpu.make_async_copy(wt_hbm.at[rows], stage, sem)

    @pl.when(pl.program_id(1) == 0)
    def _():
        @pl.when(j == 0)
        def _():
            fetch(j).start(priority=WEIGHT_DMA_PRIORITY)

        fetch(j).wait()
        wt_bf[...] = stage[...].astype(BF16)

        @pl.when(j + 1 < pl.num_programs(0))
        def _():
            fetch(j + 1).start(priority=WEIGHT_DMA_PRIORITY)

    rows = a_ref.shape[0] // PROJ_ROW_GROUPS
    groups = [slice(s * rows, (s + 1) * rows) for s in range(PROJ_ROW_GROUPS)]
    accs = [_nt_dot(a_ref[sl, :], wt_bf[...]) for sl in groups]
    for sl, acc in zip(groups, accs):
        o_ref[sl, :] = epilogue(acc, gb_ref).astype(o_ref.dtype)


def _inproj_cols(xn, w_in_t, first_col, n_cols, gb, epilogue, tm, tn):
    m, k = xn.shape
    assert n_cols % tn == 0 and first_col % 8 == 0 and tm % PROJ_ROW_GROUPS == 0
    blocks = _nbytes((tm, k), BF16) + _nbytes((tm, tn), BF16) + _nbytes((1, tn), F32)
    resident = _nbytes((tn, k), F32) + _nbytes((tn, k), BF16) + 3 * _nbytes((tm, tn), F32)
    gb_specs = [] if gb is None else [pl.BlockSpec((1, tn), lambda j, i: (0, j))]
    gb_args = [] if gb is None else [gb.reshape(1, n_cols)]
    return pl.pallas_call(
        functools.partial(_inproj_cols_kernel, first_col=first_col, epilogue=epilogue),
        grid=(n_cols // tn, m // tm),
        in_specs=[pl.BlockSpec((tm, k), lambda j, i: (i, 0)), pl.BlockSpec(memory_space=pl.ANY)] + gb_specs,
        out_specs=pl.BlockSpec((tm, tn), lambda j, i: (i, j)),
        out_shape=jax.ShapeDtypeStruct((m, n_cols), BF16),
        scratch_shapes=[pltpu.VMEM((tn, k), F32), pltpu.VMEM((tn, k), BF16), pltpu.SemaphoreType.DMA(())],
        compiler_params=_params(("arbitrary", "arbitrary"), blocks, resident),
        name="inproj_" + epilogue.__name__.strip("_").replace("_epilogue", ""),
    )(xn, w_in_t, *gb_args)


def _rope_padded(v, cos, sin_lo, sin_hi):
    half = A_ROPE // 2
    return v * cos + pltpu.roll(v, LANE - half, 1) * sin_lo + pltpu.roll(v, half, 1) * sin_hi


def _ones_column(rows):
    lane = lax.broadcasted_iota(jnp.int32, (rows, V_PAD - A_V), 1)
    return jnp.where(lane == 0, 1.0, 0.0).astype(BF16)


def _mla_proj_kernel(za_ref, cos_ref, sl_ref, sh_ref, wuq_ref, wukv_ref, gq_ref, gkv_ref, gaq_ref, gak_ref,
                     q_ref, k_ref, v_ref, *, heads, q_lora, kv_lora):
    cos, sin_lo, sin_hi = cos_ref[...], sl_ref[...], sh_ref[...]

    def norm(z, g):
        z = z.astype(F32)
        r = lax.rsqrt(jnp.mean(z * z, axis=-1, keepdims=True) + EPS)
        return (z * r * g).astype(BF16)

    cq = norm(za_ref[:, :q_lora], gq_ref[...])
    ckv = norm(za_ref[:, q_lora:q_lora + kv_lora], gkv_ref[...])
    slab = za_ref[:, q_lora + kv_lora:q_lora + kv_lora + LANE].astype(F32)
    k_rope = jnp.where(lax.broadcasted_iota(jnp.int32, slab.shape, 1) < A_ROPE, slab, 0.0)
    qacc = jnp.dot(cq, wuq_ref[...], preferred_element_type=F32)
    kvacc = jnp.dot(ckv, wukv_ref[...], preferred_element_type=F32)

    gq_lo, gq_hi = gaq_ref[:, :LANE], gaq_ref[:, LANE:]
    gk_lo, gk_hi = gak_ref[:, :LANE], gak_ref[:, LANE:]
    kr_ss = jnp.sum(k_rope * k_rope, axis=-1, keepdims=True)
    kr_rot = _rope_padded(k_rope * gk_hi, cos, sin_lo, sin_hi)
    ones_col = _ones_column(slab.shape[0])
    for h in range(heads):
        base = h * A_HEAD_PAD
        q_lo = qacc[:, base:base + LANE]
        q_hi = qacc[:, base + LANE:base + A_HEAD_PAD]
        ss = jnp.sum(q_lo * q_lo, axis=-1, keepdims=True) + jnp.sum(q_hi * q_hi, axis=-1, keepdims=True)
        r = lax.rsqrt(ss / A_QK + EPS)
        q_ref[:, base:base + LANE] = (q_lo * r * gq_lo).astype(BF16)
        q_ref[:, base + LANE:base + A_HEAD_PAD] = _rope_padded(q_hi * r * gq_hi, cos, sin_lo, sin_hi).astype(BF16)

        k_lo = kvacc[:, base:base + LANE]
        ssk = jnp.sum(k_lo * k_lo, axis=-1, keepdims=True) + kr_ss
        rk = lax.rsqrt(ssk / A_QK + EPS)
        k_ref[:, base:base + LANE] = (k_lo * rk * gk_lo).astype(BF16)
        k_ref[:, base + LANE:base + A_HEAD_PAD] = (kr_rot * rk).astype(BF16)
        v_ref[:, h * V_PAD:h * V_PAD + A_V] = kvacc[:, base + LANE:base + A_HEAD_PAD].astype(BF16)
        v_ref[:, h * V_PAD + A_V:(h + 1) * V_PAD] = ones_col


def _mla_proj(za, cos, sin_lo, sin_hi, wuq, wukv, gq, gkv, gaq, gak, heads, tm):
    t, za_cols = za.shape
    q_lora, kv_lora = wuq.shape[0], wukv.shape[0]
    hp = heads * A_HEAD_PAD
    row = lambda i: (i, 0)
    fix = lambda i: (0, 0)
    blocks = (_nbytes((tm, za_cols), za.dtype) + 3 * _nbytes((tm, LANE), F32) + _nbytes(wuq.shape, BF16)
              + _nbytes(wukv.shape, BF16) + 2 * _nbytes((tm, hp), BF16) + _nbytes((tm, heads * V_PAD), BF16))
    return pl.pallas_call(
        functools.partial(_mla_proj_kernel, heads=heads, q_lora=q_lora, kv_lora=kv_lora),
        grid=(t // tm,),
        in_specs=[pl.BlockSpec((tm, za_cols), row),
                  pl.BlockSpec((tm, LANE), row), pl.BlockSpec((tm, LANE), row), pl.BlockSpec((tm, LANE), row),
                  pl.BlockSpec(wuq.shape, fix), pl.BlockSpec(wukv.shape, fix),
                  pl.BlockSpec((1, q_lora), fix), pl.BlockSpec((1, kv_lora), fix),
                  pl.BlockSpec((1, A_HEAD_PAD), fix), pl.BlockSpec((1, A_HEAD_PAD), fix)],
        out_specs=[pl.BlockSpec((tm, hp), row), pl.BlockSpec((tm, hp), row), pl.BlockSpec((tm, heads * V_PAD), row)],
        out_shape=[jax.ShapeDtypeStruct((t, hp), BF16), jax.ShapeDtypeStruct((t, hp), BF16),
                   jax.ShapeDtypeStruct((t, heads * V_PAD), BF16)],
        compiler_params=_params(("parallel",), blocks, 3 * _nbytes((tm, hp), F32)),
        name="mla_proj",
    )(za, cos, sin_lo, sin_hi, wuq, wukv, gq, gkv, gaq, gak)


def _mla_scores(i, q_ref, k_ref, diag_ok):
    tq = ATTN_TILE
    q = q_ref[i * tq:(i + 1) * tq, :]
    sd = jnp.where(diag_ok, _nt_dot(q, k_ref[i * tq:(i + 1) * tq, :]), NEG_INF)
    s0 = _nt_dot(q, k_ref[:i * tq, :]) if i > 0 else None
    return sd, s0


def _mla_softmax(sd, s0):
    m = jnp.max(sd, axis=-1, keepdims=True)
    if s0 is None:
        return jnp.exp(sd - m).astype(BF16), None
    m = jnp.maximum(m, jnp.max(s0, axis=-1, keepdims=True))
    return jnp.exp(sd - m).astype(BF16), jnp.exp(s0 - m).astype(BF16)


def _mla_values(i, pd, p0, v_ref, o_ref):
    tq = ATTN_TILE
    o = jnp.dot(pd, v_ref[i * tq:(i + 1) * tq, :], preferred_element_type=F32)
    if p0 is not None:
        o = o + jnp.dot(p0, v_ref[:i * tq, :], preferred_element_type=F32)
    o_ref[i * tq:(i + 1) * tq, :] = (o[:, :A_V] / o[:, A_V:A_V + 1]).astype(o_ref.dtype)


def _band_window(i):
    left = B_LEFT_CHUNKS * CHUNK
    q0 = i * ATTN_TILE
    k0 = max(0, q0 - left)
    return q0, k0, q0 + ATTN_TILE - k0, left - q0 + k0


def _band_scores(i, q_ref, k_ref, table):
    q0, k0, kw, u0 = _band_window(i)
    return _nt_dot(q_ref[q0:q0 + ATTN_TILE, :], k_ref[k0:k0 + kw, :]) + table[:, u0:u0 + kw]


def _band_softmax(s):
    return jnp.exp(s - jnp.max(s, axis=-1, keepdims=True)).astype(BF16)


def _band_values(i, p, v_ref, o_ref):
    q0, k0, kw, _ = _band_window(i)
    o = jnp.dot(p, v_ref[k0:k0 + kw, :], preferred_element_type=F32)
    o_ref[q0:q0 + ATTN_TILE, :] = (o[:, :B_HEAD_DIM] / o[:, B_HEAD_DIM:B_HEAD_DIM + 1]).astype(o_ref.dtype)


def _attn_kernel(aq_ref, ak_ref, av_ref, bq_ref, bk_ref, bv_ref, r_ref, oa_ref, ob_ref, bv_pad, *, seq):
    tq = ATTN_TILE
    left = B_LEFT_CHUNKS * CHUNK
    width = r_ref.shape[1]
    rc = lax.broadcasted_iota(jnp.int32, (tq, tq), 0) // CHUNK
    cc = lax.broadcasted_iota(jnp.int32, (tq, tq), 1) // CHUNK
    diag_ok = cc <= rc
    bias = pltpu.roll(jnp.broadcast_to(r_ref[...], (tq, width)), 0, 1, stride=1, stride_axis=0)[:, :left + tq]
    q_chunk = lax.broadcasted_iota(jnp.int32, (tq, left + tq), 0) // CHUNK
    k_chunk = lax.broadcasted_iota(jnp.int32, (tq, left + tq), 1) // CHUNK
    table = jnp.where((k_chunk >= q_chunk) & (k_chunk <= q_chunk + B_LEFT_CHUNKS), bias, NEG_INF)
    bv_pad[:, :B_HEAD_DIM] = bv_ref[...]
    bv_pad[:, B_HEAD_DIM:] = _ones_column(seq)

    n_tiles = seq // tq
    sd, s0 = _mla_scores(0, aq_ref, ak_ref, diag_ok)
    for i in range(n_tiles):
        sb = _band_scores(i, bq_ref, bk_ref, table)
        pd, p0 = _mla_softmax(sd, s0)
        if i + 1 < n_tiles:
            sd, s0 = _mla_scores(i + 1, aq_ref, ak_ref, diag_ok)
        _mla_values(i, pd, p0, av_ref, oa_ref)
        _band_values(i, _band_softmax(sb), bv_pad, ob_ref)


def _band_bias_rows(rel_bias):
    left = B_LEFT_CHUNKS * CHUNK
    width = left + 2 * ATTN_TILE
    m = jnp.arange(width, dtype=jnp.int32)
    j = jnp.where(m < left + ATTN_TILE, m, m - width)
    dist = left - j
    rows = rel_bias[:, jnp.clip(dist, -B_MAX_REL, B_MAX_REL) + B_MAX_REL].astype(F32)
    return rows.reshape(rel_bias.shape[0], 1, width)


def _attention(q, k, v, bqk, bv, bias_rows, batch, seq, heads):
    t = batch * seq
    d = B_HEAD_DIM
    width = bias_rows.shape[2]
    head = lambda b, h: (b, h)
    blocks = (2 * _nbytes((seq, A_HEAD_PAD), BF16) + _nbytes((seq, V_PAD), BF16) + _nbytes((seq, A_V), BF16)
              + 4 * _nbytes((seq, d), BF16) + _nbytes((1, width), F32))
    temps = _nbytes((seq, V_PAD), BF16) + 8 * _nbytes((ATTN_TILE, seq), F32) + 8 * _nbytes((ATTN_TILE, width), F32)
    return pl.pallas_call(
        functools.partial(_attn_kernel, seq=seq),
        grid=(batch, heads),
        in_specs=[pl.BlockSpec((seq, A_HEAD_PAD), head), pl.BlockSpec((seq, A_HEAD_PAD), head),
                  pl.BlockSpec((seq, V_PAD), head),
                  pl.BlockSpec((seq, d), head),
                  pl.BlockSpec((seq, d), lambda b, h: (b, heads + h)),
                  pl.BlockSpec((seq, d), head),
                  pl.BlockSpec((None, 1, width), lambda b, h: (h, 0, 0))],
        out_specs=[pl.BlockSpec((seq, A_V), head), pl.BlockSpec((seq, d), head)],
        out_shape=[jax.ShapeDtypeStruct((t, heads * A_V), BF16), jax.ShapeDtypeStruct((t, heads * d), BF16)],
        scratch_shapes=[pltpu.VMEM((seq, V_PAD), BF16)],
        compiler_params=_params(("parallel", "parallel"), blocks, temps),
        name="attention",
    )(q, k, v, bqk, bqk, bv, bias_rows)


def _merge_kernel(oa_ref, ob_ref, woa_ref, wob_ref, g0_ref, g1_ref, o_ref, woa_bf, wob_bf):
    @pl.when(pl.program_id(1) == 0)
    def _():
        woa_bf[...] = woa_ref[...].astype(BF16)
        wob_bf[...] = wob_ref[...].astype(BF16)

    a = jnp.dot(oa_ref[...], woa_bf[...], preferred_element_type=F32)
    b = jnp.dot(ob_ref[...], wob_bf[...], preferred_element_type=F32)
    o_ref[...] = (g0_ref[...].astype(F32) * a + g1_ref[...].astype(F32) * b).astype(o_ref.dtype)


def _merge(oa, ob, woa, wob, gates, tm, tn):
    m = oa.shape[0]
    d = woa.shape[1]
    g1 = d // tn
    blocks = (_nbytes((tm, oa.shape[1]), BF16) + _nbytes((tm, ob.shape[1]), BF16) + _nbytes((woa.shape[0], tn), F32)
              + _nbytes((wob.shape[0], tn), F32) + 3 * _nbytes((tm, tn), BF16))
    resident = (2 * _nbytes((woa.shape[0], tn), BF16) + 2 * _nbytes((wob.shape[0], tn), BF16)
                + 3 * _nbytes((tm, tn), F32))
    return pl.pallas_call(
        _merge_kernel,
        grid=(d // tn, m // tm),
        in_specs=[pl.BlockSpec((tm, oa.shape[1]), lambda j, i: (i, 0)),
                  pl.BlockSpec((tm, ob.shape[1]), lambda j, i: (i, 0)),
                  pl.BlockSpec((woa.shape[0], tn), lambda j, i: (0, j)),
                  pl.BlockSpec((wob.shape[0], tn), lambda j, i: (0, j)),
                  pl.BlockSpec((tm, tn), lambda j, i: (i, j)),
                  pl.BlockSpec((tm, tn), lambda j, i: (i, g1 + j))],
        out_specs=pl.BlockSpec((tm, tn), lambda j, i: (i, j)),
        out_shape=jax.ShapeDtypeStruct((m, d), BF16),
        scratch_shapes=[pltpu.VMEM((woa.shape[0], tn), BF16), pltpu.VMEM((wob.shape[0], tn), BF16)],
        compiler_params=_params(("parallel", "arbitrary"), blocks, resident),
        name="merge",
    )(oa, ob, woa, wob, gates, gates)


def _out_proj_kernel(a_ref, w_ref, x_ref, o_ref, w_bf):
    @pl.when(pl.program_id(1) == 0)
    def _():
        w_bf[...] = w_ref[...].astype(BF16)

    o_ref[...] = x_ref[...] + jnp.dot(a_ref[...], w_bf[...], preferred_element_type=F32)


def _out_proj(merged, wout, x, tm, tn):
    m, k = merged.shape
    n = wout.shape[1]
    blocks = _nbytes((tm, k), BF16) + _nbytes((k, tn), F32) + 2 * _nbytes((tm, tn), F32)
    return pl.pallas_call(
        _out_proj_kernel,
        grid=(n // tn, m // tm),
        in_specs=[pl.BlockSpec((tm, k), lambda j, i: (i, 0)),
                  pl.BlockSpec((k, tn), lambda j, i: (0, j)),
                  pl.BlockSpec((tm, tn), lambda j, i: (i, j))],
        out_specs=pl.BlockSpec((tm, tn), lambda j, i: (i, j)),
        out_shape=jax.ShapeDtypeStruct((m, n), F32),
        scratch_shapes=[pltpu.VMEM((k, tn), BF16)],
        compiler_params=_params(("parallel", "arbitrary"), blocks, 2 * _nbytes((k, tn), BF16) + _nbytes((tm, tn), F32)),
        name="out_proj",
    )(merged, wout, x)


def _split_bf16(v):
    hi = v.astype(BF16)
    return hi, (v - hi.astype(F32)).astype(BF16)


def _router_kernel(x_ref, g_ref, wr_ref, xn_ref, route_ref, *, n_groups, per_group):
    x = x_ref[...]
    r = lax.rsqrt(jnp.mean(x * x, axis=-1, keepdims=True) + EPS)
    xn = x * r * g_ref[...]
    xn_ref[...] = xn
    x_hi, x_lo = _split_bf16(xn)
    w_hi, w_lo = _split_bf16(wr_ref[...])
    dot = functools.partial(jnp.dot, preferred_element_type=F32)
    logits = dot(x_hi, w_hi) + (dot(x_lo, w_hi) + dot(x_hi, w_lo))
    lane = lax.broadcasted_iota(jnp.int32, logits.shape, 1).astype(F32)
    far = float(LANE)

    def top(vals):
        best = jnp.max(vals, axis=-1, keepdims=True)
        return best, jnp.min(jnp.where(vals == best, lane, far), axis=-1, keepdims=True)

    gl = jnp.where(lane < n_groups, logits, NEG_INF)
    gmax, grp = top(gl)
    p_grp = 1.0 / jnp.sum(jnp.exp(gl - gmax), axis=-1, keepdims=True)
    lo = n_groups + grp * per_group
    el = jnp.where((lane >= lo) & (lane < lo + per_group), logits, NEG_INF)
    t1, i1 = top(el)
    t2, i2 = top(jnp.where(lane == i1, NEG_INF, el))
    d = jnp.exp(t2 - t1)
    w1 = p_grp / (1.0 + d)
    w2 = p_grp * d / (1.0 + d)
    route_ref[...] = jnp.where(lane == 0, i1 - n_groups,
                               jnp.where(lane == 1, i2 - n_groups,
                                         jnp.where(lane == 2, w1, jnp.where(lane == 3, w2, 0.0))))


def _router(x1, g, wr, n_groups, per_group, tm):
    t, d = x1.shape
    blocks = 2 * _nbytes((tm, d), F32) + _nbytes((1, d), F32) + _nbytes((d, LANE), F32) + _nbytes((tm, LANE), F32)
    return pl.pallas_call(
        functools.partial(_router_kernel, n_groups=n_groups, per_group=per_group),
        grid=(t // tm,),
        in_specs=[pl.BlockSpec((tm, d), lambda i: (i, 0)), pl.BlockSpec((1, d), lambda i: (0, 0)),
                  pl.BlockSpec((d, LANE), lambda i: (0, 0))],
        out_specs=[pl.BlockSpec((tm, d), lambda i: (i, 0)), pl.BlockSpec((tm, LANE), lambda i: (i, 0))],
        out_shape=[jax.ShapeDtypeStruct((t, d), F32), jax.ShapeDtypeStruct((t, LANE), F32)],
        compiler_params=_params(("parallel",), blocks, 2 * _nbytes((tm, d), F32)),
        name="router",
    )(x1, g.reshape(1, d), wr)


def _one_hots(route):
    lane = lax.broadcasted_iota(jnp.int32, route.shape, 1).astype(F32)
    return (lane == route[:, 0:1]).astype(F32), (lane == route[:, 1:2]).astype(F32)


def _rank_kernel(route_ref, rank_ref, starts_ref, count_acc, start_acc):
    i = pl.program_id(0)

    @pl.when(i == 0)
    def _():
        count_acc[...] = jnp.zeros_like(count_acc)
        start_acc[...] = jnp.zeros_like(start_acc)

    oh1, oh2 = _one_hots(route_ref[...])
    oh = (oh1 + oh2).astype(BF16)
    tm = oh.shape[0]
    earlier = (lax.broadcasted_iota(jnp.int32, (tm, tm), 0) > lax.broadcasted_iota(jnp.int32, (tm, tm), 1))
    before = jnp.dot(earlier.astype(BF16), oh, preferred_element_type=F32) + count_acc[...]
    lane = lax.broadcasted_iota(jnp.int32, (tm, LANE), 1)
    rank_ref[...] = jnp.where(lane == 0, jnp.sum(oh1 * before, axis=-1, keepdims=True),
                              jnp.where(lane == 1, jnp.sum(oh2 * before, axis=-1, keepdims=True), 0.0))
    lower = (lax.broadcasted_iota(jnp.int32, (LANE, LANE), 0) < lax.broadcasted_iota(jnp.int32, (LANE, LANE), 1))
    below = jnp.dot(oh, lower.astype(BF16), preferred_element_type=F32)
    count_acc[...] += jnp.sum(oh.astype(F32), axis=0, keepdims=True)
    start_acc[...] += jnp.sum(below, axis=0, keepdims=True)
    starts_ref[...] = start_acc[...]


def _rank(route, tm):
    t = route.shape[0]
    blocks = 2 * _nbytes((tm, LANE), F32) + _nbytes((1, LANE), F32)
    return pl.pallas_call(
        _rank_kernel,
        grid=(t // tm,),
        in_specs=[pl.BlockSpec((tm, LANE), lambda i: (i, 0))],
        out_specs=[pl.BlockSpec((tm, LANE), lambda i: (i, 0)), pl.BlockSpec((1, LANE), lambda i: (0, 0))],
        out_shape=[jax.ShapeDtypeStruct((t, LANE), F32), jax.ShapeDtypeStruct((1, LANE), F32)],
        scratch_shapes=[pltpu.VMEM((1, LANE), F32), pltpu.VMEM((1, LANE), F32)],
        compiler_params=_params(("arbitrary",), blocks, 2 * _nbytes((tm, tm), F32)),
        name="moe_rank",
    )(route)


def _dest_kernel(route_ref, rank_ref, starts_ref, dest_ref):
    oh1, oh2 = _one_hots(route_ref[...])
    rank = rank_ref[...]
    starts = starts_ref[...]
    d1 = jnp.sum(oh1 * starts, axis=-1, keepdims=True) + rank[:, 0:1]
    d2 = jnp.sum(oh2 * starts, axis=-1, keepdims=True) + rank[:, 1:2]
    lane = lax.broadcasted_iota(jnp.int32, rank.shape, 1)
    dest_ref[...] = jnp.where(lane == 0, d1, jnp.where(lane == 1, d2, 0.0)).astype(jnp.int32)


def _dest(route, rank, starts, tm):
    t = route.shape[0]
    blocks = 3 * _nbytes((tm, LANE), F32) + _nbytes((1, LANE), F32)
    return pl.pallas_call(
        _dest_kernel,
        grid=(t // tm,),
        in_specs=[pl.BlockSpec((tm, LANE), lambda i: (i, 0)), pl.BlockSpec((tm, LANE), lambda i: (i, 0)),
                  pl.BlockSpec((1, LANE), lambda i: (0, 0))],
        out_specs=pl.BlockSpec((tm, LANE), lambda i: (i, 0)),
        out_shape=jax.ShapeDtypeStruct((t, LANE), jnp.int32),
        compiler_params=_params(("parallel",), blocks, 4 * _nbytes((tm, LANE), F32)),
        name="moe_dest",
    )(route, rank, starts)


def _work_items(starts, n_rows):
    n_exp = starts.shape[0]
    n_blk = n_rows // MOE_ROWS
    total = jnp.full((1,), n_rows, jnp.int32)
    pts = jnp.concatenate([jnp.arange(n_blk, dtype=jnp.int32) * MOE_ROWS, starts[1:]])
    idx = jnp.arange(pts.shape[0], dtype=jnp.int32)
    before = (pts[None, :] < pts[:, None]) | ((pts[None, :] == pts[:, None]) & (idx[None, :] < idx[:, None]))
    pos = jnp.sum(before.astype(jnp.int32), axis=1)
    lo = jnp.sum(jnp.where(pos[:, None] == idx[None, :], pts[:, None], 0), axis=0)
    hi = jnp.concatenate([lo[1:], total])
    ends = jnp.concatenate([starts[1:], total])
    r = jnp.minimum(lo // MOE_ROWS, n_blk - 1)
    e = jnp.minimum(jnp.sum((ends[None, :] <= lo[:, None]).astype(jnp.int32), axis=1), n_exp - 1)
    later = jnp.where(e[None, :] > e[:, None], e[None, :], n_exp)
    nxt = jnp.min(later, axis=1)
    nxt = jnp.where(nxt == n_exp, -1, nxt)
    return r, e, lo, hi, nxt


def _stream_expert_weights(w, e_ref, nxt_ref, streams):
    e = e_ref[w]

    def copies(stream, expert):
        hbm, stage, _, sem = stream
        rows = stage.shape[0] // WEIGHT_DMA_PARTS
        parts = [pl.ds(part * rows, rows) for part in range(WEIGHT_DMA_PARTS)]
        return [pltpu.make_async_copy(hbm.at[expert, sl], stage.at[sl], sem) for sl in parts]

    @pl.when(w == 0)
    def _():
        for stream in streams:
            for c in copies(stream, e):
                c.start(priority=WEIGHT_DMA_PRIORITY)

    @pl.when((w == 0) | (e != e_ref[jnp.maximum(w - 1, 0)]))
    def _():
        nxt = nxt_ref[w]
        for stream in streams:
            for c in copies(stream, e):
                c.wait()
            _, stage, w_bf, _ = stream
            rows = stage.shape[0] // CONVERT_PARTS
            for part in range(CONVERT_PARTS):
                sl = slice(part * rows, (part + 1) * rows)
                w_bf[sl, :] = stage[sl, :].astype(BF16)

            @pl.when(nxt >= 0)
            def _():
                for c in copies(stream, nxt):
                    c.start(priority=WEIGHT_DMA_PRIORITY)


def _store_item_rows(o_ref, val, r, lo, hi):
    rows = r * MOE_ROWS + lax.broadcasted_iota(jnp.int32, val.shape, 0)
    mine = (rows >= lo) & (rows < hi)

    @pl.when(lo == r * MOE_ROWS)
    def _():
        o_ref[...] = val

    @pl.when(lo != r * MOE_ROWS)
    def _():
        pltpu.store(o_ref, val, mask=mine)


def _gather_block_rows(w, r_ref, dest_ref, x_hbm, source, rows_f32, x_bf, sem, *, n_assign):
    r = r_ref[w]
    n_blocks = n_assign // MOE_ROWS

    def copy(src_row, slot, j):
        return pltpu.make_async_copy(x_hbm.at[pl.ds(src_row, 1)], rows_f32.at[slot, pl.ds(j, 1)], sem.at[slot])

    def start_block(block, slot):
        for j in range(MOE_ROWS):
            copy(source[block * MOE_ROWS + j], slot, j).start()

    def finish_block(slot):
        for j in range(MOE_ROWS):
            copy(0, slot, j).wait()
        x_bf[...] = rows_f32[slot].astype(BF16)

    @pl.when(w == 0)
    def _():
        def invert(tok, carry):
            for k in range(TOP_K):
                source[dest_ref[TOP_K * tok + k]] = tok
            return carry
        lax.fori_loop(0, n_assign // TOP_K, invert, 0, unroll=4)
        start_block(r, 0)

    @pl.when((w == 0) | (r != r_ref[jnp.maximum(w - 1, 0)]))
    def _():
        for slot in range(2):
            @pl.when(r % 2 == slot)
            def _():
                finish_block(slot)

                @pl.when(r + 1 < n_blocks)
                def _():
                    start_block(r + 1, 1 - slot)


def _moe_experts_kernel(r_ref, e_ref, lo_ref, hi_ref, nxt_ref, dest_ref, x_hbm, wg_hbm, wu_hbm, wd_hbm, y_ref,
                        wg_stage, wu_stage, wd_stage, wg_bf, wu_bf, wd_bf, source, rows_f32, x_bf, sem, row_sem,
                        *, n_assign):
    w = pl.program_id(0)
    _stream_expert_weights(w, e_ref, nxt_ref, [(wg_hbm, wg_stage, wg_bf, sem.at[0]),
                                                 (wu_hbm, wu_stage, wu_bf, sem.at[1]),
                                                 (wd_hbm, wd_stage, wd_bf, sem.at[2])])
    _gather_block_rows(w, r_ref, dest_ref, x_hbm, source, rows_f32, x_bf, row_sem, n_assign=n_assign)
    r, lo, hi = r_ref[w], lo_ref[w], hi_ref[w]

    @pl.when(hi > lo)
    def _():
        x = x_bf[...]
        g = jnp.dot(x, wg_bf[...], preferred_element_type=F32)
        u = jnp.dot(x, wu_bf[...], preferred_element_type=F32)
        h = ((g * (1.0 / (1.0 + jnp.exp(-g)))) * u).astype(BF16)
        y = jnp.dot(h, wd_bf[...], preferred_element_type=F32)
        _store_item_rows(y_ref, y, r, lo, hi)


def _moe_experts(items, dest_flat, xn, wg, wu, wd):
    n_assign = dest_flat.shape[0]
    d = xn.shape[1]
    f = wg.shape[2]
    n_items = items[0].shape[0]
    blocks = _nbytes((MOE_ROWS, d), F32)
    resident = (3 * _nbytes((d, f), F32) + 3 * _nbytes((d, f), BF16) + 5 * _nbytes((MOE_ROWS, d), F32)
                + _nbytes((MOE_ROWS, d), BF16) + 6 * _nbytes((MOE_ROWS, f), F32))
    any_space = pl.BlockSpec(memory_space=pl.ANY)
    return pl.pallas_call(
        functools.partial(_moe_experts_kernel, n_assign=n_assign),
        grid_spec=pltpu.PrefetchScalarGridSpec(
            num_scalar_prefetch=6,
            grid=(n_items,),
            in_specs=[any_space, any_space, any_space, any_space],
            out_specs=pl.BlockSpec((MOE_ROWS, d), lambda w, r, e, lo, hi, nxt, dest: (r[w], 0)),
            scratch_shapes=[pltpu.VMEM((d, f), F32), pltpu.VMEM((d, f), F32), pltpu.VMEM((f, d), F32),
                            pltpu.VMEM((d, f), BF16), pltpu.VMEM((d, f), BF16), pltpu.VMEM((f, d), BF16),
                            pltpu.SMEM((n_assign,), jnp.int32), pltpu.VMEM((2, MOE_ROWS, d), F32),
                            pltpu.VMEM((MOE_ROWS, d), BF16),
                            pltpu.SemaphoreType.DMA((3,)), pltpu.SemaphoreType.DMA((2,))]),
        out_shape=jax.ShapeDtypeStruct((n_assign, d), F32),
        compiler_params=_params(("arbitrary",), blocks, resident),
        name="moe_experts",
    )(*items, dest_flat, xn, wg, wu, wd)


def _combine_kernel(dest_ref, x_ref, route_ref, y_hbm, o_ref, ybuf, sem, *, tm):
    half = tm // 2
    i = pl.program_id(0)

    def copy(src_row, slot, k, t):
        return pltpu.make_async_copy(y_hbm.at[pl.ds(src_row, 1)], ybuf.at[slot, k, pl.ds(t, 1)], sem.at[slot])

    def start_half(index, slot):
        for t in range(half):
            for k in range(TOP_K):
                copy(dest_ref[TOP_K * (index * half + t) + k], slot, k, t).start(priority=ROW_DMA_PRIORITY)

    def finish_half(slot):
        for t in range(half):
            for k in range(TOP_K):
                copy(0, slot, k, t).wait()
        rows = slice(slot * half, (slot + 1) * half)
        route = route_ref[rows, :]
        o_ref[rows, :] = x_ref[rows, :] + (route[:, 2:3] * ybuf[slot, 0] + route[:, 3:4] * ybuf[slot, 1])

    @pl.when(i == 0)
    def _():
        start_half(2 * i, 0)

    start_half(2 * i + 1, 1)
    finish_half(0)

    @pl.when(i + 1 < pl.num_programs(0))
    def _():
        start_half(2 * i + 2, 0)

    finish_half(1)


def _combine(dest_flat, x1, route, y, tm):
    t, d = x1.shape
    blocks = 2 * _nbytes((tm, d), F32) + _nbytes((tm, LANE), F32)
    resident = TOP_K * _nbytes((tm, d), F32) + _nbytes((tm, d), F32)
    return pl.pallas_call(
        functools.partial(_combine_kernel, tm=tm),
        grid_spec=pltpu.PrefetchScalarGridSpec(
            num_scalar_prefetch=1,
            grid=(t // tm,),
            in_specs=[pl.BlockSpec((tm, d), lambda i, dest: (i, 0)),
                      pl.BlockSpec((tm, LANE), lambda i, dest: (i, 0)),
                      pl.BlockSpec(memory_space=pl.ANY)],
            out_specs=pl.BlockSpec((tm, d), lambda i, dest: (i, 0)),
            scratch_shapes=[pltpu.VMEM((2, TOP_K, tm // 2, d), F32), pltpu.SemaphoreType.DMA((2,))]),
        out_shape=jax.ShapeDtypeStruct((t, d), F32),
        compiler_params=_params(("arbitrary",), blocks, resident),
        name="moe_combine",
    )(dest_flat, x1, route, y)


def _pad_cols(w, n):
    return jnp.pad(w, ((0, 0), (0, n - w.shape[1])))


def kernel(x, positions, g_mix, w_in, b_gate, q_norm_g, kv_norm_g, w_uq, w_ukv, a_q_norm_g, a_k_norm_g,
           b_q_norm_g, b_k_norm_g, rel_bias, w_o_a, w_o_b, w_out, g_ffn, w_group, w_expert,
           w_exp_gate, w_exp_up, w_exp_down):
    batch, seq, d = x.shape
    t = batch * seq
    q_lora, kv_lora = q_norm_g.shape[0], kv_norm_g.shape[0]
    a_heads = w_uq.shape[1] // A_QK
    b_heads = w_o_b.shape[0] // B_HEAD_DIM
    b_width = b_heads * B_HEAD_DIM
    n_groups, n_experts = w_group.shape[1], w_expert.shape[1]
    per_group = n_experts // n_groups
    off_b = q_lora + kv_lora + A_ROPE
    assert seq % ATTN_TILE == 0 and (TOP_K * t) % MOE_ROWS == 0 and n_groups + n_experts <= LANE
    assert a_heads == b_heads

    xf = x.reshape(t, d)
    tm_big = min(1024, t)
    tn = _tile(b_width, 512)
    tn_wide = _tile(b_width, 1024)
    assert d % tn == 0 and d % tn_wide == 0

    za_cols = -(-(q_lora + kv_lora + LANE) // tn_wide) * tn_wide
    wuq = jnp.pad(w_uq.reshape(q_lora, a_heads, A_QK), ((0, 0), (0, 0), (0, A_HEAD_PAD - A_QK)))
    wuq = wuq.reshape(q_lora, a_heads * A_HEAD_PAD).astype(BF16)
    wukv = w_ukv.astype(BF16)
    pad_gain = lambda g, s: jnp.pad(g * s, (0, A_HEAD_PAD - A_QK)).reshape(1, A_HEAD_PAD)
    gaq = pad_gain(a_q_norm_g, A_QK ** -0.5)
    gak = pad_gain(a_k_norm_g, 1.0)
    g_bqk = jnp.concatenate([jnp.tile(b_q_norm_g * B_HEAD_DIM ** -0.5, b_heads), jnp.tile(b_k_norm_g, b_heads)])

    half = A_ROPE // 2
    inv = ROPE_THETA ** (-jnp.arange(half, dtype=F32) / half)
    ang = positions.astype(F32).reshape(t, 1) * inv
    cos, sin = jnp.cos(ang), jnp.sin(ang)
    zeros = jnp.zeros((t, half), F32)
    cos_t = jnp.concatenate([cos, cos, zeros, zeros], axis=1)
    sin_lo = jnp.concatenate([-sin, zeros, zeros, zeros], axis=1)
    sin_hi = jnp.concatenate([zeros, sin, zeros, zeros], axis=1)

    xn = _rmsnorm_rows(xf, g_mix, BF16, min(256, t))
    w_in_t = w_in.T
    za = _inproj_cols(xn, w_in_t, 0, za_cols, None, _plain_epilogue, tm_big, tn_wide)
    bqk = _inproj_cols(xn, w_in_t, off_b, 2 * b_width, g_bqk, _head_norm_epilogue, tm_big, tn_wide)
    bv = _inproj_cols(xn, w_in_t, off_b + 2 * b_width, b_width, None, _plain_epilogue, tm_big, tn_wide)
    gates = _inproj_cols(xn, w_in_t, off_b + 3 * b_width, 2 * d, b_gate, _sigmoid_epilogue, tm_big, tn_wide)
    q, k, v = _mla_proj(za, cos_t, sin_lo, sin_hi, wuq, wukv, q_norm_g.reshape(1, -1), kv_norm_g.reshape(1, -1),
                        gaq, gak, a_heads, min(256, t))
    o_a, o_b = _attention(q, k, v, bqk, bv, _band_bias_rows(rel_bias), batch, seq, a_heads)
    merged = _merge(o_a, o_b, w_o_a, w_o_b, gates, tm_big, tn)
    x1 = _out_proj(merged, w_out, xf, tm_big, tn)

    wr = _pad_cols(jnp.concatenate([w_group, w_expert], axis=1), LANE)
    xn2, route = _router(x1, g_ffn, wr, n_groups, per_group, min(256, t))
    rank, starts_f = _rank(route, min(512, t))
    dest = _dest(route, rank, starts_f, min(512, t))[:, :TOP_K].reshape(-1)
    items = _work_items(starts_f[0, :n_experts].astype(jnp.int32), TOP_K * t)
    y = _moe_experts(items, dest, xn2, w_exp_gate, w_exp_up, w_exp_down)
    out = _combine(dest, x1, route, y, min(256, t))
    return out.reshape(batch, seq, d)
```

```python
import functools

import jax
import jax.numpy as jnp
from jax import lax
from jax.experimental import pallas as pl
from jax.experimental.pallas import tpu as pltpu

F32 = jnp.float32
BF16 = jnp.bfloat16

CHUNK = 64
EPS = 1e-6
A_NOPE = 128
A_ROPE = 64
A_V = 128
A_QK = A_NOPE + A_ROPE
B_HEAD_DIM = 128
B_LEFT_CHUNKS = 8
B_MAX_REL = 128
ROPE_THETA = 10000.0
TOP_K = 2

LANE = 128
A_HEAD_PAD = 2 * LANE
V_PAD = 2 * LANE
V7X_VMEM_BYTES = 64 * 2**20

ATTN_TILE = 256
MOE_ROWS = 128
WEIGHT_DMA_PARTS = 4
WEIGHT_DMA_PRIORITY = 1
ROW_DMA_PRIORITY = 1
CONVERT_PARTS = 8
PROJ_ROW_GROUPS = 4
NEG_INF = float("-inf")


def _nbytes(shape, dtype):
    n = 1
    for s in shape:
        n *= s
    return n * jnp.dtype(dtype).itemsize


def _params(semantics, pipelined_bytes, resident_bytes=0):
    need = 2 * pipelined_bytes + resident_bytes
    return pltpu.CompilerParams(dimension_semantics=semantics,
                                vmem_limit_bytes=min(int(need), V7X_VMEM_BYTES))


def _tile(n, want):
    t = want
    while t > LANE and n % t:
        t //= 2
    assert n % t == 0, (n, want)
    return t


def _rmsnorm_kernel(x_ref, g_ref, o_ref):
    x = x_ref[...]
    r = lax.rsqrt(jnp.mean(x * x, axis=-1, keepdims=True) + EPS)
    o_ref[...] = (x * r * g_ref[...]).astype(o_ref.dtype)


def _rmsnorm_rows(x, g, out_dtype, tm):
    t, d = x.shape
    blocks = _nbytes((tm, d), F32) + _nbytes((tm, d), out_dtype) + _nbytes((1, d), F32)
    return pl.pallas_call(
        _rmsnorm_kernel,
        grid=(t // tm,),
        in_specs=[pl.BlockSpec((tm, d), lambda i: (i, 0)), pl.BlockSpec((1, d), lambda i: (0, 0))],
        out_specs=pl.BlockSpec((tm, d), lambda i: (i, 0)),
        out_shape=jax.ShapeDtypeStruct((t, d), out_dtype),
        compiler_params=_params(("parallel",), blocks, _nbytes((tm, d), F32)),
        name="rmsnorm",
    )(x, g.reshape(1, d))


def _nt_dot(a, b):
    return lax.dot_general(a, b, (((1,), (1,)), ((), ())), preferred_element_type=F32)


def _head_norm_epilogue(acc, gb_ref):
    heads = []
    for h in range(acc.shape[1] // B_HEAD_DIM):
        sl = slice(h * B_HEAD_DIM, (h + 1) * B_HEAD_DIM)
        z = acc[:, sl]
        r = lax.rsqrt(jnp.mean(z * z, axis=-1, keepdims=True) + EPS)
        heads.append(z * r * gb_ref[:, sl])
    return jnp.concatenate(heads, axis=1)


def _plain_epilogue(acc, gb_ref):
    del gb_ref
    return acc


def _sigmoid_epilogue(acc, gb_ref):
    return 1.0 / (1.0 + jnp.exp(-(acc + gb_ref[...])))


def _inproj_cols_kernel(a_ref, wt_hbm, *refs, first_col, epilogue):
    gb_ref = refs[0] if len(refs) == 5 else None
    o_ref, stage, wt_bf, sem = refs[-4:]
    j = pl.program_id(0)
    tn = wt_bf.shape[0]

    def fetch(block):
        rows = pl.ds(pl.multiple_of(first_col + block * tn, 8), tn)
        return pltpu.make_async_copy(wt_hbm.at[rows], stage, sem)

    @pl.when(pl.program_id(1) == 0)
    def _():
        @pl.when(j == 0)
        def _():
            fetch(j).start(priority=WEIGHT_DMA_PRIORITY)

        fetch(j).wait()
        wt_bf[...] = stage[...].astype(BF16)

        @pl.when(j + 1 < pl.num_programs(0))
        def _():
            fetch(j + 1).start(priority=WEIGHT_DMA_PRIORITY)

    rows = a_ref.shape[0] // PROJ_ROW_GROUPS
    groups = [slice(s * rows, (s + 1) * rows) for s in range(PROJ_ROW_GROUPS)]
    accs = [_nt_dot(a_ref[sl, :], wt_bf[...]) for sl in groups]
    for sl, acc in zip(groups, accs):
        o_ref[sl, :] = epilogue(acc, gb_ref).astype(o_ref.dtype)


def _inproj_cols(xn, w_in_t, first_col, n_cols, gb, epilogue, tm, tn):
    m, k = xn.shape
    assert n_cols % tn == 0 and first_col % 8 == 0 and tm % PROJ_ROW_GROUPS == 0
    blocks = _nbytes((tm, k), BF16) + _nbytes((tm, tn), BF16) + _nbytes((1, tn), F32)
    resident = _nbytes((tn, k), F32) + _nbytes((tn, k), BF16) + 3 * _nbytes((tm, tn), F32)
    gb_specs = [] if gb is None else [pl.BlockSpec((1, tn), lambda j, i: (0, j))]
    gb_args = [] if gb is None else [gb.reshape(1, n_cols)]
    return pl.pallas_call(
        functools.partial(_inproj_cols_kernel, first_col=first_col, epilogue=epilogue),
        grid=(n_cols // tn, m // tm),
        in_specs=[pl.BlockSpec((tm, k), lambda j, i: (i, 0)), pl.BlockSpec(memory_space=pl.ANY)] + gb_specs,
        out_specs=pl.BlockSpec((tm, tn), lambda j, i: (i, j)),
        out_shape=jax.ShapeDtypeStruct((m, n_cols), BF16),
        scratch_shapes=[pltpu.VMEM((tn, k), F32), pltpu.VMEM((tn, k), BF16), pltpu.SemaphoreType.DMA(())],
        compiler_params=_params(("arbitrary", "arbitrary"), blocks, resident),
        name="inproj_" + epilogue.__name__.strip("_").replace("_epilogue", ""),
    )(xn, w_in_t, *gb_args)


def _rope_padded(v, cos, sin_lo, sin_hi):
    half = A_ROPE // 2
    return v * cos + pltpu.roll(v, LANE - half, 1) * sin_lo + pltpu.roll(v, half, 1) * sin_hi


def _ones_column(rows):
    lane = lax.broadcasted_iota(jnp.int32, (rows, V_PAD - A_V), 1)
    return jnp.where(lane == 0, 1.0, 0.0).astype(BF16)


def _mla_proj_kernel(za_ref, cos_ref, sl_ref, sh_ref, wuq_ref, wukv_ref, gq_ref, gkv_ref, gaq_ref, gak_ref,
                     q_ref, k_ref, v_ref, *, heads, q_lora, kv_lora):
    cos, sin_lo, sin_hi = cos_ref[...], sl_ref[...], sh_ref[...]

    def norm(z, g):
        z = z.astype(F32)
        r = lax.rsqrt(jnp.mean(z * z, axis=-1, keepdims=True) + EPS)
        return (z * r * g).astype(BF16)

    cq = norm(za_ref[:, :q_lora], gq_ref[...])
    ckv = norm(za_ref[:, q_lora:q_lora + kv_lora], gkv_ref[...])
    slab = za_ref[:, q_lora + kv_lora:q_lora + kv_lora + LANE].astype(F32)
    k_rope = jnp.where(lax.broadcasted_iota(jnp.int32, slab.shape, 1) < A_ROPE, slab, 0.0)
    qacc = jnp.dot(cq, wuq_ref[...], preferred_element_type=F32)
    kvacc = jnp.dot(ckv, wukv_ref[...], preferred_element_type=F32)

    gq_lo, gq_hi = gaq_ref[:, :LANE], gaq_ref[:, LANE:]
    gk_lo, gk_hi = gak_ref[:, :LANE], gak_ref[:, LANE:]
    kr_ss = jnp.sum(k_rope * k_rope, axis=-1, keepdims=True)
    kr_rot = _rope_padded(k_rope * gk_hi, cos, sin_lo, sin_hi)
    ones_col = _ones_column(slab.shape[0])
    for h in range(heads):
        base = h * A_HEAD_PAD
        q_lo = qacc[:, base:base + LANE]
        q_hi = qacc[:, base + LANE:base + A_HEAD_PAD]
        ss = jnp.sum(q_lo * q_lo, axis=-1, keepdims=True) + jnp.sum(q_hi * q_hi, axis=-1, keepdims=True)
        r = lax.rsqrt(ss / A_QK + EPS)
        q_ref[:, base:base + LANE] = (q_lo * r * gq_lo).astype(BF16)
        q_ref[:, base + LANE:base + A_HEAD_PAD] = _rope_padded(q_hi * r * gq_hi, cos, sin_lo, sin_hi).astype(BF16)

        k_lo = kvacc[:, base:base + LANE]
        ssk = jnp.sum(k_lo * k_lo, axis=-1, keepdims=True) + kr_ss
        rk = lax.rsqrt(ssk / A_QK + EPS)
        k_ref[:, base:base + LANE] = (k_lo * rk * gk_lo).astype(BF16)
        k_ref[:, base + LANE:base + A_HEAD_PAD] = (kr_rot * rk).astype(BF16)
        v_ref[:, h * V_PAD:h * V_PAD + A_V] = kvacc[:, base + LANE:base + A_HEAD_PAD].astype(BF16)
        v_ref[:, h * V_PAD + A_V:(h + 1) * V_PAD] = ones_col


def _mla_proj(za, cos, sin_lo, sin_hi, wuq, wukv, gq, gkv, gaq, gak, heads, tm):
    t, za_cols = za.shape
    q_lora, kv_lora = wuq.shape[0], wukv.shape[0]
    hp = heads * A_HEAD_PAD
    row = lambda i: (i, 0)
    fix = lambda i: (0, 0)
    blocks = (_nbytes((tm, za_cols), za.dtype) + 3 * _nbytes((tm, LANE), F32) + _nbytes(wuq.shape, BF16)
              + _nbytes(wukv.shape, BF16) + 2 * _nbytes((tm, hp), BF16) + _nbytes((tm, heads * V_PAD), BF16))
    return pl.pallas_call(
        functools.partial(_mla_proj_kernel, heads=heads, q_lora=q_lora, kv_lora=kv_lora),
        grid=(t // tm,),
        in_specs=[pl.BlockSpec((tm, za_cols), row),
                  pl.BlockSpec((tm, LANE), row), pl.BlockSpec((tm, LANE), row), pl.BlockSpec((tm, LANE), row),
                  pl.BlockSpec(wuq.shape, fix), pl.BlockSpec(wukv.shape, fix),
                  pl.BlockSpec((1, q_lora), fix), pl.BlockSpec((1, kv_lora), fix),
                  pl.BlockSpec((1, A_HEAD_PAD), fix), pl.BlockSpec((1, A_HEAD_PAD), fix)],
        out_specs=[pl.BlockSpec((tm, hp), row), pl.BlockSpec((tm, hp), row), pl.BlockSpec((tm, heads * V_PAD), row)],
        out_shape=[jax.ShapeDtypeStruct((t, hp), BF16), jax.ShapeDtypeStruct((t, hp), BF16),
                   jax.ShapeDtypeStruct((t, heads * V_PAD), BF16)],
        compiler_params=_params(("parallel",), blocks, 3 * _nbytes((tm, hp), F32)),
        name="mla_proj",
    )(za, cos, sin_lo, sin_hi, wuq, wukv, gq, gkv, gaq, gak)


def _mla_scores(i, q_ref, k_ref, diag_ok):
    tq = ATTN_TILE
    q = q_ref[i * tq:(i + 1) * tq, :]
    sd = jnp.where(diag_ok, _nt_dot(q, k_ref[i * tq:(i + 1) * tq, :]), NEG_INF)
    s0 = _nt_dot(q, k_ref[:i * tq, :]) if i > 0 else None
    return sd, s0


def _mla_softmax(sd, s0):
    m = jnp.max(sd, axis=-1, keepdims=True)
    if s0 is None:
        return jnp.exp(sd - m).astype(BF16), None
    m = jnp.maximum(m, jnp.max(s0, axis=-1, keepdims=True))
    return jnp.exp(sd - m).astype(BF16), jnp.exp(s0 - m).astype(BF16)


def _mla_values(i, pd, p0, v_ref, o_ref):
    tq = ATTN_TILE
    o = jnp.dot(pd, v_ref[i * tq:(i + 1) * tq, :], preferred_element_type=F32)
    if p0 is not None:
        o = o + jnp.dot(p0, v_ref[:i * tq, :], preferred_element_type=F32)
    o_ref[i * tq:(i + 1) * tq, :] = (o[:, :A_V] / o[:, A_V:A_V + 1]).astype(o_ref.dtype)


def _band_window(i):
    left = B_LEFT_CHUNKS * CHUNK
    q0 = i * ATTN_TILE
    k0 = max(0, q0 - left)
    return q0, k0, q0 + ATTN_TILE - k0, left - q0 + k0


def _band_scores(i, q_ref, k_ref, table):
    q0, k0, kw, u0 = _band_window(i)
    return _nt_dot(q_ref[q0:q0 + ATTN_TILE, :], k_ref[k0:k0 + kw, :]) + table[:, u0:u0 + kw]


def _band_softmax(s):
    return jnp.exp(s - jnp.max(s, axis=-1, keepdims=True)).astype(BF16)


def _band_values(i, p, v_ref, o_ref):
    q0, k0, kw, _ = _band_window(i)
    o = jnp.dot(p, v_ref[k0:k0 + kw, :], preferred_element_type=F32)
    o_ref[q0:q0 + ATTN_TILE, :] = (o[:, :B_HEAD_DIM] / o[:, B_HEAD_DIM:B_HEAD_DIM + 1]).astype(o_ref.dtype)


def _attn_kernel(aq_ref, ak_ref, av_ref, bq_ref, bk_ref, bv_ref, r_ref, oa_ref, ob_ref, bv_pad, *, seq):
    tq = ATTN_TILE
    left = B_LEFT_CHUNKS * CHUNK
    width = r_ref.shape[1]
    rc = lax.broadcasted_iota(jnp.int32, (tq, tq), 0) // CHUNK
    cc = lax.broadcasted_iota(jnp.int32, (tq, tq), 1) // CHUNK
    diag_ok = cc <= rc
    bias = pltpu.roll(jnp.broadcast_to(r_ref[...], (tq, width)), 0, 1, stride=1, stride_axis=0)[:, :left + tq]
    q_chunk = lax.broadcasted_iota(jnp.int32, (tq, left + tq), 0) // CHUNK
    k_chunk = lax.broadcasted_iota(jnp.int32, (tq, left + tq), 1) // CHUNK
    table = jnp.where((k_chunk >= q_chunk) & (k_chunk <= q_chunk + B_LEFT_CHUNKS), bias, NEG_INF)
    bv_pad[:, :B_HEAD_DIM] = bv_ref[...]
    bv_pad[:, B_HEAD_DIM:] = _ones_column(seq)

    n_tiles = seq // tq
    order = [n_tiles - 1 - i // 2 if i % 2 == 0 else i // 2 for i in range(n_tiles)]
    sd, s0 = _mla_scores(order[0], aq_ref, ak_ref, diag_ok)
    for n, i in enumerate(order):
        sb = _band_scores(i, bq_ref, bk_ref, table)
        pd, p0 = _mla_softmax(sd, s0)
        if n + 1 < n_tiles:
            sd, s0 = _mla_scores(order[n + 1], aq_ref, ak_ref, diag_ok)
        _mla_values(i, pd, p0, av_ref, oa_ref)
        _band_values(i, _band_softmax(sb), bv_pad, ob_ref)


def _band_bias_rows(rel_bias):
    left = B_LEFT_CHUNKS * CHUNK
    width = left + 2 * ATTN_TILE
    m = jnp.arange(width, dtype=jnp.int32)
    j = jnp.where(m < left + ATTN_TILE, m, m - width)
    dist = left - j
    rows = rel_bias[:, jnp.clip(dist, -B_MAX_REL, B_MAX_REL) + B_MAX_REL].astype(F32)
    return rows.reshape(rel_bias.shape[0], 1, width)


def _attention(q, k, v, bqk, bv, bias_rows, batch, seq, heads):
    t = batch * seq
    d = B_HEAD_DIM
    width = bias_rows.shape[2]
    head = lambda b, h: (b, h)
    blocks = (2 * _nbytes((seq, A_HEAD_PAD), BF16) + _nbytes((seq, V_PAD), BF16) + _nbytes((seq, A_V), BF16)
              + 4 * _nbytes((seq, d), BF16) + _nbytes((1, width), F32))
    temps = _nbytes((seq, V_PAD), BF16) + 8 * _nbytes((ATTN_TILE, seq), F32) + 8 * _nbytes((ATTN_TILE, width), F32)
    return pl.pallas_call(
        functools.partial(_attn_kernel, seq=seq),
        grid=(batch, heads),
        in_specs=[pl.BlockSpec((seq, A_HEAD_PAD), head), pl.BlockSpec((seq, A_HEAD_PAD), head),
                  pl.BlockSpec((seq, V_PAD), head),
                  pl.BlockSpec((seq, d), head),
                  pl.BlockSpec((seq, d), lambda b, h: (b, heads + h)),
                  pl.BlockSpec((seq, d), head),
                  pl.BlockSpec((None, 1, width), lambda b, h: (h, 0, 0))],
        out_specs=[pl.BlockSpec((seq, A_V), head), pl.BlockSpec((seq, d), head)],
        out_shape=[jax.ShapeDtypeStruct((t, heads * A_V), BF16), jax.ShapeDtypeStruct((t, heads * d), BF16)],
        scratch_shapes=[pltpu.VMEM((seq, V_PAD), BF16)],
        compiler_params=_params(("parallel", "parallel"), blocks, temps),
        name="attention",
    )(q, k, v, bqk, bqk, bv, bias_rows)


def _merge_kernel(oa_ref, ob_ref, woa_ref, wob_ref, g0_ref, g1_ref, o_ref, woa_bf, wob_bf):
    @pl.when(pl.program_id(1) == 0)
    def _():
        woa_bf[...] = woa_ref[...].astype(BF16)
        wob_bf[...] = wob_ref[...].astype(BF16)

    a = jnp.dot(oa_ref[...], woa_bf[...], preferred_element_type=F32)
    b = jnp.dot(ob_ref[...], wob_bf[...], preferred_element_type=F32)
    o_ref[...] = (g0_ref[...].astype(F32) * a + g1_ref[...].astype(F32) * b).astype(o_ref.dtype)


def _merge(oa, ob, woa, wob, gates, tm, tn):
    m = oa.shape[0]
    d = woa.shape[1]
    g1 = d // tn
    blocks = (_nbytes((tm, oa.shape[1]), BF16) + _nbytes((tm, ob.shape[1]), BF16) + _nbytes((woa.shape[0], tn), F32)
              + _nbytes((wob.shape[0], tn), F32) + 3 * _nbytes((tm, tn), BF16))
    resident = (2 * _nbytes((woa.shape[0], tn), BF16) + 2 * _nbytes((wob.shape[0], tn), BF16)
                + 3 * _nbytes((tm, tn), F32))
    return pl.pallas_call(
        _merge_kernel,
        grid=(d // tn, m // tm),
        in_specs=[pl.BlockSpec((tm, oa.shape[1]), lambda j, i: (i, 0)),
                  pl.BlockSpec((tm, ob.shape[1]), lambda j, i: (i, 0)),
                  pl.BlockSpec((woa.shape[0], tn), lambda j, i: (0, j)),
                  pl.BlockSpec((wob.shape[0], tn), lambda j, i: (0, j)),
                  pl.BlockSpec((tm, tn), lambda j, i: (i, j)),
                  pl.BlockSpec((tm, tn), lambda j, i: (i, g1 + j))],
        out_specs=pl.BlockSpec((tm, tn), lambda j, i: (i, j)),
        out_shape=jax.ShapeDtypeStruct((m, d), BF16),
        scratch_shapes=[pltpu.VMEM((woa.shape[0], tn), BF16), pltpu.VMEM((wob.shape[0], tn), BF16)],
        compiler_params=_params(("parallel", "arbitrary"), blocks, resident),
        name="merge",
    )(oa, ob, woa, wob, gates, gates)


def _out_proj_kernel(a_ref, w_ref, x_ref, o_ref, w_bf):
    @pl.when(pl.program_id(1) == 0)
    def _():
        w_bf[...] = w_ref[...].astype(BF16)

    o_ref[...] = x_ref[...] + jnp.dot(a_ref[...], w_bf[...], preferred_element_type=F32)


def _out_proj(merged, wout, x, tm, tn):
    m, k = merged.shape
    n = wout.shape[1]
    blocks = _nbytes((tm, k), BF16) + _nbytes((k, tn), F32) + 2 * _nbytes((tm, tn), F32)
    return pl.pallas_call(
        _out_proj_kernel,
        grid=(n // tn, m // tm),
        in_specs=[pl.BlockSpec((tm, k), lambda j, i: (i, 0)),
                  pl.BlockSpec((k, tn), lambda j, i: (0, j)),
                  pl.BlockSpec((tm, tn), lambda j, i: (i, j))],
        out_specs=pl.BlockSpec((tm, tn), lambda j, i: (i, j)),
        out_shape=jax.ShapeDtypeStruct((m, n), F32),
        scratch_shapes=[pltpu.VMEM((k, tn), BF16)],
        compiler_params=_params(("parallel", "arbitrary"), blocks, 2 * _nbytes((k, tn), BF16) + _nbytes((tm, tn), F32)),
        name="out_proj",
    )(merged, wout, x)


def _split_bf16(v):
    hi = v.astype(BF16)
    return hi, (v - hi.astype(F32)).astype(BF16)


def _router_kernel(x_ref, g_ref, wr_ref, xn_ref, route_ref, *, n_groups, per_group):
    x = x_ref[...]
    r = lax.rsqrt(jnp.mean(x * x, axis=-1, keepdims=True) + EPS)
    xn = x * r * g_ref[...]
    xn_ref[...] = xn
    x_hi, x_lo = _split_bf16(xn)
    w_hi, w_lo = _split_bf16(wr_ref[...])
    dot = functools.partial(jnp.dot, preferred_element_type=F32)
    logits = dot(x_hi, w_hi) + (dot(x_lo, w_hi) + dot(x_hi, w_lo))
    lane = lax.broadcasted_iota(jnp.int32, logits.shape, 1).astype(F32)
    far = float(LANE)

    def top(vals):
        best = jnp.max(vals, axis=-1, keepdims=True)
        return best, jnp.min(jnp.where(vals == best, lane, far), axis=-1, keepdims=True)

    gl = jnp.where(lane < n_groups, logits, NEG_INF)
    gmax, grp = top(gl)
    p_grp = 1.0 / jnp.sum(jnp.exp(gl - gmax), axis=-1, keepdims=True)
    lo = n_groups + grp * per_group
    el = jnp.where((lane >= lo) & (lane < lo + per_group), logits, NEG_INF)
    t1, i1 = top(el)
    t2, i2 = top(jnp.where(lane == i1, NEG_INF, el))
    d = jnp.exp(t2 - t1)
    w1 = p_grp / (1.0 + d)
    w2 = p_grp * d / (1.0 + d)
    route_ref[...] = jnp.where(lane == 0, i1 - n_groups,
                               jnp.where(lane == 1, i2 - n_groups,
                                         jnp.where(lane == 2, w1, jnp.where(lane == 3, w2, 0.0))))


def _router(x1, g, wr, n_groups, per_group, tm):
    t, d = x1.shape
    blocks = 2 * _nbytes((tm, d), F32) + _nbytes((1, d), F32) + _nbytes((d, LANE), F32) + _nbytes((tm, LANE), F32)
    return pl.pallas_call(
        functools.partial(_router_kernel, n_groups=n_groups, per_group=per_group),
        grid=(t // tm,),
        in_specs=[pl.BlockSpec((tm, d), lambda i: (i, 0)), pl.BlockSpec((1, d), lambda i: (0, 0)),
                  pl.BlockSpec((d, LANE), lambda i: (0, 0))],
        out_specs=[pl.BlockSpec((tm, d), lambda i: (i, 0)), pl.BlockSpec((tm, LANE), lambda i: (i, 0))],
        out_shape=[jax.ShapeDtypeStruct((t, d), F32), jax.ShapeDtypeStruct((t, LANE), F32)],
        compiler_params=_params(("parallel",), blocks, 2 * _nbytes((tm, d), F32)),
        name="router",
    )(x1, g.reshape(1, d), wr)


def _one_hots(route):
    lane = lax.broadcasted_iota(jnp.int32, route.shape, 1).astype(F32)
    return (lane == route[:, 0:1]).astype(F32), (lane == route[:, 1:2]).astype(F32)


def _rank_kernel(route_ref, rank_ref, starts_ref, count_acc, start_acc):
    i = pl.program_id(0)

    @pl.when(i == 0)
    def _():
        count_acc[...] = jnp.zeros_like(count_acc)
        start_acc[...] = jnp.zeros_like(start_acc)

    oh1, oh2 = _one_hots(route_ref[...])
    oh = (oh1 + oh2).astype(BF16)
    tm = oh.shape[0]
    earlier = (lax.broadcasted_iota(jnp.int32, (tm, tm), 0) > lax.broadcasted_iota(jnp.int32, (tm, tm), 1))
    before = jnp.dot(earlier.astype(BF16), oh, preferred_element_type=F32) + count_acc[...]
    lane = lax.broadcasted_iota(jnp.int32, (tm, LANE), 1)
    rank_ref[...] = jnp.where(lane == 0, jnp.sum(oh1 * before, axis=-1, keepdims=True),
                              jnp.where(lane == 1, jnp.sum(oh2 * before, axis=-1, keepdims=True), 0.0))
    lower = (lax.broadcasted_iota(jnp.int32, (LANE, LANE), 0) < lax.broadcasted_iota(jnp.int32, (LANE, LANE), 1))
    below = jnp.dot(oh, lower.astype(BF16), preferred_element_type=F32)
    count_acc[...] += jnp.sum(oh.astype(F32), axis=0, keepdims=True)
    start_acc[...] += jnp.sum(below, axis=0, keepdims=True)
    starts_ref[...] = start_acc[...]


def _rank(route, tm):
    t = route.shape[0]
    blocks = 2 * _nbytes((tm, LANE), F32) + _nbytes((1, LANE), F32)
    return pl.pallas_call(
        _rank_kernel,
        grid=(t // tm,),
        in_specs=[pl.BlockSpec((tm, LANE), lambda i: (i, 0))],
        out_specs=[pl.BlockSpec((tm, LANE), lambda i: (i, 0)), pl.BlockSpec((1, LANE), lambda i: (0, 0))],
        out_shape=[jax.ShapeDtypeStruct((t, LANE), F32), jax.ShapeDtypeStruct((1, LANE), F32)],
        scratch_shapes=[pltpu.VMEM((1, LANE), F32), pltpu.VMEM((1, LANE), F32)],
        compiler_params=_params(("arbitrary",), blocks, 2 * _nbytes((tm, tm), F32)),
        name="moe_rank",
    )(route)


def _dest_kernel(route_ref, rank_ref, starts_ref, dest_ref):
    oh1, oh2 = _one_hots(route_ref[...])
    rank = rank_ref[...]
    starts = starts_ref[...]
    d1 = jnp.sum(oh1 * starts, axis=-1, keepdims=True) + rank[:, 0:1]
    d2 = jnp.sum(oh2 * starts, axis=-1, keepdims=True) + rank[:, 1:2]
    lane = lax.broadcasted_iota(jnp.int32, rank.shape, 1)
    dest_ref[...] = jnp.where(lane == 0, d1, jnp.where(lane == 1, d2, 0.0)).astype(jnp.int32)


def _dest(route, rank, starts, tm):
    t = route.shape[0]
    blocks = 3 * _nbytes((tm, LANE), F32) + _nbytes((1, LANE), F32)
    return pl.pallas_call(
        _dest_kernel,
        grid=(t // tm,),
        in_specs=[pl.BlockSpec((tm, LANE), lambda i: (i, 0)), pl.BlockSpec((tm, LANE), lambda i: (i, 0)),
                  pl.BlockSpec((1, LANE), lambda i: (0, 0))],
        out_specs=pl.BlockSpec((tm, LANE), lambda i: (i, 0)),
        out_shape=jax.ShapeDtypeStruct((t, LANE), jnp.int32),
        compiler_params=_params(("parallel",), blocks, 4 * _nbytes((tm, LANE), F32)),
        name="moe_dest",
    )(route, rank, starts)


def _work_items(starts, n_rows):
    n_exp = starts.shape[0]
    n_blk = n_rows // MOE_ROWS
    total = jnp.full((1,), n_rows, jnp.int32)
    pts = jnp.concatenate([jnp.arange(n_blk, dtype=jnp.int32) * MOE_ROWS, starts[1:]])
    idx = jnp.arange(pts.shape[0], dtype=jnp.int32)
    before = (pts[None, :] < pts[:, None]) | ((pts[None, :] == pts[:, None]) & (idx[None, :] < idx[:, None]))
    pos = jnp.sum(before.astype(jnp.int32), axis=1)
    lo = jnp.sum(jnp.where(pos[:, None] == idx[None, :], pts[:, None], 0), axis=0)
    hi = jnp.concatenate([lo[1:], total])
    ends = jnp.concatenate([starts[1:], total])
    r = jnp.minimum(lo // MOE_ROWS, n_blk - 1)
    e = jnp.minimum(jnp.sum((ends[None, :] <= lo[:, None]).astype(jnp.int32), axis=1), n_exp - 1)
    later = jnp.where(e[None, :] > e[:, None], e[None, :], n_exp)
    nxt = jnp.min(later, axis=1)
    nxt = jnp.where(nxt == n_exp, -1, nxt)
    return r, e, lo, hi, nxt


def _stream_expert_weights(w, e_ref, nxt_ref, streams):
    e = e_ref[w]

    def copies(stream, expert):
        hbm, stage, _, sem = stream
        rows = stage.shape[0] // WEIGHT_DMA_PARTS
        parts = [pl.ds(part * rows, rows) for part in range(WEIGHT_DMA_PARTS)]
        return [pltpu.make_async_copy(hbm.at[expert, sl], stage.at[sl], sem) for sl in parts]

    @pl.when(w == 0)
    def _():
        for stream in streams:
            for c in copies(stream, e):
                c.start(priority=WEIGHT_DMA_PRIORITY)

    @pl.when((w == 0) | (e != e_ref[jnp.maximum(w - 1, 0)]))
    def _():
        nxt = nxt_ref[w]
        for stream in streams:
            for c in copies(stream, e):
                c.wait()
            _, stage, w_bf, _ = stream
            rows = stage.shape[0] // CONVERT_PARTS
            for part in range(CONVERT_PARTS):
                sl = slice(part * rows, (part + 1) * rows)
                w_bf[sl, :] = stage[sl, :].astype(BF16)

            @pl.when(nxt >= 0)
            def _():
                for c in copies(stream, nxt):
                    c.start(priority=WEIGHT_DMA_PRIORITY)


def _store_item_rows(o_ref, val, r, lo, hi):
    rows = r * MOE_ROWS + lax.broadcasted_iota(jnp.int32, val.shape, 0)
    mine = (rows >= lo) & (rows < hi)

    @pl.when(lo == r * MOE_ROWS)
    def _():
        o_ref[...] = val

    @pl.when(lo != r * MOE_ROWS)
    def _():
        pltpu.store(o_ref, val, mask=mine)


def _gather_block_rows(w, r_ref, dest_ref, x_hbm, source, rows_f32, x_bf, sem, *, n_assign):
    r = r_ref[w]
    n_blocks = n_assign // MOE_ROWS

    def copy(src_row, slot, j):
        return pltpu.make_async_copy(x_hbm.at[pl.ds(src_row, 1)], rows_f32.at[slot, pl.ds(j, 1)], sem.at[slot])

    def start_block(block, slot):
        for j in range(MOE_ROWS):
            copy(source[block * MOE_ROWS + j], slot, j).start()

    def finish_block(slot):
        for j in range(MOE_ROWS):
            copy(0, slot, j).wait()
        x_bf[...] = rows_f32[slot].astype(BF16)

    @pl.when(w == 0)
    def _():
        def invert(tok, carry):
            for k in range(TOP_K):
                source[dest_ref[TOP_K * tok + k]] = tok
            return carry
        lax.fori_loop(0, n_assign // TOP_K, invert, 0, unroll=4)
        start_block(r, 0)

    @pl.when((w == 0) | (r != r_ref[jnp.maximum(w - 1, 0)]))
    def _():
        for slot in range(2):
            @pl.when(r % 2 == slot)
            def _():
                finish_block(slot)

                @pl.when(r + 1 < n_blocks)
                def _():
                    start_block(r + 1, 1 - slot)


def _moe_experts_kernel(r_ref, e_ref, lo_ref, hi_ref, nxt_ref, dest_ref, x_hbm, wg_hbm, wu_hbm, wd_hbm, y_ref,
                        wg_stage, wu_stage, wd_stage, wg_bf, wu_bf, wd_bf, source, rows_f32, x_bf, sem, row_sem,
                        *, n_assign):
    w = pl.program_id(0)
    _stream_expert_weights(w, e_ref, nxt_ref, [(wg_hbm, wg_stage, wg_bf, sem.at[0]),
                                                 (wu_hbm, wu_stage, wu_bf, sem.at[1]),
                                                 (wd_hbm, wd_stage, wd_bf, sem.at[2])])
    _gather_block_rows(w, r_ref, dest_ref, x_hbm, source, rows_f32, x_bf, row_sem, n_assign=n_assign)
    r, lo, hi = r_ref[w], lo_ref[w], hi_ref[w]

    @pl.when(hi > lo)
    def _():
        x = x_bf[...]
        g = jnp.dot(x, wg_bf[...], preferred_element_type=F32)
        u = jnp.dot(x, wu_bf[...], preferred_element_type=F32)
        h = ((g * (1.0 / (1.0 + jnp.exp(-g)))) * u).astype(BF16)
        y = jnp.dot(h, wd_bf[...], preferred_element_type=F32)
        _store_item_rows(y_ref, y, r, lo, hi)


def _moe_experts(items, dest_flat, xn, wg, wu, wd):
    n_assign = dest_flat.shape[0]
    d = xn.shape[1]
    f = wg.shape[2]
    n_items = items[0].shape[0]
    blocks = _nbytes((MOE_ROWS, d), F32)
    resident = (3 * _nbytes((d, f), F32) + 3 * _nbytes((d, f), BF16) + 5 * _nbytes((MOE_ROWS, d), F32)
                + _nbytes((MOE_ROWS, d), BF16) + 6 * _nbytes((MOE_ROWS, f), F32))
    any_space = pl.BlockSpec(memory_space=pl.ANY)
    return pl.pallas_call(
        functools.partial(_moe_experts_kernel, n_assign=n_assign),
        grid_spec=pltpu.PrefetchScalarGridSpec(
            num_scalar_prefetch=6,
            grid=(n_items,),
            in_specs=[any_space, any_space, any_space, any_space],
            out_specs=pl.BlockSpec((MOE_ROWS, d), lambda w, r, e, lo, hi, nxt, dest: (r[w], 0)),
            scratch_shapes=[pltpu.VMEM((d, f), F32), pltpu.VMEM((d, f), F32), pltpu.VMEM((f, d), F32),
                            pltpu.VMEM((d, f), BF16), pltpu.VMEM((d, f), BF16), pltpu.VMEM((f, d), BF16),
                            pltpu.SMEM((n_assign,), jnp.int32), pltpu.VMEM((2, MOE_ROWS, d), F32),
                            pltpu.VMEM((MOE_ROWS, d), BF16),
                            pltpu.SemaphoreType.DMA((3,)), pltpu.SemaphoreType.DMA((2,))]),
        out_shape=jax.ShapeDtypeStruct((n_assign, d), F32),
        compiler_params=_params(("arbitrary",), blocks, resident),
        name="moe_experts",
    )(*items, dest_flat, xn, wg, wu, wd)


def _combine_kernel(dest_ref, x_ref, route_ref, y_hbm, o_ref, ybuf, sem, *, tm):
    half = tm // 2
    i = pl.program_id(0)

    def copy(src_row, slot, k, t):
        return pltpu.make_async_copy(y_hbm.at[pl.ds(src_row, 1)], ybuf.at[slot, k, pl.ds(t, 1)], sem.at[slot])

    def start_half(index, slot):
        for t in range(half):
            for k in range(TOP_K):
                copy(dest_ref[TOP_K * (index * half + t) + k], slot, k, t).start(priority=ROW_DMA_PRIORITY)

    def finish_half(slot):
        for t in range(half):
            for k in range(TOP_K):
                copy(0, slot, k, t).wait()
        rows = slice(slot * half, (slot + 1) * half)
        route = route_ref[rows, :]
        o_ref[rows, :] = x_ref[rows, :] + (route[:, 2:3] * ybuf[slot, 0] + route[:, 3:4] * ybuf[slot, 1])

    @pl.when(i == 0)
    def _():
        start_half(2 * i, 0)

    start_half(2 * i + 1, 1)
    finish_half(0)

    @pl.when(i + 1 < pl.num_programs(0))
    def _():
        start_half(2 * i + 2, 0)

    finish_half(1)


def _combine(dest_flat, x1, route, y, tm):
    t, d = x1.shape
    blocks = 2 * _nbytes((tm, d), F32) + _nbytes((tm, LANE), F32)
    resident = TOP_K * _nbytes((tm, d), F32) + _nbytes((tm, d), F32)
    return pl.pallas_call(
        functools.partial(_combine_kernel, tm=tm),
        grid_spec=pltpu.PrefetchScalarGridSpec(
            num_scalar_prefetch=1,
            grid=(t // tm,),
            in_specs=[pl.BlockSpec((tm, d), lambda i, dest: (i, 0)),
                      pl.BlockSpec((tm, LANE), lambda i, dest: (i, 0)),
                      pl.BlockSpec(memory_space=pl.ANY)],
            out_specs=pl.BlockSpec((tm, d), lambda i, dest: (i, 0)),
            scratch_shapes=[pltpu.VMEM((2, TOP_K, tm // 2, d), F32), pltpu.SemaphoreType.DMA((2,))]),
        out_shape=jax.ShapeDtypeStruct((t, d), F32),
        compiler_params=_params(("arbitrary",), blocks, resident),
        name="moe_combine",
    )(dest_flat, x1, route, y)


def _pad_cols(w, n):
    return jnp.pad(w, ((0, 0), (0, n - w.shape[1])))


def kernel(x, positions, g_mix, w_in, b_gate, q_norm_g, kv_norm_g, w_uq, w_ukv, a_q_norm_g, a_k_norm_g,
           b_q_norm_g, b_k_norm_g, rel_bias, w_o_a, w_o_b, w_out, g_ffn, w_group, w_expert,
           w_exp_gate, w_exp_up, w_exp_down):
    batch, seq, d = x.shape
    t = batch * seq
    q_lora, kv_lora = q_norm_g.shape[0], kv_norm_g.shape[0]
    a_heads = w_uq.shape[1] // A_QK
    b_heads = w_o_b.shape[0] // B_HEAD_DIM
    b_width = b_heads * B_HEAD_DIM
    n_groups, n_experts = w_group.shape[1], w_expert.shape[1]
    per_group = n_experts // n_groups
    off_b = q_lora + kv_lora + A_ROPE
    assert seq % ATTN_TILE == 0 and (TOP_K * t) % MOE_ROWS == 0 and n_groups + n_experts <= LANE
    assert a_heads == b_heads

    xf = x.reshape(t, d)
    tm_big = min(1024, t)
    tn = _tile(b_width, 512)
    tn_wide = _tile(b_width, 1024)
    assert d % tn == 0 and d % tn_wide == 0

    za_cols = -(-(q_lora + kv_lora + LANE) // tn_wide) * tn_wide
    wuq = jnp.pad(w_uq.reshape(q_lora, a_heads, A_QK), ((0, 0), (0, 0), (0, A_HEAD_PAD - A_QK)))
    wuq = wuq.reshape(q_lora, a_heads * A_HEAD_PAD).astype(BF16)
    wukv = w_ukv.astype(BF16)
    pad_gain = lambda g, s: jnp.pad(g * s, (0, A_HEAD_PAD - A_QK)).reshape(1, A_HEAD_PAD)
    gaq = pad_gain(a_q_norm_g, A_QK ** -0.5)
    gak = pad_gain(a_k_norm_g, 1.0)
    g_bqk = jnp.concatenate([jnp.tile(b_q_norm_g * B_HEAD_DIM ** -0.5, b_heads), jnp.tile(b_k_norm_g, b_heads)])

    half = A_ROPE // 2
    inv = ROPE_THETA ** (-jnp.arange(half, dtype=F32) / half)
    ang = positions.astype(F32).reshape(t, 1) * inv
    cos, sin = jnp.cos(ang), jnp.sin(ang)
    zeros = jnp.zeros((t, half), F32)
    cos_t = jnp.concatenate([cos, cos, zeros, zeros], axis=1)
    sin_lo = jnp.concatenate([-sin, zeros, zeros, zeros], axis=1)
    sin_hi = jnp.concatenate([zeros, sin, zeros, zeros], axis=1)

    xn = _rmsnorm_rows(xf, g_mix, BF16, min(256, t))
    w_in_t = w_in.T
    za = _inproj_cols(xn, w_in_t, 0, za_cols, None, _plain_epilogue, tm_big, tn_wide)
    bqk = _inproj_cols(xn, w_in_t, off_b, 2 * b_width, g_bqk, _head_norm_epilogue, tm_big, tn_wide)
    bv = _inproj_cols(xn, w_in_t, off_b + 2 * b_width, b_width, None, _plain_epilogue, tm_big, tn_wide)
    gates = _inproj_cols(xn, w_in_t, off_b + 3 * b_width, 2 * d, b_gate, _sigmoid_epilogue, tm_big, tn_wide)
    q, k, v = _mla_proj(za, cos_t, sin_lo, sin_hi, wuq, wukv, q_norm_g.reshape(1, -1), kv_norm_g.reshape(1, -1),
                        gaq, gak, a_heads, min(256, t))
    o_a, o_b = _attention(q, k, v, bqk, bv, _band_bias_rows(rel_bias), batch, seq, a_heads)
    merged = _merge(o_a, o_b, w_o_a, w_o_b, gates, tm_big, tn)
    x1 = _out_proj(merged, w_out, xf, tm_big, tn)

    wr = _pad_cols(jnp.concatenate([w_group, w_expert], axis=1), LANE)
    xn2, route = _router(x1, g_ffn, wr, n_groups, per_group, min(256, t))
    rank, starts_f = _rank(route, min(512, t))
    dest = _dest(route, rank, starts_f, min(512, t))[:, :TOP_K].reshape(-1)
    items = _work_items(starts_f[0, :n_experts].astype(jnp.int32), TOP_K * t)
    y = _moe_experts(items, dest, xn2, w_exp_gate, w_exp_up, w_exp_down)
    out = _combine(dest, x1, route, y, min(256, t))
    return out.reshape(batch, seq, d)
```

```python
import functools

import jax
import jax.numpy as jnp
from jax import lax
from jax.experimental import pallas as pl
from jax.experimental.pallas import tpu as pltpu

F32 = jnp.float32
BF16 = jnp.bfloat16

CHUNK = 64
EPS = 1e-6
A_NOPE = 128
A_ROPE = 64
A_V = 128
A_QK = A_NOPE + A_ROPE
B_HEAD_DIM = 128
B_LEFT_CHUNKS = 8
B_MAX_REL = 128
ROPE_THETA = 10000.0
TOP_K = 2

LANE = 128
A_HEAD_PAD = 2 * LANE
V_PAD = 2 * LANE
V7X_VMEM_BYTES = 64 * 2**20

ATTN_TILE = 256
MOE_ROWS = 128
WEIGHT_DMA_PARTS = 4
WEIGHT_DMA_PRIORITY = 1
ROW_DMA_PRIORITY = 1
CONVERT_PARTS = 8
PROJ_GROUP_ROWS = 256
NEG_INF = float("-inf")


def _nbytes(shape, dtype):
    n = 1
    for s in shape:
        n *= s
    return n * jnp.dtype(dtype).itemsize


def _params(semantics, pipelined_bytes, resident_bytes=0):
    need = 2 * pipelined_bytes + resident_bytes
    return pltpu.CompilerParams(dimension_semantics=semantics,
                                vmem_limit_bytes=min(int(need), V7X_VMEM_BYTES))


def _tile(n, want):
    t = want
    while t > LANE and n % t:
        t //= 2
    assert n % t == 0, (n, want)
    return t


def _rmsnorm_kernel(x_ref, g_ref, o_ref):
    x = x_ref[...]
    r = lax.rsqrt(jnp.mean(x * x, axis=-1, keepdims=True) + EPS)
    o_ref[...] = (x * r * g_ref[...]).astype(o_ref.dtype)


def _rmsnorm_rows(x, g, out_dtype, tm):
    t, d = x.shape
    blocks = _nbytes((tm, d), F32) + _nbytes((tm, d), out_dtype) + _nbytes((1, d), F32)
    return pl.pallas_call(
        _rmsnorm_kernel,
        grid=(t // tm,),
        in_specs=[pl.BlockSpec((tm, d), lambda i: (i, 0)), pl.BlockSpec((1, d), lambda i: (0, 0))],
        out_specs=pl.BlockSpec((tm, d), lambda i: (i, 0)),
        out_shape=jax.ShapeDtypeStruct((t, d), out_dtype),
        compiler_params=_params(("parallel",), blocks, _nbytes((tm, d), F32)),
        name="rmsnorm",
    )(x, g.reshape(1, d))


def _nt_dot(a, b):
    return lax.dot_general(a, b, (((1,), (1,)), ((), ())), preferred_element_type=F32)


def _row_groups(rows):
    step = min(rows, PROJ_GROUP_ROWS)
    assert rows % step == 0
    return [slice(s, s + step) for s in range(0, rows, step)]


def _head_norm_epilogue(acc, gb_ref):
    heads = []
    for h in range(acc.shape[1] // B_HEAD_DIM):
        sl = slice(h * B_HEAD_DIM, (h + 1) * B_HEAD_DIM)
        z = acc[:, sl]
        r = lax.rsqrt(jnp.mean(z * z, axis=-1, keepdims=True) + EPS)
        heads.append(z * r * gb_ref[:, sl])
    return jnp.concatenate(heads, axis=1)


def _plain_epilogue(acc, gb_ref):
    del gb_ref
    return acc


def _sigmoid_epilogue(acc, gb_ref):
    return 1.0 / (1.0 + jnp.exp(-(acc + gb_ref[...])))


def _inproj_cols_kernel(a_ref, wt_hbm, *refs, first_col, epilogue):
    gb_ref = refs[0] if len(refs) == 5 else None
    o_ref, stage, wt_bf, sem = refs[-4:]
    j = pl.program_id(0)
    tn = wt_bf.shape[0]

    def fetch(block):
        rows = pl.ds(pl.multiple_of(first_col + block * tn, 8), tn)
        return pltpu.make_async_copy(wt_hbm.at[rows], stage, sem)

    @pl.when(pl.program_id(1) == 0)
    def _():
        @pl.when(j == 0)
        def _():
            fetch(j).start(priority=WEIGHT_DMA_PRIORITY)

        fetch(j).wait()
        wt_bf[...] = stage[...].astype(BF16)

        @pl.when(j + 1 < pl.num_programs(0))
        def _():
            fetch(j + 1).start(priority=WEIGHT_DMA_PRIORITY)

    groups = _row_groups(a_ref.shape[0])
    accs = [_nt_dot(a_ref[sl, :], wt_bf[...]) for sl in groups]
    for sl, acc in zip(groups, accs):
        o_ref[sl, :] = epilogue(acc, gb_ref).astype(o_ref.dtype)


def _inproj_cols(xn, w_in_t, first_col, n_cols, gb, epilogue, tm, tn):
    m, k = xn.shape
    assert n_cols % tn == 0 and first_col % 8 == 0
    blocks = _nbytes((tm, k), BF16) + _nbytes((tm, tn), BF16) + _nbytes((1, tn), F32)
    resident = _nbytes((tn, k), F32) + _nbytes((tn, k), BF16) + 3 * _nbytes((tm, tn), F32)
    gb_specs = [] if gb is None else [pl.BlockSpec((1, tn), lambda j, i: (0, j))]
    gb_args = [] if gb is None else [gb.reshape(1, n_cols)]
    return pl.pallas_call(
        functools.partial(_inproj_cols_kernel, first_col=first_col, epilogue=epilogue),
        grid=(n_cols // tn, m // tm),
        in_specs=[pl.BlockSpec((tm, k), lambda j, i: (i, 0)), pl.BlockSpec(memory_space=pl.ANY)] + gb_specs,
        out_specs=pl.BlockSpec((tm, tn), lambda j, i: (i, j)),
        out_shape=jax.ShapeDtypeStruct((m, n_cols), BF16),
        scratch_shapes=[pltpu.VMEM((tn, k), F32), pltpu.VMEM((tn, k), BF16), pltpu.SemaphoreType.DMA(())],
        compiler_params=_params(("arbitrary", "arbitrary"), blocks, resident),
        name="inproj_" + epilogue.__name__.strip("_").replace("_epilogue", ""),
    )(xn, w_in_t, *gb_args)


def _rope_padded(v, cos, sin_lo, sin_hi):
    half = A_ROPE // 2
    return v * cos + pltpu.roll(v, LANE - half, 1) * sin_lo + pltpu.roll(v, half, 1) * sin_hi


def _ones_column(rows):
    lane = lax.broadcasted_iota(jnp.int32, (rows, V_PAD - A_V), 1)
    return jnp.where(lane == 0, 1.0, 0.0).astype(BF16)


def _mla_proj_kernel(za_ref, cos_ref, sl_ref, sh_ref, wuq_ref, wukv_ref, gq_ref, gkv_ref, gaq_ref, gak_ref,
                     q_ref, k_ref, v_ref, *, heads, q_lora, kv_lora):
    cos, sin_lo, sin_hi = cos_ref[...], sl_ref[...], sh_ref[...]

    def norm(z, g):
        z = z.astype(F32)
        r = lax.rsqrt(jnp.mean(z * z, axis=-1, keepdims=True) + EPS)
        return (z * r * g).astype(BF16)

    cq = norm(za_ref[:, :q_lora], gq_ref[...])
    ckv = norm(za_ref[:, q_lora:q_lora + kv_lora], gkv_ref[...])
    slab = za_ref[:, q_lora + kv_lora:q_lora + kv_lora + LANE].astype(F32)
    k_rope = jnp.where(lax.broadcasted_iota(jnp.int32, slab.shape, 1) < A_ROPE, slab, 0.0)
    qacc = jnp.dot(cq, wuq_ref[...], preferred_element_type=F32)
    kvacc = jnp.dot(ckv, wukv_ref[...], preferred_element_type=F32)

    gq_lo, gq_hi = gaq_ref[:, :LANE], gaq_ref[:, LANE:]
    gk_lo, gk_hi = gak_ref[:, :LANE], gak_ref[:, LANE:]
    kr_ss = jnp.sum(k_rope * k_rope, axis=-1, keepdims=True)
    kr_rot = _rope_padded(k_rope * gk_hi, cos, sin_lo, sin_hi)
    ones_col = _ones_column(slab.shape[0])
    for h in range(heads):
        base = h * A_HEAD_PAD
        q_lo = qacc[:, base:base + LANE]
        q_hi = qacc[:, base + LANE:base + A_HEAD_PAD]
        ss = jnp.sum(q_lo * q_lo, axis=-1, keepdims=True) + jnp.sum(q_hi * q_hi, axis=-1, keepdims=True)
        r = lax.rsqrt(ss / A_QK + EPS)
        q_ref[:, base:base + LANE] = (q_lo * r * gq_lo).astype(BF16)
        q_ref[:, base + LANE:base + A_HEAD_PAD] = _rope_padded(q_hi * r * gq_hi, cos, sin_lo, sin_hi).astype(BF16)

        k_lo = kvacc[:, base:base + LANE]
        ssk = jnp.sum(k_lo * k_lo, axis=-1, keepdims=True) + kr_ss
        rk = lax.rsqrt(ssk / A_QK + EPS)
        k_ref[:, base:base + LANE] = (k_lo * rk * gk_lo).astype(BF16)
        k_ref[:, base + LANE:base + A_HEAD_PAD] = (kr_rot * rk).astype(BF16)
        v_ref[:, h * V_PAD:h * V_PAD + A_V] = kvacc[:, base + LANE:base + A_HEAD_PAD].astype(BF16)
        v_ref[:, h * V_PAD + A_V:(h + 1) * V_PAD] = ones_col


def _mla_proj(za, cos, sin_lo, sin_hi, wuq, wukv, gq, gkv, gaq, gak, heads, tm):
    t, za_cols = za.shape
    q_lora, kv_lora = wuq.shape[0], wukv.shape[0]
    hp = heads * A_HEAD_PAD
    row = lambda i: (i, 0)
    fix = lambda i: (0, 0)
    blocks = (_nbytes((tm, za_cols), za.dtype) + 3 * _nbytes((tm, LANE), F32) + _nbytes(wuq.shape, BF16)
              + _nbytes(wukv.shape, BF16) + 2 * _nbytes((tm, hp), BF16) + _nbytes((tm, heads * V_PAD), BF16))
    return pl.pallas_call(
        functools.partial(_mla_proj_kernel, heads=heads, q_lora=q_lora, kv_lora=kv_lora),
        grid=(t // tm,),
        in_specs=[pl.BlockSpec((tm, za_cols), row),
                  pl.BlockSpec((tm, LANE), row), pl.BlockSpec((tm, LANE), row), pl.BlockSpec((tm, LANE), row),
                  pl.BlockSpec(wuq.shape, fix), pl.BlockSpec(wukv.shape, fix),
                  pl.BlockSpec((1, q_lora), fix), pl.BlockSpec((1, kv_lora), fix),
                  pl.BlockSpec((1, A_HEAD_PAD), fix), pl.BlockSpec((1, A_HEAD_PAD), fix)],
        out_specs=[pl.BlockSpec((tm, hp), row), pl.BlockSpec((tm, hp), row), pl.BlockSpec((tm, heads * V_PAD), row)],
        out_shape=[jax.ShapeDtypeStruct((t, hp), BF16), jax.ShapeDtypeStruct((t, hp), BF16),
                   jax.ShapeDtypeStruct((t, heads * V_PAD), BF16)],
        compiler_params=_params(("parallel",), blocks, 3 * _nbytes((tm, hp), F32)),
        name="mla_proj",
    )(za, cos, sin_lo, sin_hi, wuq, wukv, gq, gkv, gaq, gak)


def _mla_scores(i, q_ref, k_ref, diag_ok):
    tq = ATTN_TILE
    q = q_ref[i * tq:(i + 1) * tq, :]
    sd = jnp.where(diag_ok, _nt_dot(q, k_ref[i * tq:(i + 1) * tq, :]), NEG_INF)
    s0 = _nt_dot(q, k_ref[:i * tq, :]) if i > 0 else None
    return sd, s0


def _mla_softmax(sd, s0):
    m = jnp.max(sd, axis=-1, keepdims=True)
    if s0 is None:
        return jnp.exp(sd - m).astype(BF16), None
    m = jnp.maximum(m, jnp.max(s0, axis=-1, keepdims=True))
    return jnp.exp(sd - m).astype(BF16), jnp.exp(s0 - m).astype(BF16)


def _mla_values(i, pd, p0, v_ref, o_ref):
    tq = ATTN_TILE
    o = jnp.dot(pd, v_ref[i * tq:(i + 1) * tq, :], preferred_element_type=F32)
    if p0 is not None:
        o = o + jnp.dot(p0, v_ref[:i * tq, :], preferred_element_type=F32)
    o_ref[i * tq:(i + 1) * tq, :] = (o[:, :A_V] / o[:, A_V:A_V + 1]).astype(o_ref.dtype)


def _band_window(i):
    left = B_LEFT_CHUNKS * CHUNK
    q0 = i * ATTN_TILE
    k0 = max(0, q0 - left)
    return q0, k0, q0 + ATTN_TILE - k0, left - q0 + k0


def _band_scores(i, q_ref, k_ref, table):
    q0, k0, kw, u0 = _band_window(i)
    return _nt_dot(q_ref[q0:q0 + ATTN_TILE, :], k_ref[k0:k0 + kw, :]) + table[:, u0:u0 + kw]


def _band_softmax(s):
    return jnp.exp(s - jnp.max(s, axis=-1, keepdims=True)).astype(BF16)


def _band_values(i, p, v_ref, o_ref):
    q0, k0, kw, _ = _band_window(i)
    o = jnp.dot(p, v_ref[k0:k0 + kw, :], preferred_element_type=F32)
    o_ref[q0:q0 + ATTN_TILE, :] = (o[:, :B_HEAD_DIM] / o[:, B_HEAD_DIM:B_HEAD_DIM + 1]).astype(o_ref.dtype)


def _attn_kernel(aq_ref, ak_ref, av_ref, bq_ref, bk_ref, bv_ref, r_ref, oa_ref, ob_ref, bv_pad, *, seq):
    tq = ATTN_TILE
    left = B_LEFT_CHUNKS * CHUNK
    width = r_ref.shape[1]
    rc = lax.broadcasted_iota(jnp.int32, (tq, tq), 0) // CHUNK
    cc = lax.broadcasted_iota(jnp.int32, (tq, tq), 1) // CHUNK
    diag_ok = cc <= rc
    bias = pltpu.roll(jnp.broadcast_to(r_ref[...], (tq, width)), 0, 1, stride=1, stride_axis=0)[:, :left + tq]
    q_chunk = lax.broadcasted_iota(jnp.int32, (tq, left + tq), 0) // CHUNK
    k_chunk = lax.broadcasted_iota(jnp.int32, (tq, left + tq), 1) // CHUNK
    table = jnp.where((k_chunk >= q_chunk) & (k_chunk <= q_chunk + B_LEFT_CHUNKS), bias, NEG_INF)
    bv_pad[:, :B_HEAD_DIM] = bv_ref[...]
    bv_pad[:, B_HEAD_DIM:] = _ones_column(seq)

    n_tiles = seq // tq
    order = [n_tiles - 1 - i // 2 if i % 2 == 0 else i // 2 for i in range(n_tiles)]
    sd, s0 = _mla_scores(order[0], aq_ref, ak_ref, diag_ok)
    for n, i in enumerate(order):
        sb = _band_scores(i, bq_ref, bk_ref, table)
        pd, p0 = _mla_softmax(sd, s0)
        if n + 1 < n_tiles:
            sd, s0 = _mla_scores(order[n + 1], aq_ref, ak_ref, diag_ok)
        _mla_values(i, pd, p0, av_ref, oa_ref)
        _band_values(i, _band_softmax(sb), bv_pad, ob_ref)


def _band_bias_rows(rel_bias):
    left = B_LEFT_CHUNKS * CHUNK
    width = left + 2 * ATTN_TILE
    m = jnp.arange(width, dtype=jnp.int32)
    j = jnp.where(m < left + ATTN_TILE, m, m - width)
    dist = left - j
    rows = rel_bias[:, jnp.clip(dist, -B_MAX_REL, B_MAX_REL) + B_MAX_REL].astype(F32)
    return rows.reshape(rel_bias.shape[0], 1, width)


def _attention(q, k, v, bqk, bv, bias_rows, batch, seq, heads):
    t = batch * seq
    d = B_HEAD_DIM
    width = bias_rows.shape[2]
    head = lambda b, h: (b, h)
    blocks = (2 * _nbytes((seq, A_HEAD_PAD), BF16) + _nbytes((seq, V_PAD), BF16) + _nbytes((seq, A_V), BF16)
              + 4 * _nbytes((seq, d), BF16) + _nbytes((1, width), F32))
    temps = _nbytes((seq, V_PAD), BF16) + 8 * _nbytes((ATTN_TILE, seq), F32) + 8 * _nbytes((ATTN_TILE, width), F32)
    return pl.pallas_call(
        functools.partial(_attn_kernel, seq=seq),
        grid=(batch, heads),
        in_specs=[pl.BlockSpec((seq, A_HEAD_PAD), head), pl.BlockSpec((seq, A_HEAD_PAD), head),
                  pl.BlockSpec((seq, V_PAD), head),
                  pl.BlockSpec((seq, d), head),
                  pl.BlockSpec((seq, d), lambda b, h: (b, heads + h)),
                  pl.BlockSpec((seq, d), head),
                  pl.BlockSpec((None, 1, width), lambda b, h: (h, 0, 0))],
        out_specs=[pl.BlockSpec((seq, A_V), head), pl.BlockSpec((seq, d), head)],
        out_shape=[jax.ShapeDtypeStruct((t, heads * A_V), BF16), jax.ShapeDtypeStruct((t, heads * d), BF16)],
        scratch_shapes=[pltpu.VMEM((seq, V_PAD), BF16)],
        compiler_params=_params(("parallel", "parallel"), blocks, temps),
        name="attention",
    )(q, k, v, bqk, bqk, bv, bias_rows)


def _merge_kernel(oa_ref, ob_ref, woa_ref, wob_ref, g0_ref, g1_ref, o_ref, woa_bf, wob_bf):
    @pl.when(pl.program_id(1) == 0)
    def _():
        woa_bf[...] = woa_ref[...].astype(BF16)
        wob_bf[...] = wob_ref[...].astype(BF16)

    a = jnp.dot(oa_ref[...], woa_bf[...], preferred_element_type=F32)
    b = jnp.dot(ob_ref[...], wob_bf[...], preferred_element_type=F32)
    o_ref[...] = (g0_ref[...].astype(F32) * a + g1_ref[...].astype(F32) * b).astype(o_ref.dtype)


def _merge(oa, ob, woa, wob, gates, tm, tn):
    m = oa.shape[0]
    d = woa.shape[1]
    g1 = d // tn
    blocks = (_nbytes((tm, oa.shape[1]), BF16) + _nbytes((tm, ob.shape[1]), BF16) + _nbytes((woa.shape[0], tn), F32)
              + _nbytes((wob.shape[0], tn), F32) + 3 * _nbytes((tm, tn), BF16))
    resident = (2 * _nbytes((woa.shape[0], tn), BF16) + 2 * _nbytes((wob.shape[0], tn), BF16)
                + 3 * _nbytes((tm, tn), F32))
    return pl.pallas_call(
        _merge_kernel,
        grid=(d // tn, m // tm),
        in_specs=[pl.BlockSpec((tm, oa.shape[1]), lambda j, i: (i, 0)),
                  pl.BlockSpec((tm, ob.shape[1]), lambda j, i: (i, 0)),
                  pl.BlockSpec((woa.shape[0], tn), lambda j, i: (0, j)),
                  pl.BlockSpec((wob.shape[0], tn), lambda j, i: (0, j)),
                  pl.BlockSpec((tm, tn), lambda j, i: (i, j)),
                  pl.BlockSpec((tm, tn), lambda j, i: (i, g1 + j))],
        out_specs=pl.BlockSpec((tm, tn), lambda j, i: (i, j)),
        out_shape=jax.ShapeDtypeStruct((m, d), BF16),
        scratch_shapes=[pltpu.VMEM((woa.shape[0], tn), BF16), pltpu.VMEM((wob.shape[0], tn), BF16)],
        compiler_params=_params(("parallel", "arbitrary"), blocks, resident),
        name="merge",
    )(oa, ob, woa, wob, gates, gates)


def _out_proj_kernel(a_ref, w_hbm, x_ref, o_ref, stage, w_bf, sem):
    j = pl.program_id(0)
    tn = w_bf.shape[1]

    def fetch(block):
        return pltpu.make_async_copy(w_hbm.at[:, pl.ds(pl.multiple_of(block * tn, LANE), tn)], stage, sem)

    @pl.when(pl.program_id(1) == 0)
    def _():
        @pl.when(j == 0)
        def _():
            fetch(j).start(priority=WEIGHT_DMA_PRIORITY)

        fetch(j).wait()
        w_bf[...] = stage[...].astype(BF16)

        @pl.when(j + 1 < pl.num_programs(0))
        def _():
            fetch(j + 1).start(priority=WEIGHT_DMA_PRIORITY)

    groups = _row_groups(a_ref.shape[0])
    accs = [jnp.dot(a_ref[sl, :], w_bf[...], preferred_element_type=F32) for sl in groups]
    for sl, acc in zip(groups, accs):
        o_ref[sl, :] = x_ref[sl, :] + acc


def _out_proj(merged, wout, x, tm, tn):
    m, k = merged.shape
    n = wout.shape[1]
    assert n % tn == 0
    blocks = _nbytes((tm, k), BF16) + 2 * _nbytes((tm, tn), F32)
    resident = _nbytes((k, tn), F32) + 2 * _nbytes((k, tn), BF16) + 2 * _nbytes((tm, tn), F32)
    return pl.pallas_call(
        _out_proj_kernel,
        grid=(n // tn, m // tm),
        in_specs=[pl.BlockSpec((tm, k), lambda j, i: (i, 0)),
                  pl.BlockSpec(memory_space=pl.ANY),
                  pl.BlockSpec((tm, tn), lambda j, i: (i, j))],
        out_specs=pl.BlockSpec((tm, tn), lambda j, i: (i, j)),
        out_shape=jax.ShapeDtypeStruct((m, n), F32),
        scratch_shapes=[pltpu.VMEM((k, tn), F32), pltpu.VMEM((k, tn), BF16), pltpu.SemaphoreType.DMA(())],
        compiler_params=_params(("arbitrary", "arbitrary"), blocks, resident),
        name="out_proj",
    )(merged, wout, x)


def _split_bf16(v):
    hi = v.astype(BF16)
    return hi, (v - hi.astype(F32)).astype(BF16)


def _router_kernel(x_ref, g_ref, wr_ref, xn_ref, route_ref, *, n_groups, per_group):
    x = x_ref[...]
    r = lax.rsqrt(jnp.mean(x * x, axis=-1, keepdims=True) + EPS)
    xn = x * r * g_ref[...]
    xn_ref[...] = xn
    x_hi, x_lo = _split_bf16(xn)
    w_hi, w_lo = _split_bf16(wr_ref[...])
    dot = functools.partial(jnp.dot, preferred_element_type=F32)
    logits = dot(x_hi, w_hi) + (dot(x_lo, w_hi) + dot(x_hi, w_lo))
    lane = lax.broadcasted_iota(jnp.int32, logits.shape, 1).astype(F32)
    far = float(LANE)

    def top(vals):
        best = jnp.max(vals, axis=-1, keepdims=True)
        return best, jnp.min(jnp.where(vals == best, lane, far), axis=-1, keepdims=True)

    gl = jnp.where(lane < n_groups, logits, NEG_INF)
    gmax, grp = top(gl)
    p_grp = 1.0 / jnp.sum(jnp.exp(gl - gmax), axis=-1, keepdims=True)
    lo = n_groups + grp * per_group
    el = jnp.where((lane >= lo) & (lane < lo + per_group), logits, NEG_INF)
    t1, i1 = top(el)
    t2, i2 = top(jnp.where(lane == i1, NEG_INF, el))
    d = jnp.exp(t2 - t1)
    w1 = p_grp / (1.0 + d)
    w2 = p_grp * d / (1.0 + d)
    route_ref[...] = jnp.where(lane == 0, i1 - n_groups,
                               jnp.where(lane == 1, i2 - n_groups,
                                         jnp.where(lane == 2, w1, jnp.where(lane == 3, w2, 0.0))))


def _router(x1, g, wr, n_groups, per_group, tm):
    t, d = x1.shape
    blocks = 2 * _nbytes((tm, d), F32) + _nbytes((1, d), F32) + _nbytes((d, LANE), F32) + _nbytes((tm, LANE), F32)
    return pl.pallas_call(
        functools.partial(_router_kernel, n_groups=n_groups, per_group=per_group),
        grid=(t // tm,),
        in_specs=[pl.BlockSpec((tm, d), lambda i: (i, 0)), pl.BlockSpec((1, d), lambda i: (0, 0)),
                  pl.BlockSpec((d, LANE), lambda i: (0, 0))],
        out_specs=[pl.BlockSpec((tm, d), lambda i: (i, 0)), pl.BlockSpec((tm, LANE), lambda i: (i, 0))],
        out_shape=[jax.ShapeDtypeStruct((t, d), F32), jax.ShapeDtypeStruct((t, LANE), F32)],
        compiler_params=_params(("parallel",), blocks, 2 * _nbytes((tm, d), F32)),
        name="router",
    )(x1, g.reshape(1, d), wr)


def _one_hots(route):
    lane = lax.broadcasted_iota(jnp.int32, route.shape, 1).astype(F32)
    return (lane == route[:, 0:1]).astype(F32), (lane == route[:, 1:2]).astype(F32)


def _rank_kernel(route_ref, rank_ref, starts_ref, count_acc, start_acc):
    i = pl.program_id(0)

    @pl.when(i == 0)
    def _():
        count_acc[...] = jnp.zeros_like(count_acc)
        start_acc[...] = jnp.zeros_like(start_acc)

    oh1, oh2 = _one_hots(route_ref[...])
    oh = (oh1 + oh2).astype(BF16)
    tm = oh.shape[0]
    earlier = (lax.broadcasted_iota(jnp.int32, (tm, tm), 0) > lax.broadcasted_iota(jnp.int32, (tm, tm), 1))
    before = jnp.dot(earlier.astype(BF16), oh, preferred_element_type=F32) + count_acc[...]
    lane = lax.broadcasted_iota(jnp.int32, (tm, LANE), 1)
    rank_ref[...] = jnp.where(lane == 0, jnp.sum(oh1 * before, axis=-1, keepdims=True),
                              jnp.where(lane == 1, jnp.sum(oh2 * before, axis=-1, keepdims=True), 0.0))
    lower = (lax.broadcasted_iota(jnp.int32, (LANE, LANE), 0) < lax.broadcasted_iota(jnp.int32, (LANE, LANE), 1))
    below = jnp.dot(oh, lower.astype(BF16), preferred_element_type=F32)
    count_acc[...] += jnp.sum(oh.astype(F32), axis=0, keepdims=True)
    start_acc[...] += jnp.sum(below, axis=0, keepdims=True)
    starts_ref[...] = start_acc[...]


def _rank(route, tm):
    t = route.shape[0]
    blocks = 2 * _nbytes((tm, LANE), F32) + _nbytes((1, LANE), F32)
    return pl.pallas_call(
        _rank_kernel,
        grid=(t // tm,),
        in_specs=[pl.BlockSpec((tm, LANE), lambda i: (i, 0))],
        out_specs=[pl.BlockSpec((tm, LANE), lambda i: (i, 0)), pl.BlockSpec((1, LANE), lambda i: (0, 0))],
        out_shape=[jax.ShapeDtypeStruct((t, LANE), F32), jax.ShapeDtypeStruct((1, LANE), F32)],
        scratch_shapes=[pltpu.VMEM((1, LANE), F32), pltpu.VMEM((1, LANE), F32)],
        compiler_params=_params(("arbitrary",), blocks, 2 * _nbytes((tm, tm), F32)),
        name="moe_rank",
    )(route)


def _dest_kernel(route_ref, rank_ref, starts_ref, dest_ref):
    oh1, oh2 = _one_hots(route_ref[...])
    rank = rank_ref[...]
    starts = starts_ref[...]
    d1 = jnp.sum(oh1 * starts, axis=-1, keepdims=True) + rank[:, 0:1]
    d2 = jnp.sum(oh2 * starts, axis=-1, keepdims=True) + rank[:, 1:2]
    lane = lax.broadcasted_iota(jnp.int32, rank.shape, 1)
    dest_ref[...] = jnp.where(lane == 0, d1, jnp.where(lane == 1, d2, 0.0)).astype(jnp.int32)


def _dest(route, rank, starts, tm):
    t = route.shape[0]
    blocks = 3 * _nbytes((tm, LANE), F32) + _nbytes((1, LANE), F32)
    return pl.pallas_call(
        _dest_kernel,
        grid=(t // tm,),
        in_specs=[pl.BlockSpec((tm, LANE), lambda i: (i, 0)), pl.BlockSpec((tm, LANE), lambda i: (i, 0)),
                  pl.BlockSpec((1, LANE), lambda i: (0, 0))],
        out_specs=pl.BlockSpec((tm, LANE), lambda i: (i, 0)),
        out_shape=jax.ShapeDtypeStruct((t, LANE), jnp.int32),
        compiler_params=_params(("parallel",), blocks, 4 * _nbytes((tm, LANE), F32)),
        name="moe_dest",
    )(route, rank, starts)


def _work_items(starts, n_rows):
    n_exp = starts.shape[0]
    n_blk = n_rows // MOE_ROWS
    total = jnp.full((1,), n_rows, jnp.int32)
    pts = jnp.concatenate([jnp.arange(n_blk, dtype=jnp.int32) * MOE_ROWS, starts[1:]])
    idx = jnp.arange(pts.shape[0], dtype=jnp.int32)
    before = (pts[None, :] < pts[:, None]) | ((pts[None, :] == pts[:, None]) & (idx[None, :] < idx[:, None]))
    pos = jnp.sum(before.astype(jnp.int32), axis=1)
    lo = jnp.sum(jnp.where(pos[:, None] == idx[None, :], pts[:, None], 0), axis=0)
    hi = jnp.concatenate([lo[1:], total])
    ends = jnp.concatenate([starts[1:], total])
    r = jnp.minimum(lo // MOE_ROWS, n_blk - 1)
    e = jnp.minimum(jnp.sum((ends[None, :] <= lo[:, None]).astype(jnp.int32), axis=1), n_exp - 1)
    later = jnp.where(e[None, :] > e[:, None], e[None, :], n_exp)
    nxt = jnp.min(later, axis=1)
    nxt = jnp.where(nxt == n_exp, -1, nxt)
    return r, e, lo, hi, nxt


def _stream_expert_weights(w, e_ref, nxt_ref, streams):
    e = e_ref[w]

    def copies(stream, expert):
        hbm, stage, _, sem = stream
        rows = stage.shape[0] // WEIGHT_DMA_PARTS
        parts = [pl.ds(part * rows, rows) for part in range(WEIGHT_DMA_PARTS)]
        return [pltpu.make_async_copy(hbm.at[expert, sl], stage.at[sl], sem) for sl in parts]

    @pl.when(w == 0)
    def _():
        for stream in streams:
            for c in copies(stream, e):
                c.start(priority=WEIGHT_DMA_PRIORITY)

    @pl.when((w == 0) | (e != e_ref[jnp.maximum(w - 1, 0)]))
    def _():
        nxt = nxt_ref[w]
        for stream in streams:
            for c in copies(stream, e):
                c.wait()
            _, stage, w_bf, _ = stream
            rows = stage.shape[0] // CONVERT_PARTS
            for part in range(CONVERT_PARTS):
                sl = slice(part * rows, (part + 1) * rows)
                w_bf[sl, :] = stage[sl, :].astype(BF16)

            @pl.when(nxt >= 0)
            def _():
                for c in copies(stream, nxt):
                    c.start(priority=WEIGHT_DMA_PRIORITY)


def _store_item_rows(o_ref, val, r, lo, hi):
    rows = r * MOE_ROWS + lax.broadcasted_iota(jnp.int32, val.shape, 0)
    mine = (rows >= lo) & (rows < hi)

    @pl.when(lo == r * MOE_ROWS)
    def _():
        o_ref[...] = val

    @pl.when(lo != r * MOE_ROWS)
    def _():
        pltpu.store(o_ref, val, mask=mine)


def _gather_block_rows(w, r_ref, dest_ref, x_hbm, source, rows_f32, x_bf, sem, *, n_assign):
    r = r_ref[w]
    n_blocks = n_assign // MOE_ROWS

    def copy(src_row, slot, j):
        return pltpu.make_async_copy(x_hbm.at[pl.ds(src_row, 1)], rows_f32.at[slot, pl.ds(j, 1)], sem.at[slot])

    def start_block(block, slot):
        for j in range(MOE_ROWS):
            copy(source[block * MOE_ROWS + j], slot, j).start()

    def finish_block(slot):
        for j in range(MOE_ROWS):
            copy(0, slot, j).wait()
        x_bf[...] = rows_f32[slot].astype(BF16)

    @pl.when(w == 0)
    def _():
        def invert(tok, carry):
            for k in range(TOP_K):
                source[dest_ref[TOP_K * tok + k]] = tok
            return carry
        lax.fori_loop(0, n_assign // TOP_K, invert, 0, unroll=4)
        start_block(r, 0)

    @pl.when((w == 0) | (r != r_ref[jnp.maximum(w - 1, 0)]))
    def _():
        for slot in range(2):
            @pl.when(r % 2 == slot)
            def _():
                finish_block(slot)

                @pl.when(r + 1 < n_blocks)
                def _():
                    start_block(r + 1, 1 - slot)


def _moe_experts_kernel(r_ref, e_ref, lo_ref, hi_ref, nxt_ref, dest_ref, x_hbm, wg_hbm, wu_hbm, wd_hbm, y_ref,
                        wg_stage, wu_stage, wd_stage, wg_bf, wu_bf, wd_bf, source, rows_f32, x_bf, sem, row_sem,
                        *, n_assign):
    w = pl.program_id(0)
    _stream_expert_weights(w, e_ref, nxt_ref, [(wg_hbm, wg_stage, wg_bf, sem.at[0]),
                                                 (wu_hbm, wu_stage, wu_bf, sem.at[1]),
                                                 (wd_hbm, wd_stage, wd_bf, sem.at[2])])
    _gather_block_rows(w, r_ref, dest_ref, x_hbm, source, rows_f32, x_bf, row_sem, n_assign=n_assign)
    r, lo, hi = r_ref[w], lo_ref[w], hi_ref[w]

    @pl.when(hi > lo)
    def _():
        x = x_bf[...]
        g = jnp.dot(x, wg_bf[...], preferred_element_type=F32)
        u = jnp.dot(x, wu_bf[...], preferred_element_type=F32)
        h = ((g * (1.0 / (1.0 + jnp.exp(-g)))) * u).astype(BF16)
        y = jnp.dot(h, wd_bf[...], preferred_element_type=F32)
        _store_item_rows(y_ref, y, r, lo, hi)


def _moe_experts(items, dest_flat, xn, wg, wu, wd):
    n_assign = dest_flat.shape[0]
    d = xn.shape[1]
    f = wg.shape[2]
    n_items = items[0].shape[0]
    blocks = _nbytes((MOE_ROWS, d), F32)
    resident = (3 * _nbytes((d, f), F32) + 3 * _nbytes((d, f), BF16) + 5 * _nbytes((MOE_ROWS, d), F32)
                + _nbytes((MOE_ROWS, d), BF16) + 6 * _nbytes((MOE_ROWS, f), F32))
    any_space = pl.BlockSpec(memory_space=pl.ANY)
    return pl.pallas_call(
        functools.partial(_moe_experts_kernel, n_assign=n_assign),
        grid_spec=pltpu.PrefetchScalarGridSpec(
            num_scalar_prefetch=6,
            grid=(n_items,),
            in_specs=[any_space, any_space, any_space, any_space],
            out_specs=pl.BlockSpec((MOE_ROWS, d), lambda w, r, e, lo, hi, nxt, dest: (r[w], 0)),
            scratch_shapes=[pltpu.VMEM((d, f), F32), pltpu.VMEM((d, f), F32), pltpu.VMEM((f, d), F32),
                            pltpu.VMEM((d, f), BF16), pltpu.VMEM((d, f), BF16), pltpu.VMEM((f, d), BF16),
                            pltpu.SMEM((n_assign,), jnp.int32), pltpu.VMEM((2, MOE_ROWS, d), F32),
                            pltpu.VMEM((MOE_ROWS, d), BF16),
                            pltpu.SemaphoreType.DMA((3,)), pltpu.SemaphoreType.DMA((2,))]),
        out_shape=jax.ShapeDtypeStruct((n_assign, d), F32),
        compiler_params=_params(("arbitrary",), blocks, resident),
        name="moe_experts",
    )(*items, dest_flat, xn, wg, wu, wd)


def _combine_kernel(dest_ref, x_ref, route_ref, y_hbm, o_ref, ybuf, sem, *, tm):
    half = tm // 2
    i = pl.program_id(0)

    def copy(src_row, slot, k, t):
        return pltpu.make_async_copy(y_hbm.at[pl.ds(src_row, 1)], ybuf.at[slot, k, pl.ds(t, 1)], sem.at[slot])

    def start_half(index, slot):
        for t in range(half):
            for k in range(TOP_K):
                copy(dest_ref[TOP_K * (index * half + t) + k], slot, k, t).start(priority=ROW_DMA_PRIORITY)

    def finish_half(slot):
        for t in range(half):
            for k in range(TOP_K):
                copy(0, slot, k, t).wait()
        rows = slice(slot * half, (slot + 1) * half)
        route = route_ref[rows, :]
        o_ref[rows, :] = x_ref[rows, :] + (route[:, 2:3] * ybuf[slot, 0] + route[:, 3:4] * ybuf[slot, 1])

    @pl.when(i == 0)
    def _():
        start_half(2 * i, 0)

    start_half(2 * i + 1, 1)
    finish_half(0)

    @pl.when(i + 1 < pl.num_programs(0))
    def _():
        start_half(2 * i + 2, 0)

    finish_half(1)


def _combine(dest_flat, x1, route, y, tm):
    t, d = x1.shape
    blocks = 2 * _nbytes((tm, d), F32) + _nbytes((tm, LANE), F32)
    resident = TOP_K * _nbytes((tm, d), F32) + _nbytes((tm, d), F32)
    return pl.pallas_call(
        functools.partial(_combine_kernel, tm=tm),
        grid_spec=pltpu.PrefetchScalarGridSpec(
            num_scalar_prefetch=1,
            grid=(t // tm,),
            in_specs=[pl.BlockSpec((tm, d), lambda i, dest: (i, 0)),
                      pl.BlockSpec((tm, LANE), lambda i, dest: (i, 0)),
                      pl.BlockSpec(memory_space=pl.ANY)],
            out_specs=pl.BlockSpec((tm, d), lambda i, dest: (i, 0)),
            scratch_shapes=[pltpu.VMEM((2, TOP_K, tm // 2, d), F32), pltpu.SemaphoreType.DMA((2,))]),
        out_shape=jax.ShapeDtypeStruct((t, d), F32),
        compiler_params=_params(("arbitrary",), blocks, resident),
        name="moe_combine",
    )(dest_flat, x1, route, y)


def _pad_cols(w, n):
    return jnp.pad(w, ((0, 0), (0, n - w.shape[1])))


def kernel(x, positions, g_mix, w_in, b_gate, q_norm_g, kv_norm_g, w_uq, w_ukv, a_q_norm_g, a_k_norm_g,
           b_q_norm_g, b_k_norm_g, rel_bias, w_o_a, w_o_b, w_out, g_ffn, w_group, w_expert,
           w_exp_gate, w_exp_up, w_exp_down):
    batch, seq, d = x.shape
    t = batch * seq
    q_lora, kv_lora = q_norm_g.shape[0], kv_norm_g.shape[0]
    a_heads = w_uq.shape[1] // A_QK
    b_heads = w_o_b.shape[0] // B_HEAD_DIM
    b_width = b_heads * B_HEAD_DIM
    n_groups, n_experts = w_group.shape[1], w_expert.shape[1]
    per_group = n_experts // n_groups
    off_b = q_lora + kv_lora + A_ROPE
    assert seq % ATTN_TILE == 0 and (TOP_K * t) % MOE_ROWS == 0 and n_groups + n_experts <= LANE
    assert a_heads == b_heads

    xf = x.reshape(t, d)
    tm_big = min(1024, t)
    tn = _tile(b_width, 512)
    tn_wide = _tile(b_width, 1024)
    assert d % tn == 0 and d % tn_wide == 0

    za_cols = -(-(q_lora + kv_lora + LANE) // tn_wide) * tn_wide
    wuq = jnp.pad(w_uq.reshape(q_lora, a_heads, A_QK), ((0, 0), (0, 0), (0, A_HEAD_PAD - A_QK)))
    wuq = wuq.reshape(q_lora, a_heads * A_HEAD_PAD).astype(BF16)
    wukv = w_ukv.astype(BF16)
    pad_gain = lambda g, s: jnp.pad(g * s, (0, A_HEAD_PAD - A_QK)).reshape(1, A_HEAD_PAD)
    gaq = pad_gain(a_q_norm_g, A_QK ** -0.5)
    gak = pad_gain(a_k_norm_g, 1.0)
    g_bqk = jnp.concatenate([jnp.tile(b_q_norm_g * B_HEAD_DIM ** -0.5, b_heads), jnp.tile(b_k_norm_g, b_heads)])

    half = A_ROPE // 2
    inv = ROPE_THETA ** (-jnp.arange(half, dtype=F32) / half)
    ang = positions.astype(F32).reshape(t, 1) * inv
    cos, sin = jnp.cos(ang), jnp.sin(ang)
    zeros = jnp.zeros((t, half), F32)
    cos_t = jnp.concatenate([cos, cos, zeros, zeros], axis=1)
    sin_lo = jnp.concatenate([-sin, zeros, zeros, zeros], axis=1)
    sin_hi = jnp.concatenate([zeros, sin, zeros, zeros], axis=1)

    xn = _rmsnorm_rows(xf, g_mix, BF16, min(256, t))
    w_in_t = w_in.T
    za = _inproj_cols(xn, w_in_t, 0, za_cols, None, _plain_epilogue, tm_big, tn_wide)
    bqk = _inproj_cols(xn, w_in_t, off_b, 2 * b_width, g_bqk, _head_norm_epilogue, tm_big, tn_wide)
    bv = _inproj_cols(xn, w_in_t, off_b + 2 * b_width, b_width, None, _plain_epilogue, tm_big, tn_wide)
    gates = _inproj_cols(xn, w_in_t, off_b + 3 * b_width, 2 * d, b_gate, _sigmoid_epilogue, tm_big, tn_wide)
    q, k, v = _mla_proj(za, cos_t, sin_lo, sin_hi, wuq, wukv, q_norm_g.reshape(1, -1), kv_norm_g.reshape(1, -1),
                        gaq, gak, a_heads, min(256, t))
    o_a, o_b = _attention(q, k, v, bqk, bv, _band_bias_rows(rel_bias), batch, seq, a_heads)
    merged = _merge(o_a, o_b, w_o_a, w_o_b, gates, tm_big, tn)
    x1 = _out_proj(merged, w_out, xf, min(512, t), tn_wide)

    wr = _pad_cols(jnp.concatenate([w_group, w_expert], axis=1), LANE)
    xn2, route = _router(x1, g_ffn, wr, n_groups, per_group, min(256, t))
    rank, starts_f = _rank(route, min(512, t))
    dest = _dest(route, rank, starts_f, min(512, t))[:, :TOP_K].reshape(-1)
    items = _work_items(starts_f[0, :n_experts].astype(jnp.int32), TOP_K * t)
    y = _moe_experts(items, dest, xn2, w_exp_gate, w_exp_up, w_exp_down)
    out = _combine(dest, x1, route, y, min(256, t))
    return out.reshape(batch, seq, d)
```

```python
import functools

import jax
import jax.numpy as jnp
from jax import lax
from jax.experimental import pallas as pl
from jax.experimental.pallas import tpu as pltpu

F32 = jnp.float32
BF16 = jnp.bfloat16

CHUNK = 64
EPS = 1e-6
A_NOPE = 128
A_ROPE = 64
A_V = 128
A_QK = A_NOPE + A_ROPE
B_HEAD_DIM = 128
B_LEFT_CHUNKS = 8
B_MAX_REL = 128
ROPE_THETA = 10000.0
TOP_K = 2

LANE = 128
A_HEAD_PAD = 2 * LANE
V_PAD = 2 * LANE
V7X_VMEM_BYTES = 64 * 2**20

ATTN_TILE = 256
MOE_ROWS = 128
WEIGHT_DMA_PARTS = 4
WEIGHT_DMA_PRIORITY = 1
ROW_DMA_PRIORITY = 1
CONVERT_PARTS = 8
PROJ_GROUP_ROWS = 256
NEG_INF = float("-inf")


def _nbytes(shape, dtype):
    n = 1
    for s in shape:
        n *= s
    return n * jnp.dtype(dtype).itemsize


def _params(semantics, pipelined_bytes, resident_bytes=0):
    need = 2 * pipelined_bytes + resident_bytes
    return pltpu.CompilerParams(dimension_semantics=semantics,
                                vmem_limit_bytes=min(int(need), V7X_VMEM_BYTES))


def _tile(n, want):
    t = want
    while t > LANE and n % t:
        t //= 2
    assert n % t == 0, (n, want)
    return t


def _rmsnorm_kernel(x_ref, g_ref, o_ref):
    x = x_ref[...]
    r = lax.rsqrt(jnp.mean(x * x, axis=-1, keepdims=True) + EPS)
    o_ref[...] = (x * r * g_ref[...]).astype(o_ref.dtype)


def _rmsnorm_rows(x, g, out_dtype, tm):
    t, d = x.shape
    blocks = _nbytes((tm, d), F32) + _nbytes((tm, d), out_dtype) + _nbytes((1, d), F32)
    return pl.pallas_call(
        _rmsnorm_kernel,
        grid=(t // tm,),
        in_specs=[pl.BlockSpec((tm, d), lambda i: (i, 0)), pl.BlockSpec((1, d), lambda i: (0, 0))],
        out_specs=pl.BlockSpec((tm, d), lambda i: (i, 0)),
        out_shape=jax.ShapeDtypeStruct((t, d), out_dtype),
        compiler_params=_params(("parallel",), blocks, _nbytes((tm, d), F32)),
        name="rmsnorm",
    )(x, g.reshape(1, d))


def _nt_dot(a, b):
    return lax.dot_general(a, b, (((1,), (1,)), ((), ())), preferred_element_type=F32)


def _stage_weight_blocks(streams):
    j = pl.program_id(0)

    @pl.when(pl.program_id(1) == 0)
    def _():
        @pl.when(j == 0)
        def _():
            for fetch, _, _ in streams:
                fetch(j).start(priority=WEIGHT_DMA_PRIORITY)

        for fetch, stage, w_bf in streams:
            fetch(j).wait()
            w_bf[...] = stage[...].astype(BF16)

            @pl.when(j + 1 < pl.num_programs(0))
            def _():
                fetch(j + 1).start(priority=WEIGHT_DMA_PRIORITY)


def _row_groups(rows):
    step = min(rows, PROJ_GROUP_ROWS)
    assert rows % step == 0
    return [slice(s, s + step) for s in range(0, rows, step)]


def _head_norm_epilogue(acc, gb_ref):
    heads = []
    for h in range(acc.shape[1] // B_HEAD_DIM):
        sl = slice(h * B_HEAD_DIM, (h + 1) * B_HEAD_DIM)
        z = acc[:, sl]
        r = lax.rsqrt(jnp.mean(z * z, axis=-1, keepdims=True) + EPS)
        heads.append(z * r * gb_ref[:, sl])
    return jnp.concatenate(heads, axis=1)


def _plain_epilogue(acc, gb_ref):
    del gb_ref
    return acc


def _sigmoid_epilogue(acc, gb_ref):
    return 1.0 / (1.0 + jnp.exp(-(acc + gb_ref[...])))


def _inproj_cols_kernel(a_ref, wt_hbm, *refs, first_col, epilogue):
    gb_ref = refs[0] if len(refs) == 5 else None
    o_ref, stage, wt_bf, sem = refs[-4:]
    tn = wt_bf.shape[0]

    def fetch(block):
        rows = pl.ds(pl.multiple_of(first_col + block * tn, 8), tn)
        return pltpu.make_async_copy(wt_hbm.at[rows], stage, sem)

    _stage_weight_blocks([(fetch, stage, wt_bf)])
    groups = _row_groups(a_ref.shape[0])
    accs = [_nt_dot(a_ref[sl, :], wt_bf[...]) for sl in groups]
    for sl, acc in zip(groups, accs):
        o_ref[sl, :] = epilogue(acc, gb_ref).astype(o_ref.dtype)


def _inproj_cols(xn, w_in_t, first_col, n_cols, gb, epilogue, tm, tn):
    m, k = xn.shape
    assert n_cols % tn == 0 and first_col % 8 == 0
    blocks = _nbytes((tm, k), BF16) + _nbytes((tm, tn), BF16) + _nbytes((1, tn), F32)
    resident = _nbytes((tn, k), F32) + _nbytes((tn, k), BF16) + 3 * _nbytes((tm, tn), F32)
    gb_specs = [] if gb is None else [pl.BlockSpec((1, tn), lambda j, i: (0, j))]
    gb_args = [] if gb is None else [gb.reshape(1, n_cols)]
    return pl.pallas_call(
        functools.partial(_inproj_cols_kernel, first_col=first_col, epilogue=epilogue),
        grid=(n_cols // tn, m // tm),
        in_specs=[pl.BlockSpec((tm, k), lambda j, i: (i, 0)), pl.BlockSpec(memory_space=pl.ANY)] + gb_specs,
        out_specs=pl.BlockSpec((tm, tn), lambda j, i: (i, j)),
        out_shape=jax.ShapeDtypeStruct((m, n_cols), BF16),
        scratch_shapes=[pltpu.VMEM((tn, k), F32), pltpu.VMEM((tn, k), BF16), pltpu.SemaphoreType.DMA(())],
        compiler_params=_params(("arbitrary", "arbitrary"), blocks, resident),
        name="inproj_" + epilogue.__name__.strip("_").replace("_epilogue", ""),
    )(xn, w_in_t, *gb_args)


def _rope_padded(v, cos, sin_lo, sin_hi):
    half = A_ROPE // 2
    return v * cos + pltpu.roll(v, LANE - half, 1) * sin_lo + pltpu.roll(v, half, 1) * sin_hi


def _ones_column(rows):
    lane = lax.broadcasted_iota(jnp.int32, (rows, V_PAD - A_V), 1)
    return jnp.where(lane == 0, 1.0, 0.0).astype(BF16)


def _mla_proj_kernel(za_ref, cos_ref, sl_ref, sh_ref, wuq_ref, wukv_ref, gq_ref, gkv_ref, gaq_ref, gak_ref,
                     q_ref, k_ref, v_ref, *, heads, q_lora, kv_lora):
    cos, sin_lo, sin_hi = cos_ref[...], sl_ref[...], sh_ref[...]

    def norm(z, g):
        z = z.astype(F32)
        r = lax.rsqrt(jnp.mean(z * z, axis=-1, keepdims=True) + EPS)
        return (z * r * g).astype(BF16)

    cq = norm(za_ref[:, :q_lora], gq_ref[...])
    ckv = norm(za_ref[:, q_lora:q_lora + kv_lora], gkv_ref[...])
    slab = za_ref[:, q_lora + kv_lora:q_lora + kv_lora + LANE].astype(F32)
    k_rope = jnp.where(lax.broadcasted_iota(jnp.int32, slab.shape, 1) < A_ROPE, slab, 0.0)
    qacc = jnp.dot(cq, wuq_ref[...], preferred_element_type=F32)
    kvacc = jnp.dot(ckv, wukv_ref[...], preferred_element_type=F32)

    gq_lo, gq_hi = gaq_ref[:, :LANE], gaq_ref[:, LANE:]
    gk_lo, gk_hi = gak_ref[:, :LANE], gak_ref[:, LANE:]
    kr_ss = jnp.sum(k_rope * k_rope, axis=-1, keepdims=True)
    kr_rot = _rope_padded(k_rope * gk_hi, cos, sin_lo, sin_hi)
    ones_col = _ones_column(slab.shape[0])
    for h in range(heads):
        base = h * A_HEAD_PAD
        q_lo = qacc[:, base:base + LANE]
        q_hi = qacc[:, base + LANE:base + A_HEAD_PAD]
        ss = jnp.sum(q_lo * q_lo, axis=-1, keepdims=True) + jnp.sum(q_hi * q_hi, axis=-1, keepdims=True)
        r = lax.rsqrt(ss / A_QK + EPS)
        q_ref[:, base:base + LANE] = (q_lo * r * gq_lo).astype(BF16)
        q_ref[:, base + LANE:base + A_HEAD_PAD] = _rope_padded(q_hi * r * gq_hi, cos, sin_lo, sin_hi).astype(BF16)

        k_lo = kvacc[:, base:base + LANE]
        ssk = jnp.sum(k_lo * k_lo, axis=-1, keepdims=True) + kr_ss
        rk = lax.rsqrt(ssk / A_QK + EPS)
        k_ref[:, base:base + LANE] = (k_lo * rk * gk_lo).astype(BF16)
        k_ref[:, base + LANE:base + A_HEAD_PAD] = (kr_rot * rk).astype(BF16)
        v_ref[:, h * V_PAD:h * V_PAD + A_V] = kvacc[:, base + LANE:base + A_HEAD_PAD].astype(BF16)
        v_ref[:, h * V_PAD + A_V:(h + 1) * V_PAD] = ones_col


def _mla_proj(za, cos, sin_lo, sin_hi, wuq, wukv, gq, gkv, gaq, gak, heads, tm):
    t, za_cols = za.shape
    q_lora, kv_lora = wuq.shape[0], wukv.shape[0]
    hp = heads * A_HEAD_PAD
    row = lambda i: (i, 0)
    fix = lambda i: (0, 0)
    blocks = (_nbytes((tm, za_cols), za.dtype) + 3 * _nbytes((tm, LANE), F32) + _nbytes(wuq.shape, BF16)
              + _nbytes(wukv.shape, BF16) + 2 * _nbytes((tm, hp), BF16) + _nbytes((tm, heads * V_PAD), BF16))
    return pl.pallas_call(
        functools.partial(_mla_proj_kernel, heads=heads, q_lora=q_lora, kv_lora=kv_lora),
        grid=(t // tm,),
        in_specs=[pl.BlockSpec((tm, za_cols), row),
                  pl.BlockSpec((tm, LANE), row), pl.BlockSpec((tm, LANE), row), pl.BlockSpec((tm, LANE), row),
                  pl.BlockSpec(wuq.shape, fix), pl.BlockSpec(wukv.shape, fix),
                  pl.BlockSpec((1, q_lora), fix), pl.BlockSpec((1, kv_lora), fix),
                  pl.BlockSpec((1, A_HEAD_PAD), fix), pl.BlockSpec((1, A_HEAD_PAD), fix)],
        out_specs=[pl.BlockSpec((tm, hp), row), pl.BlockSpec((tm, hp), row), pl.BlockSpec((tm, heads * V_PAD), row)],
        out_shape=[jax.ShapeDtypeStruct((t, hp), BF16), jax.ShapeDtypeStruct((t, hp), BF16),
                   jax.ShapeDtypeStruct((t, heads * V_PAD), BF16)],
        compiler_params=_params(("parallel",), blocks, 3 * _nbytes((tm, hp), F32)),
        name="mla_proj",
    )(za, cos, sin_lo, sin_hi, wuq, wukv, gq, gkv, gaq, gak)


def _mla_scores(i, q_ref, k_ref, diag_ok):
    tq = ATTN_TILE
    q = q_ref[i * tq:(i + 1) * tq, :]
    sd = jnp.where(diag_ok, _nt_dot(q, k_ref[i * tq:(i + 1) * tq, :]), NEG_INF)
    s0 = _nt_dot(q, k_ref[:i * tq, :]) if i > 0 else None
    return sd, s0


def _mla_softmax(sd, s0):
    m = jnp.max(sd, axis=-1, keepdims=True)
    if s0 is None:
        return jnp.exp(sd - m).astype(BF16), None
    m = jnp.maximum(m, jnp.max(s0, axis=-1, keepdims=True))
    return jnp.exp(sd - m).astype(BF16), jnp.exp(s0 - m).astype(BF16)


def _mla_values(i, pd, p0, v_ref, o_ref):
    tq = ATTN_TILE
    o = jnp.dot(pd, v_ref[i * tq:(i + 1) * tq, :], preferred_element_type=F32)
    if p0 is not None:
        o = o + jnp.dot(p0, v_ref[:i * tq, :], preferred_element_type=F32)
    o_ref[i * tq:(i + 1) * tq, :] = (o[:, :A_V] / o[:, A_V:A_V + 1]).astype(o_ref.dtype)


def _band_window(i):
    left = B_LEFT_CHUNKS * CHUNK
    q0 = i * ATTN_TILE
    k0 = max(0, q0 - left)
    return q0, k0, q0 + ATTN_TILE - k0, left - q0 + k0


def _band_scores(i, q_ref, k_ref, table):
    q0, k0, kw, u0 = _band_window(i)
    return _nt_dot(q_ref[q0:q0 + ATTN_TILE, :], k_ref[k0:k0 + kw, :]) + table[:, u0:u0 + kw]


def _band_softmax(s):
    return jnp.exp(s - jnp.max(s, axis=-1, keepdims=True)).astype(BF16)


def _band_values(i, p, v_ref, o_ref):
    q0, k0, kw, _ = _band_window(i)
    o = jnp.dot(p, v_ref[k0:k0 + kw, :], preferred_element_type=F32)
    o_ref[q0:q0 + ATTN_TILE, :] = (o[:, :B_HEAD_DIM] / o[:, B_HEAD_DIM:B_HEAD_DIM + 1]).astype(o_ref.dtype)


def _attn_kernel(aq_ref, ak_ref, av_ref, bq_ref, bk_ref, bv_ref, r_ref, oa_ref, ob_ref, bv_pad, *, seq):
    tq = ATTN_TILE
    left = B_LEFT_CHUNKS * CHUNK
    width = r_ref.shape[1]
    rc = lax.broadcasted_iota(jnp.int32, (tq, tq), 0) // CHUNK
    cc = lax.broadcasted_iota(jnp.int32, (tq, tq), 1) // CHUNK
    diag_ok = cc <= rc
    bias = pltpu.roll(jnp.broadcast_to(r_ref[...], (tq, width)), 0, 1, stride=1, stride_axis=0)[:, :left + tq]
    q_chunk = lax.broadcasted_iota(jnp.int32, (tq, left + tq), 0) // CHUNK
    k_chunk = lax.broadcasted_iota(jnp.int32, (tq, left + tq), 1) // CHUNK
    table = jnp.where((k_chunk >= q_chunk) & (k_chunk <= q_chunk + B_LEFT_CHUNKS), bias, NEG_INF)
    bv_pad[:, :B_HEAD_DIM] = bv_ref[...]
    bv_pad[:, B_HEAD_DIM:] = _ones_column(seq)

    n_tiles = seq // tq
    order = [n_tiles - 1 - i // 2 if i % 2 == 0 else i // 2 for i in range(n_tiles)]
    sd, s0 = _mla_scores(order[0], aq_ref, ak_ref, diag_ok)
    for n, i in enumerate(order):
        sb = _band_scores(i, bq_ref, bk_ref, table)
        pd, p0 = _mla_softmax(sd, s0)
        if n + 1 < n_tiles:
            sd, s0 = _mla_scores(order[n + 1], aq_ref, ak_ref, diag_ok)
        _mla_values(i, pd, p0, av_ref, oa_ref)
        _band_values(i, _band_softmax(sb), bv_pad, ob_ref)


def _band_bias_rows(rel_bias):
    left = B_LEFT_CHUNKS * CHUNK
    width = left + 2 * ATTN_TILE
    m = jnp.arange(width, dtype=jnp.int32)
    j = jnp.where(m < left + ATTN_TILE, m, m - width)
    dist = left - j
    rows = rel_bias[:, jnp.clip(dist, -B_MAX_REL, B_MAX_REL) + B_MAX_REL].astype(F32)
    return rows.reshape(rel_bias.shape[0], 1, width)


def _attention(q, k, v, bqk, bv, bias_rows, batch, seq, heads):
    t = batch * seq
    d = B_HEAD_DIM
    width = bias_rows.shape[2]
    head = lambda b, h: (b, h)
    blocks = (2 * _nbytes((seq, A_HEAD_PAD), BF16) + _nbytes((seq, V_PAD), BF16) + _nbytes((seq, A_V), BF16)
              + 4 * _nbytes((seq, d), BF16) + _nbytes((1, width), F32))
    temps = _nbytes((seq, V_PAD), BF16) + 8 * _nbytes((ATTN_TILE, seq), F32) + 8 * _nbytes((ATTN_TILE, width), F32)
    return pl.pallas_call(
        functools.partial(_attn_kernel, seq=seq),
        grid=(batch, heads),
        in_specs=[pl.BlockSpec((seq, A_HEAD_PAD), head), pl.BlockSpec((seq, A_HEAD_PAD), head),
                  pl.BlockSpec((seq, V_PAD), head),
                  pl.BlockSpec((seq, d), head),
                  pl.BlockSpec((seq, d), lambda b, h: (b, heads + h)),
                  pl.BlockSpec((seq, d), head),
                  pl.BlockSpec((None, 1, width), lambda b, h: (h, 0, 0))],
        out_specs=[pl.BlockSpec((seq, A_V), head), pl.BlockSpec((seq, d), head)],
        out_shape=[jax.ShapeDtypeStruct((t, heads * A_V), BF16), jax.ShapeDtypeStruct((t, heads * d), BF16)],
        scratch_shapes=[pltpu.VMEM((seq, V_PAD), BF16)],
        compiler_params=_params(("parallel", "parallel"), blocks, temps),
        name="attention",
    )(q, k, v, bqk, bqk, bv, bias_rows)


def _column_block_fetch(w_hbm, stage, sem):
    tn = stage.shape[1]
    return lambda block: pltpu.make_async_copy(w_hbm.at[:, pl.ds(pl.multiple_of(block * tn, LANE), tn)], stage, sem)


def _merge_kernel(oa_ref, ob_ref, woa_hbm, wob_hbm, g0_ref, g1_ref, o_ref, a_stage, b_stage, woa_bf, wob_bf, sem):
    _stage_weight_blocks([(_column_block_fetch(woa_hbm, a_stage, sem.at[0]), a_stage, woa_bf),
                          (_column_block_fetch(wob_hbm, b_stage, sem.at[1]), b_stage, wob_bf)])
    for sl in _row_groups(oa_ref.shape[0]):
        a = jnp.dot(oa_ref[sl, :], woa_bf[...], preferred_element_type=F32)
        b = jnp.dot(ob_ref[sl, :], wob_bf[...], preferred_element_type=F32)
        o_ref[sl, :] = (g0_ref[sl, :].astype(F32) * a + g1_ref[sl, :].astype(F32) * b).astype(o_ref.dtype)


def _merge(oa, ob, woa, wob, gates, tm, tn):
    m = oa.shape[0]
    ka, kb = woa.shape[0], wob.shape[0]
    d = woa.shape[1]
    assert d % tn == 0
    g1 = d // tn
    blocks = _nbytes((tm, ka), BF16) + _nbytes((tm, kb), BF16) + 3 * _nbytes((tm, tn), BF16)
    resident = (_nbytes((ka + kb, tn), F32) + 2 * _nbytes((ka + kb, tn), BF16) + 3 * _nbytes((tm, tn), F32))
    return pl.pallas_call(
        _merge_kernel,
        grid=(d // tn, m // tm),
        in_specs=[pl.BlockSpec((tm, ka), lambda j, i: (i, 0)),
                  pl.BlockSpec((tm, kb), lambda j, i: (i, 0)),
                  pl.BlockSpec(memory_space=pl.ANY), pl.BlockSpec(memory_space=pl.ANY),
                  pl.BlockSpec((tm, tn), lambda j, i: (i, j)),
                  pl.BlockSpec((tm, tn), lambda j, i: (i, g1 + j))],
        out_specs=pl.BlockSpec((tm, tn), lambda j, i: (i, j)),
        out_shape=jax.ShapeDtypeStruct((m, d), BF16),
        scratch_shapes=[pltpu.VMEM((ka, tn), F32), pltpu.VMEM((kb, tn), F32),
                        pltpu.VMEM((ka, tn), BF16), pltpu.VMEM((kb, tn), BF16), pltpu.SemaphoreType.DMA((2,))],
        compiler_params=_params(("arbitrary", "arbitrary"), blocks, resident),
        name="merge",
    )(oa, ob, woa, wob, gates, gates)


def _out_proj_kernel(a_ref, w_hbm, x_ref, o_ref, stage, w_bf, sem):
    _stage_weight_blocks([(_column_block_fetch(w_hbm, stage, sem), stage, w_bf)])
    groups = _row_groups(a_ref.shape[0])
    accs = [jnp.dot(a_ref[sl, :], w_bf[...], preferred_element_type=F32) for sl in groups]
    for sl, acc in zip(groups, accs):
        o_ref[sl, :] = x_ref[sl, :] + acc


def _out_proj(merged, wout, x, tm, tn):
    m, k = merged.shape
    n = wout.shape[1]
    assert n % tn == 0
    blocks = _nbytes((tm, k), BF16) + 2 * _nbytes((tm, tn), F32)
    resident = _nbytes((k, tn), F32) + 2 * _nbytes((k, tn), BF16) + 2 * _nbytes((tm, tn), F32)
    return pl.pallas_call(
        _out_proj_kernel,
        grid=(n // tn, m // tm),
        in_specs=[pl.BlockSpec((tm, k), lambda j, i: (i, 0)),
                  pl.BlockSpec(memory_space=pl.ANY),
                  pl.BlockSpec((tm, tn), lambda j, i: (i, j))],
        out_specs=pl.BlockSpec((tm, tn), lambda j, i: (i, j)),
        out_shape=jax.ShapeDtypeStruct((m, n), F32),
        scratch_shapes=[pltpu.VMEM((k, tn), F32), pltpu.VMEM((k, tn), BF16), pltpu.SemaphoreType.DMA(())],
        compiler_params=_params(("arbitrary", "arbitrary"), blocks, resident),
        name="out_proj",
    )(merged, wout, x)


def _split_bf16(v):
    hi = v.astype(BF16)
    return hi, (v - hi.astype(F32)).astype(BF16)


def _router_kernel(x_ref, g_ref, wr_ref, xn_ref, route_ref, *, n_groups, per_group):
    x = x_ref[...]
    r = lax.rsqrt(jnp.mean(x * x, axis=-1, keepdims=True) + EPS)
    xn = x * r * g_ref[...]
    xn_ref[...] = xn
    x_hi, x_lo = _split_bf16(xn)
    w_hi, w_lo = _split_bf16(wr_ref[...])
    dot = functools.partial(jnp.dot, preferred_element_type=F32)
    logits = dot(x_hi, w_hi) + (dot(x_lo, w_hi) + dot(x_hi, w_lo))
    lane = lax.broadcasted_iota(jnp.int32, logits.shape, 1).astype(F32)
    far = float(LANE)

    def top(vals):
        best = jnp.max(vals, axis=-1, keepdims=True)
        return best, jnp.min(jnp.where(vals == best, lane, far), axis=-1, keepdims=True)

    gl = jnp.where(lane < n_groups, logits, NEG_INF)
    gmax, grp = top(gl)
    p_grp = 1.0 / jnp.sum(jnp.exp(gl - gmax), axis=-1, keepdims=True)
    lo = n_groups + grp * per_group
    el = jnp.where((lane >= lo) & (lane < lo + per_group), logits, NEG_INF)
    t1, i1 = top(el)
    t2, i2 = top(jnp.where(lane == i1, NEG_INF, el))
    d = jnp.exp(t2 - t1)
    w1 = p_grp / (1.0 + d)
    w2 = p_grp * d / (1.0 + d)
    route_ref[...] = jnp.where(lane == 0, i1 - n_groups,
                               jnp.where(lane == 1, i2 - n_groups,
                                         jnp.where(lane == 2, w1, jnp.where(lane == 3, w2, 0.0))))


def _router(x1, g, wr, n_groups, per_group, tm):
    t, d = x1.shape
    blocks = 2 * _nbytes((tm, d), F32) + _nbytes((1, d), F32) + _nbytes((d, LANE), F32) + _nbytes((tm, LANE), F32)
    return pl.pallas_call(
        functools.partial(_router_kernel, n_groups=n_groups, per_group=per_group),
        grid=(t // tm,),
        in_specs=[pl.BlockSpec((tm, d), lambda i: (i, 0)), pl.BlockSpec((1, d), lambda i: (0, 0)),
                  pl.BlockSpec((d, LANE), lambda i: (0, 0))],
        out_specs=[pl.BlockSpec((tm, d), lambda i: (i, 0)), pl.BlockSpec((tm, LANE), lambda i: (i, 0))],
        out_shape=[jax.ShapeDtypeStruct((t, d), F32), jax.ShapeDtypeStruct((t, LANE), F32)],
        compiler_params=_params(("parallel",), blocks, 2 * _nbytes((tm, d), F32)),
        name="router",
    )(x1, g.reshape(1, d), wr)


def _one_hots(route):
    lane = lax.broadcasted_iota(jnp.int32, route.shape, 1).astype(F32)
    return (lane == route[:, 0:1]).astype(F32), (lane == route[:, 1:2]).astype(F32)


def _rank_kernel(route_ref, rank_ref, starts_ref, count_acc, start_acc):
    i = pl.program_id(0)

    @pl.when(i == 0)
    def _():
        count_acc[...] = jnp.zeros_like(count_acc)
        start_acc[...] = jnp.zeros_like(start_acc)

    oh1, oh2 = _one_hots(route_ref[...])
    oh = (oh1 + oh2).astype(BF16)
    tm = oh.shape[0]
    earlier = (lax.broadcasted_iota(jnp.int32, (tm, tm), 0) > lax.broadcasted_iota(jnp.int32, (tm, tm), 1))
    before = jnp.dot(earlier.astype(BF16), oh, preferred_element_type=F32) + count_acc[...]
    lane = lax.broadcasted_iota(jnp.int32, (tm, LANE), 1)
    rank_ref[...] = jnp.where(lane == 0, jnp.sum(oh1 * before, axis=-1, keepdims=True),
                              jnp.where(lane == 1, jnp.sum(oh2 * before, axis=-1, keepdims=True), 0.0))
    lower = (lax.broadcasted_iota(jnp.int32, (LANE, LANE), 0) < lax.broadcasted_iota(jnp.int32, (LANE, LANE), 1))
    below = jnp.dot(oh, lower.astype(BF16), preferred_element_type=F32)
    count_acc[...] += jnp.sum(oh.astype(F32), axis=0, keepdims=True)
    start_acc[...] += jnp.sum(below, axis=0, keepdims=True)
    starts_ref[...] = start_acc[...]


def _rank(route, tm):
    t = route.shape[0]
    blocks = 2 * _nbytes((tm, LANE), F32) + _nbytes((1, LANE), F32)
    return pl.pallas_call(
        _rank_kernel,
        grid=(t // tm,),
        in_specs=[pl.BlockSpec((tm, LANE), lambda i: (i, 0))],
        out_specs=[pl.BlockSpec((tm, LANE), lambda i: (i, 0)), pl.BlockSpec((1, LANE), lambda i: (0, 0))],
        out_shape=[jax.ShapeDtypeStruct((t, LANE), F32), jax.ShapeDtypeStruct((1, LANE), F32)],
        scratch_shapes=[pltpu.VMEM((1, LANE), F32), pltpu.VMEM((1, LANE), F32)],
        compiler_params=_params(("arbitrary",), blocks, 2 * _nbytes((tm, tm), F32)),
        name="moe_rank",
    )(route)


def _dest_kernel(route_ref, rank_ref, starts_ref, dest_ref):
    oh1, oh2 = _one_hots(route_ref[...])
    rank = rank_ref[...]
    starts = starts_ref[...]
    d1 = jnp.sum(oh1 * starts, axis=-1, keepdims=True) + rank[:, 0:1]
    d2 = jnp.sum(oh2 * starts, axis=-1, keepdims=True) + rank[:, 1:2]
    lane = lax.broadcasted_iota(jnp.int32, rank.shape, 1)
    dest_ref[...] = jnp.where(lane == 0, d1, jnp.where(lane == 1, d2, 0.0)).astype(jnp.int32)


def _dest(route, rank, starts, tm):
    t = route.shape[0]
    blocks = 3 * _nbytes((tm, LANE), F32) + _nbytes((1, LANE), F32)
    return pl.pallas_call(
        _dest_kernel,
        grid=(t // tm,),
        in_specs=[pl.BlockSpec((tm, LANE), lambda i: (i, 0)), pl.BlockSpec((tm, LANE), lambda i: (i, 0)),
                  pl.BlockSpec((1, LANE), lambda i: (0, 0))],
        out_specs=pl.BlockSpec((tm, LANE), lambda i: (i, 0)),
        out_shape=jax.ShapeDtypeStruct((t, LANE), jnp.int32),
        compiler_params=_params(("parallel",), blocks, 4 * _nbytes((tm, LANE), F32)),
        name="moe_dest",
    )(route, rank, starts)


def _work_items(starts, n_rows):
    n_exp = starts.shape[0]
    n_blk = n_rows // MOE_ROWS
    total = jnp.full((1,), n_rows, jnp.int32)
    pts = jnp.concatenate([jnp.arange(n_blk, dtype=jnp.int32) * MOE_ROWS, starts[1:]])
    idx = jnp.arange(pts.shape[0], dtype=jnp.int32)
    before = (pts[None, :] < pts[:, None]) | ((pts[None, :] == pts[:, None]) & (idx[None, :] < idx[:, None]))
    pos = jnp.sum(before.astype(jnp.int32), axis=1)
    lo = jnp.sum(jnp.where(pos[:, None] == idx[None, :], pts[:, None], 0), axis=0)
    hi = jnp.concatenate([lo[1:], total])
    ends = jnp.concatenate([starts[1:], total])
    r = jnp.minimum(lo // MOE_ROWS, n_blk - 1)
    e = jnp.minimum(jnp.sum((ends[None, :] <= lo[:, None]).astype(jnp.int32), axis=1), n_exp - 1)
    later = jnp.where(e[None, :] > e[:, None], e[None, :], n_exp)
    nxt = jnp.min(later, axis=1)
    nxt = jnp.where(nxt == n_exp, -1, nxt)
    return r, e, lo, hi, nxt


def _stream_expert_weights(w, e_ref, nxt_ref, streams):
    e = e_ref[w]

    def copies(stream, expert):
        hbm, stage, _, sem = stream
        rows = stage.shape[0] // WEIGHT_DMA_PARTS
        parts = [pl.ds(part * rows, rows) for part in range(WEIGHT_DMA_PARTS)]
        return [pltpu.make_async_copy(hbm.at[expert, sl], stage.at[sl], sem) for sl in parts]

    @pl.when(w == 0)
    def _():
        for stream in streams:
            for c in copies(stream, e):
                c.start(priority=WEIGHT_DMA_PRIORITY)

    @pl.when((w == 0) | (e != e_ref[jnp.maximum(w - 1, 0)]))
    def _():
        nxt = nxt_ref[w]
        for stream in streams:
            for c in copies(stream, e):
                c.wait()
            _, stage, w_bf, _ = stream
            rows = stage.shape[0] // CONVERT_PARTS
            for part in range(CONVERT_PARTS):
                sl = slice(part * rows, (part + 1) * rows)
                w_bf[sl, :] = stage[sl, :].astype(BF16)

            @pl.when(nxt >= 0)
            def _():
                for c in copies(stream, nxt):
                    c.start(priority=WEIGHT_DMA_PRIORITY)


def _store_item_rows(o_ref, val, r, lo, hi):
    rows = r * MOE_ROWS + lax.broadcasted_iota(jnp.int32, val.shape, 0)
    mine = (rows >= lo) & (rows < hi)

    @pl.when(lo == r * MOE_ROWS)
    def _():
        o_ref[...] = val

    @pl.when(lo != r * MOE_ROWS)
    def _():
        pltpu.store(o_ref, val, mask=mine)


def _gather_block_rows(w, r_ref, dest_ref, x_hbm, source, rows_f32, x_bf, sem, *, n_assign):
    r = r_ref[w]
    n_blocks = n_assign // MOE_ROWS

    def copy(src_row, slot, j):
        return pltpu.make_async_copy(x_hbm.at[pl.ds(src_row, 1)], rows_f32.at[slot, pl.ds(j, 1)], sem.at[slot])

    def start_block(block, slot):
        for j in range(MOE_ROWS):
            copy(source[block * MOE_ROWS + j], slot, j).start()

    def finish_block(slot):
        for j in range(MOE_ROWS):
            copy(0, slot, j).wait()
        x_bf[...] = rows_f32[slot].astype(BF16)

    @pl.when(w == 0)
    def _():
        def invert(tok, carry):
            for k in range(TOP_K):
                source[dest_ref[TOP_K * tok + k]] = tok
            return carry
        lax.fori_loop(0, n_assign // TOP_K, invert, 0, unroll=4)
        start_block(r, 0)

    @pl.when((w == 0) | (r != r_ref[jnp.maximum(w - 1, 0)]))
    def _():
        for slot in range(2):
            @pl.when(r % 2 == slot)
            def _():
                finish_block(slot)

                @pl.when(r + 1 < n_blocks)
                def _():
                    start_block(r + 1, 1 - slot)


def _moe_experts_kernel(r_ref, e_ref, lo_ref, hi_ref, nxt_ref, dest_ref, x_hbm, wg_hbm, wu_hbm, wd_hbm, y_ref,
                        wg_stage, wu_stage, wd_stage, wg_bf, wu_bf, wd_bf, source, rows_f32, x_bf, sem, row_sem,
                        *, n_assign):
    w = pl.program_id(0)
    _stream_expert_weights(w, e_ref, nxt_ref, [(wg_hbm, wg_stage, wg_bf, sem.at[0]),
                                                 (wu_hbm, wu_stage, wu_bf, sem.at[1]),
                                                 (wd_hbm, wd_stage, wd_bf, sem.at[2])])
    _gather_block_rows(w, r_ref, dest_ref, x_hbm, source, rows_f32, x_bf, row_sem, n_assign=n_assign)
    r, lo, hi = r_ref[w], lo_ref[w], hi_ref[w]

    @pl.when(hi > lo)
    def _():
        x = x_bf[...]
        g = jnp.dot(x, wg_bf[...], preferred_element_type=F32)
        u = jnp.dot(x, wu_bf[...], preferred_element_type=F32)
        h = ((g * (1.0 / (1.0 + jnp.exp(-g)))) * u).astype(BF16)
        y = jnp.dot(h, wd_bf[...], preferred_element_type=F32)
        _store_item_rows(y_ref, y, r, lo, hi)


def _moe_experts(items, dest_flat, xn, wg, wu, wd):
    n_assign = dest_flat.shape[0]
    d = xn.shape[1]
    f = wg.shape[2]
    n_items = items[0].shape[0]
    blocks = _nbytes((MOE_ROWS, d), F32)
    resident = (3 * _nbytes((d, f), F32) + 3 * _nbytes((d, f), BF16) + 5 * _nbytes((MOE_ROWS, d), F32)
                + _nbytes((MOE_ROWS, d), BF16) + 6 * _nbytes((MOE_ROWS, f), F32))
    any_space = pl.BlockSpec(memory_space=pl.ANY)
    return pl.pallas_call(
        functools.partial(_moe_experts_kernel, n_assign=n_assign),
        grid_spec=pltpu.PrefetchScalarGridSpec(
            num_scalar_prefetch=6,
            grid=(n_items,),
            in_specs=[any_space, any_space, any_space, any_space],
            out_specs=pl.BlockSpec((MOE_ROWS, d), lambda w, r, e, lo, hi, nxt, dest: (r[w], 0)),
            scratch_shapes=[pltpu.VMEM((d, f), F32), pltpu.VMEM((d, f), F32), pltpu.VMEM((f, d), F32),
                            pltpu.VMEM((d, f), BF16), pltpu.VMEM((d, f), BF16), pltpu.VMEM((f, d), BF16),
                            pltpu.SMEM((n_assign,), jnp.int32), pltpu.VMEM((2, MOE_ROWS, d), F32),
                            pltpu.VMEM((MOE_ROWS, d), BF16),
                            pltpu.SemaphoreType.DMA((3,)), pltpu.SemaphoreType.DMA((2,))]),
        out_shape=jax.ShapeDtypeStruct((n_assign, d), F32),
        compiler_params=_params(("arbitrary",), blocks, resident),
        name="moe_experts",
    )(*items, dest_flat, xn, wg, wu, wd)


def _combine_kernel(dest_ref, x_ref, route_ref, y_hbm, o_ref, ybuf, sem, *, tm):
    half = tm // 2
    i = pl.program_id(0)

    def copy(src_row, slot, k, t):
        return pltpu.make_async_copy(y_hbm.at[pl.ds(src_row, 1)], ybuf.at[slot, k, pl.ds(t, 1)], sem.at[slot])

    def start_half(index, slot):
        for t in range(half):
            for k in range(TOP_K):
                copy(dest_ref[TOP_K * (index * half + t) + k], slot, k, t).start(priority=ROW_DMA_PRIORITY)

    def finish_half(slot):
        for t in range(half):
            for k in range(TOP_K):
                copy(0, slot, k, t).wait()
        rows = slice(slot * half, (slot + 1) * half)
        route = route_ref[rows, :]
        o_ref[rows, :] = x_ref[rows, :] + (route[:, 2:3] * ybuf[slot, 0] + route[:, 3:4] * ybuf[slot, 1])

    @pl.when(i == 0)
    def _():
        start_half(2 * i, 0)

    start_half(2 * i + 1, 1)
    finish_half(0)

    @pl.when(i + 1 < pl.num_programs(0))
    def _():
        start_half(2 * i + 2, 0)

    finish_half(1)


def _combine(dest_flat, x1, route, y, tm):
    t, d = x1.shape
    blocks = 2 * _nbytes((tm, d), F32) + _nbytes((tm, LANE), F32)
    resident = TOP_K * _nbytes((tm, d), F32) + _nbytes((tm, d), F32)
    return pl.pallas_call(
        functools.partial(_combine_kernel, tm=tm),
        grid_spec=pltpu.PrefetchScalarGridSpec(
            num_scalar_prefetch=1,
            grid=(t // tm,),
            in_specs=[pl.BlockSpec((tm, d), lambda i, dest: (i, 0)),
                      pl.BlockSpec((tm, LANE), lambda i, dest: (i, 0)),
                      pl.BlockSpec(memory_space=pl.ANY)],
            out_specs=pl.BlockSpec((tm, d), lambda i, dest: (i, 0)),
            scratch_shapes=[pltpu.VMEM((2, TOP_K, tm // 2, d), F32), pltpu.SemaphoreType.DMA((2,))]),
        out_shape=jax.ShapeDtypeStruct((t, d), F32),
        compiler_params=_params(("arbitrary",), blocks, resident),
        name="moe_combine",
    )(dest_flat, x1, route, y)


def _pad_cols(w, n):
    return jnp.pad(w, ((0, 0), (0, n - w.shape[1])))


def kernel(x, positions, g_mix, w_in, b_gate, q_norm_g, kv_norm_g, w_uq, w_ukv, a_q_norm_g, a_k_norm_g,
           b_q_norm_g, b_k_norm_g, rel_bias, w_o_a, w_o_b, w_out, g_ffn, w_group, w_expert,
           w_exp_gate, w_exp_up, w_exp_down):
    batch, seq, d = x.shape
    t = batch * seq
    q_lora, kv_lora = q_norm_g.shape[0], kv_norm_g.shape[0]
    a_heads = w_uq.shape[1] // A_QK
    b_heads = w_o_b.shape[0] // B_HEAD_DIM
    b_width = b_heads * B_HEAD_DIM
    n_groups, n_experts = w_group.shape[1], w_expert.shape[1]
    per_group = n_experts // n_groups
    off_b = q_lora + kv_lora + A_ROPE
    assert seq % ATTN_TILE == 0 and (TOP_K * t) % MOE_ROWS == 0 and n_groups + n_experts <= LANE
    assert a_heads == b_heads

    xf = x.reshape(t, d)
    tm_big = min(1024, t)
    tm_half = min(512, t)
    tn_wide = _tile(b_width, 1024)
    assert d % tn_wide == 0

    za_cols = -(-(q_lora + kv_lora + LANE) // tn_wide) * tn_wide
    wuq = jnp.pad(w_uq.reshape(q_lora, a_heads, A_QK), ((0, 0), (0, 0), (0, A_HEAD_PAD - A_QK)))
    wuq = wuq.reshape(q_lora, a_heads * A_HEAD_PAD).astype(BF16)
    wukv = w_ukv.astype(BF16)
    pad_gain = lambda g, s: jnp.pad(g * s, (0, A_HEAD_PAD - A_QK)).reshape(1, A_HEAD_PAD)
    gaq = pad_gain(a_q_norm_g, A_QK ** -0.5)
    gak = pad_gain(a_k_norm_g, 1.0)
    g_bqk = jnp.concatenate([jnp.tile(b_q_norm_g * B_HEAD_DIM ** -0.5, b_heads), jnp.tile(b_k_norm_g, b_heads)])

    half = A_ROPE // 2
    inv = ROPE_THETA ** (-jnp.arange(half, dtype=F32) / half)
    ang = positions.astype(F32).reshape(t, 1) * inv
    cos, sin = jnp.cos(ang), jnp.sin(ang)
    zeros = jnp.zeros((t, half), F32)
    cos_t = jnp.concatenate([cos, cos, zeros, zeros], axis=1)
    sin_lo = jnp.concatenate([-sin, zeros, zeros, zeros], axis=1)
    sin_hi = jnp.concatenate([zeros, sin, zeros, zeros], axis=1)

    xn = _rmsnorm_rows(xf, g_mix, BF16, min(256, t))
    w_in_t = w_in.T
    za = _inproj_cols(xn, w_in_t, 0, za_cols, None, _plain_epilogue, tm_big, tn_wide)
    bqk = _inproj_cols(xn, w_in_t, off_b, 2 * b_width, g_bqk, _head_norm_epilogue, tm_big, tn_wide)
    bv = _inproj_cols(xn, w_in_t, off_b + 2 * b_width, b_width, None, _plain_epilogue, tm_big, tn_wide)
    gates = _inproj_cols(xn, w_in_t, off_b + 3 * b_width, 2 * d, b_gate, _sigmoid_epilogue, tm_big, tn_wide)
    q, k, v = _mla_proj(za, cos_t, sin_lo, sin_hi, wuq, wukv, q_norm_g.reshape(1, -1), kv_norm_g.reshape(1, -1),
                        gaq, gak, a_heads, min(256, t))
    o_a, o_b = _attention(q, k, v, bqk, bv, _band_bias_rows(rel_bias), batch, seq, a_heads)
    merged = _merge(o_a, o_b, w_o_a, w_o_b, gates, tm_half, tn_wide)
    x1 = _out_proj(merged, w_out, xf, tm_half, tn_wide)

    wr = _pad_cols(jnp.concatenate([w_group, w_expert], axis=1), LANE)
    xn2, route = _router(x1, g_ffn, wr, n_groups, per_group, min(256, t))
    rank, starts_f = _rank(route, min(512, t))
    dest = _dest(route, rank, starts_f, min(512, t))[:, :TOP_K].reshape(-1)
    items = _work_items(starts_f[0, :n_experts].astype(jnp.int32), TOP_K * t)
    y = _moe_experts(items, dest, xn2, w_exp_gate, w_exp_up, w_exp_down)
    out = _combine(dest, x1, route, y, min(256, t))
    return out.reshape(batch, seq, d)
```

```python
import functools

import jax
import jax.numpy as jnp
from jax import lax
from jax.experimental import pallas as pl
from jax.experimental.pallas import tpu as pltpu

F32 = jnp.float32
BF16 = jnp.bfloat16

CHUNK = 64
EPS = 1e-6
A_NOPE = 128
A_ROPE = 64
A_V = 128
A_QK = A_NOPE + A_ROPE
B_HEAD_DIM = 128
B_LEFT_CHUNKS = 8
B_MAX_REL = 128
ROPE_THETA = 10000.0
TOP_K = 2

LANE = 128
A_HEAD_PAD = 2 * LANE
V_PAD = 2 * LANE
V7X_VMEM_BYTES = 64 * 2**20

ATTN_TILE = 256
MOE_ROWS = 128
WEIGHT_DMA_PARTS = 4
WEIGHT_DMA_PRIORITY = 1
ROW_DMA_PRIORITY = 1
PROJ_GROUP_ROWS = 256
NEG_INF = float("-inf")


def _nbytes(shape, dtype):
    n = 1
    for s in shape:
        n *= s
    return n * jnp.dtype(dtype).itemsize


def _params(semantics, pipelined_bytes, resident_bytes=0):
    need = 2 * pipelined_bytes + resident_bytes
    return pltpu.CompilerParams(dimension_semantics=semantics,
                                vmem_limit_bytes=min(int(need), V7X_VMEM_BYTES))


def _tile(n, want):
    t = want
    while t > LANE and n % t:
        t //= 2
    assert n % t == 0, (n, want)
    return t


def _rmsnorm_kernel(x_ref, g_ref, o_ref):
    x = x_ref[...]
    r = lax.rsqrt(jnp.mean(x * x, axis=-1, keepdims=True) + EPS)
    o_ref[...] = (x * r * g_ref[...]).astype(o_ref.dtype)


def _rmsnorm_rows(x, g, out_dtype, tm):
    t, d = x.shape
    blocks = _nbytes((tm, d), F32) + _nbytes((tm, d), out_dtype) + _nbytes((1, d), F32)
    return pl.pallas_call(
        _rmsnorm_kernel,
        grid=(t // tm,),
        in_specs=[pl.BlockSpec((tm, d), lambda i: (i, 0)), pl.BlockSpec((1, d), lambda i: (0, 0))],
        out_specs=pl.BlockSpec((tm, d), lambda i: (i, 0)),
        out_shape=jax.ShapeDtypeStruct((t, d), out_dtype),
        compiler_params=_params(("parallel",), blocks, _nbytes((tm, d), F32)),
        name="rmsnorm",
    )(x, g.reshape(1, d))


def _nt_dot(a, b):
    return lax.dot_general(a, b, (((1,), (1,)), ((), ())), preferred_element_type=F32)


def _stage_weight_blocks(streams):
    j = pl.program_id(0)

    @pl.when(pl.program_id(1) == 0)
    def _():
        @pl.when(j == 0)
        def _():
            for fetch, _, _ in streams:
                fetch(j).start(priority=WEIGHT_DMA_PRIORITY)

        for fetch, stage, w_bf in streams:
            fetch(j).wait()
            w_bf[...] = stage[...].astype(BF16)

            @pl.when(j + 1 < pl.num_programs(0))
            def _():
                fetch(j + 1).start(priority=WEIGHT_DMA_PRIORITY)


def _row_groups(rows):
    step = min(rows, PROJ_GROUP_ROWS)
    assert rows % step == 0
    return [slice(s, s + step) for s in range(0, rows, step)]


def _head_norm_epilogue(acc, gb_ref):
    heads = []
    for h in range(acc.shape[1] // B_HEAD_DIM):
        sl = slice(h * B_HEAD_DIM, (h + 1) * B_HEAD_DIM)
        z = acc[:, sl]
        r = lax.rsqrt(jnp.mean(z * z, axis=-1, keepdims=True) + EPS)
        heads.append(z * r * gb_ref[:, sl])
    return jnp.concatenate(heads, axis=1)


def _plain_epilogue(acc, gb_ref):
    del gb_ref
    return acc


def _sigmoid_epilogue(acc, gb_ref):
    return 1.0 / (1.0 + jnp.exp(-(acc + gb_ref[...])))


def _inproj_cols_kernel(a_ref, wt_hbm, *refs, first_col, epilogue):
    gb_ref = refs[0] if len(refs) == 5 else None
    o_ref, stage, wt_bf, sem = refs[-4:]
    tn = wt_bf.shape[0]

    def fetch(block):
        rows = pl.ds(pl.multiple_of(first_col + block * tn, 8), tn)
        return pltpu.make_async_copy(wt_hbm.at[rows], stage, sem)

    _stage_weight_blocks([(fetch, stage, wt_bf)])
    groups = _row_groups(a_ref.shape[0])
    accs = [_nt_dot(a_ref[sl, :], wt_bf[...]) for sl in groups]
    for sl, acc in zip(groups, accs):
        o_ref[sl, :] = epilogue(acc, gb_ref).astype(o_ref.dtype)


def _inproj_cols(xn, w_in_t, first_col, n_cols, gb, epilogue, tm, tn):
    m, k = xn.shape
    assert n_cols % tn == 0 and first_col % 8 == 0
    blocks = _nbytes((tm, k), BF16) + _nbytes((tm, tn), BF16) + _nbytes((1, tn), F32)
    resident = _nbytes((tn, k), F32) + _nbytes((tn, k), BF16) + 3 * _nbytes((tm, tn), F32)
    gb_specs = [] if gb is None else [pl.BlockSpec((1, tn), lambda j, i: (0, j))]
    gb_args = [] if gb is None else [gb.reshape(1, n_cols)]
    return pl.pallas_call(
        functools.partial(_inproj_cols_kernel, first_col=first_col, epilogue=epilogue),
        grid=(n_cols // tn, m // tm),
        in_specs=[pl.BlockSpec((tm, k), lambda j, i: (i, 0)), pl.BlockSpec(memory_space=pl.ANY)] + gb_specs,
        out_specs=pl.BlockSpec((tm, tn), lambda j, i: (i, j)),
        out_shape=jax.ShapeDtypeStruct((m, n_cols), BF16),
        scratch_shapes=[pltpu.VMEM((tn, k), F32), pltpu.VMEM((tn, k), BF16), pltpu.SemaphoreType.DMA(())],
        compiler_params=_params(("arbitrary", "arbitrary"), blocks, resident),
        name="inproj_" + epilogue.__name__.strip("_").replace("_epilogue", ""),
    )(xn, w_in_t, *gb_args)


def _rope_padded(v, cos, sin_lo, sin_hi):
    half = A_ROPE // 2
    return v * cos + pltpu.roll(v, LANE - half, 1) * sin_lo + pltpu.roll(v, half, 1) * sin_hi


def _ones_column(rows):
    lane = lax.broadcasted_iota(jnp.int32, (rows, V_PAD - A_V), 1)
    return jnp.where(lane == 0, 1.0, 0.0).astype(BF16)


def _mla_proj_kernel(za_ref, cos_ref, sl_ref, sh_ref, wuq_ref, wukv_ref, gq_ref, gkv_ref, gaq_ref, gak_ref,
                     q_ref, k_ref, v_ref, *, heads, q_lora, kv_lora):
    cos, sin_lo, sin_hi = cos_ref[...], sl_ref[...], sh_ref[...]

    def norm(z, g):
        z = z.astype(F32)
        r = lax.rsqrt(jnp.mean(z * z, axis=-1, keepdims=True) + EPS)
        return (z * r * g).astype(BF16)

    cq = norm(za_ref[:, :q_lora], gq_ref[...])
    ckv = norm(za_ref[:, q_lora:q_lora + kv_lora], gkv_ref[...])
    slab = za_ref[:, q_lora + kv_lora:q_lora + kv_lora + LANE].astype(F32)
    k_rope = jnp.where(lax.broadcasted_iota(jnp.int32, slab.shape, 1) < A_ROPE, slab, 0.0)
    qacc = jnp.dot(cq, wuq_ref[...], preferred_element_type=F32)
    kvacc = jnp.dot(ckv, wukv_ref[...], preferred_element_type=F32)

    gq_lo, gq_hi = gaq_ref[:, :LANE], gaq_ref[:, LANE:]
    gk_lo, gk_hi = gak_ref[:, :LANE], gak_ref[:, LANE:]
    kr_ss = jnp.sum(k_rope * k_rope, axis=-1, keepdims=True)
    kr_rot = _rope_padded(k_rope * gk_hi, cos, sin_lo, sin_hi)
    ones_col = _ones_column(slab.shape[0])
    for h in range(heads):
        base = h * A_HEAD_PAD
        q_lo = qacc[:, base:base + LANE]
        q_hi = qacc[:, base + LANE:base + A_HEAD_PAD]
        ss = jnp.sum(q_lo * q_lo, axis=-1, keepdims=True) + jnp.sum(q_hi * q_hi, axis=-1, keepdims=True)
        r = lax.rsqrt(ss / A_QK + EPS)
        q_ref[:, base:base + LANE] = (q_lo * r * gq_lo).astype(BF16)
        q_ref[:, base + LANE:base + A_HEAD_PAD] = _rope_padded(q_hi * r * gq_hi, cos, sin_lo, sin_hi).astype(BF16)

        k_lo = kvacc[:, base:base + LANE]
        ssk = jnp.sum(k_lo * k_lo, axis=-1, keepdims=True) + kr_ss
        rk = lax.rsqrt(ssk / A_QK + EPS)
        k_ref[:, base:base + LANE] = (k_lo * rk * gk_lo).astype(BF16)
        k_ref[:, base + LANE:base + A_HEAD_PAD] = (kr_rot * rk).astype(BF16)
        v_ref[:, h * V_PAD:h * V_PAD + A_V] = kvacc[:, base + LANE:base + A_HEAD_PAD].astype(BF16)
        v_ref[:, h * V_PAD + A_V:(h + 1) * V_PAD] = ones_col


def _mla_proj(za, cos, sin_lo, sin_hi, wuq, wukv, gq, gkv, gaq, gak, heads, tm):
    t, za_cols = za.shape
    q_lora, kv_lora = wuq.shape[0], wukv.shape[0]
    hp = heads * A_HEAD_PAD
    row = lambda i: (i, 0)
    fix = lambda i: (0, 0)
    blocks = (_nbytes((tm, za_cols), za.dtype) + 3 * _nbytes((tm, LANE), F32) + _nbytes(wuq.shape, BF16)
              + _nbytes(wukv.shape, BF16) + 2 * _nbytes((tm, hp), BF16) + _nbytes((tm, heads * V_PAD), BF16))
    return pl.pallas_call(
        functools.partial(_mla_proj_kernel, heads=heads, q_lora=q_lora, kv_lora=kv_lora),
        grid=(t // tm,),
        in_specs=[pl.BlockSpec((tm, za_cols), row),
                  pl.BlockSpec((tm, LANE), row), pl.BlockSpec((tm, LANE), row), pl.BlockSpec((tm, LANE), row),
                  pl.BlockSpec(wuq.shape, fix), pl.BlockSpec(wukv.shape, fix),
                  pl.BlockSpec((1, q_lora), fix), pl.BlockSpec((1, kv_lora), fix),
                  pl.BlockSpec((1, A_HEAD_PAD), fix), pl.BlockSpec((1, A_HEAD_PAD), fix)],
        out_specs=[pl.BlockSpec((tm, hp), row), pl.BlockSpec((tm, hp), row), pl.BlockSpec((tm, heads * V_PAD), row)],
        out_shape=[jax.ShapeDtypeStruct((t, hp), BF16), jax.ShapeDtypeStruct((t, hp), BF16),
                   jax.ShapeDtypeStruct((t, heads * V_PAD), BF16)],
        compiler_params=_params(("parallel",), blocks, 3 * _nbytes((tm, hp), F32)),
        name="mla_proj",
    )(za, cos, sin_lo, sin_hi, wuq, wukv, gq, gkv, gaq, gak)


def _mla_scores(i, q_ref, k_ref, diag_ok):
    tq = ATTN_TILE
    q = q_ref[i * tq:(i + 1) * tq, :]
    sd = jnp.where(diag_ok, _nt_dot(q, k_ref[i * tq:(i + 1) * tq, :]), NEG_INF)
    s0 = _nt_dot(q, k_ref[:i * tq, :]) if i > 0 else None
    return sd, s0


def _mla_softmax(sd, s0):
    m = jnp.max(sd, axis=-1, keepdims=True)
    if s0 is None:
        return jnp.exp(sd - m).astype(BF16), None
    m = jnp.maximum(m, jnp.max(s0, axis=-1, keepdims=True))
    return jnp.exp(sd - m).astype(BF16), jnp.exp(s0 - m).astype(BF16)


def _mla_values(i, pd, p0, v_ref, o_ref):
    tq = ATTN_TILE
    o = jnp.dot(pd, v_ref[i * tq:(i + 1) * tq, :], preferred_element_type=F32)
    if p0 is not None:
        o = o + jnp.dot(p0, v_ref[:i * tq, :], preferred_element_type=F32)
    o_ref[i * tq:(i + 1) * tq, :] = (o[:, :A_V] / o[:, A_V:A_V + 1]).astype(o_ref.dtype)


def _band_window(i):
    left = B_LEFT_CHUNKS * CHUNK
    q0 = i * ATTN_TILE
    k0 = max(0, q0 - left)
    return q0, k0, q0 + ATTN_TILE - k0, left - q0 + k0


def _band_scores(i, q_ref, k_ref, table):
    q0, k0, kw, u0 = _band_window(i)
    return _nt_dot(q_ref[q0:q0 + ATTN_TILE, :], k_ref[k0:k0 + kw, :]) + table[:, u0:u0 + kw]


def _band_softmax(s):
    return jnp.exp(s - jnp.max(s, axis=-1, keepdims=True)).astype(BF16)


def _band_values(i, p, v_ref, o_ref):
    q0, k0, kw, _ = _band_window(i)
    o = jnp.dot(p, v_ref[k0:k0 + kw, :], preferred_element_type=F32)
    o_ref[q0:q0 + ATTN_TILE, :] = (o[:, :B_HEAD_DIM] / o[:, B_HEAD_DIM:B_HEAD_DIM + 1]).astype(o_ref.dtype)


def _attn_kernel(aq_ref, ak_ref, av_ref, bq_ref, bk_ref, bv_ref, r_ref, oa_ref, ob_ref, bv_pad, *, seq):
    tq = ATTN_TILE
    left = B_LEFT_CHUNKS * CHUNK
    width = r_ref.shape[1]
    rc = lax.broadcasted_iota(jnp.int32, (tq, tq), 0) // CHUNK
    cc = lax.broadcasted_iota(jnp.int32, (tq, tq), 1) // CHUNK
    diag_ok = cc <= rc
    bias = pltpu.roll(jnp.broadcast_to(r_ref[...], (tq, width)), 0, 1, stride=1, stride_axis=0)[:, :left + tq]
    q_chunk = lax.broadcasted_iota(jnp.int32, (tq, left + tq), 0) // CHUNK
    k_chunk = lax.broadcasted_iota(jnp.int32, (tq, left + tq), 1) // CHUNK
    table = jnp.where((k_chunk >= q_chunk) & (k_chunk <= q_chunk + B_LEFT_CHUNKS), bias, NEG_INF)
    bv_pad[:, :B_HEAD_DIM] = bv_ref[...]
    bv_pad[:, B_HEAD_DIM:] = _ones_column(seq)

    n_tiles = seq // tq
    order = [n_tiles - 1 - i // 2 if i % 2 == 0 else i // 2 for i in range(n_tiles)]
    sd, s0 = _mla_scores(order[0], aq_ref, ak_ref, diag_ok)
    for n, i in enumerate(order):
        sb = _band_scores(i, bq_ref, bk_ref, table)
        pd, p0 = _mla_softmax(sd, s0)
        if n + 1 < n_tiles:
            sd, s0 = _mla_scores(order[n + 1], aq_ref, ak_ref, diag_ok)
        _mla_values(i, pd, p0, av_ref, oa_ref)
        _band_values(i, _band_softmax(sb), bv_pad, ob_ref)


def _band_bias_rows(rel_bias):
    left = B_LEFT_CHUNKS * CHUNK
    width = left + 2 * ATTN_TILE
    m = jnp.arange(width, dtype=jnp.int32)
    j = jnp.where(m < left + ATTN_TILE, m, m - width)
    dist = left - j
    rows = rel_bias[:, jnp.clip(dist, -B_MAX_REL, B_MAX_REL) + B_MAX_REL].astype(F32)
    return rows.reshape(rel_bias.shape[0], 1, width)


def _attention(q, k, v, bqk, bv, bias_rows, batch, seq, heads):
    t = batch * seq
    d = B_HEAD_DIM
    width = bias_rows.shape[2]
    head = lambda b, h: (b, h)
    blocks = (2 * _nbytes((seq, A_HEAD_PAD), BF16) + _nbytes((seq, V_PAD), BF16) + _nbytes((seq, A_V), BF16)
              + 4 * _nbytes((seq, d), BF16) + _nbytes((1, width), F32))
    temps = _nbytes((seq, V_PAD), BF16) + 8 * _nbytes((ATTN_TILE, seq), F32) + 8 * _nbytes((ATTN_TILE, width), F32)
    return pl.pallas_call(
        functools.partial(_attn_kernel, seq=seq),
        grid=(batch, heads),
        in_specs=[pl.BlockSpec((seq, A_HEAD_PAD), head), pl.BlockSpec((seq, A_HEAD_PAD), head),
                  pl.BlockSpec((seq, V_PAD), head),
                  pl.BlockSpec((seq, d), head),
                  pl.BlockSpec((seq, d), lambda b, h: (b, heads + h)),
                  pl.BlockSpec((seq, d), head),
                  pl.BlockSpec((None, 1, width), lambda b, h: (h, 0, 0))],
        out_specs=[pl.BlockSpec((seq, A_V), head), pl.BlockSpec((seq, d), head)],
        out_shape=[jax.ShapeDtypeStruct((t, heads * A_V), BF16), jax.ShapeDtypeStruct((t, heads * d), BF16)],
        scratch_shapes=[pltpu.VMEM((seq, V_PAD), BF16)],
        compiler_params=_params(("parallel", "parallel"), blocks, temps),
        name="attention",
    )(q, k, v, bqk, bqk, bv, bias_rows)


def _column_block_fetch(w_hbm, stage, sem):
    tn = stage.shape[1]
    return lambda block: pltpu.make_async_copy(w_hbm.at[:, pl.ds(pl.multiple_of(block * tn, LANE), tn)], stage, sem)


def _merge_kernel(oa_ref, ob_ref, woa_hbm, wob_hbm, g0_ref, g1_ref, o_ref, a_stage, b_stage, woa_bf, wob_bf, sem):
    _stage_weight_blocks([(_column_block_fetch(woa_hbm, a_stage, sem.at[0]), a_stage, woa_bf),
                          (_column_block_fetch(wob_hbm, b_stage, sem.at[1]), b_stage, wob_bf)])
    for sl in _row_groups(oa_ref.shape[0]):
        a = jnp.dot(oa_ref[sl, :], woa_bf[...], preferred_element_type=F32)
        b = jnp.dot(ob_ref[sl, :], wob_bf[...], preferred_element_type=F32)
        o_ref[sl, :] = (g0_ref[sl, :].astype(F32) * a + g1_ref[sl, :].astype(F32) * b).astype(o_ref.dtype)


def _merge(oa, ob, woa, wob, gates, tm, tn):
    m = oa.shape[0]
    ka, kb = woa.shape[0], wob.shape[0]
    d = woa.shape[1]
    assert d % tn == 0
    g1 = d // tn
    blocks = _nbytes((tm, ka), BF16) + _nbytes((tm, kb), BF16) + 3 * _nbytes((tm, tn), BF16)
    resident = (_nbytes((ka + kb, tn), F32) + 2 * _nbytes((ka + kb, tn), BF16) + 3 * _nbytes((tm, tn), F32))
    return pl.pallas_call(
        _merge_kernel,
        grid=(d // tn, m // tm),
        in_specs=[pl.BlockSpec((tm, ka), lambda j, i: (i, 0)),
                  pl.BlockSpec((tm, kb), lambda j, i: (i, 0)),
                  pl.BlockSpec(memory_space=pl.ANY), pl.BlockSpec(memory_space=pl.ANY),
                  pl.BlockSpec((tm, tn), lambda j, i: (i, j)),
                  pl.BlockSpec((tm, tn), lambda j, i: (i, g1 + j))],
        out_specs=pl.BlockSpec((tm, tn), lambda j, i: (i, j)),
        out_shape=jax.ShapeDtypeStruct((m, d), BF16),
        scratch_shapes=[pltpu.VMEM((ka, tn), F32), pltpu.VMEM((kb, tn), F32),
                        pltpu.VMEM((ka, tn), BF16), pltpu.VMEM((kb, tn), BF16), pltpu.SemaphoreType.DMA((2,))],
        compiler_params=_params(("arbitrary", "arbitrary"), blocks, resident),
        name="merge",
    )(oa, ob, woa, wob, gates, gates)


def _out_proj_kernel(a_ref, w_hbm, x_ref, o_ref, stage, w_bf, sem):
    _stage_weight_blocks([(_column_block_fetch(w_hbm, stage, sem), stage, w_bf)])
    groups = _row_groups(a_ref.shape[0])
    accs = [jnp.dot(a_ref[sl, :], w_bf[...], preferred_element_type=F32) for sl in groups]
    for sl, acc in zip(groups, accs):
        o_ref[sl, :] = x_ref[sl, :] + acc


def _out_proj(merged, wout, x, tm, tn):
    m, k = merged.shape
    n = wout.shape[1]
    assert n % tn == 0
    blocks = _nbytes((tm, k), BF16) + 2 * _nbytes((tm, tn), F32)
    resident = _nbytes((k, tn), F32) + 2 * _nbytes((k, tn), BF16) + 2 * _nbytes((tm, tn), F32)
    return pl.pallas_call(
        _out_proj_kernel,
        grid=(n // tn, m // tm),
        in_specs=[pl.BlockSpec((tm, k), lambda j, i: (i, 0)),
                  pl.BlockSpec(memory_space=pl.ANY),
                  pl.BlockSpec((tm, tn), lambda j, i: (i, j))],
        out_specs=pl.BlockSpec((tm, tn), lambda j, i: (i, j)),
        out_shape=jax.ShapeDtypeStruct((m, n), F32),
        scratch_shapes=[pltpu.VMEM((k, tn), F32), pltpu.VMEM((k, tn), BF16), pltpu.SemaphoreType.DMA(())],
        compiler_params=_params(("arbitrary", "arbitrary"), blocks, resident),
        name="out_proj",
    )(merged, wout, x)


def _split_bf16(v):
    hi = v.astype(BF16)
    return hi, (v - hi.astype(F32)).astype(BF16)


def _router_kernel(x_ref, g_ref, wr_ref, xn_ref, route_ref, *, n_groups, per_group):
    x = x_ref[...]
    r = lax.rsqrt(jnp.mean(x * x, axis=-1, keepdims=True) + EPS)
    xn = x * r * g_ref[...]
    xn_ref[...] = xn
    x_hi, x_lo = _split_bf16(xn)
    w_hi, w_lo = _split_bf16(wr_ref[...])
    dot = functools.partial(jnp.dot, preferred_element_type=F32)
    logits = dot(x_hi, w_hi) + (dot(x_lo, w_hi) + dot(x_hi, w_lo))
    lane = lax.broadcasted_iota(jnp.int32, logits.shape, 1).astype(F32)
    far = float(LANE)

    def top(vals):
        best = jnp.max(vals, axis=-1, keepdims=True)
        return best, jnp.min(jnp.where(vals == best, lane, far), axis=-1, keepdims=True)

    gl = jnp.where(lane < n_groups, logits, NEG_INF)
    gmax, grp = top(gl)
    p_grp = 1.0 / jnp.sum(jnp.exp(gl - gmax), axis=-1, keepdims=True)
    lo = n_groups + grp * per_group
    el = jnp.where((lane >= lo) & (lane < lo + per_group), logits, NEG_INF)
    t1, i1 = top(el)
    t2, i2 = top(jnp.where(lane == i1, NEG_INF, el))
    d = jnp.exp(t2 - t1)
    w1 = p_grp / (1.0 + d)
    w2 = p_grp * d / (1.0 + d)
    route_ref[...] = jnp.where(lane == 0, i1 - n_groups,
                               jnp.where(lane == 1, i2 - n_groups,
                                         jnp.where(lane == 2, w1, jnp.where(lane == 3, w2, 0.0))))


def _router(x1, g, wr, n_groups, per_group, tm):
    t, d = x1.shape
    blocks = 2 * _nbytes((tm, d), F32) + _nbytes((1, d), F32) + _nbytes((d, LANE), F32) + _nbytes((tm, LANE), F32)
    return pl.pallas_call(
        functools.partial(_router_kernel, n_groups=n_groups, per_group=per_group),
        grid=(t // tm,),
        in_specs=[pl.BlockSpec((tm, d), lambda i: (i, 0)), pl.BlockSpec((1, d), lambda i: (0, 0)),
                  pl.BlockSpec((d, LANE), lambda i: (0, 0))],
        out_specs=[pl.BlockSpec((tm, d), lambda i: (i, 0)), pl.BlockSpec((tm, LANE), lambda i: (i, 0))],
        out_shape=[jax.ShapeDtypeStruct((t, d), F32), jax.ShapeDtypeStruct((t, LANE), F32)],
        compiler_params=_params(("parallel",), blocks, 2 * _nbytes((tm, d), F32)),
        name="router",
    )(x1, g.reshape(1, d), wr)


def _one_hots(route):
    lane = lax.broadcasted_iota(jnp.int32, route.shape, 1).astype(F32)
    return (lane == route[:, 0:1]).astype(F32), (lane == route[:, 1:2]).astype(F32)


def _rank_kernel(route_ref, rank_ref, starts_ref, count_acc, start_acc):
    i = pl.program_id(0)

    @pl.when(i == 0)
    def _():
        count_acc[...] = jnp.zeros_like(count_acc)
        start_acc[...] = jnp.zeros_like(start_acc)

    oh1, oh2 = _one_hots(route_ref[...])
    oh = (oh1 + oh2).astype(BF16)
    tm = oh.shape[0]
    earlier = (lax.broadcasted_iota(jnp.int32, (tm, tm), 0) > lax.broadcasted_iota(jnp.int32, (tm, tm), 1))
    before = jnp.dot(earlier.astype(BF16), oh, preferred_element_type=F32) + count_acc[...]
    lane = lax.broadcasted_iota(jnp.int32, (tm, LANE), 1)
    rank_ref[...] = jnp.where(lane == 0, jnp.sum(oh1 * before, axis=-1, keepdims=True),
                              jnp.where(lane == 1, jnp.sum(oh2 * before, axis=-1, keepdims=True), 0.0))
    lower = (lax.broadcasted_iota(jnp.int32, (LANE, LANE), 0) < lax.broadcasted_iota(jnp.int32, (LANE, LANE), 1))
    below = jnp.dot(oh, lower.astype(BF16), preferred_element_type=F32)
    count_acc[...] += jnp.sum(oh.astype(F32), axis=0, keepdims=True)
    start_acc[...] += jnp.sum(below, axis=0, keepdims=True)
    starts_ref[...] = start_acc[...]


def _rank(route, tm):
    t = route.shape[0]
    blocks = 2 * _nbytes((tm, LANE), F32) + _nbytes((1, LANE), F32)
    return pl.pallas_call(
        _rank_kernel,
        grid=(t // tm,),
        in_specs=[pl.BlockSpec((tm, LANE), lambda i: (i, 0))],
        out_specs=[pl.BlockSpec((tm, LANE), lambda i: (i, 0)), pl.BlockSpec((1, LANE), lambda i: (0, 0))],
        out_shape=[jax.ShapeDtypeStruct((t, LANE), F32), jax.ShapeDtypeStruct((1, LANE), F32)],
        scratch_shapes=[pltpu.VMEM((1, LANE), F32), pltpu.VMEM((1, LANE), F32)],
        compiler_params=_params(("arbitrary",), blocks, 2 * _nbytes((tm, tm), F32)),
        name="moe_rank",
    )(route)


def _dest_kernel(route_ref, rank_ref, starts_ref, dest_ref):
    oh1, oh2 = _one_hots(route_ref[...])
    rank = rank_ref[...]
    starts = starts_ref[...]
    d1 = jnp.sum(oh1 * starts, axis=-1, keepdims=True) + rank[:, 0:1]
    d2 = jnp.sum(oh2 * starts, axis=-1, keepdims=True) + rank[:, 1:2]
    lane = lax.broadcasted_iota(jnp.int32, rank.shape, 1)
    dest_ref[...] = jnp.where(lane == 0, d1, jnp.where(lane == 1, d2, 0.0)).astype(jnp.int32)


def _dest(route, rank, starts, tm):
    t = route.shape[0]
    blocks = 3 * _nbytes((tm, LANE), F32) + _nbytes((1, LANE), F32)
    return pl.pallas_call(
        _dest_kernel,
        grid=(t // tm,),
        in_specs=[pl.BlockSpec((tm, LANE), lambda i: (i, 0)), pl.BlockSpec((tm, LANE), lambda i: (i, 0)),
                  pl.BlockSpec((1, LANE), lambda i: (0, 0))],
        out_specs=pl.BlockSpec((tm, LANE), lambda i: (i, 0)),
        out_shape=jax.ShapeDtypeStruct((t, LANE), jnp.int32),
        compiler_params=_params(("parallel",), blocks, 4 * _nbytes((tm, LANE), F32)),
        name="moe_dest",
    )(route, rank, starts)


def _work_items(starts, n_rows):
    n_exp = starts.shape[0]
    n_blk = n_rows // MOE_ROWS
    total = jnp.full((1,), n_rows, jnp.int32)
    pts = jnp.concatenate([jnp.arange(n_blk, dtype=jnp.int32) * MOE_ROWS, starts[1:]])
    idx = jnp.arange(pts.shape[0], dtype=jnp.int32)
    before = (pts[None, :] < pts[:, None]) | ((pts[None, :] == pts[:, None]) & (idx[None, :] < idx[:, None]))
    pos = jnp.sum(before.astype(jnp.int32), axis=1)
    lo = jnp.sum(jnp.where(pos[:, None] == idx[None, :], pts[:, None], 0), axis=0)
    hi = jnp.concatenate([lo[1:], total])
    ends = jnp.concatenate([starts[1:], total])
    r = jnp.minimum(lo // MOE_ROWS, n_blk - 1)
    e = jnp.minimum(jnp.sum((ends[None, :] <= lo[:, None]).astype(jnp.int32), axis=1), n_exp - 1)
    later = jnp.where(e[None, :] > e[:, None], e[None, :], n_exp)
    nxt = jnp.min(later, axis=1)
    nxt = jnp.where(nxt == n_exp, -1, nxt)
    return r, e, lo, hi, nxt


def _stream_expert_weights(w, e_ref, nxt_ref, streams):
    e = e_ref[w]

    def copies(stream, expert):
        hbm, stage, _, sem = stream
        rows = stage.shape[0] // WEIGHT_DMA_PARTS
        parts = [pl.ds(part * rows, rows) for part in range(WEIGHT_DMA_PARTS)]
        return [pltpu.make_async_copy(hbm.at[expert, sl], stage.at[sl], sem) for sl in parts]

    @pl.when(w == 0)
    def _():
        for stream in streams:
            for c in copies(stream, e):
                c.start(priority=WEIGHT_DMA_PRIORITY)

    @pl.when((w == 0) | (e != e_ref[jnp.maximum(w - 1, 0)]))
    def _():
        nxt = nxt_ref[w]
        for stream in streams:
            for c in copies(stream, e):
                c.wait()
            _, stage, w_bf, _ = stream
            w_bf[...] = stage[...].astype(BF16)

            @pl.when(nxt >= 0)
            def _():
                for c in copies(stream, nxt):
                    c.start(priority=WEIGHT_DMA_PRIORITY)


def _store_item_rows(o_ref, val, r, lo, hi):
    rows = r * MOE_ROWS + lax.broadcasted_iota(jnp.int32, val.shape, 0)
    mine = (rows >= lo) & (rows < hi)

    @pl.when(lo == r * MOE_ROWS)
    def _():
        o_ref[...] = val

    @pl.when(lo != r * MOE_ROWS)
    def _():
        pltpu.store(o_ref, val, mask=mine)


def _gather_block_rows(w, r_ref, dest_ref, x_hbm, source, rows_f32, x_bf, sem, *, n_assign):
    r = r_ref[w]
    n_blocks = n_assign // MOE_ROWS

    def copy(src_row, slot, j):
        return pltpu.make_async_copy(x_hbm.at[pl.ds(src_row, 1)], rows_f32.at[slot, pl.ds(j, 1)], sem.at[slot])

    def start_block(block, slot):
        for j in range(MOE_ROWS):
            copy(source[block * MOE_ROWS + j], slot, j).start()

    def finish_block(slot):
        for j in range(MOE_ROWS):
            copy(0, slot, j).wait()
        x_bf[...] = rows_f32[slot].astype(BF16)

    @pl.when(w == 0)
    def _():
        def invert(tok, carry):
            for k in range(TOP_K):
                source[dest_ref[TOP_K * tok + k]] = tok
            return carry
        lax.fori_loop(0, n_assign // TOP_K, invert, 0, unroll=4)
        start_block(r, 0)

    @pl.when((w == 0) | (r != r_ref[jnp.maximum(w - 1, 0)]))
    def _():
        for slot in range(2):
            @pl.when(r % 2 == slot)
            def _():
                finish_block(slot)

                @pl.when(r + 1 < n_blocks)
                def _():
                    start_block(r + 1, 1 - slot)


def _moe_experts_kernel(r_ref, e_ref, lo_ref, hi_ref, nxt_ref, dest_ref, x_hbm, wg_hbm, wu_hbm, wd_hbm, y_ref,
                        wg_stage, wu_stage, wd_stage, wg_bf, wu_bf, wd_bf, source, rows_f32, x_bf, sem, row_sem,
                        *, n_assign):
    w = pl.program_id(0)
    _stream_expert_weights(w, e_ref, nxt_ref, [(wg_hbm, wg_stage, wg_bf, sem.at[0]),
                                                 (wu_hbm, wu_stage, wu_bf, sem.at[1]),
                                                 (wd_hbm, wd_stage, wd_bf, sem.at[2])])
    _gather_block_rows(w, r_ref, dest_ref, x_hbm, source, rows_f32, x_bf, row_sem, n_assign=n_assign)
    r, lo, hi = r_ref[w], lo_ref[w], hi_ref[w]

    @pl.when(hi > lo)
    def _():
        x = x_bf[...]
        g = jnp.dot(x, wg_bf[...], preferred_element_type=F32)
        u = jnp.dot(x, wu_bf[...], preferred_element_type=F32)
        h = ((g * (1.0 / (1.0 + jnp.exp(-g)))) * u).astype(BF16)
        y = jnp.dot(h, wd_bf[...], preferred_element_type=F32)
        _store_item_rows(y_ref, y, r, lo, hi)


def _moe_experts(items, dest_flat, xn, wg, wu, wd):
    n_assign = dest_flat.shape[0]
    d = xn.shape[1]
    f = wg.shape[2]
    n_items = items[0].shape[0]
    blocks = _nbytes((MOE_ROWS, d), F32)
    resident = (3 * _nbytes((d, f), F32) + 3 * _nbytes((d, f), BF16) + 5 * _nbytes((MOE_ROWS, d), F32)
                + _nbytes((MOE_ROWS, d), BF16) + 6 * _nbytes((MOE_ROWS, f), F32))
    any_space = pl.BlockSpec(memory_space=pl.ANY)
    return pl.pallas_call(
        functools.partial(_moe_experts_kernel, n_assign=n_assign),
        grid_spec=pltpu.PrefetchScalarGridSpec(
            num_scalar_prefetch=6,
            grid=(n_items,),
            in_specs=[any_space, any_space, any_space, any_space],
            out_specs=pl.BlockSpec((MOE_ROWS, d), lambda w, r, e, lo, hi, nxt, dest: (r[w], 0)),
            scratch_shapes=[pltpu.VMEM((d, f), F32), pltpu.VMEM((d, f), F32), pltpu.VMEM((f, d), F32),
                            pltpu.VMEM((d, f), BF16), pltpu.VMEM((d, f), BF16), pltpu.VMEM((f, d), BF16),
                            pltpu.SMEM((n_assign,), jnp.int32), pltpu.VMEM((2, MOE_ROWS, d), F32),
                            pltpu.VMEM((MOE_ROWS, d), BF16),
                            pltpu.SemaphoreType.DMA((3,)), pltpu.SemaphoreType.DMA((2,))]),
        out_shape=jax.ShapeDtypeStruct((n_assign, d), F32),
        compiler_params=_params(("arbitrary",), blocks, resident),
        name="moe_experts",
    )(*items, dest_flat, xn, wg, wu, wd)


def _combine_kernel(dest_ref, x_ref, route_ref, y_hbm, o_ref, ybuf, sem, *, tm):
    half = tm // 2
    i = pl.program_id(0)

    def copy(src_row, slot, k, t):
        return pltpu.make_async_copy(y_hbm.at[pl.ds(src_row, 1)], ybuf.at[slot, k, pl.ds(t, 1)], sem.at[slot])

    def start_half(index, slot):
        for t in range(half):
            for k in range(TOP_K):
                copy(dest_ref[TOP_K * (index * half + t) + k], slot, k, t).start(priority=ROW_DMA_PRIORITY)

    def finish_half(slot):
        for t in range(half):
            for k in range(TOP_K):
                copy(0, slot, k, t).wait()
        rows = slice(slot * half, (slot + 1) * half)
        route = route_ref[rows, :]
        o_ref[rows, :] = x_ref[rows, :] + (route[:, 2:3] * ybuf[slot, 0] + route[:, 3:4] * ybuf[slot, 1])

    @pl.when(i == 0)
    def _():
        start_half(2 * i, 0)

    start_half(2 * i + 1, 1)
    finish_half(0)

    @pl.when(i + 1 < pl.num_programs(0))
    def _():
        start_half(2 * i + 2, 0)

    finish_half(1)


def _combine(dest_flat, x1, route, y, tm):
    t, d = x1.shape
    blocks = 2 * _nbytes((tm, d), F32) + _nbytes((tm, LANE), F32)
    resident = TOP_K * _nbytes((tm, d), F32) + _nbytes((tm, d), F32)
    return pl.pallas_call(
        functools.partial(_combine_kernel, tm=tm),
        grid_spec=pltpu.PrefetchScalarGridSpec(
            num_scalar_prefetch=1,
            grid=(t // tm,),
            in_specs=[pl.BlockSpec((tm, d), lambda i, dest: (i, 0)),
                      pl.BlockSpec((tm, LANE), lambda i, dest: (i, 0)),
                      pl.BlockSpec(memory_space=pl.ANY)],
            out_specs=pl.BlockSpec((tm, d), lambda i, dest: (i, 0)),
            scratch_shapes=[pltpu.VMEM((2, TOP_K, tm // 2, d), F32), pltpu.SemaphoreType.DMA((2,))]),
        out_shape=jax.ShapeDtypeStruct((t, d), F32),
        compiler_params=_params(("arbitrary",), blocks, resident),
        name="moe_combine",
    )(dest_flat, x1, route, y)


def _pad_cols(w, n):
    return jnp.pad(w, ((0, 0), (0, n - w.shape[1])))


def kernel(x, positions, g_mix, w_in, b_gate, q_norm_g, kv_norm_g, w_uq, w_ukv, a_q_norm_g, a_k_norm_g,
           b_q_norm_g, b_k_norm_g, rel_bias, w_o_a, w_o_b, w_out, g_ffn, w_group, w_expert,
           w_exp_gate, w_exp_up, w_exp_down):
    batch, seq, d = x.shape
    t = batch * seq
    q_lora, kv_lora = q_norm_g.shape[0], kv_norm_g.shape[0]
    a_heads = w_uq.shape[1] // A_QK
    b_heads = w_o_b.shape[0] // B_HEAD_DIM
    b_width = b_heads * B_HEAD_DIM
    n_groups, n_experts = w_group.shape[1], w_expert.shape[1]
    per_group = n_experts // n_groups
    off_b = q_lora + kv_lora + A_ROPE
    assert seq % ATTN_TILE == 0 and (TOP_K * t) % MOE_ROWS == 0 and n_groups + n_experts <= LANE
    assert a_heads == b_heads

    xf = x.reshape(t, d)
    tm_big = min(1024, t)
    tm_half = min(512, t)
    tn_wide = _tile(b_width, 1024)
    assert d % tn_wide == 0

    za_cols = -(-(q_lora + kv_lora + LANE) // tn_wide) * tn_wide
    wuq = jnp.pad(w_uq.reshape(q_lora, a_heads, A_QK), ((0, 0), (0, 0), (0, A_HEAD_PAD - A_QK)))
    wuq = wuq.reshape(q_lora, a_heads * A_HEAD_PAD).astype(BF16)
    wukv = w_ukv.astype(BF16)
    pad_gain = lambda g, s: jnp.pad(g * s, (0, A_HEAD_PAD - A_QK)).reshape(1, A_HEAD_PAD)
    gaq = pad_gain(a_q_norm_g, A_QK ** -0.5)
    gak = pad_gain(a_k_norm_g, 1.0)
    g_bqk = jnp.concatenate([jnp.tile(b_q_norm_g * B_HEAD_DIM ** -0.5, b_heads), jnp.tile(b_k_norm_g, b_heads)])

    half = A_ROPE // 2
    inv = ROPE_THETA ** (-jnp.arange(half, dtype=F32) / half)
    ang = positions.astype(F32).reshape(t, 1) * inv
    cos, sin = jnp.cos(ang), jnp.sin(ang)
    zeros = jnp.zeros((t, half), F32)
    cos_t = jnp.concatenate([cos, cos, zeros, zeros], axis=1)
    sin_lo = jnp.concatenate([-sin, zeros, zeros, zeros], axis=1)
    sin_hi = jnp.concatenate([zeros, sin, zeros, zeros], axis=1)

    xn = _rmsnorm_rows(xf, g_mix, BF16, min(256, t))
    w_in_t = w_in.T
    za = _inproj_cols(xn, w_in_t, 0, za_cols, None, _plain_epilogue, tm_big, tn_wide)
    bqk = _inproj_cols(xn, w_in_t, off_b, 2 * b_width, g_bqk, _head_norm_epilogue, tm_big, tn_wide)
    bv = _inproj_cols(xn, w_in_t, off_b + 2 * b_width, b_width, None, _plain_epilogue, tm_big, tn_wide)
    gates = _inproj_cols(xn, w_in_t, off_b + 3 * b_width, 2 * d, b_gate, _sigmoid_epilogue, tm_big, tn_wide)
    q, k, v = _mla_proj(za, cos_t, sin_lo, sin_hi, wuq, wukv, q_norm_g.reshape(1, -1), kv_norm_g.reshape(1, -1),
                        gaq, gak, a_heads, min(256, t))
    o_a, o_b = _attention(q, k, v, bqk, bv, _band_bias_rows(rel_bias), batch, seq, a_heads)
    merged = _merge(o_a, o_b, w_o_a, w_o_b, gates, tm_half, tn_wide)
    x1 = _out_proj(merged, w_out, xf, tm_half, tn_wide)

    wr = _pad_cols(jnp.concatenate([w_group, w_expert], axis=1), LANE)
    xn2, route = _router(x1, g_ffn, wr, n_groups, per_group, min(256, t))
    rank, starts_f = _rank(route, min(512, t))
    dest = _dest(route, rank, starts_f, min(512, t))[:, :TOP_K].reshape(-1)
    items = _work_items(starts_f[0, :n_experts].astype(jnp.int32), TOP_K * t)
    y = _moe_experts(items, dest, xn2, w_exp_gate, w_exp_up, w_exp_down)
    out = _combine(dest, x1, route, y, min(256, t))
    return out.reshape(batch, seq, d)
```

```python
import functools

import jax
import jax.numpy as jnp
from jax import lax
from jax.experimental import pallas as pl
from jax.experimental.pallas import tpu as pltpu

F32 = jnp.float32
BF16 = jnp.bfloat16

CHUNK = 64
EPS = 1e-6
A_NOPE = 128
A_ROPE = 64
A_V = 128
A_QK = A_NOPE + A_ROPE
B_HEAD_DIM = 128
B_LEFT_CHUNKS = 8
B_MAX_REL = 128
ROPE_THETA = 10000.0
TOP_K = 2

LANE = 128
A_HEAD_PAD = 2 * LANE
V_PAD = 2 * LANE
V7X_VMEM_BYTES = 64 * 2**20

ATTN_TILE = 256
MOE_ROWS = 128
WEIGHT_DMA_PARTS = 1
WEIGHT_DMA_PRIORITY = 1
ROW_DMA_PRIORITY = 1
PROJ_GROUP_ROWS = 256
NEG_INF = float("-inf")


def _nbytes(shape, dtype):
    n = 1
    for s in shape:
        n *= s
    return n * jnp.dtype(dtype).itemsize


def _params(semantics, pipelined_bytes, resident_bytes=0):
    need = 2 * pipelined_bytes + resident_bytes
    return pltpu.CompilerParams(dimension_semantics=semantics,
                                vmem_limit_bytes=min(int(need), V7X_VMEM_BYTES))


def _tile(n, want):
    t = want
    while t > LANE and n % t:
        t //= 2
    assert n % t == 0, (n, want)
    return t


def _rmsnorm_kernel(x_ref, g_ref, o_ref):
    x = x_ref[...]
    r = lax.rsqrt(jnp.mean(x * x, axis=-1, keepdims=True) + EPS)
    o_ref[...] = (x * r * g_ref[...]).astype(o_ref.dtype)


def _rmsnorm_rows(x, g, out_dtype, tm):
    t, d = x.shape
    blocks = _nbytes((tm, d), F32) + _nbytes((tm, d), out_dtype) + _nbytes((1, d), F32)
    return pl.pallas_call(
        _rmsnorm_kernel,
        grid=(t // tm,),
        in_specs=[pl.BlockSpec((tm, d), lambda i: (i, 0)), pl.BlockSpec((1, d), lambda i: (0, 0))],
        out_specs=pl.BlockSpec((tm, d), lambda i: (i, 0)),
        out_shape=jax.ShapeDtypeStruct((t, d), out_dtype),
        compiler_params=_params(("parallel",), blocks, _nbytes((tm, d), F32)),
        name="rmsnorm",
    )(x, g.reshape(1, d))


def _nt_dot(a, b):
    return lax.dot_general(a, b, (((1,), (1,)), ((), ())), preferred_element_type=F32)


def _stage_weight_blocks(streams):
    j = pl.program_id(0)

    @pl.when(pl.program_id(1) == 0)
    def _():
        @pl.when(j == 0)
        def _():
            for fetch, _, _ in streams:
                fetch(j).start(priority=WEIGHT_DMA_PRIORITY)

        for fetch, stage, w_bf in streams:
            fetch(j).wait()
            w_bf[...] = stage[...].astype(BF16)

            @pl.when(j + 1 < pl.num_programs(0))
            def _():
                fetch(j + 1).start(priority=WEIGHT_DMA_PRIORITY)


def _row_groups(rows):
    step = min(rows, PROJ_GROUP_ROWS)
    assert rows % step == 0
    return [slice(s, s + step) for s in range(0, rows, step)]


def _head_norm_epilogue(acc, gb_ref):
    heads = []
    for h in range(acc.shape[1] // B_HEAD_DIM):
        sl = slice(h * B_HEAD_DIM, (h + 1) * B_HEAD_DIM)
        z = acc[:, sl]
        r = lax.rsqrt(jnp.mean(z * z, axis=-1, keepdims=True) + EPS)
        heads.append(z * r * gb_ref[:, sl])
    return jnp.concatenate(heads, axis=1)


def _plain_epilogue(acc, gb_ref):
    del gb_ref
    return acc


def _sigmoid_epilogue(acc, gb_ref):
    return 1.0 / (1.0 + jnp.exp(-(acc + gb_ref[...])))


def _inproj_cols_kernel(a_ref, wt_hbm, *refs, first_col, epilogue):
    gb_ref = refs[0] if len(refs) == 5 else None
    o_ref, stage, wt_bf, sem = refs[-4:]
    tn = wt_bf.shape[0]

    def fetch(block):
        rows = pl.ds(pl.multiple_of(first_col + block * tn, 8), tn)
        return pltpu.make_async_copy(wt_hbm.at[rows], stage, sem)

    _stage_weight_blocks([(fetch, stage, wt_bf)])
    groups = _row_groups(a_ref.shape[0])
    accs = [_nt_dot(a_ref[sl, :], wt_bf[...]) for sl in groups]
    for sl, acc in zip(groups, accs):
        o_ref[sl, :] = epilogue(acc, gb_ref).astype(o_ref.dtype)


def _inproj_cols(xn, w_in_t, first_col, n_cols, gb, epilogue, tm, tn):
    m, k = xn.shape
    assert n_cols % tn == 0 and first_col % 8 == 0
    blocks = _nbytes((tm, k), BF16) + _nbytes((tm, tn), BF16) + _nbytes((1, tn), F32)
    resident = _nbytes((tn, k), F32) + _nbytes((tn, k), BF16) + 3 * _nbytes((tm, tn), F32)
    gb_specs = [] if gb is None else [pl.BlockSpec((1, tn), lambda j, i: (0, j))]
    gb_args = [] if gb is None else [gb.reshape(1, n_cols)]
    return pl.pallas_call(
        functools.partial(_inproj_cols_kernel, first_col=first_col, epilogue=epilogue),
        grid=(n_cols // tn, m // tm),
        in_specs=[pl.BlockSpec((tm, k), lambda j, i: (i, 0)), pl.BlockSpec(memory_space=pl.ANY)] + gb_specs,
        out_specs=pl.BlockSpec((tm, tn), lambda j, i: (i, j)),
        out_shape=jax.ShapeDtypeStruct((m, n_cols), BF16),
        scratch_shapes=[pltpu.VMEM((tn, k), F32), pltpu.VMEM((tn, k), BF16), pltpu.SemaphoreType.DMA(())],
        compiler_params=_params(("arbitrary", "arbitrary"), blocks, resident),
        name="inproj_" + epilogue.__name__.strip("_").replace("_epilogue", ""),
    )(xn, w_in_t, *gb_args)


def _rope_padded(v, cos, sin_lo, sin_hi):
    half = A_ROPE // 2
    return v * cos + pltpu.roll(v, LANE - half, 1) * sin_lo + pltpu.roll(v, half, 1) * sin_hi


def _ones_column(rows):
    lane = lax.broadcasted_iota(jnp.int32, (rows, V_PAD - A_V), 1)
    return jnp.where(lane == 0, 1.0, 0.0).astype(BF16)


def _mla_proj_kernel(za_ref, cos_ref, sl_ref, sh_ref, wuq_ref, wukv_ref, gq_ref, gkv_ref, gaq_ref, gak_ref,
                     q_ref, k_ref, v_ref, *, heads, q_lora, kv_lora):
    cos, sin_lo, sin_hi = cos_ref[...], sl_ref[...], sh_ref[...]

    def norm(z, g):
        z = z.astype(F32)
        r = lax.rsqrt(jnp.mean(z * z, axis=-1, keepdims=True) + EPS)
        return (z * r * g).astype(BF16)

    cq = norm(za_ref[:, :q_lora], gq_ref[...])
    ckv = norm(za_ref[:, q_lora:q_lora + kv_lora], gkv_ref[...])
    slab = za_ref[:, q_lora + kv_lora:q_lora + kv_lora + LANE].astype(F32)
    k_rope = jnp.where(lax.broadcasted_iota(jnp.int32, slab.shape, 1) < A_ROPE, slab, 0.0)
    qacc = jnp.dot(cq, wuq_ref[...], preferred_element_type=F32)
    kvacc = jnp.dot(ckv, wukv_ref[...], preferred_element_type=F32)

    gq_lo, gq_hi = gaq_ref[:, :LANE], gaq_ref[:, LANE:]
    gk_lo, gk_hi = gak_ref[:, :LANE], gak_ref[:, LANE:]
    kr_ss = jnp.sum(k_rope * k_rope, axis=-1, keepdims=True)
    kr_rot = _rope_padded(k_rope * gk_hi, cos, sin_lo, sin_hi)
    ones_col = _ones_column(slab.shape[0])
    for h in range(heads):
        base = h * A_HEAD_PAD
        q_lo = qacc[:, base:base + LANE]
        q_hi = qacc[:, base + LANE:base + A_HEAD_PAD]
        ss = jnp.sum(q_lo * q_lo, axis=-1, keepdims=True) + jnp.sum(q_hi * q_hi, axis=-1, keepdims=True)
        r = lax.rsqrt(ss / A_QK + EPS)
        q_ref[:, base:base + LANE] = (q_lo * r * gq_lo).astype(BF16)
        q_ref[:, base + LANE:base + A_HEAD_PAD] = _rope_padded(q_hi * r * gq_hi, cos, sin_lo, sin_hi).astype(BF16)

        k_lo = kvacc[:, base:base + LANE]
        ssk = jnp.sum(k_lo * k_lo, axis=-1, keepdims=True) + kr_ss
        rk = lax.rsqrt(ssk / A_QK + EPS)
        k_ref[:, base:base + LANE] = (k_lo * rk * gk_lo).astype(BF16)
        k_ref[:, base + LANE:base + A_HEAD_PAD] = (kr_rot * rk).astype(BF16)
        v_ref[:, h * V_PAD:h * V_PAD + A_V] = kvacc[:, base + LANE:base + A_HEAD_PAD].astype(BF16)
        v_ref[:, h * V_PAD + A_V:(h + 1) * V_PAD] = ones_col


def _mla_proj(za, cos, sin_lo, sin_hi, wuq, wukv, gq, gkv, gaq, gak, heads, tm):
    t, za_cols = za.shape
    q_lora, kv_lora = wuq.shape[0], wukv.shape[0]
    hp = heads * A_HEAD_PAD
    row = lambda i: (i, 0)
    fix = lambda i: (0, 0)
    blocks = (_nbytes((tm, za_cols), za.dtype) + 3 * _nbytes((tm, LANE), F32) + _nbytes(wuq.shape, BF16)
              + _nbytes(wukv.shape, BF16) + 2 * _nbytes((tm, hp), BF16) + _nbytes((tm, heads * V_PAD), BF16))
    return pl.pallas_call(
        functools.partial(_mla_proj_kernel, heads=heads, q_lora=q_lora, kv_lora=kv_lora),
        grid=(t // tm,),
        in_specs=[pl.BlockSpec((tm, za_cols), row),
                  pl.BlockSpec((tm, LANE), row), pl.BlockSpec((tm, LANE), row), pl.BlockSpec((tm, LANE), row),
                  pl.BlockSpec(wuq.shape, fix), pl.BlockSpec(wukv.shape, fix),
                  pl.BlockSpec((1, q_lora), fix), pl.BlockSpec((1, kv_lora), fix),
                  pl.BlockSpec((1, A_HEAD_PAD), fix), pl.BlockSpec((1, A_HEAD_PAD), fix)],
        out_specs=[pl.BlockSpec((tm, hp), row), pl.BlockSpec((tm, hp), row), pl.BlockSpec((tm, heads * V_PAD), row)],
        out_shape=[jax.ShapeDtypeStruct((t, hp), BF16), jax.ShapeDtypeStruct((t, hp), BF16),
                   jax.ShapeDtypeStruct((t, heads * V_PAD), BF16)],
        compiler_params=_params(("parallel",), blocks, 3 * _nbytes((tm, hp), F32)),
        name="mla_proj",
    )(za, cos, sin_lo, sin_hi, wuq, wukv, gq, gkv, gaq, gak)


def _mla_scores(i, q_ref, k_ref, diag_ok):
    tq = ATTN_TILE
    q = q_ref[i * tq:(i + 1) * tq, :]
    sd = jnp.where(diag_ok, _nt_dot(q, k_ref[i * tq:(i + 1) * tq, :]), NEG_INF)
    s0 = _nt_dot(q, k_ref[:i * tq, :]) if i > 0 else None
    return sd, s0


def _mla_softmax(sd, s0):
    m = jnp.max(sd, axis=-1, keepdims=True)
    if s0 is None:
        return jnp.exp(sd - m).astype(BF16), None
    m = jnp.maximum(m, jnp.max(s0, axis=-1, keepdims=True))
    return jnp.exp(sd - m).astype(BF16), jnp.exp(s0 - m).astype(BF16)


def _mla_values(i, pd, p0, v_ref, o_ref):
    tq = ATTN_TILE
    o = jnp.dot(pd, v_ref[i * tq:(i + 1) * tq, :], preferred_element_type=F32)
    if p0 is not None:
        o = o + jnp.dot(p0, v_ref[:i * tq, :], preferred_element_type=F32)
    o_ref[i * tq:(i + 1) * tq, :] = (o[:, :A_V] / o[:, A_V:A_V + 1]).astype(o_ref.dtype)


def _band_window(i):
    left = B_LEFT_CHUNKS * CHUNK
    q0 = i * ATTN_TILE
    k0 = max(0, q0 - left)
    return q0, k0, q0 + ATTN_TILE - k0, left - q0 + k0


def _band_scores(i, q_ref, k_ref, table):
    q0, k0, kw, u0 = _band_window(i)
    return _nt_dot(q_ref[q0:q0 + ATTN_TILE, :], k_ref[k0:k0 + kw, :]) + table[:, u0:u0 + kw]


def _band_softmax(s):
    return jnp.exp(s - jnp.max(s, axis=-1, keepdims=True)).astype(BF16)


def _band_values(i, p, v_ref, o_ref):
    q0, k0, kw, _ = _band_window(i)
    o = jnp.dot(p, v_ref[k0:k0 + kw, :], preferred_element_type=F32)
    o_ref[q0:q0 + ATTN_TILE, :] = (o[:, :B_HEAD_DIM] / o[:, B_HEAD_DIM:B_HEAD_DIM + 1]).astype(o_ref.dtype)


def _attn_kernel(aq_ref, ak_ref, av_ref, bq_ref, bk_ref, bv_ref, r_ref, oa_ref, ob_ref, bv_pad, *, seq):
    tq = ATTN_TILE
    left = B_LEFT_CHUNKS * CHUNK
    width = r_ref.shape[1]
    rc = lax.broadcasted_iota(jnp.int32, (tq, tq), 0) // CHUNK
    cc = lax.broadcasted_iota(jnp.int32, (tq, tq), 1) // CHUNK
    diag_ok = cc <= rc
    bias = pltpu.roll(jnp.broadcast_to(r_ref[...], (tq, width)), 0, 1, stride=1, stride_axis=0)[:, :left + tq]
    q_chunk = lax.broadcasted_iota(jnp.int32, (tq, left + tq), 0) // CHUNK
    k_chunk = lax.broadcasted_iota(jnp.int32, (tq, left + tq), 1) // CHUNK
    table = jnp.where((k_chunk >= q_chunk) & (k_chunk <= q_chunk + B_LEFT_CHUNKS), bias, NEG_INF)
    bv_pad[:, :B_HEAD_DIM] = bv_ref[...]
    bv_pad[:, B_HEAD_DIM:] = _ones_column(seq)

    n_tiles = seq // tq
    order = [n_tiles - 1 - i // 2 if i % 2 == 0 else i // 2 for i in range(n_tiles)]
    sd, s0 = _mla_scores(order[0], aq_ref, ak_ref, diag_ok)
    for n, i in enumerate(order):
        sb = _band_scores(i, bq_ref, bk_ref, table)
        pd, p0 = _mla_softmax(sd, s0)
        if n + 1 < n_tiles:
            sd, s0 = _mla_scores(order[n + 1], aq_ref, ak_ref, diag_ok)
        _mla_values(i, pd, p0, av_ref, oa_ref)
        _band_values(i, _band_softmax(sb), bv_pad, ob_ref)


def _band_bias_rows(rel_bias):
    left = B_LEFT_CHUNKS * CHUNK
    width = left + 2 * ATTN_TILE
    m = jnp.arange(width, dtype=jnp.int32)
    j = jnp.where(m < left + ATTN_TILE, m, m - width)
    dist = left - j
    rows = rel_bias[:, jnp.clip(dist, -B_MAX_REL, B_MAX_REL) + B_MAX_REL].astype(F32)
    return rows.reshape(rel_bias.shape[0], 1, width)


def _attention(q, k, v, bqk, bv, bias_rows, batch, seq, heads):
    t = batch * seq
    d = B_HEAD_DIM
    width = bias_rows.shape[2]
    head = lambda b, h: (b, h)
    blocks = (2 * _nbytes((seq, A_HEAD_PAD), BF16) + _nbytes((seq, V_PAD), BF16) + _nbytes((seq, A_V), BF16)
              + 4 * _nbytes((seq, d), BF16) + _nbytes((1, width), F32))
    temps = _nbytes((seq, V_PAD), BF16) + 8 * _nbytes((ATTN_TILE, seq), F32) + 8 * _nbytes((ATTN_TILE, width), F32)
    return pl.pallas_call(
        functools.partial(_attn_kernel, seq=seq),
        grid=(batch, heads),
        in_specs=[pl.BlockSpec((seq, A_HEAD_PAD), head), pl.BlockSpec((seq, A_HEAD_PAD), head),
                  pl.BlockSpec((seq, V_PAD), head),
                  pl.BlockSpec((seq, d), head),
                  pl.BlockSpec((seq, d), lambda b, h: (b, heads + h)),
                  pl.BlockSpec((seq, d), head),
                  pl.BlockSpec((None, 1, width), lambda b, h: (h, 0, 0))],
        out_specs=[pl.BlockSpec((seq, A_V), head), pl.BlockSpec((seq, d), head)],
        out_shape=[jax.ShapeDtypeStruct((t, heads * A_V), BF16), jax.ShapeDtypeStruct((t, heads * d), BF16)],
        scratch_shapes=[pltpu.VMEM((seq, V_PAD), BF16)],
        compiler_params=_params(("parallel", "parallel"), blocks, temps),
        name="attention",
    )(q, k, v, bqk, bqk, bv, bias_rows)


def _column_block_fetch(w_hbm, stage, sem):
    tn = stage.shape[1]
    return lambda block: pltpu.make_async_copy(w_hbm.at[:, pl.ds(pl.multiple_of(block * tn, LANE), tn)], stage, sem)


def _merge_kernel(oa_ref, ob_ref, woa_hbm, wob_hbm, g0_ref, g1_ref, o_ref, a_stage, b_stage, woa_bf, wob_bf, sem):
    _stage_weight_blocks([(_column_block_fetch(woa_hbm, a_stage, sem.at[0]), a_stage, woa_bf),
                          (_column_block_fetch(wob_hbm, b_stage, sem.at[1]), b_stage, wob_bf)])
    for sl in _row_groups(oa_ref.shape[0]):
        a = jnp.dot(oa_ref[sl, :], woa_bf[...], preferred_element_type=F32)
        b = jnp.dot(ob_ref[sl, :], wob_bf[...], preferred_element_type=F32)
        o_ref[sl, :] = (g0_ref[sl, :].astype(F32) * a + g1_ref[sl, :].astype(F32) * b).astype(o_ref.dtype)


def _merge(oa, ob, woa, wob, gates, tm, tn):
    m = oa.shape[0]
    ka, kb = woa.shape[0], wob.shape[0]
    d = woa.shape[1]
    assert d % tn == 0
    g1 = d // tn
    blocks = _nbytes((tm, ka), BF16) + _nbytes((tm, kb), BF16) + 3 * _nbytes((tm, tn), BF16)
    resident = (_nbytes((ka + kb, tn), F32) + 2 * _nbytes((ka + kb, tn), BF16) + 3 * _nbytes((tm, tn), F32))
    return pl.pallas_call(
        _merge_kernel,
        grid=(d // tn, m // tm),
        in_specs=[pl.BlockSpec((tm, ka), lambda j, i: (i, 0)),
                  pl.BlockSpec((tm, kb), lambda j, i: (i, 0)),
                  pl.BlockSpec(memory_space=pl.ANY), pl.BlockSpec(memory_space=pl.ANY),
                  pl.BlockSpec((tm, tn), lambda j, i: (i, j)),
                  pl.BlockSpec((tm, tn), lambda j, i: (i, g1 + j))],
        out_specs=pl.BlockSpec((tm, tn), lambda j, i: (i, j)),
        out_shape=jax.ShapeDtypeStruct((m, d), BF16),
        scratch_shapes=[pltpu.VMEM((ka, tn), F32), pltpu.VMEM((kb, tn), F32),
                        pltpu.VMEM((ka, tn), BF16), pltpu.VMEM((kb, tn), BF16), pltpu.SemaphoreType.DMA((2,))],
        compiler_params=_params(("arbitrary", "arbitrary"), blocks, resident),
        name="merge",
    )(oa, ob, woa, wob, gates, gates)


def _out_proj_kernel(a_ref, w_hbm, x_ref, o_ref, stage, w_bf, sem):
    _stage_weight_blocks([(_column_block_fetch(w_hbm, stage, sem), stage, w_bf)])
    groups = _row_groups(a_ref.shape[0])
    accs = [jnp.dot(a_ref[sl, :], w_bf[...], preferred_element_type=F32) for sl in groups]
    for sl, acc in zip(groups, accs):
        o_ref[sl, :] = x_ref[sl, :] + acc


def _out_proj(merged, wout, x, tm, tn):
    m, k = merged.shape
    n = wout.shape[1]
    assert n % tn == 0
    blocks = _nbytes((tm, k), BF16) + 2 * _nbytes((tm, tn), F32)
    resident = _nbytes((k, tn), F32) + 2 * _nbytes((k, tn), BF16) + 2 * _nbytes((tm, tn), F32)
    return pl.pallas_call(
        _out_proj_kernel,
        grid=(n // tn, m // tm),
        in_specs=[pl.BlockSpec((tm, k), lambda j, i: (i, 0)),
                  pl.BlockSpec(memory_space=pl.ANY),
                  pl.BlockSpec((tm, tn), lambda j, i: (i, j))],
        out_specs=pl.BlockSpec((tm, tn), lambda j, i: (i, j)),
        out_shape=jax.ShapeDtypeStruct((m, n), F32),
        scratch_shapes=[pltpu.VMEM((k, tn), F32), pltpu.VMEM((k, tn), BF16), pltpu.SemaphoreType.DMA(())],
        compiler_params=_params(("arbitrary", "arbitrary"), blocks, resident),
        name="out_proj",
    )(merged, wout, x)


def _split_bf16(v):
    hi = v.astype(BF16)
    return hi, (v - hi.astype(F32)).astype(BF16)


def _router_kernel(x_ref, g_ref, wr_ref, xn_ref, route_ref, *, n_groups, per_group):
    x = x_ref[...]
    r = lax.rsqrt(jnp.mean(x * x, axis=-1, keepdims=True) + EPS)
    xn = x * r * g_ref[...]
    xn_ref[...] = xn
    x_hi, x_lo = _split_bf16(xn)
    w_hi, w_lo = _split_bf16(wr_ref[...])
    dot = functools.partial(jnp.dot, preferred_element_type=F32)
    logits = dot(x_hi, w_hi) + (dot(x_lo, w_hi) + dot(x_hi, w_lo))
    lane = lax.broadcasted_iota(jnp.int32, logits.shape, 1).astype(F32)
    far = float(LANE)

    def top(vals):
        best = jnp.max(vals, axis=-1, keepdims=True)
        return best, jnp.min(jnp.where(vals == best, lane, far), axis=-1, keepdims=True)

    gl = jnp.where(lane < n_groups, logits, NEG_INF)
    gmax, grp = top(gl)
    p_grp = 1.0 / jnp.sum(jnp.exp(gl - gmax), axis=-1, keepdims=True)
    lo = n_groups + grp * per_group
    el = jnp.where((lane >= lo) & (lane < lo + per_group), logits, NEG_INF)
    t1, i1 = top(el)
    t2, i2 = top(jnp.where(lane == i1, NEG_INF, el))
    d = jnp.exp(t2 - t1)
    w1 = p_grp / (1.0 + d)
    w2 = p_grp * d / (1.0 + d)
    route_ref[...] = jnp.where(lane == 0, i1 - n_groups,
                               jnp.where(lane == 1, i2 - n_groups,
                                         jnp.where(lane == 2, w1, jnp.where(lane == 3, w2, 0.0))))


def _router(x1, g, wr, n_groups, per_group, tm):
    t, d = x1.shape
    blocks = 2 * _nbytes((tm, d), F32) + _nbytes((1, d), F32) + _nbytes((d, LANE), F32) + _nbytes((tm, LANE), F32)
    return pl.pallas_call(
        functools.partial(_router_kernel, n_groups=n_groups, per_group=per_group),
        grid=(t // tm,),
        in_specs=[pl.BlockSpec((tm, d), lambda i: (i, 0)), pl.BlockSpec((1, d), lambda i: (0, 0)),
                  pl.BlockSpec((d, LANE), lambda i: (0, 0))],
        out_specs=[pl.BlockSpec((tm, d), lambda i: (i, 0)), pl.BlockSpec((tm, LANE), lambda i: (i, 0))],
        out_shape=[jax.ShapeDtypeStruct((t, d), F32), jax.ShapeDtypeStruct((t, LANE), F32)],
        compiler_params=_params(("parallel",), blocks, 2 * _nbytes((tm, d), F32)),
        name="router",
    )(x1, g.reshape(1, d), wr)


def _one_hots(route):
    lane = lax.broadcasted_iota(jnp.int32, route.shape, 1).astype(F32)
    return (lane == route[:, 0:1]).astype(F32), (lane == route[:, 1:2]).astype(F32)


def _rank_kernel(route_ref, rank_ref, starts_ref, count_acc, start_acc):
    i = pl.program_id(0)

    @pl.when(i == 0)
    def _():
        count_acc[...] = jnp.zeros_like(count_acc)
        start_acc[...] = jnp.zeros_like(start_acc)

    oh1, oh2 = _one_hots(route_ref[...])
    oh = (oh1 + oh2).astype(BF16)
    tm = oh.shape[0]
    earlier = (lax.broadcasted_iota(jnp.int32, (tm, tm), 0) > lax.broadcasted_iota(jnp.int32, (tm, tm), 1))
    before = jnp.dot(earlier.astype(BF16), oh, preferred_element_type=F32) + count_acc[...]
    lane = lax.broadcasted_iota(jnp.int32, (tm, LANE), 1)
    rank_ref[...] = jnp.where(lane == 0, jnp.sum(oh1 * before, axis=-1, keepdims=True),
                              jnp.where(lane == 1, jnp.sum(oh2 * before, axis=-1, keepdims=True), 0.0))
    lower = (lax.broadcasted_iota(jnp.int32, (LANE, LANE), 0) < lax.broadcasted_iota(jnp.int32, (LANE, LANE), 1))
    below = jnp.dot(oh, lower.astype(BF16), preferred_element_type=F32)
    count_acc[...] += jnp.sum(oh.astype(F32), axis=0, keepdims=True)
    start_acc[...] += jnp.sum(below, axis=0, keepdims=True)
    starts_ref[...] = start_acc[...]


def _rank(route, tm):
    t = route.shape[0]
    blocks = 2 * _nbytes((tm, LANE), F32) + _nbytes((1, LANE), F32)
    return pl.pallas_call(
        _rank_kernel,
        grid=(t // tm,),
        in_specs=[pl.BlockSpec((tm, LANE), lambda i: (i, 0))],
        out_specs=[pl.BlockSpec((tm, LANE), lambda i: (i, 0)), pl.BlockSpec((1, LANE), lambda i: (0, 0))],
        out_shape=[jax.ShapeDtypeStruct((t, LANE), F32), jax.ShapeDtypeStruct((1, LANE), F32)],
        scratch_shapes=[pltpu.VMEM((1, LANE), F32), pltpu.VMEM((1, LANE), F32)],
        compiler_params=_params(("arbitrary",), blocks, 2 * _nbytes((tm, tm), F32)),
        name="moe_rank",
    )(route)


def _dest_kernel(route_ref, rank_ref, starts_ref, dest_ref):
    oh1, oh2 = _one_hots(route_ref[...])
    rank = rank_ref[...]
    starts = starts_ref[...]
    d1 = jnp.sum(oh1 * starts, axis=-1, keepdims=True) + rank[:, 0:1]
    d2 = jnp.sum(oh2 * starts, axis=-1, keepdims=True) + rank[:, 1:2]
    lane = lax.broadcasted_iota(jnp.int32, rank.shape, 1)
    dest_ref[...] = jnp.where(lane == 0, d1, jnp.where(lane == 1, d2, 0.0)).astype(jnp.int32)


def _dest(route, rank, starts, tm):
    t = route.shape[0]
    blocks = 3 * _nbytes((tm, LANE), F32) + _nbytes((1, LANE), F32)
    return pl.pallas_call(
        _dest_kernel,
        grid=(t // tm,),
        in_specs=[pl.BlockSpec((tm, LANE), lambda i: (i, 0)), pl.BlockSpec((tm, LANE), lambda i: (i, 0)),
                  pl.BlockSpec((1, LANE), lambda i: (0, 0))],
        out_specs=pl.BlockSpec((tm, LANE), lambda i: (i, 0)),
        out_shape=jax.ShapeDtypeStruct((t, LANE), jnp.int32),
        compiler_params=_params(("parallel",), blocks, 4 * _nbytes((tm, LANE), F32)),
        name="moe_dest",
    )(route, rank, starts)


def _work_items(starts, n_rows):
    n_exp = starts.shape[0]
    n_blk = n_rows // MOE_ROWS
    total = jnp.full((1,), n_rows, jnp.int32)
    pts = jnp.concatenate([jnp.arange(n_blk, dtype=jnp.int32) * MOE_ROWS, starts[1:]])
    idx = jnp.arange(pts.shape[0], dtype=jnp.int32)
    before = (pts[None, :] < pts[:, None]) | ((pts[None, :] == pts[:, None]) & (idx[None, :] < idx[:, None]))
    pos = jnp.sum(before.astype(jnp.int32), axis=1)
    lo = jnp.sum(jnp.where(pos[:, None] == idx[None, :], pts[:, None], 0), axis=0)
    hi = jnp.concatenate([lo[1:], total])
    ends = jnp.concatenate([starts[1:], total])
    r = jnp.minimum(lo // MOE_ROWS, n_blk - 1)
    e = jnp.minimum(jnp.sum((ends[None, :] <= lo[:, None]).astype(jnp.int32), axis=1), n_exp - 1)
    later = jnp.where(e[None, :] > e[:, None], e[None, :], n_exp)
    nxt = jnp.min(later, axis=1)
    nxt = jnp.where(nxt == n_exp, -1, nxt)
    return r, e, lo, hi, nxt


def _stream_expert_weights(w, e_ref, nxt_ref, streams):
    e = e_ref[w]

    def copies(stream, expert):
        hbm, stage, _, sem = stream
        rows = stage.shape[0] // WEIGHT_DMA_PARTS
        parts = [pl.ds(part * rows, rows) for part in range(WEIGHT_DMA_PARTS)]
        return [pltpu.make_async_copy(hbm.at[expert, sl], stage.at[sl], sem) for sl in parts]

    @pl.when(w == 0)
    def _():
        for stream in streams:
            for c in copies(stream, e):
                c.start(priority=WEIGHT_DMA_PRIORITY)

    @pl.when((w == 0) | (e != e_ref[jnp.maximum(w - 1, 0)]))
    def _():
        nxt = nxt_ref[w]
        for stream in streams:
            for c in copies(stream, e):
                c.wait()
            _, stage, w_bf, _ = stream
            w_bf[...] = stage[...].astype(BF16)

            @pl.when(nxt >= 0)
            def _():
                for c in copies(stream, nxt):
                    c.start(priority=WEIGHT_DMA_PRIORITY)


def _store_item_rows(o_ref, val, r, lo, hi):
    rows = r * MOE_ROWS + lax.broadcasted_iota(jnp.int32, val.shape, 0)
    mine = (rows >= lo) & (rows < hi)

    @pl.when(lo == r * MOE_ROWS)
    def _():
        o_ref[...] = val

    @pl.when(lo != r * MOE_ROWS)
    def _():
        pltpu.store(o_ref, val, mask=mine)


def _gather_block_rows(w, r_ref, dest_ref, x_hbm, source, rows_f32, x_bf, sem, *, n_assign):
    r = r_ref[w]
    n_blocks = n_assign // MOE_ROWS

    def copy(src_row, slot, j):
        return pltpu.make_async_copy(x_hbm.at[pl.ds(src_row, 1)], rows_f32.at[slot, pl.ds(j, 1)], sem.at[slot])

    def start_block(block, slot):
        for j in range(MOE_ROWS):
            copy(source[block * MOE_ROWS + j], slot, j).start()

    def finish_block(slot):
        for j in range(MOE_ROWS):
            copy(0, slot, j).wait()
        x_bf[...] = rows_f32[slot].astype(BF16)

    @pl.when(w == 0)
    def _():
        def invert(tok, carry):
            for k in range(TOP_K):
                source[dest_ref[TOP_K * tok + k]] = tok
            return carry
        lax.fori_loop(0, n_assign // TOP_K, invert, 0, unroll=4)
        start_block(r, 0)

    @pl.when((w == 0) | (r != r_ref[jnp.maximum(w - 1, 0)]))
    def _():
        for slot in range(2):
            @pl.when(r % 2 == slot)
            def _():
                finish_block(slot)

                @pl.when(r + 1 < n_blocks)
                def _():
                    start_block(r + 1, 1 - slot)


def _moe_experts_kernel(r_ref, e_ref, lo_ref, hi_ref, nxt_ref, dest_ref, x_hbm, wg_hbm, wu_hbm, wd_hbm, y_ref,
                        wg_stage, wu_stage, wd_stage, wg_bf, wu_bf, wd_bf, source, rows_f32, x_bf, sem, row_sem,
                        *, n_assign):
    w = pl.program_id(0)
    _stream_expert_weights(w, e_ref, nxt_ref, [(wg_hbm, wg_stage, wg_bf, sem.at[0]),
                                                 (wu_hbm, wu_stage, wu_bf, sem.at[1]),
                                                 (wd_hbm, wd_stage, wd_bf, sem.at[2])])
    _gather_block_rows(w, r_ref, dest_ref, x_hbm, source, rows_f32, x_bf, row_sem, n_assign=n_assign)
    r, lo, hi = r_ref[w], lo_ref[w], hi_ref[w]

    @pl.when(hi > lo)
    def _():
        x = x_bf[...]
        g = jnp.dot(x, wg_bf[...], preferred_element_type=F32)
        u = jnp.dot(x, wu_bf[...], preferred_element_type=F32)
        h = ((g * (1.0 / (1.0 + jnp.exp(-g)))) * u).astype(BF16)
        y = jnp.dot(h, wd_bf[...], preferred_element_type=F32)
        _store_item_rows(y_ref, y, r, lo, hi)


def _moe_experts(items, dest_flat, xn, wg, wu, wd):
    n_assign = dest_flat.shape[0]
    d = xn.shape[1]
    f = wg.shape[2]
    n_items = items[0].shape[0]
    blocks = _nbytes((MOE_ROWS, d), F32)
    resident = (3 * _nbytes((d, f), F32) + 3 * _nbytes((d, f), BF16) + 5 * _nbytes((MOE_ROWS, d), F32)
                + _nbytes((MOE_ROWS, d), BF16) + 6 * _nbytes((MOE_ROWS, f), F32))
    any_space = pl.BlockSpec(memory_space=pl.ANY)
    return pl.pallas_call(
        functools.partial(_moe_experts_kernel, n_assign=n_assign),
        grid_spec=pltpu.PrefetchScalarGridSpec(
            num_scalar_prefetch=6,
            grid=(n_items,),
            in_specs=[any_space, any_space, any_space, any_space],
            out_specs=pl.BlockSpec((MOE_ROWS, d), lambda w, r, e, lo, hi, nxt, dest: (r[w], 0)),
            scratch_shapes=[pltpu.VMEM((d, f), F32), pltpu.VMEM((d, f), F32), pltpu.VMEM((f, d), F32),
                            pltpu.VMEM((d, f), BF16), pltpu.VMEM((d, f), BF16), pltpu.VMEM((f, d), BF16),
                            pltpu.SMEM((n_assign,), jnp.int32), pltpu.VMEM((2, MOE_ROWS, d), F32),
                            pltpu.VMEM((MOE_ROWS, d), BF16),
                            pltpu.SemaphoreType.DMA((3,)), pltpu.SemaphoreType.DMA((2,))]),
        out_shape=jax.ShapeDtypeStruct((n_assign, d), F32),
        compiler_params=_params(("arbitrary",), blocks, resident),
        name="moe_experts",
    )(*items, dest_flat, xn, wg, wu, wd)


def _combine_kernel(dest_ref, x_ref, route_ref, y_hbm, o_ref, ybuf, sem, *, tm):
    half = tm // 2
    i = pl.program_id(0)

    def copy(src_row, slot, k, t):
        return pltpu.make_async_copy(y_hbm.at[pl.ds(src_row, 1)], ybuf.at[slot, k, pl.ds(t, 1)], sem.at[slot])

    def start_half(index, slot):
        for t in range(half):
            for k in range(TOP_K):
                copy(dest_ref[TOP_K * (index * half + t) + k], slot, k, t).start(priority=ROW_DMA_PRIORITY)

    def finish_half(slot):
        for t in range(half):
            for k in range(TOP_K):
                copy(0, slot, k, t).wait()
        rows = slice(slot * half, (slot + 1) * half)
        route = route_ref[rows, :]
        o_ref[rows, :] = x_ref[rows, :] + (route[:, 2:3] * ybuf[slot, 0] + route[:, 3:4] * ybuf[slot, 1])

    @pl.when(i == 0)
    def _():
        start_half(2 * i, 0)

    start_half(2 * i + 1, 1)
    finish_half(0)

    @pl.when(i + 1 < pl.num_programs(0))
    def _():
        start_half(2 * i + 2, 0)

    finish_half(1)


def _combine(dest_flat, x1, route, y, tm):
    t, d = x1.shape
    blocks = 2 * _nbytes((tm, d), F32) + _nbytes((tm, LANE), F32)
    resident = TOP_K * _nbytes((tm, d), F32) + _nbytes((tm, d), F32)
    return pl.pallas_call(
        functools.partial(_combine_kernel, tm=tm),
        grid_spec=pltpu.PrefetchScalarGridSpec(
            num_scalar_prefetch=1,
            grid=(t // tm,),
            in_specs=[pl.BlockSpec((tm, d), lambda i, dest: (i, 0)),
                      pl.BlockSpec((tm, LANE), lambda i, dest: (i, 0)),
                      pl.BlockSpec(memory_space=pl.ANY)],
            out_specs=pl.BlockSpec((tm, d), lambda i, dest: (i, 0)),
            scratch_shapes=[pltpu.VMEM((2, TOP_K, tm // 2, d), F32), pltpu.SemaphoreType.DMA((2,))]),
        out_shape=jax.ShapeDtypeStruct((t, d), F32),
        compiler_params=_params(("arbitrary",), blocks, resident),
        name="moe_combine",
    )(dest_flat, x1, route, y)


def _pad_cols(w, n):
    return jnp.pad(w, ((0, 0), (0, n - w.shape[1])))


def kernel(x, positions, g_mix, w_in, b_gate, q_norm_g, kv_norm_g, w_uq, w_ukv, a_q_norm_g, a_k_norm_g,
           b_q_norm_g, b_k_norm_g, rel_bias, w_o_a, w_o_b, w_out, g_ffn, w_group, w_expert,
           w_exp_gate, w_exp_up, w_exp_down):
    batch, seq, d = x.shape
    t = batch * seq
    q_lora, kv_lora = q_norm_g.shape[0], kv_norm_g.shape[0]
    a_heads = w_uq.shape[1] // A_QK
    b_heads = w_o_b.shape[0] // B_HEAD_DIM
    b_width = b_heads * B_HEAD_DIM
    n_groups, n_experts = w_group.shape[1], w_expert.shape[1]
    per_group = n_experts // n_groups
    off_b = q_lora + kv_lora + A_ROPE
    assert seq % ATTN_TILE == 0 and (TOP_K * t) % MOE_ROWS == 0 and n_groups + n_experts <= LANE
    assert a_heads == b_heads

    xf = x.reshape(t, d)
    tm_big = min(1024, t)
    tm_half = min(512, t)
    tn_wide = _tile(b_width, 1024)
    assert d % tn_wide == 0

    za_cols = -(-(q_lora + kv_lora + LANE) // tn_wide) * tn_wide
    wuq = jnp.pad(w_uq.reshape(q_lora, a_heads, A_QK), ((0, 0), (0, 0), (0, A_HEAD_PAD - A_QK)))
    wuq = wuq.reshape(q_lora, a_heads * A_HEAD_PAD).astype(BF16)
    wukv = w_ukv.astype(BF16)
    pad_gain = lambda g, s: jnp.pad(g * s, (0, A_HEAD_PAD - A_QK)).reshape(1, A_HEAD_PAD)
    gaq = pad_gain(a_q_norm_g, A_QK ** -0.5)
    gak = pad_gain(a_k_norm_g, 1.0)
    g_bqk = jnp.concatenate([jnp.tile(b_q_norm_g * B_HEAD_DIM ** -0.5, b_heads), jnp.tile(b_k_norm_g, b_heads)])

    half = A_ROPE // 2
    inv = ROPE_THETA ** (-jnp.arange(half, dtype=F32) / half)
    ang = positions.astype(F32).reshape(t, 1) * inv
    cos, sin = jnp.cos(ang), jnp.sin(ang)
    zeros = jnp.zeros((t, half), F32)
    cos_t = jnp.concatenate([cos, cos, zeros, zeros], axis=1)
    sin_lo = jnp.concatenate([-sin, zeros, zeros, zeros], axis=1)
    sin_hi = jnp.concatenate([zeros, sin, zeros, zeros], axis=1)

    xn = _rmsnorm_rows(xf, g_mix, BF16, min(256, t))
    w_in_t = w_in.T
    za = _inproj_cols(xn, w_in_t, 0, za_cols, None, _plain_epilogue, tm_big, tn_wide)
    bqk = _inproj_cols(xn, w_in_t, off_b, 2 * b_width, g_bqk, _head_norm_epilogue, tm_big, tn_wide)
    bv = _inproj_cols(xn, w_in_t, off_b + 2 * b_width, b_width, None, _plain_epilogue, tm_big, tn_wide)
    gates = _inproj_cols(xn, w_in_t, off_b + 3 * b_width, 2 * d, b_gate, _sigmoid_epilogue, tm_big, tn_wide)
    q, k, v = _mla_proj(za, cos_t, sin_lo, sin_hi, wuq, wukv, q_norm_g.reshape(1, -1), kv_norm_g.reshape(1, -1),
                        gaq, gak, a_heads, min(256, t))
    o_a, o_b = _attention(q, k, v, bqk, bv, _band_bias_rows(rel_bias), batch, seq, a_heads)
    merged = _merge(o_a, o_b, w_o_a, w_o_b, gates, tm_half, tn_wide)
    x1 = _out_proj(merged, w_out, xf, tm_half, tn_wide)

    wr = _pad_cols(jnp.concatenate([w_group, w_expert], axis=1), LANE)
    xn2, route = _router(x1, g_ffn, wr, n_groups, per_group, min(256, t))
    rank, starts_f = _rank(route, min(512, t))
    dest = _dest(route, rank, starts_f, min(512, t))[:, :TOP_K].reshape(-1)
    items = _work_items(starts_f[0, :n_experts].astype(jnp.int32), TOP_K * t)
    y = _moe_experts(items, dest, xn2, w_exp_gate, w_exp_up, w_exp_down)
    out = _combine(dest, x1, route, y, min(256, t))
    return out.reshape(batch, seq, d)
```

```python
import functools

import jax
import jax.numpy as jnp
from jax import lax
from jax.experimental import pallas as pl
from jax.experimental.pallas import tpu as pltpu

F32 = jnp.float32
BF16 = jnp.bfloat16

CHUNK = 64
EPS = 1e-6
A_NOPE = 128
A_ROPE = 64
A_V = 128
A_QK = A_NOPE + A_ROPE
B_HEAD_DIM = 128
B_LEFT_CHUNKS = 8
B_MAX_REL = 128
ROPE_THETA = 10000.0
TOP_K = 2

LANE = 128
A_HEAD_PAD = 2 * LANE
V_PAD = 2 * LANE
V7X_VMEM_BYTES = 64 * 2**20

ATTN_TILE = 256
MOE_ROWS = 128
WEIGHT_DMA_PRIORITY = 1
ROW_DMA_PRIORITY = 1
PROJ_GROUP_ROWS = 256
NEG_INF = float("-inf")


def _nbytes(shape, dtype):
    n = 1
    for s in shape:
        n *= s
    return n * jnp.dtype(dtype).itemsize


def _params(semantics, pipelined_bytes, resident_bytes=0):
    need = 2 * pipelined_bytes + resident_bytes
    return pltpu.CompilerParams(dimension_semantics=semantics,
                                vmem_limit_bytes=min(int(need), V7X_VMEM_BYTES))


def _tile(n, want):
    t = want
    while t > LANE and n % t:
        t //= 2
    assert n % t == 0, (n, want)
    return t


def _rmsnorm_kernel(x_ref, g_ref, o_ref):
    x = x_ref[...]
    r = lax.rsqrt(jnp.mean(x * x, axis=-1, keepdims=True) + EPS)
    o_ref[...] = (x * r * g_ref[...]).astype(o_ref.dtype)


def _rmsnorm_rows(x, g, out_dtype, tm):
    t, d = x.shape
    blocks = _nbytes((tm, d), F32) + _nbytes((tm, d), out_dtype) + _nbytes((1, d), F32)
    return pl.pallas_call(
        _rmsnorm_kernel,
        grid=(t // tm,),
        in_specs=[pl.BlockSpec((tm, d), lambda i: (i, 0)), pl.BlockSpec((1, d), lambda i: (0, 0))],
        out_specs=pl.BlockSpec((tm, d), lambda i: (i, 0)),
        out_shape=jax.ShapeDtypeStruct((t, d), out_dtype),
        compiler_params=_params(("parallel",), blocks, _nbytes((tm, d), F32)),
        name="rmsnorm",
    )(x, g.reshape(1, d))


def _nt_dot(a, b):
    return lax.dot_general(a, b, (((1,), (1,)), ((), ())), preferred_element_type=F32)


def _stage_weight_blocks(streams):
    j = pl.program_id(0)

    @pl.when(pl.program_id(1) == 0)
    def _():
        @pl.when(j == 0)
        def _():
            for fetch, _, _ in streams:
                fetch(j).start(priority=WEIGHT_DMA_PRIORITY)

        for fetch, stage, w_bf in streams:
            fetch(j).wait()
            w_bf[...] = stage[...].astype(BF16)

            @pl.when(j + 1 < pl.num_programs(0))
            def _():
                fetch(j + 1).start(priority=WEIGHT_DMA_PRIORITY)


def _row_groups(rows):
    step = min(rows, PROJ_GROUP_ROWS)
    assert rows % step == 0
    return [slice(s, s + step) for s in range(0, rows, step)]


def _head_norm_epilogue(acc, gb_ref):
    heads = []
    for h in range(acc.shape[1] // B_HEAD_DIM):
        sl = slice(h * B_HEAD_DIM, (h + 1) * B_HEAD_DIM)
        z = acc[:, sl]
        r = lax.rsqrt(jnp.mean(z * z, axis=-1, keepdims=True) + EPS)
        heads.append(z * r * gb_ref[:, sl])
    return jnp.concatenate(heads, axis=1)


def _plain_epilogue(acc, gb_ref):
    del gb_ref
    return acc


def _sigmoid_epilogue(acc, gb_ref):
    return 1.0 / (1.0 + jnp.exp(-(acc + gb_ref[...])))


def _inproj_cols_kernel(a_ref, wt_hbm, *refs, first_col, epilogue):
    gb_ref = refs[0] if len(refs) == 5 else None
    o_ref, stage, wt_bf, sem = refs[-4:]
    tn = wt_bf.shape[0]

    def fetch(block):
        rows = pl.ds(pl.multiple_of(first_col + block * tn, 8), tn)
        return pltpu.make_async_copy(wt_hbm.at[rows], stage, sem)

    _stage_weight_blocks([(fetch, stage, wt_bf)])
    groups = _row_groups(a_ref.shape[0])
    accs = [_nt_dot(a_ref[sl, :], wt_bf[...]) for sl in groups]
    for sl, acc in zip(groups, accs):
        o_ref[sl, :] = epilogue(acc, gb_ref).astype(o_ref.dtype)


def _inproj_cols(xn, w_in_t, first_col, n_cols, gb, epilogue, tm, tn):
    m, k = xn.shape
    assert n_cols % tn == 0 and first_col % 8 == 0
    blocks = _nbytes((tm, k), BF16) + _nbytes((tm, tn), BF16) + _nbytes((1, tn), F32)
    resident = _nbytes((tn, k), F32) + _nbytes((tn, k), BF16) + 3 * _nbytes((tm, tn), F32)
    gb_specs = [] if gb is None else [pl.BlockSpec((1, tn), lambda j, i: (0, j))]
    gb_args = [] if gb is None else [gb.reshape(1, n_cols)]
    return pl.pallas_call(
        functools.partial(_inproj_cols_kernel, first_col=first_col, epilogue=epilogue),
        grid=(n_cols // tn, m // tm),
        in_specs=[pl.BlockSpec((tm, k), lambda j, i: (i, 0)), pl.BlockSpec(memory_space=pl.ANY)] + gb_specs,
        out_specs=pl.BlockSpec((tm, tn), lambda j, i: (i, j)),
        out_shape=jax.ShapeDtypeStruct((m, n_cols), BF16),
        scratch_shapes=[pltpu.VMEM((tn, k), F32), pltpu.VMEM((tn, k), BF16), pltpu.SemaphoreType.DMA(())],
        compiler_params=_params(("arbitrary", "arbitrary"), blocks, resident),
        name="inproj_" + epilogue.__name__.strip("_").replace("_epilogue", ""),
    )(xn, w_in_t, *gb_args)


def _rope_padded(v, cos, sin_lo, sin_hi):
    half = A_ROPE // 2
    return v * cos + pltpu.roll(v, LANE - half, 1) * sin_lo + pltpu.roll(v, half, 1) * sin_hi


def _ones_column(rows):
    lane = lax.broadcasted_iota(jnp.int32, (rows, V_PAD - A_V), 1)
    return jnp.where(lane == 0, 1.0, 0.0).astype(BF16)


def _mla_proj_kernel(za_ref, cos_ref, sl_ref, sh_ref, wuq_ref, wukv_ref, gq_ref, gkv_ref, gaq_ref, gak_ref,
                     q_ref, k_ref, v_ref, *, heads, q_lora, kv_lora):
    cos, sin_lo, sin_hi = cos_ref[...], sl_ref[...], sh_ref[...]

    def norm(z, g):
        z = z.astype(F32)
        r = lax.rsqrt(jnp.mean(z * z, axis=-1, keepdims=True) + EPS)
        return (z * r * g).astype(BF16)

    cq = norm(za_ref[:, :q_lora], gq_ref[...])
    ckv = norm(za_ref[:, q_lora:q_lora + kv_lora], gkv_ref[...])
    slab = za_ref[:, q_lora + kv_lora:q_lora + kv_lora + LANE].astype(F32)
    k_rope = jnp.where(lax.broadcasted_iota(jnp.int32, slab.shape, 1) < A_ROPE, slab, 0.0)
    qacc = jnp.dot(cq, wuq_ref[...], preferred_element_type=F32)
    kvacc = jnp.dot(ckv, wukv_ref[...], preferred_element_type=F32)

    gq_lo, gq_hi = gaq_ref[:, :LANE], gaq_ref[:, LANE:]
    gk_lo, gk_hi = gak_ref[:, :LANE], gak_ref[:, LANE:]
    kr_ss = jnp.sum(k_rope * k_rope, axis=-1, keepdims=True)
    kr_rot = _rope_padded(k_rope * gk_hi, cos, sin_lo, sin_hi)
    ones_col = _ones_column(slab.shape[0])
    for h in range(heads):
        base = h * A_HEAD_PAD
        q_lo = qacc[:, base:base + LANE]
        q_hi = qacc[:, base + LANE:base + A_HEAD_PAD]
        ss = jnp.sum(q_lo * q_lo, axis=-1, keepdims=True) + jnp.sum(q_hi * q_hi, axis=-1, keepdims=True)
        r = lax.rsqrt(ss / A_QK + EPS)
        q_ref[:, base:base + LANE] = (q_lo * r * gq_lo).astype(BF16)
        q_ref[:, base + LANE:base + A_HEAD_PAD] = _rope_padded(q_hi * r * gq_hi, cos, sin_lo, sin_hi).astype(BF16)

        k_lo = kvacc[:, base:base + LANE]
        ssk = jnp.sum(k_lo * k_lo, axis=-1, keepdims=True) + kr_ss
        rk = lax.rsqrt(ssk / A_QK + EPS)
        k_ref[:, base:base + LANE] = (k_lo * rk * gk_lo).astype(BF16)
        k_ref[:, base + LANE:base + A_HEAD_PAD] = (kr_rot * rk).astype(BF16)
        v_ref[:, h * V_PAD:h * V_PAD + A_V] = kvacc[:, base + LANE:base + A_HEAD_PAD].astype(BF16)
        v_ref[:, h * V_PAD + A_V:(h + 1) * V_PAD] = ones_col


def _mla_proj(za, cos, sin_lo, sin_hi, wuq, wukv, gq, gkv, gaq, gak, heads, tm):
    t, za_cols = za.shape
    q_lora, kv_lora = wuq.shape[0], wukv.shape[0]
    hp = heads * A_HEAD_PAD
    row = lambda i: (i, 0)
    fix = lambda i: (0, 0)
    blocks = (_nbytes((tm, za_cols), za.dtype) + 3 * _nbytes((tm, LANE), F32) + _nbytes(wuq.shape, BF16)
              + _nbytes(wukv.shape, BF16) + 2 * _nbytes((tm, hp), BF16) + _nbytes((tm, heads * V_PAD), BF16))
    return pl.pallas_call(
        functools.partial(_mla_proj_kernel, heads=heads, q_lora=q_lora, kv_lora=kv_lora),
        grid=(t // tm,),
        in_specs=[pl.BlockSpec((tm, za_cols), row),
                  pl.BlockSpec((tm, LANE), row), pl.BlockSpec((tm, LANE), row), pl.BlockSpec((tm, LANE), row),
                  pl.BlockSpec(wuq.shape, fix), pl.BlockSpec(wukv.shape, fix),
                  pl.BlockSpec((1, q_lora), fix), pl.BlockSpec((1, kv_lora), fix),
                  pl.BlockSpec((1, A_HEAD_PAD), fix), pl.BlockSpec((1, A_HEAD_PAD), fix)],
        out_specs=[pl.BlockSpec((tm, hp), row), pl.BlockSpec((tm, hp), row), pl.BlockSpec((tm, heads * V_PAD), row)],
        out_shape=[jax.ShapeDtypeStruct((t, hp), BF16), jax.ShapeDtypeStruct((t, hp), BF16),
                   jax.ShapeDtypeStruct((t, heads * V_PAD), BF16)],
        compiler_params=_params(("parallel",), blocks, 3 * _nbytes((tm, hp), F32)),
        name="mla_proj",
    )(za, cos, sin_lo, sin_hi, wuq, wukv, gq, gkv, gaq, gak)


def _mla_scores(i, q_ref, k_ref, diag_ok):
    tq = ATTN_TILE
    q = q_ref[i * tq:(i + 1) * tq, :]
    sd = jnp.where(diag_ok, _nt_dot(q, k_ref[i * tq:(i + 1) * tq, :]), NEG_INF)
    s0 = _nt_dot(q, k_ref[:i * tq, :]) if i > 0 else None
    return sd, s0


def _mla_softmax(sd, s0):
    m = jnp.max(sd, axis=-1, keepdims=True)
    if s0 is None:
        return jnp.exp(sd - m).astype(BF16), None
    m = jnp.maximum(m, jnp.max(s0, axis=-1, keepdims=True))
    return jnp.exp(sd - m).astype(BF16), jnp.exp(s0 - m).astype(BF16)


def _mla_values(i, pd, p0, v_ref, o_ref):
    tq = ATTN_TILE
    o = jnp.dot(pd, v_ref[i * tq:(i + 1) * tq, :], preferred_element_type=F32)
    if p0 is not None:
        o = o + jnp.dot(p0, v_ref[:i * tq, :], preferred_element_type=F32)
    o_ref[i * tq:(i + 1) * tq, :] = (o[:, :A_V] / o[:, A_V:A_V + 1]).astype(o_ref.dtype)


def _band_window(i):
    left = B_LEFT_CHUNKS * CHUNK
    q0 = i * ATTN_TILE
    k0 = max(0, q0 - left)
    return q0, k0, q0 + ATTN_TILE - k0, left - q0 + k0


def _band_scores(i, q_ref, k_ref, table):
    q0, k0, kw, u0 = _band_window(i)
    return _nt_dot(q_ref[q0:q0 + ATTN_TILE, :], k_ref[k0:k0 + kw, :]) + table[:, u0:u0 + kw]


def _band_softmax(s):
    return jnp.exp(s - jnp.max(s, axis=-1, keepdims=True)).astype(BF16)


def _band_values(i, p, v_ref, o_ref):
    q0, k0, kw, _ = _band_window(i)
    o = jnp.dot(p, v_ref[k0:k0 + kw, :], preferred_element_type=F32)
    o_ref[q0:q0 + ATTN_TILE, :] = (o[:, :B_HEAD_DIM] / o[:, B_HEAD_DIM:B_HEAD_DIM + 1]).astype(o_ref.dtype)


def _attn_kernel(aq_ref, ak_ref, av_ref, bq_ref, bk_ref, bv_ref, r_ref, oa_ref, ob_ref, bv_pad, *, seq):
    tq = ATTN_TILE
    left = B_LEFT_CHUNKS * CHUNK
    width = r_ref.shape[1]
    rc = lax.broadcasted_iota(jnp.int32, (tq, tq), 0) // CHUNK
    cc = lax.broadcasted_iota(jnp.int32, (tq, tq), 1) // CHUNK
    diag_ok = cc <= rc
    bias = pltpu.roll(jnp.broadcast_to(r_ref[...], (tq, width)), 0, 1, stride=1, stride_axis=0)[:, :left + tq]
    q_chunk = lax.broadcasted_iota(jnp.int32, (tq, left + tq), 0) // CHUNK
    k_chunk = lax.broadcasted_iota(jnp.int32, (tq, left + tq), 1) // CHUNK
    table = jnp.where((k_chunk >= q_chunk) & (k_chunk <= q_chunk + B_LEFT_CHUNKS), bias, NEG_INF)
    bv_pad[:, :B_HEAD_DIM] = bv_ref[...]
    bv_pad[:, B_HEAD_DIM:] = _ones_column(seq)

    n_tiles = seq // tq
    order = [n_tiles - 1 - i // 2 if i % 2 == 0 else i // 2 for i in range(n_tiles)]
    sd, s0 = _mla_scores(order[0], aq_ref, ak_ref, diag_ok)
    for n, i in enumerate(order):
        sb = _band_scores(i, bq_ref, bk_ref, table)
        pd, p0 = _mla_softmax(sd, s0)
        if n + 1 < n_tiles:
            sd, s0 = _mla_scores(order[n + 1], aq_ref, ak_ref, diag_ok)
        _mla_values(i, pd, p0, av_ref, oa_ref)
        _band_values(i, _band_softmax(sb), bv_pad, ob_ref)


def _band_bias_rows(rel_bias):
    left = B_LEFT_CHUNKS * CHUNK
    width = left + 2 * ATTN_TILE
    m = jnp.arange(width, dtype=jnp.int32)
    j = jnp.where(m < left + ATTN_TILE, m, m - width)
    dist = left - j
    rows = rel_bias[:, jnp.clip(dist, -B_MAX_REL, B_MAX_REL) + B_MAX_REL].astype(F32)
    return rows.reshape(rel_bias.shape[0], 1, width)


def _attention(q, k, v, bqk, bv, bias_rows, batch, seq, heads):
    t = batch * seq
    d = B_HEAD_DIM
    width = bias_rows.shape[2]
    head = lambda b, h: (b, h)
    blocks = (2 * _nbytes((seq, A_HEAD_PAD), BF16) + _nbytes((seq, V_PAD), BF16) + _nbytes((seq, A_V), BF16)
              + 4 * _nbytes((seq, d), BF16) + _nbytes((1, width), F32))
    temps = _nbytes((seq, V_PAD), BF16) + 8 * _nbytes((ATTN_TILE, seq), F32) + 8 * _nbytes((ATTN_TILE, width), F32)
    return pl.pallas_call(
        functools.partial(_attn_kernel, seq=seq),
        grid=(batch, heads),
        in_specs=[pl.BlockSpec((seq, A_HEAD_PAD), head), pl.BlockSpec((seq, A_HEAD_PAD), head),
                  pl.BlockSpec((seq, V_PAD), head),
                  pl.BlockSpec((seq, d), head),
                  pl.BlockSpec((seq, d), lambda b, h: (b, heads + h)),
                  pl.BlockSpec((seq, d), head),
                  pl.BlockSpec((None, 1, width), lambda b, h: (h, 0, 0))],
        out_specs=[pl.BlockSpec((seq, A_V), head), pl.BlockSpec((seq, d), head)],
        out_shape=[jax.ShapeDtypeStruct((t, heads * A_V), BF16), jax.ShapeDtypeStruct((t, heads * d), BF16)],
        scratch_shapes=[pltpu.VMEM((seq, V_PAD), BF16)],
        compiler_params=_params(("parallel", "parallel"), blocks, temps),
        name="attention",
    )(q, k, v, bqk, bqk, bv, bias_rows)


def _column_block_fetch(w_hbm, stage, sem):
    tn = stage.shape[1]
    return lambda block: pltpu.make_async_copy(w_hbm.at[:, pl.ds(pl.multiple_of(block * tn, LANE), tn)], stage, sem)


def _merge_kernel(oa_ref, ob_ref, woa_hbm, wob_hbm, g0_ref, g1_ref, o_ref, a_stage, b_stage, woa_bf, wob_bf, sem):
    _stage_weight_blocks([(_column_block_fetch(woa_hbm, a_stage, sem.at[0]), a_stage, woa_bf),
                          (_column_block_fetch(wob_hbm, b_stage, sem.at[1]), b_stage, wob_bf)])
    for sl in _row_groups(oa_ref.shape[0]):
        a = jnp.dot(oa_ref[sl, :], woa_bf[...], preferred_element_type=F32)
        b = jnp.dot(ob_ref[sl, :], wob_bf[...], preferred_element_type=F32)
        o_ref[sl, :] = (g0_ref[sl, :].astype(F32) * a + g1_ref[sl, :].astype(F32) * b).astype(o_ref.dtype)


def _merge(oa, ob, woa, wob, gates, tm, tn):
    m = oa.shape[0]
    ka, kb = woa.shape[0], wob.shape[0]
    d = woa.shape[1]
    assert d % tn == 0
    g1 = d // tn
    blocks = _nbytes((tm, ka), BF16) + _nbytes((tm, kb), BF16) + 3 * _nbytes((tm, tn), BF16)
    resident = (_nbytes((ka + kb, tn), F32) + 2 * _nbytes((ka + kb, tn), BF16) + 3 * _nbytes((tm, tn), F32))
    return pl.pallas_call(
        _merge_kernel,
        grid=(d // tn, m // tm),
        in_specs=[pl.BlockSpec((tm, ka), lambda j, i: (i, 0)),
                  pl.BlockSpec((tm, kb), lambda j, i: (i, 0)),
                  pl.BlockSpec(memory_space=pl.ANY), pl.BlockSpec(memory_space=pl.ANY),
                  pl.BlockSpec((tm, tn), lambda j, i: (i, j)),
                  pl.BlockSpec((tm, tn), lambda j, i: (i, g1 + j))],
        out_specs=pl.BlockSpec((tm, tn), lambda j, i: (i, j)),
        out_shape=jax.ShapeDtypeStruct((m, d), BF16),
        scratch_shapes=[pltpu.VMEM((ka, tn), F32), pltpu.VMEM((kb, tn), F32),
                        pltpu.VMEM((ka, tn), BF16), pltpu.VMEM((kb, tn), BF16), pltpu.SemaphoreType.DMA((2,))],
        compiler_params=_params(("arbitrary", "arbitrary"), blocks, resident),
        name="merge",
    )(oa, ob, woa, wob, gates, gates)


def _out_proj_kernel(a_ref, w_hbm, x_ref, o_ref, stage, w_bf, sem):
    _stage_weight_blocks([(_column_block_fetch(w_hbm, stage, sem), stage, w_bf)])
    groups = _row_groups(a_ref.shape[0])
    accs = [jnp.dot(a_ref[sl, :], w_bf[...], preferred_element_type=F32) for sl in groups]
    for sl, acc in zip(groups, accs):
        o_ref[sl, :] = x_ref[sl, :] + acc


def _out_proj(merged, wout, x, tm, tn):
    m, k = merged.shape
    n = wout.shape[1]
    assert n % tn == 0
    blocks = _nbytes((tm, k), BF16) + 2 * _nbytes((tm, tn), F32)
    resident = _nbytes((k, tn), F32) + 2 * _nbytes((k, tn), BF16) + 2 * _nbytes((tm, tn), F32)
    return pl.pallas_call(
        _out_proj_kernel,
        grid=(n // tn, m // tm),
        in_specs=[pl.BlockSpec((tm, k), lambda j, i: (i, 0)),
                  pl.BlockSpec(memory_space=pl.ANY),
                  pl.BlockSpec((tm, tn), lambda j, i: (i, j))],
        out_specs=pl.BlockSpec((tm, tn), lambda j, i: (i, j)),
        out_shape=jax.ShapeDtypeStruct((m, n), F32),
        scratch_shapes=[pltpu.VMEM((k, tn), F32), pltpu.VMEM((k, tn), BF16), pltpu.SemaphoreType.DMA(())],
        compiler_params=_params(("arbitrary", "arbitrary"), blocks, resident),
        name="out_proj",
    )(merged, wout, x)


def _split_bf16(v):
    hi = v.astype(BF16)
    return hi, (v - hi.astype(F32)).astype(BF16)


def _router_kernel(x_ref, g_ref, wr_ref, xn_ref, route_ref, *, n_groups, per_group):
    x = x_ref[...]
    r = lax.rsqrt(jnp.mean(x * x, axis=-1, keepdims=True) + EPS)
    xn = x * r * g_ref[...]
    xn_ref[...] = xn
    x_hi, x_lo = _split_bf16(xn)
    w_hi, w_lo = _split_bf16(wr_ref[...])
    dot = functools.partial(jnp.dot, preferred_element_type=F32)
    logits = dot(x_hi, w_hi) + (dot(x_lo, w_hi) + dot(x_hi, w_lo))
    lane = lax.broadcasted_iota(jnp.int32, logits.shape, 1).astype(F32)
    far = float(LANE)

    def top(vals):
        best = jnp.max(vals, axis=-1, keepdims=True)
        return best, jnp.min(jnp.where(vals == best, lane, far), axis=-1, keepdims=True)

    gl = jnp.where(lane < n_groups, logits, NEG_INF)
    gmax, grp = top(gl)
    p_grp = 1.0 / jnp.sum(jnp.exp(gl - gmax), axis=-1, keepdims=True)
    lo = n_groups + grp * per_group
    el = jnp.where((lane >= lo) & (lane < lo + per_group), logits, NEG_INF)
    t1, i1 = top(el)
    t2, i2 = top(jnp.where(lane == i1, NEG_INF, el))
    d = jnp.exp(t2 - t1)
    w1 = p_grp / (1.0 + d)
    w2 = p_grp * d / (1.0 + d)
    route_ref[...] = jnp.where(lane == 0, i1 - n_groups,
                               jnp.where(lane == 1, i2 - n_groups,
                                         jnp.where(lane == 2, w1, jnp.where(lane == 3, w2, 0.0))))


def _router(x1, g, wr, n_groups, per_group, tm):
    t, d = x1.shape
    blocks = 2 * _nbytes((tm, d), F32) + _nbytes((1, d), F32) + _nbytes((d, LANE), F32) + _nbytes((tm, LANE), F32)
    return pl.pallas_call(
        functools.partial(_router_kernel, n_groups=n_groups, per_group=per_group),
        grid=(t // tm,),
        in_specs=[pl.BlockSpec((tm, d), lambda i: (i, 0)), pl.BlockSpec((1, d), lambda i: (0, 0)),
                  pl.BlockSpec((d, LANE), lambda i: (0, 0))],
        out_specs=[pl.BlockSpec((tm, d), lambda i: (i, 0)), pl.BlockSpec((tm, LANE), lambda i: (i, 0))],
        out_shape=[jax.ShapeDtypeStruct((t, d), F32), jax.ShapeDtypeStruct((t, LANE), F32)],
        compiler_params=_params(("parallel",), blocks, 2 * _nbytes((tm, d), F32)),
        name="router",
    )(x1, g.reshape(1, d), wr)


def _one_hots(route):
    lane = lax.broadcasted_iota(jnp.int32, route.shape, 1).astype(F32)
    return (lane == route[:, 0:1]).astype(F32), (lane == route[:, 1:2]).astype(F32)


def _rank_kernel(route_ref, rank_ref, starts_ref, count_acc, start_acc):
    i = pl.program_id(0)

    @pl.when(i == 0)
    def _():
        count_acc[...] = jnp.zeros_like(count_acc)
        start_acc[...] = jnp.zeros_like(start_acc)

    oh1, oh2 = _one_hots(route_ref[...])
    oh = (oh1 + oh2).astype(BF16)
    tm = oh.shape[0]
    earlier = (lax.broadcasted_iota(jnp.int32, (tm, tm), 0) > lax.broadcasted_iota(jnp.int32, (tm, tm), 1))
    before = jnp.dot(earlier.astype(BF16), oh, preferred_element_type=F32) + count_acc[...]
    lane = lax.broadcasted_iota(jnp.int32, (tm, LANE), 1)
    rank_ref[...] = jnp.where(lane == 0, jnp.sum(oh1 * before, axis=-1, keepdims=True),
                              jnp.where(lane == 1, jnp.sum(oh2 * before, axis=-1, keepdims=True), 0.0))
    lower = (lax.broadcasted_iota(jnp.int32, (LANE, LANE), 0) < lax.broadcasted_iota(jnp.int32, (LANE, LANE), 1))
    below = jnp.dot(oh, lower.astype(BF16), preferred_element_type=F32)
    count_acc[...] += jnp.sum(oh.astype(F32), axis=0, keepdims=True)
    start_acc[...] += jnp.sum(below, axis=0, keepdims=True)
    starts_ref[...] = start_acc[...]


def _rank(route, tm):
    t = route.shape[0]
    blocks = 2 * _nbytes((tm, LANE), F32) + _nbytes((1, LANE), F32)
    return pl.pallas_call(
        _rank_kernel,
        grid=(t // tm,),
        in_specs=[pl.BlockSpec((tm, LANE), lambda i: (i, 0))],
        out_specs=[pl.BlockSpec((tm, LANE), lambda i: (i, 0)), pl.BlockSpec((1, LANE), lambda i: (0, 0))],
        out_shape=[jax.ShapeDtypeStruct((t, LANE), F32), jax.ShapeDtypeStruct((1, LANE), F32)],
        scratch_shapes=[pltpu.VMEM((1, LANE), F32), pltpu.VMEM((1, LANE), F32)],
        compiler_params=_params(("arbitrary",), blocks, 2 * _nbytes((tm, tm), F32)),
        name="moe_rank",
    )(route)


def _dest_kernel(route_ref, rank_ref, starts_ref, dest_ref):
    oh1, oh2 = _one_hots(route_ref[...])
    rank = rank_ref[...]
    starts = starts_ref[...]
    d1 = jnp.sum(oh1 * starts, axis=-1, keepdims=True) + rank[:, 0:1]
    d2 = jnp.sum(oh2 * starts, axis=-1, keepdims=True) + rank[:, 1:2]
    lane = lax.broadcasted_iota(jnp.int32, rank.shape, 1)
    dest_ref[...] = jnp.where(lane == 0, d1, jnp.where(lane == 1, d2, 0.0)).astype(jnp.int32)


def _dest(route, rank, starts, tm):
    t = route.shape[0]
    blocks = 3 * _nbytes((tm, LANE), F32) + _nbytes((1, LANE), F32)
    return pl.pallas_call(
        _dest_kernel,
        grid=(t // tm,),
        in_specs=[pl.BlockSpec((tm, LANE), lambda i: (i, 0)), pl.BlockSpec((tm, LANE), lambda i: (i, 0)),
                  pl.BlockSpec((1, LANE), lambda i: (0, 0))],
        out_specs=pl.BlockSpec((tm, LANE), lambda i: (i, 0)),
        out_shape=jax.ShapeDtypeStruct((t, LANE), jnp.int32),
        compiler_params=_params(("parallel",), blocks, 4 * _nbytes((tm, LANE), F32)),
        name="moe_dest",
    )(route, rank, starts)


def _work_items(starts, n_rows):
    n_exp = starts.shape[0]
    n_blk = n_rows // MOE_ROWS
    total = jnp.full((1,), n_rows, jnp.int32)
    pts = jnp.concatenate([jnp.arange(n_blk, dtype=jnp.int32) * MOE_ROWS, starts[1:]])
    idx = jnp.arange(pts.shape[0], dtype=jnp.int32)
    before = (pts[None, :] < pts[:, None]) | ((pts[None, :] == pts[:, None]) & (idx[None, :] < idx[:, None]))
    pos = jnp.sum(before.astype(jnp.int32), axis=1)
    lo = jnp.sum(jnp.where(pos[:, None] == idx[None, :], pts[:, None], 0), axis=0)
    hi = jnp.concatenate([lo[1:], total])
    ends = jnp.concatenate([starts[1:], total])
    r = jnp.minimum(lo // MOE_ROWS, n_blk - 1)
    e = jnp.minimum(jnp.sum((ends[None, :] <= lo[:, None]).astype(jnp.int32), axis=1), n_exp - 1)
    later = jnp.where(e[None, :] > e[:, None], e[None, :], n_exp)
    nxt = jnp.min(later, axis=1)
    nxt = jnp.where(nxt == n_exp, -1, nxt)
    return r, e, lo, hi, nxt


def _stream_expert_weights(w, e_ref, nxt_ref, streams):
    e = e_ref[w]

    def copy(stream, expert):
        hbm, stage, _, sem = stream
        return pltpu.make_async_copy(hbm.at[expert], stage, sem)

    @pl.when(w == 0)
    def _():
        for stream in streams:
            copy(stream, e).start(priority=WEIGHT_DMA_PRIORITY)

    @pl.when((w == 0) | (e != e_ref[jnp.maximum(w - 1, 0)]))
    def _():
        nxt = nxt_ref[w]
        for stream in streams:
            copy(stream, e).wait()
            _, stage, w_bf, _ = stream
            w_bf[...] = stage[...].astype(BF16)

            @pl.when(nxt >= 0)
            def _():
                copy(stream, nxt).start(priority=WEIGHT_DMA_PRIORITY)


def _store_item_rows(o_ref, val, r, lo, hi):
    rows = r * MOE_ROWS + lax.broadcasted_iota(jnp.int32, val.shape, 0)
    mine = (rows >= lo) & (rows < hi)

    @pl.when(lo == r * MOE_ROWS)
    def _():
        o_ref[...] = val

    @pl.when(lo != r * MOE_ROWS)
    def _():
        pltpu.store(o_ref, val, mask=mine)


def _gather_block_rows(w, r_ref, dest_ref, x_hbm, source, rows_f32, x_bf, sem, *, n_assign):
    r = r_ref[w]
    n_blocks = n_assign // MOE_ROWS

    def copy(src_row, slot, j):
        return pltpu.make_async_copy(x_hbm.at[pl.ds(src_row, 1)], rows_f32.at[slot, pl.ds(j, 1)], sem.at[slot])

    def start_block(block, slot):
        for j in range(MOE_ROWS):
            copy(source[block * MOE_ROWS + j], slot, j).start()

    def finish_block(slot):
        for j in range(MOE_ROWS):
            copy(0, slot, j).wait()
        x_bf[...] = rows_f32[slot].astype(BF16)

    @pl.when(w == 0)
    def _():
        def invert(tok, carry):
            for k in range(TOP_K):
                source[dest_ref[TOP_K * tok + k]] = tok
            return carry
        lax.fori_loop(0, n_assign // TOP_K, invert, 0, unroll=4)
        start_block(r, 0)

    @pl.when((w == 0) | (r != r_ref[jnp.maximum(w - 1, 0)]))
    def _():
        for slot in range(2):
            @pl.when(r % 2 == slot)
            def _():
                finish_block(slot)

                @pl.when(r + 1 < n_blocks)
                def _():
                    start_block(r + 1, 1 - slot)


def _moe_experts_kernel(r_ref, e_ref, lo_ref, hi_ref, nxt_ref, dest_ref, x_hbm, wg_hbm, wu_hbm, wd_hbm, y_ref,
                        wg_stage, wu_stage, wd_stage, wg_bf, wu_bf, wd_bf, source, rows_f32, x_bf, sem, row_sem,
                        *, n_assign):
    w = pl.program_id(0)
    _stream_expert_weights(w, e_ref, nxt_ref, [(wg_hbm, wg_stage, wg_bf, sem.at[0]),
                                                 (wu_hbm, wu_stage, wu_bf, sem.at[1]),
                                                 (wd_hbm, wd_stage, wd_bf, sem.at[2])])
    _gather_block_rows(w, r_ref, dest_ref, x_hbm, source, rows_f32, x_bf, row_sem, n_assign=n_assign)
    r, lo, hi = r_ref[w], lo_ref[w], hi_ref[w]

    @pl.when(hi > lo)
    def _():
        x = x_bf[...]
        g = jnp.dot(x, wg_bf[...], preferred_element_type=F32)
        u = jnp.dot(x, wu_bf[...], preferred_element_type=F32)
        h = ((g * (1.0 / (1.0 + jnp.exp(-g)))) * u).astype(BF16)
        y = jnp.dot(h, wd_bf[...], preferred_element_type=F32)
        _store_item_rows(y_ref, y, r, lo, hi)


def _moe_experts(items, dest_flat, xn, wg, wu, wd):
    n_assign = dest_flat.shape[0]
    d = xn.shape[1]
    f = wg.shape[2]
    n_items = items[0].shape[0]
    blocks = _nbytes((MOE_ROWS, d), F32)
    resident = (3 * _nbytes((d, f), F32) + 3 * _nbytes((d, f), BF16) + 5 * _nbytes((MOE_ROWS, d), F32)
                + _nbytes((MOE_ROWS, d), BF16) + 6 * _nbytes((MOE_ROWS, f), F32))
    any_space = pl.BlockSpec(memory_space=pl.ANY)
    return pl.pallas_call(
        functools.partial(_moe_experts_kernel, n_assign=n_assign),
        grid_spec=pltpu.PrefetchScalarGridSpec(
            num_scalar_prefetch=6,
            grid=(n_items,),
            in_specs=[any_space, any_space, any_space, any_space],
            out_specs=pl.BlockSpec((MOE_ROWS, d), lambda w, r, e, lo, hi, nxt, dest: (r[w], 0)),
            scratch_shapes=[pltpu.VMEM((d, f), F32), pltpu.VMEM((d, f), F32), pltpu.VMEM((f, d), F32),
                            pltpu.VMEM((d, f), BF16), pltpu.VMEM((d, f), BF16), pltpu.VMEM((f, d), BF16),
                            pltpu.SMEM((n_assign,), jnp.int32), pltpu.VMEM((2, MOE_ROWS, d), F32),
                            pltpu.VMEM((MOE_ROWS, d), BF16),
                            pltpu.SemaphoreType.DMA((3,)), pltpu.SemaphoreType.DMA((2,))]),
        out_shape=jax.ShapeDtypeStruct((n_assign, d), F32),
        compiler_params=_params(("arbitrary",), blocks, resident),
        name="moe_experts",
    )(*items, dest_flat, xn, wg, wu, wd)


def _combine_kernel(dest_ref, x_ref, route_ref, y_hbm, o_ref, ybuf, sem, *, tm):
    half = tm // 2
    i = pl.program_id(0)

    def copy(src_row, slot, k, t):
        return pltpu.make_async_copy(y_hbm.at[pl.ds(src_row, 1)], ybuf.at[slot, k, pl.ds(t, 1)], sem.at[slot])

    def start_half(index, slot):
        for t in range(half):
            for k in range(TOP_K):
                copy(dest_ref[TOP_K * (index * half + t) + k], slot, k, t).start(priority=ROW_DMA_PRIORITY)

    def finish_half(slot):
        for t in range(half):
            for k in range(TOP_K):
                copy(0, slot, k, t).wait()
        rows = slice(slot * half, (slot + 1) * half)
        route = route_ref[rows, :]
        o_ref[rows, :] = x_ref[rows, :] + (route[:, 2:3] * ybuf[slot, 0] + route[:, 3:4] * ybuf[slot, 1])

    @pl.when(i == 0)
    def _():
        start_half(2 * i, 0)

    start_half(2 * i + 1, 1)
    finish_half(0)

    @pl.when(i + 1 < pl.num_programs(0))
    def _():
        start_half(2 * i + 2, 0)

    finish_half(1)


def _combine(dest_flat, x1, route, y, tm):
    t, d = x1.shape
    blocks = 2 * _nbytes((tm, d), F32) + _nbytes((tm, LANE), F32)
    resident = TOP_K * _nbytes((tm, d), F32) + _nbytes((tm, d), F32)
    return pl.pallas_call(
        functools.partial(_combine_kernel, tm=tm),
        grid_spec=pltpu.PrefetchScalarGridSpec(
            num_scalar_prefetch=1,
            grid=(t // tm,),
            in_specs=[pl.BlockSpec((tm, d), lambda i, dest: (i, 0)),
                      pl.BlockSpec((tm, LANE), lambda i, dest: (i, 0)),
                      pl.BlockSpec(memory_space=pl.ANY)],
            out_specs=pl.BlockSpec((tm, d), lambda i, dest: (i, 0)),
            scratch_shapes=[pltpu.VMEM((2, TOP_K, tm // 2, d), F32), pltpu.SemaphoreType.DMA((2,))]),
        out_shape=jax.ShapeDtypeStruct((t, d), F32),
        compiler_params=_params(("arbitrary",), blocks, resident),
        name="moe_combine",
    )(dest_flat, x1, route, y)


def _pad_cols(w, n):
    return jnp.pad(w, ((0, 0), (0, n - w.shape[1])))


def kernel(x, positions, g_mix, w_in, b_gate, q_norm_g, kv_norm_g, w_uq, w_ukv, a_q_norm_g, a_k_norm_g,
           b_q_norm_g, b_k_norm_g, rel_bias, w_o_a, w_o_b, w_out, g_ffn, w_group, w_expert,
           w_exp_gate, w_exp_up, w_exp_down):
    batch, seq, d = x.shape
    t = batch * seq
    q_lora, kv_lora = q_norm_g.shape[0], kv_norm_g.shape[0]
    a_heads = w_uq.shape[1] // A_QK
    b_heads = w_o_b.shape[0] // B_HEAD_DIM
    b_width = b_heads * B_HEAD_DIM
    n_groups, n_experts = w_group.shape[1], w_expert.shape[1]
    per_group = n_experts // n_groups
    off_b = q_lora + kv_lora + A_ROPE
    assert seq % ATTN_TILE == 0 and (TOP_K * t) % MOE_ROWS == 0 and n_groups + n_experts <= LANE
    assert a_heads == b_heads

    xf = x.reshape(t, d)
    tm_big = min(1024, t)
    tm_half = min(512, t)
    tn_wide = _tile(b_width, 1024)
    assert d % tn_wide == 0

    za_cols = -(-(q_lora + kv_lora + LANE) // tn_wide) * tn_wide
    wuq = jnp.pad(w_uq.reshape(q_lora, a_heads, A_QK), ((0, 0), (0, 0), (0, A_HEAD_PAD - A_QK)))
    wuq = wuq.reshape(q_lora, a_heads * A_HEAD_PAD).astype(BF16)
    wukv = w_ukv.astype(BF16)
    pad_gain = lambda g, s: jnp.pad(g * s, (0, A_HEAD_PAD - A_QK)).reshape(1, A_HEAD_PAD)
    gaq = pad_gain(a_q_norm_g, A_QK ** -0.5)
    gak = pad_gain(a_k_norm_g, 1.0)
    g_bqk = jnp.concatenate([jnp.tile(b_q_norm_g * B_HEAD_DIM ** -0.5, b_heads), jnp.tile(b_k_norm_g, b_heads)])

    half = A_ROPE // 2
    inv = ROPE_THETA ** (-jnp.arange(half, dtype=F32) / half)
    ang = positions.astype(F32).reshape(t, 1) * inv
    cos, sin = jnp.cos(ang), jnp.sin(ang)
    zeros = jnp.zeros((t, half), F32)
    cos_t = jnp.concatenate([cos, cos, zeros, zeros], axis=1)
    sin_lo = jnp.concatenate([-sin, zeros, zeros, zeros], axis=1)
    sin_hi = jnp.concatenate([zeros, sin, zeros, zeros], axis=1)

    xn = _rmsnorm_rows(xf, g_mix, BF16, min(256, t))
    w_in_t = w_in.T
    za = _inproj_cols(xn, w_in_t, 0, za_cols, None, _plain_epilogue, tm_big, tn_wide)
    bqk = _inproj_cols(xn, w_in_t, off_b, 2 * b_width, g_bqk, _head_norm_epilogue, tm_big, tn_wide)
    bv = _inproj_cols(xn, w_in_t, off_b + 2 * b_width, b_width, None, _plain_epilogue, tm_big, tn_wide)
    gates = _inproj_cols(xn, w_in_t, off_b + 3 * b_width, 2 * d, b_gate, _sigmoid_epilogue, tm_big, tn_wide)
    q, k, v = _mla_proj(za, cos_t, sin_lo, sin_hi, wuq, wukv, q_norm_g.reshape(1, -1), kv_norm_g.reshape(1, -1),
                        gaq, gak, a_heads, min(256, t))
    o_a, o_b = _attention(q, k, v, bqk, bv, _band_bias_rows(rel_bias), batch, seq, a_heads)
    merged = _merge(o_a, o_b, w_o_a, w_o_b, gates, tm_half, tn_wide)
    x1 = _out_proj(merged, w_out, xf, tm_half, tn_wide)

    wr = _pad_cols(jnp.concatenate([w_group, w_expert], axis=1), LANE)
    xn2, route = _router(x1, g_ffn, wr, n_groups, per_group, min(256, t))
    rank, starts_f = _rank(route, min(512, t))
    dest = _dest(route, rank, starts_f, min(512, t))[:, :TOP_K].reshape(-1)
    items = _work_items(starts_f[0, :n_experts].astype(jnp.int32), TOP_K * t)
    y = _moe_experts(items, dest, xn2, w_exp_gate, w_exp_up, w_exp_down)
    out = _combine(dest, x1, route, y, min(256, t))
    return out.reshape(batch, seq, d)
```

```python
import functools

import jax
import jax.numpy as jnp
from jax import lax
from jax.experimental import pallas as pl
from jax.experimental.pallas import tpu as pltpu

F32 = jnp.float32
BF16 = jnp.bfloat16

CHUNK = 64
EPS = 1e-6
A_NOPE = 128
A_ROPE = 64
A_V = 128
A_QK = A_NOPE + A_ROPE
B_HEAD_DIM = 128
B_LEFT_CHUNKS = 8
B_MAX_REL = 128
ROPE_THETA = 10000.0
TOP_K = 2

LANE = 128
A_HEAD_PAD = 2 * LANE
V_PAD = 2 * LANE
V7X_VMEM_BYTES = 64 * 2**20

ATTN_TILE = 256
MOE_ROWS = 128
WEIGHT_DMA_PRIORITY = 1
ROW_DMA_PRIORITY = 1
PROJ_GROUP_ROWS = 256
NEG_INF = float("-inf")


def _nbytes(shape, dtype):
    n = 1
    for s in shape:
        n *= s
    return n * jnp.dtype(dtype).itemsize


def _params(semantics, pipelined_bytes, resident_bytes=0):
    need = 2 * pipelined_bytes + resident_bytes
    return pltpu.CompilerParams(dimension_semantics=semantics,
                                vmem_limit_bytes=min(int(need), V7X_VMEM_BYTES))


def _tile(n, want):
    t = want
    while t > LANE and n % t:
        t //= 2
    assert n % t == 0, (n, want)
    return t


def _rmsnorm_kernel(x_ref, g_ref, o_ref):
    x = x_ref[...]
    r = lax.rsqrt(jnp.mean(x * x, axis=-1, keepdims=True) + EPS)
    o_ref[...] = (x * r * g_ref[...]).astype(o_ref.dtype)


def _rmsnorm_rows(x, g, out_dtype, tm):
    t, d = x.shape
    blocks = _nbytes((tm, d), F32) + _nbytes((tm, d), out_dtype) + _nbytes((1, d), F32)
    return pl.pallas_call(
        _rmsnorm_kernel,
        grid=(t // tm,),
        in_specs=[pl.BlockSpec((tm, d), lambda i: (i, 0)), pl.BlockSpec((1, d), lambda i: (0, 0))],
        out_specs=pl.BlockSpec((tm, d), lambda i: (i, 0)),
        out_shape=jax.ShapeDtypeStruct((t, d), out_dtype),
        compiler_params=_params(("parallel",), blocks, _nbytes((tm, d), F32)),
        name="rmsnorm",
    )(x, g.reshape(1, d))


def _nt_dot(a, b):
    return lax.dot_general(a, b, (((1,), (1,)), ((), ())), preferred_element_type=F32)


def _stage_weight_blocks(streams):
    j = pl.program_id(0)

    @pl.when(pl.program_id(1) == 0)
    def _():
        @pl.when(j == 0)
        def _():
            for fetch, _, _ in streams:
                fetch(j).start(priority=WEIGHT_DMA_PRIORITY)

        for fetch, stage, w_bf in streams:
            fetch(j).wait()
            w_bf[...] = stage[...].astype(BF16)

            @pl.when(j + 1 < pl.num_programs(0))
            def _():
                fetch(j + 1).start(priority=WEIGHT_DMA_PRIORITY)


def _row_groups(rows):
    step = min(rows, PROJ_GROUP_ROWS)
    assert rows % step == 0
    return [slice(s, s + step) for s in range(0, rows, step)]


def _head_norm_epilogue(acc, gb_ref):
    heads = []
    for h in range(acc.shape[1] // B_HEAD_DIM):
        sl = slice(h * B_HEAD_DIM, (h + 1) * B_HEAD_DIM)
        z = acc[:, sl]
        r = lax.rsqrt(jnp.mean(z * z, axis=-1, keepdims=True) + EPS)
        heads.append(z * r * gb_ref[:, sl])
    return jnp.concatenate(heads, axis=1)


def _plain_epilogue(acc, gb_ref):
    del gb_ref
    return acc


def _sigmoid_epilogue(acc, gb_ref):
    return 1.0 / (1.0 + jnp.exp(-(acc + gb_ref[...])))


def _inproj_cols_kernel(a_ref, wt_hbm, *refs, first_col, epilogue):
    gb_ref = refs[0] if len(refs) == 5 else None
    o_ref, stage, wt_bf, sem = refs[-4:]
    tn = wt_bf.shape[0]

    def fetch(block):
        rows = pl.ds(pl.multiple_of(first_col + block * tn, 8), tn)
        return pltpu.make_async_copy(wt_hbm.at[rows], stage, sem)

    _stage_weight_blocks([(fetch, stage, wt_bf)])
    groups = _row_groups(a_ref.shape[0])
    accs = [_nt_dot(a_ref[sl, :], wt_bf[...]) for sl in groups]
    for sl, acc in zip(groups, accs):
        o_ref[sl, :] = epilogue(acc, gb_ref).astype(o_ref.dtype)


def _inproj_cols(xn, w_in_t, first_col, n_cols, gb, epilogue, tm, tn):
    m, k = xn.shape
    assert n_cols % tn == 0 and first_col % 8 == 0
    blocks = _nbytes((tm, k), BF16) + _nbytes((tm, tn), BF16) + _nbytes((1, tn), F32)
    resident = _nbytes((tn, k), F32) + _nbytes((tn, k), BF16) + 3 * _nbytes((tm, tn), F32)
    gb_specs = [] if gb is None else [pl.BlockSpec((1, tn), lambda j, i: (0, j))]
    gb_args = [] if gb is None else [gb.reshape(1, n_cols)]
    return pl.pallas_call(
        functools.partial(_inproj_cols_kernel, first_col=first_col, epilogue=epilogue),
        grid=(n_cols // tn, m // tm),
        in_specs=[pl.BlockSpec((tm, k), lambda j, i: (i, 0)), pl.BlockSpec(memory_space=pl.ANY)] + gb_specs,
        out_specs=pl.BlockSpec((tm, tn), lambda j, i: (i, j)),
        out_shape=jax.ShapeDtypeStruct((m, n_cols), BF16),
        scratch_shapes=[pltpu.VMEM((tn, k), F32), pltpu.VMEM((tn, k), BF16), pltpu.SemaphoreType.DMA(())],
        compiler_params=_params(("arbitrary", "arbitrary"), blocks, resident),
        name="inproj_" + epilogue.__name__.strip("_").replace("_epilogue", ""),
    )(xn, w_in_t, *gb_args)


def _rope_padded(v, cos, sin_lo, sin_hi):
    half = A_ROPE // 2
    return v * cos + pltpu.roll(v, LANE - half, 1) * sin_lo + pltpu.roll(v, half, 1) * sin_hi


def _ones_column(rows):
    lane = lax.broadcasted_iota(jnp.int32, (rows, V_PAD - A_V), 1)
    return jnp.where(lane == 0, 1.0, 0.0).astype(BF16)


def _mla_proj_kernel(za_ref, cos_ref, sl_ref, sh_ref, wuq_ref, wukv_ref, gq_ref, gkv_ref, gaq_ref, gak_ref,
                     q_ref, k_ref, v_ref, *, heads, q_lora, kv_lora):
    cos, sin_lo, sin_hi = cos_ref[...], sl_ref[...], sh_ref[...]

    def norm(z, g):
        z = z.astype(F32)
        r = lax.rsqrt(jnp.mean(z * z, axis=-1, keepdims=True) + EPS)
        return (z * r * g).astype(BF16)

    cq = norm(za_ref[:, :q_lora], gq_ref[...])
    ckv = norm(za_ref[:, q_lora:q_lora + kv_lora], gkv_ref[...])
    slab = za_ref[:, q_lora + kv_lora:q_lora + kv_lora + LANE].astype(F32)
    k_rope = jnp.where(lax.broadcasted_iota(jnp.int32, slab.shape, 1) < A_ROPE, slab, 0.0)
    qacc = jnp.dot(cq, wuq_ref[...], preferred_element_type=F32)
    kvacc = jnp.dot(ckv, wukv_ref[...], preferred_element_type=F32)

    gq_lo, gq_hi = gaq_ref[:, :LANE], gaq_ref[:, LANE:]
    gk_lo, gk_hi = gak_ref[:, :LANE], gak_ref[:, LANE:]
    kr_ss = jnp.sum(k_rope * k_rope, axis=-1, keepdims=True)
    kr_rot = _rope_padded(k_rope * gk_hi, cos, sin_lo, sin_hi)
    ones_col = _ones_column(slab.shape[0])
    for h in range(heads):
        base = h * A_HEAD_PAD
        q_lo = qacc[:, base:base + LANE]
        q_hi = qacc[:, base + LANE:base + A_HEAD_PAD]
        ss = jnp.sum(q_lo * q_lo, axis=-1, keepdims=True) + jnp.sum(q_hi * q_hi, axis=-1, keepdims=True)
        r = lax.rsqrt(ss / A_QK + EPS)
        q_ref[:, base:base + LANE] = (q_lo * r * gq_lo).astype(BF16)
        q_ref[:, base + LANE:base + A_HEAD_PAD] = _rope_padded(q_hi * r * gq_hi, cos, sin_lo, sin_hi).astype(BF16)

        k_lo = kvacc[:, base:base + LANE]
        ssk = jnp.sum(k_lo * k_lo, axis=-1, keepdims=True) + kr_ss
        rk = lax.rsqrt(ssk / A_QK + EPS)
        k_ref[:, base:base + LANE] = (k_lo * rk * gk_lo).astype(BF16)
        k_ref[:, base + LANE:base + A_HEAD_PAD] = (kr_rot * rk).astype(BF16)
        v_ref[:, h * V_PAD:h * V_PAD + A_V] = kvacc[:, base + LANE:base + A_HEAD_PAD].astype(BF16)
        v_ref[:, h * V_PAD + A_V:(h + 1) * V_PAD] = ones_col


def _mla_proj(za, cos, sin_lo, sin_hi, wuq, wukv, gq, gkv, gaq, gak, heads, tm):
    t, za_cols = za.shape
    q_lora, kv_lora = wuq.shape[0], wukv.shape[0]
    hp = heads * A_HEAD_PAD
    row = lambda i: (i, 0)
    fix = lambda i: (0, 0)
    blocks = (_nbytes((tm, za_cols), za.dtype) + 3 * _nbytes((tm, LANE), F32) + _nbytes(wuq.shape, BF16)
              + _nbytes(wukv.shape, BF16) + 2 * _nbytes((tm, hp), BF16) + _nbytes((tm, heads * V_PAD), BF16))
    return pl.pallas_call(
        functools.partial(_mla_proj_kernel, heads=heads, q_lora=q_lora, kv_lora=kv_lora),
        grid=(t // tm,),
        in_specs=[pl.BlockSpec((tm, za_cols), row),
                  pl.BlockSpec((tm, LANE), row), pl.BlockSpec((tm, LANE), row), pl.BlockSpec((tm, LANE), row),
                  pl.BlockSpec(wuq.shape, fix), pl.BlockSpec(wukv.shape, fix),
                  pl.BlockSpec((1, q_lora), fix), pl.BlockSpec((1, kv_lora), fix),
                  pl.BlockSpec((1, A_HEAD_PAD), fix), pl.BlockSpec((1, A_HEAD_PAD), fix)],
        out_specs=[pl.BlockSpec((tm, hp), row), pl.BlockSpec((tm, hp), row), pl.BlockSpec((tm, heads * V_PAD), row)],
        out_shape=[jax.ShapeDtypeStruct((t, hp), BF16), jax.ShapeDtypeStruct((t, hp), BF16),
                   jax.ShapeDtypeStruct((t, heads * V_PAD), BF16)],
        compiler_params=_params(("parallel",), blocks, 3 * _nbytes((tm, hp), F32)),
        name="mla_proj",
    )(za, cos, sin_lo, sin_hi, wuq, wukv, gq, gkv, gaq, gak)


def _mla_scores(i, q_ref, k_ref, diag_ok):
    tq = ATTN_TILE
    q = q_ref[i * tq:(i + 1) * tq, :]
    sd = jnp.where(diag_ok, _nt_dot(q, k_ref[i * tq:(i + 1) * tq, :]), NEG_INF)
    s0 = _nt_dot(q, k_ref[:i * tq, :]) if i > 0 else None
    return sd, s0


def _mla_softmax(sd, s0):
    m = jnp.max(sd, axis=-1, keepdims=True)
    if s0 is None:
        return jnp.exp(sd - m).astype(BF16), None
    m = jnp.maximum(m, jnp.max(s0, axis=-1, keepdims=True))
    return jnp.exp(sd - m).astype(BF16), jnp.exp(s0 - m).astype(BF16)


def _mla_values(i, pd, p0, v_ref, o_ref):
    tq = ATTN_TILE
    o = jnp.dot(pd, v_ref[i * tq:(i + 1) * tq, :], preferred_element_type=F32)
    if p0 is not None:
        o = o + jnp.dot(p0, v_ref[:i * tq, :], preferred_element_type=F32)
    o_ref[i * tq:(i + 1) * tq, :] = (o[:, :A_V] / o[:, A_V:A_V + 1]).astype(o_ref.dtype)


def _band_window(i):
    left = B_LEFT_CHUNKS * CHUNK
    q0 = i * ATTN_TILE
    k0 = max(0, q0 - left)
    return q0, k0, q0 + ATTN_TILE - k0, left - q0 + k0


def _band_scores(i, q_ref, k_ref, table):
    q0, k0, kw, u0 = _band_window(i)
    return _nt_dot(q_ref[q0:q0 + ATTN_TILE, :], k_ref[k0:k0 + kw, :]) + table[:, u0:u0 + kw]


def _band_softmax(s):
    return jnp.exp(s - jnp.max(s, axis=-1, keepdims=True)).astype(BF16)


def _band_values(i, p, v_ref, o_ref):
    q0, k0, kw, _ = _band_window(i)
    o = jnp.dot(p, v_ref[k0:k0 + kw, :], preferred_element_type=F32)
    o_ref[q0:q0 + ATTN_TILE, :] = (o[:, :B_HEAD_DIM] / o[:, B_HEAD_DIM:B_HEAD_DIM + 1]).astype(o_ref.dtype)


def _attn_kernel(aq_ref, ak_ref, av_ref, bq_ref, bk_ref, bv_ref, r_ref, oa_ref, ob_ref, bv_pad, *, seq):
    tq = ATTN_TILE
    left = B_LEFT_CHUNKS * CHUNK
    width = r_ref.shape[1]
    rc = lax.broadcasted_iota(jnp.int32, (tq, tq), 0) // CHUNK
    cc = lax.broadcasted_iota(jnp.int32, (tq, tq), 1) // CHUNK
    diag_ok = cc <= rc
    bias = pltpu.roll(jnp.broadcast_to(r_ref[...], (tq, width)), 0, 1, stride=1, stride_axis=0)[:, :left + tq]
    q_chunk = lax.broadcasted_iota(jnp.int32, (tq, left + tq), 0) // CHUNK
    k_chunk = lax.broadcasted_iota(jnp.int32, (tq, left + tq), 1) // CHUNK
    table = jnp.where((k_chunk >= q_chunk) & (k_chunk <= q_chunk + B_LEFT_CHUNKS), bias, NEG_INF)
    bv_pad[:, :B_HEAD_DIM] = bv_ref[...]
    bv_pad[:, B_HEAD_DIM:] = _ones_column(seq)

    n_tiles = seq // tq
    order = [n_tiles - 1 - i // 2 if i % 2 == 0 else i // 2 for i in range(n_tiles)]
    sd, s0 = _mla_scores(order[0], aq_ref, ak_ref, diag_ok)
    for n, i in enumerate(order):
        sb = _band_scores(i, bq_ref, bk_ref, table)
        pd, p0 = _mla_softmax(sd, s0)
        if n + 1 < n_tiles:
            sd, s0 = _mla_scores(order[n + 1], aq_ref, ak_ref, diag_ok)
        _mla_values(i, pd, p0, av_ref, oa_ref)
        _band_values(i, _band_softmax(sb), bv_pad, ob_ref)


def _band_bias_rows(rel_bias):
    left = B_LEFT_CHUNKS * CHUNK
    width = left + 2 * ATTN_TILE
    m = jnp.arange(width, dtype=jnp.int32)
    j = jnp.where(m < left + ATTN_TILE, m, m - width)
    dist = left - j
    rows = rel_bias[:, jnp.clip(dist, -B_MAX_REL, B_MAX_REL) + B_MAX_REL].astype(F32)
    return rows.reshape(rel_bias.shape[0], 1, width)


def _attention(q, k, v, bqk, bv, bias_rows, batch, seq, heads):
    t = batch * seq
    d = B_HEAD_DIM
    width = bias_rows.shape[2]
    head = lambda b, h: (b, h)
    blocks = (2 * _nbytes((seq, A_HEAD_PAD), BF16) + _nbytes((seq, V_PAD), BF16) + _nbytes((seq, A_V), BF16)
              + 4 * _nbytes((seq, d), BF16) + _nbytes((1, width), F32))
    temps = _nbytes((seq, V_PAD), BF16) + 8 * _nbytes((ATTN_TILE, seq), F32) + 8 * _nbytes((ATTN_TILE, width), F32)
    return pl.pallas_call(
        functools.partial(_attn_kernel, seq=seq),
        grid=(batch, heads),
        in_specs=[pl.BlockSpec((seq, A_HEAD_PAD), head), pl.BlockSpec((seq, A_HEAD_PAD), head),
                  pl.BlockSpec((seq, V_PAD), head),
                  pl.BlockSpec((seq, d), head),
                  pl.BlockSpec((seq, d), lambda b, h: (b, heads + h)),
                  pl.BlockSpec((seq, d), head),
                  pl.BlockSpec((None, 1, width), lambda b, h: (h, 0, 0))],
        out_specs=[pl.BlockSpec((seq, A_V), head), pl.BlockSpec((seq, d), head)],
        out_shape=[jax.ShapeDtypeStruct((t, heads * A_V), BF16), jax.ShapeDtypeStruct((t, heads * d), BF16)],
        scratch_shapes=[pltpu.VMEM((seq, V_PAD), BF16)],
        compiler_params=_params(("parallel", "parallel"), blocks, temps),
        name="attention",
    )(q, k, v, bqk, bqk, bv, bias_rows)


def _column_block_fetch(w_hbm, stage, sem):
    tn = stage.shape[1]
    return lambda block: pltpu.make_async_copy(w_hbm.at[:, pl.ds(pl.multiple_of(block * tn, LANE), tn)], stage, sem)


def _merge_kernel(oa_ref, ob_ref, woa_hbm, wob_hbm, g0_ref, g1_ref, o_ref, a_stage, b_stage, woa_bf, wob_bf, sem):
    _stage_weight_blocks([(_column_block_fetch(woa_hbm, a_stage, sem.at[0]), a_stage, woa_bf),
                          (_column_block_fetch(wob_hbm, b_stage, sem.at[1]), b_stage, wob_bf)])
    for sl in _row_groups(oa_ref.shape[0]):
        a = jnp.dot(oa_ref[sl, :], woa_bf[...], preferred_element_type=F32)
        b = jnp.dot(ob_ref[sl, :], wob_bf[...], preferred_element_type=F32)
        o_ref[sl, :] = (g0_ref[sl, :].astype(F32) * a + g1_ref[sl, :].astype(F32) * b).astype(o_ref.dtype)


def _merge(oa, ob, woa, wob, gates, tm, tn):
    m = oa.shape[0]
    ka, kb = woa.shape[0], wob.shape[0]
    d = woa.shape[1]
    assert d % tn == 0
    g1 = d // tn
    blocks = _nbytes((tm, ka), BF16) + _nbytes((tm, kb), BF16) + 3 * _nbytes((tm, tn), BF16)
    resident = (_nbytes((ka + kb, tn), F32) + 2 * _nbytes((ka + kb, tn), BF16) + 3 * _nbytes((tm, tn), F32))
    return pl.pallas_call(
        _merge_kernel,
        grid=(d // tn, m // tm),
        in_specs=[pl.BlockSpec((tm, ka), lambda j, i: (i, 0)),
                  pl.BlockSpec((tm, kb), lambda j, i: (i, 0)),
                  pl.BlockSpec(memory_space=pl.ANY), pl.BlockSpec(memory_space=pl.ANY),
                  pl.BlockSpec((tm, tn), lambda j, i: (i, j)),
                  pl.BlockSpec((tm, tn), lambda j, i: (i, g1 + j))],
        out_specs=pl.BlockSpec((tm, tn), lambda j, i: (i, j)),
        out_shape=jax.ShapeDtypeStruct((m, d), BF16),
        scratch_shapes=[pltpu.VMEM((ka, tn), F32), pltpu.VMEM((kb, tn), F32),
                        pltpu.VMEM((ka, tn), BF16), pltpu.VMEM((kb, tn), BF16), pltpu.SemaphoreType.DMA((2,))],
        compiler_params=_params(("arbitrary", "arbitrary"), blocks, resident),
        name="merge",
    )(oa, ob, woa, wob, gates, gates)


def _out_proj_kernel(a_ref, w_hbm, x_ref, o_ref, stage, w_bf, sem):
    _stage_weight_blocks([(_column_block_fetch(w_hbm, stage, sem), stage, w_bf)])
    groups = _row_groups(a_ref.shape[0])
    accs = [jnp.dot(a_ref[sl, :], w_bf[...], preferred_element_type=F32) for sl in groups]
    for sl, acc in zip(groups, accs):
        o_ref[sl, :] = x_ref[sl, :] + acc


def _out_proj(merged, wout, x, tm, tn):
    m, k = merged.shape
    n = wout.shape[1]
    assert n % tn == 0
    blocks = _nbytes((tm, k), BF16) + 2 * _nbytes((tm, tn), F32)
    resident = _nbytes((k, tn), F32) + 2 * _nbytes((k, tn), BF16) + 2 * _nbytes((tm, tn), F32)
    return pl.pallas_call(
        _out_proj_kernel,
        grid=(n // tn, m // tm),
        in_specs=[pl.BlockSpec((tm, k), lambda j, i: (i, 0)),
                  pl.BlockSpec(memory_space=pl.ANY),
                  pl.BlockSpec((tm, tn), lambda j, i: (i, j))],
        out_specs=pl.BlockSpec((tm, tn), lambda j, i: (i, j)),
        out_shape=jax.ShapeDtypeStruct((m, n), F32),
        scratch_shapes=[pltpu.VMEM((k, tn), F32), pltpu.VMEM((k, tn), BF16), pltpu.SemaphoreType.DMA(())],
        compiler_params=_params(("arbitrary", "arbitrary"), blocks, resident),
        name="out_proj",
    )(merged, wout, x)


def _split_bf16(v):
    hi = v.astype(BF16)
    return hi, (v - hi.astype(F32)).astype(BF16)


def _router_kernel(x_ref, g_ref, wr_ref, xn_ref, route_ref, *, n_groups, per_group):
    x = x_ref[...]
    r = lax.rsqrt(jnp.mean(x * x, axis=-1, keepdims=True) + EPS)
    xn = x * r * g_ref[...]
    xn_ref[...] = xn
    x_hi, x_lo = _split_bf16(xn)
    w_hi, w_lo = _split_bf16(wr_ref[...])
    dot = functools.partial(jnp.dot, preferred_element_type=F32)
    logits = dot(x_hi, w_hi) + (dot(x_lo, w_hi) + dot(x_hi, w_lo))
    lane = lax.broadcasted_iota(jnp.int32, logits.shape, 1).astype(F32)
    far = float(LANE)

    def top(vals):
        best = jnp.max(vals, axis=-1, keepdims=True)
        return best, jnp.min(jnp.where(vals == best, lane, far), axis=-1, keepdims=True)

    gl = jnp.where(lane < n_groups, logits, NEG_INF)
    gmax, grp = top(gl)
    p_grp = 1.0 / jnp.sum(jnp.exp(gl - gmax), axis=-1, keepdims=True)
    lo = n_groups + grp * per_group
    el = jnp.where((lane >= lo) & (lane < lo + per_group), logits, NEG_INF)
    t1, i1 = top(el)
    t2, i2 = top(jnp.where(lane == i1, NEG_INF, el))
    d = jnp.exp(t2 - t1)
    w1 = p_grp / (1.0 + d)
    w2 = p_grp * d / (1.0 + d)
    route_ref[...] = jnp.where(lane == 0, i1 - n_groups,
                               jnp.where(lane == 1, i2 - n_groups,
                                         jnp.where(lane == 2, w1, jnp.where(lane == 3, w2, 0.0))))


def _router(x1, g, wr, n_groups, per_group, tm):
    t, d = x1.shape
    blocks = 2 * _nbytes((tm, d), F32) + _nbytes((1, d), F32) + _nbytes((d, LANE), F32) + _nbytes((tm, LANE), F32)
    return pl.pallas_call(
        functools.partial(_router_kernel, n_groups=n_groups, per_group=per_group),
        grid=(t // tm,),
        in_specs=[pl.BlockSpec((tm, d), lambda i: (i, 0)), pl.BlockSpec((1, d), lambda i: (0, 0)),
                  pl.BlockSpec((d, LANE), lambda i: (0, 0))],
        out_specs=[pl.BlockSpec((tm, d), lambda i: (i, 0)), pl.BlockSpec((tm, LANE), lambda i: (i, 0))],
        out_shape=[jax.ShapeDtypeStruct((t, d), F32), jax.ShapeDtypeStruct((t, LANE), F32)],
        compiler_params=_params(("parallel",), blocks, 2 * _nbytes((tm, d), F32)),
        name="router",
    )(x1, g.reshape(1, d), wr)


def _one_hots(route):
    lane = lax.broadcasted_iota(jnp.int32, route.shape, 1).astype(F32)
    return (lane == route[:, 0:1]).astype(F32), (lane == route[:, 1:2]).astype(F32)


def _sort_kernel(route_ref, dest_ref, starts_ref, count_acc, start_acc):
    phase, i = pl.program_id(0), pl.program_id(1)
    oh1, oh2 = _one_hots(route_ref[...])
    oh = (oh1 + oh2).astype(BF16)
    tm = oh.shape[0]

    @pl.when((phase == 0) & (i == 0))
    def _():
        start_acc[...] = jnp.zeros_like(start_acc)

    @pl.when(phase == 0)
    def _():
        lower = (lax.broadcasted_iota(jnp.int32, (LANE, LANE), 0) < lax.broadcasted_iota(jnp.int32, (LANE, LANE), 1))
        below = jnp.dot(oh, lower.astype(BF16), preferred_element_type=F32)
        start_acc[...] += jnp.sum(below, axis=0, keepdims=True)

    @pl.when(phase == 1)
    def _():
        @pl.when(i == 0)
        def _():
            count_acc[...] = jnp.zeros_like(count_acc)

        earlier = (lax.broadcasted_iota(jnp.int32, (tm, tm), 0) > lax.broadcasted_iota(jnp.int32, (tm, tm), 1))
        row = jnp.dot(earlier.astype(BF16), oh, preferred_element_type=F32) + (count_acc[...] + start_acc[...])
        d1 = jnp.sum(oh1 * row, axis=-1, keepdims=True)
        d2 = jnp.sum(oh2 * row, axis=-1, keepdims=True)
        lane = lax.broadcasted_iota(jnp.int32, (tm, LANE), 1)
        dest_ref[...] = jnp.where(lane == 0, d1, jnp.where(lane == 1, d2, 0.0)).astype(jnp.int32)
        count_acc[...] += jnp.sum(oh.astype(F32), axis=0, keepdims=True)

    starts_ref[...] = start_acc[...]


def _sort_assignments(route, tm):
    t = route.shape[0]
    blocks = 2 * _nbytes((tm, LANE), F32) + _nbytes((1, LANE), F32)
    return pl.pallas_call(
        _sort_kernel,
        grid=(2, t // tm),
        in_specs=[pl.BlockSpec((tm, LANE), lambda p, i: (i, 0))],
        out_specs=[pl.BlockSpec((tm, LANE), lambda p, i: (i * p, 0)), pl.BlockSpec((1, LANE), lambda p, i: (0, 0))],
        out_shape=[jax.ShapeDtypeStruct((t, LANE), jnp.int32), jax.ShapeDtypeStruct((1, LANE), F32)],
        scratch_shapes=[pltpu.VMEM((1, LANE), F32), pltpu.VMEM((1, LANE), F32)],
        compiler_params=_params(("arbitrary", "arbitrary"), blocks, 2 * _nbytes((tm, tm), F32)),
        name="moe_sort",
    )(route)


def _work_items(starts, n_rows):
    n_exp = starts.shape[0]
    n_blk = n_rows // MOE_ROWS
    total = jnp.full((1,), n_rows, jnp.int32)
    pts = jnp.concatenate([jnp.arange(n_blk, dtype=jnp.int32) * MOE_ROWS, starts[1:]])
    idx = jnp.arange(pts.shape[0], dtype=jnp.int32)
    before = (pts[None, :] < pts[:, None]) | ((pts[None, :] == pts[:, None]) & (idx[None, :] < idx[:, None]))
    pos = jnp.sum(before.astype(jnp.int32), axis=1)
    lo = jnp.sum(jnp.where(pos[:, None] == idx[None, :], pts[:, None], 0), axis=0)
    hi = jnp.concatenate([lo[1:], total])
    ends = jnp.concatenate([starts[1:], total])
    r = jnp.minimum(lo // MOE_ROWS, n_blk - 1)
    e = jnp.minimum(jnp.sum((ends[None, :] <= lo[:, None]).astype(jnp.int32), axis=1), n_exp - 1)
    later = jnp.where(e[None, :] > e[:, None], e[None, :], n_exp)
    nxt = jnp.min(later, axis=1)
    nxt = jnp.where(nxt == n_exp, -1, nxt)
    return r, e, lo, hi, nxt


def _stream_expert_weights(w, e_ref, nxt_ref, streams):
    e = e_ref[w]

    def copy(stream, expert):
        hbm, stage, _, sem = stream
        return pltpu.make_async_copy(hbm.at[expert], stage, sem)

    @pl.when(w == 0)
    def _():
        for stream in streams:
            copy(stream, e).start(priority=WEIGHT_DMA_PRIORITY)

    @pl.when((w == 0) | (e != e_ref[jnp.maximum(w - 1, 0)]))
    def _():
        nxt = nxt_ref[w]
        for stream in streams:
            copy(stream, e).wait()
            _, stage, w_bf, _ = stream
            w_bf[...] = stage[...].astype(BF16)

            @pl.when(nxt >= 0)
            def _():
                copy(stream, nxt).start(priority=WEIGHT_DMA_PRIORITY)


def _store_item_rows(o_ref, val, r, lo, hi):
    rows = r * MOE_ROWS + lax.broadcasted_iota(jnp.int32, val.shape, 0)
    mine = (rows >= lo) & (rows < hi)

    @pl.when(lo == r * MOE_ROWS)
    def _():
        o_ref[...] = val

    @pl.when(lo != r * MOE_ROWS)
    def _():
        pltpu.store(o_ref, val, mask=mine)


def _gather_block_rows(w, r_ref, dest_ref, x_hbm, source, rows_f32, x_bf, sem, *, n_assign):
    r = r_ref[w]
    n_blocks = n_assign // MOE_ROWS

    def copy(src_row, slot, j):
        return pltpu.make_async_copy(x_hbm.at[pl.ds(src_row, 1)], rows_f32.at[slot, pl.ds(j, 1)], sem.at[slot])

    def start_block(block, slot):
        for j in range(MOE_ROWS):
            copy(source[block * MOE_ROWS + j], slot, j).start()

    def finish_block(slot):
        for j in range(MOE_ROWS):
            copy(0, slot, j).wait()
        x_bf[...] = rows_f32[slot].astype(BF16)

    @pl.when(w == 0)
    def _():
        def invert(tok, carry):
            for k in range(TOP_K):
                source[dest_ref[TOP_K * tok + k]] = tok
            return carry
        lax.fori_loop(0, n_assign // TOP_K, invert, 0, unroll=4)
        start_block(r, 0)

    @pl.when((w == 0) | (r != r_ref[jnp.maximum(w - 1, 0)]))
    def _():
        for slot in range(2):
            @pl.when(r % 2 == slot)
            def _():
                finish_block(slot)

                @pl.when(r + 1 < n_blocks)
                def _():
                    start_block(r + 1, 1 - slot)


def _moe_experts_kernel(r_ref, e_ref, lo_ref, hi_ref, nxt_ref, dest_ref, x_hbm, wg_hbm, wu_hbm, wd_hbm, y_ref,
                        wg_stage, wu_stage, wd_stage, wg_bf, wu_bf, wd_bf, source, rows_f32, x_bf, sem, row_sem,
                        *, n_assign):
    w = pl.program_id(0)
    _stream_expert_weights(w, e_ref, nxt_ref, [(wg_hbm, wg_stage, wg_bf, sem.at[0]),
                                                 (wu_hbm, wu_stage, wu_bf, sem.at[1]),
                                                 (wd_hbm, wd_stage, wd_bf, sem.at[2])])
    _gather_block_rows(w, r_ref, dest_ref, x_hbm, source, rows_f32, x_bf, row_sem, n_assign=n_assign)
    r, lo, hi = r_ref[w], lo_ref[w], hi_ref[w]

    @pl.when(hi > lo)
    def _():
        x = x_bf[...]
        g = jnp.dot(x, wg_bf[...], preferred_element_type=F32)
        u = jnp.dot(x, wu_bf[...], preferred_element_type=F32)
        h = ((g * (1.0 / (1.0 + jnp.exp(-g)))) * u).astype(BF16)
        y = jnp.dot(h, wd_bf[...], preferred_element_type=F32)
        _store_item_rows(y_ref, y, r, lo, hi)


def _moe_experts(items, dest_flat, xn, wg, wu, wd):
    n_assign = dest_flat.shape[0]
    d = xn.shape[1]
    f = wg.shape[2]
    n_items = items[0].shape[0]
    blocks = _nbytes((MOE_ROWS, d), F32)
    resident = (3 * _nbytes((d, f), F32) + 3 * _nbytes((d, f), BF16) + 5 * _nbytes((MOE_ROWS, d), F32)
                + _nbytes((MOE_ROWS, d), BF16) + 6 * _nbytes((MOE_ROWS, f), F32))
    any_space = pl.BlockSpec(memory_space=pl.ANY)
    return pl.pallas_call(
        functools.partial(_moe_experts_kernel, n_assign=n_assign),
        grid_spec=pltpu.PrefetchScalarGridSpec(
            num_scalar_prefetch=6,
            grid=(n_items,),
            in_specs=[any_space, any_space, any_space, any_space],
            out_specs=pl.BlockSpec((MOE_ROWS, d), lambda w, r, e, lo, hi, nxt, dest: (r[w], 0)),
            scratch_shapes=[pltpu.VMEM((d, f), F32), pltpu.VMEM((d, f), F32), pltpu.VMEM((f, d), F32),
                            pltpu.VMEM((d, f), BF16), pltpu.VMEM((d, f), BF16), pltpu.VMEM((f, d), BF16),
                            pltpu.SMEM((n_assign,), jnp.int32), pltpu.VMEM((2, MOE_ROWS, d), F32),
                            pltpu.VMEM((MOE_ROWS, d), BF16),
                            pltpu.SemaphoreType.DMA((3,)), pltpu.SemaphoreType.DMA((2,))]),
        out_shape=jax.ShapeDtypeStruct((n_assign, d), F32),
        compiler_params=_params(("arbitrary",), blocks, resident),
        name="moe_experts",
    )(*items, dest_flat, xn, wg, wu, wd)


def _combine_kernel(dest_ref, x_ref, route_ref, y_hbm, o_ref, ybuf, sem, *, tm):
    half = tm // 2
    i = pl.program_id(0)

    def copy(src_row, slot, k, t):
        return pltpu.make_async_copy(y_hbm.at[pl.ds(src_row, 1)], ybuf.at[slot, k, pl.ds(t, 1)], sem.at[slot])

    def start_half(index, slot):
        for t in range(half):
            for k in range(TOP_K):
                copy(dest_ref[TOP_K * (index * half + t) + k], slot, k, t).start(priority=ROW_DMA_PRIORITY)

    def finish_half(slot):
        for t in range(half):
            for k in range(TOP_K):
                copy(0, slot, k, t).wait()
        rows = slice(slot * half, (slot + 1) * half)
        route = route_ref[rows, :]
        o_ref[rows, :] = x_ref[rows, :] + (route[:, 2:3] * ybuf[slot, 0] + route[:, 3:4] * ybuf[slot, 1])

    @pl.when(i == 0)
    def _():
        start_half(2 * i, 0)

    start_half(2 * i + 1, 1)
    finish_half(0)

    @pl.when(i + 1 < pl.num_programs(0))
    def _():
        start_half(2 * i + 2, 0)

    finish_half(1)


def _combine(dest_flat, x1, route, y, tm):
    t, d = x1.shape
    blocks = 2 * _nbytes((tm, d), F32) + _nbytes((tm, LANE), F32)
    resident = TOP_K * _nbytes((tm, d), F32) + _nbytes((tm, d), F32)
    return pl.pallas_call(
        functools.partial(_combine_kernel, tm=tm),
        grid_spec=pltpu.PrefetchScalarGridSpec(
            num_scalar_prefetch=1,
            grid=(t // tm,),
            in_specs=[pl.BlockSpec((tm, d), lambda i, dest: (i, 0)),
                      pl.BlockSpec((tm, LANE), lambda i, dest: (i, 0)),
                      pl.BlockSpec(memory_space=pl.ANY)],
            out_specs=pl.BlockSpec((tm, d), lambda i, dest: (i, 0)),
            scratch_shapes=[pltpu.VMEM((2, TOP_K, tm // 2, d), F32), pltpu.SemaphoreType.DMA((2,))]),
        out_shape=jax.ShapeDtypeStruct((t, d), F32),
        compiler_params=_params(("arbitrary",), blocks, resident),
        name="moe_combine",
    )(dest_flat, x1, route, y)


def _pad_cols(w, n):
    return jnp.pad(w, ((0, 0), (0, n - w.shape[1])))


def kernel(x, positions, g_mix, w_in, b_gate, q_norm_g, kv_norm_g, w_uq, w_ukv, a_q_norm_g, a_k_norm_g,
           b_q_norm_g, b_k_norm_g, rel_bias, w_o_a, w_o_b, w_out, g_ffn, w_group, w_expert,
           w_exp_gate, w_exp_up, w_exp_down):
    batch, seq, d = x.shape
    t = batch * seq
    q_lora, kv_lora = q_norm_g.shape[0], kv_norm_g.shape[0]
    a_heads = w_uq.shape[1] // A_QK
    b_heads = w_o_b.shape[0] // B_HEAD_DIM
    b_width = b_heads * B_HEAD_DIM
    n_groups, n_experts = w_group.shape[1], w_expert.shape[1]
    per_group = n_experts // n_groups
    off_b = q_lora + kv_lora + A_ROPE
    assert seq % ATTN_TILE == 0 and (TOP_K * t) % MOE_ROWS == 0 and n_groups + n_experts <= LANE
    assert a_heads == b_heads

    xf = x.reshape(t, d)
    tm_big = min(1024, t)
    tm_half = min(512, t)
    tn_wide = _tile(b_width, 1024)
    assert d % tn_wide == 0

    za_cols = -(-(q_lora + kv_lora + LANE) // tn_wide) * tn_wide
    wuq = jnp.pad(w_uq.reshape(q_lora, a_heads, A_QK), ((0, 0), (0, 0), (0, A_HEAD_PAD - A_QK)))
    wuq = wuq.reshape(q_lora, a_heads * A_HEAD_PAD).astype(BF16)
    wukv = w_ukv.astype(BF16)
    pad_gain = lambda g, s: jnp.pad(g * s, (0, A_HEAD_PAD - A_QK)).reshape(1, A_HEAD_PAD)
    gaq = pad_gain(a_q_norm_g, A_QK ** -0.5)
    gak = pad_gain(a_k_norm_g, 1.0)
    g_bqk = jnp.concatenate([jnp.tile(b_q_norm_g * B_HEAD_DIM ** -0.5, b_heads), jnp.tile(b_k_norm_g, b_heads)])

    half = A_ROPE // 2
    inv = ROPE_THETA ** (-jnp.arange(half, dtype=F32) / half)
    ang = positions.astype(F32).reshape(t, 1) * inv
    cos, sin = jnp.cos(ang), jnp.sin(ang)
    zeros = jnp.zeros((t, half), F32)
    cos_t = jnp.concatenate([cos, cos, zeros, zeros], axis=1)
    sin_lo = jnp.concatenate([-sin, zeros, zeros, zeros], axis=1)
    sin_hi = jnp.concatenate([zeros, sin, zeros, zeros], axis=1)

    xn = _rmsnorm_rows(xf, g_mix, BF16, min(256, t))
    w_in_t = w_in.T
    za = _inproj_cols(xn, w_in_t, 0, za_cols, None, _plain_epilogue, tm_big, tn_wide)
    bqk = _inproj_cols(xn, w_in_t, off_b, 2 * b_width, g_bqk, _head_norm_epilogue, tm_big, tn_wide)
    bv = _inproj_cols(xn, w_in_t, off_b + 2 * b_width, b_width, None, _plain_epilogue, tm_big, tn_wide)
    gates = _inproj_cols(xn, w_in_t, off_b + 3 * b_width, 2 * d, b_gate, _sigmoid_epilogue, tm_big, tn_wide)
    q, k, v = _mla_proj(za, cos_t, sin_lo, sin_hi, wuq, wukv, q_norm_g.reshape(1, -1), kv_norm_g.reshape(1, -1),
                        gaq, gak, a_heads, min(256, t))
    o_a, o_b = _attention(q, k, v, bqk, bv, _band_bias_rows(rel_bias), batch, seq, a_heads)
    merged = _merge(o_a, o_b, w_o_a, w_o_b, gates, tm_half, tn_wide)
    x1 = _out_proj(merged, w_out, xf, tm_half, tn_wide)

    wr = _pad_cols(jnp.concatenate([w_group, w_expert], axis=1), LANE)
    xn2, route = _router(x1, g_ffn, wr, n_groups, per_group, min(256, t))
    dest_lanes, starts_f = _sort_assignments(route, min(512, t))
    dest = dest_lanes[:, :TOP_K].reshape(-1)
    items = _work_items(starts_f[0, :n_experts].astype(jnp.int32), TOP_K * t)
    y = _moe_experts(items, dest, xn2, w_exp_gate, w_exp_up, w_exp_down)
    out = _combine(dest, x1, route, y, min(256, t))
    return out.reshape(batch, seq, d)
```
